```python
import math
import jax, jax.numpy as jnp
from jax import lax
import numpy as np

D_MODEL = 2048
BATCH = 8
SEQ = 4096
DEPTH = 1

SSM_HEAD_DIM = 64
SSM_HEADS = D_MODEL // SSM_HEAD_DIM
D_SSM = SSM_HEADS * SSM_HEAD_DIM
SSM_GROUPS = 8
HEADS_PER_GROUP = SSM_HEADS // SSM_GROUPS
D_STATE = 128
CONV_WIDTH = 4
SSD_CHUNK = 128
D_XBC = D_SSM + 2 * SSM_GROUPS * D_STATE

ATT_HEAD_DIM = 128
ATT_HEADS = D_MODEL // ATT_HEAD_DIM
D_ATT = ATT_HEADS * ATT_HEAD_DIM
DILATION_PAIRS = ((128, 1), (512, 4), (2048, 16))
ATT_BLOCK = 128

D_MIX = D_SSM + D_ATT
IN_SPLITS = (D_SSM,
             D_SSM + D_XBC,
             D_SSM + D_XBC + SSM_HEADS,
             D_SSM + D_XBC + SSM_HEADS + D_ATT,
             D_SSM + D_XBC + SSM_HEADS + 2 * D_ATT)
D_IN_PROJ = D_SSM + D_XBC + SSM_HEADS + 3 * D_ATT
D_FF = 4 * D_MODEL
EPS = 1e-6

kernel_name = "hymba_ssd_dilated_swa_sqrelu"


def rmsnorm(x, w):
    xf = x.astype(jnp.float32)
    xf = xf * lax.rsqrt(jnp.mean(xf * xf, axis=-1, keepdims=True) + EPS)
    return (xf * w.astype(jnp.float32)).astype(x.dtype)


def causal_depthwise_conv(u, w, b):
    out = lax.conv_general_dilated(
        u, w[:, None, :].astype(u.dtype), window_strides=(1,),
        padding=[(CONV_WIDTH - 1, 0)],
        dimension_numbers=('NWC', 'WIO', 'NWC'),
        feature_group_count=u.shape[-1])
    return out + b.astype(u.dtype)


def ssd_chunked(xh, dt, a, bm, cm):
    b_, s_ = xh.shape[:2]
    nc = s_ // SSD_CHUNK

    def chunk(t):
        return t.reshape((b_, nc, SSD_CHUNK) + t.shape[2:])

    xc, dtc, bc, cc = chunk(xh), chunk(dt), chunk(bm), chunk(cm)
    a_cs = jnp.cumsum(dtc * a, axis=2).transpose(0, 1, 3, 4, 2)
    causal = jnp.tril(jnp.ones((SSD_CHUNK, SSD_CHUNK), dtype=bool))
    decay_in = jnp.exp(jnp.where(causal, a_cs[..., :, None] - a_cs[..., None, :], -jnp.inf))
    cb = jnp.einsum('bcign,bcjgn->bcgij', cc, bc)
    xdt = xc * dtc[..., None]
    y_diag = jnp.einsum('bcgrij,bcjgrp->bcigrp', cb[:, :, :, None] * decay_in, xdt)

    decay_to_end = jnp.exp(a_cs[..., -1:] - a_cs)
    states = jnp.einsum('bcjgn,bcgrj,bcjgrp->bcgrpn', bc, decay_to_end, xdt)
    chunk_decay = jnp.exp(a_cs[..., -1])

    def step(h, inp):
        st, dec = inp
        return h * dec[..., None, None] + st, h

    h0 = jnp.zeros(states.shape[:1] + states.shape[2:], jnp.float32)
    _, prev = lax.scan(step, h0, (jnp.swapaxes(states, 0, 1), jnp.swapaxes(chunk_decay, 0, 1)))
    prev = jnp.swapaxes(prev, 0, 1)
    y_off = jnp.einsum('bcign,bcgrpn,bcgri->bcigrp', cc, prev, jnp.exp(a_cs))
    return (y_diag + y_off).reshape(xh.shape)


def dilated_window_attention(q, k, v, window, dilation):
    b_, s_, h_, d_ = q.shape
    sd = s_ // dilation
    reach = window // dilation
    nb = -(-sd // ATT_BLOCK)
    lp = nb * ATT_BLOCK
    bd = b_ * dilation

    def decimate(t):
        t = t.reshape(b_, sd, dilation, h_, d_).transpose(0, 2, 1, 3, 4)
        t = t.reshape(bd, sd, h_, d_)
        return jnp.pad(t, ((0, 0), (0, lp - sd), (0, 0), (0, 0)))

    def with_prev(t):
        t = jnp.pad(t, ((0, 0), (ATT_BLOCK, 0), (0, 0), (0, 0)))
        t = t.reshape(bd, nb + 1, ATT_BLOCK, h_, d_)
        return jnp.concatenate([t[:, :-1], t[:, 1:]], axis=2)

    qb = decimate(q).reshape(bd, nb, ATT_BLOCK, h_, d_)
    kb = with_prev(decimate(k))
    vb = with_prev(decimate(v))
    s = jnp.einsum('bnqhd,bnkhd->bnhqk', qb, kb)
    qi = jnp.arange(ATT_BLOCK)[:, None]
    kj = jnp.arange(2 * ATT_BLOCK)[None, :]
    dist = ATT_BLOCK + qi - kj
    key_pos = (jnp.arange(nb)[:, None, None] - 1) * ATT_BLOCK + kj[None]
    mask = (dist >= 0) & (dist <= reach) & (key_pos >= 0)
    s = jnp.where(mask[None, :, None], s, -jnp.inf)
    m = jnp.max(s, axis=-1, keepdims=True)
    p = jnp.exp(s - m)
    den = jnp.sum(p, axis=-1)
    o = jnp.einsum('bnhqk,bnkhd->bnqhd', p, vb) / jnp.swapaxes(den, 2, 3)[..., None]
    lse = jnp.swapaxes(m[..., 0] + jnp.log(den), 2, 3)

    def undecimate(t):
        t = t.reshape((bd, lp) + t.shape[3:])[:, :sd]
        t = t.reshape((b_, dilation, sd) + t.shape[2:])
        return jnp.moveaxis(t, 1, 2).reshape((b_, s_) + t.shape[3:])

    return undecimate(o), undecimate(lse)


def hybrid_mixer(u, w_in, conv_w, conv_b, dt_bias, a_log, d_skip, ssm_norm_w, w_out):
    b_, s_, _ = u.shape
    f32 = jnp.float32
    proj = jnp.einsum('bsd,de->bse', u, w_in)
    z, xbc, dt_raw, q, k, v = jnp.split(proj, IN_SPLITS, axis=-1)

    xbc = jax.nn.silu(causal_depthwise_conv(xbc, conv_w, conv_b))
    xs, bm, cm = jnp.split(xbc, (D_SSM, D_SSM + SSM_GROUPS * D_STATE), axis=-1)
    xh = xs.astype(f32).reshape(b_, s_, SSM_GROUPS, HEADS_PER_GROUP, SSM_HEAD_DIM)
    dt = jax.nn.softplus(dt_raw.astype(f32) + dt_bias.astype(f32))
    dt = dt.reshape(b_, s_, SSM_GROUPS, HEADS_PER_GROUP)
    a = -jnp.exp(a_log.astype(f32)).reshape(SSM_GROUPS, HEADS_PER_GROUP)
    bm = bm.astype(f32).reshape(b_, s_, SSM_GROUPS, D_STATE)
    cm = cm.astype(f32).reshape(b_, s_, SSM_GROUPS, D_STATE)
    y = ssd_chunked(xh, dt, a, bm, cm)
    y = y + d_skip.astype(f32).reshape(SSM_GROUPS, HEADS_PER_GROUP)[:, :, None] * xh
    yg = y.reshape(b_, s_, SSM_GROUPS, -1) * jax.nn.silu(z.astype(f32)).reshape(b_, s_, SSM_GROUPS, -1)
    yg = yg * lax.rsqrt(jnp.mean(yg * yg, axis=-1, keepdims=True) + EPS)
    y_ssm = (yg.reshape(b_, s_, D_SSM) * ssm_norm_w.astype(f32)).astype(u.dtype)

    qh = q.astype(f32).reshape(b_, s_, ATT_HEADS, ATT_HEAD_DIM) * (ATT_HEAD_DIM ** -0.5)
    kh = k.astype(f32).reshape(b_, s_, ATT_HEADS, ATT_HEAD_DIM)
    vh = v.astype(f32).reshape(b_, s_, ATT_HEADS, ATT_HEAD_DIM)
    outs, lses = [], []
    for window, dilation in DILATION_PAIRS:
        o, l = dilated_window_attention(qh, kh, vh, window, dilation)
        outs.append(o)
        lses.append(l)
    wts = jax.nn.softmax(jnp.stack(lses, axis=0), axis=0)
    y_att = jnp.einsum('ibsh,ibshd->bshd', wts, jnp.stack(outs, axis=0))
    y_att = y_att.reshape(b_, s_, D_ATT).astype(u.dtype)

    y_mix = jnp.concatenate([y_ssm, y_att], axis=-1)
    return jnp.einsum('bse,ed->bsd', y_mix, w_out)


def squared_relu_mlp(u, w_up, w_down):
    hdn = jax.nn.relu(jnp.einsum('bsd,df->bsf', u, w_up))
    return jnp.einsum('bsf,fd->bsd', hdn * hdn, w_down)


def _fwd_setup_inputs(seed: int = 0) -> dict:
    key = jax.random.key(seed)
    ks = jax.random.split(key, 16)
    L = DEPTH

    def gain(k, n):
        return 1.0 + 0.1 * jax.random.normal(k, (L, n), jnp.float32)

    dt0 = jnp.exp(jax.random.uniform(ks[5], (L, SSM_HEADS), jnp.float32,
                                     math.log(1e-3), math.log(1e-1)))
    dt_bias = dt0 + jnp.log(-jnp.expm1(-dt0))
    return {
        "x": jax.random.normal(ks[0], (BATCH, SEQ, D_MODEL), jnp.float32),
        "norm_mix_pre": gain(ks[1], D_MODEL),
        "w_in": jax.random.normal(ks[2], (L, D_MODEL, D_IN_PROJ), jnp.float32) * D_MODEL ** -0.5,
        "conv_w": jax.random.normal(ks[3], (L, CONV_WIDTH, D_XBC), jnp.float32) * CONV_WIDTH ** -0.5,
        "conv_b": 0.01 * jax.random.normal(ks[4], (L, D_XBC), jnp.float32),
        "dt_bias": dt_bias,
        "a_log": jnp.log(jax.random.uniform(ks[6], (L, SSM_HEADS), jnp.float32, 1.0, 16.0)),
        "d_skip": gain(ks[7], SSM_HEADS),
        "ssm_norm_w": gain(ks[8], D_SSM),
        "w_out": jax.random.normal(ks[9], (L, D_MIX, D_MODEL), jnp.float32) * D_MIX ** -0.5,
        "norm_mix_post": gain(ks[10], D_MODEL),
        "norm_mlp_pre": gain(ks[11], D_MODEL),
        "w_up": jax.random.normal(ks[12], (L, D_MODEL, D_FF), jnp.float32) * D_MODEL ** -0.5,
        "w_down": jax.random.normal(ks[13], (L, D_FF, D_MODEL), jnp.float32) * D_FF ** -0.5,
        "norm_mlp_post": gain(ks[14], D_MODEL),
    }


def _fwd_reference(x, norm_mix_pre, w_in, conv_w, conv_b, dt_bias, a_log, d_skip, ssm_norm_w,
              w_out, norm_mix_post, norm_mlp_pre, w_up, w_down, norm_mlp_post):
    h = x
    for i in range(DEPTH):
        mix = hybrid_mixer(rmsnorm(h, norm_mix_pre[i]), w_in[i], conv_w[i], conv_b[i],
                           dt_bias[i], a_log[i], d_skip[i], ssm_norm_w[i], w_out[i])
        h = h + rmsnorm(mix, norm_mix_post[i])
        ff = squared_relu_mlp(rmsnorm(h, norm_mlp_pre[i]), w_up[i], w_down[i])
        h = h + rmsnorm(ff, norm_mlp_post[i])
    return h


import jax as _jax
import jax.numpy as _jnp

TWIN_FORMAT = 'train_step'
FWD_PARAMS = ['x', 'norm_mix_pre', 'w_in', 'conv_w', 'conv_b', 'dt_bias', 'a_log', 'd_skip', 'ssm_norm_w', 'w_out', 'norm_mix_post', 'norm_mlp_pre', 'w_up', 'w_down', 'norm_mlp_post']
TWIN_WEIGHTS = ['norm_mix_pre', 'w_in', 'conv_w', 'conv_b', 'dt_bias', 'a_log', 'd_skip', 'ssm_norm_w', 'w_out', 'norm_mix_post', 'norm_mlp_pre', 'w_up', 'w_down', 'norm_mlp_post']
TWIN_DIFF_INPUT = 'x'
TWIN_INPUTS = ['x', 'norm_mix_pre', 'w_in', 'conv_w', 'conv_b', 'dt_bias', 'a_log', 'd_skip', 'ssm_norm_w', 'w_out', 'norm_mix_post', 'norm_mlp_pre', 'w_up', 'w_down', 'norm_mlp_post', 'loss_target', 'm_norm_mix_pre', 'm_w_in', 'm_conv_w', 'm_conv_b', 'm_dt_bias', 'm_a_log', 'm_d_skip', 'm_ssm_norm_w', 'm_w_out', 'm_norm_mix_post', 'm_norm_mlp_pre', 'm_w_up', 'm_w_down', 'm_norm_mlp_post', 'v_norm_mix_pre', 'v_w_in', 'v_conv_w', 'v_conv_b', 'v_dt_bias', 'v_a_log', 'v_d_skip', 'v_ssm_norm_w', 'v_w_out', 'v_norm_mix_post', 'v_norm_mlp_pre', 'v_w_up', 'v_w_down', 'v_norm_mlp_post']
TWIN_OUTPUTS = ['loss', 'grad_x', 'grad_norm_mix_pre', 'grad_w_in', 'grad_conv_w', 'grad_conv_b', 'grad_dt_bias', 'grad_a_log', 'grad_d_skip', 'grad_ssm_norm_w', 'grad_w_out', 'grad_norm_mix_post', 'grad_norm_mlp_pre', 'grad_w_up', 'grad_w_down', 'grad_norm_mlp_post', 'delta_norm_mix_pre', 'delta_w_in', 'delta_conv_w', 'delta_conv_b', 'delta_dt_bias', 'delta_a_log', 'delta_d_skip', 'delta_ssm_norm_w', 'delta_w_out', 'delta_norm_mix_post', 'delta_norm_mlp_pre', 'delta_w_up', 'delta_w_down', 'delta_norm_mlp_post', 'new_m_norm_mix_pre', 'new_m_w_in', 'new_m_conv_w', 'new_m_conv_b', 'new_m_dt_bias', 'new_m_a_log', 'new_m_d_skip', 'new_m_ssm_norm_w', 'new_m_w_out', 'new_m_norm_mix_post', 'new_m_norm_mlp_pre', 'new_m_w_up', 'new_m_w_down', 'new_m_norm_mlp_post', 'new_v_norm_mix_pre', 'new_v_w_in', 'new_v_conv_w', 'new_v_conv_b', 'new_v_dt_bias', 'new_v_a_log', 'new_v_d_skip', 'new_v_ssm_norm_w', 'new_v_w_out', 'new_v_norm_mix_post', 'new_v_norm_mlp_pre', 'new_v_w_up', 'new_v_w_down', 'new_v_norm_mlp_post']
TWIN_LEAF_KINDS = {'loss': 'loss', 'grad_x': 'grad_x', 'grad_norm_mix_pre': 'grad_w', 'grad_w_in': 'grad_w', 'grad_conv_w': 'grad_w', 'grad_conv_b': 'grad_w', 'grad_dt_bias': 'grad_w', 'grad_a_log': 'grad_w', 'grad_d_skip': 'grad_w', 'grad_ssm_norm_w': 'grad_w', 'grad_w_out': 'grad_w', 'grad_norm_mix_post': 'grad_w', 'grad_norm_mlp_pre': 'grad_w', 'grad_w_up': 'grad_w', 'grad_w_down': 'grad_w', 'grad_norm_mlp_post': 'grad_w', 'delta_norm_mix_pre': 'delta_w', 'delta_w_in': 'delta_w', 'delta_conv_w': 'delta_w', 'delta_conv_b': 'delta_w', 'delta_dt_bias': 'delta_w', 'delta_a_log': 'delta_w', 'delta_d_skip': 'delta_w', 'delta_ssm_norm_w': 'delta_w', 'delta_w_out': 'delta_w', 'delta_norm_mix_post': 'delta_w', 'delta_norm_mlp_pre': 'delta_w', 'delta_w_up': 'delta_w', 'delta_w_down': 'delta_w', 'delta_norm_mlp_post': 'delta_w', 'new_m_norm_mix_pre': 'new_m', 'new_m_w_in': 'new_m', 'new_m_conv_w': 'new_m', 'new_m_conv_b': 'new_m', 'new_m_dt_bias': 'new_m', 'new_m_a_log': 'new_m', 'new_m_d_skip': 'new_m', 'new_m_ssm_norm_w': 'new_m', 'new_m_w_out': 'new_m', 'new_m_norm_mix_post': 'new_m', 'new_m_norm_mlp_pre': 'new_m', 'new_m_w_up': 'new_m', 'new_m_w_down': 'new_m', 'new_m_norm_mlp_post': 'new_m', 'new_v_norm_mix_pre': 'new_v', 'new_v_w_in': 'new_v', 'new_v_conv_w': 'new_v', 'new_v_conv_b': 'new_v', 'new_v_dt_bias': 'new_v', 'new_v_a_log': 'new_v', 'new_v_d_skip': 'new_v', 'new_v_ssm_norm_w': 'new_v', 'new_v_w_out': 'new_v', 'new_v_norm_mix_post': 'new_v', 'new_v_norm_mlp_pre': 'new_v', 'new_v_w_up': 'new_v', 'new_v_w_down': 'new_v', 'new_v_norm_mlp_post': 'new_v'}


def _forward(args):
    return _fwd_reference(*[args[k] for k in FWD_PARAMS])


def _output_shape():
    def fwd():
        inp = _fwd_setup_inputs(0)
        return _fwd_reference(*[inp[k] for k in FWD_PARAMS])
    out = _jax.eval_shape(fwd)
    return out.shape, out.dtype

N_MICROBATCH = 1
ADAM_LR = 0.001
ADAM_B1 = 0.9
ADAM_B2 = 0.999
ADAM_EPS = 1e-08
ADAM_WD = 0.01
ADAM_STEP = 10
PER_EXAMPLE_BATCH_AXIS = {'x': 0, 'loss_target': 0}
SHARED_INPUTS = []
_WEIGHT_DTYPES = {'norm_mix_pre': _jnp.float32, 'w_in': _jnp.float32, 'conv_w': _jnp.float32, 'conv_b': _jnp.float32, 'dt_bias': _jnp.float32, 'a_log': _jnp.float32, 'd_skip': _jnp.float32, 'ssm_norm_w': _jnp.float32, 'w_out': _jnp.float32, 'norm_mix_post': _jnp.float32, 'norm_mlp_pre': _jnp.float32, 'w_up': _jnp.float32, 'w_down': _jnp.float32, 'norm_mlp_post': _jnp.float32}
MOMENT_SCALE = {'norm_mix_pre': 3.450277e-01, 'w_in': 1.405166e-01, 'conv_w': 6.193231e-01, 'conv_b': 2.122518e+00, 'dt_bias': 5.347370e-01, 'a_log': 4.562120e+00, 'd_skip': 4.589632e+00, 'ssm_norm_w': 1.387316e+00, 'w_out': 1.310117e+00, 'norm_mix_post': 1.617202e+01, 'norm_mlp_pre': 4.390172e-01, 'w_up': 2.233328e-01, 'w_down': 1.402876e+00, 'norm_mlp_post': 1.655918e+01}


def _to_microbatches(a, axis):
    t = _jnp.moveaxis(a, axis, 0)
    t = t.reshape((N_MICROBATCH, t.shape[0] // N_MICROBATCH) + t.shape[1:])
    return _jnp.moveaxis(t, 1, axis + 1)


def setup_inputs(seed: int = 0) -> dict:
    inp = _fwd_setup_inputs(seed)
    key = _jax.random.fold_in(_jax.random.key(seed), 7919)
    shape, _ = _output_shape()
    out = dict(inp)
    out["loss_target"] = _jax.random.normal(_jax.random.fold_in(key, 0), shape, _jnp.float32)
    for i, name in enumerate(TWIN_WEIGHTS):
        w = inp[name].astype(_jnp.float32)
        if MOMENT_SCALE is None:
            s = _jnp.sqrt(_jnp.mean(_jnp.square(w)) + 1e-30)
        else:
            s = MOMENT_SCALE[name]
        km, kv = _jax.random.split(_jax.random.fold_in(key, i + 1))
        out[name] = w
        out["m_" + name] = s * _jax.random.normal(km, w.shape, _jnp.float32)
        out["v_" + name] = (s * s) * _jax.random.uniform(kv, w.shape, _jnp.float32, 0.5, 1.5)
    if N_MICROBATCH > 1:
        for name, axis in PER_EXAMPLE_BATCH_AXIS.items():
            out[name] = _to_microbatches(out[name], axis)
    return {'x': out['x'], 'norm_mix_pre': out['norm_mix_pre'], 'w_in': out['w_in'], 'conv_w': out['conv_w'], 'conv_b': out['conv_b'], 'dt_bias': out['dt_bias'], 'a_log': out['a_log'], 'd_skip': out['d_skip'], 'ssm_norm_w': out['ssm_norm_w'], 'w_out': out['w_out'], 'norm_mix_post': out['norm_mix_post'], 'norm_mlp_pre': out['norm_mlp_pre'], 'w_up': out['w_up'], 'w_down': out['w_down'], 'norm_mlp_post': out['norm_mlp_post'], 'loss_target': out['loss_target'], 'm_norm_mix_pre': out['m_norm_mix_pre'], 'm_w_in': out['m_w_in'], 'm_conv_w': out['m_conv_w'], 'm_conv_b': out['m_conv_b'], 'm_dt_bias': out['m_dt_bias'], 'm_a_log': out['m_a_log'], 'm_d_skip': out['m_d_skip'], 'm_ssm_norm_w': out['m_ssm_norm_w'], 'm_w_out': out['m_w_out'], 'm_norm_mix_post': out['m_norm_mix_post'], 'm_norm_mlp_pre': out['m_norm_mlp_pre'], 'm_w_up': out['m_w_up'], 'm_w_down': out['m_w_down'], 'm_norm_mlp_post': out['m_norm_mlp_post'], 'v_norm_mix_pre': out['v_norm_mix_pre'], 'v_w_in': out['v_w_in'], 'v_conv_w': out['v_conv_w'], 'v_conv_b': out['v_conv_b'], 'v_dt_bias': out['v_dt_bias'], 'v_a_log': out['v_a_log'], 'v_d_skip': out['v_d_skip'], 'v_ssm_norm_w': out['v_ssm_norm_w'], 'v_w_out': out['v_w_out'], 'v_norm_mix_post': out['v_norm_mix_post'], 'v_norm_mlp_pre': out['v_norm_mlp_pre'], 'v_w_up': out['v_w_up'], 'v_w_down': out['v_w_down'], 'v_norm_mlp_post': out['v_norm_mlp_post']}


def _loss(weights, diff, rest, loss_target):
    with _jax.named_scope("forward"):
        args = {**rest, TWIN_DIFF_INPUT: diff, **{k: w.astype(_WEIGHT_DTYPES[k]) for k, w in weights.items()}}
        y = _forward(args)
    with _jax.named_scope("loss_head"):
        err = _jnp.square(y.astype(_jnp.float32) - loss_target)
        return 0.5 * _jnp.sum(_jnp.mean(err, axis=-1)) if err.ndim else 0.5 * err


def _adamw(w, g, m, v):
    m = ADAM_B1 * m + (1.0 - ADAM_B1) * g
    v = ADAM_B2 * v + (1.0 - ADAM_B2) * _jnp.square(g)
    m_hat = m / (1.0 - ADAM_B1 ** ADAM_STEP)
    v_hat = v / (1.0 - ADAM_B2 ** ADAM_STEP)
    delta = -ADAM_LR * (m_hat / (_jnp.sqrt(v_hat) + ADAM_EPS) + ADAM_WD * w)
    return delta, m, v


def reference(x, norm_mix_pre, w_in, conv_w, conv_b, dt_bias, a_log, d_skip, ssm_norm_w, w_out, norm_mix_post, norm_mlp_pre, w_up, w_down, norm_mlp_post, loss_target, m_norm_mix_pre, m_w_in, m_conv_w, m_conv_b, m_dt_bias, m_a_log, m_d_skip, m_ssm_norm_w, m_w_out, m_norm_mix_post, m_norm_mlp_pre, m_w_up, m_w_down, m_norm_mlp_post, v_norm_mix_pre, v_w_in, v_conv_w, v_conv_b, v_dt_bias, v_a_log, v_d_skip, v_ssm_norm_w, v_w_out, v_norm_mix_post, v_norm_mlp_pre, v_w_up, v_w_down, v_norm_mlp_post):
    given = dict(x=x, norm_mix_pre=norm_mix_pre, w_in=w_in, conv_w=conv_w, conv_b=conv_b, dt_bias=dt_bias, a_log=a_log, d_skip=d_skip, ssm_norm_w=ssm_norm_w, w_out=w_out, norm_mix_post=norm_mix_post, norm_mlp_pre=norm_mlp_pre, w_up=w_up, w_down=w_down, norm_mlp_post=norm_mlp_post, loss_target=loss_target, m_norm_mix_pre=m_norm_mix_pre, m_w_in=m_w_in, m_conv_w=m_conv_w, m_conv_b=m_conv_b, m_dt_bias=m_dt_bias, m_a_log=m_a_log, m_d_skip=m_d_skip, m_ssm_norm_w=m_ssm_norm_w, m_w_out=m_w_out, m_norm_mix_post=m_norm_mix_post, m_norm_mlp_pre=m_norm_mlp_pre, m_w_up=m_w_up, m_w_down=m_w_down, m_norm_mlp_post=m_norm_mlp_post, v_norm_mix_pre=v_norm_mix_pre, v_w_in=v_w_in, v_conv_w=v_conv_w, v_conv_b=v_conv_b, v_dt_bias=v_dt_bias, v_a_log=v_a_log, v_d_skip=v_d_skip, v_ssm_norm_w=v_ssm_norm_w, v_w_out=v_w_out, v_norm_mix_post=v_norm_mix_post, v_norm_mlp_pre=v_norm_mlp_pre, v_w_up=v_w_up, v_w_down=v_w_down, v_norm_mlp_post=v_norm_mlp_post)
    weights = {n: given[n] for n in TWIN_WEIGHTS}
    shared = {n: given[n] for n in SHARED_INPUTS}
    per_example = {n: given[n] for n in ['x']}
    grad_fn = _jax.value_and_grad(_loss, argnums=(0, 1))

    def one_microbatch(ex, loss_target):
        ex = dict(ex)
        diff = ex.pop(TWIN_DIFF_INPUT)
        return grad_fn(weights, diff, {**shared, **ex}, loss_target)

    if N_MICROBATCH == 1:
        loss, (grad_w, grad_x) = one_microbatch(per_example, given["loss_target"])
    else:
        def body(carry, xs):
            loss_sum, grad_sum = carry
            l_k, (gw_k, gx_k) = one_microbatch(xs[0], xs[1])
            with _jax.named_scope("update"):
                return (loss_sum + l_k, _jax.tree.map(_jnp.add, grad_sum, gw_k)), gx_k

        init = (_jnp.zeros((), _jnp.float32), _jax.tree.map(_jnp.zeros_like, weights))
        (loss, grad_w), grad_x = _jax.lax.scan(body, init, (per_example, given["loss_target"]))
    with _jax.named_scope("update"):
        delta_w, new_m, new_v = {}, {}, {}
        for n in TWIN_WEIGHTS:
            delta_w[n], new_m[n], new_v[n] = _adamw(weights[n], grad_w[n], given["m_" + n], given["v_" + n])
    return (loss, grad_x, *[grad_w[n] for n in TWIN_WEIGHTS], *[delta_w[n] for n in TWIN_WEIGHTS],
            *[new_m[n] for n in TWIN_WEIGHTS], *[new_v[n] for n in TWIN_WEIGHTS])
```

```python
import functools
import math

import numpy as np
import jax
import jax.numpy as jnp
from jax import lax
from jax.experimental import pallas as pl
from jax.experimental.pallas import tpu as pltpu

F32 = jnp.float32
BF16 = jnp.bfloat16

D_MODEL = 2048
SSM_HEAD_DIM = 64
SSM_GROUPS = 8
HEADS_PER_GROUP = 4
SSM_HEADS = SSM_GROUPS * HEADS_PER_GROUP
D_SSM = SSM_HEADS * SSM_HEAD_DIM
D_STATE = 128
CONV_WIDTH = 4
SSD_CHUNK = 128
D_XBC = D_SSM + 2 * SSM_GROUPS * D_STATE
GROUP_X = HEADS_PER_GROUP * SSM_HEAD_DIM
GROUP_COLS = GROUP_X + 2 * D_STATE
ATT_HEAD_DIM = 128
ATT_HEADS = 16
D_ATT = ATT_HEADS * ATT_HEAD_DIM
DILATIONS = (1, 4, 16)
ATT_BLOCK = 128
D_MIX = D_SSM + D_ATT
D_IN_PROJ = D_SSM + D_XBC + SSM_HEADS + 3 * D_ATT
D_FF = 4 * D_MODEL
EPS = 1e-6
N_CHIPS = 4
W_IN_SHARD = D_IN_PROJ // N_CHIPS

ADAM_LR = 0.001
ADAM_B1 = 0.9
ADAM_B2 = 0.999
ADAM_EPS = 1e-08
ADAM_WD = 0.01
ADAM_STEP = 10

LANES = 128
VMEM_LIMIT = 48 * 1024 * 1024
MESH = pl.DeviceIdType.MESH

_NN = (((1,), (0,)), ((), ()))
_NT = (((1,), (1,)), ((), ()))
_TN = (((0,), (0,)), ((), ()))


def _dot(a, b, dims=_NN):
    return lax.dot_general(a, b, dims, preferred_element_type=F32)


def _cparams(*sem):
    return pltpu.CompilerParams(dimension_semantics=sem, vmem_limit_bytes=VMEM_LIMIT)


def _matmul(pairs, mode, out_dtypes, *, name, tm=512, tn=1024, epilogue=None, extras=()):
    a0, b0, _ = pairs[0]
    m_dim = a0.shape[1] if mode == "tn" else a0.shape[0]
    n_dim = b0.shape[0] if mode == "nt" else b0.shape[1]
    tm, tn = min(tm, m_dim), min(tn, n_dim)
    nks, offs = [], []
    for a, _, tk in pairs:
        k_dim = a.shape[0] if mode == "tn" else a.shape[1]
        assert k_dim % tk == 0, (name, k_dim, tk)
        offs.append(sum(nks))
        nks.append(k_dim // tk)
    nk_total = sum(nks)
    assert m_dim % tm == 0 and n_dim % tn == 0, (name, m_dim, n_dim)
    dims = {"nn": _NN, "nt": _NT, "tn": _TN}[mode]
    n_pairs, n_extra, n_out = len(pairs), len(extras), len(out_dtypes)

    in_specs, operands = [], []
    for (a, b, tk), off, nk in zip(pairs, offs, nks):
        def kidx(k, off=off, nk=nk):
            return k if n_pairs == 1 else jnp.clip(k - off, 0, nk - 1)
        if mode == "tn":
            in_specs.append(pl.BlockSpec((tk, tm), lambda m, n, k, f=kidx: (f(k), m)))
        else:
            in_specs.append(pl.BlockSpec((tm, tk), lambda m, n, k, f=kidx: (m, f(k))))
        if mode == "nt":
            in_specs.append(pl.BlockSpec((tn, tk), lambda m, n, k, f=kidx: (n, f(k))))
        else:
            in_specs.append(pl.BlockSpec((tk, tn), lambda m, n, k, f=kidx: (f(k), n)))
        operands += [a, b]
    for e in extras:
        in_specs.append(pl.BlockSpec((tm, tn), lambda m, n, k: (m, n)))
        operands.append(e)

    def body(*refs):
        ab = refs[:2 * n_pairs]
        e_refs = refs[2 * n_pairs:2 * n_pairs + n_extra]
        o_refs = refs[2 * n_pairs + n_extra:2 * n_pairs + n_extra + n_out]
        acc = refs[-1]
        k = pl.program_id(2)

        @pl.when(k == 0)
        def _():
            acc[...] = jnp.zeros_like(acc)

        for i in range(n_pairs):
            def accumulate(i=i):
                acc[...] += _dot(ab[2 * i][...], ab[2 * i + 1][...], dims)
            if n_pairs == 1:
                accumulate()
            else:
                pl.when((k >= offs[i]) & (k < offs[i] + nks[i]))(accumulate)

        @pl.when(k == nk_total - 1)
        def _():
            vals = (acc[...],) if epilogue is None else epilogue(acc[...], *[e[...] for e in e_refs])
            for o_ref, v in zip(o_refs, vals):
                o_ref[...] = v.astype(o_ref.dtype)

    outs = pl.pallas_call(
        body,
        grid=(m_dim // tm, n_dim // tn, nk_total),
        in_specs=in_specs,
        out_specs=[pl.BlockSpec((tm, tn), lambda m, n, k: (m, n)) for _ in out_dtypes],
        out_shape=[jax.ShapeDtypeStruct((m_dim, n_dim), dt) for dt in out_dtypes],
        scratch_shapes=[pltpu.VMEM((tm, tn), F32)],
        compiler_params=_cparams("parallel", "parallel", "arbitrary"),
        name=name,
    )(*operands)
    return outs[0] if n_out == 1 else outs


def _rowcall(fn, rows, vecs, row_outs, acc_widths, *, name, tr=256, row_cols=None):
    s_dim = rows[0].shape[0]
    assert s_dim % tr == 0
    row_cols = row_cols or [None] * len(rows)
    n_r, n_v, n_ro, n_acc = len(rows), len(vecs), len(row_outs), len(acc_widths)
    in_specs = []
    for r, rc in zip(rows, row_cols):
        if rc is None:
            in_specs.append(pl.BlockSpec((tr, r.shape[1]), lambda i: (i, 0)))
        else:
            in_specs.append(pl.BlockSpec((tr, rc[0]), lambda i, c=rc[1]: (i, c)))
    for v in vecs:
        in_specs.append(pl.BlockSpec(v.shape, lambda i, nd=v.ndim: (0,) * nd))

    def body(*refs):
        ins = [r[...] for r in refs[:n_r + n_v]]
        ro = refs[n_r + n_v:n_r + n_v + n_ro]
        ao = refs[n_r + n_v + n_ro:]
        outs = fn(*ins)
        for ref, v in zip(ro, outs[:n_ro]):
            ref[...] = v.astype(ref.dtype)
        if n_acc:
            @pl.when(pl.program_id(0) == 0)
            def _():
                for ref in ao:
                    ref[...] = jnp.zeros_like(ref)
            for ref, v in zip(ao, outs[n_ro:]):
                ref[...] += v

    outs = pl.pallas_call(
        body,
        grid=(s_dim // tr,),
        in_specs=in_specs,
        out_specs=[pl.BlockSpec((tr, w), lambda i: (i, 0)) for w, _ in row_outs]
        + [pl.BlockSpec((1, w), lambda i: (0, 0)) for w in acc_widths],
        out_shape=[jax.ShapeDtypeStruct((s_dim, w), dt) for w, dt in row_outs]
        + [jax.ShapeDtypeStruct((1, w), F32) for w in acc_widths],
        compiler_params=_cparams("arbitrary"),
        name=name,
    )(*rows, *vecs)
    return outs


def _nrm(x, g):
    r = lax.rsqrt(jnp.mean(x * x, axis=-1, keepdims=True) + EPS)
    n = x * r
    return n * g, n, r


def _nrm_bwd(dy, n, r, g):
    dn = dy * g
    dx = r * (dn - n * jnp.mean(dn * n, axis=-1, keepdims=True))
    return dx, jnp.sum(dy * n, axis=0, keepdims=True)


def _sigmoid(x):
    return 1.0 / (1.0 + jnp.exp(-x))


def _softplus(x):
    return jnp.maximum(x, 0.0) + jnp.log(1.0 + jnp.exp(-jnp.abs(x)))


def _pre_norm(x, g1):
    def fn(xb, g):
        return (_nrm(xb, g)[0],)
    return _rowcall(fn, [x], [g1], [(D_MODEL, BF16)], [], name="pre_norm")[0]


def _post_pre_norm(x, mix, g2, g3):
    def fn(xb, mb, g2b, g3b):
        h1 = xb + _nrm(mb, g2b)[0]
        return h1, _nrm(h1, g3b)[0]
    return _rowcall(fn, [x, mix], [g2, g3], [(D_MODEL, F32), (D_MODEL, BF16)], [], name="post_pre_norm")


def _tail(ff, h1, target, g4):
    def fn(ffb, h1b, tb, g):
        y, n, r = _nrm(ffb, g)
        e = h1b + y - tb
        loss = 0.5 * jnp.sum(jnp.sum(e * e, axis=-1, keepdims=True) * (1.0 / D_MODEL), axis=0, keepdims=True)
        dh2 = e * (1.0 / D_MODEL)
        dff, dg = _nrm_bwd(dh2, n, r, g)
        return dh2, dff, dg, jnp.broadcast_to(loss, (1, LANES))
    return _rowcall(fn, [ff, h1, target], [g4], [(D_MODEL, F32), (D_MODEL, BF16)], [D_MODEL, LANES], name="tail")


def _mid_bwd(du2, h1, dh2, mix, g2, g3):
    def fn(du2b, h1b, dh2b, mb, g2b, g3b):
        _, n3, r3 = _nrm(h1b, g3b)
        d3, dg3 = _nrm_bwd(du2b, n3, r3, g3b)
        dh1 = dh2b + d3
        _, n2, r2 = _nrm(mb, g2b)
        dmix, dg2 = _nrm_bwd(dh1, n2, r2, g2b)
        return dh1, dmix, dg3, dg2
    return _rowcall(fn, [du2, h1, dh2, mix], [g2, g3], [(D_MODEL, F32), (D_MODEL, BF16)], [D_MODEL, D_MODEL],
                    name="mid_bwd")


def _first_bwd(du, x, dh1, g1):
    def fn(dub, xb, dh1b, g):
        _, n, r = _nrm(xb, g)
        dx, dg = _nrm_bwd(dub, n, r, g)
        return dh1b + dx, dg
    return _rowcall(fn, [du, x, dh1], [g1], [(D_MODEL, F32)], [D_MODEL], name="first_bwd")


CONV_TILE = 256
CONV_ROWS = 256
PAD = 8


def _conv_taps(w):
    return [w[k:k + 1, :] for k in range(CONV_WIDTH)], w[CONV_WIDTH:CONV_WIDTH + 1, :]


def _conv_fwd(xbc, w8):
    s_dim, c_dim = xbc.shape
    n_steps = s_dim // CONV_ROWS

    def body(x_ref, w_ref, o_ref, xp):
        xp[0:PAD, :] = jnp.zeros((PAD, CONV_TILE), F32)
        xp[PAD:PAD + s_dim, :] = x_ref[...]
        taps, bias = _conv_taps(w_ref[...])

        def step(c, carry):
            base = pl.multiple_of(c * CONV_ROWS, CONV_ROWS)
            win = xp[pl.ds(base, CONV_ROWS + PAD), :]
            pre = bias + taps[3] * win[PAD:, :]
            for j in range(1, CONV_WIDTH):
                pre = pre + taps[3 - j] * pltpu.roll(win, j, axis=0)[PAD:, :]
            o_ref[pl.ds(base, CONV_ROWS), :] = pre * _sigmoid(pre)
            return carry

        lax.fori_loop(0, n_steps, step, 0)

    return pl.pallas_call(
        body,
        grid=(c_dim // CONV_TILE,),
        in_specs=[pl.BlockSpec((s_dim, CONV_TILE), lambda j: (0, j)), pl.BlockSpec((8, CONV_TILE), lambda j: (0, j))],
        out_specs=pl.BlockSpec((s_dim, CONV_TILE), lambda j: (0, j)),
        out_shape=jax.ShapeDtypeStruct((s_dim, c_dim), F32),
        scratch_shapes=[pltpu.VMEM((s_dim + 2 * PAD, CONV_TILE), F32)],
        compiler_params=_cparams("parallel"),
        name="conv_fwd",
    )(xbc, w8)


def _conv_bwd(xbc, w8, dxc):
    s_dim, c_dim = xbc.shape
    n_steps = s_dim // CONV_ROWS

    def body(x_ref, w_ref, d_ref, dx_ref, dw_ref, xp, dp):
        xp[0:PAD, :] = jnp.zeros((PAD, CONV_TILE), F32)
        xp[PAD:PAD + s_dim, :] = x_ref[...]
        dp[PAD + s_dim:, :] = jnp.zeros((PAD, CONV_TILE), F32)
        taps, bias = _conv_taps(w_ref[...])

        def step1(c, sums):
            base = pl.multiple_of(c * CONV_ROWS, CONV_ROWS)
            win = xp[pl.ds(base, CONV_ROWS + PAD), :]
            shifted = [win[PAD:, :]] + [pltpu.roll(win, j, axis=0)[PAD:, :] for j in range(1, CONV_WIDTH)]
            pre = bias
            for j in range(CONV_WIDTH):
                pre = pre + taps[3 - j] * shifted[j]
            sg = _sigmoid(pre)
            dpre = d_ref[pl.ds(base, CONV_ROWS), :] * (sg * (1.0 + pre * (1.0 - sg)))
            dp[pl.ds(base + PAD, CONV_ROWS), :] = dpre
            new = [sums[k] + jnp.sum(dpre * shifted[3 - k], axis=0, keepdims=True) for k in range(CONV_WIDTH)]
            new.append(sums[CONV_WIDTH] + jnp.sum(dpre, axis=0, keepdims=True))
            return tuple(new)

        zero = jnp.zeros((1, CONV_TILE), F32)
        sums = lax.fori_loop(0, n_steps, step1, (zero,) * (CONV_WIDTH + 1))
        dw_ref[...] = jnp.zeros((8, CONV_TILE), F32)
        for k in range(CONV_WIDTH + 1):
            dw_ref[k:k + 1, :] = sums[k]

        def step2(c, carry):
            base = pl.multiple_of(c * CONV_ROWS, CONV_ROWS)
            win = dp[pl.ds(base + PAD, CONV_ROWS + PAD), :]
            dx = taps[3] * win[:CONV_ROWS, :]
            for j in range(1, CONV_WIDTH):
                dx = dx + taps[3 - j] * pltpu.roll(win, CONV_ROWS + PAD - j, axis=0)[:CONV_ROWS, :]
            dx_ref[pl.ds(base, CONV_ROWS), :] = dx.astype(BF16)
            return carry

        lax.fori_loop(0, n_steps, step2, 0)

    col = lambda j: (0, j)
    return pl.pallas_call(
        body,
        grid=(c_dim // CONV_TILE,),
        in_specs=[pl.BlockSpec((s_dim, CONV_TILE), col), pl.BlockSpec((8, CONV_TILE), col),
                  pl.BlockSpec((s_dim, CONV_TILE), col)],
        out_specs=[pl.BlockSpec((s_dim, CONV_TILE), col), pl.BlockSpec((8, CONV_TILE), col)],
        out_shape=[jax.ShapeDtypeStruct((s_dim, c_dim), BF16), jax.ShapeDtypeStruct((8, c_dim), F32)],
        scratch_shapes=[pltpu.VMEM((s_dim + 2 * PAD, CONV_TILE), F32), pltpu.VMEM((s_dim + 2 * PAD, CONV_TILE), F32)],
        compiler_params=_cparams("parallel"),
        name="conv_bwd",
    )(xbc, w8, dxc)


def _perm_cols(a):
    parts = []
    for g in range(SSM_GROUPS):
        parts += [a[..., g * GROUP_X:(g + 1) * GROUP_X],
                  a[..., D_SSM + g * D_STATE:D_SSM + (g + 1) * D_STATE],
                  a[..., D_SSM + SSM_GROUPS * D_STATE + g * D_STATE:D_SSM + SSM_GROUPS * D_STATE + (g + 1) * D_STATE]]
    return jnp.concatenate(parts, axis=-1)


def _unperm_cols(a):
    xs = [a[..., g * GROUP_COLS:g * GROUP_COLS + GROUP_X] for g in range(SSM_GROUPS)]
    bs = [a[..., g * GROUP_COLS + GROUP_X:g * GROUP_COLS + GROUP_X + D_STATE] for g in range(SSM_GROUPS)]
    cs = [a[..., g * GROUP_COLS + GROUP_X + D_STATE:(g + 1) * GROUP_COLS] for g in range(SSM_GROUPS)]
    return jnp.concatenate(xs + bs + cs, axis=-1)


def _dt_to_groups(dt):
    s_dim = dt.shape[0]
    t = dt[:, :SSM_HEADS].reshape(s_dim, SSM_GROUPS, HEADS_PER_GROUP).transpose(1, 0, 2)
    return jnp.pad(t, ((0, 0), (0, 0), (0, LANES - HEADS_PER_GROUP)))


def _dt_from_groups(dtg):
    s_dim = dtg.shape[1]
    return dtg[:, :, :HEADS_PER_GROUP].transpose(1, 0, 2).reshape(s_dim, SSM_HEADS)


def _pack_ssd_params(dt_bias, a_log, d_skip):
    rows = jnp.stack([p.reshape(SSM_GROUPS, HEADS_PER_GROUP) for p in (dt_bias, a_log, d_skip)], axis=1)
    return jnp.pad(rows, ((0, 0), (0, 8 - 3), (0, LANES - HEADS_PER_GROUP)))


def _unpack_ssd_params(par):
    return tuple(par[:, k, :HEADS_PER_GROUP].reshape(SSM_HEADS) for k in range(3))


Q = SSD_CHUNK


def _split3(v):
    hi = v.astype(BF16)
    r1 = v - hi.astype(F32)
    mid = r1.astype(BF16)
    lo = (r1 - mid.astype(F32)).astype(BF16)
    return hi, mid, lo


def _dot_l01(t01, v):
    return sum(_dot(t01, p) for p in _split3(v))


def _dot_r01(v, e01):
    return sum(_dot(p, e01) for p in _split3(v))


def _ssd_consts():
    row = lax.broadcasted_iota(jnp.int32, (Q, Q), 0)
    col = lax.broadcasted_iota(jnp.int32, (Q, Q), 1)
    causal = row >= col
    tril = causal.astype(BF16)
    triu = (col >= row).astype(BF16)
    er = lax.broadcasted_iota(jnp.int32, (LANES, GROUP_X), 0)
    ec = lax.broadcasted_iota(jnp.int32, (LANES, GROUP_X), 1) // SSM_HEAD_DIM
    expand = (er == ec).astype(BF16)
    rr = lax.broadcasted_iota(jnp.int32, (GROUP_X, LANES), 0) // SSM_HEAD_DIM
    rc = lax.broadcasted_iota(jnp.int32, (GROUP_X, LANES), 1)
    reduce = (rr == rc).astype(BF16)
    lane_head = lax.broadcasted_iota(jnp.int32, (Q, GROUP_X), 1) // SSM_HEAD_DIM
    return causal, tril, triu, expand, reduce, lane_head


def _ssd_common(xc_ref, dt_ref, par_ref, consts):
    causal, tril, _, expand, _, _ = consts
    par = par_ref[...]
    bias, alog, dsk = par[0:1, :], par[1:2, :], par[2:3, :]
    a_neg = -jnp.exp(alog)
    dtr = dt_ref[...] + bias
    dt = _softplus(dtr)
    s = _dot_l01(tril, dt * a_neg)
    dt_x = _dot_r01(dt, expand)
    s_x = _dot_r01(s, expand)
    dsk_x = _dot_r01(jnp.broadcast_to(dsk, (8, LANES)), expand)[0:1, :]
    blk = xc_ref[...]
    x = blk[:, :GROUP_X]
    bm = blk[:, GROUP_X:GROUP_X + D_STATE].astype(BF16)
    cm = blk[:, GROUP_X + D_STATE:].astype(BF16)
    xdt = x * dt_x
    g = _dot(cm, bm, _NT)
    return dict(a_neg=a_neg, dtr=dtr, dt=dt, s=s, s_t=s.T, dt_x=dt_x, s_x=s_x, dsk_x=dsk_x, x=x, bm=bm, cm=cm,
                xdt=xdt, g=g)


def _decay(v, r, causal):
    diff = v["s"][:, r:r + 1] - v["s_t"][r:r + 1, :]
    return jnp.exp(jnp.where(causal, diff, -jnp.inf))


def _ssd_specs(n_chunks, rev):
    cidx = (lambda c: n_chunks - 1 - c) if rev else (lambda c: c)
    xc = pl.BlockSpec((Q, GROUP_COLS), lambda g, c: (cidx(c), g))
    gx = pl.BlockSpec((Q, GROUP_X), lambda g, c: (cidx(c), g))
    dt = pl.BlockSpec((None, Q, LANES), lambda g, c: (g, cidx(c), 0))
    par = pl.BlockSpec((None, 8, LANES), lambda g, c: (g, 0, 0))
    nw = pl.BlockSpec((1, GROUP_X), lambda g, c: (0, g))
    hs = pl.BlockSpec((None, None, D_STATE, GROUP_X), lambda g, c: (cidx(c), g, 0, 0))
    return xc, gx, dt, par, nw, hs


def _ssd_fwd(xc, z, dtg, par, nw):
    s_dim = xc.shape[0]
    n_chunks = s_dim // Q
    xc_s, gx_s, dt_s, par_s, nw_s, hs_s = _ssd_specs(n_chunks, False)

    def body(xc_ref, z_ref, dt_ref, par_ref, nw_ref, y_ref, ys_ref, hs_ref, ht):
        @pl.when(pl.program_id(1) == 0)
        def _():
            ht[...] = jnp.zeros_like(ht)

        consts = _ssd_consts()
        causal, lane_head = consts[0], consts[5]
        v = _ssd_common(xc_ref, dt_ref, par_ref, consts)
        xdt_b = v["xdt"].astype(BF16)
        yd = jnp.zeros((Q, GROUP_X), F32)
        for r in range(HEADS_PER_GROUP):
            m = (v["g"] * _decay(v, r, causal)).astype(BF16)
            yd = yd + _dot(m, jnp.where(lane_head == r, xdt_b, jnp.zeros_like(xdt_b)))
        h = ht[...]
        hs_ref[...] = h
        yo = jnp.exp(v["s_x"]) * _dot(v["cm"], h.astype(BF16))
        y = yd + yo + v["dsk_x"] * v["x"]
        s_last = v["s_x"][Q - 1:Q, :]
        snew = _dot(v["bm"], (v["xdt"] * jnp.exp(s_last - v["s_x"])).astype(BF16), _TN)
        ht[...] = jnp.exp(s_last) * h + snew
        zz = z_ref[...]
        yg = y * (zz * _sigmoid(zz))
        y_ref[...] = y
        ys_ref[...] = _nrm(yg, nw_ref[...])[0].astype(BF16)

    return pl.pallas_call(
        body,
        grid=(SSM_GROUPS, n_chunks),
        in_specs=[xc_s, gx_s, dt_s, par_s, nw_s],
        out_specs=[gx_s, gx_s, hs_s],
        out_shape=[jax.ShapeDtypeStruct((s_dim, D_SSM), F32), jax.ShapeDtypeStruct((s_dim, D_SSM), BF16),
                   jax.ShapeDtypeStruct((n_chunks, SSM_GROUPS, D_STATE, GROUP_X), F32)],
        scratch_shapes=[pltpu.VMEM((D_STATE, GROUP_X), F32)],
        compiler_params=_cparams("parallel", "arbitrary"),
        name="ssd_fwd",
    )(xc, z, dtg, par, nw)


def _ssd_bwd(xc, z, dtg, par, nw, y, hs, dymix):
    s_dim = xc.shape[0]
    n_chunks = s_dim // Q
    xc_s, gx_s, dt_s, par_s, nw_s, hs_s = _ssd_specs(n_chunks, True)

    def body(xc_ref, z_ref, dt_ref, par_ref, nw_ref, y_ref, hs_ref, dys_ref,
             dxc_ref, dz_ref, ddt_ref, dpar_ref, dnw_ref, dht):
        @pl.when(pl.program_id(1) == 0)
        def _():
            dht[...] = jnp.zeros_like(dht)
            dpar_ref[...] = jnp.zeros_like(dpar_ref)
            dnw_ref[...] = jnp.zeros_like(dnw_ref)

        consts = _ssd_consts()
        causal, _, triu, _, reduce, lane_head = consts
        v = _ssd_common(xc_ref, dt_ref, par_ref, consts)
        x, bm, cm, xdt, s_x = v["x"], v["bm"], v["cm"], v["xdt"], v["s_x"]
        h = hs_ref[...]
        hb = h.astype(BF16)
        es_x = jnp.exp(s_x)
        yo = es_x * _dot(cm, hb)
        s_last = s_x[Q - 1:Q, :]
        e_x = jnp.exp(s_last - s_x)
        es_last = jnp.exp(s_last)

        yv, zz, nw_v = y_ref[...], z_ref[...], nw_ref[...]
        sg = _sigmoid(zz)
        gz = zz * sg
        _, n, rstd = _nrm(yv * gz, nw_v)
        dout = dys_ref[...]
        dyg, dnw = _nrm_bwd(dout, n, rstd, nw_v)
        dnw_ref[...] += dnw
        dy = dyg * gz
        dz_ref[...] = (dyg * yv * (sg * (1.0 + zz * (1.0 - sg)))).astype(BF16)

        dyb = dy.astype(BF16)
        xdt_b = xdt.astype(BF16)
        dhp = dht[...]
        dhpb = dhp.astype(BF16)
        lane = lax.broadcasted_iota(jnp.int32, (Q, LANES), 1)
        sub = lax.broadcasted_iota(jnp.int32, (LANES, Q), 0)
        dxdt = jnp.zeros((Q, GROUP_X), F32)
        dg = jnp.zeros((Q, Q), F32)
        ds = jnp.zeros((Q, LANES), F32)
        ds_t = jnp.zeros((LANES, Q), F32)
        for r in range(HEADS_PER_GROUP):
            dec = _decay(v, r, causal)
            mf = v["g"] * dec
            dyr = jnp.where(lane_head == r, dyb, jnp.zeros_like(dyb))
            dm = _dot(dyr, xdt_b, _NT)
            dxdt = dxdt + _dot(mf.astype(BF16), dyr, _TN)
            dg = dg + dm * dec
            dd = dm * mf
            ds = ds + jnp.where(lane == r, jnp.sum(dd, axis=1, keepdims=True), 0.0)
            ds_t = ds_t + jnp.where(sub == r, jnp.sum(dd, axis=0, keepdims=True), 0.0)
        ds = ds - ds_t.T
        dgb = dg.astype(BF16)
        dwb = (es_x * dy).astype(BF16)
        dcm = _dot(dgb, bm) + _dot(dwb, hb, _NT)
        dh_prev = _dot(cm, dwb, _TN)
        zst = _dot(bm, dhpb)
        xe = xdt * e_x
        dxdt = dxdt + e_x * zst
        dee = xe * zst
        dbm = _dot(dgb, cm, _TN) + _dot(xe.astype(BF16), dhpb, _NT)
        v_last = jnp.sum(dee, axis=0, keepdims=True) + es_last * jnp.sum(dhp * h, axis=0, keepdims=True)
        row_x = lax.broadcasted_iota(jnp.int32, (Q, GROUP_X), 0)
        tx = dy * yo - dee + jnp.where(row_x == Q - 1, v_last, 0.0)
        ds = ds + _dot_r01(tx, reduce)
        ddta = _dot_l01(triu, ds)
        ddt = ddta * v["a_neg"] + _dot_r01(dxdt * x, reduce)
        dalog = jnp.sum(ddta * v["dt"], axis=0, keepdims=True) * v["a_neg"]
        draw = jnp.where(lane < HEADS_PER_GROUP, ddt * _sigmoid(v["dtr"]), 0.0)
        dbias = jnp.sum(draw, axis=0, keepdims=True)
        ddsk = _dot_r01(jnp.broadcast_to(jnp.sum(dy * x, axis=0, keepdims=True), (8, GROUP_X)), reduce)[0:1, :]
        dht[...] = es_last * dhp + dh_prev
        dxc_ref[:, :GROUP_X] = dxdt * v["dt_x"] + v["dsk_x"] * dy
        dxc_ref[:, GROUP_X:GROUP_X + D_STATE] = dbm
        dxc_ref[:, GROUP_X + D_STATE:] = dcm
        ddt_ref[...] = draw
        dpar_ref[0:1, :] += dbias
        dpar_ref[1:2, :] += dalog
        dpar_ref[2:3, :] += ddsk

    return pl.pallas_call(
        body,
        grid=(SSM_GROUPS, n_chunks),
        in_specs=[xc_s, gx_s, dt_s, par_s, nw_s, gx_s, hs_s, gx_s],
        out_specs=[xc_s, gx_s, dt_s, par_s, nw_s],
        out_shape=[jax.ShapeDtypeStruct((s_dim, SSM_GROUPS * GROUP_COLS), F32),
                   jax.ShapeDtypeStruct((s_dim, D_SSM), BF16),
                   jax.ShapeDtypeStruct((SSM_GROUPS, s_dim, LANES), F32),
                   jax.ShapeDtypeStruct((SSM_GROUPS, 8, LANES), F32),
                   jax.ShapeDtypeStruct((1, D_SSM), F32)],
        scratch_shapes=[pltpu.VMEM((D_STATE, GROUP_X), F32)],
        compiler_params=_cparams("parallel", "arbitrary"),
        name="ssd_bwd",
    )(xc, z, dtg, par, nw, y, hs, dymix)


ATT_SCALE = ATT_HEAD_DIM ** -0.5
NEG_INF = -jnp.inf


def _head(h):
    return slice(h * ATT_HEAD_DIM, (h + 1) * ATT_HEAD_DIM)


def _band_masks():
    qi = lax.broadcasted_iota(jnp.int32, (ATT_BLOCK, ATT_BLOCK), 0)
    kj = lax.broadcasted_iota(jnp.int32, (ATT_BLOCK, ATT_BLOCK), 1)
    return kj <= qi, kj >= qi


def _attn_fwd(qkv_v, d):
    rows = qkv_v.shape[0]
    nb = rows // ATT_BLOCK
    blk = (ATT_BLOCK, D_ATT)
    prev = lambda i: jnp.maximum(i - 1, 0)

    def body(q_ref, kc_ref, kp_ref, vc_ref, vp_ref, o_ref, lse_ref):
        own, before = _band_masks()
        before = before & (pl.program_id(1) > 0)
        lane = lax.broadcasted_iota(jnp.int32, (ATT_BLOCK, LANES), 1)
        lse_all = jnp.zeros((ATT_BLOCK, LANES), F32)
        for h in range(ATT_HEADS):
            q = q_ref[:, _head(h)]
            sc = jnp.where(own, _dot(q, kc_ref[:, _head(h)], _NT) * ATT_SCALE, NEG_INF)
            sp = jnp.where(before, _dot(q, kp_ref[:, _head(h)], _NT) * ATT_SCALE, NEG_INF)
            m = jnp.maximum(jnp.max(sc, axis=1, keepdims=True), jnp.max(sp, axis=1, keepdims=True))
            pc, pp = jnp.exp(sc - m), jnp.exp(sp - m)
            den = jnp.sum(pc, axis=1, keepdims=True) + jnp.sum(pp, axis=1, keepdims=True)
            o = _dot(pc.astype(BF16), vc_ref[:, _head(h)]) + _dot(pp.astype(BF16), vp_ref[:, _head(h)])
            o_ref[:, _head(h)] = o / den
            lse_all = jnp.where(lane == h, m + jnp.log(den), lse_all)
        lse_ref[...] = lse_all

    return pl.pallas_call(
        body,
        grid=(d, nb),
        in_specs=[pl.BlockSpec(blk, lambda r, i: (i, 3 * r)),
                  pl.BlockSpec(blk, lambda r, i: (i, 3 * r + 1)),
                  pl.BlockSpec(blk, lambda r, i: (prev(i), 3 * r + 1)),
                  pl.BlockSpec(blk, lambda r, i: (i, 3 * r + 2)),
                  pl.BlockSpec(blk, lambda r, i: (prev(i), 3 * r + 2))],
        out_specs=[pl.BlockSpec(blk, lambda r, i: (i, r)), pl.BlockSpec((ATT_BLOCK, LANES), lambda r, i: (i, r))],
        out_shape=[jax.ShapeDtypeStruct((rows, d * D_ATT), F32), jax.ShapeDtypeStruct((rows, d * LANES), F32)],
        compiler_params=_cparams("parallel", "arbitrary"),
        name=f"attn_fwd_d{d}",
    )(qkv_v, qkv_v, qkv_v, qkv_v, qkv_v)


def _attn_combine(os_, lses):
    def fn(o1, o2, o3, l1, l2, l3):
        m = jnp.maximum(jnp.maximum(l1, l2), l3)
        tot = m + jnp.log(jnp.exp(l1 - m) + jnp.exp(l2 - m) + jnp.exp(l3 - m))
        w1, w2, w3 = jnp.exp(l1 - tot), jnp.exp(l2 - tot), jnp.exp(l3 - tot)
        cols = []
        for h in range(ATT_HEADS):
            cols.append(w1[:, h:h + 1] * o1[:, _head(h)] + w2[:, h:h + 1] * o2[:, _head(h)]
                        + w3[:, h:h + 1] * o3[:, _head(h)])
        y = jnp.concatenate(cols, axis=1)
        return y, y, tot
    return _rowcall(fn, list(os_) + list(lses), [], [(D_ATT, BF16), (D_ATT, F32), (LANES, F32)], [],
                    name="attn_combine", tr=128)


def _attn_delta(dymix, y_att):
    def fn(dy, y):
        lane = lax.broadcasted_iota(jnp.int32, (dy.shape[0], LANES), 1)
        delta = jnp.zeros((dy.shape[0], LANES), F32)
        for h in range(ATT_HEADS):
            delta = jnp.where(lane == h, jnp.sum(dy[:, _head(h)] * y[:, _head(h)], axis=1, keepdims=True), delta)
        return dy, delta
    return _rowcall(fn, [dymix, y_att], [], [(D_ATT, BF16), (LANES, F32)], [], name="attn_delta",
                    row_cols=[(D_ATT, 1), None])


def _attn_bwd(qkv_v, dy_v, lse_v, delta_v, d):
    rows = qkv_v.shape[0]
    nb = rows // ATT_BLOCK
    blk = (ATT_BLOCK, D_ATT)
    sblk = (ATT_BLOCK, LANES)
    prev = lambda i: jnp.maximum(i - 1, 0)
    nxt = lambda i: jnp.minimum(i + 1, nb - 1)

    def body(qc_ref, qn_ref, kc_ref, kp_ref, vc_ref, vp_ref, dyc_ref, dyn_ref, lc_ref, ln_ref, dc_ref, dn_ref,
             dq_ref, dk_ref, dv_ref):
        i = pl.program_id(1)
        own, before = _band_masks()
        before_c = before & (i > 0)
        before_n = before & (i < nb - 1)
        lc, ln, dc, dn = lc_ref[...], ln_ref[...], dc_ref[...], dn_ref[...]
        for h in range(ATT_HEADS):
            hs = _head(h)
            q, qn, kc, kp, vc, vp = qc_ref[:, hs], qn_ref[:, hs], kc_ref[:, hs], kp_ref[:, hs], vc_ref[:, hs], vp_ref[:, hs]
            dy, dyn = dyc_ref[:, hs], dyn_ref[:, hs]
            lse, lse_n, dl, dl_n = lc[:, h:h + 1], ln[:, h:h + 1], dc[:, h:h + 1], dn[:, h:h + 1]
            pc = jnp.exp(jnp.where(own, _dot(q, kc, _NT) * ATT_SCALE - lse, NEG_INF))
            pp = jnp.exp(jnp.where(before_c, _dot(q, kp, _NT) * ATT_SCALE - lse, NEG_INF))
            pn = jnp.exp(jnp.where(before_n, _dot(qn, kc, _NT) * ATT_SCALE - lse_n, NEG_INF))
            dsc = (pc * (_dot(dy, vc, _NT) - dl)).astype(BF16)
            dsp = (pp * (_dot(dy, vp, _NT) - dl)).astype(BF16)
            dsn = (pn * (_dot(dyn, vc, _NT) - dl_n)).astype(BF16)
            dq_ref[:, hs] = (_dot(dsc, kc) + _dot(dsp, kp)) * ATT_SCALE
            dk_ref[:, hs] = (_dot(dsc, q, _TN) + _dot(dsn, qn, _TN)) * ATT_SCALE
            dv_ref[:, hs] = _dot(pc.astype(BF16), dy, _TN) + _dot(pn.astype(BF16), dyn, _TN)

    return pl.pallas_call(
        body,
        grid=(d, nb),
        in_specs=[pl.BlockSpec(blk, lambda r, i: (i, 3 * r)), pl.BlockSpec(blk, lambda r, i: (nxt(i), 3 * r)),
                  pl.BlockSpec(blk, lambda r, i: (i, 3 * r + 1)), pl.BlockSpec(blk, lambda r, i: (prev(i), 3 * r + 1)),
                  pl.BlockSpec(blk, lambda r, i: (i, 3 * r + 2)), pl.BlockSpec(blk, lambda r, i: (prev(i), 3 * r + 2)),
                  pl.BlockSpec(blk, lambda r, i: (i, r)), pl.BlockSpec(blk, lambda r, i: (nxt(i), r)),
                  pl.BlockSpec(sblk, lambda r, i: (i, r)), pl.BlockSpec(sblk, lambda r, i: (nxt(i), r)),
                  pl.BlockSpec(sblk, lambda r, i: (i, r)), pl.BlockSpec(sblk, lambda r, i: (nxt(i), r))],
        out_specs=[pl.BlockSpec(blk, lambda r, i: (i, r))] * 3,
        out_shape=[jax.ShapeDtypeStruct((rows, d * D_ATT), F32)] * 3,
        compiler_params=_cparams("parallel", "arbitrary"),
        name=f"attn_bwd_d{d}",
    )(qkv_v, qkv_v, qkv_v, qkv_v, qkv_v, qkv_v, dy_v, dy_v, lse_v, lse_v, delta_v, delta_v)


def _attn_sum(dqs, dks, dvs):
    def fn(*parts):
        return (jnp.concatenate([parts[0] + parts[1] + parts[2], parts[3] + parts[4] + parts[5],
                                 parts[6] + parts[7] + parts[8]], axis=1),)
    return _rowcall(fn, list(dqs) + list(dks) + list(dvs), [], [(3 * D_ATT, BF16)], [], name="attn_sum", tr=128)[0]


def _attention_fwd(qkv):
    s_dim = qkv.shape[0]
    os_, lses = [], []
    for d in DILATIONS:
        o, lse = _attn_fwd(qkv.reshape(s_dim // d, d * 3 * D_ATT), d)
        os_.append(o.reshape(s_dim, D_ATT))
        lses.append(lse.reshape(s_dim, LANES))
    return _attn_combine(os_, lses)


def _attention_bwd(qkv, dymix, y_att, lse):
    s_dim = qkv.shape[0]
    dy, delta = _attn_delta(dymix, y_att)
    dqs, dks, dvs = [], [], []
    for d in DILATIONS:
        dq, dk, dv = _attn_bwd(qkv.reshape(s_dim // d, d * 3 * D_ATT), dy.reshape(s_dim // d, d * D_ATT),
                               lse.reshape(s_dim // d, d * LANES), delta.reshape(s_dim // d, d * LANES), d)
        dqs.append(dq.reshape(s_dim, D_ATT))
        dks.append(dk.reshape(s_dim, D_ATT))
        dvs.append(dv.reshape(s_dim, D_ATT))
    return _attn_sum(dqs, dks, dvs)


def _adamw(w, g, m, v, name):
    def fn(wb, gb, mb, vb):
        m2 = ADAM_B1 * mb + (1.0 - ADAM_B1) * gb
        v2 = ADAM_B2 * vb + (1.0 - ADAM_B2) * (gb * gb)
        m_hat = m2 / (1.0 - ADAM_B1 ** ADAM_STEP)
        v_hat = v2 / (1.0 - ADAM_B2 ** ADAM_STEP)
        delta = -ADAM_LR * (m_hat / (jnp.sqrt(v_hat) + ADAM_EPS) + ADAM_WD * wb)
        return delta, m2, v2
    cols = w.shape[1]
    tr = 128 if w.shape[0] % 128 == 0 else w.shape[0]
    return _rowcall(fn, [w, g, m, v], [], [(cols, F32)] * 3, [], name=name, tr=tr)


ANY = pl.BlockSpec(memory_space=pl.ANY)


def _position():
    x, y, c = lax.axis_index("x"), lax.axis_index("y"), lax.axis_index("c")
    chips = [(1 - x, y), (x, 1 - y), (1 - x, 1 - y)]
    return x, y, c, chips


def _remote(src, dst, send_sem, recv_sem, device):
    return pltpu.make_async_remote_copy(src_ref=src, dst_ref=dst, send_sem=send_sem, recv_sem=recv_sem,
                                        device_id=device, device_id_type=MESH)


def _gather_shards(shards):
    n = len(shards)

    def body(*refs):
        ins, outs = refs[:n], refs[n:2 * n]
        send_sems, recv_sems, local_sems = refs[2 * n:]
        x, y, c, chips = _position()
        sibling = (x, y, 1 - c)
        me = 2 * x + y

        def half(a, chip, cc):
            h = ins[a].shape[0] // 2
            return outs[a].at[2 * chip[0] + chip[1], pl.ds(cc * h, h), :]

        local, sent = [], []
        for a in range(n):
            h = ins[a].shape[0] // 2
            cp = pltpu.make_async_copy(ins[a], outs[a].at[me], local_sems.at[a])
            cp.start()
            local.append(cp)
            for j, chip in enumerate(chips):
                cp = _remote(ins[a].at[pl.ds(c * h, h), :], half(a, (x, y), c), send_sems.at[6 * a + j],
                             recv_sems.at[6 * a + j], (chip[0], chip[1], c))
                cp.start()
                sent.append(cp)
        for a in range(n):
            for j, chip in enumerate(chips):
                landed = half(a, chip, c)
                _remote(landed, landed, send_sems.at[6 * a + j], recv_sems.at[6 * a + j], (x, y, c)).wait_recv()
                cp = _remote(landed, landed, send_sems.at[6 * a + 3 + j], recv_sems.at[6 * a + 3 + j], sibling)
                cp.start()
                sent.append(cp)
        for a in range(n):
            for j, chip in enumerate(chips):
                handed = half(a, chip, 1 - c)
                _remote(handed, handed, send_sems.at[6 * a + 3 + j], recv_sems.at[6 * a + 3 + j], (x, y, c)).wait_recv()
        for cp in sent:
            cp.wait_send()
        for cp in local:
            cp.wait()

    return pl.pallas_call(
        body,
        in_specs=[ANY] * n,
        out_specs=[ANY] * n,
        out_shape=[jax.ShapeDtypeStruct((N_CHIPS,) + s.shape, s.dtype) for s in shards],
        scratch_shapes=[pltpu.SemaphoreType.DMA((6 * n,)), pltpu.SemaphoreType.DMA((6 * n,)),
                        pltpu.SemaphoreType.DMA((n,))],
        name="gather_shards",
    )(*shards)


def _exchange_sibling_halves(grads):
    n = len(grads)

    def body(*refs):
        ins, outs = refs[:n], refs[n:2 * n]
        send_sems, recv_sems = refs[2 * n:]
        x, y, c, _ = _position()
        copies = []
        for a in range(n):
            h = ins[a].shape[1] // 2
            cp = _remote(ins[a].at[:, pl.ds((1 - c) * h, h), :], outs[a], send_sems.at[a], recv_sems.at[a], (x, y, 1 - c))
            cp.start()
            copies.append(cp)
        for cp in copies:
            cp.wait()

    return pl.pallas_call(
        body,
        in_specs=[ANY] * n,
        out_specs=[ANY] * n,
        out_shape=[jax.ShapeDtypeStruct((g.shape[0], g.shape[1] // 2, g.shape[2]), g.dtype) for g in grads],
        scratch_shapes=[pltpu.SemaphoreType.DMA((n,)), pltpu.SemaphoreType.DMA((n,))],
        name="exchange_sibling_halves",
    )(*grads)


def _exchange_quarters(parts):
    n = len(parts)

    def body(*refs):
        ins, outs = refs[:n], refs[n:2 * n]
        send_sems, recv_sems = refs[2 * n:]
        x, y, c, chips = _position()
        copies = []
        for a in range(n):
            for j, chip in enumerate(chips):
                cp = _remote(ins[a].at[2 * chip[0] + chip[1]], outs[a].at[j], send_sems.at[3 * a + j],
                             recv_sems.at[3 * a + j], (chip[0], chip[1], c))
                cp.start()
                copies.append(cp)
        for cp in copies:
            cp.wait()

    return pl.pallas_call(
        body,
        in_specs=[ANY] * n,
        out_specs=[ANY] * n,
        out_shape=[jax.ShapeDtypeStruct((3,) + p.shape[1:], p.dtype) for p in parts],
        scratch_shapes=[pltpu.SemaphoreType.DMA((3 * n,)), pltpu.SemaphoreType.DMA((3 * n,))],
        name="exchange_quarters",
    )(*parts)


def _share_reduced_halves(halves):
    n = len(halves)

    def body(*refs):
        ins, outs = refs[:n], refs[n:2 * n]
        send_sems, recv_sems, local_sems = refs[2 * n:]
        x, y, c, _ = _position()
        copies = []
        for a in range(n):
            h = ins[a].shape[0]
            mine = outs[a].at[pl.ds(c * h, h), :]
            lc = pltpu.make_async_copy(ins[a], mine, local_sems.at[a])
            lc.start()
            cp = _remote(ins[a], mine, send_sems.at[a], recv_sems.at[a], (x, y, 1 - c))
            cp.start()
            copies += [lc, cp]
        for cp in copies:
            cp.wait()

    return pl.pallas_call(
        body,
        in_specs=[ANY] * n,
        out_specs=[ANY] * n,
        out_shape=[jax.ShapeDtypeStruct((2 * p.shape[0], p.shape[1]), p.dtype) for p in halves],
        scratch_shapes=[pltpu.SemaphoreType.DMA((n,)), pltpu.SemaphoreType.DMA((n,)), pltpu.SemaphoreType.DMA((n,))],
        name="share_reduced_halves",
    )(*halves)


def _add_sibling(grad, got, c_arr, name):
    nq, rows, cols = grad.shape
    h = rows // 2
    tr = 128
    nb = h // tr

    def body(c_ref, a_ref, b_ref, o_ref):
        o_ref[...] = a_ref[...] + b_ref[...]

    return pl.pallas_call(
        body,
        grid_spec=pltpu.PrefetchScalarGridSpec(
            num_scalar_prefetch=1, grid=(nq, nb),
            in_specs=[pl.BlockSpec((None, tr, cols), lambda q, i, c: (q, c[0] * nb + i, 0)),
                      pl.BlockSpec((None, tr, cols), lambda q, i, c: (q, i, 0))],
            out_specs=pl.BlockSpec((None, tr, cols), lambda q, i, c: (q, i, 0))),
        out_shape=jax.ShapeDtypeStruct((nq, h, cols), F32),
        compiler_params=_cparams("parallel", "parallel"),
        name=name,
    )(c_arr, grad, got)


def _add_chips(part, got, chip_arr, name):
    _, h, cols = part.shape
    tr = 128

    def body(q_ref, p_ref, g0_ref, g1_ref, g2_ref, o_ref):
        o_ref[...] = ((p_ref[...] + g0_ref[...]) + g1_ref[...]) + g2_ref[...]

    got_spec = lambda j: pl.BlockSpec((None, tr, cols), lambda i, q: (j, i, 0))
    return pl.pallas_call(
        body,
        grid_spec=pltpu.PrefetchScalarGridSpec(
            num_scalar_prefetch=1, grid=(h // tr,),
            in_specs=[pl.BlockSpec((None, tr, cols), lambda i, q: (q[0], i, 0)), got_spec(0), got_spec(1), got_spec(2)],
            out_specs=pl.BlockSpec((tr, cols), lambda i, q: (i, 0))),
        out_shape=jax.ShapeDtypeStruct((h, cols), F32),
        compiler_params=_cparams("parallel"),
        name=name,
    )(chip_arr, part, got, got, got)


def _reduce_scatter(grads, names):
    c_arr = lax.axis_index("c").astype(jnp.int32).reshape(1)
    chip_arr = (2 * lax.axis_index("x") + lax.axis_index("y")).astype(jnp.int32).reshape(1)
    from_sibling = _exchange_sibling_halves(grads)
    parts = [_add_sibling(g, s, c_arr, f"add_sibling_{nm}") for g, s, nm in zip(grads, from_sibling, names)]
    from_chips = _exchange_quarters(parts)
    halves = [_add_chips(p, f, chip_arr, f"add_chips_{nm}") for p, f, nm in zip(parts, from_chips, names)]
    return _share_reduced_halves(halves)


def _all_sum_small(v):
    n_dev = 8

    def body(v_ref, o_ref, gath, send_sems, recv_sems):
        x, y, c, _ = _position()
        me = 4 * x + 2 * y + c
        gath[me] = v_ref[...]
        copies = []
        for k in range(1, n_dev):
            peer = tuple(1 - p if (k >> s) & 1 else p for p, s in ((x, 2), (y, 1), (c, 0)))
            cp = _remote(v_ref, gath.at[me], send_sems.at[k - 1], recv_sems.at[k - 1], peer)
            cp.start()
            copies.append(cp)
        for cp in copies:
            cp.wait()
        acc = gath[0]
        for i in range(1, n_dev):
            acc = acc + gath[i]
        o_ref[...] = acc

    vm = pl.BlockSpec(memory_space=pltpu.VMEM)
    return pl.pallas_call(
        body,
        in_specs=[vm],
        out_specs=vm,
        out_shape=jax.ShapeDtypeStruct(v.shape, F32),
        scratch_shapes=[pltpu.VMEM((n_dev,) + v.shape, F32), pltpu.SemaphoreType.DMA((n_dev - 1,)),
                        pltpu.SemaphoreType.DMA((n_dev - 1,))],
        name="all_sum_small",
    )(v)


def _pack_rows(vectors):
    rows = []
    for v in vectors:
        flat = v.reshape(-1).astype(F32)
        rows.append(jnp.pad(flat, (0, (-flat.shape[0]) % LANES)).reshape(-1, LANES))
    out = jnp.concatenate(rows, axis=0)
    return jnp.pad(out, ((0, (-out.shape[0]) % 8), (0, 0)))


def _unpack_rows(packed, shapes):
    outs, r = [], 0
    for shp in shapes:
        size = math.prod(shp)
        nr = -(-size // LANES)
        outs.append(packed[r:r + nr].reshape(-1)[:size].reshape(shp))
        r += nr
    return outs


def _relu_sq(acc):
    r = jnp.maximum(acc, 0.0)
    return r, r * r


def _relu_sq_bwd(acc, r):
    return (acc * (2.0 * r.astype(F32)),)


def kernel(x, norm_mix_pre, w_in, conv_w, conv_b, dt_bias, a_log, d_skip, ssm_norm_w, w_out, norm_mix_post, norm_mlp_pre, w_up, w_down, norm_mlp_post, loss_target, m_norm_mix_pre, m_w_in, m_conv_w, m_conv_b, m_dt_bias, m_a_log, m_d_skip, m_ssm_norm_w, m_w_out, m_norm_mix_post, m_norm_mlp_pre, m_w_up, m_w_down, m_norm_mlp_post, v_norm_mix_pre, v_w_in, v_conv_w, v_conv_b, v_dt_bias, v_a_log, v_d_skip, v_ssm_norm_w, v_w_out, v_norm_mix_post, v_norm_mlp_pre, v_w_up, v_w_down, v_norm_mlp_post):
    s_dim = x.shape[1]
    xs, target = x[0], loss_target[0]
    chip = 2 * lax.axis_index("x") + lax.axis_index("y")

    g_in, g_out, g_up, g_down = _gather_shards(
        [w_in[0].astype(BF16), w_out[0].astype(BF16), w_up[0].astype(BF16), w_down[0].astype(BF16)])
    w_in_full = g_in.transpose(1, 0, 2).reshape(D_MODEL, D_IN_PROJ)
    w_z = w_in_full[:, :D_SSM]
    w_xbc = _perm_cols(w_in_full[:, D_SSM:D_SSM + D_XBC])
    w_dt = jnp.pad(w_in_full[:, D_SSM + D_XBC:D_SSM + D_XBC + SSM_HEADS], ((0, 0), (0, LANES - SSM_HEADS)))
    w_qkv = w_in_full[:, D_SSM + D_XBC + SSM_HEADS:]
    w_out_full = g_out.reshape(D_MIX, D_MODEL)
    w_up_full = g_up.transpose(1, 0, 2).reshape(D_MODEL, D_FF)
    w_down_full = g_down.reshape(D_FF, D_MODEL)

    conv_cols = D_XBC // N_CHIPS
    conv_placed = lax.dynamic_update_slice(jnp.zeros((8, D_XBC), F32), 0.5 * conv_w[0], (0, chip * conv_cols))
    conv_full = _all_sum_small(conv_placed.reshape(-1, LANES)).reshape(8, D_XBC)
    w8 = _perm_cols(conv_full.at[CONV_WIDTH].set(conv_b[0]))

    u = _pre_norm(xs, norm_mix_pre)
    z = _matmul([(u, w_z, 512)], "nn", [F32], name="proj_z")
    xbc = _matmul([(u, w_xbc, 512)], "nn", [F32], name="proj_xbc")
    dt_raw = _matmul([(u, w_dt, 512)], "nn", [F32], name="proj_dt")
    qkv = _matmul([(u, w_qkv, 512)], "nn", [BF16], name="proj_qkv")
    xc = _conv_fwd(xbc, w8)
    dtg = _dt_to_groups(dt_raw)
    par = _pack_ssd_params(dt_bias[0], a_log[0], d_skip[0])
    y, y_ssm, states = _ssd_fwd(xc, z, dtg, par, ssm_norm_w)
    y_att, y_att_f32, lse = _attention_fwd(qkv)
    y_mix = jnp.concatenate([y_ssm, y_att], axis=1)
    mix = _matmul([(y_mix, w_out_full, 512)], "nn", [F32], name="out_proj")
    h1, u2 = _post_pre_norm(xs, mix, norm_mix_post, norm_mlp_pre)
    hid, act = _matmul([(u2, w_up_full, 512)], "nn", [BF16, BF16], name="mlp_up", epilogue=_relu_sq)
    ff = _matmul([(act, w_down_full, 512)], "nn", [F32], name="mlp_down")
    dh2, dff, d_g4, loss_part = _tail(ff, h1, target, norm_mlp_post)

    dhid = _matmul([(dff, w_down_full, 512)], "nt", [BF16], name="mlp_down_dx", epilogue=_relu_sq_bwd, extras=[hid])
    dw_down = _matmul([(act, dff, 512)], "tn", [F32], name="mlp_down_dw")
    dw_up = _matmul([(u2, dhid, 512)], "tn", [F32], name="mlp_up_dw")
    du2 = _matmul([(dhid, w_up_full, 512)], "nt", [F32], name="mlp_up_dx")
    dh1, dmix, d_g3, d_g2 = _mid_bwd(du2, h1, dh2, mix, norm_mix_post, norm_mlp_pre)
    dymix = _matmul([(dmix, w_out_full, 512)], "nt", [F32], name="out_proj_dx")
    dw_out = _matmul([(y_mix, dmix, 512)], "tn", [F32], name="out_proj_dw")
    dqkv = _attention_bwd(qkv, dymix, y_att_f32, lse)
    dxc, dz, ddtg, dpar, d_nw = _ssd_bwd(xc, z, dtg, par, ssm_norm_w, y, states, dymix)
    dxbc, dw8 = _conv_bwd(xbc, w8, dxc)
    ddt = jnp.pad(_dt_from_groups(ddtg), ((0, 0), (0, LANES - SSM_HEADS))).astype(BF16)
    du = _matmul([(dz, w_z, 512), (dxbc, w_xbc, 512), (dqkv, w_qkv, 512), (ddt, w_dt, LANES)], "nt", [F32],
                 name="proj_dx")
    dw_z = _matmul([(u, dz, 512)], "tn", [F32], name="proj_z_dw")
    dw_xbc = _matmul([(u, dxbc, 512)], "tn", [F32], name="proj_xbc_dw")
    dw_dt = _matmul([(u, ddt, 512)], "tn", [F32], name="proj_dt_dw")
    dw_qkv = _matmul([(u, dqkv, 512)], "tn", [F32], name="proj_qkv_dw")
    grad_x, d_g1 = _first_bwd(du, xs, dh1, norm_mix_pre)

    dw_in = jnp.concatenate([dw_z, _unperm_cols(dw_xbc), dw_dt[:, :SSM_HEADS], dw_qkv], axis=1)
    big = _reduce_scatter(
        [dw_in.reshape(D_MODEL, N_CHIPS, W_IN_SHARD).transpose(1, 0, 2),
         dw_out.reshape(N_CHIPS, D_MIX // N_CHIPS, D_MODEL),
         dw_up.reshape(D_MODEL, N_CHIPS, D_FF // N_CHIPS).transpose(1, 0, 2),
         dw_down.reshape(N_CHIPS, D_FF // N_CHIPS, D_MODEL)],
        ["w_in", "w_out", "w_up", "w_down"])
    dconv = _unperm_cols(dw8)
    d_bias, d_alog, d_dskip = _unpack_ssd_params(dpar)
    small_shapes = [(1, D_MODEL), (CONV_WIDTH, D_XBC), (1, D_XBC), (1, SSM_HEADS), (1, SSM_HEADS), (1, SSM_HEADS),
                    (1, D_SSM), (1, D_MODEL), (1, D_MODEL), (1, D_MODEL), (1, LANES)]
    summed = _unpack_rows(
        _all_sum_small(_pack_rows([d_g1, dconv[:CONV_WIDTH], dconv[CONV_WIDTH:CONV_WIDTH + 1], d_bias, d_alog,
                                   d_dskip, d_nw, d_g2, d_g3, d_g4, loss_part])), small_shapes)
    (g_g1, g_conv_full, g_conv_b, g_bias, g_alog, g_dskip, g_nw, g_g2, g_g3, g_g4, loss_row) = summed
    loss = loss_row[0, 0]
    g_conv_w = lax.dynamic_slice(g_conv_full, (0, chip * conv_cols), (CONV_WIDTH, conv_cols))[None]

    grads = {"norm_mix_pre": g_g1, "w_in": big[0][None], "conv_w": g_conv_w, "conv_b": g_conv_b, "dt_bias": g_bias,
             "a_log": g_alog, "d_skip": g_dskip, "ssm_norm_w": g_nw, "w_out": big[1][None], "norm_mix_post": g_g2,
             "norm_mlp_pre": g_g3, "w_up": big[2][None], "w_down": big[3][None], "norm_mlp_post": g_g4}
    weights = {"norm_mix_pre": (norm_mix_pre, m_norm_mix_pre, v_norm_mix_pre), "w_in": (w_in, m_w_in, v_w_in),
               "conv_w": (conv_w, m_conv_w, v_conv_w), "conv_b": (conv_b, m_conv_b, v_conv_b),
               "dt_bias": (dt_bias, m_dt_bias, v_dt_bias), "a_log": (a_log, m_a_log, v_a_log),
               "d_skip": (d_skip, m_d_skip, v_d_skip), "ssm_norm_w": (ssm_norm_w, m_ssm_norm_w, v_ssm_norm_w),
               "w_out": (w_out, m_w_out, v_w_out), "norm_mix_post": (norm_mix_post, m_norm_mix_post, v_norm_mix_post),
               "norm_mlp_pre": (norm_mlp_pre, m_norm_mlp_pre, v_norm_mlp_pre), "w_up": (w_up, m_w_up, v_w_up),
               "w_down": (w_down, m_w_down, v_w_down),
               "norm_mlp_post": (norm_mlp_post, m_norm_mlp_post, v_norm_mlp_post)}
    order = list(weights)
    big_names = ("w_in", "w_out", "w_up", "w_down")
    small_names = [n for n in order if n not in big_names]
    delta, new_m, new_v = {}, {}, {}
    for n in big_names:
        w, m, v = weights[n]
        d_, m_, v_ = _adamw(w[0], grads[n][0], m[0], v[0], f"adamw_{n}")
        delta[n], new_m[n], new_v[n] = d_[None], m_[None], v_[None]
    small_w_shapes = [weights[n][0].shape for n in small_names]
    packed = [_pack_rows([weights[n][k] for n in small_names]) for k in range(3)]
    packed_g = _pack_rows([grads[n].reshape(weights[n][0].shape) for n in small_names])
    sd, sm, sv = _adamw(packed[0], packed_g, packed[1], packed[2], "adamw_small")
    for k, n in enumerate(small_names):
        grads[n] = grads[n].reshape(weights[n][0].shape)
    for res, pk in ((delta, sd), (new_m, sm), (new_v, sv)):
        for n, val in zip(small_names, _unpack_rows(pk, small_w_shapes)):
            res[n] = val

    return (loss, grad_x[None], *[grads[n] for n in order], *[delta[n] for n in order],
            *[new_m[n] for n in order], *[new_v[n] for n in order])
```

```python
import functools
import math

import numpy as np
import jax
import jax.numpy as jnp
from jax import lax
from jax.experimental import pallas as pl
from jax.experimental.pallas import tpu as pltpu

F32 = jnp.float32
BF16 = jnp.bfloat16

D_MODEL = 2048
SSM_HEAD_DIM = 64
SSM_GROUPS = 8
HEADS_PER_GROUP = 4
SSM_HEADS = SSM_GROUPS * HEADS_PER_GROUP
D_SSM = SSM_HEADS * SSM_HEAD_DIM
D_STATE = 128
CONV_WIDTH = 4
SSD_CHUNK = 128
D_XBC = D_SSM + 2 * SSM_GROUPS * D_STATE
GROUP_X = HEADS_PER_GROUP * SSM_HEAD_DIM
GROUP_COLS = GROUP_X + 2 * D_STATE
ATT_HEAD_DIM = 128
ATT_HEADS = 16
D_ATT = ATT_HEADS * ATT_HEAD_DIM
DILATIONS = (1, 4, 16)
ATT_BLOCK = 128
D_MIX = D_SSM + D_ATT
D_IN_PROJ = D_SSM + D_XBC + SSM_HEADS + 3 * D_ATT
D_FF = 4 * D_MODEL
EPS = 1e-6
N_CHIPS = 4
W_IN_SHARD = D_IN_PROJ // N_CHIPS

ADAM_LR = 0.001
ADAM_B1 = 0.9
ADAM_B2 = 0.999
ADAM_EPS = 1e-08
ADAM_WD = 0.01
ADAM_STEP = 10

LANES = 128
VMEM_LIMIT = 48 * 1024 * 1024
MESH = pl.DeviceIdType.MESH

_NN = (((1,), (0,)), ((), ()))
_NT = (((1,), (1,)), ((), ()))
_TN = (((0,), (0,)), ((), ()))


def _dot(a, b, dims=_NN):
    return lax.dot_general(a, b, dims, preferred_element_type=F32)


def _cparams(*sem):
    return pltpu.CompilerParams(dimension_semantics=sem, vmem_limit_bytes=VMEM_LIMIT)


TK = 2048
TK_MULTI = 1024


def _matmul(pairs, mode, out_dtypes, *, name, tm=1024, tn=1024, epilogue=None, extras=()):
    a0, b0, _ = pairs[0]
    m_dim = a0.shape[1] if mode == "tn" else a0.shape[0]
    n_dim = b0.shape[0] if mode == "nt" else b0.shape[1]
    tm, tn = min(tm, m_dim), min(tn, n_dim)
    nks, offs = [], []
    for a, _, tk in pairs:
        k_dim = a.shape[0] if mode == "tn" else a.shape[1]
        assert k_dim % tk == 0, (name, k_dim, tk)
        offs.append(sum(nks))
        nks.append(k_dim // tk)
    nk_total = sum(nks)
    assert m_dim % tm == 0 and n_dim % tn == 0, (name, m_dim, n_dim)
    dims = {"nn": _NN, "nt": _NT, "tn": _TN}[mode]
    n_pairs, n_extra, n_out = len(pairs), len(extras), len(out_dtypes)

    in_specs, operands = [], []
    for (a, b, tk), off, nk in zip(pairs, offs, nks):
        def kidx(k, off=off, nk=nk):
            return k if n_pairs == 1 else jnp.clip(k - off, 0, nk - 1)
        if mode == "tn":
            in_specs.append(pl.BlockSpec((tk, tm), lambda m, n, k, f=kidx: (f(k), m)))
        else:
            in_specs.append(pl.BlockSpec((tm, tk), lambda m, n, k, f=kidx: (m, f(k))))
        if mode == "nt":
            in_specs.append(pl.BlockSpec((tn, tk), lambda m, n, k, f=kidx: (n, f(k))))
        else:
            in_specs.append(pl.BlockSpec((tk, tn), lambda m, n, k, f=kidx: (f(k), n)))
        operands += [a, b]
    for e in extras:
        in_specs.append(pl.BlockSpec((tm, tn), lambda m, n, k: (m, n)))
        operands.append(e)

    def body(*refs):
        ab = refs[:2 * n_pairs]
        e_refs = refs[2 * n_pairs:2 * n_pairs + n_extra]
        o_refs = refs[2 * n_pairs + n_extra:2 * n_pairs + n_extra + n_out]

        def finish(total):
            vals = (total,) if epilogue is None else epilogue(total, *[e[...] for e in e_refs])
            for o_ref, v in zip(o_refs, vals):
                o_ref[...] = v.astype(o_ref.dtype)

        if nk_total == 1:
            finish(_dot(ab[0][...], ab[1][...], dims))
            return
        acc = refs[-1]
        k = pl.program_id(2)

        @pl.when(k == 0)
        def _():
            acc[...] = jnp.zeros_like(acc)

        for i in range(n_pairs):
            def accumulate(i=i):
                acc[...] += _dot(ab[2 * i][...], ab[2 * i + 1][...], dims)
            if n_pairs == 1:
                accumulate()
            else:
                pl.when((k >= offs[i]) & (k < offs[i] + nks[i]))(accumulate)

        @pl.when(k == nk_total - 1)
        def _():
            finish(acc[...])

    outs = pl.pallas_call(
        body,
        grid=(m_dim // tm, n_dim // tn, nk_total),
        in_specs=in_specs,
        out_specs=[pl.BlockSpec((tm, tn), lambda m, n, k: (m, n)) for _ in out_dtypes],
        out_shape=[jax.ShapeDtypeStruct((m_dim, n_dim), dt) for dt in out_dtypes],
        scratch_shapes=[pltpu.VMEM((tm, tn), F32)] if nk_total > 1 else [],
        compiler_params=_cparams("parallel", "parallel", "arbitrary"),
        name=name,
    )(*operands)
    return outs[0] if n_out == 1 else outs


def _rowcall(fn, rows, vecs, row_outs, acc_widths, *, name, tr=256, row_cols=None):
    s_dim = rows[0].shape[0]
    assert s_dim % tr == 0
    row_cols = row_cols or [None] * len(rows)
    n_r, n_v, n_ro, n_acc = len(rows), len(vecs), len(row_outs), len(acc_widths)
    in_specs = []
    for r, rc in zip(rows, row_cols):
        if rc is None:
            in_specs.append(pl.BlockSpec((tr, r.shape[1]), lambda i: (i, 0)))
        else:
            in_specs.append(pl.BlockSpec((tr, rc[0]), lambda i, c=rc[1]: (i, c)))
    for v in vecs:
        in_specs.append(pl.BlockSpec(v.shape, lambda i, nd=v.ndim: (0,) * nd))

    def body(*refs):
        ins = [r[...] for r in refs[:n_r + n_v]]
        ro = refs[n_r + n_v:n_r + n_v + n_ro]
        ao = refs[n_r + n_v + n_ro:]
        outs = fn(*ins)
        for ref, v in zip(ro, outs[:n_ro]):
            ref[...] = v.astype(ref.dtype)
        if n_acc:
            @pl.when(pl.program_id(0) == 0)
            def _():
                for ref in ao:
                    ref[...] = jnp.zeros_like(ref)
            for ref, v in zip(ao, outs[n_ro:]):
                ref[...] += v

    outs = pl.pallas_call(
        body,
        grid=(s_dim // tr,),
        in_specs=in_specs,
        out_specs=[pl.BlockSpec((tr, w), lambda i: (i, 0)) for w, _ in row_outs]
        + [pl.BlockSpec((1, w), lambda i: (0, 0)) for w in acc_widths],
        out_shape=[jax.ShapeDtypeStruct((s_dim, w), dt) for w, dt in row_outs]
        + [jax.ShapeDtypeStruct((1, w), F32) for w in acc_widths],
        compiler_params=_cparams("arbitrary"),
        name=name,
    )(*rows, *vecs)
    return outs


def _nrm(x, g):
    r = lax.rsqrt(jnp.mean(x * x, axis=-1, keepdims=True) + EPS)
    n = x * r
    return n * g, n, r


def _nrm_bwd(dy, n, r, g):
    dn = dy * g
    dx = r * (dn - n * jnp.mean(dn * n, axis=-1, keepdims=True))
    return dx, jnp.sum(dy * n, axis=0, keepdims=True)


def _sigmoid(x):
    return 1.0 / (1.0 + jnp.exp(-x))


def _softplus(x):
    return jnp.maximum(x, 0.0) + jnp.log(1.0 + jnp.exp(-jnp.abs(x)))


def _pre_norm(x, g1):
    def fn(xb, g):
        return (_nrm(xb, g)[0],)
    return _rowcall(fn, [x], [g1], [(D_MODEL, BF16)], [], name="pre_norm")[0]


def _post_pre_norm(x, mix, g2, g3):
    def fn(xb, mb, g2b, g3b):
        h1 = xb + _nrm(mb, g2b)[0]
        return h1, _nrm(h1, g3b)[0]
    return _rowcall(fn, [x, mix], [g2, g3], [(D_MODEL, F32), (D_MODEL, BF16)], [], name="post_pre_norm")


def _tail(ff, h1, target, g4):
    def fn(ffb, h1b, tb, g):
        y, n, r = _nrm(ffb, g)
        e = h1b + y - tb
        loss = 0.5 * jnp.sum(jnp.sum(e * e, axis=-1, keepdims=True) * (1.0 / D_MODEL), axis=0, keepdims=True)
        dh2 = e * (1.0 / D_MODEL)
        dff, dg = _nrm_bwd(dh2, n, r, g)
        return dh2, dff, dg, jnp.broadcast_to(loss, (1, LANES))
    return _rowcall(fn, [ff, h1, target], [g4], [(D_MODEL, F32), (D_MODEL, BF16)], [D_MODEL, LANES], name="tail")


def _mid_bwd(du2, h1, dh2, mix, g2, g3):
    def fn(du2b, h1b, dh2b, mb, g2b, g3b):
        _, n3, r3 = _nrm(h1b, g3b)
        d3, dg3 = _nrm_bwd(du2b, n3, r3, g3b)
        dh1 = dh2b + d3
        _, n2, r2 = _nrm(mb, g2b)
        dmix, dg2 = _nrm_bwd(dh1, n2, r2, g2b)
        return dh1, dmix, dg3, dg2
    return _rowcall(fn, [du2, h1, dh2, mix], [g2, g3], [(D_MODEL, F32), (D_MODEL, BF16)], [D_MODEL, D_MODEL],
                    name="mid_bwd")


def _first_bwd(du, x, dh1, g1):
    def fn(dub, xb, dh1b, g):
        _, n, r = _nrm(xb, g)
        dx, dg = _nrm_bwd(dub, n, r, g)
        return dh1b + dx, dg
    return _rowcall(fn, [du, x, dh1], [g1], [(D_MODEL, F32)], [D_MODEL], name="first_bwd")


CONV_TILE = 256
CONV_ROWS = 256
PAD = 8


def _conv_taps(w):
    return [w[k:k + 1, :] for k in range(CONV_WIDTH)], w[CONV_WIDTH:CONV_WIDTH + 1, :]


def _conv_fwd(xbc, w8):
    s_dim, c_dim = xbc.shape
    n_steps = s_dim // CONV_ROWS

    def body(x_ref, w_ref, o_ref, xp):
        xp[0:PAD, :] = jnp.zeros((PAD, CONV_TILE), F32)
        xp[PAD:PAD + s_dim, :] = x_ref[...]
        taps, bias = _conv_taps(w_ref[...])

        def step(c, carry):
            base = pl.multiple_of(c * CONV_ROWS, CONV_ROWS)
            win = xp[pl.ds(base, CONV_ROWS + PAD), :]
            pre = bias + taps[3] * win[PAD:, :]
            for j in range(1, CONV_WIDTH):
                pre = pre + taps[3 - j] * pltpu.roll(win, j, axis=0)[PAD:, :]
            o_ref[pl.ds(base, CONV_ROWS), :] = pre * _sigmoid(pre)
            return carry

        lax.fori_loop(0, n_steps, step, 0)

    return pl.pallas_call(
        body,
        grid=(c_dim // CONV_TILE,),
        in_specs=[pl.BlockSpec((s_dim, CONV_TILE), lambda j: (0, j)), pl.BlockSpec((8, CONV_TILE), lambda j: (0, j))],
        out_specs=pl.BlockSpec((s_dim, CONV_TILE), lambda j: (0, j)),
        out_shape=jax.ShapeDtypeStruct((s_dim, c_dim), F32),
        scratch_shapes=[pltpu.VMEM((s_dim + 2 * PAD, CONV_TILE), F32)],
        compiler_params=_cparams("parallel"),
        name="conv_fwd",
    )(xbc, w8)


def _conv_bwd(xbc, w8, dxc):
    s_dim, c_dim = xbc.shape
    n_steps = s_dim // CONV_ROWS

    def body(x_ref, w_ref, d_ref, dx_ref, dw_ref, xp, dp):
        xp[0:PAD, :] = jnp.zeros((PAD, CONV_TILE), F32)
        xp[PAD:PAD + s_dim, :] = x_ref[...]
        dp[PAD + s_dim:, :] = jnp.zeros((PAD, CONV_TILE), F32)
        taps, bias = _conv_taps(w_ref[...])

        def step1(c, sums):
            base = pl.multiple_of(c * CONV_ROWS, CONV_ROWS)
            win = xp[pl.ds(base, CONV_ROWS + PAD), :]
            shifted = [win[PAD:, :]] + [pltpu.roll(win, j, axis=0)[PAD:, :] for j in range(1, CONV_WIDTH)]
            pre = bias
            for j in range(CONV_WIDTH):
                pre = pre + taps[3 - j] * shifted[j]
            sg = _sigmoid(pre)
            dpre = d_ref[pl.ds(base, CONV_ROWS), :] * (sg * (1.0 + pre * (1.0 - sg)))
            dp[pl.ds(base + PAD, CONV_ROWS), :] = dpre
            new = [sums[k] + jnp.sum(dpre * shifted[3 - k], axis=0, keepdims=True) for k in range(CONV_WIDTH)]
            new.append(sums[CONV_WIDTH] + jnp.sum(dpre, axis=0, keepdims=True))
            return tuple(new)

        zero = jnp.zeros((1, CONV_TILE), F32)
        sums = lax.fori_loop(0, n_steps, step1, (zero,) * (CONV_WIDTH + 1))
        dw_ref[...] = jnp.zeros((8, CONV_TILE), F32)
        for k in range(CONV_WIDTH + 1):
            dw_ref[k:k + 1, :] = sums[k]

        def step2(c, carry):
            base = pl.multiple_of(c * CONV_ROWS, CONV_ROWS)
            win = dp[pl.ds(base + PAD, CONV_ROWS + PAD), :]
            dx = taps[3] * win[:CONV_ROWS, :]
            for j in range(1, CONV_WIDTH):
                dx = dx + taps[3 - j] * pltpu.roll(win, CONV_ROWS + PAD - j, axis=0)[:CONV_ROWS, :]
            dx_ref[pl.ds(base, CONV_ROWS), :] = dx.astype(BF16)
            return carry

        lax.fori_loop(0, n_steps, step2, 0)

    col = lambda j: (0, j)
    return pl.pallas_call(
        body,
        grid=(c_dim // CONV_TILE,),
        in_specs=[pl.BlockSpec((s_dim, CONV_TILE), col), pl.BlockSpec((8, CONV_TILE), col),
                  pl.BlockSpec((s_dim, CONV_TILE), col)],
        out_specs=[pl.BlockSpec((s_dim, CONV_TILE), col), pl.BlockSpec((8, CONV_TILE), col)],
        out_shape=[jax.ShapeDtypeStruct((s_dim, c_dim), BF16), jax.ShapeDtypeStruct((8, c_dim), F32)],
        scratch_shapes=[pltpu.VMEM((s_dim + 2 * PAD, CONV_TILE), F32), pltpu.VMEM((s_dim + 2 * PAD, CONV_TILE), F32)],
        compiler_params=_cparams("parallel"),
        name="conv_bwd",
    )(xbc, w8, dxc)


def _perm_cols(a):
    parts = []
    for g in range(SSM_GROUPS):
        parts += [a[..., g * GROUP_X:(g + 1) * GROUP_X],
                  a[..., D_SSM + g * D_STATE:D_SSM + (g + 1) * D_STATE],
                  a[..., D_SSM + SSM_GROUPS * D_STATE + g * D_STATE:D_SSM + SSM_GROUPS * D_STATE + (g + 1) * D_STATE]]
    return jnp.concatenate(parts, axis=-1)


def _unperm_cols(a):
    xs = [a[..., g * GROUP_COLS:g * GROUP_COLS + GROUP_X] for g in range(SSM_GROUPS)]
    bs = [a[..., g * GROUP_COLS + GROUP_X:g * GROUP_COLS + GROUP_X + D_STATE] for g in range(SSM_GROUPS)]
    cs = [a[..., g * GROUP_COLS + GROUP_X + D_STATE:(g + 1) * GROUP_COLS] for g in range(SSM_GROUPS)]
    return jnp.concatenate(xs + bs + cs, axis=-1)


def _dt_to_groups(dt):
    s_dim = dt.shape[0]
    t = dt[:, :SSM_HEADS].reshape(s_dim, SSM_GROUPS, HEADS_PER_GROUP).transpose(1, 0, 2)
    return jnp.pad(t, ((0, 0), (0, 0), (0, LANES - HEADS_PER_GROUP)))


def _dt_from_groups(dtg):
    s_dim = dtg.shape[1]
    return dtg[:, :, :HEADS_PER_GROUP].transpose(1, 0, 2).reshape(s_dim, SSM_HEADS)


def _pack_ssd_params(dt_bias, a_log, d_skip):
    rows = jnp.stack([p.reshape(SSM_GROUPS, HEADS_PER_GROUP) for p in (dt_bias, a_log, d_skip)], axis=1)
    return jnp.pad(rows, ((0, 0), (0, 8 - 3), (0, LANES - HEADS_PER_GROUP)))


def _unpack_ssd_params(par):
    return tuple(par[:, k, :HEADS_PER_GROUP].reshape(SSM_HEADS) for k in range(3))


Q = SSD_CHUNK


def _split3(v):
    hi = v.astype(BF16)
    r1 = v - hi.astype(F32)
    mid = r1.astype(BF16)
    lo = (r1 - mid.astype(F32)).astype(BF16)
    return hi, mid, lo


def _dot_l01(t01, v):
    return sum(_dot(t01, p) for p in _split3(v))


def _dot_r01(v, e01):
    return sum(_dot(p, e01) for p in _split3(v))


def _ssd_consts():
    row = lax.broadcasted_iota(jnp.int32, (Q, Q), 0)
    col = lax.broadcasted_iota(jnp.int32, (Q, Q), 1)
    causal = row >= col
    tril = causal.astype(BF16)
    triu = (col >= row).astype(BF16)
    er = lax.broadcasted_iota(jnp.int32, (LANES, GROUP_X), 0)
    ec = lax.broadcasted_iota(jnp.int32, (LANES, GROUP_X), 1) // SSM_HEAD_DIM
    expand = (er == ec).astype(BF16)
    rr = lax.broadcasted_iota(jnp.int32, (GROUP_X, LANES), 0) // SSM_HEAD_DIM
    rc = lax.broadcasted_iota(jnp.int32, (GROUP_X, LANES), 1)
    reduce = (rr == rc).astype(BF16)
    lane_head = lax.broadcasted_iota(jnp.int32, (Q, GROUP_X), 1) // SSM_HEAD_DIM
    return causal, tril, triu, expand, reduce, lane_head


def _ssd_common(xc_ref, dt_ref, par_ref, consts):
    causal, tril, _, expand, _, _ = consts
    par = par_ref[...]
    bias, alog, dsk = par[0:1, :], par[1:2, :], par[2:3, :]
    a_neg = -jnp.exp(alog)
    dtr = dt_ref[...] + bias
    dt = _softplus(dtr)
    s = _dot_l01(tril, dt * a_neg)
    dt_x = _dot_r01(dt, expand)
    s_x = _dot_r01(s, expand)
    dsk_x = _dot_r01(jnp.broadcast_to(dsk, (8, LANES)), expand)[0:1, :]
    blk = xc_ref[...]
    x = blk[:, :GROUP_X]
    bm = blk[:, GROUP_X:GROUP_X + D_STATE].astype(BF16)
    cm = blk[:, GROUP_X + D_STATE:].astype(BF16)
    xdt = x * dt_x
    g = _dot(cm, bm, _NT)
    return dict(a_neg=a_neg, dtr=dtr, dt=dt, s=s, s_t=s.T, dt_x=dt_x, s_x=s_x, dsk_x=dsk_x, x=x, bm=bm, cm=cm,
                xdt=xdt, g=g)


def _decay(v, r, causal):
    diff = v["s"][:, r:r + 1] - v["s_t"][r:r + 1, :]
    return jnp.exp(jnp.where(causal, diff, -jnp.inf))


def _ssd_specs(n_chunks, rev):
    cidx = (lambda c: n_chunks - 1 - c) if rev else (lambda c: c)
    xc = pl.BlockSpec((Q, GROUP_COLS), lambda g, c: (cidx(c), g))
    gx = pl.BlockSpec((Q, GROUP_X), lambda g, c: (cidx(c), g))
    dt = pl.BlockSpec((None, Q, LANES), lambda g, c: (g, cidx(c), 0))
    par = pl.BlockSpec((None, 8, LANES), lambda g, c: (g, 0, 0))
    nw = pl.BlockSpec((1, GROUP_X), lambda g, c: (0, g))
    hs = pl.BlockSpec((None, None, D_STATE, GROUP_X), lambda g, c: (cidx(c), g, 0, 0))
    return xc, gx, dt, par, nw, hs


def _ssd_fwd(xc, z, dtg, par, nw):
    s_dim = xc.shape[0]
    n_chunks = s_dim // Q
    xc_s, gx_s, dt_s, par_s, nw_s, hs_s = _ssd_specs(n_chunks, False)

    def body(xc_ref, z_ref, dt_ref, par_ref, nw_ref, y_ref, ys_ref, hs_ref, ht):
        @pl.when(pl.program_id(1) == 0)
        def _():
            ht[...] = jnp.zeros_like(ht)

        consts = _ssd_consts()
        causal, lane_head = consts[0], consts[5]
        v = _ssd_common(xc_ref, dt_ref, par_ref, consts)
        xdt_b = v["xdt"].astype(BF16)
        yd = jnp.zeros((Q, GROUP_X), F32)
        for r in range(HEADS_PER_GROUP):
            m = (v["g"] * _decay(v, r, causal)).astype(BF16)
            yd = yd + _dot(m, jnp.where(lane_head == r, xdt_b, jnp.zeros_like(xdt_b)))
        h = ht[...]
        hs_ref[...] = h
        yo = jnp.exp(v["s_x"]) * _dot(v["cm"], h.astype(BF16))
        y = yd + yo + v["dsk_x"] * v["x"]
        s_last = v["s_x"][Q - 1:Q, :]
        snew = _dot(v["bm"], (v["xdt"] * jnp.exp(s_last - v["s_x"])).astype(BF16), _TN)
        ht[...] = jnp.exp(s_last) * h + snew
        zz = z_ref[...]
        yg = y * (zz * _sigmoid(zz))
        y_ref[...] = y
        ys_ref[...] = _nrm(yg, nw_ref[...])[0].astype(BF16)

    return pl.pallas_call(
        body,
        grid=(SSM_GROUPS, n_chunks),
        in_specs=[xc_s, gx_s, dt_s, par_s, nw_s],
        out_specs=[gx_s, gx_s, hs_s],
        out_shape=[jax.ShapeDtypeStruct((s_dim, D_SSM), F32), jax.ShapeDtypeStruct((s_dim, D_SSM), BF16),
                   jax.ShapeDtypeStruct((n_chunks, SSM_GROUPS, D_STATE, GROUP_X), F32)],
        scratch_shapes=[pltpu.VMEM((D_STATE, GROUP_X), F32)],
        compiler_params=_cparams("parallel", "arbitrary"),
        name="ssd_fwd",
    )(xc, z, dtg, par, nw)


def _ssd_bwd(xc, z, dtg, par, nw, y, hs, dymix):
    s_dim = xc.shape[0]
    n_chunks = s_dim // Q
    xc_s, gx_s, dt_s, par_s, nw_s, hs_s = _ssd_specs(n_chunks, True)

    def body(xc_ref, z_ref, dt_ref, par_ref, nw_ref, y_ref, hs_ref, dys_ref,
             dxc_ref, dz_ref, ddt_ref, dpar_ref, dnw_ref, dht):
        @pl.when(pl.program_id(1) == 0)
        def _():
            dht[...] = jnp.zeros_like(dht)
            dpar_ref[...] = jnp.zeros_like(dpar_ref)
            dnw_ref[...] = jnp.zeros_like(dnw_ref)

        consts = _ssd_consts()
        causal, _, triu, _, reduce, lane_head = consts
        v = _ssd_common(xc_ref, dt_ref, par_ref, consts)
        x, bm, cm, xdt, s_x = v["x"], v["bm"], v["cm"], v["xdt"], v["s_x"]
        h = hs_ref[...]
        hb = h.astype(BF16)
        es_x = jnp.exp(s_x)
        yo = es_x * _dot(cm, hb)
        s_last = s_x[Q - 1:Q, :]
        e_x = jnp.exp(s_last - s_x)
        es_last = jnp.exp(s_last)

        yv, zz, nw_v = y_ref[...], z_ref[...], nw_ref[...]
        sg = _sigmoid(zz)
        gz = zz * sg
        _, n, rstd = _nrm(yv * gz, nw_v)
        dout = dys_ref[...]
        dyg, dnw = _nrm_bwd(dout, n, rstd, nw_v)
        dnw_ref[...] += dnw
        dy = dyg * gz
        dz_ref[...] = (dyg * yv * (sg * (1.0 + zz * (1.0 - sg)))).astype(BF16)

        dyb = dy.astype(BF16)
        xdt_b = xdt.astype(BF16)
        dhp = dht[...]
        dhpb = dhp.astype(BF16)
        lane = lax.broadcasted_iota(jnp.int32, (Q, LANES), 1)
        sub = lax.broadcasted_iota(jnp.int32, (LANES, Q), 0)
        dxdt = jnp.zeros((Q, GROUP_X), F32)
        dg = jnp.zeros((Q, Q), F32)
        ds = jnp.zeros((Q, LANES), F32)
        ds_t = jnp.zeros((LANES, Q), F32)
        for r in range(HEADS_PER_GROUP):
            dec = _decay(v, r, causal)
            mf = v["g"] * dec
            dyr = jnp.where(lane_head == r, dyb, jnp.zeros_like(dyb))
            dm = _dot(dyr, xdt_b, _NT)
            dxdt = dxdt + _dot(mf.astype(BF16), dyr, _TN)
            dg = dg + dm * dec
            dd = dm * mf
            ds = ds + jnp.where(lane == r, jnp.sum(dd, axis=1, keepdims=True), 0.0)
            ds_t = ds_t + jnp.where(sub == r, jnp.sum(dd, axis=0, keepdims=True), 0.0)
        ds = ds - ds_t.T
        dgb = dg.astype(BF16)
        dwb = (es_x * dy).astype(BF16)
        dcm = _dot(dgb, bm) + _dot(dwb, hb, _NT)
        dh_prev = _dot(cm, dwb, _TN)
        zst = _dot(bm, dhpb)
        xe = xdt * e_x
        dxdt = dxdt + e_x * zst
        dee = xe * zst
        dbm = _dot(dgb, cm, _TN) + _dot(xe.astype(BF16), dhpb, _NT)
        v_last = jnp.sum(dee, axis=0, keepdims=True) + es_last * jnp.sum(dhp * h, axis=0, keepdims=True)
        row_x = lax.broadcasted_iota(jnp.int32, (Q, GROUP_X), 0)
        tx = dy * yo - dee + jnp.where(row_x == Q - 1, v_last, 0.0)
        ds = ds + _dot_r01(tx, reduce)
        ddta = _dot_l01(triu, ds)
        ddt = ddta * v["a_neg"] + _dot_r01(dxdt * x, reduce)
        dalog = jnp.sum(ddta * v["dt"], axis=0, keepdims=True) * v["a_neg"]
        draw = jnp.where(lane < HEADS_PER_GROUP, ddt * _sigmoid(v["dtr"]), 0.0)
        dbias = jnp.sum(draw, axis=0, keepdims=True)
        ddsk = _dot_r01(jnp.broadcast_to(jnp.sum(dy * x, axis=0, keepdims=True), (8, GROUP_X)), reduce)[0:1, :]
        dht[...] = es_last * dhp + dh_prev
        dxc_ref[:, :GROUP_X] = dxdt * v["dt_x"] + v["dsk_x"] * dy
        dxc_ref[:, GROUP_X:GROUP_X + D_STATE] = dbm
        dxc_ref[:, GROUP_X + D_STATE:] = dcm
        ddt_ref[...] = draw
        dpar_ref[0:1, :] += dbias
        dpar_ref[1:2, :] += dalog
        dpar_ref[2:3, :] += ddsk

    return pl.pallas_call(
        body,
        grid=(SSM_GROUPS, n_chunks),
        in_specs=[xc_s, gx_s, dt_s, par_s, nw_s, gx_s, hs_s, gx_s],
        out_specs=[xc_s, gx_s, dt_s, par_s, nw_s],
        out_shape=[jax.ShapeDtypeStruct((s_dim, SSM_GROUPS * GROUP_COLS), F32),
                   jax.ShapeDtypeStruct((s_dim, D_SSM), BF16),
                   jax.ShapeDtypeStruct((SSM_GROUPS, s_dim, LANES), F32),
                   jax.ShapeDtypeStruct((SSM_GROUPS, 8, LANES), F32),
                   jax.ShapeDtypeStruct((1, D_SSM), F32)],
        scratch_shapes=[pltpu.VMEM((D_STATE, GROUP_X), F32)],
        compiler_params=_cparams("parallel", "arbitrary"),
        name="ssd_bwd",
    )(xc, z, dtg, par, nw, y, hs, dymix)


ATT_SCALE = ATT_HEAD_DIM ** -0.5
NEG_INF = -jnp.inf


def _head(h):
    return slice(h * ATT_HEAD_DIM, (h + 1) * ATT_HEAD_DIM)


def _band_masks():
    qi = lax.broadcasted_iota(jnp.int32, (ATT_BLOCK, ATT_BLOCK), 0)
    kj = lax.broadcasted_iota(jnp.int32, (ATT_BLOCK, ATT_BLOCK), 1)
    return kj <= qi, kj >= qi


def _attn_fwd(qkv_v, d):
    rows = qkv_v.shape[0]
    nb = rows // ATT_BLOCK
    blk = (ATT_BLOCK, D_ATT)
    prev = lambda i: jnp.maximum(i - 1, 0)

    def body(q_ref, kc_ref, kp_ref, vc_ref, vp_ref, o_ref, lse_ref):
        own, before = _band_masks()
        before = before & (pl.program_id(1) > 0)
        lane = lax.broadcasted_iota(jnp.int32, (ATT_BLOCK, LANES), 1)
        lse_all = jnp.zeros((ATT_BLOCK, LANES), F32)
        for h in range(ATT_HEADS):
            q = q_ref[:, _head(h)]
            sc = jnp.where(own, _dot(q, kc_ref[:, _head(h)], _NT) * ATT_SCALE, NEG_INF)
            sp = jnp.where(before, _dot(q, kp_ref[:, _head(h)], _NT) * ATT_SCALE, NEG_INF)
            m = jnp.maximum(jnp.max(sc, axis=1, keepdims=True), jnp.max(sp, axis=1, keepdims=True))
            pc, pp = jnp.exp(sc - m), jnp.exp(sp - m)
            den = jnp.sum(pc, axis=1, keepdims=True) + jnp.sum(pp, axis=1, keepdims=True)
            o = _dot(pc.astype(BF16), vc_ref[:, _head(h)]) + _dot(pp.astype(BF16), vp_ref[:, _head(h)])
            o_ref[:, _head(h)] = o / den
            lse_all = jnp.where(lane == h, m + jnp.log(den), lse_all)
        lse_ref[...] = lse_all

    return pl.pallas_call(
        body,
        grid=(d, nb),
        in_specs=[pl.BlockSpec(blk, lambda r, i: (i, 3 * r)),
                  pl.BlockSpec(blk, lambda r, i: (i, 3 * r + 1)),
                  pl.BlockSpec(blk, lambda r, i: (prev(i), 3 * r + 1)),
                  pl.BlockSpec(blk, lambda r, i: (i, 3 * r + 2)),
                  pl.BlockSpec(blk, lambda r, i: (prev(i), 3 * r + 2))],
        out_specs=[pl.BlockSpec(blk, lambda r, i: (i, r)), pl.BlockSpec((ATT_BLOCK, LANES), lambda r, i: (i, r))],
        out_shape=[jax.ShapeDtypeStruct((rows, d * D_ATT), F32), jax.ShapeDtypeStruct((rows, d * LANES), F32)],
        compiler_params=_cparams("parallel", "arbitrary"),
        name=f"attn_fwd_d{d}",
    )(qkv_v, qkv_v, qkv_v, qkv_v, qkv_v)


def _attn_combine(os_, lses):
    def fn(o1, o2, o3, l1, l2, l3):
        m = jnp.maximum(jnp.maximum(l1, l2), l3)
        tot = m + jnp.log(jnp.exp(l1 - m) + jnp.exp(l2 - m) + jnp.exp(l3 - m))
        w1, w2, w3 = jnp.exp(l1 - tot), jnp.exp(l2 - tot), jnp.exp(l3 - tot)
        cols = []
        for h in range(ATT_HEADS):
            cols.append(w1[:, h:h + 1] * o1[:, _head(h)] + w2[:, h:h + 1] * o2[:, _head(h)]
                        + w3[:, h:h + 1] * o3[:, _head(h)])
        y = jnp.concatenate(cols, axis=1)
        return y, y, tot
    return _rowcall(fn, list(os_) + list(lses), [], [(D_ATT, BF16), (D_ATT, F32), (LANES, F32)], [],
                    name="attn_combine", tr=128)


def _attn_delta(dymix, y_att):
    def fn(dy, y):
        lane = lax.broadcasted_iota(jnp.int32, (dy.shape[0], LANES), 1)
        delta = jnp.zeros((dy.shape[0], LANES), F32)
        for h in range(ATT_HEADS):
            delta = jnp.where(lane == h, jnp.sum(dy[:, _head(h)] * y[:, _head(h)], axis=1, keepdims=True), delta)
        return dy, delta
    return _rowcall(fn, [dymix, y_att], [], [(D_ATT, BF16), (LANES, F32)], [], name="attn_delta",
                    row_cols=[(D_ATT, 1), None])


def _attn_bwd(qkv_v, dy_v, lse_v, delta_v, d):
    rows = qkv_v.shape[0]
    nb = rows // ATT_BLOCK
    blk = (ATT_BLOCK, D_ATT)
    sblk = (ATT_BLOCK, LANES)
    prev = lambda i: jnp.maximum(i - 1, 0)
    nxt = lambda i: jnp.minimum(i + 1, nb - 1)

    def body(qc_ref, qn_ref, kc_ref, kp_ref, vc_ref, vp_ref, dyc_ref, dyn_ref, lc_ref, ln_ref, dc_ref, dn_ref,
             dq_ref, dk_ref, dv_ref):
        i = pl.program_id(1)
        own, before = _band_masks()
        before_c = before & (i > 0)
        before_n = before & (i < nb - 1)
        lc, ln, dc, dn = lc_ref[...], ln_ref[...], dc_ref[...], dn_ref[...]
        for h in range(ATT_HEADS):
            hs = _head(h)
            q, qn, kc, kp, vc, vp = qc_ref[:, hs], qn_ref[:, hs], kc_ref[:, hs], kp_ref[:, hs], vc_ref[:, hs], vp_ref[:, hs]
            dy, dyn = dyc_ref[:, hs], dyn_ref[:, hs]
            lse, lse_n, dl, dl_n = lc[:, h:h + 1], ln[:, h:h + 1], dc[:, h:h + 1], dn[:, h:h + 1]
            pc = jnp.exp(jnp.where(own, _dot(q, kc, _NT) * ATT_SCALE - lse, NEG_INF))
            pp = jnp.exp(jnp.where(before_c, _dot(q, kp, _NT) * ATT_SCALE - lse, NEG_INF))
            pn = jnp.exp(jnp.where(before_n, _dot(qn, kc, _NT) * ATT_SCALE - lse_n, NEG_INF))
            dsc = (pc * (_dot(dy, vc, _NT) - dl)).astype(BF16)
            dsp = (pp * (_dot(dy, vp, _NT) - dl)).astype(BF16)
            dsn = (pn * (_dot(dyn, vc, _NT) - dl_n)).astype(BF16)
            dq_ref[:, hs] = (_dot(dsc, kc) + _dot(dsp, kp)) * ATT_SCALE
            dk_ref[:, hs] = (_dot(dsc, q, _TN) + _dot(dsn, qn, _TN)) * ATT_SCALE
            dv_ref[:, hs] = _dot(pc.astype(BF16), dy, _TN) + _dot(pn.astype(BF16), dyn, _TN)

    return pl.pallas_call(
        body,
        grid=(d, nb),
        in_specs=[pl.BlockSpec(blk, lambda r, i: (i, 3 * r)), pl.BlockSpec(blk, lambda r, i: (nxt(i), 3 * r)),
                  pl.BlockSpec(blk, lambda r, i: (i, 3 * r + 1)), pl.BlockSpec(blk, lambda r, i: (prev(i), 3 * r + 1)),
                  pl.BlockSpec(blk, lambda r, i: (i, 3 * r + 2)), pl.BlockSpec(blk, lambda r, i: (prev(i), 3 * r + 2)),
                  pl.BlockSpec(blk, lambda r, i: (i, r)), pl.BlockSpec(blk, lambda r, i: (nxt(i), r)),
                  pl.BlockSpec(sblk, lambda r, i: (i, r)), pl.BlockSpec(sblk, lambda r, i: (nxt(i), r)),
                  pl.BlockSpec(sblk, lambda r, i: (i, r)), pl.BlockSpec(sblk, lambda r, i: (nxt(i), r))],
        out_specs=[pl.BlockSpec(blk, lambda r, i: (i, r))] * 3,
        out_shape=[jax.ShapeDtypeStruct((rows, d * D_ATT), F32)] * 3,
        compiler_params=_cparams("parallel", "arbitrary"),
        name=f"attn_bwd_d{d}",
    )(qkv_v, qkv_v, qkv_v, qkv_v, qkv_v, qkv_v, dy_v, dy_v, lse_v, lse_v, delta_v, delta_v)


def _attn_sum(dqs, dks, dvs):
    def fn(*parts):
        return (jnp.concatenate([parts[0] + parts[1] + parts[2], parts[3] + parts[4] + parts[5],
                                 parts[6] + parts[7] + parts[8]], axis=1),)
    return _rowcall(fn, list(dqs) + list(dks) + list(dvs), [], [(3 * D_ATT, BF16)], [], name="attn_sum", tr=128)[0]


def _attention_fwd(qkv):
    s_dim = qkv.shape[0]
    os_, lses = [], []
    for d in DILATIONS:
        o, lse = _attn_fwd(qkv.reshape(s_dim // d, d * 3 * D_ATT), d)
        os_.append(o.reshape(s_dim, D_ATT))
        lses.append(lse.reshape(s_dim, LANES))
    return _attn_combine(os_, lses)


def _attention_bwd(qkv, dymix, y_att, lse):
    s_dim = qkv.shape[0]
    dy, delta = _attn_delta(dymix, y_att)
    dqs, dks, dvs = [], [], []
    for d in DILATIONS:
        dq, dk, dv = _attn_bwd(qkv.reshape(s_dim // d, d * 3 * D_ATT), dy.reshape(s_dim // d, d * D_ATT),
                               lse.reshape(s_dim // d, d * LANES), delta.reshape(s_dim // d, d * LANES), d)
        dqs.append(dq.reshape(s_dim, D_ATT))
        dks.append(dk.reshape(s_dim, D_ATT))
        dvs.append(dv.reshape(s_dim, D_ATT))
    return _attn_sum(dqs, dks, dvs)


def _adamw(w, g, m, v, name):
    def fn(wb, gb, mb, vb):
        m2 = ADAM_B1 * mb + (1.0 - ADAM_B1) * gb
        v2 = ADAM_B2 * vb + (1.0 - ADAM_B2) * (gb * gb)
        m_hat = m2 / (1.0 - ADAM_B1 ** ADAM_STEP)
        v_hat = v2 / (1.0 - ADAM_B2 ** ADAM_STEP)
        delta = -ADAM_LR * (m_hat / (jnp.sqrt(v_hat) + ADAM_EPS) + ADAM_WD * wb)
        return delta, m2, v2
    cols = w.shape[1]
    tr = 128 if w.shape[0] % 128 == 0 else w.shape[0]
    return _rowcall(fn, [w, g, m, v], [], [(cols, F32)] * 3, [], name=name, tr=tr)


ANY = pl.BlockSpec(memory_space=pl.ANY)


def _position():
    x, y, c = lax.axis_index("x"), lax.axis_index("y"), lax.axis_index("c")
    chips = [(1 - x, y), (x, 1 - y), (1 - x, 1 - y)]
    return x, y, c, chips


def _remote(src, dst, send_sem, recv_sem, device):
    return pltpu.make_async_remote_copy(src_ref=src, dst_ref=dst, send_sem=send_sem, recv_sem=recv_sem,
                                        device_id=device, device_id_type=MESH)


def _gather_shards(shards):
    n = len(shards)

    def body(*refs):
        ins, outs = refs[:n], refs[n:2 * n]
        send_sems, recv_sems, local_sems = refs[2 * n:]
        x, y, c, chips = _position()
        sibling = (x, y, 1 - c)
        me = 2 * x + y

        def half(a, chip, cc):
            h = ins[a].shape[0] // 2
            return outs[a].at[2 * chip[0] + chip[1], pl.ds(cc * h, h), :]

        local, sent = [], []
        for a in range(n):
            h = ins[a].shape[0] // 2
            cp = pltpu.make_async_copy(ins[a], outs[a].at[me], local_sems.at[a])
            cp.start()
            local.append(cp)
            for j, chip in enumerate(chips):
                cp = _remote(ins[a].at[pl.ds(c * h, h), :], half(a, (x, y), c), send_sems.at[6 * a + j],
                             recv_sems.at[6 * a + j], (chip[0], chip[1], c))
                cp.start()
                sent.append(cp)
        for a in range(n):
            for j, chip in enumerate(chips):
                landed = half(a, chip, c)
                _remote(landed, landed, send_sems.at[6 * a + j], recv_sems.at[6 * a + j], (x, y, c)).wait_recv()
                cp = _remote(landed, landed, send_sems.at[6 * a + 3 + j], recv_sems.at[6 * a + 3 + j], sibling)
                cp.start()
                sent.append(cp)
        for a in range(n):
            for j, chip in enumerate(chips):
                handed = half(a, chip, 1 - c)
                _remote(handed, handed, send_sems.at[6 * a + 3 + j], recv_sems.at[6 * a + 3 + j], (x, y, c)).wait_recv()
        for cp in sent:
            cp.wait_send()
        for cp in local:
            cp.wait()

    return pl.pallas_call(
        body,
        in_specs=[ANY] * n,
        out_specs=[ANY] * n,
        out_shape=[jax.ShapeDtypeStruct((N_CHIPS,) + s.shape, s.dtype) for s in shards],
        scratch_shapes=[pltpu.SemaphoreType.DMA((6 * n,)), pltpu.SemaphoreType.DMA((6 * n,)),
                        pltpu.SemaphoreType.DMA((n,))],
        name="gather_shards",
    )(*shards)


def _exchange_sibling_halves(grads):
    n = len(grads)

    def body(*refs):
        ins, outs = refs[:n], refs[n:2 * n]
        send_sems, recv_sems = refs[2 * n:]
        x, y, c, _ = _position()
        copies = []
        for a in range(n):
            h = ins[a].shape[1] // 2
            cp = _remote(ins[a].at[:, pl.ds((1 - c) * h, h), :], outs[a], send_sems.at[a], recv_sems.at[a], (x, y, 1 - c))
            cp.start()
            copies.append(cp)
        for cp in copies:
            cp.wait()

    return pl.pallas_call(
        body,
        in_specs=[ANY] * n,
        out_specs=[ANY] * n,
        out_shape=[jax.ShapeDtypeStruct((g.shape[0], g.shape[1] // 2, g.shape[2]), g.dtype) for g in grads],
        scratch_shapes=[pltpu.SemaphoreType.DMA((n,)), pltpu.SemaphoreType.DMA((n,))],
        name="exchange_sibling_halves",
    )(*grads)


def _exchange_quarters(parts):
    n = len(parts)

    def body(*refs):
        ins, outs = refs[:n], refs[n:2 * n]
        send_sems, recv_sems = refs[2 * n:]
        x, y, c, chips = _position()
        copies = []
        for a in range(n):
            for j, chip in enumerate(chips):
                cp = _remote(ins[a].at[2 * chip[0] + chip[1]], outs[a].at[j], send_sems.at[3 * a + j],
                             recv_sems.at[3 * a + j], (chip[0], chip[1], c))
                cp.start()
                copies.append(cp)
        for cp in copies:
            cp.wait()

    return pl.pallas_call(
        body,
        in_specs=[ANY] * n,
        out_specs=[ANY] * n,
        out_shape=[jax.ShapeDtypeStruct((3,) + p.shape[1:], p.dtype) for p in parts],
        scratch_shapes=[pltpu.SemaphoreType.DMA((3 * n,)), pltpu.SemaphoreType.DMA((3 * n,))],
        name="exchange_quarters",
    )(*parts)


def _share_reduced_halves(halves):
    n = len(halves)

    def body(*refs):
        ins, outs = refs[:n], refs[n:2 * n]
        send_sems, recv_sems, local_sems = refs[2 * n:]
        x, y, c, _ = _position()
        copies = []
        for a in range(n):
            h = ins[a].shape[0]
            mine = outs[a].at[pl.ds(c * h, h), :]
            lc = pltpu.make_async_copy(ins[a], mine, local_sems.at[a])
            lc.start()
            cp = _remote(ins[a], mine, send_sems.at[a], recv_sems.at[a], (x, y, 1 - c))
            cp.start()
            copies += [lc, cp]
        for cp in copies:
            cp.wait()

    return pl.pallas_call(
        body,
        in_specs=[ANY] * n,
        out_specs=[ANY] * n,
        out_shape=[jax.ShapeDtypeStruct((2 * p.shape[0], p.shape[1]), p.dtype) for p in halves],
        scratch_shapes=[pltpu.SemaphoreType.DMA((n,)), pltpu.SemaphoreType.DMA((n,)), pltpu.SemaphoreType.DMA((n,))],
        name="share_reduced_halves",
    )(*halves)


def _add_sibling(grad, got, c_arr, name):
    nq, rows, cols = grad.shape
    h = rows // 2
    tr = 128
    nb = h // tr

    def body(c_ref, a_ref, b_ref, o_ref, ob_ref):
        total = a_ref[...] + b_ref[...]
        o_ref[...] = total
        ob_ref[...] = total.astype(BF16)

    out_spec = pl.BlockSpec((None, tr, cols), lambda q, i, c: (q, i, 0))
    return pl.pallas_call(
        body,
        grid_spec=pltpu.PrefetchScalarGridSpec(
            num_scalar_prefetch=1, grid=(nq, nb),
            in_specs=[pl.BlockSpec((None, tr, cols), lambda q, i, c: (q, c[0] * nb + i, 0)),
                      pl.BlockSpec((None, tr, cols), lambda q, i, c: (q, i, 0))],
            out_specs=[out_spec, out_spec]),
        out_shape=[jax.ShapeDtypeStruct((nq, h, cols), F32), jax.ShapeDtypeStruct((nq, h, cols), BF16)],
        compiler_params=_cparams("parallel", "parallel"),
        name=name,
    )(c_arr, grad, got)


def _add_chips(part, got, chip_arr, name):
    _, h, cols = part.shape
    tr = 128

    def body(q_ref, p_ref, g0_ref, g1_ref, g2_ref, o_ref):
        o_ref[...] = ((p_ref[...] + g0_ref[...].astype(F32)) + g1_ref[...].astype(F32)) + g2_ref[...].astype(F32)

    got_spec = lambda j: pl.BlockSpec((None, tr, cols), lambda i, q: (j, i, 0))
    return pl.pallas_call(
        body,
        grid_spec=pltpu.PrefetchScalarGridSpec(
            num_scalar_prefetch=1, grid=(h // tr,),
            in_specs=[pl.BlockSpec((None, tr, cols), lambda i, q: (q[0], i, 0)), got_spec(0), got_spec(1), got_spec(2)],
            out_specs=pl.BlockSpec((tr, cols), lambda i, q: (i, 0))),
        out_shape=jax.ShapeDtypeStruct((h, cols), F32),
        compiler_params=_cparams("parallel"),
        name=name,
    )(chip_arr, part, got, got, got)


def _reduce_scatter(grads, names):
    c_arr = lax.axis_index("c").astype(jnp.int32).reshape(1)
    chip_arr = (2 * lax.axis_index("x") + lax.axis_index("y")).astype(jnp.int32).reshape(1)
    from_sibling = _exchange_sibling_halves(grads)
    parts = [_add_sibling(g, s, c_arr, f"add_sibling_{nm}") for g, s, nm in zip(grads, from_sibling, names)]
    from_chips = _exchange_quarters([pb for _, pb in parts])
    halves = [_add_chips(p, f, chip_arr, f"add_chips_{nm}") for (p, _), f, nm in zip(parts, from_chips, names)]
    return _share_reduced_halves(halves)


def _all_sum_small(v):
    n_dev = 8

    def body(v_ref, o_ref, gath, send_sems, recv_sems):
        x, y, c, _ = _position()
        me = 4 * x + 2 * y + c
        gath[me] = v_ref[...]
        copies = []
        for k in range(1, n_dev):
            peer = tuple(1 - p if (k >> s) & 1 else p for p, s in ((x, 2), (y, 1), (c, 0)))
            cp = _remote(v_ref, gath.at[me], send_sems.at[k - 1], recv_sems.at[k - 1], peer)
            cp.start()
            copies.append(cp)
        for cp in copies:
            cp.wait()
        acc = gath[0]
        for i in range(1, n_dev):
            acc = acc + gath[i]
        o_ref[...] = acc

    vm = pl.BlockSpec(memory_space=pltpu.VMEM)
    return pl.pallas_call(
        body,
        in_specs=[vm],
        out_specs=vm,
        out_shape=jax.ShapeDtypeStruct(v.shape, F32),
        scratch_shapes=[pltpu.VMEM((n_dev,) + v.shape, F32), pltpu.SemaphoreType.DMA((n_dev - 1,)),
                        pltpu.SemaphoreType.DMA((n_dev - 1,))],
        name="all_sum_small",
    )(v)


def _pack_rows(vectors):
    rows = []
    for v in vectors:
        flat = v.reshape(-1).astype(F32)
        rows.append(jnp.pad(flat, (0, (-flat.shape[0]) % LANES)).reshape(-1, LANES))
    out = jnp.concatenate(rows, axis=0)
    return jnp.pad(out, ((0, (-out.shape[0]) % 8), (0, 0)))


def _unpack_rows(packed, shapes):
    outs, r = [], 0
    for shp in shapes:
        size = math.prod(shp)
        nr = -(-size // LANES)
        outs.append(packed[r:r + nr].reshape(-1)[:size].reshape(shp))
        r += nr
    return outs


def _relu_sq(acc):
    r = jnp.maximum(acc, 0.0)
    return r, r * r


def _relu_sq_bwd(acc, r):
    return (acc * (2.0 * r.astype(F32)),)


def kernel(x, norm_mix_pre, w_in, conv_w, conv_b, dt_bias, a_log, d_skip, ssm_norm_w, w_out, norm_mix_post, norm_mlp_pre, w_up, w_down, norm_mlp_post, loss_target, m_norm_mix_pre, m_w_in, m_conv_w, m_conv_b, m_dt_bias, m_a_log, m_d_skip, m_ssm_norm_w, m_w_out, m_norm_mix_post, m_norm_mlp_pre, m_w_up, m_w_down, m_norm_mlp_post, v_norm_mix_pre, v_w_in, v_conv_w, v_conv_b, v_dt_bias, v_a_log, v_d_skip, v_ssm_norm_w, v_w_out, v_norm_mix_post, v_norm_mlp_pre, v_w_up, v_w_down, v_norm_mlp_post):
    s_dim = x.shape[1]
    xs, target = x[0], loss_target[0]
    chip = 2 * lax.axis_index("x") + lax.axis_index("y")

    g_in, g_out, g_up, g_down = _gather_shards(
        [w_in[0].astype(BF16), w_out[0].astype(BF16), w_up[0].astype(BF16), w_down[0].astype(BF16)])
    w_in_full = g_in.transpose(1, 0, 2).reshape(D_MODEL, D_IN_PROJ)
    w_z = w_in_full[:, :D_SSM]
    w_xbc = _perm_cols(w_in_full[:, D_SSM:D_SSM + D_XBC])
    w_dt = jnp.pad(w_in_full[:, D_SSM + D_XBC:D_SSM + D_XBC + SSM_HEADS], ((0, 0), (0, LANES - SSM_HEADS)))
    w_qkv = w_in_full[:, D_SSM + D_XBC + SSM_HEADS:]
    w_out_full = g_out.reshape(D_MIX, D_MODEL)
    w_up_full = g_up.transpose(1, 0, 2).reshape(D_MODEL, D_FF)
    w_down_full = g_down.reshape(D_FF, D_MODEL)

    conv_cols = D_XBC // N_CHIPS
    conv_placed = lax.dynamic_update_slice(jnp.zeros((8, D_XBC), F32), 0.5 * conv_w[0], (0, chip * conv_cols))
    conv_full = _all_sum_small(conv_placed.reshape(-1, LANES)).reshape(8, D_XBC)
    w8 = _perm_cols(conv_full.at[CONV_WIDTH].set(conv_b[0]))

    u = _pre_norm(xs, norm_mix_pre)
    z = _matmul([(u, w_z, TK)], "nn", [F32], name="proj_z")
    xbc = _matmul([(u, w_xbc, TK)], "nn", [F32], name="proj_xbc")
    dt_raw = _matmul([(u, w_dt, TK)], "nn", [F32], name="proj_dt")
    qkv = _matmul([(u, w_qkv, TK)], "nn", [BF16], name="proj_qkv")
    xc = _conv_fwd(xbc, w8)
    dtg = _dt_to_groups(dt_raw)
    par = _pack_ssd_params(dt_bias[0], a_log[0], d_skip[0])
    y, y_ssm, states = _ssd_fwd(xc, z, dtg, par, ssm_norm_w)
    y_att, y_att_f32, lse = _attention_fwd(qkv)
    y_mix = jnp.concatenate([y_ssm, y_att], axis=1)
    mix = _matmul([(y_mix, w_out_full, TK)], "nn", [F32], name="out_proj")
    h1, u2 = _post_pre_norm(xs, mix, norm_mix_post, norm_mlp_pre)
    hid, act = _matmul([(u2, w_up_full, TK)], "nn", [BF16, BF16], name="mlp_up", epilogue=_relu_sq)
    ff = _matmul([(act, w_down_full, TK)], "nn", [F32], name="mlp_down")
    dh2, dff, d_g4, loss_part = _tail(ff, h1, target, norm_mlp_post)

    dhid = _matmul([(dff, w_down_full, TK)], "nt", [BF16], name="mlp_down_dx", epilogue=_relu_sq_bwd, extras=[hid])
    dw_down = _matmul([(act, dff, TK)], "tn", [F32], name="mlp_down_dw")
    dw_up = _matmul([(u2, dhid, TK)], "tn", [F32], name="mlp_up_dw")
    du2 = _matmul([(dhid, w_up_full, TK)], "nt", [F32], name="mlp_up_dx")
    dh1, dmix, d_g3, d_g2 = _mid_bwd(du2, h1, dh2, mix, norm_mix_post, norm_mlp_pre)
    dymix = _matmul([(dmix, w_out_full, TK)], "nt", [F32], name="out_proj_dx")
    dw_out = _matmul([(y_mix, dmix, TK)], "tn", [F32], name="out_proj_dw")
    dqkv = _attention_bwd(qkv, dymix, y_att_f32, lse)
    dxc, dz, ddtg, dpar, d_nw = _ssd_bwd(xc, z, dtg, par, ssm_norm_w, y, states, dymix)
    dxbc, dw8 = _conv_bwd(xbc, w8, dxc)
    ddt = jnp.pad(_dt_from_groups(ddtg), ((0, 0), (0, LANES - SSM_HEADS))).astype(BF16)
    du = _matmul([(dz, w_z, TK_MULTI), (dxbc, w_xbc, TK_MULTI), (dqkv, w_qkv, TK_MULTI), (ddt, w_dt, LANES)], "nt", [F32],
                 name="proj_dx")
    dw_z = _matmul([(u, dz, TK)], "tn", [F32], name="proj_z_dw")
    dw_xbc = _matmul([(u, dxbc, TK)], "tn", [F32], name="proj_xbc_dw")
    dw_dt = _matmul([(u, ddt, TK)], "tn", [F32], name="proj_dt_dw")
    dw_qkv = _matmul([(u, dqkv, TK)], "tn", [F32], name="proj_qkv_dw")
    grad_x, d_g1 = _first_bwd(du, xs, dh1, norm_mix_pre)

    dw_in = jnp.concatenate([dw_z, _unperm_cols(dw_xbc), dw_dt[:, :SSM_HEADS], dw_qkv], axis=1)
    big = _reduce_scatter(
        [dw_in.reshape(D_MODEL, N_CHIPS, W_IN_SHARD).transpose(1, 0, 2),
         dw_out.reshape(N_CHIPS, D_MIX // N_CHIPS, D_MODEL),
         dw_up.reshape(D_MODEL, N_CHIPS, D_FF // N_CHIPS).transpose(1, 0, 2),
         dw_down.reshape(N_CHIPS, D_FF // N_CHIPS, D_MODEL)],
        ["w_in", "w_out", "w_up", "w_down"])
    dconv = _unperm_cols(dw8)
    d_bias, d_alog, d_dskip = _unpack_ssd_params(dpar)
    small_shapes = [(1, D_MODEL), (CONV_WIDTH, D_XBC), (1, D_XBC), (1, SSM_HEADS), (1, SSM_HEADS), (1, SSM_HEADS),
                    (1, D_SSM), (1, D_MODEL), (1, D_MODEL), (1, D_MODEL), (1, LANES)]
    summed = _unpack_rows(
        _all_sum_small(_pack_rows([d_g1, dconv[:CONV_WIDTH], dconv[CONV_WIDTH:CONV_WIDTH + 1], d_bias, d_alog,
                                   d_dskip, d_nw, d_g2, d_g3, d_g4, loss_part])), small_shapes)
    (g_g1, g_conv_full, g_conv_b, g_bias, g_alog, g_dskip, g_nw, g_g2, g_g3, g_g4, loss_row) = summed
    loss = loss_row[0, 0]
    g_conv_w = lax.dynamic_slice(g_conv_full, (0, chip * conv_cols), (CONV_WIDTH, conv_cols))[None]

    grads = {"norm_mix_pre": g_g1, "w_in": big[0][None], "conv_w": g_conv_w, "conv_b": g_conv_b, "dt_bias": g_bias,
             "a_log": g_alog, "d_skip": g_dskip, "ssm_norm_w": g_nw, "w_out": big[1][None], "norm_mix_post": g_g2,
             "norm_mlp_pre": g_g3, "w_up": big[2][None], "w_down": big[3][None], "norm_mlp_post": g_g4}
    weights = {"norm_mix_pre": (norm_mix_pre, m_norm_mix_pre, v_norm_mix_pre), "w_in": (w_in, m_w_in, v_w_in),
               "conv_w": (conv_w, m_conv_w, v_conv_w), "conv_b": (conv_b, m_conv_b, v_conv_b),
               "dt_bias": (dt_bias, m_dt_bias, v_dt_bias), "a_log": (a_log, m_a_log, v_a_log),
               "d_skip": (d_skip, m_d_skip, v_d_skip), "ssm_norm_w": (ssm_norm_w, m_ssm_norm_w, v_ssm_norm_w),
               "w_out": (w_out, m_w_out, v_w_out), "norm_mix_post": (norm_mix_post, m_norm_mix_post, v_norm_mix_post),
               "norm_mlp_pre": (norm_mlp_pre, m_norm_mlp_pre, v_norm_mlp_pre), "w_up": (w_up, m_w_up, v_w_up),
               "w_down": (w_down, m_w_down, v_w_down),
               "norm_mlp_post": (norm_mlp_post, m_norm_mlp_post, v_norm_mlp_post)}
    order = list(weights)
    big_names = ("w_in", "w_out", "w_up", "w_down")
    small_names = [n for n in order if n not in big_names]
    delta, new_m, new_v = {}, {}, {}
    for n in big_names:
        w, m, v = weights[n]
        d_, m_, v_ = _adamw(w[0], grads[n][0], m[0], v[0], f"adamw_{n}")
        delta[n], new_m[n], new_v[n] = d_[None], m_[None], v_[None]
    small_w_shapes = [weights[n][0].shape for n in small_names]
    packed = [_pack_rows([weights[n][k] for n in small_names]) for k in range(3)]
    packed_g = _pack_rows([grads[n].reshape(weights[n][0].shape) for n in small_names])
    sd, sm, sv = _adamw(packed[0], packed_g, packed[1], packed[2], "adamw_small")
    for k, n in enumerate(small_names):
        grads[n] = grads[n].reshape(weights[n][0].shape)
    for res, pk in ((delta, sd), (new_m, sm), (new_v, sv)):
        for n, val in zip(small_names, _unpack_rows(pk, small_w_shapes)):
            res[n] = val

    return (loss, grad_x[None], *[grads[n] for n in order], *[delta[n] for n in order],
            *[new_m[n] for n in order], *[new_v[n] for n in order])
```

```python
import functools
import math

import numpy as np
import jax
import jax.numpy as jnp
from jax import lax
from jax.experimental import pallas as pl
from jax.experimental.pallas import tpu as pltpu

F32 = jnp.float32
BF16 = jnp.bfloat16

D_MODEL = 2048
SSM_HEAD_DIM = 64
SSM_GROUPS = 8
HEADS_PER_GROUP = 4
SSM_HEADS = SSM_GROUPS * HEADS_PER_GROUP
D_SSM = SSM_HEADS * SSM_HEAD_DIM
D_STATE = 128
CONV_WIDTH = 4
SSD_CHUNK = 128
D_XBC = D_SSM + 2 * SSM_GROUPS * D_STATE
GROUP_X = HEADS_PER_GROUP * SSM_HEAD_DIM
GROUP_COLS = GROUP_X + 2 * D_STATE
ATT_HEAD_DIM = 128
ATT_HEADS = 16
D_ATT = ATT_HEADS * ATT_HEAD_DIM
DILATIONS = (1, 4, 16)
ATT_BLOCK = 128
D_MIX = D_SSM + D_ATT
D_IN_PROJ = D_SSM + D_XBC + SSM_HEADS + 3 * D_ATT
D_FF = 4 * D_MODEL
EPS = 1e-6
N_CHIPS = 4
W_IN_SHARD = D_IN_PROJ // N_CHIPS

ADAM_LR = 0.001
ADAM_B1 = 0.9
ADAM_B2 = 0.999
ADAM_EPS = 1e-08
ADAM_WD = 0.01
ADAM_STEP = 10

LANES = 128
VMEM_LIMIT = 48 * 1024 * 1024
MESH = pl.DeviceIdType.MESH

_NN = (((1,), (0,)), ((), ()))
_NT = (((1,), (1,)), ((), ()))
_TN = (((0,), (0,)), ((), ()))


def _dot(a, b, dims=_NN):
    return lax.dot_general(a, b, dims, preferred_element_type=F32)


def _cparams(*sem):
    return pltpu.CompilerParams(dimension_semantics=sem, vmem_limit_bytes=VMEM_LIMIT)


TK = 2048
TK_MULTI = 1024


def _matmul(pairs, mode, out_dtypes, *, name, tm=1024, tn=1024, epilogue=None, extras=()):
    a0, b0, _ = pairs[0]
    m_dim = a0.shape[1] if mode == "tn" else a0.shape[0]
    n_dim = b0.shape[0] if mode == "nt" else b0.shape[1]
    tm, tn = min(tm, m_dim), min(tn, n_dim)
    nks, offs = [], []
    for a, _, tk in pairs:
        k_dim = a.shape[0] if mode == "tn" else a.shape[1]
        assert k_dim % tk == 0, (name, k_dim, tk)
        offs.append(sum(nks))
        nks.append(k_dim // tk)
    nk_total = sum(nks)
    assert m_dim % tm == 0 and n_dim % tn == 0, (name, m_dim, n_dim)
    dims = {"nn": _NN, "nt": _NT, "tn": _TN}[mode]
    n_pairs, n_extra, n_out = len(pairs), len(extras), len(out_dtypes)

    in_specs, operands = [], []
    for (a, b, tk), off, nk in zip(pairs, offs, nks):
        def kidx(k, off=off, nk=nk):
            return k if n_pairs == 1 else jnp.clip(k - off, 0, nk - 1)
        if mode == "tn":
            in_specs.append(pl.BlockSpec((tk, tm), lambda m, n, k, f=kidx: (f(k), m)))
        else:
            in_specs.append(pl.BlockSpec((tm, tk), lambda m, n, k, f=kidx: (m, f(k))))
        if mode == "nt":
            in_specs.append(pl.BlockSpec((tn, tk), lambda m, n, k, f=kidx: (n, f(k))))
        else:
            in_specs.append(pl.BlockSpec((tk, tn), lambda m, n, k, f=kidx: (f(k), n)))
        operands += [a, b]
    for e in extras:
        in_specs.append(pl.BlockSpec((tm, tn), lambda m, n, k: (m, n)))
        operands.append(e)

    def body(*refs):
        ab = refs[:2 * n_pairs]
        e_refs = refs[2 * n_pairs:2 * n_pairs + n_extra]
        o_refs = refs[2 * n_pairs + n_extra:2 * n_pairs + n_extra + n_out]

        def finish(total):
            vals = (total,) if epilogue is None else epilogue(total, *[e[...] for e in e_refs])
            for o_ref, v in zip(o_refs, vals):
                o_ref[...] = v.astype(o_ref.dtype)

        if nk_total == 1:
            finish(_dot(ab[0][...], ab[1][...], dims))
            return
        acc = refs[-1]
        k = pl.program_id(2)

        @pl.when(k == 0)
        def _():
            acc[...] = jnp.zeros_like(acc)

        for i in range(n_pairs):
            def accumulate(i=i):
                acc[...] += _dot(ab[2 * i][...], ab[2 * i + 1][...], dims)
            if n_pairs == 1:
                accumulate()
            else:
                pl.when((k >= offs[i]) & (k < offs[i] + nks[i]))(accumulate)

        @pl.when(k == nk_total - 1)
        def _():
            finish(acc[...])

    outs = pl.pallas_call(
        body,
        grid=(m_dim // tm, n_dim // tn, nk_total),
        in_specs=in_specs,
        out_specs=[pl.BlockSpec((tm, tn), lambda m, n, k: (m, n)) for _ in out_dtypes],
        out_shape=[jax.ShapeDtypeStruct((m_dim, n_dim), dt) for dt in out_dtypes],
        scratch_shapes=[pltpu.VMEM((tm, tn), F32)] if nk_total > 1 else [],
        compiler_params=_cparams("parallel", "parallel", "arbitrary"),
        name=name,
    )(*operands)
    return outs[0] if n_out == 1 else outs


def _rowcall(fn, rows, vecs, row_outs, acc_widths, *, name, tr=256, row_cols=None):
    s_dim = rows[0].shape[0]
    assert s_dim % tr == 0
    row_cols = row_cols or [None] * len(rows)
    n_r, n_v, n_ro, n_acc = len(rows), len(vecs), len(row_outs), len(acc_widths)
    in_specs = []
    for r, rc in zip(rows, row_cols):
        if rc is None:
            in_specs.append(pl.BlockSpec((tr, r.shape[1]), lambda i: (i, 0)))
        else:
            in_specs.append(pl.BlockSpec((tr, rc[0]), lambda i, c=rc[1]: (i, c)))
    for v in vecs:
        in_specs.append(pl.BlockSpec(v.shape, lambda i, nd=v.ndim: (0,) * nd))

    def body(*refs):
        ins = [r[...] for r in refs[:n_r + n_v]]
        ro = refs[n_r + n_v:n_r + n_v + n_ro]
        ao = refs[n_r + n_v + n_ro:]
        outs = fn(*ins)
        for ref, v in zip(ro, outs[:n_ro]):
            ref[...] = v.astype(ref.dtype)
        if n_acc:
            @pl.when(pl.program_id(0) == 0)
            def _():
                for ref in ao:
                    ref[...] = jnp.zeros_like(ref)
            for ref, v in zip(ao, outs[n_ro:]):
                ref[...] += v

    outs = pl.pallas_call(
        body,
        grid=(s_dim // tr,),
        in_specs=in_specs,
        out_specs=[pl.BlockSpec((tr, w), lambda i: (i, 0)) for w, _ in row_outs]
        + [pl.BlockSpec((1, w), lambda i: (0, 0)) for w in acc_widths],
        out_shape=[jax.ShapeDtypeStruct((s_dim, w), dt) for w, dt in row_outs]
        + [jax.ShapeDtypeStruct((1, w), F32) for w in acc_widths],
        compiler_params=_cparams("arbitrary"),
        name=name,
    )(*rows, *vecs)
    return outs


def _nrm(x, g):
    r = lax.rsqrt(jnp.mean(x * x, axis=-1, keepdims=True) + EPS)
    n = x * r
    return n * g, n, r


def _nrm_bwd(dy, n, r, g):
    dn = dy * g
    dx = r * (dn - n * jnp.mean(dn * n, axis=-1, keepdims=True))
    return dx, jnp.sum(dy * n, axis=0, keepdims=True)


def _sigmoid(x):
    return 1.0 / (1.0 + jnp.exp(-x))


def _softplus(x):
    return jnp.maximum(x, 0.0) + jnp.log(1.0 + jnp.exp(-jnp.abs(x)))


def _pre_norm(x, g1):
    def fn(xb, g):
        return (_nrm(xb, g)[0],)
    return _rowcall(fn, [x], [g1], [(D_MODEL, BF16)], [], name="pre_norm")[0]


def _post_pre_norm(x, mix, g2, g3):
    def fn(xb, mb, g2b, g3b):
        h1 = xb + _nrm(mb, g2b)[0]
        return h1, _nrm(h1, g3b)[0]
    return _rowcall(fn, [x, mix], [g2, g3], [(D_MODEL, F32), (D_MODEL, BF16)], [], name="post_pre_norm")


def _tail(ff, h1, target, g4):
    def fn(ffb, h1b, tb, g):
        y, n, r = _nrm(ffb, g)
        e = h1b + y - tb
        loss = 0.5 * jnp.sum(jnp.sum(e * e, axis=-1, keepdims=True) * (1.0 / D_MODEL), axis=0, keepdims=True)
        dh2 = e * (1.0 / D_MODEL)
        dff, dg = _nrm_bwd(dh2, n, r, g)
        return dh2, dff, dg, jnp.broadcast_to(loss, (1, LANES))
    return _rowcall(fn, [ff, h1, target], [g4], [(D_MODEL, F32), (D_MODEL, BF16)], [D_MODEL, LANES], name="tail")


def _mid_bwd(du2, h1, dh2, mix, g2, g3):
    def fn(du2b, h1b, dh2b, mb, g2b, g3b):
        _, n3, r3 = _nrm(h1b, g3b)
        d3, dg3 = _nrm_bwd(du2b, n3, r3, g3b)
        dh1 = dh2b + d3
        _, n2, r2 = _nrm(mb, g2b)
        dmix, dg2 = _nrm_bwd(dh1, n2, r2, g2b)
        return dh1, dmix, dg3, dg2
    return _rowcall(fn, [du2, h1, dh2, mix], [g2, g3], [(D_MODEL, F32), (D_MODEL, BF16)], [D_MODEL, D_MODEL],
                    name="mid_bwd")


def _first_bwd(du, x, dh1, g1):
    def fn(dub, xb, dh1b, g):
        _, n, r = _nrm(xb, g)
        dx, dg = _nrm_bwd(dub, n, r, g)
        return dh1b + dx, dg
    return _rowcall(fn, [du, x, dh1], [g1], [(D_MODEL, F32)], [D_MODEL], name="first_bwd")


CONV_TILE = 256
CONV_ROWS = 256
PAD = 8


def _conv_taps(w):
    return [w[k:k + 1, :] for k in range(CONV_WIDTH)], w[CONV_WIDTH:CONV_WIDTH + 1, :]


def _conv_fwd(xbc, w8):
    s_dim, c_dim = xbc.shape
    n_steps = s_dim // CONV_ROWS

    def body(x_ref, w_ref, o_ref, xp):
        xp[0:PAD, :] = jnp.zeros((PAD, CONV_TILE), F32)
        xp[PAD:PAD + s_dim, :] = x_ref[...]
        taps, bias = _conv_taps(w_ref[...])

        def step(c, carry):
            base = pl.multiple_of(c * CONV_ROWS, CONV_ROWS)
            win = xp[pl.ds(base, CONV_ROWS + PAD), :]
            pre = bias + taps[3] * win[PAD:, :]
            for j in range(1, CONV_WIDTH):
                pre = pre + taps[3 - j] * pltpu.roll(win, j, axis=0)[PAD:, :]
            o_ref[pl.ds(base, CONV_ROWS), :] = pre * _sigmoid(pre)
            return carry

        lax.fori_loop(0, n_steps, step, 0)

    return pl.pallas_call(
        body,
        grid=(c_dim // CONV_TILE,),
        in_specs=[pl.BlockSpec((s_dim, CONV_TILE), lambda j: (0, j)), pl.BlockSpec((8, CONV_TILE), lambda j: (0, j))],
        out_specs=pl.BlockSpec((s_dim, CONV_TILE), lambda j: (0, j)),
        out_shape=jax.ShapeDtypeStruct((s_dim, c_dim), F32),
        scratch_shapes=[pltpu.VMEM((s_dim + 2 * PAD, CONV_TILE), F32)],
        compiler_params=_cparams("parallel"),
        name="conv_fwd",
    )(xbc, w8)


def _conv_bwd(xbc, w8, dxc):
    s_dim, c_dim = xbc.shape
    n_steps = s_dim // CONV_ROWS

    def body(x_ref, w_ref, d_ref, dx_ref, dw_ref, xp, dp):
        xp[0:PAD, :] = jnp.zeros((PAD, CONV_TILE), F32)
        xp[PAD:PAD + s_dim, :] = x_ref[...]
        dp[PAD + s_dim:, :] = jnp.zeros((PAD, CONV_TILE), F32)
        taps, bias = _conv_taps(w_ref[...])

        def step1(c, sums):
            base = pl.multiple_of(c * CONV_ROWS, CONV_ROWS)
            win = xp[pl.ds(base, CONV_ROWS + PAD), :]
            shifted = [win[PAD:, :]] + [pltpu.roll(win, j, axis=0)[PAD:, :] for j in range(1, CONV_WIDTH)]
            pre = bias
            for j in range(CONV_WIDTH):
                pre = pre + taps[3 - j] * shifted[j]
            sg = _sigmoid(pre)
            dpre = d_ref[pl.ds(base, CONV_ROWS), :] * (sg * (1.0 + pre * (1.0 - sg)))
            dp[pl.ds(base + PAD, CONV_ROWS), :] = dpre
            new = [sums[k] + jnp.sum(dpre * shifted[3 - k], axis=0, keepdims=True) for k in range(CONV_WIDTH)]
            new.append(sums[CONV_WIDTH] + jnp.sum(dpre, axis=0, keepdims=True))
            return tuple(new)

        zero = jnp.zeros((1, CONV_TILE), F32)
        sums = lax.fori_loop(0, n_steps, step1, (zero,) * (CONV_WIDTH + 1))
        dw_ref[...] = jnp.zeros((8, CONV_TILE), F32)
        for k in range(CONV_WIDTH + 1):
            dw_ref[k:k + 1, :] = sums[k]

        def step2(c, carry):
            base = pl.multiple_of(c * CONV_ROWS, CONV_ROWS)
            win = dp[pl.ds(base + PAD, CONV_ROWS + PAD), :]
            dx = taps[3] * win[:CONV_ROWS, :]
            for j in range(1, CONV_WIDTH):
                dx = dx + taps[3 - j] * pltpu.roll(win, CONV_ROWS + PAD - j, axis=0)[:CONV_ROWS, :]
            dx_ref[pl.ds(base, CONV_ROWS), :] = dx.astype(BF16)
            return carry

        lax.fori_loop(0, n_steps, step2, 0)

    col = lambda j: (0, j)
    return pl.pallas_call(
        body,
        grid=(c_dim // CONV_TILE,),
        in_specs=[pl.BlockSpec((s_dim, CONV_TILE), col), pl.BlockSpec((8, CONV_TILE), col),
                  pl.BlockSpec((s_dim, CONV_TILE), col)],
        out_specs=[pl.BlockSpec((s_dim, CONV_TILE), col), pl.BlockSpec((8, CONV_TILE), col)],
        out_shape=[jax.ShapeDtypeStruct((s_dim, c_dim), BF16), jax.ShapeDtypeStruct((8, c_dim), F32)],
        scratch_shapes=[pltpu.VMEM((s_dim + 2 * PAD, CONV_TILE), F32), pltpu.VMEM((s_dim + 2 * PAD, CONV_TILE), F32)],
        compiler_params=_cparams("parallel"),
        name="conv_bwd",
    )(xbc, w8, dxc)


def _perm_cols(a):
    parts = []
    for g in range(SSM_GROUPS):
        parts += [a[..., g * GROUP_X:(g + 1) * GROUP_X],
                  a[..., D_SSM + g * D_STATE:D_SSM + (g + 1) * D_STATE],
                  a[..., D_SSM + SSM_GROUPS * D_STATE + g * D_STATE:D_SSM + SSM_GROUPS * D_STATE + (g + 1) * D_STATE]]
    return jnp.concatenate(parts, axis=-1)


def _unperm_cols(a):
    xs = [a[..., g * GROUP_COLS:g * GROUP_COLS + GROUP_X] for g in range(SSM_GROUPS)]
    bs = [a[..., g * GROUP_COLS + GROUP_X:g * GROUP_COLS + GROUP_X + D_STATE] for g in range(SSM_GROUPS)]
    cs = [a[..., g * GROUP_COLS + GROUP_X + D_STATE:(g + 1) * GROUP_COLS] for g in range(SSM_GROUPS)]
    return jnp.concatenate(xs + bs + cs, axis=-1)


def _dt_to_groups(dt):
    s_dim = dt.shape[0]
    t = dt[:, :SSM_HEADS].reshape(s_dim, SSM_GROUPS, HEADS_PER_GROUP).transpose(1, 0, 2)
    return jnp.pad(t, ((0, 0), (0, 0), (0, LANES - HEADS_PER_GROUP)))


def _dt_from_groups(dtg):
    s_dim = dtg.shape[1]
    return dtg[:, :, :HEADS_PER_GROUP].transpose(1, 0, 2).reshape(s_dim, SSM_HEADS)


def _pack_ssd_params(dt_bias, a_log, d_skip):
    rows = jnp.stack([p.reshape(SSM_GROUPS, HEADS_PER_GROUP) for p in (dt_bias, a_log, d_skip)], axis=1)
    return jnp.pad(rows, ((0, 0), (0, 8 - 3), (0, LANES - HEADS_PER_GROUP)))


def _unpack_ssd_params(par):
    return tuple(par[:, k, :HEADS_PER_GROUP].reshape(SSM_HEADS) for k in range(3))


Q = SSD_CHUNK


def _split3(v):
    hi = v.astype(BF16)
    r1 = v - hi.astype(F32)
    mid = r1.astype(BF16)
    lo = (r1 - mid.astype(F32)).astype(BF16)
    return hi, mid, lo


def _dot_l01(t01, v):
    return sum(_dot(t01, p) for p in _split3(v))


def _dot_r01(v, e01):
    return sum(_dot(p, e01) for p in _split3(v))


def _ssd_consts():
    row = lax.broadcasted_iota(jnp.int32, (Q, Q), 0)
    col = lax.broadcasted_iota(jnp.int32, (Q, Q), 1)
    causal = row >= col
    tril = causal.astype(BF16)
    triu = (col >= row).astype(BF16)
    er = lax.broadcasted_iota(jnp.int32, (LANES, GROUP_X), 0)
    ec = lax.broadcasted_iota(jnp.int32, (LANES, GROUP_X), 1) // SSM_HEAD_DIM
    expand = (er == ec).astype(BF16)
    rr = lax.broadcasted_iota(jnp.int32, (GROUP_X, LANES), 0) // SSM_HEAD_DIM
    rc = lax.broadcasted_iota(jnp.int32, (GROUP_X, LANES), 1)
    reduce = (rr == rc).astype(BF16)
    lane_head = lax.broadcasted_iota(jnp.int32, (Q, GROUP_X), 1) // SSM_HEAD_DIM
    return causal, tril, triu, expand, reduce, lane_head


def _ssd_common(xc_ref, dt_ref, par_ref, consts):
    causal, tril, _, expand, _, _ = consts
    par = par_ref[...]
    bias, alog, dsk = par[0:1, :], par[1:2, :], par[2:3, :]
    a_neg = -jnp.exp(alog)
    dtr = dt_ref[...] + bias
    dt = _softplus(dtr)
    s = _dot_l01(tril, dt * a_neg)
    dt_x = _dot_r01(dt, expand)
    s_x = _dot_r01(s, expand)
    dsk_x = _dot_r01(jnp.broadcast_to(dsk, (8, LANES)), expand)[0:1, :]
    blk = xc_ref[...]
    x = blk[:, :GROUP_X]
    bm = blk[:, GROUP_X:GROUP_X + D_STATE].astype(BF16)
    cm = blk[:, GROUP_X + D_STATE:].astype(BF16)
    xdt = x * dt_x
    g = _dot(cm, bm, _NT)
    return dict(a_neg=a_neg, dtr=dtr, dt=dt, s=s, s_t=s.T, dt_x=dt_x, s_x=s_x, dsk_x=dsk_x, x=x, bm=bm, cm=cm,
                xdt=xdt, g=g)


def _decay(v, r, causal):
    diff = v["s"][:, r:r + 1] - v["s_t"][r:r + 1, :]
    return jnp.exp(jnp.where(causal, diff, -jnp.inf))


def _ssd_specs(n_chunks, rev):
    cidx = (lambda c: n_chunks - 1 - c) if rev else (lambda c: c)
    xc = pl.BlockSpec((Q, GROUP_COLS), lambda g, c: (cidx(c), g))
    gx = pl.BlockSpec((Q, GROUP_X), lambda g, c: (cidx(c), g))
    dt = pl.BlockSpec((None, Q, LANES), lambda g, c: (g, cidx(c), 0))
    par = pl.BlockSpec((None, 8, LANES), lambda g, c: (g, 0, 0))
    nw = pl.BlockSpec((1, GROUP_X), lambda g, c: (0, g))
    hs = pl.BlockSpec((None, None, D_STATE, GROUP_X), lambda g, c: (cidx(c), g, 0, 0))
    return xc, gx, dt, par, nw, hs


def _ssd_fwd(xc, z, dtg, par, nw):
    s_dim = xc.shape[0]
    n_chunks = s_dim // Q
    xc_s, gx_s, dt_s, par_s, nw_s, hs_s = _ssd_specs(n_chunks, False)

    def body(xc_ref, z_ref, dt_ref, par_ref, nw_ref, y_ref, ys_ref, hs_ref, ht):
        @pl.when(pl.program_id(1) == 0)
        def _():
            ht[...] = jnp.zeros_like(ht)

        consts = _ssd_consts()
        causal, lane_head = consts[0], consts[5]
        v = _ssd_common(xc_ref, dt_ref, par_ref, consts)
        xdt_b = v["xdt"].astype(BF16)
        yd = jnp.zeros((Q, GROUP_X), F32)
        for r in range(HEADS_PER_GROUP):
            m = (v["g"] * _decay(v, r, causal)).astype(BF16)
            yd = yd + _dot(m, jnp.where(lane_head == r, xdt_b, jnp.zeros_like(xdt_b)))
        h = ht[...]
        hs_ref[...] = h
        yo = jnp.exp(v["s_x"]) * _dot(v["cm"], h.astype(BF16))
        y = yd + yo + v["dsk_x"] * v["x"]
        s_last = v["s_x"][Q - 1:Q, :]
        snew = _dot(v["bm"], (v["xdt"] * jnp.exp(s_last - v["s_x"])).astype(BF16), _TN)
        ht[...] = jnp.exp(s_last) * h + snew
        zz = z_ref[...]
        yg = y * (zz * _sigmoid(zz))
        y_ref[...] = y
        ys_ref[...] = _nrm(yg, nw_ref[...])[0].astype(BF16)

    return pl.pallas_call(
        body,
        grid=(SSM_GROUPS, n_chunks),
        in_specs=[xc_s, gx_s, dt_s, par_s, nw_s],
        out_specs=[gx_s, gx_s, hs_s],
        out_shape=[jax.ShapeDtypeStruct((s_dim, D_SSM), F32), jax.ShapeDtypeStruct((s_dim, D_SSM), BF16),
                   jax.ShapeDtypeStruct((n_chunks, SSM_GROUPS, D_STATE, GROUP_X), F32)],
        scratch_shapes=[pltpu.VMEM((D_STATE, GROUP_X), F32)],
        compiler_params=_cparams("parallel", "arbitrary"),
        name="ssd_fwd",
    )(xc, z, dtg, par, nw)


def _ssd_bwd(xc, z, dtg, par, nw, y, hs, dymix):
    s_dim = xc.shape[0]
    n_chunks = s_dim // Q
    xc_s, gx_s, dt_s, par_s, nw_s, hs_s = _ssd_specs(n_chunks, True)

    def body(xc_ref, z_ref, dt_ref, par_ref, nw_ref, y_ref, hs_ref, dys_ref,
             dxc_ref, dz_ref, ddt_ref, dpar_ref, dnw_ref, dht):
        @pl.when(pl.program_id(1) == 0)
        def _():
            dht[...] = jnp.zeros_like(dht)
            dpar_ref[...] = jnp.zeros_like(dpar_ref)
            dnw_ref[...] = jnp.zeros_like(dnw_ref)

        consts = _ssd_consts()
        causal, _, triu, _, reduce, lane_head = consts
        v = _ssd_common(xc_ref, dt_ref, par_ref, consts)
        x, bm, cm, xdt, s_x = v["x"], v["bm"], v["cm"], v["xdt"], v["s_x"]
        h = hs_ref[...]
        hb = h.astype(BF16)
        es_x = jnp.exp(s_x)
        yo = es_x * _dot(cm, hb)
        s_last = s_x[Q - 1:Q, :]
        e_x = jnp.exp(s_last - s_x)
        es_last = jnp.exp(s_last)

        yv, zz, nw_v = y_ref[...], z_ref[...], nw_ref[...]
        sg = _sigmoid(zz)
        gz = zz * sg
        _, n, rstd = _nrm(yv * gz, nw_v)
        dout = dys_ref[...]
        dyg, dnw = _nrm_bwd(dout, n, rstd, nw_v)
        dnw_ref[...] += dnw
        dy = dyg * gz
        dz_ref[...] = (dyg * yv * (sg * (1.0 + zz * (1.0 - sg)))).astype(BF16)

        dyb = dy.astype(BF16)
        xdt_b = xdt.astype(BF16)
        dhp = dht[...]
        dhpb = dhp.astype(BF16)
        lane = lax.broadcasted_iota(jnp.int32, (Q, LANES), 1)
        sub = lax.broadcasted_iota(jnp.int32, (LANES, Q), 0)
        dxdt = jnp.zeros((Q, GROUP_X), F32)
        dg = jnp.zeros((Q, Q), F32)
        ds = jnp.zeros((Q, LANES), F32)
        ds_t = jnp.zeros((LANES, Q), F32)
        for r in range(HEADS_PER_GROUP):
            dec = _decay(v, r, causal)
            mf = v["g"] * dec
            dyr = jnp.where(lane_head == r, dyb, jnp.zeros_like(dyb))
            dm = _dot(dyr, xdt_b, _NT)
            dxdt = dxdt + _dot(mf.astype(BF16), dyr, _TN)
            dg = dg + dm * dec
            dd = dm * mf
            ds = ds + jnp.where(lane == r, jnp.sum(dd, axis=1, keepdims=True), 0.0)
            ds_t = ds_t + jnp.where(sub == r, jnp.sum(dd, axis=0, keepdims=True), 0.0)
        ds = ds - ds_t.T
        dgb = dg.astype(BF16)
        dwb = (es_x * dy).astype(BF16)
        dcm = _dot(dgb, bm) + _dot(dwb, hb, _NT)
        dh_prev = _dot(cm, dwb, _TN)
        zst = _dot(bm, dhpb)
        xe = xdt * e_x
        dxdt = dxdt + e_x * zst
        dee = xe * zst
        dbm = _dot(dgb, cm, _TN) + _dot(xe.astype(BF16), dhpb, _NT)
        v_last = jnp.sum(dee, axis=0, keepdims=True) + es_last * jnp.sum(dhp * h, axis=0, keepdims=True)
        row_x = lax.broadcasted_iota(jnp.int32, (Q, GROUP_X), 0)
        tx = dy * yo - dee + jnp.where(row_x == Q - 1, v_last, 0.0)
        ds = ds + _dot_r01(tx, reduce)
        ddta = _dot_l01(triu, ds)
        ddt = ddta * v["a_neg"] + _dot_r01(dxdt * x, reduce)
        dalog = jnp.sum(ddta * v["dt"], axis=0, keepdims=True) * v["a_neg"]
        draw = jnp.where(lane < HEADS_PER_GROUP, ddt * _sigmoid(v["dtr"]), 0.0)
        dbias = jnp.sum(draw, axis=0, keepdims=True)
        ddsk = _dot_r01(jnp.broadcast_to(jnp.sum(dy * x, axis=0, keepdims=True), (8, GROUP_X)), reduce)[0:1, :]
        dht[...] = es_last * dhp + dh_prev
        dxc_ref[:, :GROUP_X] = dxdt * v["dt_x"] + v["dsk_x"] * dy
        dxc_ref[:, GROUP_X:GROUP_X + D_STATE] = dbm
        dxc_ref[:, GROUP_X + D_STATE:] = dcm
        ddt_ref[...] = draw
        dpar_ref[0:1, :] += dbias
        dpar_ref[1:2, :] += dalog
        dpar_ref[2:3, :] += ddsk

    return pl.pallas_call(
        body,
        grid=(SSM_GROUPS, n_chunks),
        in_specs=[xc_s, gx_s, dt_s, par_s, nw_s, gx_s, hs_s, gx_s],
        out_specs=[xc_s, gx_s, dt_s, par_s, nw_s],
        out_shape=[jax.ShapeDtypeStruct((s_dim, SSM_GROUPS * GROUP_COLS), F32),
                   jax.ShapeDtypeStruct((s_dim, D_SSM), BF16),
                   jax.ShapeDtypeStruct((SSM_GROUPS, s_dim, LANES), F32),
                   jax.ShapeDtypeStruct((SSM_GROUPS, 8, LANES), F32),
                   jax.ShapeDtypeStruct((1, D_SSM), F32)],
        scratch_shapes=[pltpu.VMEM((D_STATE, GROUP_X), F32)],
        compiler_params=_cparams("parallel", "arbitrary"),
        name="ssd_bwd",
    )(xc, z, dtg, par, nw, y, hs, dymix)


ATT_SCALE = ATT_HEAD_DIM ** -0.5
NEG_INF = -jnp.inf


def _head(h):
    return slice(h * ATT_HEAD_DIM, (h + 1) * ATT_HEAD_DIM)


def _band_masks():
    qi = lax.broadcasted_iota(jnp.int32, (ATT_BLOCK, ATT_BLOCK), 0)
    kj = lax.broadcasted_iota(jnp.int32, (ATT_BLOCK, ATT_BLOCK), 1)
    return kj <= qi, kj >= qi


def _attn_fwd(qkv_v, d):
    rows = qkv_v.shape[0]
    nb = rows // ATT_BLOCK
    blk = (ATT_BLOCK, D_ATT)
    prev = lambda i: jnp.maximum(i - 1, 0)

    def body(q_ref, kc_ref, kp_ref, vc_ref, vp_ref, o_ref, lse_ref):
        own, before = _band_masks()
        before = before & (pl.program_id(1) > 0)
        lane = lax.broadcasted_iota(jnp.int32, (ATT_BLOCK, LANES), 1)
        lse_all = jnp.zeros((ATT_BLOCK, LANES), F32)
        for h in range(ATT_HEADS):
            q = q_ref[:, _head(h)]
            sc = jnp.where(own, _dot(q, kc_ref[:, _head(h)], _NT) * ATT_SCALE, NEG_INF)
            sp = jnp.where(before, _dot(q, kp_ref[:, _head(h)], _NT) * ATT_SCALE, NEG_INF)
            m = jnp.maximum(jnp.max(sc, axis=1, keepdims=True), jnp.max(sp, axis=1, keepdims=True))
            pc, pp = jnp.exp(sc - m), jnp.exp(sp - m)
            den = jnp.sum(pc, axis=1, keepdims=True) + jnp.sum(pp, axis=1, keepdims=True)
            o = _dot(pc.astype(BF16), vc_ref[:, _head(h)]) + _dot(pp.astype(BF16), vp_ref[:, _head(h)])
            o_ref[:, _head(h)] = o / den
            lse_all = jnp.where(lane == h, m + jnp.log(den), lse_all)
        lse_ref[...] = lse_all

    return pl.pallas_call(
        body,
        grid=(d, nb),
        in_specs=[pl.BlockSpec(blk, lambda r, i: (i, 3 * r)),
                  pl.BlockSpec(blk, lambda r, i: (i, 3 * r + 1)),
                  pl.BlockSpec(blk, lambda r, i: (prev(i), 3 * r + 1)),
                  pl.BlockSpec(blk, lambda r, i: (i, 3 * r + 2)),
                  pl.BlockSpec(blk, lambda r, i: (prev(i), 3 * r + 2))],
        out_specs=[pl.BlockSpec(blk, lambda r, i: (i, r)), pl.BlockSpec((ATT_BLOCK, LANES), lambda r, i: (i, r))],
        out_shape=[jax.ShapeDtypeStruct((rows, d * D_ATT), F32), jax.ShapeDtypeStruct((rows, d * LANES), F32)],
        compiler_params=_cparams("parallel", "arbitrary"),
        name=f"attn_fwd_d{d}",
    )(qkv_v, qkv_v, qkv_v, qkv_v, qkv_v)


def _attn_combine(os_, lses):
    def fn(o1, o2, o3, l1, l2, l3):
        m = jnp.maximum(jnp.maximum(l1, l2), l3)
        tot = m + jnp.log(jnp.exp(l1 - m) + jnp.exp(l2 - m) + jnp.exp(l3 - m))
        w1, w2, w3 = jnp.exp(l1 - tot), jnp.exp(l2 - tot), jnp.exp(l3 - tot)
        cols = []
        for h in range(ATT_HEADS):
            cols.append(w1[:, h:h + 1] * o1[:, _head(h)] + w2[:, h:h + 1] * o2[:, _head(h)]
                        + w3[:, h:h + 1] * o3[:, _head(h)])
        y = jnp.concatenate(cols, axis=1)
        return y, y, tot
    return _rowcall(fn, list(os_) + list(lses), [], [(D_ATT, BF16), (D_ATT, F32), (LANES, F32)], [],
                    name="attn_combine", tr=128)


def _attn_delta(dymix, y_att):
    def fn(dy, y):
        lane = lax.broadcasted_iota(jnp.int32, (dy.shape[0], LANES), 1)
        delta = jnp.zeros((dy.shape[0], LANES), F32)
        for h in range(ATT_HEADS):
            delta = jnp.where(lane == h, jnp.sum(dy[:, _head(h)] * y[:, _head(h)], axis=1, keepdims=True), delta)
        return dy, delta
    return _rowcall(fn, [dymix, y_att], [], [(D_ATT, BF16), (LANES, F32)], [], name="attn_delta",
                    row_cols=[(D_ATT, 1), None])


def _attn_bwd(qkv_v, dy_v, lse_v, delta_v, d):
    rows = qkv_v.shape[0]
    nb = rows // ATT_BLOCK
    blk = (ATT_BLOCK, D_ATT)
    sblk = (ATT_BLOCK, LANES)
    prev = lambda i: jnp.maximum(i - 1, 0)
    nxt = lambda i: jnp.minimum(i + 1, nb - 1)

    def body(qc_ref, qn_ref, kc_ref, kp_ref, vc_ref, vp_ref, dyc_ref, dyn_ref, lc_ref, ln_ref, dc_ref, dn_ref,
             dq_ref, dk_ref, dv_ref):
        i = pl.program_id(1)
        own, before = _band_masks()
        before_c = before & (i > 0)
        before_n = before & (i < nb - 1)
        lc, ln, dc, dn = lc_ref[...], ln_ref[...], dc_ref[...], dn_ref[...]
        for h in range(ATT_HEADS):
            hs = _head(h)
            q, qn, kc, kp, vc, vp = qc_ref[:, hs], qn_ref[:, hs], kc_ref[:, hs], kp_ref[:, hs], vc_ref[:, hs], vp_ref[:, hs]
            dy, dyn = dyc_ref[:, hs], dyn_ref[:, hs]
            lse, lse_n, dl, dl_n = lc[:, h:h + 1], ln[:, h:h + 1], dc[:, h:h + 1], dn[:, h:h + 1]
            pc = jnp.exp(jnp.where(own, _dot(q, kc, _NT) * ATT_SCALE - lse, NEG_INF))
            pp = jnp.exp(jnp.where(before_c, _dot(q, kp, _NT) * ATT_SCALE - lse, NEG_INF))
            pn = jnp.exp(jnp.where(before_n, _dot(qn, kc, _NT) * ATT_SCALE - lse_n, NEG_INF))
            dsc = (pc * (_dot(dy, vc, _NT) - dl)).astype(BF16)
            dsp = (pp * (_dot(dy, vp, _NT) - dl)).astype(BF16)
            dsn = (pn * (_dot(dyn, vc, _NT) - dl_n)).astype(BF16)
            dq_ref[:, hs] = (_dot(dsc, kc) + _dot(dsp, kp)) * ATT_SCALE
            dk_ref[:, hs] = (_dot(dsc, q, _TN) + _dot(dsn, qn, _TN)) * ATT_SCALE
            dv_ref[:, hs] = _dot(pc.astype(BF16), dy, _TN) + _dot(pn.astype(BF16), dyn, _TN)

    return pl.pallas_call(
        body,
        grid=(d, nb),
        in_specs=[pl.BlockSpec(blk, lambda r, i: (i, 3 * r)), pl.BlockSpec(blk, lambda r, i: (nxt(i), 3 * r)),
                  pl.BlockSpec(blk, lambda r, i: (i, 3 * r + 1)), pl.BlockSpec(blk, lambda r, i: (prev(i), 3 * r + 1)),
                  pl.BlockSpec(blk, lambda r, i: (i, 3 * r + 2)), pl.BlockSpec(blk, lambda r, i: (prev(i), 3 * r + 2)),
                  pl.BlockSpec(blk, lambda r, i: (i, r)), pl.BlockSpec(blk, lambda r, i: (nxt(i), r)),
                  pl.BlockSpec(sblk, lambda r, i: (i, r)), pl.BlockSpec(sblk, lambda r, i: (nxt(i), r)),
                  pl.BlockSpec(sblk, lambda r, i: (i, r)), pl.BlockSpec(sblk, lambda r, i: (nxt(i), r))],
        out_specs=[pl.BlockSpec(blk, lambda r, i: (i, r))] * 3,
        out_shape=[jax.ShapeDtypeStruct((rows, d * D_ATT), F32)] * 3,
        compiler_params=_cparams("parallel", "arbitrary"),
        name=f"attn_bwd_d{d}",
    )(qkv_v, qkv_v, qkv_v, qkv_v, qkv_v, qkv_v, dy_v, dy_v, lse_v, lse_v, delta_v, delta_v)


def _attn_sum(dqs, dks, dvs):
    def fn(*parts):
        return (jnp.concatenate([parts[0] + parts[1] + parts[2], parts[3] + parts[4] + parts[5],
                                 parts[6] + parts[7] + parts[8]], axis=1),)
    return _rowcall(fn, list(dqs) + list(dks) + list(dvs), [], [(3 * D_ATT, BF16)], [], name="attn_sum", tr=128)[0]


def _attention_fwd(qkv):
    s_dim = qkv.shape[0]
    os_, lses = [], []
    for d in DILATIONS:
        o, lse = _attn_fwd(qkv.reshape(s_dim // d, d * 3 * D_ATT), d)
        os_.append(o.reshape(s_dim, D_ATT))
        lses.append(lse.reshape(s_dim, LANES))
    return _attn_combine(os_, lses)


def _attention_bwd(qkv, dymix, y_att, lse):
    s_dim = qkv.shape[0]
    dy, delta = _attn_delta(dymix, y_att)
    dqs, dks, dvs = [], [], []
    for d in DILATIONS:
        dq, dk, dv = _attn_bwd(qkv.reshape(s_dim // d, d * 3 * D_ATT), dy.reshape(s_dim // d, d * D_ATT),
                               lse.reshape(s_dim // d, d * LANES), delta.reshape(s_dim // d, d * LANES), d)
        dqs.append(dq.reshape(s_dim, D_ATT))
        dks.append(dk.reshape(s_dim, D_ATT))
        dvs.append(dv.reshape(s_dim, D_ATT))
    return _attn_sum(dqs, dks, dvs)


def _adamw(w, g, m, v, name):
    def fn(wb, gb, mb, vb):
        m2 = ADAM_B1 * mb + (1.0 - ADAM_B1) * gb
        v2 = ADAM_B2 * vb + (1.0 - ADAM_B2) * (gb * gb)
        m_hat = m2 / (1.0 - ADAM_B1 ** ADAM_STEP)
        v_hat = v2 / (1.0 - ADAM_B2 ** ADAM_STEP)
        delta = -ADAM_LR * (m_hat / (jnp.sqrt(v_hat) + ADAM_EPS) + ADAM_WD * wb)
        return delta, m2, v2
    cols = w.shape[1]
    tr = 128 if w.shape[0] % 128 == 0 else w.shape[0]
    return _rowcall(fn, [w, g, m, v], [], [(cols, F32)] * 3, [], name=name, tr=tr)


ANY = pl.BlockSpec(memory_space=pl.ANY)


def _position():
    x, y, c = lax.axis_index("x"), lax.axis_index("y"), lax.axis_index("c")
    chips = [(1 - x, y), (x, 1 - y), (1 - x, 1 - y)]
    return x, y, c, chips


def _remote(src, dst, send_sem, recv_sem, device):
    return pltpu.make_async_remote_copy(src_ref=src, dst_ref=dst, send_sem=send_sem, recv_sem=recv_sem,
                                        device_id=device, device_id_type=MESH)


def _gather_shards(shards):
    n = len(shards)

    def body(*refs):
        ins, outs = refs[:n], refs[n:2 * n]
        send_sems, recv_sems = refs[2 * n:]
        x, y, c, chips = _position()
        sibling = (x, y, 1 - c)

        def half(a, j, cc):
            h = ins[a].shape[0] // 2
            return outs[a].at[j, pl.ds(cc * h, h), :]

        sent = []
        for a in range(n):
            h = ins[a].shape[0] // 2
            for j, chip in enumerate(chips):
                cp = _remote(ins[a].at[pl.ds(c * h, h), :], half(a, j, c), send_sems.at[6 * a + j],
                             recv_sems.at[6 * a + j], (chip[0], chip[1], c))
                cp.start()
                sent.append(cp)
        for a in range(n):
            for j in range(3):
                landed = half(a, j, c)
                _remote(landed, landed, send_sems.at[6 * a + j], recv_sems.at[6 * a + j], (x, y, c)).wait_recv()
                cp = _remote(landed, landed, send_sems.at[6 * a + 3 + j], recv_sems.at[6 * a + 3 + j], sibling)
                cp.start()
                sent.append(cp)
        for a in range(n):
            for j in range(3):
                handed = half(a, j, 1 - c)
                _remote(handed, handed, send_sems.at[6 * a + 3 + j], recv_sems.at[6 * a + 3 + j], (x, y, c)).wait_recv()
        for cp in sent:
            cp.wait_send()

    return pl.pallas_call(
        body,
        in_specs=[ANY] * n,
        out_specs=[ANY] * n,
        out_shape=[jax.ShapeDtypeStruct((3,) + s.shape, s.dtype) for s in shards],
        scratch_shapes=[pltpu.SemaphoreType.DMA((6 * n,)), pltpu.SemaphoreType.DMA((6 * n,))],
        name="gather_shards",
    )(*shards)


def _by_chip(own, others):
    me = 2 * lax.axis_index("x") + lax.axis_index("y")
    rel = jnp.stack([own, others[1], others[0], others[2]])
    return jnp.stack([lax.dynamic_index_in_dim(rel, q ^ me, 0, keepdims=False) for q in range(N_CHIPS)])


def _exchange_sibling_halves(grads):
    n = len(grads)

    def body(*refs):
        ins, outs = refs[:n], refs[n:2 * n]
        send_sems, recv_sems = refs[2 * n:]
        x, y, c, _ = _position()
        copies = []
        for a in range(n):
            h = ins[a].shape[1] // 2
            cp = _remote(ins[a].at[:, pl.ds((1 - c) * h, h), :], outs[a], send_sems.at[a], recv_sems.at[a], (x, y, 1 - c))
            cp.start()
            copies.append(cp)
        for cp in copies:
            cp.wait()

    return pl.pallas_call(
        body,
        in_specs=[ANY] * n,
        out_specs=[ANY] * n,
        out_shape=[jax.ShapeDtypeStruct((g.shape[0], g.shape[1] // 2, g.shape[2]), g.dtype) for g in grads],
        scratch_shapes=[pltpu.SemaphoreType.DMA((n,)), pltpu.SemaphoreType.DMA((n,))],
        name="exchange_sibling_halves",
    )(*grads)


def _exchange_quarters(parts):
    n = len(parts)

    def body(*refs):
        ins, outs = refs[:n], refs[n:2 * n]
        send_sems, recv_sems = refs[2 * n:]
        x, y, c, chips = _position()
        copies = []
        for a in range(n):
            for j, chip in enumerate(chips):
                cp = _remote(ins[a].at[2 * chip[0] + chip[1]], outs[a].at[j], send_sems.at[3 * a + j],
                             recv_sems.at[3 * a + j], (chip[0], chip[1], c))
                cp.start()
                copies.append(cp)
        for cp in copies:
            cp.wait()

    return pl.pallas_call(
        body,
        in_specs=[ANY] * n,
        out_specs=[ANY] * n,
        out_shape=[jax.ShapeDtypeStruct((3,) + p.shape[1:], p.dtype) for p in parts],
        scratch_shapes=[pltpu.SemaphoreType.DMA((3 * n,)), pltpu.SemaphoreType.DMA((3 * n,))],
        name="exchange_quarters",
    )(*parts)


def _share_reduced_halves(halves):
    n = len(halves)

    def body(*refs):
        ins, outs = refs[:n], refs[n:2 * n]
        send_sems, recv_sems = refs[2 * n:]
        x, y, c, _ = _position()
        copies = []
        for a in range(n):
            cp = _remote(ins[a], outs[a], send_sems.at[a], recv_sems.at[a], (x, y, 1 - c))
            cp.start()
            copies.append(cp)
        for cp in copies:
            cp.wait()

    return pl.pallas_call(
        body,
        in_specs=[ANY] * n,
        out_specs=[ANY] * n,
        out_shape=[jax.ShapeDtypeStruct(p.shape, p.dtype) for p in halves],
        scratch_shapes=[pltpu.SemaphoreType.DMA((n,)), pltpu.SemaphoreType.DMA((n,))],
        name="share_reduced_halves",
    )(*halves)


def _add_sibling(grad, got, c_arr, name):
    nq, rows, cols = grad.shape
    h = rows // 2
    tr = 128
    nb = h // tr

    def body(c_ref, a_ref, b_ref, o_ref, ob_ref):
        total = a_ref[...] + b_ref[...]
        o_ref[...] = total
        ob_ref[...] = total.astype(BF16)

    out_spec = pl.BlockSpec((None, tr, cols), lambda q, i, c: (q, i, 0))
    return pl.pallas_call(
        body,
        grid_spec=pltpu.PrefetchScalarGridSpec(
            num_scalar_prefetch=1, grid=(nq, nb),
            in_specs=[pl.BlockSpec((None, tr, cols), lambda q, i, c: (q, c[0] * nb + i, 0)),
                      pl.BlockSpec((None, tr, cols), lambda q, i, c: (q, i, 0))],
            out_specs=[out_spec, out_spec]),
        out_shape=[jax.ShapeDtypeStruct((nq, h, cols), F32), jax.ShapeDtypeStruct((nq, h, cols), BF16)],
        compiler_params=_cparams("parallel", "parallel"),
        name=name,
    )(c_arr, grad, got)


def _add_chips(part, got, chip_arr, name):
    _, h, cols = part.shape
    tr = 128

    def body(q_ref, p_ref, g0_ref, g1_ref, g2_ref, o_ref):
        o_ref[...] = ((p_ref[...] + g0_ref[...].astype(F32)) + g1_ref[...].astype(F32)) + g2_ref[...].astype(F32)

    got_spec = lambda j: pl.BlockSpec((None, tr, cols), lambda i, q: (j, i, 0))
    return pl.pallas_call(
        body,
        grid_spec=pltpu.PrefetchScalarGridSpec(
            num_scalar_prefetch=1, grid=(h // tr,),
            in_specs=[pl.BlockSpec((None, tr, cols), lambda i, q: (q[0], i, 0)), got_spec(0), got_spec(1), got_spec(2)],
            out_specs=pl.BlockSpec((tr, cols), lambda i, q: (i, 0))),
        out_shape=jax.ShapeDtypeStruct((h, cols), F32),
        compiler_params=_cparams("parallel"),
        name=name,
    )(chip_arr, part, got, got, got)


def _reduce_scatter(grads, names):
    c_arr = lax.axis_index("c").astype(jnp.int32).reshape(1)
    chip_arr = (2 * lax.axis_index("x") + lax.axis_index("y")).astype(jnp.int32).reshape(1)
    from_sibling = _exchange_sibling_halves(grads)
    parts = [_add_sibling(g, s, c_arr, f"add_sibling_{nm}") for g, s, nm in zip(grads, from_sibling, names)]
    from_chips = _exchange_quarters([pb for _, pb in parts])
    halves = [_add_chips(p, f, chip_arr, f"add_chips_{nm}") for (p, _), f, nm in zip(parts, from_chips, names)]
    return list(zip(halves, _share_reduced_halves(halves)))


def _adamw_halves(w, mine, other, m, v, name):
    rows, cols = w.shape
    tr = 128
    nb = rows // 2 // tr
    c_arr = lax.axis_index("c").astype(jnp.int32).reshape(1)

    def body(c_ref, w_ref, a_ref, b_ref, m_ref, v_ref, g_out, d_out, m_out, v_out):
        is_mine = (pl.program_id(0) // nb) == c_ref[0]
        g = jnp.where(is_mine, a_ref[...], b_ref[...])
        wb, mb, vb = w_ref[...], m_ref[...], v_ref[...]
        m2 = ADAM_B1 * mb + (1.0 - ADAM_B1) * g
        v2 = ADAM_B2 * vb + (1.0 - ADAM_B2) * (g * g)
        m_hat = m2 / (1.0 - ADAM_B1 ** ADAM_STEP)
        v_hat = v2 / (1.0 - ADAM_B2 ** ADAM_STEP)
        g_out[...] = g
        d_out[...] = -ADAM_LR * (m_hat / (jnp.sqrt(v_hat) + ADAM_EPS) + ADAM_WD * wb)
        m_out[...] = m2
        v_out[...] = v2

    full = pl.BlockSpec((tr, cols), lambda i, c: (i, 0))
    half = pl.BlockSpec((tr, cols), lambda i, c: (i % nb, 0))
    return pl.pallas_call(
        body,
        grid_spec=pltpu.PrefetchScalarGridSpec(
            num_scalar_prefetch=1, grid=(rows // tr,),
            in_specs=[full, half, half, full, full], out_specs=[full] * 4),
        out_shape=[jax.ShapeDtypeStruct((rows, cols), F32)] * 4,
        compiler_params=_cparams("parallel"),
        name=name,
    )(c_arr, w, mine, other, m, v)


def _all_sum_small(v):
    n_dev = 8

    def body(v_ref, o_ref, gath, send_sems, recv_sems):
        x, y, c, _ = _position()
        me = 4 * x + 2 * y + c
        gath[me] = v_ref[...]
        copies = []
        for k in range(1, n_dev):
            peer = tuple(1 - p if (k >> s) & 1 else p for p, s in ((x, 2), (y, 1), (c, 0)))
            cp = _remote(v_ref, gath.at[me], send_sems.at[k - 1], recv_sems.at[k - 1], peer)
            cp.start()
            copies.append(cp)
        for cp in copies:
            cp.wait()
        acc = gath[0]
        for i in range(1, n_dev):
            acc = acc + gath[i]
        o_ref[...] = acc

    vm = pl.BlockSpec(memory_space=pltpu.VMEM)
    return pl.pallas_call(
        body,
        in_specs=[vm],
        out_specs=vm,
        out_shape=jax.ShapeDtypeStruct(v.shape, F32),
        scratch_shapes=[pltpu.VMEM((n_dev,) + v.shape, F32), pltpu.SemaphoreType.DMA((n_dev - 1,)),
                        pltpu.SemaphoreType.DMA((n_dev - 1,))],
        name="all_sum_small",
    )(v)


def _pack_rows(vectors):
    rows = []
    for v in vectors:
        flat = v.reshape(-1).astype(F32)
        rows.append(jnp.pad(flat, (0, (-flat.shape[0]) % LANES)).reshape(-1, LANES))
    out = jnp.concatenate(rows, axis=0)
    return jnp.pad(out, ((0, (-out.shape[0]) % 8), (0, 0)))


def _unpack_rows(packed, shapes):
    outs, r = [], 0
    for shp in shapes:
        size = math.prod(shp)
        nr = -(-size // LANES)
        outs.append(packed[r:r + nr].reshape(-1)[:size].reshape(shp))
        r += nr
    return outs


def _relu_sq(acc):
    r = jnp.maximum(acc, 0.0)
    return r, r * r


def _relu_sq_bwd(acc, r):
    return (acc * (2.0 * r.astype(F32)),)


def kernel(x, norm_mix_pre, w_in, conv_w, conv_b, dt_bias, a_log, d_skip, ssm_norm_w, w_out, norm_mix_post, norm_mlp_pre, w_up, w_down, norm_mlp_post, loss_target, m_norm_mix_pre, m_w_in, m_conv_w, m_conv_b, m_dt_bias, m_a_log, m_d_skip, m_ssm_norm_w, m_w_out, m_norm_mix_post, m_norm_mlp_pre, m_w_up, m_w_down, m_norm_mlp_post, v_norm_mix_pre, v_w_in, v_conv_w, v_conv_b, v_dt_bias, v_a_log, v_d_skip, v_ssm_norm_w, v_w_out, v_norm_mix_post, v_norm_mlp_pre, v_w_up, v_w_down, v_norm_mlp_post):
    s_dim = x.shape[1]
    xs, target = x[0], loss_target[0]
    chip = 2 * lax.axis_index("x") + lax.axis_index("y")

    own = [w_in[0].astype(BF16), w_out[0].astype(BF16), w_up[0].astype(BF16), w_down[0].astype(BF16)]
    g_in, g_out, g_up, g_down = [_by_chip(o, f) for o, f in zip(own, _gather_shards(own))]
    w_in_full = g_in.transpose(1, 0, 2).reshape(D_MODEL, D_IN_PROJ)
    w_z = w_in_full[:, :D_SSM]
    w_xbc = _perm_cols(w_in_full[:, D_SSM:D_SSM + D_XBC])
    w_dt = jnp.pad(w_in_full[:, D_SSM + D_XBC:D_SSM + D_XBC + SSM_HEADS], ((0, 0), (0, LANES - SSM_HEADS)))
    w_qkv = w_in_full[:, D_SSM + D_XBC + SSM_HEADS:]
    w_out_full = g_out.reshape(D_MIX, D_MODEL)
    w_up_full = g_up.transpose(1, 0, 2).reshape(D_MODEL, D_FF)
    w_down_full = g_down.reshape(D_FF, D_MODEL)

    conv_cols = D_XBC // N_CHIPS
    conv_placed = lax.dynamic_update_slice(jnp.zeros((8, D_XBC), F32), 0.5 * conv_w[0], (0, chip * conv_cols))
    conv_full = _all_sum_small(conv_placed.reshape(-1, LANES)).reshape(8, D_XBC)
    w8 = _perm_cols(conv_full.at[CONV_WIDTH].set(conv_b[0]))

    u = _pre_norm(xs, norm_mix_pre)
    z = _matmul([(u, w_z, TK)], "nn", [F32], name="proj_z")
    xbc = _matmul([(u, w_xbc, TK)], "nn", [F32], name="proj_xbc")
    dt_raw = _matmul([(u, w_dt, TK)], "nn", [F32], name="proj_dt")
    qkv = _matmul([(u, w_qkv, TK)], "nn", [BF16], name="proj_qkv")
    xc = _conv_fwd(xbc, w8)
    dtg = _dt_to_groups(dt_raw)
    par = _pack_ssd_params(dt_bias[0], a_log[0], d_skip[0])
    y, y_ssm, states = _ssd_fwd(xc, z, dtg, par, ssm_norm_w)
    y_att, y_att_f32, lse = _attention_fwd(qkv)
    y_mix = jnp.concatenate([y_ssm, y_att], axis=1)
    mix = _matmul([(y_mix, w_out_full, TK)], "nn", [F32], name="out_proj")
    h1, u2 = _post_pre_norm(xs, mix, norm_mix_post, norm_mlp_pre)
    hid, act = _matmul([(u2, w_up_full, TK)], "nn", [BF16, BF16], name="mlp_up", epilogue=_relu_sq)
    ff = _matmul([(act, w_down_full, TK)], "nn", [F32], name="mlp_down")
    dh2, dff, d_g4, loss_part = _tail(ff, h1, target, norm_mlp_post)

    dhid = _matmul([(dff, w_down_full, TK)], "nt", [BF16], name="mlp_down_dx", epilogue=_relu_sq_bwd, extras=[hid])
    dw_down = _matmul([(act, dff, TK)], "tn", [F32], name="mlp_down_dw")
    dw_up = _matmul([(u2, dhid, TK)], "tn", [F32], name="mlp_up_dw")
    du2 = _matmul([(dhid, w_up_full, TK)], "nt", [F32], name="mlp_up_dx")
    dh1, dmix, d_g3, d_g2 = _mid_bwd(du2, h1, dh2, mix, norm_mix_post, norm_mlp_pre)
    dymix = _matmul([(dmix, w_out_full, TK)], "nt", [F32], name="out_proj_dx")
    dw_out = _matmul([(y_mix, dmix, TK)], "tn", [F32], name="out_proj_dw")
    dqkv = _attention_bwd(qkv, dymix, y_att_f32, lse)
    dxc, dz, ddtg, dpar, d_nw = _ssd_bwd(xc, z, dtg, par, ssm_norm_w, y, states, dymix)
    dxbc, dw8 = _conv_bwd(xbc, w8, dxc)
    ddt = jnp.pad(_dt_from_groups(ddtg), ((0, 0), (0, LANES - SSM_HEADS))).astype(BF16)
    du = _matmul([(dz, w_z, TK_MULTI), (dxbc, w_xbc, TK_MULTI), (dqkv, w_qkv, TK_MULTI), (ddt, w_dt, LANES)], "nt", [F32],
                 name="proj_dx")
    dw_z = _matmul([(u, dz, TK)], "tn", [F32], name="proj_z_dw")
    dw_xbc = _matmul([(u, dxbc, TK)], "tn", [F32], name="proj_xbc_dw")
    dw_dt = _matmul([(u, ddt, TK)], "tn", [F32], name="proj_dt_dw")
    dw_qkv = _matmul([(u, dqkv, TK)], "tn", [F32], name="proj_qkv_dw")
    grad_x, d_g1 = _first_bwd(du, xs, dh1, norm_mix_pre)

    dw_in = jnp.concatenate([dw_z, _unperm_cols(dw_xbc), dw_dt[:, :SSM_HEADS], dw_qkv], axis=1)
    big = _reduce_scatter(
        [dw_in.reshape(D_MODEL, N_CHIPS, W_IN_SHARD).transpose(1, 0, 2),
         dw_out.reshape(N_CHIPS, D_MIX // N_CHIPS, D_MODEL),
         dw_up.reshape(D_MODEL, N_CHIPS, D_FF // N_CHIPS).transpose(1, 0, 2),
         dw_down.reshape(N_CHIPS, D_FF // N_CHIPS, D_MODEL)],
        ["w_in", "w_out", "w_up", "w_down"])
    dconv = _unperm_cols(dw8)
    d_bias, d_alog, d_dskip = _unpack_ssd_params(dpar)
    small_shapes = [(1, D_MODEL), (CONV_WIDTH, D_XBC), (1, D_XBC), (1, SSM_HEADS), (1, SSM_HEADS), (1, SSM_HEADS),
                    (1, D_SSM), (1, D_MODEL), (1, D_MODEL), (1, D_MODEL), (1, LANES)]
    summed = _unpack_rows(
        _all_sum_small(_pack_rows([d_g1, dconv[:CONV_WIDTH], dconv[CONV_WIDTH:CONV_WIDTH + 1], d_bias, d_alog,
                                   d_dskip, d_nw, d_g2, d_g3, d_g4, loss_part])), small_shapes)
    (g_g1, g_conv_full, g_conv_b, g_bias, g_alog, g_dskip, g_nw, g_g2, g_g3, g_g4, loss_row) = summed
    loss = loss_row[0, 0]
    g_conv_w = lax.dynamic_slice(g_conv_full, (0, chip * conv_cols), (CONV_WIDTH, conv_cols))[None]

    grads = {"norm_mix_pre": g_g1, "conv_w": g_conv_w, "conv_b": g_conv_b, "dt_bias": g_bias,
             "a_log": g_alog, "d_skip": g_dskip, "ssm_norm_w": g_nw, "norm_mix_post": g_g2,
             "norm_mlp_pre": g_g3, "norm_mlp_post": g_g4}
    weights = {"norm_mix_pre": (norm_mix_pre, m_norm_mix_pre, v_norm_mix_pre), "w_in": (w_in, m_w_in, v_w_in),
               "conv_w": (conv_w, m_conv_w, v_conv_w), "conv_b": (conv_b, m_conv_b, v_conv_b),
               "dt_bias": (dt_bias, m_dt_bias, v_dt_bias), "a_log": (a_log, m_a_log, v_a_log),
               "d_skip": (d_skip, m_d_skip, v_d_skip), "ssm_norm_w": (ssm_norm_w, m_ssm_norm_w, v_ssm_norm_w),
               "w_out": (w_out, m_w_out, v_w_out), "norm_mix_post": (norm_mix_post, m_norm_mix_post, v_norm_mix_post),
               "norm_mlp_pre": (norm_mlp_pre, m_norm_mlp_pre, v_norm_mlp_pre), "w_up": (w_up, m_w_up, v_w_up),
               "w_down": (w_down, m_w_down, v_w_down),
               "norm_mlp_post": (norm_mlp_post, m_norm_mlp_post, v_norm_mlp_post)}
    order = list(weights)
    big_names = ("w_in", "w_out", "w_up", "w_down")
    small_names = [n for n in order if n not in big_names]
    delta, new_m, new_v = {}, {}, {}
    for n, (mine, other) in zip(big_names, big):
        w, m, v = weights[n]
        g_, d_, m_, v_ = _adamw_halves(w[0], mine, other, m[0], v[0], f"adamw_{n}")
        grads[n], delta[n], new_m[n], new_v[n] = g_[None], d_[None], m_[None], v_[None]
    small_w_shapes = [weights[n][0].shape for n in small_names]
    packed = [_pack_rows([weights[n][k] for n in small_names]) for k in range(3)]
    packed_g = _pack_rows([grads[n].reshape(weights[n][0].shape) for n in small_names])
    sd, sm, sv = _adamw(packed[0], packed_g, packed[1], packed[2], "adamw_small")
    for k, n in enumerate(small_names):
        grads[n] = grads[n].reshape(weights[n][0].shape)
    for res, pk in ((delta, sd), (new_m, sm), (new_v, sv)):
        for n, val in zip(small_names, _unpack_rows(pk, small_w_shapes)):
            res[n] = val

    return (loss, grad_x[None], *[grads[n] for n in order], *[delta[n] for n in order],
            *[new_m[n] for n in order], *[new_v[n] for n in order])
```

```python
import functools
import math

import numpy as np
import jax
import jax.numpy as jnp
from jax import lax
from jax.experimental import pallas as pl
from jax.experimental.pallas import tpu as pltpu
from jax.experimental.pallas import tpu_sc as plsc

F32 = jnp.float32
BF16 = jnp.bfloat16

D_MODEL = 2048
SSM_HEAD_DIM = 64
SSM_GROUPS = 8
HEADS_PER_GROUP = 4
SSM_HEADS = SSM_GROUPS * HEADS_PER_GROUP
D_SSM = SSM_HEADS * SSM_HEAD_DIM
D_STATE = 128
CONV_WIDTH = 4
SSD_CHUNK = 128
D_XBC = D_SSM + 2 * SSM_GROUPS * D_STATE
GROUP_X = HEADS_PER_GROUP * SSM_HEAD_DIM
GROUP_COLS = GROUP_X + 2 * D_STATE
ATT_HEAD_DIM = 128
ATT_HEADS = 16
D_ATT = ATT_HEADS * ATT_HEAD_DIM
DILATIONS = (1, 4, 16)
ATT_BLOCK = 128
D_MIX = D_SSM + D_ATT
D_IN_PROJ = D_SSM + D_XBC + SSM_HEADS + 3 * D_ATT
D_FF = 4 * D_MODEL
EPS = 1e-6
N_CHIPS = 4
W_IN_SHARD = D_IN_PROJ // N_CHIPS

ADAM_LR = 0.001
ADAM_B1 = 0.9
ADAM_B2 = 0.999
ADAM_EPS = 1e-08
ADAM_WD = 0.01
ADAM_STEP = 10

LANES = 128
VMEM_LIMIT = 48 * 1024 * 1024
MESH = pl.DeviceIdType.MESH

_NN = (((1,), (0,)), ((), ()))
_NT = (((1,), (1,)), ((), ()))
_TN = (((0,), (0,)), ((), ()))


def _dot(a, b, dims=_NN):
    return lax.dot_general(a, b, dims, preferred_element_type=F32)


def _cparams(*sem):
    return pltpu.CompilerParams(dimension_semantics=sem, vmem_limit_bytes=VMEM_LIMIT)


TK = 2048
TK_MULTI = 1024


def _matmul(pairs, mode, out_dtypes, *, name, tm=1024, tn=1024, epilogue=None, extras=()):
    a0, b0, _ = pairs[0]
    m_dim = a0.shape[1] if mode == "tn" else a0.shape[0]
    n_dim = b0.shape[0] if mode == "nt" else b0.shape[1]
    tm, tn = min(tm, m_dim), min(tn, n_dim)
    nks, offs = [], []
    for a, _, tk in pairs:
        k_dim = a.shape[0] if mode == "tn" else a.shape[1]
        assert k_dim % tk == 0, (name, k_dim, tk)
        offs.append(sum(nks))
        nks.append(k_dim // tk)
    nk_total = sum(nks)
    assert m_dim % tm == 0 and n_dim % tn == 0, (name, m_dim, n_dim)
    dims = {"nn": _NN, "nt": _NT, "tn": _TN}[mode]
    n_pairs, n_extra, n_out = len(pairs), len(extras), len(out_dtypes)

    in_specs, operands = [], []
    for (a, b, tk), off, nk in zip(pairs, offs, nks):
        def kidx(k, off=off, nk=nk):
            return k if n_pairs == 1 else jnp.clip(k - off, 0, nk - 1)
        if mode == "tn":
            in_specs.append(pl.BlockSpec((tk, tm), lambda m, n, k, f=kidx: (f(k), m)))
        else:
            in_specs.append(pl.BlockSpec((tm, tk), lambda m, n, k, f=kidx: (m, f(k))))
        if mode == "nt":
            in_specs.append(pl.BlockSpec((tn, tk), lambda m, n, k, f=kidx: (n, f(k))))
        else:
            in_specs.append(pl.BlockSpec((tk, tn), lambda m, n, k, f=kidx: (f(k), n)))
        operands += [a, b]
    for e in extras:
        in_specs.append(pl.BlockSpec((tm, tn), lambda m, n, k: (m, n)))
        operands.append(e)

    def body(*refs):
        ab = refs[:2 * n_pairs]
        e_refs = refs[2 * n_pairs:2 * n_pairs + n_extra]
        o_refs = refs[2 * n_pairs + n_extra:2 * n_pairs + n_extra + n_out]

        def finish(total):
            vals = (total,) if epilogue is None else epilogue(total, *[e[...] for e in e_refs])
            for o_ref, v in zip(o_refs, vals):
                o_ref[...] = v.astype(o_ref.dtype)

        if nk_total == 1:
            finish(_dot(ab[0][...], ab[1][...], dims))
            return
        acc = refs[-1]
        k = pl.program_id(2)

        @pl.when(k == 0)
        def _():
            acc[...] = jnp.zeros_like(acc)

        for i in range(n_pairs):
            def accumulate(i=i):
                acc[...] += _dot(ab[2 * i][...], ab[2 * i + 1][...], dims)
            if n_pairs == 1:
                accumulate()
            else:
                pl.when((k >= offs[i]) & (k < offs[i] + nks[i]))(accumulate)

        @pl.when(k == nk_total - 1)
        def _():
            finish(acc[...])

    outs = pl.pallas_call(
        body,
        grid=(m_dim // tm, n_dim // tn, nk_total),
        in_specs=in_specs,
        out_specs=[pl.BlockSpec((tm, tn), lambda m, n, k: (m, n)) for _ in out_dtypes],
        out_shape=[jax.ShapeDtypeStruct((m_dim, n_dim), dt) for dt in out_dtypes],
        scratch_shapes=[pltpu.VMEM((tm, tn), F32)] if nk_total > 1 else [],
        compiler_params=_cparams("parallel", "parallel", "arbitrary"),
        name=name,
    )(*operands)
    return outs[0] if n_out == 1 else outs


def _rowcall(fn, rows, vecs, row_outs, acc_widths, *, name, tr=256, row_cols=None):
    s_dim = rows[0].shape[0]
    assert s_dim % tr == 0
    row_cols = row_cols or [None] * len(rows)
    n_r, n_v, n_ro, n_acc = len(rows), len(vecs), len(row_outs), len(acc_widths)
    in_specs = []
    for r, rc in zip(rows, row_cols):
        if rc is None:
            in_specs.append(pl.BlockSpec((tr, r.shape[1]), lambda i: (i, 0)))
        else:
            in_specs.append(pl.BlockSpec((tr, rc[0]), lambda i, c=rc[1]: (i, c)))
    for v in vecs:
        in_specs.append(pl.BlockSpec(v.shape, lambda i, nd=v.ndim: (0,) * nd))

    def body(*refs):
        ins = [r[...] for r in refs[:n_r + n_v]]
        ro = refs[n_r + n_v:n_r + n_v + n_ro]
        ao = refs[n_r + n_v + n_ro:]
        outs = fn(*ins)
        for ref, v in zip(ro, outs[:n_ro]):
            ref[...] = v.astype(ref.dtype)
        if n_acc:
            @pl.when(pl.program_id(0) == 0)
            def _():
                for ref in ao:
                    ref[...] = jnp.zeros_like(ref)
            for ref, v in zip(ao, outs[n_ro:]):
                ref[...] += v

    outs = pl.pallas_call(
        body,
        grid=(s_dim // tr,),
        in_specs=in_specs,
        out_specs=[pl.BlockSpec((tr, w), lambda i: (i, 0)) for w, _ in row_outs]
        + [pl.BlockSpec((1, w), lambda i: (0, 0)) for w in acc_widths],
        out_shape=[jax.ShapeDtypeStruct((s_dim, w), dt) for w, dt in row_outs]
        + [jax.ShapeDtypeStruct((1, w), F32) for w in acc_widths],
        compiler_params=_cparams("arbitrary"),
        name=name,
    )(*rows, *vecs)
    return outs


def _nrm(x, g):
    r = lax.rsqrt(jnp.mean(x * x, axis=-1, keepdims=True) + EPS)
    n = x * r
    return n * g, n, r


def _nrm_bwd(dy, n, r, g):
    dn = dy * g
    dx = r * (dn - n * jnp.mean(dn * n, axis=-1, keepdims=True))
    return dx, jnp.sum(dy * n, axis=0, keepdims=True)


def _sigmoid(x):
    return 1.0 / (1.0 + jnp.exp(-x))


def _softplus(x):
    return jnp.maximum(x, 0.0) + jnp.log(1.0 + jnp.exp(-jnp.abs(x)))


def _pre_norm(x, g1):
    def fn(xb, g):
        return (_nrm(xb, g)[0],)
    return _rowcall(fn, [x], [g1], [(D_MODEL, BF16)], [], name="pre_norm")[0]


def _post_pre_norm(x, mix, g2, g3):
    def fn(xb, mb, g2b, g3b):
        h1 = xb + _nrm(mb, g2b)[0]
        return h1, _nrm(h1, g3b)[0]
    return _rowcall(fn, [x, mix], [g2, g3], [(D_MODEL, F32), (D_MODEL, BF16)], [], name="post_pre_norm")


def _tail(ff, h1, target, g4):
    def fn(ffb, h1b, tb, g):
        y, n, r = _nrm(ffb, g)
        e = h1b + y - tb
        loss = 0.5 * jnp.sum(jnp.sum(e * e, axis=-1, keepdims=True) * (1.0 / D_MODEL), axis=0, keepdims=True)
        dh2 = e * (1.0 / D_MODEL)
        dff, dg = _nrm_bwd(dh2, n, r, g)
        return dh2, dff, dg, jnp.broadcast_to(loss, (1, LANES))
    return _rowcall(fn, [ff, h1, target], [g4], [(D_MODEL, F32), (D_MODEL, BF16)], [D_MODEL, LANES], name="tail")


def _mid_bwd(du2, h1, dh2, mix, g2, g3):
    def fn(du2b, h1b, dh2b, mb, g2b, g3b):
        _, n3, r3 = _nrm(h1b, g3b)
        d3, dg3 = _nrm_bwd(du2b, n3, r3, g3b)
        dh1 = dh2b + d3
        _, n2, r2 = _nrm(mb, g2b)
        dmix, dg2 = _nrm_bwd(dh1, n2, r2, g2b)
        return dh1, dmix, dg3, dg2
    return _rowcall(fn, [du2, h1, dh2, mix], [g2, g3], [(D_MODEL, F32), (D_MODEL, BF16)], [D_MODEL, D_MODEL],
                    name="mid_bwd")


def _first_bwd(du, x, dh1, g1):
    def fn(dub, xb, dh1b, g):
        _, n, r = _nrm(xb, g)
        dx, dg = _nrm_bwd(dub, n, r, g)
        return dh1b + dx, dg
    return _rowcall(fn, [du, x, dh1], [g1], [(D_MODEL, F32)], [D_MODEL], name="first_bwd")


CONV_TILE = 256
CONV_ROWS = 256
PAD = 8


def _conv_taps(w):
    return [w[k:k + 1, :] for k in range(CONV_WIDTH)], w[CONV_WIDTH:CONV_WIDTH + 1, :]


def _conv_fwd(xbc, w8):
    s_dim, c_dim = xbc.shape
    n_steps = s_dim // CONV_ROWS

    def body(x_ref, w_ref, o_ref, xp):
        xp[0:PAD, :] = jnp.zeros((PAD, CONV_TILE), F32)
        xp[PAD:PAD + s_dim, :] = x_ref[...]
        taps, bias = _conv_taps(w_ref[...])

        def step(c, carry):
            base = pl.multiple_of(c * CONV_ROWS, CONV_ROWS)
            win = xp[pl.ds(base, CONV_ROWS + PAD), :]
            pre = bias + taps[3] * win[PAD:, :]
            for j in range(1, CONV_WIDTH):
                pre = pre + taps[3 - j] * pltpu.roll(win, j, axis=0)[PAD:, :]
            o_ref[pl.ds(base, CONV_ROWS), :] = pre * _sigmoid(pre)
            return carry

        lax.fori_loop(0, n_steps, step, 0)

    return pl.pallas_call(
        body,
        grid=(c_dim // CONV_TILE,),
        in_specs=[pl.BlockSpec((s_dim, CONV_TILE), lambda j: (0, j)), pl.BlockSpec((8, CONV_TILE), lambda j: (0, j))],
        out_specs=pl.BlockSpec((s_dim, CONV_TILE), lambda j: (0, j)),
        out_shape=jax.ShapeDtypeStruct((s_dim, c_dim), F32),
        scratch_shapes=[pltpu.VMEM((s_dim + 2 * PAD, CONV_TILE), F32)],
        compiler_params=_cparams("parallel"),
        name="conv_fwd",
    )(xbc, w8)


def _conv_bwd(xbc, w8, dxc):
    s_dim, c_dim = xbc.shape
    n_steps = s_dim // CONV_ROWS

    def body(x_ref, w_ref, d_ref, dx_ref, dw_ref, xp, dp):
        xp[0:PAD, :] = jnp.zeros((PAD, CONV_TILE), F32)
        xp[PAD:PAD + s_dim, :] = x_ref[...]
        dp[PAD + s_dim:, :] = jnp.zeros((PAD, CONV_TILE), F32)
        taps, bias = _conv_taps(w_ref[...])

        def step1(c, sums):
            base = pl.multiple_of(c * CONV_ROWS, CONV_ROWS)
            win = xp[pl.ds(base, CONV_ROWS + PAD), :]
            shifted = [win[PAD:, :]] + [pltpu.roll(win, j, axis=0)[PAD:, :] for j in range(1, CONV_WIDTH)]
            pre = bias
            for j in range(CONV_WIDTH):
                pre = pre + taps[3 - j] * shifted[j]
            sg = _sigmoid(pre)
            dpre = d_ref[pl.ds(base, CONV_ROWS), :] * (sg * (1.0 + pre * (1.0 - sg)))
            dp[pl.ds(base + PAD, CONV_ROWS), :] = dpre
            new = [sums[k] + jnp.sum(dpre * shifted[3 - k], axis=0, keepdims=True) for k in range(CONV_WIDTH)]
            new.append(sums[CONV_WIDTH] + jnp.sum(dpre, axis=0, keepdims=True))
            return tuple(new)

        zero = jnp.zeros((1, CONV_TILE), F32)
        sums = lax.fori_loop(0, n_steps, step1, (zero,) * (CONV_WIDTH + 1))
        dw_ref[...] = jnp.zeros((8, CONV_TILE), F32)
        for k in range(CONV_WIDTH + 1):
            dw_ref[k:k + 1, :] = sums[k]

        def step2(c, carry):
            base = pl.multiple_of(c * CONV_ROWS, CONV_ROWS)
            win = dp[pl.ds(base + PAD, CONV_ROWS + PAD), :]
            dx = taps[3] * win[:CONV_ROWS, :]
            for j in range(1, CONV_WIDTH):
                dx = dx + taps[3 - j] * pltpu.roll(win, CONV_ROWS + PAD - j, axis=0)[:CONV_ROWS, :]
            dx_ref[pl.ds(base, CONV_ROWS), :] = dx.astype(BF16)
            return carry

        lax.fori_loop(0, n_steps, step2, 0)

    col = lambda j: (0, j)
    return pl.pallas_call(
        body,
        grid=(c_dim // CONV_TILE,),
        in_specs=[pl.BlockSpec((s_dim, CONV_TILE), col), pl.BlockSpec((8, CONV_TILE), col),
                  pl.BlockSpec((s_dim, CONV_TILE), col)],
        out_specs=[pl.BlockSpec((s_dim, CONV_TILE), col), pl.BlockSpec((8, CONV_TILE), col)],
        out_shape=[jax.ShapeDtypeStruct((s_dim, c_dim), BF16), jax.ShapeDtypeStruct((8, c_dim), F32)],
        scratch_shapes=[pltpu.VMEM((s_dim + 2 * PAD, CONV_TILE), F32), pltpu.VMEM((s_dim + 2 * PAD, CONV_TILE), F32)],
        compiler_params=_cparams("parallel"),
        name="conv_bwd",
    )(xbc, w8, dxc)


def _perm_cols(a):
    parts = []
    for g in range(SSM_GROUPS):
        parts += [a[..., g * GROUP_X:(g + 1) * GROUP_X],
                  a[..., D_SSM + g * D_STATE:D_SSM + (g + 1) * D_STATE],
                  a[..., D_SSM + SSM_GROUPS * D_STATE + g * D_STATE:D_SSM + SSM_GROUPS * D_STATE + (g + 1) * D_STATE]]
    return jnp.concatenate(parts, axis=-1)


def _unperm_cols(a):
    xs = [a[..., g * GROUP_COLS:g * GROUP_COLS + GROUP_X] for g in range(SSM_GROUPS)]
    bs = [a[..., g * GROUP_COLS + GROUP_X:g * GROUP_COLS + GROUP_X + D_STATE] for g in range(SSM_GROUPS)]
    cs = [a[..., g * GROUP_COLS + GROUP_X + D_STATE:(g + 1) * GROUP_COLS] for g in range(SSM_GROUPS)]
    return jnp.concatenate(xs + bs + cs, axis=-1)


def _dt_to_groups(dt):
    s_dim = dt.shape[0]
    t = dt[:, :SSM_HEADS].reshape(s_dim, SSM_GROUPS, HEADS_PER_GROUP).transpose(1, 0, 2)
    return jnp.pad(t, ((0, 0), (0, 0), (0, LANES - HEADS_PER_GROUP)))


def _dt_from_groups(dtg):
    s_dim = dtg.shape[1]
    return dtg[:, :, :HEADS_PER_GROUP].transpose(1, 0, 2).reshape(s_dim, SSM_HEADS)


def _pack_ssd_params(dt_bias, a_log, d_skip):
    rows = jnp.stack([p.reshape(SSM_GROUPS, HEADS_PER_GROUP) for p in (dt_bias, a_log, d_skip)], axis=1)
    return jnp.pad(rows, ((0, 0), (0, 8 - 3), (0, LANES - HEADS_PER_GROUP)))


def _unpack_ssd_params(par):
    return tuple(par[:, k, :HEADS_PER_GROUP].reshape(SSM_HEADS) for k in range(3))


Q = SSD_CHUNK


def _split3(v):
    hi = v.astype(BF16)
    r1 = v - hi.astype(F32)
    mid = r1.astype(BF16)
    lo = (r1 - mid.astype(F32)).astype(BF16)
    return hi, mid, lo


def _dot_l01(t01, v):
    return sum(_dot(t01, p) for p in _split3(v))


def _dot_r01(v, e01):
    return sum(_dot(p, e01) for p in _split3(v))


def _ssd_consts():
    row = lax.broadcasted_iota(jnp.int32, (Q, Q), 0)
    col = lax.broadcasted_iota(jnp.int32, (Q, Q), 1)
    causal = row >= col
    tril = causal.astype(BF16)
    triu = (col >= row).astype(BF16)
    er = lax.broadcasted_iota(jnp.int32, (LANES, GROUP_X), 0)
    ec = lax.broadcasted_iota(jnp.int32, (LANES, GROUP_X), 1) // SSM_HEAD_DIM
    expand = (er == ec).astype(BF16)
    rr = lax.broadcasted_iota(jnp.int32, (GROUP_X, LANES), 0) // SSM_HEAD_DIM
    rc = lax.broadcasted_iota(jnp.int32, (GROUP_X, LANES), 1)
    reduce = (rr == rc).astype(BF16)
    lane_head = lax.broadcasted_iota(jnp.int32, (Q, GROUP_X), 1) // SSM_HEAD_DIM
    return causal, tril, triu, expand, reduce, lane_head


def _ssd_common(xc_ref, dt_ref, par_ref, consts):
    causal, tril, _, expand, _, _ = consts
    par = par_ref[...]
    bias, alog, dsk = par[0:1, :], par[1:2, :], par[2:3, :]
    a_neg = -jnp.exp(alog)
    dtr = dt_ref[...] + bias
    dt = _softplus(dtr)
    s = _dot_l01(tril, dt * a_neg)
    dt_x = _dot_r01(dt, expand)
    s_x = _dot_r01(s, expand)
    dsk_x = _dot_r01(jnp.broadcast_to(dsk, (8, LANES)), expand)[0:1, :]
    blk = xc_ref[...]
    x = blk[:, :GROUP_X]
    bm = blk[:, GROUP_X:GROUP_X + D_STATE].astype(BF16)
    cm = blk[:, GROUP_X + D_STATE:].astype(BF16)
    xdt = x * dt_x
    g = _dot(cm, bm, _NT)
    return dict(a_neg=a_neg, dtr=dtr, dt=dt, s=s, s_t=s.T, dt_x=dt_x, s_x=s_x, dsk_x=dsk_x, x=x, bm=bm, cm=cm,
                xdt=xdt, g=g)


def _decay(v, r, causal):
    diff = v["s"][:, r:r + 1] - v["s_t"][r:r + 1, :]
    return jnp.exp(jnp.where(causal, diff, -jnp.inf))


def _ssd_specs(n_chunks, rev):
    cidx = (lambda c: n_chunks - 1 - c) if rev else (lambda c: c)
    xc = pl.BlockSpec((Q, GROUP_COLS), lambda g, c: (cidx(c), g))
    gx = pl.BlockSpec((Q, GROUP_X), lambda g, c: (cidx(c), g))
    dt = pl.BlockSpec((None, Q, LANES), lambda g, c: (g, cidx(c), 0))
    par = pl.BlockSpec((None, 8, LANES), lambda g, c: (g, 0, 0))
    nw = pl.BlockSpec((1, GROUP_X), lambda g, c: (0, g))
    hs = pl.BlockSpec((None, None, D_STATE, GROUP_X), lambda g, c: (cidx(c), g, 0, 0))
    return xc, gx, dt, par, nw, hs


def _ssd_fwd(xc, z, dtg, par, nw):
    s_dim = xc.shape[0]
    n_chunks = s_dim // Q
    xc_s, gx_s, dt_s, par_s, nw_s, hs_s = _ssd_specs(n_chunks, False)

    def body(xc_ref, z_ref, dt_ref, par_ref, nw_ref, y_ref, ys_ref, hs_ref, ht):
        @pl.when(pl.program_id(1) == 0)
        def _():
            ht[...] = jnp.zeros_like(ht)

        consts = _ssd_consts()
        causal, lane_head = consts[0], consts[5]
        v = _ssd_common(xc_ref, dt_ref, par_ref, consts)
        xdt_b = v["xdt"].astype(BF16)
        yd = jnp.zeros((Q, GROUP_X), F32)
        for r in range(HEADS_PER_GROUP):
            m = (v["g"] * _decay(v, r, causal)).astype(BF16)
            yd = yd + _dot(m, jnp.where(lane_head == r, xdt_b, jnp.zeros_like(xdt_b)))
        h = ht[...]
        hs_ref[...] = h
        yo = jnp.exp(v["s_x"]) * _dot(v["cm"], h.astype(BF16))
        y = yd + yo + v["dsk_x"] * v["x"]
        s_last = v["s_x"][Q - 1:Q, :]
        snew = _dot(v["bm"], (v["xdt"] * jnp.exp(s_last - v["s_x"])).astype(BF16), _TN)
        ht[...] = jnp.exp(s_last) * h + snew
        zz = z_ref[...]
        yg = y * (zz * _sigmoid(zz))
        y_ref[...] = y
        ys_ref[...] = _nrm(yg, nw_ref[...])[0].astype(BF16)

    return pl.pallas_call(
        body,
        grid=(SSM_GROUPS, n_chunks),
        in_specs=[xc_s, gx_s, dt_s, par_s, nw_s],
        out_specs=[gx_s, gx_s, hs_s],
        out_shape=[jax.ShapeDtypeStruct((s_dim, D_SSM), F32), jax.ShapeDtypeStruct((s_dim, D_SSM), BF16),
                   jax.ShapeDtypeStruct((n_chunks, SSM_GROUPS, D_STATE, GROUP_X), F32)],
        scratch_shapes=[pltpu.VMEM((D_STATE, GROUP_X), F32)],
        compiler_params=_cparams("parallel", "arbitrary"),
        name="ssd_fwd",
    )(xc, z, dtg, par, nw)


def _ssd_bwd(xc, z, dtg, par, nw, y, hs, dymix):
    s_dim = xc.shape[0]
    n_chunks = s_dim // Q
    xc_s, gx_s, dt_s, par_s, nw_s, hs_s = _ssd_specs(n_chunks, True)

    def body(xc_ref, z_ref, dt_ref, par_ref, nw_ref, y_ref, hs_ref, dys_ref,
             dxc_ref, dz_ref, ddt_ref, dpar_ref, dnw_ref, dht):
        @pl.when(pl.program_id(1) == 0)
        def _():
            dht[...] = jnp.zeros_like(dht)
            dpar_ref[...] = jnp.zeros_like(dpar_ref)
            dnw_ref[...] = jnp.zeros_like(dnw_ref)

        consts = _ssd_consts()
        causal, _, triu, _, reduce, lane_head = consts
        v = _ssd_common(xc_ref, dt_ref, par_ref, consts)
        x, bm, cm, xdt, s_x = v["x"], v["bm"], v["cm"], v["xdt"], v["s_x"]
        h = hs_ref[...]
        hb = h.astype(BF16)
        es_x = jnp.exp(s_x)
        yo = es_x * _dot(cm, hb)
        s_last = s_x[Q - 1:Q, :]
        e_x = jnp.exp(s_last - s_x)
        es_last = jnp.exp(s_last)

        yv, zz, nw_v = y_ref[...], z_ref[...], nw_ref[...]
        sg = _sigmoid(zz)
        gz = zz * sg
        _, n, rstd = _nrm(yv * gz, nw_v)
        dout = dys_ref[...]
        dyg, dnw = _nrm_bwd(dout, n, rstd, nw_v)
        dnw_ref[...] += dnw
        dy = dyg * gz
        dz_ref[...] = (dyg * yv * (sg * (1.0 + zz * (1.0 - sg)))).astype(BF16)

        dyb = dy.astype(BF16)
        xdt_b = xdt.astype(BF16)
        dhp = dht[...]
        dhpb = dhp.astype(BF16)
        lane = lax.broadcasted_iota(jnp.int32, (Q, LANES), 1)
        sub = lax.broadcasted_iota(jnp.int32, (LANES, Q), 0)
        dxdt = jnp.zeros((Q, GROUP_X), F32)
        dg = jnp.zeros((Q, Q), F32)
        ds = jnp.zeros((Q, LANES), F32)
        ds_t = jnp.zeros((LANES, Q), F32)
        for r in range(HEADS_PER_GROUP):
            dec = _decay(v, r, causal)
            mf = v["g"] * dec
            dyr = jnp.where(lane_head == r, dyb, jnp.zeros_like(dyb))
            dm = _dot(dyr, xdt_b, _NT)
            dxdt = dxdt + _dot(mf.astype(BF16), dyr, _TN)
            dg = dg + dm * dec
            dd = dm * mf
            ds = ds + jnp.where(lane == r, jnp.sum(dd, axis=1, keepdims=True), 0.0)
            ds_t = ds_t + jnp.where(sub == r, jnp.sum(dd, axis=0, keepdims=True), 0.0)
        ds = ds - ds_t.T
        dgb = dg.astype(BF16)
        dwb = (es_x * dy).astype(BF16)
        dcm = _dot(dgb, bm) + _dot(dwb, hb, _NT)
        dh_prev = _dot(cm, dwb, _TN)
        zst = _dot(bm, dhpb)
        xe = xdt * e_x
        dxdt = dxdt + e_x * zst
        dee = xe * zst
        dbm = _dot(dgb, cm, _TN) + _dot(xe.astype(BF16), dhpb, _NT)
        v_last = jnp.sum(dee, axis=0, keepdims=True) + es_last * jnp.sum(dhp * h, axis=0, keepdims=True)
        row_x = lax.broadcasted_iota(jnp.int32, (Q, GROUP_X), 0)
        tx = dy * yo - dee + jnp.where(row_x == Q - 1, v_last, 0.0)
        ds = ds + _dot_r01(tx, reduce)
        ddta = _dot_l01(triu, ds)
        ddt = ddta * v["a_neg"] + _dot_r01(dxdt * x, reduce)
        dalog = jnp.sum(ddta * v["dt"], axis=0, keepdims=True) * v["a_neg"]
        draw = jnp.where(lane < HEADS_PER_GROUP, ddt * _sigmoid(v["dtr"]), 0.0)
        dbias = jnp.sum(draw, axis=0, keepdims=True)
        ddsk = _dot_r01(jnp.broadcast_to(jnp.sum(dy * x, axis=0, keepdims=True), (8, GROUP_X)), reduce)[0:1, :]
        dht[...] = es_last * dhp + dh_prev
        dxc_ref[:, :GROUP_X] = dxdt * v["dt_x"] + v["dsk_x"] * dy
        dxc_ref[:, GROUP_X:GROUP_X + D_STATE] = dbm
        dxc_ref[:, GROUP_X + D_STATE:] = dcm
        ddt_ref[...] = draw
        dpar_ref[0:1, :] += dbias
        dpar_ref[1:2, :] += dalog
        dpar_ref[2:3, :] += ddsk

    return pl.pallas_call(
        body,
        grid=(SSM_GROUPS, n_chunks),
        in_specs=[xc_s, gx_s, dt_s, par_s, nw_s, gx_s, hs_s, gx_s],
        out_specs=[xc_s, gx_s, dt_s, par_s, nw_s],
        out_shape=[jax.ShapeDtypeStruct((s_dim, SSM_GROUPS * GROUP_COLS), F32),
                   jax.ShapeDtypeStruct((s_dim, D_SSM), BF16),
                   jax.ShapeDtypeStruct((SSM_GROUPS, s_dim, LANES), F32),
                   jax.ShapeDtypeStruct((SSM_GROUPS, 8, LANES), F32),
                   jax.ShapeDtypeStruct((1, D_SSM), F32)],
        scratch_shapes=[pltpu.VMEM((D_STATE, GROUP_X), F32)],
        compiler_params=_cparams("parallel", "arbitrary"),
        name="ssd_bwd",
    )(xc, z, dtg, par, nw, y, hs, dymix)


ATT_SCALE = ATT_HEAD_DIM ** -0.5
NEG_INF = -jnp.inf


def _head(h):
    return slice(h * ATT_HEAD_DIM, (h + 1) * ATT_HEAD_DIM)


def _band_masks():
    qi = lax.broadcasted_iota(jnp.int32, (ATT_BLOCK, ATT_BLOCK), 0)
    kj = lax.broadcasted_iota(jnp.int32, (ATT_BLOCK, ATT_BLOCK), 1)
    return kj <= qi, kj >= qi


def _attn_fwd(qkv_v, d):
    rows = qkv_v.shape[0]
    nb = rows // ATT_BLOCK
    blk = (ATT_BLOCK, D_ATT)
    prev = lambda i: jnp.maximum(i - 1, 0)

    def body(q_ref, kc_ref, kp_ref, vc_ref, vp_ref, o_ref, lse_ref):
        own, before = _band_masks()
        before = before & (pl.program_id(1) > 0)
        lane = lax.broadcasted_iota(jnp.int32, (ATT_BLOCK, LANES), 1)
        lse_all = jnp.zeros((ATT_BLOCK, LANES), F32)
        for h in range(ATT_HEADS):
            q = q_ref[:, _head(h)]
            sc = jnp.where(own, _dot(q, kc_ref[:, _head(h)], _NT) * ATT_SCALE, NEG_INF)
            sp = jnp.where(before, _dot(q, kp_ref[:, _head(h)], _NT) * ATT_SCALE, NEG_INF)
            m = jnp.maximum(jnp.max(sc, axis=1, keepdims=True), jnp.max(sp, axis=1, keepdims=True))
            pc, pp = jnp.exp(sc - m), jnp.exp(sp - m)
            den = jnp.sum(pc, axis=1, keepdims=True) + jnp.sum(pp, axis=1, keepdims=True)
            o = _dot(pc.astype(BF16), vc_ref[:, _head(h)]) + _dot(pp.astype(BF16), vp_ref[:, _head(h)])
            o_ref[:, _head(h)] = o / den
            lse_all = jnp.where(lane == h, m + jnp.log(den), lse_all)
        lse_ref[...] = lse_all

    return pl.pallas_call(
        body,
        grid=(d, nb),
        in_specs=[pl.BlockSpec(blk, lambda r, i: (i, 3 * r)),
                  pl.BlockSpec(blk, lambda r, i: (i, 3 * r + 1)),
                  pl.BlockSpec(blk, lambda r, i: (prev(i), 3 * r + 1)),
                  pl.BlockSpec(blk, lambda r, i: (i, 3 * r + 2)),
                  pl.BlockSpec(blk, lambda r, i: (prev(i), 3 * r + 2))],
        out_specs=[pl.BlockSpec(blk, lambda r, i: (i, r)), pl.BlockSpec((ATT_BLOCK, LANES), lambda r, i: (i, r))],
        out_shape=[jax.ShapeDtypeStruct((rows, d * D_ATT), F32), jax.ShapeDtypeStruct((rows, d * LANES), F32)],
        compiler_params=_cparams("parallel", "arbitrary"),
        name=f"attn_fwd_d{d}",
    )(qkv_v, qkv_v, qkv_v, qkv_v, qkv_v)


def _attn_combine(os_, lses):
    def fn(o1, o2, o3, l1, l2, l3):
        m = jnp.maximum(jnp.maximum(l1, l2), l3)
        tot = m + jnp.log(jnp.exp(l1 - m) + jnp.exp(l2 - m) + jnp.exp(l3 - m))
        w1, w2, w3 = jnp.exp(l1 - tot), jnp.exp(l2 - tot), jnp.exp(l3 - tot)
        cols = []
        for h in range(ATT_HEADS):
            cols.append(w1[:, h:h + 1] * o1[:, _head(h)] + w2[:, h:h + 1] * o2[:, _head(h)]
                        + w3[:, h:h + 1] * o3[:, _head(h)])
        y = jnp.concatenate(cols, axis=1)
        return y, y, tot
    return _rowcall(fn, list(os_) + list(lses), [], [(D_ATT, BF16), (D_ATT, F32), (LANES, F32)], [],
                    name="attn_combine", tr=128)


def _attn_delta(dymix, y_att):
    def fn(dy, y):
        lane = lax.broadcasted_iota(jnp.int32, (dy.shape[0], LANES), 1)
        delta = jnp.zeros((dy.shape[0], LANES), F32)
        for h in range(ATT_HEADS):
            delta = jnp.where(lane == h, jnp.sum(dy[:, _head(h)] * y[:, _head(h)], axis=1, keepdims=True), delta)
        return dy, delta
    return _rowcall(fn, [dymix, y_att], [], [(D_ATT, BF16), (LANES, F32)], [], name="attn_delta",
                    row_cols=[(D_ATT, 1), None])


def _attn_bwd(qkv_v, dy_v, lse_v, delta_v, d):
    rows = qkv_v.shape[0]
    nb = rows // ATT_BLOCK
    blk = (ATT_BLOCK, D_ATT)
    sblk = (ATT_BLOCK, LANES)
    prev = lambda i: jnp.maximum(i - 1, 0)
    nxt = lambda i: jnp.minimum(i + 1, nb - 1)

    def body(qc_ref, qn_ref, kc_ref, kp_ref, vc_ref, vp_ref, dyc_ref, dyn_ref, lc_ref, ln_ref, dc_ref, dn_ref,
             dq_ref, dk_ref, dv_ref):
        i = pl.program_id(1)
        own, before = _band_masks()
        before_c = before & (i > 0)
        before_n = before & (i < nb - 1)
        lc, ln, dc, dn = lc_ref[...], ln_ref[...], dc_ref[...], dn_ref[...]
        for h in range(ATT_HEADS):
            hs = _head(h)
            q, qn, kc, kp, vc, vp = qc_ref[:, hs], qn_ref[:, hs], kc_ref[:, hs], kp_ref[:, hs], vc_ref[:, hs], vp_ref[:, hs]
            dy, dyn = dyc_ref[:, hs], dyn_ref[:, hs]
            lse, lse_n, dl, dl_n = lc[:, h:h + 1], ln[:, h:h + 1], dc[:, h:h + 1], dn[:, h:h + 1]
            pc = jnp.exp(jnp.where(own, _dot(q, kc, _NT) * ATT_SCALE - lse, NEG_INF))
            pp = jnp.exp(jnp.where(before_c, _dot(q, kp, _NT) * ATT_SCALE - lse, NEG_INF))
            pn = jnp.exp(jnp.where(before_n, _dot(qn, kc, _NT) * ATT_SCALE - lse_n, NEG_INF))
            dsc = (pc * (_dot(dy, vc, _NT) - dl)).astype(BF16)
            dsp = (pp * (_dot(dy, vp, _NT) - dl)).astype(BF16)
            dsn = (pn * (_dot(dyn, vc, _NT) - dl_n)).astype(BF16)
            dq_ref[:, hs] = (_dot(dsc, kc) + _dot(dsp, kp)) * ATT_SCALE
            dk_ref[:, hs] = (_dot(dsc, q, _TN) + _dot(dsn, qn, _TN)) * ATT_SCALE
            dv_ref[:, hs] = _dot(pc.astype(BF16), dy, _TN) + _dot(pn.astype(BF16), dyn, _TN)

    return pl.pallas_call(
        body,
        grid=(d, nb),
        in_specs=[pl.BlockSpec(blk, lambda r, i: (i, 3 * r)), pl.BlockSpec(blk, lambda r, i: (nxt(i), 3 * r)),
                  pl.BlockSpec(blk, lambda r, i: (i, 3 * r + 1)), pl.BlockSpec(blk, lambda r, i: (prev(i), 3 * r + 1)),
                  pl.BlockSpec(blk, lambda r, i: (i, 3 * r + 2)), pl.BlockSpec(blk, lambda r, i: (prev(i), 3 * r + 2)),
                  pl.BlockSpec(blk, lambda r, i: (i, r)), pl.BlockSpec(blk, lambda r, i: (nxt(i), r)),
                  pl.BlockSpec(sblk, lambda r, i: (i, r)), pl.BlockSpec(sblk, lambda r, i: (nxt(i), r)),
                  pl.BlockSpec(sblk, lambda r, i: (i, r)), pl.BlockSpec(sblk, lambda r, i: (nxt(i), r))],
        out_specs=[pl.BlockSpec(blk, lambda r, i: (i, r))] * 3,
        out_shape=[jax.ShapeDtypeStruct((rows, d * D_ATT), F32)] * 3,
        compiler_params=_cparams("parallel", "arbitrary"),
        name=f"attn_bwd_d{d}",
    )(qkv_v, qkv_v, qkv_v, qkv_v, qkv_v, qkv_v, dy_v, dy_v, lse_v, lse_v, delta_v, delta_v)


def _attn_sum(dqs, dks, dvs):
    def fn(*parts):
        return (jnp.concatenate([parts[0] + parts[1] + parts[2], parts[3] + parts[4] + parts[5],
                                 parts[6] + parts[7] + parts[8]], axis=1),)
    return _rowcall(fn, list(dqs) + list(dks) + list(dvs), [], [(3 * D_ATT, BF16)], [], name="attn_sum", tr=128)[0]


def _attention_fwd(qkv):
    s_dim = qkv.shape[0]
    os_, lses = [], []
    for d in DILATIONS:
        o, lse = _attn_fwd(qkv.reshape(s_dim // d, d * 3 * D_ATT), d)
        os_.append(o.reshape(s_dim, D_ATT))
        lses.append(lse.reshape(s_dim, LANES))
    return _attn_combine(os_, lses)


def _attention_bwd(qkv, dymix, y_att, lse):
    s_dim = qkv.shape[0]
    dy, delta = _attn_delta(dymix, y_att)
    dqs, dks, dvs = [], [], []
    for d in DILATIONS:
        dq, dk, dv = _attn_bwd(qkv.reshape(s_dim // d, d * 3 * D_ATT), dy.reshape(s_dim // d, d * D_ATT),
                               lse.reshape(s_dim // d, d * LANES), delta.reshape(s_dim // d, d * LANES), d)
        dqs.append(dq.reshape(s_dim, D_ATT))
        dks.append(dk.reshape(s_dim, D_ATT))
        dvs.append(dv.reshape(s_dim, D_ATT))
    return _attn_sum(dqs, dks, dvs)


def _adamw(w, g, m, v, name):
    def fn(wb, gb, mb, vb):
        m2 = ADAM_B1 * mb + (1.0 - ADAM_B1) * gb
        v2 = ADAM_B2 * vb + (1.0 - ADAM_B2) * (gb * gb)
        m_hat = m2 / (1.0 - ADAM_B1 ** ADAM_STEP)
        v_hat = v2 / (1.0 - ADAM_B2 ** ADAM_STEP)
        delta = -ADAM_LR * (m_hat / (jnp.sqrt(v_hat) + ADAM_EPS) + ADAM_WD * wb)
        return delta, m2, v2
    cols = w.shape[1]
    tr = 128 if w.shape[0] % 128 == 0 else w.shape[0]
    return _rowcall(fn, [w, g, m, v], [], [(cols, F32)] * 3, [], name=name, tr=tr)


ANY = pl.BlockSpec(memory_space=pl.ANY)


def _position():
    x, y, c = lax.axis_index("x"), lax.axis_index("y"), lax.axis_index("c")
    chips = [(1 - x, y), (x, 1 - y), (1 - x, 1 - y)]
    return x, y, c, chips


def _remote(src, dst, send_sem, recv_sem, device):
    return pltpu.make_async_remote_copy(src_ref=src, dst_ref=dst, send_sem=send_sem, recv_sem=recv_sem,
                                        device_id=device, device_id_type=MESH)


def _gather_shards(shards):
    n = len(shards)

    def body(*refs):
        ins, outs = refs[:n], refs[n:2 * n]
        send_sems, recv_sems = refs[2 * n:]
        x, y, c, chips = _position()
        sibling = (x, y, 1 - c)

        def half(a, j, cc):
            h = ins[a].shape[0] // 2
            return outs[a].at[j, pl.ds(cc * h, h), :]

        sent = []
        for a in range(n):
            h = ins[a].shape[0] // 2
            for j, chip in enumerate(chips):
                cp = _remote(ins[a].at[pl.ds(c * h, h), :], half(a, j, c), send_sems.at[6 * a + j],
                             recv_sems.at[6 * a + j], (chip[0], chip[1], c))
                cp.start()
                sent.append(cp)
        for a in range(n):
            for j in range(3):
                landed = half(a, j, c)
                _remote(landed, landed, send_sems.at[6 * a + j], recv_sems.at[6 * a + j], (x, y, c)).wait_recv()
                cp = _remote(landed, landed, send_sems.at[6 * a + 3 + j], recv_sems.at[6 * a + 3 + j], sibling)
                cp.start()
                sent.append(cp)
        for a in range(n):
            for j in range(3):
                handed = half(a, j, 1 - c)
                _remote(handed, handed, send_sems.at[6 * a + 3 + j], recv_sems.at[6 * a + 3 + j], (x, y, c)).wait_recv()
        for cp in sent:
            cp.wait_send()

    return pl.pallas_call(
        body,
        in_specs=[ANY] * n,
        out_specs=[ANY] * n,
        out_shape=[jax.ShapeDtypeStruct((3,) + s.shape, s.dtype) for s in shards],
        scratch_shapes=[pltpu.SemaphoreType.DMA((6 * n,)), pltpu.SemaphoreType.DMA((6 * n,))],
        name="gather_shards",
    )(*shards)


def _handshake(peers):
    barrier = pltpu.get_barrier_semaphore()
    for p in peers:
        pl.semaphore_signal(barrier, inc=1, device_id=p, device_id_type=MESH)
    pl.semaphore_wait(barrier, len(peers))


def _gather_shards_async(shards, collective_id, name):
    n = len(shards)
    srcs = [jax.new_ref(s, memory_space=pltpu.MemorySpace.HBM) for s in shards]
    dsts = [jax.empty_ref(jax.ShapeDtypeStruct((3,) + s.shape, s.dtype), memory_space=pltpu.MemorySpace.HBM)
            for s in shards]

    @pl.kernel(mesh=plsc.ScalarSubcoreMesh(axis_name="seq", num_cores=1), name=name,
               scratch_types=(pltpu.SemaphoreType.DMA((6 * n,)), pltpu.SemaphoreType.DMA((6 * n,))),
               compiler_params=pltpu.CompilerParams(collective_id=collective_id))
    def launch(send_sems, recv_sems):
        x, y, c, chips = _position()
        sibling = (x, y, 1 - c)
        _handshake([(chip[0], chip[1], c) for chip in chips] + [sibling])

        def half(a, j, cc):
            h = shards[a].shape[0] // 2
            return dsts[a].at[j, pl.ds(cc * h, h), :]

        sent = []
        for a in range(n):
            h = shards[a].shape[0] // 2
            for j, chip in enumerate(chips):
                cp = _remote(srcs[a].at[pl.ds(c * h, h), :], half(a, j, c), send_sems.at[6 * a + j],
                             recv_sems.at[6 * a + j], (chip[0], chip[1], c))
                cp.start()
                sent.append(cp)
        for a in range(n):
            for j in range(3):
                landed = half(a, j, c)
                _remote(landed, landed, send_sems.at[6 * a + j], recv_sems.at[6 * a + j], (x, y, c)).wait_recv()
                cp = _remote(landed, landed, send_sems.at[6 * a + 3 + j], recv_sems.at[6 * a + 3 + j], sibling)
                cp.start()
                sent.append(cp)
        for a in range(n):
            for j in range(3):
                handed = half(a, j, 1 - c)
                _remote(handed, handed, send_sems.at[6 * a + 3 + j], recv_sems.at[6 * a + 3 + j], (x, y, c)).wait_recv()
        for cp in sent:
            cp.wait_send()

    launch()
    return [d[...] for d in dsts]


def _by_chip(own, others):
    me = 2 * lax.axis_index("x") + lax.axis_index("y")
    rel = jnp.stack([own, others[1], others[0], others[2]])
    return jnp.stack([lax.dynamic_index_in_dim(rel, q ^ me, 0, keepdims=False) for q in range(N_CHIPS)])


def _exchange_sibling_halves(grads):
    n = len(grads)

    def body(*refs):
        ins, outs = refs[:n], refs[n:2 * n]
        send_sems, recv_sems = refs[2 * n:]
        x, y, c, _ = _position()
        copies = []
        for a in range(n):
            h = ins[a].shape[1] // 2
            cp = _remote(ins[a].at[:, pl.ds((1 - c) * h, h), :], outs[a], send_sems.at[a], recv_sems.at[a], (x, y, 1 - c))
            cp.start()
            copies.append(cp)
        for cp in copies:
            cp.wait()

    return pl.pallas_call(
        body,
        in_specs=[ANY] * n,
        out_specs=[ANY] * n,
        out_shape=[jax.ShapeDtypeStruct((g.shape[0], g.shape[1] // 2, g.shape[2]), g.dtype) for g in grads],
        scratch_shapes=[pltpu.SemaphoreType.DMA((n,)), pltpu.SemaphoreType.DMA((n,))],
        name="exchange_sibling_halves",
    )(*grads)


def _exchange_quarters(parts):
    n = len(parts)

    def body(*refs):
        ins, outs = refs[:n], refs[n:2 * n]
        send_sems, recv_sems = refs[2 * n:]
        x, y, c, chips = _position()
        copies = []
        for a in range(n):
            for j, chip in enumerate(chips):
                cp = _remote(ins[a].at[2 * chip[0] + chip[1]], outs[a].at[j], send_sems.at[3 * a + j],
                             recv_sems.at[3 * a + j], (chip[0], chip[1], c))
                cp.start()
                copies.append(cp)
        for cp in copies:
            cp.wait()

    return pl.pallas_call(
        body,
        in_specs=[ANY] * n,
        out_specs=[ANY] * n,
        out_shape=[jax.ShapeDtypeStruct((3,) + p.shape[1:], p.dtype) for p in parts],
        scratch_shapes=[pltpu.SemaphoreType.DMA((3 * n,)), pltpu.SemaphoreType.DMA((3 * n,))],
        name="exchange_quarters",
    )(*parts)


def _share_reduced_halves(halves):
    n = len(halves)

    def body(*refs):
        ins, outs = refs[:n], refs[n:2 * n]
        send_sems, recv_sems = refs[2 * n:]
        x, y, c, _ = _position()
        copies = []
        for a in range(n):
            cp = _remote(ins[a], outs[a], send_sems.at[a], recv_sems.at[a], (x, y, 1 - c))
            cp.start()
            copies.append(cp)
        for cp in copies:
            cp.wait()

    return pl.pallas_call(
        body,
        in_specs=[ANY] * n,
        out_specs=[ANY] * n,
        out_shape=[jax.ShapeDtypeStruct(p.shape, p.dtype) for p in halves],
        scratch_shapes=[pltpu.SemaphoreType.DMA((n,)), pltpu.SemaphoreType.DMA((n,))],
        name="share_reduced_halves",
    )(*halves)


def _add_sibling(grad, got, c_arr, name):
    nq, rows, cols = grad.shape
    h = rows // 2
    tr = 128
    nb = h // tr

    def body(c_ref, a_ref, b_ref, o_ref, ob_ref):
        total = a_ref[...] + b_ref[...]
        o_ref[...] = total
        ob_ref[...] = total.astype(BF16)

    out_spec = pl.BlockSpec((None, tr, cols), lambda q, i, c: (q, i, 0))
    return pl.pallas_call(
        body,
        grid_spec=pltpu.PrefetchScalarGridSpec(
            num_scalar_prefetch=1, grid=(nq, nb),
            in_specs=[pl.BlockSpec((None, tr, cols), lambda q, i, c: (q, c[0] * nb + i, 0)),
                      pl.BlockSpec((None, tr, cols), lambda q, i, c: (q, i, 0))],
            out_specs=[out_spec, out_spec]),
        out_shape=[jax.ShapeDtypeStruct((nq, h, cols), F32), jax.ShapeDtypeStruct((nq, h, cols), BF16)],
        compiler_params=_cparams("parallel", "parallel"),
        name=name,
    )(c_arr, grad, got)


def _add_chips(part, got, chip_arr, name):
    _, h, cols = part.shape
    tr = 128

    def body(q_ref, p_ref, g0_ref, g1_ref, g2_ref, o_ref):
        o_ref[...] = ((p_ref[...] + g0_ref[...].astype(F32)) + g1_ref[...].astype(F32)) + g2_ref[...].astype(F32)

    got_spec = lambda j: pl.BlockSpec((None, tr, cols), lambda i, q: (j, i, 0))
    return pl.pallas_call(
        body,
        grid_spec=pltpu.PrefetchScalarGridSpec(
            num_scalar_prefetch=1, grid=(h // tr,),
            in_specs=[pl.BlockSpec((None, tr, cols), lambda i, q: (q[0], i, 0)), got_spec(0), got_spec(1), got_spec(2)],
            out_specs=pl.BlockSpec((tr, cols), lambda i, q: (i, 0))),
        out_shape=jax.ShapeDtypeStruct((h, cols), F32),
        compiler_params=_cparams("parallel"),
        name=name,
    )(chip_arr, part, got, got, got)


def _reduce_scatter(grads, names):
    c_arr = lax.axis_index("c").astype(jnp.int32).reshape(1)
    chip_arr = (2 * lax.axis_index("x") + lax.axis_index("y")).astype(jnp.int32).reshape(1)
    from_sibling = _exchange_sibling_halves(grads)
    parts = [_add_sibling(g, s, c_arr, f"add_sibling_{nm}") for g, s, nm in zip(grads, from_sibling, names)]
    from_chips = _exchange_quarters([pb for _, pb in parts])
    halves = [_add_chips(p, f, chip_arr, f"add_chips_{nm}") for (p, _), f, nm in zip(parts, from_chips, names)]
    return list(zip(halves, _share_reduced_halves(halves)))


def _adamw_halves(w, mine, other, m, v, name):
    rows, cols = w.shape
    tr = 128
    nb = rows // 2 // tr
    c_arr = lax.axis_index("c").astype(jnp.int32).reshape(1)

    def body(c_ref, w_ref, a_ref, b_ref, m_ref, v_ref, g_out, d_out, m_out, v_out):
        is_mine = (pl.program_id(0) // nb) == c_ref[0]
        g = jnp.where(is_mine, a_ref[...], b_ref[...])
        wb, mb, vb = w_ref[...], m_ref[...], v_ref[...]
        m2 = ADAM_B1 * mb + (1.0 - ADAM_B1) * g
        v2 = ADAM_B2 * vb + (1.0 - ADAM_B2) * (g * g)
        m_hat = m2 / (1.0 - ADAM_B1 ** ADAM_STEP)
        v_hat = v2 / (1.0 - ADAM_B2 ** ADAM_STEP)
        g_out[...] = g
        d_out[...] = -ADAM_LR * (m_hat / (jnp.sqrt(v_hat) + ADAM_EPS) + ADAM_WD * wb)
        m_out[...] = m2
        v_out[...] = v2

    full = pl.BlockSpec((tr, cols), lambda i, c: (i, 0))
    half = pl.BlockSpec((tr, cols), lambda i, c: (i % nb, 0))
    return pl.pallas_call(
        body,
        grid_spec=pltpu.PrefetchScalarGridSpec(
            num_scalar_prefetch=1, grid=(rows // tr,),
            in_specs=[full, half, half, full, full], out_specs=[full] * 4),
        out_shape=[jax.ShapeDtypeStruct((rows, cols), F32)] * 4,
        compiler_params=_cparams("parallel"),
        name=name,
    )(c_arr, w, mine, other, m, v)


def _all_sum_small(v):
    n_dev = 8

    def body(v_ref, o_ref, gath, send_sems, recv_sems):
        x, y, c, _ = _position()
        me = 4 * x + 2 * y + c
        gath[me] = v_ref[...]
        copies = []
        for k in range(1, n_dev):
            peer = tuple(1 - p if (k >> s) & 1 else p for p, s in ((x, 2), (y, 1), (c, 0)))
            cp = _remote(v_ref, gath.at[me], send_sems.at[k - 1], recv_sems.at[k - 1], peer)
            cp.start()
            copies.append(cp)
        for cp in copies:
            cp.wait()
        acc = gath[0]
        for i in range(1, n_dev):
            acc = acc + gath[i]
        o_ref[...] = acc

    vm = pl.BlockSpec(memory_space=pltpu.VMEM)
    return pl.pallas_call(
        body,
        in_specs=[vm],
        out_specs=vm,
        out_shape=jax.ShapeDtypeStruct(v.shape, F32),
        scratch_shapes=[pltpu.VMEM((n_dev,) + v.shape, F32), pltpu.SemaphoreType.DMA((n_dev - 1,)),
                        pltpu.SemaphoreType.DMA((n_dev - 1,))],
        name="all_sum_small",
    )(v)


def _pack_rows(vectors):
    rows = []
    for v in vectors:
        flat = v.reshape(-1).astype(F32)
        rows.append(jnp.pad(flat, (0, (-flat.shape[0]) % LANES)).reshape(-1, LANES))
    out = jnp.concatenate(rows, axis=0)
    return jnp.pad(out, ((0, (-out.shape[0]) % 8), (0, 0)))


def _unpack_rows(packed, shapes):
    outs, r = [], 0
    for shp in shapes:
        size = math.prod(shp)
        nr = -(-size // LANES)
        outs.append(packed[r:r + nr].reshape(-1)[:size].reshape(shp))
        r += nr
    return outs


def _relu_sq(acc):
    r = jnp.maximum(acc, 0.0)
    return r, r * r


def _relu_sq_bwd(acc, r):
    return (acc * (2.0 * r.astype(F32)),)


def kernel(x, norm_mix_pre, w_in, conv_w, conv_b, dt_bias, a_log, d_skip, ssm_norm_w, w_out, norm_mix_post, norm_mlp_pre, w_up, w_down, norm_mlp_post, loss_target, m_norm_mix_pre, m_w_in, m_conv_w, m_conv_b, m_dt_bias, m_a_log, m_d_skip, m_ssm_norm_w, m_w_out, m_norm_mix_post, m_norm_mlp_pre, m_w_up, m_w_down, m_norm_mlp_post, v_norm_mix_pre, v_w_in, v_conv_w, v_conv_b, v_dt_bias, v_a_log, v_d_skip, v_ssm_norm_w, v_w_out, v_norm_mix_post, v_norm_mlp_pre, v_w_up, v_w_down, v_norm_mlp_post):
    s_dim = x.shape[1]
    xs, target = x[0], loss_target[0]
    chip = 2 * lax.axis_index("x") + lax.axis_index("y")

    own = [w_in[0].astype(BF16), w_out[0].astype(BF16), w_up[0].astype(BF16), w_down[0].astype(BF16)]
    fetched = list(_gather_shards(own[:1])) + _gather_shards_async(own[1:], 1, "gather_rest")
    g_in, g_out, g_up, g_down = [_by_chip(o, f) for o, f in zip(own, fetched)]
    w_in_full = g_in.transpose(1, 0, 2).reshape(D_MODEL, D_IN_PROJ)
    w_z = w_in_full[:, :D_SSM]
    w_xbc = _perm_cols(w_in_full[:, D_SSM:D_SSM + D_XBC])
    w_dt = jnp.pad(w_in_full[:, D_SSM + D_XBC:D_SSM + D_XBC + SSM_HEADS], ((0, 0), (0, LANES - SSM_HEADS)))
    w_qkv = w_in_full[:, D_SSM + D_XBC + SSM_HEADS:]
    w_out_full = g_out.reshape(D_MIX, D_MODEL)
    w_up_full = g_up.transpose(1, 0, 2).reshape(D_MODEL, D_FF)
    w_down_full = g_down.reshape(D_FF, D_MODEL)

    conv_cols = D_XBC // N_CHIPS
    conv_placed = lax.dynamic_update_slice(jnp.zeros((8, D_XBC), F32), 0.5 * conv_w[0], (0, chip * conv_cols))
    conv_full = _all_sum_small(conv_placed.reshape(-1, LANES)).reshape(8, D_XBC)
    w8 = _perm_cols(conv_full.at[CONV_WIDTH].set(conv_b[0]))

    u = _pre_norm(xs, norm_mix_pre)
    z = _matmul([(u, w_z, TK)], "nn", [F32], name="proj_z")
    xbc = _matmul([(u, w_xbc, TK)], "nn", [F32], name="proj_xbc")
    dt_raw = _matmul([(u, w_dt, TK)], "nn", [F32], name="proj_dt")
    qkv = _matmul([(u, w_qkv, TK)], "nn", [BF16], name="proj_qkv")
    xc = _conv_fwd(xbc, w8)
    dtg = _dt_to_groups(dt_raw)
    par = _pack_ssd_params(dt_bias[0], a_log[0], d_skip[0])
    y, y_ssm, states = _ssd_fwd(xc, z, dtg, par, ssm_norm_w)
    y_att, y_att_f32, lse = _attention_fwd(qkv)
    y_mix = jnp.concatenate([y_ssm, y_att], axis=1)
    mix = _matmul([(y_mix, w_out_full, TK)], "nn", [F32], name="out_proj")
    h1, u2 = _post_pre_norm(xs, mix, norm_mix_post, norm_mlp_pre)
    hid, act = _matmul([(u2, w_up_full, TK)], "nn", [BF16, BF16], name="mlp_up", epilogue=_relu_sq)
    ff = _matmul([(act, w_down_full, TK)], "nn", [F32], name="mlp_down")
    dh2, dff, d_g4, loss_part = _tail(ff, h1, target, norm_mlp_post)

    dhid = _matmul([(dff, w_down_full, TK)], "nt", [BF16], name="mlp_down_dx", epilogue=_relu_sq_bwd, extras=[hid])
    dw_down = _matmul([(act, dff, TK)], "tn", [F32], name="mlp_down_dw")
    dw_up = _matmul([(u2, dhid, TK)], "tn", [F32], name="mlp_up_dw")
    du2 = _matmul([(dhid, w_up_full, TK)], "nt", [F32], name="mlp_up_dx")
    dh1, dmix, d_g3, d_g2 = _mid_bwd(du2, h1, dh2, mix, norm_mix_post, norm_mlp_pre)
    dymix = _matmul([(dmix, w_out_full, TK)], "nt", [F32], name="out_proj_dx")
    dw_out = _matmul([(y_mix, dmix, TK)], "tn", [F32], name="out_proj_dw")
    dqkv = _attention_bwd(qkv, dymix, y_att_f32, lse)
    dxc, dz, ddtg, dpar, d_nw = _ssd_bwd(xc, z, dtg, par, ssm_norm_w, y, states, dymix)
    dxbc, dw8 = _conv_bwd(xbc, w8, dxc)
    ddt = jnp.pad(_dt_from_groups(ddtg), ((0, 0), (0, LANES - SSM_HEADS))).astype(BF16)
    du = _matmul([(dz, w_z, TK_MULTI), (dxbc, w_xbc, TK_MULTI), (dqkv, w_qkv, TK_MULTI), (ddt, w_dt, LANES)], "nt", [F32],
                 name="proj_dx")
    dw_z = _matmul([(u, dz, TK)], "tn", [F32], name="proj_z_dw")
    dw_xbc = _matmul([(u, dxbc, TK)], "tn", [F32], name="proj_xbc_dw")
    dw_dt = _matmul([(u, ddt, TK)], "tn", [F32], name="proj_dt_dw")
    dw_qkv = _matmul([(u, dqkv, TK)], "tn", [F32], name="proj_qkv_dw")
    grad_x, d_g1 = _first_bwd(du, xs, dh1, norm_mix_pre)

    dw_in = jnp.concatenate([dw_z, _unperm_cols(dw_xbc), dw_dt[:, :SSM_HEADS], dw_qkv], axis=1)
    big = _reduce_scatter(
        [dw_in.reshape(D_MODEL, N_CHIPS, W_IN_SHARD).transpose(1, 0, 2),
         dw_out.reshape(N_CHIPS, D_MIX // N_CHIPS, D_MODEL),
         dw_up.reshape(D_MODEL, N_CHIPS, D_FF // N_CHIPS).transpose(1, 0, 2),
         dw_down.reshape(N_CHIPS, D_FF // N_CHIPS, D_MODEL)],
        ["w_in", "w_out", "w_up", "w_down"])
    dconv = _unperm_cols(dw8)
    d_bias, d_alog, d_dskip = _unpack_ssd_params(dpar)
    small_shapes = [(1, D_MODEL), (CONV_WIDTH, D_XBC), (1, D_XBC), (1, SSM_HEADS), (1, SSM_HEADS), (1, SSM_HEADS),
                    (1, D_SSM), (1, D_MODEL), (1, D_MODEL), (1, D_MODEL), (1, LANES)]
    summed = _unpack_rows(
        _all_sum_small(_pack_rows([d_g1, dconv[:CONV_WIDTH], dconv[CONV_WIDTH:CONV_WIDTH + 1], d_bias, d_alog,
                                   d_dskip, d_nw, d_g2, d_g3, d_g4, loss_part])), small_shapes)
    (g_g1, g_conv_full, g_conv_b, g_bias, g_alog, g_dskip, g_nw, g_g2, g_g3, g_g4, loss_row) = summed
    loss = loss_row[0, 0]
    g_conv_w = lax.dynamic_slice(g_conv_full, (0, chip * conv_cols), (CONV_WIDTH, conv_cols))[None]

    grads = {"norm_mix_pre": g_g1, "conv_w": g_conv_w, "conv_b": g_conv_b, "dt_bias": g_bias,
             "a_log": g_alog, "d_skip": g_dskip, "ssm_norm_w": g_nw, "norm_mix_post": g_g2,
             "norm_mlp_pre": g_g3, "norm_mlp_post": g_g4}
    weights = {"norm_mix_pre": (norm_mix_pre, m_norm_mix_pre, v_norm_mix_pre), "w_in": (w_in, m_w_in, v_w_in),
               "conv_w": (conv_w, m_conv_w, v_conv_w), "conv_b": (conv_b, m_conv_b, v_conv_b),
               "dt_bias": (dt_bias, m_dt_bias, v_dt_bias), "a_log": (a_log, m_a_log, v_a_log),
               "d_skip": (d_skip, m_d_skip, v_d_skip), "ssm_norm_w": (ssm_norm_w, m_ssm_norm_w, v_ssm_norm_w),
               "w_out": (w_out, m_w_out, v_w_out), "norm_mix_post": (norm_mix_post, m_norm_mix_post, v_norm_mix_post),
               "norm_mlp_pre": (norm_mlp_pre, m_norm_mlp_pre, v_norm_mlp_pre), "w_up": (w_up, m_w_up, v_w_up),
               "w_down": (w_down, m_w_down, v_w_down),
               "norm_mlp_post": (norm_mlp_post, m_norm_mlp_post, v_norm_mlp_post)}
    order = list(weights)
    big_names = ("w_in", "w_out", "w_up", "w_down")
    small_names = [n for n in order if n not in big_names]
    delta, new_m, new_v = {}, {}, {}
    for n, (mine, other) in zip(big_names, big):
        w, m, v = weights[n]
        g_, d_, m_, v_ = _adamw_halves(w[0], mine, other, m[0], v[0], f"adamw_{n}")
        grads[n], delta[n], new_m[n], new_v[n] = g_[None], d_[None], m_[None], v_[None]
    small_w_shapes = [weights[n][0].shape for n in small_names]
    packed = [_pack_rows([weights[n][k] for n in small_names]) for k in range(3)]
    packed_g = _pack_rows([grads[n].reshape(weights[n][0].shape) for n in small_names])
    sd, sm, sv = _adamw(packed[0], packed_g, packed[1], packed[2], "adamw_small")
    for k, n in enumerate(small_names):
        grads[n] = grads[n].reshape(weights[n][0].shape)
    for res, pk in ((delta, sd), (new_m, sm), (new_v, sv)):
        for n, val in zip(small_names, _unpack_rows(pk, small_w_shapes)):
            res[n] = val

    return (loss, grad_x[None], *[grads[n] for n in order], *[delta[n] for n in order],
            *[new_m[n] for n in order], *[new_v[n] for n in order])
```

```python
import functools
import math

import numpy as np
import jax
import jax.numpy as jnp
from jax import lax
from jax.experimental import pallas as pl
from jax.experimental.pallas import tpu as pltpu
from jax.experimental.pallas import tpu_sc as plsc

F32 = jnp.float32
BF16 = jnp.bfloat16

D_MODEL = 2048
SSM_HEAD_DIM = 64
SSM_GROUPS = 8
HEADS_PER_GROUP = 4
SSM_HEADS = SSM_GROUPS * HEADS_PER_GROUP
D_SSM = SSM_HEADS * SSM_HEAD_DIM
D_STATE = 128
CONV_WIDTH = 4
SSD_CHUNK = 128
D_XBC = D_SSM + 2 * SSM_GROUPS * D_STATE
GROUP_X = HEADS_PER_GROUP * SSM_HEAD_DIM
GROUP_COLS = GROUP_X + 2 * D_STATE
ATT_HEAD_DIM = 128
ATT_HEADS = 16
D_ATT = ATT_HEADS * ATT_HEAD_DIM
DILATIONS = (1, 4, 16)
ATT_BLOCK = 128
D_MIX = D_SSM + D_ATT
D_IN_PROJ = D_SSM + D_XBC + SSM_HEADS + 3 * D_ATT
D_FF = 4 * D_MODEL
EPS = 1e-6
N_CHIPS = 4
W_IN_SHARD = D_IN_PROJ // N_CHIPS

ADAM_LR = 0.001
ADAM_B1 = 0.9
ADAM_B2 = 0.999
ADAM_EPS = 1e-08
ADAM_WD = 0.01
ADAM_STEP = 10

LANES = 128
VMEM_LIMIT = 48 * 1024 * 1024
MESH = pl.DeviceIdType.MESH

_NN = (((1,), (0,)), ((), ()))
_NT = (((1,), (1,)), ((), ()))
_TN = (((0,), (0,)), ((), ()))


def _dot(a, b, dims=_NN):
    return lax.dot_general(a, b, dims, preferred_element_type=F32)


def _cparams(*sem):
    return pltpu.CompilerParams(dimension_semantics=sem, vmem_limit_bytes=VMEM_LIMIT)


TK = 2048
TK_MULTI = 1024


def _matmul(pairs, mode, out_dtypes, *, name, tm=1024, tn=1024, epilogue=None, extras=(), deps=()):
    a0, b0, _ = pairs[0]
    m_dim = a0.shape[1] if mode == "tn" else a0.shape[0]
    n_dim = b0.shape[0] if mode == "nt" else b0.shape[1]
    tm, tn = min(tm, m_dim), min(tn, n_dim)
    nks, offs = [], []
    for a, _, tk in pairs:
        k_dim = a.shape[0] if mode == "tn" else a.shape[1]
        assert k_dim % tk == 0, (name, k_dim, tk)
        offs.append(sum(nks))
        nks.append(k_dim // tk)
    nk_total = sum(nks)
    assert m_dim % tm == 0 and n_dim % tn == 0, (name, m_dim, n_dim)
    dims = {"nn": _NN, "nt": _NT, "tn": _TN}[mode]
    n_pairs, n_extra, n_out = len(pairs), len(extras), len(out_dtypes)

    in_specs, operands = [], []
    for (a, b, tk), off, nk in zip(pairs, offs, nks):
        def kidx(k, off=off, nk=nk):
            return k if n_pairs == 1 else jnp.clip(k - off, 0, nk - 1)
        if mode == "tn":
            in_specs.append(pl.BlockSpec((tk, tm), lambda m, n, k, f=kidx: (f(k), m)))
        else:
            in_specs.append(pl.BlockSpec((tm, tk), lambda m, n, k, f=kidx: (m, f(k))))
        if mode == "nt":
            in_specs.append(pl.BlockSpec((tn, tk), lambda m, n, k, f=kidx: (n, f(k))))
        else:
            in_specs.append(pl.BlockSpec((tk, tn), lambda m, n, k, f=kidx: (f(k), n)))
        operands += [a, b]
    for e in extras:
        in_specs.append(pl.BlockSpec((tm, tn), lambda m, n, k: (m, n)))
        operands.append(e)
    in_specs += [pl.BlockSpec(memory_space=pl.ANY)] * len(deps)
    operands += list(deps)
    first_out = 2 * n_pairs + n_extra + len(deps)

    def body(*refs):
        ab = refs[:2 * n_pairs]
        e_refs = refs[2 * n_pairs:2 * n_pairs + n_extra]
        o_refs = refs[first_out:first_out + n_out]

        def finish(total):
            vals = (total,) if epilogue is None else epilogue(total, *[e[...] for e in e_refs])
            for o_ref, v in zip(o_refs, vals):
                o_ref[...] = v.astype(o_ref.dtype)

        if nk_total == 1:
            finish(_dot(ab[0][...], ab[1][...], dims))
            return
        acc = refs[-1]
        k = pl.program_id(2)

        @pl.when(k == 0)
        def _():
            acc[...] = jnp.zeros_like(acc)

        for i in range(n_pairs):
            def accumulate(i=i):
                acc[...] += _dot(ab[2 * i][...], ab[2 * i + 1][...], dims)
            if n_pairs == 1:
                accumulate()
            else:
                pl.when((k >= offs[i]) & (k < offs[i] + nks[i]))(accumulate)

        @pl.when(k == nk_total - 1)
        def _():
            finish(acc[...])

    outs = pl.pallas_call(
        body,
        grid=(m_dim // tm, n_dim // tn, nk_total),
        in_specs=in_specs,
        out_specs=[pl.BlockSpec((tm, tn), lambda m, n, k: (m, n)) for _ in out_dtypes],
        out_shape=[jax.ShapeDtypeStruct((m_dim, n_dim), dt) for dt in out_dtypes],
        scratch_shapes=[pltpu.VMEM((tm, tn), F32)] if nk_total > 1 else [],
        compiler_params=_cparams("parallel", "parallel", "arbitrary"),
        name=name,
    )(*operands)
    return outs[0] if n_out == 1 else outs


def _rowcall(fn, rows, vecs, row_outs, acc_widths, *, name, tr=256, row_cols=None, deps=()):
    s_dim = rows[0].shape[0]
    assert s_dim % tr == 0
    row_cols = row_cols or [None] * len(rows)
    n_r, n_v, n_ro, n_acc = len(rows), len(vecs), len(row_outs), len(acc_widths)
    in_specs = []
    for r, rc in zip(rows, row_cols):
        if rc is None:
            in_specs.append(pl.BlockSpec((tr, r.shape[1]), lambda i: (i, 0)))
        else:
            in_specs.append(pl.BlockSpec((tr, rc[0]), lambda i, c=rc[1]: (i, c)))
    for v in vecs:
        in_specs.append(pl.BlockSpec(v.shape, lambda i, nd=v.ndim: (0,) * nd))
    in_specs += [pl.BlockSpec(memory_space=pl.ANY)] * len(deps)
    n_d = len(deps)

    def body(*refs):
        ins = [r[...] for r in refs[:n_r + n_v]]
        ro = refs[n_r + n_v + n_d:n_r + n_v + n_d + n_ro]
        ao = refs[n_r + n_v + n_d + n_ro:]
        outs = fn(*ins)
        for ref, v in zip(ro, outs[:n_ro]):
            ref[...] = v.astype(ref.dtype)
        if n_acc:
            @pl.when(pl.program_id(0) == 0)
            def _():
                for ref in ao:
                    ref[...] = jnp.zeros_like(ref)
            for ref, v in zip(ao, outs[n_ro:]):
                ref[...] += v

    outs = pl.pallas_call(
        body,
        grid=(s_dim // tr,),
        in_specs=in_specs,
        out_specs=[pl.BlockSpec((tr, w), lambda i: (i, 0)) for w, _ in row_outs]
        + [pl.BlockSpec((1, w), lambda i: (0, 0)) for w in acc_widths],
        out_shape=[jax.ShapeDtypeStruct((s_dim, w), dt) for w, dt in row_outs]
        + [jax.ShapeDtypeStruct((1, w), F32) for w in acc_widths],
        compiler_params=_cparams("arbitrary"),
        name=name,
    )(*rows, *vecs, *deps)
    return outs


def _nrm(x, g):
    r = lax.rsqrt(jnp.mean(x * x, axis=-1, keepdims=True) + EPS)
    n = x * r
    return n * g, n, r


def _nrm_bwd(dy, n, r, g):
    dn = dy * g
    dx = r * (dn - n * jnp.mean(dn * n, axis=-1, keepdims=True))
    return dx, jnp.sum(dy * n, axis=0, keepdims=True)


def _sigmoid(x):
    return 1.0 / (1.0 + jnp.exp(-x))


def _softplus(x):
    return jnp.maximum(x, 0.0) + jnp.log(1.0 + jnp.exp(-jnp.abs(x)))


def _pre_norm(x, g1):
    def fn(xb, g):
        return (_nrm(xb, g)[0],)
    return _rowcall(fn, [x], [g1], [(D_MODEL, BF16)], [], name="pre_norm")[0]


def _post_pre_norm(x, mix, g2, g3):
    def fn(xb, mb, g2b, g3b):
        h1 = xb + _nrm(mb, g2b)[0]
        return h1, _nrm(h1, g3b)[0]
    return _rowcall(fn, [x, mix], [g2, g3], [(D_MODEL, F32), (D_MODEL, BF16)], [], name="post_pre_norm")


def _tail(ff, h1, target, g4):
    def fn(ffb, h1b, tb, g):
        y, n, r = _nrm(ffb, g)
        e = h1b + y - tb
        loss = 0.5 * jnp.sum(jnp.sum(e * e, axis=-1, keepdims=True) * (1.0 / D_MODEL), axis=0, keepdims=True)
        dh2 = e * (1.0 / D_MODEL)
        dff, dg = _nrm_bwd(dh2, n, r, g)
        return dh2, dff, dg, jnp.broadcast_to(loss, (1, LANES))
    return _rowcall(fn, [ff, h1, target], [g4], [(D_MODEL, F32), (D_MODEL, BF16)], [D_MODEL, LANES], name="tail")


def _mid_bwd(du2, h1, dh2, mix, g2, g3, deps=()):
    def fn(du2b, h1b, dh2b, mb, g2b, g3b):
        _, n3, r3 = _nrm(h1b, g3b)
        d3, dg3 = _nrm_bwd(du2b, n3, r3, g3b)
        dh1 = dh2b + d3
        _, n2, r2 = _nrm(mb, g2b)
        dmix, dg2 = _nrm_bwd(dh1, n2, r2, g2b)
        return dh1, dmix, dg3, dg2
    return _rowcall(fn, [du2, h1, dh2, mix], [g2, g3], [(D_MODEL, F32), (D_MODEL, BF16)], [D_MODEL, D_MODEL],
                    name="mid_bwd", deps=deps)


def _first_bwd(du, x, dh1, g1):
    def fn(dub, xb, dh1b, g):
        _, n, r = _nrm(xb, g)
        dx, dg = _nrm_bwd(dub, n, r, g)
        return dh1b + dx, dg
    return _rowcall(fn, [du, x, dh1], [g1], [(D_MODEL, F32)], [D_MODEL], name="first_bwd")


CONV_TILE = 256
CONV_ROWS = 256
PAD = 8


def _conv_taps(w):
    return [w[k:k + 1, :] for k in range(CONV_WIDTH)], w[CONV_WIDTH:CONV_WIDTH + 1, :]


def _conv_fwd(xbc, w8):
    s_dim, c_dim = xbc.shape
    n_steps = s_dim // CONV_ROWS

    def body(x_ref, w_ref, o_ref, xp):
        xp[0:PAD, :] = jnp.zeros((PAD, CONV_TILE), F32)
        xp[PAD:PAD + s_dim, :] = x_ref[...]
        taps, bias = _conv_taps(w_ref[...])

        def step(c, carry):
            base = pl.multiple_of(c * CONV_ROWS, CONV_ROWS)
            win = xp[pl.ds(base, CONV_ROWS + PAD), :]
            pre = bias + taps[3] * win[PAD:, :]
            for j in range(1, CONV_WIDTH):
                pre = pre + taps[3 - j] * pltpu.roll(win, j, axis=0)[PAD:, :]
            o_ref[pl.ds(base, CONV_ROWS), :] = pre * _sigmoid(pre)
            return carry

        lax.fori_loop(0, n_steps, step, 0)

    return pl.pallas_call(
        body,
        grid=(c_dim // CONV_TILE,),
        in_specs=[pl.BlockSpec((s_dim, CONV_TILE), lambda j: (0, j)), pl.BlockSpec((8, CONV_TILE), lambda j: (0, j))],
        out_specs=pl.BlockSpec((s_dim, CONV_TILE), lambda j: (0, j)),
        out_shape=jax.ShapeDtypeStruct((s_dim, c_dim), F32),
        scratch_shapes=[pltpu.VMEM((s_dim + 2 * PAD, CONV_TILE), F32)],
        compiler_params=_cparams("parallel"),
        name="conv_fwd",
    )(xbc, w8)


def _conv_bwd(xbc, w8, dxc):
    s_dim, c_dim = xbc.shape
    n_steps = s_dim // CONV_ROWS

    def body(x_ref, w_ref, d_ref, dx_ref, dw_ref, xp, dp):
        xp[0:PAD, :] = jnp.zeros((PAD, CONV_TILE), F32)
        xp[PAD:PAD + s_dim, :] = x_ref[...]
        dp[PAD + s_dim:, :] = jnp.zeros((PAD, CONV_TILE), F32)
        taps, bias = _conv_taps(w_ref[...])

        def step1(c, sums):
            base = pl.multiple_of(c * CONV_ROWS, CONV_ROWS)
            win = xp[pl.ds(base, CONV_ROWS + PAD), :]
            shifted = [win[PAD:, :]] + [pltpu.roll(win, j, axis=0)[PAD:, :] for j in range(1, CONV_WIDTH)]
            pre = bias
            for j in range(CONV_WIDTH):
                pre = pre + taps[3 - j] * shifted[j]
            sg = _sigmoid(pre)
            dpre = d_ref[pl.ds(base, CONV_ROWS), :] * (sg * (1.0 + pre * (1.0 - sg)))
            dp[pl.ds(base + PAD, CONV_ROWS), :] = dpre
            new = [sums[k] + jnp.sum(dpre * shifted[3 - k], axis=0, keepdims=True) for k in range(CONV_WIDTH)]
            new.append(sums[CONV_WIDTH] + jnp.sum(dpre, axis=0, keepdims=True))
            return tuple(new)

        zero = jnp.zeros((1, CONV_TILE), F32)
        sums = lax.fori_loop(0, n_steps, step1, (zero,) * (CONV_WIDTH + 1))
        dw_ref[...] = jnp.zeros((8, CONV_TILE), F32)
        for k in range(CONV_WIDTH + 1):
            dw_ref[k:k + 1, :] = sums[k]

        def step2(c, carry):
            base = pl.multiple_of(c * CONV_ROWS, CONV_ROWS)
            win = dp[pl.ds(base + PAD, CONV_ROWS + PAD), :]
            dx = taps[3] * win[:CONV_ROWS, :]
            for j in range(1, CONV_WIDTH):
                dx = dx + taps[3 - j] * pltpu.roll(win, CONV_ROWS + PAD - j, axis=0)[:CONV_ROWS, :]
            dx_ref[pl.ds(base, CONV_ROWS), :] = dx.astype(BF16)
            return carry

        lax.fori_loop(0, n_steps, step2, 0)

    col = lambda j: (0, j)
    return pl.pallas_call(
        body,
        grid=(c_dim // CONV_TILE,),
        in_specs=[pl.BlockSpec((s_dim, CONV_TILE), col), pl.BlockSpec((8, CONV_TILE), col),
                  pl.BlockSpec((s_dim, CONV_TILE), col)],
        out_specs=[pl.BlockSpec((s_dim, CONV_TILE), col), pl.BlockSpec((8, CONV_TILE), col)],
        out_shape=[jax.ShapeDtypeStruct((s_dim, c_dim), BF16), jax.ShapeDtypeStruct((8, c_dim), F32)],
        scratch_shapes=[pltpu.VMEM((s_dim + 2 * PAD, CONV_TILE), F32), pltpu.VMEM((s_dim + 2 * PAD, CONV_TILE), F32)],
        compiler_params=_cparams("parallel"),
        name="conv_bwd",
    )(xbc, w8, dxc)


def _perm_cols(a):
    parts = []
    for g in range(SSM_GROUPS):
        parts += [a[..., g * GROUP_X:(g + 1) * GROUP_X],
                  a[..., D_SSM + g * D_STATE:D_SSM + (g + 1) * D_STATE],
                  a[..., D_SSM + SSM_GROUPS * D_STATE + g * D_STATE:D_SSM + SSM_GROUPS * D_STATE + (g + 1) * D_STATE]]
    return jnp.concatenate(parts, axis=-1)


def _unperm_cols(a):
    xs = [a[..., g * GROUP_COLS:g * GROUP_COLS + GROUP_X] for g in range(SSM_GROUPS)]
    bs = [a[..., g * GROUP_COLS + GROUP_X:g * GROUP_COLS + GROUP_X + D_STATE] for g in range(SSM_GROUPS)]
    cs = [a[..., g * GROUP_COLS + GROUP_X + D_STATE:(g + 1) * GROUP_COLS] for g in range(SSM_GROUPS)]
    return jnp.concatenate(xs + bs + cs, axis=-1)


def _dt_to_groups(dt):
    s_dim = dt.shape[0]
    t = dt[:, :SSM_HEADS].reshape(s_dim, SSM_GROUPS, HEADS_PER_GROUP).transpose(1, 0, 2)
    return jnp.pad(t, ((0, 0), (0, 0), (0, LANES - HEADS_PER_GROUP)))


def _dt_from_groups(dtg):
    s_dim = dtg.shape[1]
    return dtg[:, :, :HEADS_PER_GROUP].transpose(1, 0, 2).reshape(s_dim, SSM_HEADS)


def _pack_ssd_params(dt_bias, a_log, d_skip):
    rows = jnp.stack([p.reshape(SSM_GROUPS, HEADS_PER_GROUP) for p in (dt_bias, a_log, d_skip)], axis=1)
    return jnp.pad(rows, ((0, 0), (0, 8 - 3), (0, LANES - HEADS_PER_GROUP)))


def _unpack_ssd_params(par):
    return tuple(par[:, k, :HEADS_PER_GROUP].reshape(SSM_HEADS) for k in range(3))


Q = SSD_CHUNK


def _split3(v):
    hi = v.astype(BF16)
    r1 = v - hi.astype(F32)
    mid = r1.astype(BF16)
    lo = (r1 - mid.astype(F32)).astype(BF16)
    return hi, mid, lo


def _dot_l01(t01, v):
    return sum(_dot(t01, p) for p in _split3(v))


def _dot_r01(v, e01):
    return sum(_dot(p, e01) for p in _split3(v))


def _ssd_consts():
    row = lax.broadcasted_iota(jnp.int32, (Q, Q), 0)
    col = lax.broadcasted_iota(jnp.int32, (Q, Q), 1)
    causal = row >= col
    tril = causal.astype(BF16)
    triu = (col >= row).astype(BF16)
    er = lax.broadcasted_iota(jnp.int32, (LANES, GROUP_X), 0)
    ec = lax.broadcasted_iota(jnp.int32, (LANES, GROUP_X), 1) // SSM_HEAD_DIM
    expand = (er == ec).astype(BF16)
    rr = lax.broadcasted_iota(jnp.int32, (GROUP_X, LANES), 0) // SSM_HEAD_DIM
    rc = lax.broadcasted_iota(jnp.int32, (GROUP_X, LANES), 1)
    reduce = (rr == rc).astype(BF16)
    lane_head = lax.broadcasted_iota(jnp.int32, (Q, GROUP_X), 1) // SSM_HEAD_DIM
    return causal, tril, triu, expand, reduce, lane_head


def _ssd_common(xc_ref, dt_ref, par_ref, consts):
    causal, tril, _, expand, _, _ = consts
    par = par_ref[...]
    bias, alog, dsk = par[0:1, :], par[1:2, :], par[2:3, :]
    a_neg = -jnp.exp(alog)
    dtr = dt_ref[...] + bias
    dt = _softplus(dtr)
    s = _dot_l01(tril, dt * a_neg)
    dt_x = _dot_r01(dt, expand)
    s_x = _dot_r01(s, expand)
    dsk_x = _dot_r01(jnp.broadcast_to(dsk, (8, LANES)), expand)[0:1, :]
    blk = xc_ref[...]
    x = blk[:, :GROUP_X]
    bm = blk[:, GROUP_X:GROUP_X + D_STATE].astype(BF16)
    cm = blk[:, GROUP_X + D_STATE:].astype(BF16)
    xdt = x * dt_x
    g = _dot(cm, bm, _NT)
    return dict(a_neg=a_neg, dtr=dtr, dt=dt, s=s, s_t=s.T, dt_x=dt_x, s_x=s_x, dsk_x=dsk_x, x=x, bm=bm, cm=cm,
                xdt=xdt, g=g)


def _decay(v, r, causal):
    diff = v["s"][:, r:r + 1] - v["s_t"][r:r + 1, :]
    return jnp.exp(jnp.where(causal, diff, -jnp.inf))


def _ssd_specs(n_chunks, rev):
    cidx = (lambda c: n_chunks - 1 - c) if rev else (lambda c: c)
    xc = pl.BlockSpec((Q, GROUP_COLS), lambda g, c: (cidx(c), g))
    gx = pl.BlockSpec((Q, GROUP_X), lambda g, c: (cidx(c), g))
    dt = pl.BlockSpec((None, Q, LANES), lambda g, c: (g, cidx(c), 0))
    par = pl.BlockSpec((None, 8, LANES), lambda g, c: (g, 0, 0))
    nw = pl.BlockSpec((1, GROUP_X), lambda g, c: (0, g))
    hs = pl.BlockSpec((None, None, D_STATE, GROUP_X), lambda g, c: (cidx(c), g, 0, 0))
    return xc, gx, dt, par, nw, hs


def _ssd_fwd(xc, z, dtg, par, nw):
    s_dim = xc.shape[0]
    n_chunks = s_dim // Q
    xc_s, gx_s, dt_s, par_s, nw_s, hs_s = _ssd_specs(n_chunks, False)

    def body(xc_ref, z_ref, dt_ref, par_ref, nw_ref, y_ref, ys_ref, hs_ref, ht):
        @pl.when(pl.program_id(1) == 0)
        def _():
            ht[...] = jnp.zeros_like(ht)

        consts = _ssd_consts()
        causal, lane_head = consts[0], consts[5]
        v = _ssd_common(xc_ref, dt_ref, par_ref, consts)
        xdt_b = v["xdt"].astype(BF16)
        yd = jnp.zeros((Q, GROUP_X), F32)
        for r in range(HEADS_PER_GROUP):
            m = (v["g"] * _decay(v, r, causal)).astype(BF16)
            yd = yd + _dot(m, jnp.where(lane_head == r, xdt_b, jnp.zeros_like(xdt_b)))
        h = ht[...]
        hs_ref[...] = h
        yo = jnp.exp(v["s_x"]) * _dot(v["cm"], h.astype(BF16))
        y = yd + yo + v["dsk_x"] * v["x"]
        s_last = v["s_x"][Q - 1:Q, :]
        snew = _dot(v["bm"], (v["xdt"] * jnp.exp(s_last - v["s_x"])).astype(BF16), _TN)
        ht[...] = jnp.exp(s_last) * h + snew
        zz = z_ref[...]
        yg = y * (zz * _sigmoid(zz))
        y_ref[...] = y
        ys_ref[...] = _nrm(yg, nw_ref[...])[0].astype(BF16)

    return pl.pallas_call(
        body,
        grid=(SSM_GROUPS, n_chunks),
        in_specs=[xc_s, gx_s, dt_s, par_s, nw_s],
        out_specs=[gx_s, gx_s, hs_s],
        out_shape=[jax.ShapeDtypeStruct((s_dim, D_SSM), F32), jax.ShapeDtypeStruct((s_dim, D_SSM), BF16),
                   jax.ShapeDtypeStruct((n_chunks, SSM_GROUPS, D_STATE, GROUP_X), F32)],
        scratch_shapes=[pltpu.VMEM((D_STATE, GROUP_X), F32)],
        compiler_params=_cparams("parallel", "arbitrary"),
        name="ssd_fwd",
    )(xc, z, dtg, par, nw)


def _ssd_bwd(xc, z, dtg, par, nw, y, hs, dymix):
    s_dim = xc.shape[0]
    n_chunks = s_dim // Q
    xc_s, gx_s, dt_s, par_s, nw_s, hs_s = _ssd_specs(n_chunks, True)

    def body(xc_ref, z_ref, dt_ref, par_ref, nw_ref, y_ref, hs_ref, dys_ref,
             dxc_ref, dz_ref, ddt_ref, dpar_ref, dnw_ref, dht):
        @pl.when(pl.program_id(1) == 0)
        def _():
            dht[...] = jnp.zeros_like(dht)
            dpar_ref[...] = jnp.zeros_like(dpar_ref)
            dnw_ref[...] = jnp.zeros_like(dnw_ref)

        consts = _ssd_consts()
        causal, _, triu, _, reduce, lane_head = consts
        v = _ssd_common(xc_ref, dt_ref, par_ref, consts)
        x, bm, cm, xdt, s_x = v["x"], v["bm"], v["cm"], v["xdt"], v["s_x"]
        h = hs_ref[...]
        hb = h.astype(BF16)
        es_x = jnp.exp(s_x)
        yo = es_x * _dot(cm, hb)
        s_last = s_x[Q - 1:Q, :]
        e_x = jnp.exp(s_last - s_x)
        es_last = jnp.exp(s_last)

        yv, zz, nw_v = y_ref[...], z_ref[...], nw_ref[...]
        sg = _sigmoid(zz)
        gz = zz * sg
        _, n, rstd = _nrm(yv * gz, nw_v)
        dout = dys_ref[...]
        dyg, dnw = _nrm_bwd(dout, n, rstd, nw_v)
        dnw_ref[...] += dnw
        dy = dyg * gz
        dz_ref[...] = (dyg * yv * (sg * (1.0 + zz * (1.0 - sg)))).astype(BF16)

        dyb = dy.astype(BF16)
        xdt_b = xdt.astype(BF16)
        dhp = dht[...]
        dhpb = dhp.astype(BF16)
        lane = lax.broadcasted_iota(jnp.int32, (Q, LANES), 1)
        sub = lax.broadcasted_iota(jnp.int32, (LANES, Q), 0)
        dxdt = jnp.zeros((Q, GROUP_X), F32)
        dg = jnp.zeros((Q, Q), F32)
        ds = jnp.zeros((Q, LANES), F32)
        ds_t = jnp.zeros((LANES, Q), F32)
        for r in range(HEADS_PER_GROUP):
            dec = _decay(v, r, causal)
            mf = v["g"] * dec
            dyr = jnp.where(lane_head == r, dyb, jnp.zeros_like(dyb))
            dm = _dot(dyr, xdt_b, _NT)
            dxdt = dxdt + _dot(mf.astype(BF16), dyr, _TN)
            dg = dg + dm * dec
            dd = dm * mf
            ds = ds + jnp.where(lane == r, jnp.sum(dd, axis=1, keepdims=True), 0.0)
            ds_t = ds_t + jnp.where(sub == r, jnp.sum(dd, axis=0, keepdims=True), 0.0)
        ds = ds - ds_t.T
        dgb = dg.astype(BF16)
        dwb = (es_x * dy).astype(BF16)
        dcm = _dot(dgb, bm) + _dot(dwb, hb, _NT)
        dh_prev = _dot(cm, dwb, _TN)
        zst = _dot(bm, dhpb)
        xe = xdt * e_x
        dxdt = dxdt + e_x * zst
        dee = xe * zst
        dbm = _dot(dgb, cm, _TN) + _dot(xe.astype(BF16), dhpb, _NT)
        v_last = jnp.sum(dee, axis=0, keepdims=True) + es_last * jnp.sum(dhp * h, axis=0, keepdims=True)
        row_x = lax.broadcasted_iota(jnp.int32, (Q, GROUP_X), 0)
        tx = dy * yo - dee + jnp.where(row_x == Q - 1, v_last, 0.0)
        ds = ds + _dot_r01(tx, reduce)
        ddta = _dot_l01(triu, ds)
        ddt = ddta * v["a_neg"] + _dot_r01(dxdt * x, reduce)
        dalog = jnp.sum(ddta * v["dt"], axis=0, keepdims=True) * v["a_neg"]
        draw = jnp.where(lane < HEADS_PER_GROUP, ddt * _sigmoid(v["dtr"]), 0.0)
        dbias = jnp.sum(draw, axis=0, keepdims=True)
        ddsk = _dot_r01(jnp.broadcast_to(jnp.sum(dy * x, axis=0, keepdims=True), (8, GROUP_X)), reduce)[0:1, :]
        dht[...] = es_last * dhp + dh_prev
        dxc_ref[:, :GROUP_X] = dxdt * v["dt_x"] + v["dsk_x"] * dy
        dxc_ref[:, GROUP_X:GROUP_X + D_STATE] = dbm
        dxc_ref[:, GROUP_X + D_STATE:] = dcm
        ddt_ref[...] = draw
        dpar_ref[0:1, :] += dbias
        dpar_ref[1:2, :] += dalog
        dpar_ref[2:3, :] += ddsk

    return pl.pallas_call(
        body,
        grid=(SSM_GROUPS, n_chunks),
        in_specs=[xc_s, gx_s, dt_s, par_s, nw_s, gx_s, hs_s, gx_s],
        out_specs=[xc_s, gx_s, dt_s, par_s, nw_s],
        out_shape=[jax.ShapeDtypeStruct((s_dim, SSM_GROUPS * GROUP_COLS), F32),
                   jax.ShapeDtypeStruct((s_dim, D_SSM), BF16),
                   jax.ShapeDtypeStruct((SSM_GROUPS, s_dim, LANES), F32),
                   jax.ShapeDtypeStruct((SSM_GROUPS, 8, LANES), F32),
                   jax.ShapeDtypeStruct((1, D_SSM), F32)],
        scratch_shapes=[pltpu.VMEM((D_STATE, GROUP_X), F32)],
        compiler_params=_cparams("parallel", "arbitrary"),
        name="ssd_bwd",
    )(xc, z, dtg, par, nw, y, hs, dymix)


ATT_SCALE = ATT_HEAD_DIM ** -0.5
NEG_INF = -jnp.inf


def _head(h):
    return slice(h * ATT_HEAD_DIM, (h + 1) * ATT_HEAD_DIM)


def _band_masks():
    qi = lax.broadcasted_iota(jnp.int32, (ATT_BLOCK, ATT_BLOCK), 0)
    kj = lax.broadcasted_iota(jnp.int32, (ATT_BLOCK, ATT_BLOCK), 1)
    return kj <= qi, kj >= qi


def _attn_fwd(qkv_v, d):
    rows = qkv_v.shape[0]
    nb = rows // ATT_BLOCK
    blk = (ATT_BLOCK, D_ATT)
    prev = lambda i: jnp.maximum(i - 1, 0)

    def body(q_ref, kc_ref, kp_ref, vc_ref, vp_ref, o_ref, lse_ref):
        own, before = _band_masks()
        before = before & (pl.program_id(1) > 0)
        lane = lax.broadcasted_iota(jnp.int32, (ATT_BLOCK, LANES), 1)
        lse_all = jnp.zeros((ATT_BLOCK, LANES), F32)
        for h in range(ATT_HEADS):
            q = q_ref[:, _head(h)]
            sc = jnp.where(own, _dot(q, kc_ref[:, _head(h)], _NT) * ATT_SCALE, NEG_INF)
            sp = jnp.where(before, _dot(q, kp_ref[:, _head(h)], _NT) * ATT_SCALE, NEG_INF)
            m = jnp.maximum(jnp.max(sc, axis=1, keepdims=True), jnp.max(sp, axis=1, keepdims=True))
            pc, pp = jnp.exp(sc - m), jnp.exp(sp - m)
            den = jnp.sum(pc, axis=1, keepdims=True) + jnp.sum(pp, axis=1, keepdims=True)
            o = _dot(pc.astype(BF16), vc_ref[:, _head(h)]) + _dot(pp.astype(BF16), vp_ref[:, _head(h)])
            o_ref[:, _head(h)] = o / den
            lse_all = jnp.where(lane == h, m + jnp.log(den), lse_all)
        lse_ref[...] = lse_all

    return pl.pallas_call(
        body,
        grid=(d, nb),
        in_specs=[pl.BlockSpec(blk, lambda r, i: (i, 3 * r)),
                  pl.BlockSpec(blk, lambda r, i: (i, 3 * r + 1)),
                  pl.BlockSpec(blk, lambda r, i: (prev(i), 3 * r + 1)),
                  pl.BlockSpec(blk, lambda r, i: (i, 3 * r + 2)),
                  pl.BlockSpec(blk, lambda r, i: (prev(i), 3 * r + 2))],
        out_specs=[pl.BlockSpec(blk, lambda r, i: (i, r)), pl.BlockSpec((ATT_BLOCK, LANES), lambda r, i: (i, r))],
        out_shape=[jax.ShapeDtypeStruct((rows, d * D_ATT), F32), jax.ShapeDtypeStruct((rows, d * LANES), F32)],
        compiler_params=_cparams("parallel", "arbitrary"),
        name=f"attn_fwd_d{d}",
    )(qkv_v, qkv_v, qkv_v, qkv_v, qkv_v)


def _attn_combine(os_, lses):
    def fn(o1, o2, o3, l1, l2, l3):
        m = jnp.maximum(jnp.maximum(l1, l2), l3)
        tot = m + jnp.log(jnp.exp(l1 - m) + jnp.exp(l2 - m) + jnp.exp(l3 - m))
        w1, w2, w3 = jnp.exp(l1 - tot), jnp.exp(l2 - tot), jnp.exp(l3 - tot)
        cols = []
        for h in range(ATT_HEADS):
            cols.append(w1[:, h:h + 1] * o1[:, _head(h)] + w2[:, h:h + 1] * o2[:, _head(h)]
                        + w3[:, h:h + 1] * o3[:, _head(h)])
        y = jnp.concatenate(cols, axis=1)
        return y, y, tot
    return _rowcall(fn, list(os_) + list(lses), [], [(D_ATT, BF16), (D_ATT, F32), (LANES, F32)], [],
                    name="attn_combine", tr=128)


def _attn_delta(dymix, y_att):
    def fn(dy, y):
        lane = lax.broadcasted_iota(jnp.int32, (dy.shape[0], LANES), 1)
        delta = jnp.zeros((dy.shape[0], LANES), F32)
        for h in range(ATT_HEADS):
            delta = jnp.where(lane == h, jnp.sum(dy[:, _head(h)] * y[:, _head(h)], axis=1, keepdims=True), delta)
        return dy, delta
    return _rowcall(fn, [dymix, y_att], [], [(D_ATT, BF16), (LANES, F32)], [], name="attn_delta",
                    row_cols=[(D_ATT, 1), None])


def _attn_bwd(qkv_v, dy_v, lse_v, delta_v, d):
    rows = qkv_v.shape[0]
    nb = rows // ATT_BLOCK
    blk = (ATT_BLOCK, D_ATT)
    sblk = (ATT_BLOCK, LANES)
    prev = lambda i: jnp.maximum(i - 1, 0)
    nxt = lambda i: jnp.minimum(i + 1, nb - 1)

    def body(qc_ref, qn_ref, kc_ref, kp_ref, vc_ref, vp_ref, dyc_ref, dyn_ref, lc_ref, ln_ref, dc_ref, dn_ref,
             dq_ref, dk_ref, dv_ref):
        i = pl.program_id(1)
        own, before = _band_masks()
        before_c = before & (i > 0)
        before_n = before & (i < nb - 1)
        lc, ln, dc, dn = lc_ref[...], ln_ref[...], dc_ref[...], dn_ref[...]
        for h in range(ATT_HEADS):
            hs = _head(h)
            q, qn, kc, kp, vc, vp = qc_ref[:, hs], qn_ref[:, hs], kc_ref[:, hs], kp_ref[:, hs], vc_ref[:, hs], vp_ref[:, hs]
            dy, dyn = dyc_ref[:, hs], dyn_ref[:, hs]
            lse, lse_n, dl, dl_n = lc[:, h:h + 1], ln[:, h:h + 1], dc[:, h:h + 1], dn[:, h:h + 1]
            pc = jnp.exp(jnp.where(own, _dot(q, kc, _NT) * ATT_SCALE - lse, NEG_INF))
            pp = jnp.exp(jnp.where(before_c, _dot(q, kp, _NT) * ATT_SCALE - lse, NEG_INF))
            pn = jnp.exp(jnp.where(before_n, _dot(qn, kc, _NT) * ATT_SCALE - lse_n, NEG_INF))
            dsc = (pc * (_dot(dy, vc, _NT) - dl)).astype(BF16)
            dsp = (pp * (_dot(dy, vp, _NT) - dl)).astype(BF16)
            dsn = (pn * (_dot(dyn, vc, _NT) - dl_n)).astype(BF16)
            dq_ref[:, hs] = (_dot(dsc, kc) + _dot(dsp, kp)) * ATT_SCALE
            dk_ref[:, hs] = (_dot(dsc, q, _TN) + _dot(dsn, qn, _TN)) * ATT_SCALE
            dv_ref[:, hs] = _dot(pc.astype(BF16), dy, _TN) + _dot(pn.astype(BF16), dyn, _TN)

    return pl.pallas_call(
        body,
        grid=(d, nb),
        in_specs=[pl.BlockSpec(blk, lambda r, i: (i, 3 * r)), pl.BlockSpec(blk, lambda r, i: (nxt(i), 3 * r)),
                  pl.BlockSpec(blk, lambda r, i: (i, 3 * r + 1)), pl.BlockSpec(blk, lambda r, i: (prev(i), 3 * r + 1)),
                  pl.BlockSpec(blk, lambda r, i: (i, 3 * r + 2)), pl.BlockSpec(blk, lambda r, i: (prev(i), 3 * r + 2)),
                  pl.BlockSpec(blk, lambda r, i: (i, r)), pl.BlockSpec(blk, lambda r, i: (nxt(i), r)),
                  pl.BlockSpec(sblk, lambda r, i: (i, r)), pl.BlockSpec(sblk, lambda r, i: (nxt(i), r)),
                  pl.BlockSpec(sblk, lambda r, i: (i, r)), pl.BlockSpec(sblk, lambda r, i: (nxt(i), r))],
        out_specs=[pl.BlockSpec(blk, lambda r, i: (i, r))] * 3,
        out_shape=[jax.ShapeDtypeStruct((rows, d * D_ATT), F32)] * 3,
        compiler_params=_cparams("parallel", "arbitrary"),
        name=f"attn_bwd_d{d}",
    )(qkv_v, qkv_v, qkv_v, qkv_v, qkv_v, qkv_v, dy_v, dy_v, lse_v, lse_v, delta_v, delta_v)


def _attn_sum(dqs, dks, dvs, deps=()):
    def fn(*parts):
        return (jnp.concatenate([parts[0] + parts[1] + parts[2], parts[3] + parts[4] + parts[5],
                                 parts[6] + parts[7] + parts[8]], axis=1),)
    return _rowcall(fn, list(dqs) + list(dks) + list(dvs), [], [(3 * D_ATT, BF16)], [], name="attn_sum", tr=128,
                    deps=deps)[0]


def _attention_fwd(qkv):
    s_dim = qkv.shape[0]
    os_, lses = [], []
    for d in DILATIONS:
        o, lse = _attn_fwd(qkv.reshape(s_dim // d, d * 3 * D_ATT), d)
        os_.append(o.reshape(s_dim, D_ATT))
        lses.append(lse.reshape(s_dim, LANES))
    return _attn_combine(os_, lses)


def _attention_bwd(qkv, dymix, y_att, lse, sum_deps=()):
    s_dim = qkv.shape[0]
    dy, delta = _attn_delta(dymix, y_att)
    dqs, dks, dvs = [], [], []
    for d in DILATIONS:
        dq, dk, dv = _attn_bwd(qkv.reshape(s_dim // d, d * 3 * D_ATT), dy.reshape(s_dim // d, d * D_ATT),
                               lse.reshape(s_dim // d, d * LANES), delta.reshape(s_dim // d, d * LANES), d)
        dqs.append(dq.reshape(s_dim, D_ATT))
        dks.append(dk.reshape(s_dim, D_ATT))
        dvs.append(dv.reshape(s_dim, D_ATT))
    return _attn_sum(dqs, dks, dvs, sum_deps)


def _adamw(w, g, m, v, name):
    def fn(wb, gb, mb, vb):
        m2 = ADAM_B1 * mb + (1.0 - ADAM_B1) * gb
        v2 = ADAM_B2 * vb + (1.0 - ADAM_B2) * (gb * gb)
        m_hat = m2 / (1.0 - ADAM_B1 ** ADAM_STEP)
        v_hat = v2 / (1.0 - ADAM_B2 ** ADAM_STEP)
        delta = -ADAM_LR * (m_hat / (jnp.sqrt(v_hat) + ADAM_EPS) + ADAM_WD * wb)
        return delta, m2, v2
    cols = w.shape[1]
    tr = 128 if w.shape[0] % 128 == 0 else w.shape[0]
    return _rowcall(fn, [w, g, m, v], [], [(cols, F32)] * 3, [], name=name, tr=tr)


ANY = pl.BlockSpec(memory_space=pl.ANY)


def _position():
    x, y, c = lax.axis_index("x"), lax.axis_index("y"), lax.axis_index("c")
    chips = [(1 - x, y), (x, 1 - y), (1 - x, 1 - y)]
    return x, y, c, chips


def _remote(src, dst, send_sem, recv_sem, device):
    return pltpu.make_async_remote_copy(src_ref=src, dst_ref=dst, send_sem=send_sem, recv_sem=recv_sem,
                                        device_id=device, device_id_type=MESH)


def _gather_shards(shards):
    n = len(shards)

    def body(*refs):
        ins, outs = refs[:n], refs[n:2 * n]
        send_sems, recv_sems = refs[2 * n:]
        x, y, c, chips = _position()
        sibling = (x, y, 1 - c)

        def half(a, j, cc):
            h = ins[a].shape[0] // 2
            return outs[a].at[j, pl.ds(cc * h, h), :]

        sent = []
        for a in range(n):
            h = ins[a].shape[0] // 2
            for j, chip in enumerate(chips):
                cp = _remote(ins[a].at[pl.ds(c * h, h), :], half(a, j, c), send_sems.at[6 * a + j],
                             recv_sems.at[6 * a + j], (chip[0], chip[1], c))
                cp.start()
                sent.append(cp)
        for a in range(n):
            for j in range(3):
                landed = half(a, j, c)
                _remote(landed, landed, send_sems.at[6 * a + j], recv_sems.at[6 * a + j], (x, y, c)).wait_recv()
                cp = _remote(landed, landed, send_sems.at[6 * a + 3 + j], recv_sems.at[6 * a + 3 + j], sibling)
                cp.start()
                sent.append(cp)
        for a in range(n):
            for j in range(3):
                handed = half(a, j, 1 - c)
                _remote(handed, handed, send_sems.at[6 * a + 3 + j], recv_sems.at[6 * a + 3 + j], (x, y, c)).wait_recv()
        for cp in sent:
            cp.wait_send()

    return pl.pallas_call(
        body,
        in_specs=[ANY] * n,
        out_specs=[ANY] * n,
        out_shape=[jax.ShapeDtypeStruct((3,) + s.shape, s.dtype) for s in shards],
        scratch_shapes=[pltpu.SemaphoreType.DMA((6 * n,)), pltpu.SemaphoreType.DMA((6 * n,))],
        name="gather_shards",
    )(*shards)


def _handshake(peers):
    barrier = pltpu.get_barrier_semaphore()
    for p in peers:
        pl.semaphore_signal(barrier, inc=1, device_id=p, device_id_type=MESH)
    pl.semaphore_wait(barrier, len(peers))


def _gather_shards_async(shards, collective_id, name):
    n = len(shards)
    srcs = [jax.new_ref(s, memory_space=pltpu.MemorySpace.HBM) for s in shards]
    dsts = [jax.empty_ref(jax.ShapeDtypeStruct((3,) + s.shape, s.dtype), memory_space=pltpu.MemorySpace.HBM)
            for s in shards]

    @pl.kernel(mesh=plsc.ScalarSubcoreMesh(axis_name="seq", num_cores=1), name=name,
               scratch_types=(pltpu.SemaphoreType.DMA((6 * n,)), pltpu.SemaphoreType.DMA((6 * n,))),
               compiler_params=pltpu.CompilerParams(collective_id=collective_id))
    def launch(send_sems, recv_sems):
        x, y, c, chips = _position()
        sibling = (x, y, 1 - c)
        _handshake([(chip[0], chip[1], c) for chip in chips] + [sibling])

        def half(a, j, cc):
            h = shards[a].shape[0] // 2
            return dsts[a].at[j, pl.ds(cc * h, h), :]

        sent = []
        for a in range(n):
            h = shards[a].shape[0] // 2
            for j, chip in enumerate(chips):
                cp = _remote(srcs[a].at[pl.ds(c * h, h), :], half(a, j, c), send_sems.at[6 * a + j],
                             recv_sems.at[6 * a + j], (chip[0], chip[1], c))
                cp.start()
                sent.append(cp)
        for a in range(n):
            for j in range(3):
                landed = half(a, j, c)
                _remote(landed, landed, send_sems.at[6 * a + j], recv_sems.at[6 * a + j], (x, y, c)).wait_recv()
                cp = _remote(landed, landed, send_sems.at[6 * a + 3 + j], recv_sems.at[6 * a + 3 + j], sibling)
                cp.start()
                sent.append(cp)
        for a in range(n):
            for j in range(3):
                handed = half(a, j, 1 - c)
                _remote(handed, handed, send_sems.at[6 * a + 3 + j], recv_sems.at[6 * a + 3 + j], (x, y, c)).wait_recv()
        for cp in sent:
            cp.wait_send()

    launch()
    return [d[...] for d in dsts]


def _by_chip(own, others):
    me = 2 * lax.axis_index("x") + lax.axis_index("y")
    rel = jnp.stack([own, others[1], others[0], others[2]])
    return jnp.stack([lax.dynamic_index_in_dim(rel, q ^ me, 0, keepdims=False) for q in range(N_CHIPS)])


def _exchange_sibling_halves(grads):
    n = len(grads)

    def body(*refs):
        ins, outs = refs[:n], refs[n:2 * n]
        send_sems, recv_sems = refs[2 * n:]
        x, y, c, _ = _position()
        copies = []
        for a in range(n):
            h = ins[a].shape[1] // 2
            cp = _remote(ins[a].at[:, pl.ds((1 - c) * h, h), :], outs[a], send_sems.at[a], recv_sems.at[a], (x, y, 1 - c))
            cp.start()
            copies.append(cp)
        for cp in copies:
            cp.wait()

    return pl.pallas_call(
        body,
        in_specs=[ANY] * n,
        out_specs=[ANY] * n,
        out_shape=[jax.ShapeDtypeStruct((g.shape[0], g.shape[1] // 2, g.shape[2]), g.dtype) for g in grads],
        scratch_shapes=[pltpu.SemaphoreType.DMA((n,)), pltpu.SemaphoreType.DMA((n,))],
        name="exchange_sibling_halves",
    )(*grads)


def _exchange_quarters(parts):
    n = len(parts)

    def body(*refs):
        ins, outs = refs[:n], refs[n:2 * n]
        send_sems, recv_sems = refs[2 * n:]
        x, y, c, chips = _position()
        copies = []
        for a in range(n):
            for j, chip in enumerate(chips):
                cp = _remote(ins[a].at[2 * chip[0] + chip[1]], outs[a].at[j], send_sems.at[3 * a + j],
                             recv_sems.at[3 * a + j], (chip[0], chip[1], c))
                cp.start()
                copies.append(cp)
        for cp in copies:
            cp.wait()

    return pl.pallas_call(
        body,
        in_specs=[ANY] * n,
        out_specs=[ANY] * n,
        out_shape=[jax.ShapeDtypeStruct((3,) + p.shape[1:], p.dtype) for p in parts],
        scratch_shapes=[pltpu.SemaphoreType.DMA((3 * n,)), pltpu.SemaphoreType.DMA((3 * n,))],
        name="exchange_quarters",
    )(*parts)


def _share_reduced_halves(halves):
    n = len(halves)

    def body(*refs):
        ins, outs = refs[:n], refs[n:2 * n]
        send_sems, recv_sems = refs[2 * n:]
        x, y, c, _ = _position()
        copies = []
        for a in range(n):
            cp = _remote(ins[a], outs[a], send_sems.at[a], recv_sems.at[a], (x, y, 1 - c))
            cp.start()
            copies.append(cp)
        for cp in copies:
            cp.wait()

    return pl.pallas_call(
        body,
        in_specs=[ANY] * n,
        out_specs=[ANY] * n,
        out_shape=[jax.ShapeDtypeStruct(p.shape, p.dtype) for p in halves],
        scratch_shapes=[pltpu.SemaphoreType.DMA((n,)), pltpu.SemaphoreType.DMA((n,))],
        name="share_reduced_halves",
    )(*halves)


def _add_sibling(grad, got, c_arr, name, deps=()):
    nq, rows, cols = grad.shape
    h = rows // 2
    tr = 128
    nb = h // tr

    def body(c_ref, a_ref, b_ref, *rest):
        o_ref, ob_ref = rest[len(deps):]
        total = a_ref[...] + b_ref[...]
        o_ref[...] = total
        ob_ref[...] = total.astype(BF16)

    out_spec = pl.BlockSpec((None, tr, cols), lambda q, i, c: (q, i, 0))
    return pl.pallas_call(
        body,
        grid_spec=pltpu.PrefetchScalarGridSpec(
            num_scalar_prefetch=1, grid=(nq, nb),
            in_specs=[pl.BlockSpec((None, tr, cols), lambda q, i, c: (q, c[0] * nb + i, 0)),
                      pl.BlockSpec((None, tr, cols), lambda q, i, c: (q, i, 0))] + [ANY] * len(deps),
            out_specs=[out_spec, out_spec]),
        out_shape=[jax.ShapeDtypeStruct((nq, h, cols), F32), jax.ShapeDtypeStruct((nq, h, cols), BF16)],
        compiler_params=_cparams("parallel", "parallel"),
        name=name,
    )(c_arr, grad, got, *deps)


def _add_chips(part, got, chip_arr, name, deps=()):
    _, h, cols = part.shape
    tr = 128

    def body(q_ref, p_ref, g0_ref, g1_ref, g2_ref, *rest):
        o_ref = rest[len(deps)]
        o_ref[...] = ((p_ref[...] + g0_ref[...].astype(F32)) + g1_ref[...].astype(F32)) + g2_ref[...].astype(F32)

    got_spec = lambda j: pl.BlockSpec((None, tr, cols), lambda i, q: (j, i, 0))
    return pl.pallas_call(
        body,
        grid_spec=pltpu.PrefetchScalarGridSpec(
            num_scalar_prefetch=1, grid=(h // tr,),
            in_specs=[pl.BlockSpec((None, tr, cols), lambda i, q: (q[0], i, 0)), got_spec(0), got_spec(1), got_spec(2)]
            + [ANY] * len(deps),
            out_specs=pl.BlockSpec((tr, cols), lambda i, q: (i, 0))),
        out_shape=jax.ShapeDtypeStruct((h, cols), F32),
        compiler_params=_cparams("parallel"),
        name=name,
    )(chip_arr, part, got, got, got, *deps)


def _reduce_scatter(grads, names):
    c_arr = lax.axis_index("c").astype(jnp.int32).reshape(1)
    chip_arr = (2 * lax.axis_index("x") + lax.axis_index("y")).astype(jnp.int32).reshape(1)
    from_sibling = _exchange_sibling_halves(grads)
    parts = [_add_sibling(g, s, c_arr, f"add_sibling_{nm}") for g, s, nm in zip(grads, from_sibling, names)]
    from_chips = _exchange_quarters([pb for _, pb in parts])
    halves = [_add_chips(p, f, chip_arr, f"add_chips_{nm}") for (p, _), f, nm in zip(parts, from_chips, names)]
    return list(zip(halves, _share_reduced_halves(halves)))


def _sequencer_exchange(src, out_shape, collective_id, name, plan, n_copies):
    src_ref = jax.new_ref(src, memory_space=pltpu.MemorySpace.HBM)
    dst_ref = jax.empty_ref(out_shape, memory_space=pltpu.MemorySpace.HBM)

    @pl.kernel(mesh=plsc.ScalarSubcoreMesh(axis_name="seq", num_cores=1), name=name,
               scratch_types=(pltpu.SemaphoreType.DMA((n_copies,)), pltpu.SemaphoreType.DMA((n_copies,))),
               compiler_params=pltpu.CompilerParams(collective_id=collective_id))
    def launch(send_sems, recv_sems):
        x, y, c, chips = _position()
        copies = plan(src_ref, dst_ref, x, y, c, chips)
        _handshake([peer for _, _, peer in copies])
        started = []
        for k, (s, d, peer) in enumerate(copies):
            cp = _remote(s, d, send_sems.at[k], recv_sems.at[k], peer)
            cp.start()
            started.append(cp)
        for cp in started:
            cp.wait()

    launch()
    return dst_ref[...]


class _AsyncReduceScatter:
    def __init__(self, grad, nm, first_id):
        self.grad, self.nm, self.first_id = grad, nm, first_id
        nq, rows, cols = grad.shape
        h = self.h = rows // 2

        def to_sibling(s, d, x, y, c, chips):
            return [(s.at[:, pl.ds((1 - c) * h, h), :], d, (x, y, 1 - c))]

        self.from_sibling = _sequencer_exchange(grad, jax.ShapeDtypeStruct((nq, h, cols), F32), first_id,
                                                f"rs_sibling_{nm}", to_sibling, 1)

    def sibling_sum(self, not_before=()):
        cols = self.grad.shape[2]
        c_arr = lax.axis_index("c").astype(jnp.int32).reshape(1)
        self.part, self.part_b = _add_sibling(self.grad, self.from_sibling, c_arr, f"add_sibling_{self.nm}", not_before)

        def to_chips(s, d, x, y, c, chips):
            return [(s.at[2 * chip[0] + chip[1]], d.at[j], (chip[0], chip[1], c)) for j, chip in enumerate(chips)]

        self.from_chips = _sequencer_exchange(self.part_b, jax.ShapeDtypeStruct((3, self.h, cols), BF16),
                                              self.first_id + 1, f"rs_quarters_{self.nm}", to_chips, 3)
        return self.part_b

    def chip_sum(self, not_before=()):
        cols = self.grad.shape[2]
        chip_arr = (2 * lax.axis_index("x") + lax.axis_index("y")).astype(jnp.int32).reshape(1)
        self.half = _add_chips(self.part, self.from_chips, chip_arr, f"add_chips_{self.nm}", not_before)

        def whole_to_sibling(s, d, x, y, c, chips):
            return [(s, d, (x, y, 1 - c))]

        self.other = _sequencer_exchange(self.half, jax.ShapeDtypeStruct((self.h, cols), F32), self.first_id + 2,
                                         f"rs_share_{self.nm}", whole_to_sibling, 1)
        return self.half

    def share(self):
        return self.half, self.other


def _after(x, deps, name):
    def body(x_ref, *rest):
        rest[-1][...] = x_ref[...]

    vm = pl.BlockSpec(memory_space=pltpu.VMEM)
    return pl.pallas_call(body, in_specs=[vm] + [ANY] * len(deps), out_specs=vm,
                          out_shape=jax.ShapeDtypeStruct(x.shape, x.dtype), name=name)(x, *deps)


def _adamw_halves(w, mine, other, m, v, name):
    rows, cols = w.shape
    tr = 128
    nb = rows // 2 // tr
    c_arr = lax.axis_index("c").astype(jnp.int32).reshape(1)

    def body(c_ref, w_ref, a_ref, b_ref, m_ref, v_ref, g_out, d_out, m_out, v_out):
        is_mine = (pl.program_id(0) // nb) == c_ref[0]
        g = jnp.where(is_mine, a_ref[...], b_ref[...])
        wb, mb, vb = w_ref[...], m_ref[...], v_ref[...]
        m2 = ADAM_B1 * mb + (1.0 - ADAM_B1) * g
        v2 = ADAM_B2 * vb + (1.0 - ADAM_B2) * (g * g)
        m_hat = m2 / (1.0 - ADAM_B1 ** ADAM_STEP)
        v_hat = v2 / (1.0 - ADAM_B2 ** ADAM_STEP)
        g_out[...] = g
        d_out[...] = -ADAM_LR * (m_hat / (jnp.sqrt(v_hat) + ADAM_EPS) + ADAM_WD * wb)
        m_out[...] = m2
        v_out[...] = v2

    full = pl.BlockSpec((tr, cols), lambda i, c: (i, 0))
    half = pl.BlockSpec((tr, cols), lambda i, c: (i % nb, 0))
    return pl.pallas_call(
        body,
        grid_spec=pltpu.PrefetchScalarGridSpec(
            num_scalar_prefetch=1, grid=(rows // tr,),
            in_specs=[full, half, half, full, full], out_specs=[full] * 4),
        out_shape=[jax.ShapeDtypeStruct((rows, cols), F32)] * 4,
        compiler_params=_cparams("parallel"),
        name=name,
    )(c_arr, w, mine, other, m, v)


def _all_sum_small(v):
    n_dev = 8

    def body(v_ref, o_ref, gath, send_sems, recv_sems):
        x, y, c, _ = _position()
        me = 4 * x + 2 * y + c
        gath[me] = v_ref[...]
        copies = []
        for k in range(1, n_dev):
            peer = tuple(1 - p if (k >> s) & 1 else p for p, s in ((x, 2), (y, 1), (c, 0)))
            cp = _remote(v_ref, gath.at[me], send_sems.at[k - 1], recv_sems.at[k - 1], peer)
            cp.start()
            copies.append(cp)
        for cp in copies:
            cp.wait()
        acc = gath[0]
        for i in range(1, n_dev):
            acc = acc + gath[i]
        o_ref[...] = acc

    vm = pl.BlockSpec(memory_space=pltpu.VMEM)
    return pl.pallas_call(
        body,
        in_specs=[vm],
        out_specs=vm,
        out_shape=jax.ShapeDtypeStruct(v.shape, F32),
        scratch_shapes=[pltpu.VMEM((n_dev,) + v.shape, F32), pltpu.SemaphoreType.DMA((n_dev - 1,)),
                        pltpu.SemaphoreType.DMA((n_dev - 1,))],
        name="all_sum_small",
    )(v)


def _pack_rows(vectors):
    rows = []
    for v in vectors:
        flat = v.reshape(-1).astype(F32)
        rows.append(jnp.pad(flat, (0, (-flat.shape[0]) % LANES)).reshape(-1, LANES))
    out = jnp.concatenate(rows, axis=0)
    return jnp.pad(out, ((0, (-out.shape[0]) % 8), (0, 0)))


def _unpack_rows(packed, shapes):
    outs, r = [], 0
    for shp in shapes:
        size = math.prod(shp)
        nr = -(-size // LANES)
        outs.append(packed[r:r + nr].reshape(-1)[:size].reshape(shp))
        r += nr
    return outs


def _relu_sq(acc):
    r = jnp.maximum(acc, 0.0)
    return r, r * r


def _relu_sq_bwd(acc, r):
    return (acc * (2.0 * r.astype(F32)),)


def kernel(x, norm_mix_pre, w_in, conv_w, conv_b, dt_bias, a_log, d_skip, ssm_norm_w, w_out, norm_mix_post, norm_mlp_pre, w_up, w_down, norm_mlp_post, loss_target, m_norm_mix_pre, m_w_in, m_conv_w, m_conv_b, m_dt_bias, m_a_log, m_d_skip, m_ssm_norm_w, m_w_out, m_norm_mix_post, m_norm_mlp_pre, m_w_up, m_w_down, m_norm_mlp_post, v_norm_mix_pre, v_w_in, v_conv_w, v_conv_b, v_dt_bias, v_a_log, v_d_skip, v_ssm_norm_w, v_w_out, v_norm_mix_post, v_norm_mlp_pre, v_w_up, v_w_down, v_norm_mlp_post):
    s_dim = x.shape[1]
    xs, target = x[0], loss_target[0]
    chip = 2 * lax.axis_index("x") + lax.axis_index("y")

    own = [w_in[0].astype(BF16), w_out[0].astype(BF16), w_up[0].astype(BF16), w_down[0].astype(BF16)]
    fetched = list(_gather_shards(own[:1])) + _gather_shards_async(own[1:], 1, "gather_rest")
    g_in, g_out, g_up, g_down = [_by_chip(o, f) for o, f in zip(own, fetched)]
    w_in_full = g_in.transpose(1, 0, 2).reshape(D_MODEL, D_IN_PROJ)
    w_z = w_in_full[:, :D_SSM]
    w_xbc = _perm_cols(w_in_full[:, D_SSM:D_SSM + D_XBC])
    w_dt = jnp.pad(w_in_full[:, D_SSM + D_XBC:D_SSM + D_XBC + SSM_HEADS], ((0, 0), (0, LANES - SSM_HEADS)))
    w_qkv = w_in_full[:, D_SSM + D_XBC + SSM_HEADS:]
    w_out_full = g_out.reshape(D_MIX, D_MODEL)
    w_up_full = g_up.transpose(1, 0, 2).reshape(D_MODEL, D_FF)
    w_down_full = g_down.reshape(D_FF, D_MODEL)

    conv_cols = D_XBC // N_CHIPS
    conv_placed = lax.dynamic_update_slice(jnp.zeros((8, D_XBC), F32), 0.5 * conv_w[0], (0, chip * conv_cols))
    conv_full = _all_sum_small(conv_placed.reshape(-1, LANES)).reshape(8, D_XBC)
    w8 = _perm_cols(conv_full.at[CONV_WIDTH].set(conv_b[0]))

    u = _pre_norm(xs, norm_mix_pre)
    z = _matmul([(u, w_z, TK)], "nn", [F32], name="proj_z")
    xbc = _matmul([(u, w_xbc, TK)], "nn", [F32], name="proj_xbc")
    dt_raw = _matmul([(u, w_dt, TK)], "nn", [F32], name="proj_dt")
    qkv = _matmul([(u, w_qkv, TK)], "nn", [BF16], name="proj_qkv")
    xc = _conv_fwd(xbc, w8)
    dtg = _dt_to_groups(dt_raw)
    par = _pack_ssd_params(dt_bias[0], a_log[0], d_skip[0])
    y, y_ssm, states = _ssd_fwd(xc, z, dtg, par, ssm_norm_w)
    y_att, y_att_f32, lse = _attention_fwd(qkv)
    y_mix = jnp.concatenate([y_ssm, y_att], axis=1)
    mix = _matmul([(y_mix, w_out_full, TK)], "nn", [F32], name="out_proj")
    h1, u2 = _post_pre_norm(xs, mix, norm_mix_post, norm_mlp_pre)
    hid, act = _matmul([(u2, w_up_full, TK)], "nn", [BF16, BF16], name="mlp_up", epilogue=_relu_sq)
    ff = _matmul([(act, w_down_full, TK)], "nn", [F32], name="mlp_down")
    dh2, dff, d_g4, loss_part = _tail(ff, h1, target, norm_mlp_post)

    dhid = _matmul([(dff, w_down_full, TK)], "nt", [BF16], name="mlp_down_dx", epilogue=_relu_sq_bwd, extras=[hid])
    weights = {"norm_mix_pre": (norm_mix_pre, m_norm_mix_pre, v_norm_mix_pre), "w_in": (w_in, m_w_in, v_w_in),
               "conv_w": (conv_w, m_conv_w, v_conv_w), "conv_b": (conv_b, m_conv_b, v_conv_b),
               "dt_bias": (dt_bias, m_dt_bias, v_dt_bias), "a_log": (a_log, m_a_log, v_a_log),
               "d_skip": (d_skip, m_d_skip, v_d_skip), "ssm_norm_w": (ssm_norm_w, m_ssm_norm_w, v_ssm_norm_w),
               "w_out": (w_out, m_w_out, v_w_out), "norm_mix_post": (norm_mix_post, m_norm_mix_post, v_norm_mix_post),
               "norm_mlp_pre": (norm_mlp_pre, m_norm_mlp_pre, v_norm_mlp_pre), "w_up": (w_up, m_w_up, v_w_up),
               "w_down": (w_down, m_w_down, v_w_down),
               "norm_mlp_post": (norm_mlp_post, m_norm_mlp_post, v_norm_mlp_post)}
    grads, delta, new_m, new_v = {}, {}, {}, {}

    def adamw_big(n, halves):
        w, m, v = weights[n]
        g_, d_, m_, v_ = _adamw_halves(w[0], halves[0], halves[1], m[0], v[0], f"adamw_{n}")
        grads[n], delta[n], new_m[n], new_v[n] = g_[None], d_[None], m_[None], v_[None]

    dw_down = _matmul([(act, dff, TK)], "tn", [F32], name="mlp_down_dw")
    rs_down = _AsyncReduceScatter(dw_down.reshape(N_CHIPS, D_FF // N_CHIPS, D_MODEL), "w_down", 11)
    dw_up = _matmul([(u2, dhid, TK)], "tn", [F32], name="mlp_up_dw", deps=[dw_down])
    rs_up = _AsyncReduceScatter(dw_up.reshape(D_MODEL, N_CHIPS, D_FF // N_CHIPS).transpose(1, 0, 2), "w_up", 8)
    du2 = _matmul([(dhid, w_up_full, TK)], "nt", [F32], name="mlp_up_dx",
                  deps=[rs_down.sibling_sum(not_before=[dw_up])])
    dh1, dmix, d_g3, d_g2 = _mid_bwd(du2, h1, dh2, mix, norm_mix_post, norm_mlp_pre,
                                     deps=[rs_up.sibling_sum(not_before=[du2])])
    dymix = _matmul([(dmix, w_out_full, TK)], "nt", [F32], name="out_proj_dx")
    dw_out = _matmul([(y_mix, dmix, TK)], "tn", [F32], name="out_proj_dw")
    rs_out = _AsyncReduceScatter(dw_out.reshape(N_CHIPS, D_MIX // N_CHIPS, D_MODEL), "w_out", 5)
    dqkv = _attention_bwd(qkv, dymix, y_att_f32, lse,
                          sum_deps=[rs_down.chip_sum(not_before=[dymix]), rs_out.sibling_sum(not_before=[dymix])])
    g_down = rs_down.share()
    par_late = _after(par, [*g_down, rs_up.chip_sum(not_before=[dqkv])], "after_w_down")
    dxc, dz, ddtg, dpar, d_nw = _ssd_bwd(xc, z, dtg, par_late, ssm_norm_w, y, states, dymix)
    g_up = rs_up.share()
    dxbc, dw8 = _conv_bwd(xbc, _after(w8, [*g_up, rs_out.chip_sum(not_before=[dxc])], "after_w_up"), dxc)
    ddt = jnp.pad(_dt_from_groups(ddtg), ((0, 0), (0, LANES - SSM_HEADS))).astype(BF16)
    g_out = rs_out.share()
    dw_z = _matmul([(u, dz, TK)], "tn", [F32], name="proj_z_dw")
    dw_xbc = _matmul([(u, dxbc, TK)], "tn", [F32], name="proj_xbc_dw")
    dw_dt = _matmul([(u, ddt, TK)], "tn", [F32], name="proj_dt_dw")
    dw_qkv = _matmul([(u, dqkv, TK)], "tn", [F32], name="proj_qkv_dw")
    dw_in = jnp.concatenate([dw_z, _unperm_cols(dw_xbc), dw_dt[:, :SSM_HEADS], dw_qkv], axis=1)
    rs_in = _AsyncReduceScatter(dw_in.reshape(D_MODEL, N_CHIPS, W_IN_SHARD).transpose(1, 0, 2), "w_in", 2)
    adamw_big("w_down", g_down)
    adamw_big("w_up", g_up)
    rs_in.sibling_sum(not_before=[delta["w_up"]])
    du = _matmul([(dz, w_z, TK_MULTI), (dxbc, w_xbc, TK_MULTI), (dqkv, w_qkv, TK_MULTI), (ddt, w_dt, LANES)], "nt",
                 [F32], name="proj_dx", deps=[*g_out, rs_in.part_b])
    grad_x, d_g1 = _first_bwd(du, xs, dh1, norm_mix_pre)
    adamw_big("w_out", g_out)
    rs_in.chip_sum(not_before=[grad_x, delta["w_out"]])

    dconv = _unperm_cols(dw8)
    d_bias, d_alog, d_dskip = _unpack_ssd_params(dpar)
    small_shapes = [(1, D_MODEL), (CONV_WIDTH, D_XBC), (1, D_XBC), (1, SSM_HEADS), (1, SSM_HEADS), (1, SSM_HEADS),
                    (1, D_SSM), (1, D_MODEL), (1, D_MODEL), (1, D_MODEL), (1, LANES)]
    summed = _unpack_rows(
        _all_sum_small(_pack_rows([d_g1, dconv[:CONV_WIDTH], dconv[CONV_WIDTH:CONV_WIDTH + 1], d_bias, d_alog,
                                   d_dskip, d_nw, d_g2, d_g3, d_g4, loss_part])), small_shapes)
    (g_g1, g_conv_full, g_conv_b, g_bias, g_alog, g_dskip, g_nw, g_g2, g_g3, g_g4, loss_row) = summed
    loss = loss_row[0, 0]
    g_conv_w = lax.dynamic_slice(g_conv_full, (0, chip * conv_cols), (CONV_WIDTH, conv_cols))[None]

    grads.update({"norm_mix_pre": g_g1, "conv_w": g_conv_w, "conv_b": g_conv_b, "dt_bias": g_bias,
                  "a_log": g_alog, "d_skip": g_dskip, "ssm_norm_w": g_nw, "norm_mix_post": g_g2,
                  "norm_mlp_pre": g_g3, "norm_mlp_post": g_g4})
    order = list(weights)
    small_names = [n for n in order if n not in ("w_in", "w_out", "w_up", "w_down")]
    small_w_shapes = [weights[n][0].shape for n in small_names]
    packed = [_pack_rows([weights[n][k] for n in small_names]) for k in range(3)]
    packed_g = _pack_rows([grads[n].reshape(weights[n][0].shape) for n in small_names])
    sd, sm, sv = _adamw(packed[0], packed_g, packed[1], packed[2], "adamw_small")
    for k, n in enumerate(small_names):
        grads[n] = grads[n].reshape(weights[n][0].shape)
    for res, pk in ((delta, sd), (new_m, sm), (new_v, sv)):
        for n, val in zip(small_names, _unpack_rows(pk, small_w_shapes)):
            res[n] = val
    adamw_big("w_in", rs_in.share())

    return (loss, grad_x[None], *[grads[n] for n in order], *[delta[n] for n in order],
            *[new_m[n] for n in order], *[new_v[n] for n in order])
```

```python
import functools
import math

import numpy as np
import jax
import jax.numpy as jnp
from jax import lax
from jax.experimental import pallas as pl
from jax.experimental.pallas import tpu as pltpu
from jax.experimental.pallas import tpu_sc as plsc

F32 = jnp.float32
BF16 = jnp.bfloat16

D_MODEL = 2048
SSM_HEAD_DIM = 64
SSM_GROUPS = 8
HEADS_PER_GROUP = 4
SSM_HEADS = SSM_GROUPS * HEADS_PER_GROUP
D_SSM = SSM_HEADS * SSM_HEAD_DIM
D_STATE = 128
CONV_WIDTH = 4
SSD_CHUNK = 128
D_XBC = D_SSM + 2 * SSM_GROUPS * D_STATE
GROUP_X = HEADS_PER_GROUP * SSM_HEAD_DIM
GROUP_COLS = GROUP_X + 2 * D_STATE
ATT_HEAD_DIM = 128
ATT_HEADS = 16
D_ATT = ATT_HEADS * ATT_HEAD_DIM
DILATIONS = (1, 4, 16)
ATT_BLOCK = 128
D_MIX = D_SSM + D_ATT
D_IN_PROJ = D_SSM + D_XBC + SSM_HEADS + 3 * D_ATT
D_FF = 4 * D_MODEL
EPS = 1e-6
N_CHIPS = 4
W_IN_SHARD = D_IN_PROJ // N_CHIPS

ADAM_LR = 0.001
ADAM_B1 = 0.9
ADAM_B2 = 0.999
ADAM_EPS = 1e-08
ADAM_WD = 0.01
ADAM_STEP = 10

LANES = 128
VMEM_LIMIT = 48 * 1024 * 1024
MESH = pl.DeviceIdType.MESH

_NN = (((1,), (0,)), ((), ()))
_NT = (((1,), (1,)), ((), ()))
_TN = (((0,), (0,)), ((), ()))


def _dot(a, b, dims=_NN):
    return lax.dot_general(a, b, dims, preferred_element_type=F32)


def _cparams(*sem):
    return pltpu.CompilerParams(dimension_semantics=sem, vmem_limit_bytes=VMEM_LIMIT)


TK = 2048
TK_MULTI = 1024


def _matmul(pairs, mode, out_dtypes, *, name, tm=1024, tn=1024, epilogue=None, extras=(), deps=(),
            b_quarters=False, out_quarters=False):
    a0, b0, _ = pairs[0]
    m_dim = a0.shape[1] if mode == "tn" else a0.shape[0]
    if b_quarters:
        assert len(pairs) == 1 and mode in ("nn", "nt")
        n_dim = b0.shape[1] if mode == "nt" else N_CHIPS * b0.shape[2]
    else:
        n_dim = b0.shape[0] if mode == "nt" else b0.shape[1]
    tm, tn = min(tm, m_dim), min(tn, n_dim)
    nks, offs = [], []
    for a, _, tk in pairs:
        k_dim = a.shape[0] if mode == "tn" else a.shape[1]
        assert k_dim % tk == 0, (name, k_dim, tk)
        offs.append(sum(nks))
        nks.append(k_dim // tk)
    nk_total = sum(nks)
    assert m_dim % tm == 0 and n_dim % tn == 0, (name, m_dim, n_dim)
    dims = {"nn": _NN, "nt": _NT, "tn": _TN}[mode]
    n_pairs, n_extra, n_out = len(pairs), len(extras), len(out_dtypes)

    in_specs, operands = [], []
    for (a, b, tk), off, nk in zip(pairs, offs, nks):
        def kidx(k, off=off, nk=nk):
            return k if n_pairs == 1 else jnp.clip(k - off, 0, nk - 1)
        if mode == "tn":
            in_specs.append(pl.BlockSpec((tk, tm), lambda m, n, k, f=kidx: (f(k), m)))
        else:
            in_specs.append(pl.BlockSpec((tm, tk), lambda m, n, k, f=kidx: (m, f(k))))
        if b_quarters and mode == "nn":
            per_q = b.shape[2] // tn
            in_specs.append(pl.BlockSpec((None, tk, tn), lambda m, n, k: (n // per_q, k, n % per_q)))
        elif b_quarters:
            per_q = b.shape[2] // tk
            in_specs.append(pl.BlockSpec((None, tn, tk), lambda m, n, k: (k // per_q, n, k % per_q)))
        elif mode == "nt":
            in_specs.append(pl.BlockSpec((tn, tk), lambda m, n, k, f=kidx: (n, f(k))))
        else:
            in_specs.append(pl.BlockSpec((tk, tn), lambda m, n, k, f=kidx: (f(k), n)))
        operands += [a, b]
    for e in extras:
        in_specs.append(pl.BlockSpec((tm, tn), lambda m, n, k: (m, n)))
        operands.append(e)
    in_specs += [pl.BlockSpec(memory_space=pl.ANY)] * len(deps)
    operands += list(deps)
    first_out = 2 * n_pairs + n_extra + len(deps)
    if out_quarters:
        out_per_q = n_dim // N_CHIPS // tn
        out_dims = (N_CHIPS, m_dim, n_dim // N_CHIPS)
        out_spec = pl.BlockSpec((None, tm, tn), lambda m, n, k: (n // out_per_q, m, n % out_per_q))
    else:
        out_dims = (m_dim, n_dim)
        out_spec = pl.BlockSpec((tm, tn), lambda m, n, k: (m, n))

    def body(*refs):
        ab = refs[:2 * n_pairs]
        e_refs = refs[2 * n_pairs:2 * n_pairs + n_extra]
        o_refs = refs[first_out:first_out + n_out]

        def finish(total):
            vals = (total,) if epilogue is None else epilogue(total, *[e[...] for e in e_refs])
            for o_ref, v in zip(o_refs, vals):
                o_ref[...] = v.astype(o_ref.dtype)

        if nk_total == 1:
            finish(_dot(ab[0][...], ab[1][...], dims))
            return
        acc = refs[-1]
        k = pl.program_id(2)

        @pl.when(k == 0)
        def _():
            acc[...] = jnp.zeros_like(acc)

        for i in range(n_pairs):
            def accumulate(i=i):
                acc[...] += _dot(ab[2 * i][...], ab[2 * i + 1][...], dims)
            if n_pairs == 1:
                accumulate()
            else:
                pl.when((k >= offs[i]) & (k < offs[i] + nks[i]))(accumulate)

        @pl.when(k == nk_total - 1)
        def _():
            finish(acc[...])

    outs = pl.pallas_call(
        body,
        grid=(m_dim // tm, n_dim // tn, nk_total),
        in_specs=in_specs,
        out_specs=[out_spec for _ in out_dtypes],
        out_shape=[jax.ShapeDtypeStruct(out_dims, dt) for dt in out_dtypes],
        scratch_shapes=[pltpu.VMEM((tm, tn), F32)] if nk_total > 1 else [],
        compiler_params=_cparams("parallel", "parallel", "arbitrary"),
        name=name,
    )(*operands)
    return outs[0] if n_out == 1 else outs


def _rowcall(fn, rows, vecs, row_outs, acc_widths, *, name, tr=256, row_cols=None, deps=()):
    s_dim = rows[0].shape[0]
    assert s_dim % tr == 0
    row_cols = row_cols or [None] * len(rows)
    n_r, n_v, n_ro, n_acc = len(rows), len(vecs), len(row_outs), len(acc_widths)
    in_specs = []
    for r, rc in zip(rows, row_cols):
        if rc is None:
            in_specs.append(pl.BlockSpec((tr, r.shape[1]), lambda i: (i, 0)))
        else:
            in_specs.append(pl.BlockSpec((tr, rc[0]), lambda i, c=rc[1]: (i, c)))
    for v in vecs:
        in_specs.append(pl.BlockSpec(v.shape, lambda i, nd=v.ndim: (0,) * nd))
    in_specs += [pl.BlockSpec(memory_space=pl.ANY)] * len(deps)
    n_d = len(deps)

    def body(*refs):
        ins = [r[...] for r in refs[:n_r + n_v]]
        ro = refs[n_r + n_v + n_d:n_r + n_v + n_d + n_ro]
        ao = refs[n_r + n_v + n_d + n_ro:]
        outs = fn(*ins)
        for ref, v in zip(ro, outs[:n_ro]):
            ref[...] = v.astype(ref.dtype)
        if n_acc:
            @pl.when(pl.program_id(0) == 0)
            def _():
                for ref in ao:
                    ref[...] = jnp.zeros_like(ref)
            for ref, v in zip(ao, outs[n_ro:]):
                ref[...] += v

    outs = pl.pallas_call(
        body,
        grid=(s_dim // tr,),
        in_specs=in_specs,
        out_specs=[pl.BlockSpec((tr, w), lambda i: (i, 0)) for w, _ in row_outs]
        + [pl.BlockSpec((1, w), lambda i: (0, 0)) for w in acc_widths],
        out_shape=[jax.ShapeDtypeStruct((s_dim, w), dt) for w, dt in row_outs]
        + [jax.ShapeDtypeStruct((1, w), F32) for w in acc_widths],
        compiler_params=_cparams("arbitrary"),
        name=name,
    )(*rows, *vecs, *deps)
    return outs


def _nrm(x, g):
    r = lax.rsqrt(jnp.mean(x * x, axis=-1, keepdims=True) + EPS)
    n = x * r
    return n * g, n, r


def _nrm_bwd(dy, n, r, g):
    dn = dy * g
    dx = r * (dn - n * jnp.mean(dn * n, axis=-1, keepdims=True))
    return dx, jnp.sum(dy * n, axis=0, keepdims=True)


def _sigmoid(x):
    return 1.0 / (1.0 + jnp.exp(-x))


def _softplus(x):
    return jnp.maximum(x, 0.0) + jnp.log(1.0 + jnp.exp(-jnp.abs(x)))


def _pre_norm(x, g1):
    def fn(xb, g):
        return (_nrm(xb, g)[0],)
    return _rowcall(fn, [x], [g1], [(D_MODEL, BF16)], [], name="pre_norm")[0]


def _post_pre_norm(x, mix, g2, g3):
    def fn(xb, mb, g2b, g3b):
        h1 = xb + _nrm(mb, g2b)[0]
        return h1, _nrm(h1, g3b)[0]
    return _rowcall(fn, [x, mix], [g2, g3], [(D_MODEL, F32), (D_MODEL, BF16)], [], name="post_pre_norm")


def _tail(ff, h1, target, g4):
    def fn(ffb, h1b, tb, g):
        y, n, r = _nrm(ffb, g)
        e = h1b + y - tb
        loss = 0.5 * jnp.sum(jnp.sum(e * e, axis=-1, keepdims=True) * (1.0 / D_MODEL), axis=0, keepdims=True)
        dh2 = e * (1.0 / D_MODEL)
        dff, dg = _nrm_bwd(dh2, n, r, g)
        return dh2, dff, dg, jnp.broadcast_to(loss, (1, LANES))
    return _rowcall(fn, [ff, h1, target], [g4], [(D_MODEL, F32), (D_MODEL, BF16)], [D_MODEL, LANES], name="tail")


def _mid_bwd(du2, h1, dh2, mix, g2, g3, deps=()):
    def fn(du2b, h1b, dh2b, mb, g2b, g3b):
        _, n3, r3 = _nrm(h1b, g3b)
        d3, dg3 = _nrm_bwd(du2b, n3, r3, g3b)
        dh1 = dh2b + d3
        _, n2, r2 = _nrm(mb, g2b)
        dmix, dg2 = _nrm_bwd(dh1, n2, r2, g2b)
        return dh1, dmix, dg3, dg2
    return _rowcall(fn, [du2, h1, dh2, mix], [g2, g3], [(D_MODEL, F32), (D_MODEL, BF16)], [D_MODEL, D_MODEL],
                    name="mid_bwd", deps=deps)


def _first_bwd(du, x, dh1, g1):
    def fn(dub, xb, dh1b, g):
        _, n, r = _nrm(xb, g)
        dx, dg = _nrm_bwd(dub, n, r, g)
        return dh1b + dx, dg
    return _rowcall(fn, [du, x, dh1], [g1], [(D_MODEL, F32)], [D_MODEL], name="first_bwd")


CONV_TILE = 256
CONV_ROWS = 256
PAD = 8


def _conv_taps(w):
    return [w[k:k + 1, :] for k in range(CONV_WIDTH)], w[CONV_WIDTH:CONV_WIDTH + 1, :]


def _conv_fwd(xbc, w8):
    s_dim, c_dim = xbc.shape
    n_steps = s_dim // CONV_ROWS

    def body(x_ref, w_ref, o_ref, xp):
        xp[0:PAD, :] = jnp.zeros((PAD, CONV_TILE), F32)
        xp[PAD:PAD + s_dim, :] = x_ref[...]
        taps, bias = _conv_taps(w_ref[...])

        def step(c, carry):
            base = pl.multiple_of(c * CONV_ROWS, CONV_ROWS)
            win = xp[pl.ds(base, CONV_ROWS + PAD), :]
            pre = bias + taps[3] * win[PAD:, :]
            for j in range(1, CONV_WIDTH):
                pre = pre + taps[3 - j] * pltpu.roll(win, j, axis=0)[PAD:, :]
            o_ref[pl.ds(base, CONV_ROWS), :] = pre * _sigmoid(pre)
            return carry

        lax.fori_loop(0, n_steps, step, 0)

    return pl.pallas_call(
        body,
        grid=(c_dim // CONV_TILE,),
        in_specs=[pl.BlockSpec((s_dim, CONV_TILE), lambda j: (0, j)), pl.BlockSpec((8, CONV_TILE), lambda j: (0, j))],
        out_specs=pl.BlockSpec((s_dim, CONV_TILE), lambda j: (0, j)),
        out_shape=jax.ShapeDtypeStruct((s_dim, c_dim), F32),
        scratch_shapes=[pltpu.VMEM((s_dim + 2 * PAD, CONV_TILE), F32)],
        compiler_params=_cparams("parallel"),
        name="conv_fwd",
    )(xbc, w8)


def _conv_bwd(xbc, w8, dxc):
    s_dim, c_dim = xbc.shape
    n_steps = s_dim // CONV_ROWS

    def body(x_ref, w_ref, d_ref, dx_ref, dw_ref, xp, dp):
        xp[0:PAD, :] = jnp.zeros((PAD, CONV_TILE), F32)
        xp[PAD:PAD + s_dim, :] = x_ref[...]
        dp[PAD + s_dim:, :] = jnp.zeros((PAD, CONV_TILE), F32)
        taps, bias = _conv_taps(w_ref[...])

        def step1(c, sums):
            base = pl.multiple_of(c * CONV_ROWS, CONV_ROWS)
            win = xp[pl.ds(base, CONV_ROWS + PAD), :]
            shifted = [win[PAD:, :]] + [pltpu.roll(win, j, axis=0)[PAD:, :] for j in range(1, CONV_WIDTH)]
            pre = bias
            for j in range(CONV_WIDTH):
                pre = pre + taps[3 - j] * shifted[j]
            sg = _sigmoid(pre)
            dpre = d_ref[pl.ds(base, CONV_ROWS), :] * (sg * (1.0 + pre * (1.0 - sg)))
            dp[pl.ds(base + PAD, CONV_ROWS), :] = dpre
            new = [sums[k] + jnp.sum(dpre * shifted[3 - k], axis=0, keepdims=True) for k in range(CONV_WIDTH)]
            new.append(sums[CONV_WIDTH] + jnp.sum(dpre, axis=0, keepdims=True))
            return tuple(new)

        zero = jnp.zeros((1, CONV_TILE), F32)
        sums = lax.fori_loop(0, n_steps, step1, (zero,) * (CONV_WIDTH + 1))
        dw_ref[...] = jnp.zeros((8, CONV_TILE), F32)
        for k in range(CONV_WIDTH + 1):
            dw_ref[k:k + 1, :] = sums[k]

        def step2(c, carry):
            base = pl.multiple_of(c * CONV_ROWS, CONV_ROWS)
            win = dp[pl.ds(base + PAD, CONV_ROWS + PAD), :]
            dx = taps[3] * win[:CONV_ROWS, :]
            for j in range(1, CONV_WIDTH):
                dx = dx + taps[3 - j] * pltpu.roll(win, CONV_ROWS + PAD - j, axis=0)[:CONV_ROWS, :]
            dx_ref[pl.ds(base, CONV_ROWS), :] = dx.astype(BF16)
            return carry

        lax.fori_loop(0, n_steps, step2, 0)

    col = lambda j: (0, j)
    return pl.pallas_call(
        body,
        grid=(c_dim // CONV_TILE,),
        in_specs=[pl.BlockSpec((s_dim, CONV_TILE), col), pl.BlockSpec((8, CONV_TILE), col),
                  pl.BlockSpec((s_dim, CONV_TILE), col)],
        out_specs=[pl.BlockSpec((s_dim, CONV_TILE), col), pl.BlockSpec((8, CONV_TILE), col)],
        out_shape=[jax.ShapeDtypeStruct((s_dim, c_dim), BF16), jax.ShapeDtypeStruct((8, c_dim), F32)],
        scratch_shapes=[pltpu.VMEM((s_dim + 2 * PAD, CONV_TILE), F32), pltpu.VMEM((s_dim + 2 * PAD, CONV_TILE), F32)],
        compiler_params=_cparams("parallel"),
        name="conv_bwd",
    )(xbc, w8, dxc)


def _perm_cols(a):
    parts = []
    for g in range(SSM_GROUPS):
        parts += [a[..., g * GROUP_X:(g + 1) * GROUP_X],
                  a[..., D_SSM + g * D_STATE:D_SSM + (g + 1) * D_STATE],
                  a[..., D_SSM + SSM_GROUPS * D_STATE + g * D_STATE:D_SSM + SSM_GROUPS * D_STATE + (g + 1) * D_STATE]]
    return jnp.concatenate(parts, axis=-1)


def _unperm_cols(a):
    xs = [a[..., g * GROUP_COLS:g * GROUP_COLS + GROUP_X] for g in range(SSM_GROUPS)]
    bs = [a[..., g * GROUP_COLS + GROUP_X:g * GROUP_COLS + GROUP_X + D_STATE] for g in range(SSM_GROUPS)]
    cs = [a[..., g * GROUP_COLS + GROUP_X + D_STATE:(g + 1) * GROUP_COLS] for g in range(SSM_GROUPS)]
    return jnp.concatenate(xs + bs + cs, axis=-1)


def _dt_to_groups(dt):
    s_dim = dt.shape[0]
    t = dt[:, :SSM_HEADS].reshape(s_dim, SSM_GROUPS, HEADS_PER_GROUP).transpose(1, 0, 2)
    return jnp.pad(t, ((0, 0), (0, 0), (0, LANES - HEADS_PER_GROUP)))


def _dt_from_groups(dtg):
    s_dim = dtg.shape[1]
    return dtg[:, :, :HEADS_PER_GROUP].transpose(1, 0, 2).reshape(s_dim, SSM_HEADS)


def _pack_ssd_params(dt_bias, a_log, d_skip):
    rows = jnp.stack([p.reshape(SSM_GROUPS, HEADS_PER_GROUP) for p in (dt_bias, a_log, d_skip)], axis=1)
    return jnp.pad(rows, ((0, 0), (0, 8 - 3), (0, LANES - HEADS_PER_GROUP)))


def _unpack_ssd_params(par):
    return tuple(par[:, k, :HEADS_PER_GROUP].reshape(SSM_HEADS) for k in range(3))


Q = SSD_CHUNK


def _split3(v):
    hi = v.astype(BF16)
    r1 = v - hi.astype(F32)
    mid = r1.astype(BF16)
    lo = (r1 - mid.astype(F32)).astype(BF16)
    return hi, mid, lo


def _dot_l01(t01, v):
    return sum(_dot(t01, p) for p in _split3(v))


def _dot_r01(v, e01):
    return sum(_dot(p, e01) for p in _split3(v))


def _ssd_consts():
    row = lax.broadcasted_iota(jnp.int32, (Q, Q), 0)
    col = lax.broadcasted_iota(jnp.int32, (Q, Q), 1)
    causal = row >= col
    tril = causal.astype(BF16)
    triu = (col >= row).astype(BF16)
    er = lax.broadcasted_iota(jnp.int32, (LANES, GROUP_X), 0)
    ec = lax.broadcasted_iota(jnp.int32, (LANES, GROUP_X), 1) // SSM_HEAD_DIM
    expand = (er == ec).astype(BF16)
    rr = lax.broadcasted_iota(jnp.int32, (GROUP_X, LANES), 0) // SSM_HEAD_DIM
    rc = lax.broadcasted_iota(jnp.int32, (GROUP_X, LANES), 1)
    reduce = (rr == rc).astype(BF16)
    lane_head = lax.broadcasted_iota(jnp.int32, (Q, GROUP_X), 1) // SSM_HEAD_DIM
    return causal, tril, triu, expand, reduce, lane_head


def _ssd_common(xc_ref, dt_ref, par_ref, consts):
    causal, tril, _, expand, _, _ = consts
    par = par_ref[...]
    bias, alog, dsk = par[0:1, :], par[1:2, :], par[2:3, :]
    a_neg = -jnp.exp(alog)
    dtr = dt_ref[...] + bias
    dt = _softplus(dtr)
    s = _dot_l01(tril, dt * a_neg)
    dt_x = _dot_r01(dt, expand)
    s_x = _dot_r01(s, expand)
    dsk_x = _dot_r01(jnp.broadcast_to(dsk, (8, LANES)), expand)[0:1, :]
    blk = xc_ref[...]
    x = blk[:, :GROUP_X]
    bm = blk[:, GROUP_X:GROUP_X + D_STATE].astype(BF16)
    cm = blk[:, GROUP_X + D_STATE:].astype(BF16)
    xdt = x * dt_x
    g = _dot(cm, bm, _NT)
    return dict(a_neg=a_neg, dtr=dtr, dt=dt, s=s, s_t=s.T, dt_x=dt_x, s_x=s_x, dsk_x=dsk_x, x=x, bm=bm, cm=cm,
                xdt=xdt, g=g)


def _decay(v, r, causal):
    diff = v["s"][:, r:r + 1] - v["s_t"][r:r + 1, :]
    return jnp.exp(jnp.where(causal, diff, -jnp.inf))


def _ssd_specs(n_chunks, rev):
    cidx = (lambda c: n_chunks - 1 - c) if rev else (lambda c: c)
    xc = pl.BlockSpec((Q, GROUP_COLS), lambda g, c: (cidx(c), g))
    gx = pl.BlockSpec((Q, GROUP_X), lambda g, c: (cidx(c), g))
    dt = pl.BlockSpec((None, Q, LANES), lambda g, c: (g, cidx(c), 0))
    par = pl.BlockSpec((None, 8, LANES), lambda g, c: (g, 0, 0))
    nw = pl.BlockSpec((1, GROUP_X), lambda g, c: (0, g))
    hs = pl.BlockSpec((None, None, D_STATE, GROUP_X), lambda g, c: (cidx(c), g, 0, 0))
    return xc, gx, dt, par, nw, hs


def _ssd_fwd(xc, z, dtg, par, nw):
    s_dim = xc.shape[0]
    n_chunks = s_dim // Q
    xc_s, gx_s, dt_s, par_s, nw_s, hs_s = _ssd_specs(n_chunks, False)

    def body(xc_ref, z_ref, dt_ref, par_ref, nw_ref, y_ref, ys_ref, hs_ref, ht):
        @pl.when(pl.program_id(1) == 0)
        def _():
            ht[...] = jnp.zeros_like(ht)

        consts = _ssd_consts()
        causal, lane_head = consts[0], consts[5]
        v = _ssd_common(xc_ref, dt_ref, par_ref, consts)
        xdt_b = v["xdt"].astype(BF16)
        yd = jnp.zeros((Q, GROUP_X), F32)
        for r in range(HEADS_PER_GROUP):
            m = (v["g"] * _decay(v, r, causal)).astype(BF16)
            yd = yd + _dot(m, jnp.where(lane_head == r, xdt_b, jnp.zeros_like(xdt_b)))
        h = ht[...]
        hs_ref[...] = h
        yo = jnp.exp(v["s_x"]) * _dot(v["cm"], h.astype(BF16))
        y = yd + yo + v["dsk_x"] * v["x"]
        s_last = v["s_x"][Q - 1:Q, :]
        snew = _dot(v["bm"], (v["xdt"] * jnp.exp(s_last - v["s_x"])).astype(BF16), _TN)
        ht[...] = jnp.exp(s_last) * h + snew
        zz = z_ref[...]
        yg = y * (zz * _sigmoid(zz))
        y_ref[...] = y
        ys_ref[...] = _nrm(yg, nw_ref[...])[0].astype(BF16)

    return pl.pallas_call(
        body,
        grid=(SSM_GROUPS, n_chunks),
        in_specs=[xc_s, gx_s, dt_s, par_s, nw_s],
        out_specs=[gx_s, gx_s, hs_s],
        out_shape=[jax.ShapeDtypeStruct((s_dim, D_SSM), F32), jax.ShapeDtypeStruct((s_dim, D_SSM), BF16),
                   jax.ShapeDtypeStruct((n_chunks, SSM_GROUPS, D_STATE, GROUP_X), F32)],
        scratch_shapes=[pltpu.VMEM((D_STATE, GROUP_X), F32)],
        compiler_params=_cparams("parallel", "arbitrary"),
        name="ssd_fwd",
    )(xc, z, dtg, par, nw)


def _ssd_bwd(xc, z, dtg, par, nw, y, hs, dymix):
    s_dim = xc.shape[0]
    n_chunks = s_dim // Q
    xc_s, gx_s, dt_s, par_s, nw_s, hs_s = _ssd_specs(n_chunks, True)

    def body(xc_ref, z_ref, dt_ref, par_ref, nw_ref, y_ref, hs_ref, dys_ref,
             dxc_ref, dz_ref, ddt_ref, dpar_ref, dnw_ref, dht):
        @pl.when(pl.program_id(1) == 0)
        def _():
            dht[...] = jnp.zeros_like(dht)
            dpar_ref[...] = jnp.zeros_like(dpar_ref)
            dnw_ref[...] = jnp.zeros_like(dnw_ref)

        consts = _ssd_consts()
        causal, _, triu, _, reduce, lane_head = consts
        v = _ssd_common(xc_ref, dt_ref, par_ref, consts)
        x, bm, cm, xdt, s_x = v["x"], v["bm"], v["cm"], v["xdt"], v["s_x"]
        h = hs_ref[...]
        hb = h.astype(BF16)
        es_x = jnp.exp(s_x)
        yo = es_x * _dot(cm, hb)
        s_last = s_x[Q - 1:Q, :]
        e_x = jnp.exp(s_last - s_x)
        es_last = jnp.exp(s_last)

        yv, zz, nw_v = y_ref[...], z_ref[...], nw_ref[...]
        sg = _sigmoid(zz)
        gz = zz * sg
        _, n, rstd = _nrm(yv * gz, nw_v)
        dout = dys_ref[...]
        dyg, dnw = _nrm_bwd(dout, n, rstd, nw_v)
        dnw_ref[...] += dnw
        dy = dyg * gz
        dz_ref[...] = (dyg * yv * (sg * (1.0 + zz * (1.0 - sg)))).astype(BF16)

        dyb = dy.astype(BF16)
        xdt_b = xdt.astype(BF16)
        dhp = dht[...]
        dhpb = dhp.astype(BF16)
        lane = lax.broadcasted_iota(jnp.int32, (Q, LANES), 1)
        sub = lax.broadcasted_iota(jnp.int32, (LANES, Q), 0)
        dxdt = jnp.zeros((Q, GROUP_X), F32)
        dg = jnp.zeros((Q, Q), F32)
        ds = jnp.zeros((Q, LANES), F32)
        ds_t = jnp.zeros((LANES, Q), F32)
        for r in range(HEADS_PER_GROUP):
            dec = _decay(v, r, causal)
            mf = v["g"] * dec
            dyr = jnp.where(lane_head == r, dyb, jnp.zeros_like(dyb))
            dm = _dot(dyr, xdt_b, _NT)
            dxdt = dxdt + _dot(mf.astype(BF16), dyr, _TN)
            dg = dg + dm * dec
            dd = dm * mf
            ds = ds + jnp.where(lane == r, jnp.sum(dd, axis=1, keepdims=True), 0.0)
            ds_t = ds_t + jnp.where(sub == r, jnp.sum(dd, axis=0, keepdims=True), 0.0)
        ds = ds - ds_t.T
        dgb = dg.astype(BF16)
        dwb = (es_x * dy).astype(BF16)
        dcm = _dot(dgb, bm) + _dot(dwb, hb, _NT)
        dh_prev = _dot(cm, dwb, _TN)
        zst = _dot(bm, dhpb)
        xe = xdt * e_x
        dxdt = dxdt + e_x * zst
        dee = xe * zst
        dbm = _dot(dgb, cm, _TN) + _dot(xe.astype(BF16), dhpb, _NT)
        v_last = jnp.sum(dee, axis=0, keepdims=True) + es_last * jnp.sum(dhp * h, axis=0, keepdims=True)
        row_x = lax.broadcasted_iota(jnp.int32, (Q, GROUP_X), 0)
        tx = dy * yo - dee + jnp.where(row_x == Q - 1, v_last, 0.0)
        ds = ds + _dot_r01(tx, reduce)
        ddta = _dot_l01(triu, ds)
        ddt = ddta * v["a_neg"] + _dot_r01(dxdt * x, reduce)
        dalog = jnp.sum(ddta * v["dt"], axis=0, keepdims=True) * v["a_neg"]
        draw = jnp.where(lane < HEADS_PER_GROUP, ddt * _sigmoid(v["dtr"]), 0.0)
        dbias = jnp.sum(draw, axis=0, keepdims=True)
        ddsk = _dot_r01(jnp.broadcast_to(jnp.sum(dy * x, axis=0, keepdims=True), (8, GROUP_X)), reduce)[0:1, :]
        dht[...] = es_last * dhp + dh_prev
        dxc_ref[:, :GROUP_X] = dxdt * v["dt_x"] + v["dsk_x"] * dy
        dxc_ref[:, GROUP_X:GROUP_X + D_STATE] = dbm
        dxc_ref[:, GROUP_X + D_STATE:] = dcm
        ddt_ref[...] = draw
        dpar_ref[0:1, :] += dbias
        dpar_ref[1:2, :] += dalog
        dpar_ref[2:3, :] += ddsk

    return pl.pallas_call(
        body,
        grid=(SSM_GROUPS, n_chunks),
        in_specs=[xc_s, gx_s, dt_s, par_s, nw_s, gx_s, hs_s, gx_s],
        out_specs=[xc_s, gx_s, dt_s, par_s, nw_s],
        out_shape=[jax.ShapeDtypeStruct((s_dim, SSM_GROUPS * GROUP_COLS), F32),
                   jax.ShapeDtypeStruct((s_dim, D_SSM), BF16),
                   jax.ShapeDtypeStruct((SSM_GROUPS, s_dim, LANES), F32),
                   jax.ShapeDtypeStruct((SSM_GROUPS, 8, LANES), F32),
                   jax.ShapeDtypeStruct((1, D_SSM), F32)],
        scratch_shapes=[pltpu.VMEM((D_STATE, GROUP_X), F32)],
        compiler_params=_cparams("parallel", "arbitrary"),
        name="ssd_bwd",
    )(xc, z, dtg, par, nw, y, hs, dymix)


ATT_SCALE = ATT_HEAD_DIM ** -0.5
NEG_INF = -jnp.inf


def _head(h):
    return slice(h * ATT_HEAD_DIM, (h + 1) * ATT_HEAD_DIM)


def _band_masks():
    qi = lax.broadcasted_iota(jnp.int32, (ATT_BLOCK, ATT_BLOCK), 0)
    kj = lax.broadcasted_iota(jnp.int32, (ATT_BLOCK, ATT_BLOCK), 1)
    return kj <= qi, kj >= qi


def _attn_fwd(qkv_v, d):
    rows = qkv_v.shape[0]
    nb = rows // ATT_BLOCK
    blk = (ATT_BLOCK, D_ATT)
    prev = lambda i: jnp.maximum(i - 1, 0)

    def body(q_ref, kc_ref, kp_ref, vc_ref, vp_ref, o_ref, lse_ref):
        own, before = _band_masks()
        before = before & (pl.program_id(1) > 0)
        lane = lax.broadcasted_iota(jnp.int32, (ATT_BLOCK, LANES), 1)
        lse_all = jnp.zeros((ATT_BLOCK, LANES), F32)
        for h in range(ATT_HEADS):
            q = q_ref[:, _head(h)]
            sc = jnp.where(own, _dot(q, kc_ref[:, _head(h)], _NT) * ATT_SCALE, NEG_INF)
            sp = jnp.where(before, _dot(q, kp_ref[:, _head(h)], _NT) * ATT_SCALE, NEG_INF)
            m = jnp.maximum(jnp.max(sc, axis=1, keepdims=True), jnp.max(sp, axis=1, keepdims=True))
            pc, pp = jnp.exp(sc - m), jnp.exp(sp - m)
            den = jnp.sum(pc, axis=1, keepdims=True) + jnp.sum(pp, axis=1, keepdims=True)
            o = _dot(pc.astype(BF16), vc_ref[:, _head(h)]) + _dot(pp.astype(BF16), vp_ref[:, _head(h)])
            o_ref[:, _head(h)] = o / den
            lse_all = jnp.where(lane == h, m + jnp.log(den), lse_all)
        lse_ref[...] = lse_all

    return pl.pallas_call(
        body,
        grid=(d, nb),
        in_specs=[pl.BlockSpec(blk, lambda r, i: (i, 3 * r)),
                  pl.BlockSpec(blk, lambda r, i: (i, 3 * r + 1)),
                  pl.BlockSpec(blk, lambda r, i: (prev(i), 3 * r + 1)),
                  pl.BlockSpec(blk, lambda r, i: (i, 3 * r + 2)),
                  pl.BlockSpec(blk, lambda r, i: (prev(i), 3 * r + 2))],
        out_specs=[pl.BlockSpec(blk, lambda r, i: (i, r)), pl.BlockSpec((ATT_BLOCK, LANES), lambda r, i: (i, r))],
        out_shape=[jax.ShapeDtypeStruct((rows, d * D_ATT), F32), jax.ShapeDtypeStruct((rows, d * LANES), F32)],
        compiler_params=_cparams("parallel", "arbitrary"),
        name=f"attn_fwd_d{d}",
    )(qkv_v, qkv_v, qkv_v, qkv_v, qkv_v)


def _attn_combine(os_, lses):
    def fn(o1, o2, o3, l1, l2, l3):
        m = jnp.maximum(jnp.maximum(l1, l2), l3)
        tot = m + jnp.log(jnp.exp(l1 - m) + jnp.exp(l2 - m) + jnp.exp(l3 - m))
        w1, w2, w3 = jnp.exp(l1 - tot), jnp.exp(l2 - tot), jnp.exp(l3 - tot)
        cols = []
        for h in range(ATT_HEADS):
            cols.append(w1[:, h:h + 1] * o1[:, _head(h)] + w2[:, h:h + 1] * o2[:, _head(h)]
                        + w3[:, h:h + 1] * o3[:, _head(h)])
        y = jnp.concatenate(cols, axis=1)
        return y, y, tot
    return _rowcall(fn, list(os_) + list(lses), [], [(D_ATT, BF16), (D_ATT, F32), (LANES, F32)], [],
                    name="attn_combine", tr=128)


def _attn_delta(dymix, y_att):
    def fn(dy, y):
        lane = lax.broadcasted_iota(jnp.int32, (dy.shape[0], LANES), 1)
        delta = jnp.zeros((dy.shape[0], LANES), F32)
        for h in range(ATT_HEADS):
            delta = jnp.where(lane == h, jnp.sum(dy[:, _head(h)] * y[:, _head(h)], axis=1, keepdims=True), delta)
        return dy, delta
    return _rowcall(fn, [dymix, y_att], [], [(D_ATT, BF16), (LANES, F32)], [], name="attn_delta",
                    row_cols=[(D_ATT, 1), None])


def _attn_bwd(qkv_v, dy_v, lse_v, delta_v, d):
    rows = qkv_v.shape[0]
    nb = rows // ATT_BLOCK
    blk = (ATT_BLOCK, D_ATT)
    sblk = (ATT_BLOCK, LANES)
    prev = lambda i: jnp.maximum(i - 1, 0)
    nxt = lambda i: jnp.minimum(i + 1, nb - 1)

    def body(qc_ref, qn_ref, kc_ref, kp_ref, vc_ref, vp_ref, dyc_ref, dyn_ref, lc_ref, ln_ref, dc_ref, dn_ref,
             dq_ref, dk_ref, dv_ref):
        i = pl.program_id(1)
        own, before = _band_masks()
        before_c = before & (i > 0)
        before_n = before & (i < nb - 1)
        lc, ln, dc, dn = lc_ref[...], ln_ref[...], dc_ref[...], dn_ref[...]
        for h in range(ATT_HEADS):
            hs = _head(h)
            q, qn, kc, kp, vc, vp = qc_ref[:, hs], qn_ref[:, hs], kc_ref[:, hs], kp_ref[:, hs], vc_ref[:, hs], vp_ref[:, hs]
            dy, dyn = dyc_ref[:, hs], dyn_ref[:, hs]
            lse, lse_n, dl, dl_n = lc[:, h:h + 1], ln[:, h:h + 1], dc[:, h:h + 1], dn[:, h:h + 1]
            pc = jnp.exp(jnp.where(own, _dot(q, kc, _NT) * ATT_SCALE - lse, NEG_INF))
            pp = jnp.exp(jnp.where(before_c, _dot(q, kp, _NT) * ATT_SCALE - lse, NEG_INF))
            pn = jnp.exp(jnp.where(before_n, _dot(qn, kc, _NT) * ATT_SCALE - lse_n, NEG_INF))
            dsc = (pc * (_dot(dy, vc, _NT) - dl)).astype(BF16)
            dsp = (pp * (_dot(dy, vp, _NT) - dl)).astype(BF16)
            dsn = (pn * (_dot(dyn, vc, _NT) - dl_n)).astype(BF16)
            dq_ref[:, hs] = (_dot(dsc, kc) + _dot(dsp, kp)) * ATT_SCALE
            dk_ref[:, hs] = (_dot(dsc, q, _TN) + _dot(dsn, qn, _TN)) * ATT_SCALE
            dv_ref[:, hs] = _dot(pc.astype(BF16), dy, _TN) + _dot(pn.astype(BF16), dyn, _TN)

    return pl.pallas_call(
        body,
        grid=(d, nb),
        in_specs=[pl.BlockSpec(blk, lambda r, i: (i, 3 * r)), pl.BlockSpec(blk, lambda r, i: (nxt(i), 3 * r)),
                  pl.BlockSpec(blk, lambda r, i: (i, 3 * r + 1)), pl.BlockSpec(blk, lambda r, i: (prev(i), 3 * r + 1)),
                  pl.BlockSpec(blk, lambda r, i: (i, 3 * r + 2)), pl.BlockSpec(blk, lambda r, i: (prev(i), 3 * r + 2)),
                  pl.BlockSpec(blk, lambda r, i: (i, r)), pl.BlockSpec(blk, lambda r, i: (nxt(i), r)),
                  pl.BlockSpec(sblk, lambda r, i: (i, r)), pl.BlockSpec(sblk, lambda r, i: (nxt(i), r)),
                  pl.BlockSpec(sblk, lambda r, i: (i, r)), pl.BlockSpec(sblk, lambda r, i: (nxt(i), r))],
        out_specs=[pl.BlockSpec(blk, lambda r, i: (i, r))] * 3,
        out_shape=[jax.ShapeDtypeStruct((rows, d * D_ATT), F32)] * 3,
        compiler_params=_cparams("parallel", "arbitrary"),
        name=f"attn_bwd_d{d}",
    )(qkv_v, qkv_v, qkv_v, qkv_v, qkv_v, qkv_v, dy_v, dy_v, lse_v, lse_v, delta_v, delta_v)


def _attn_sum(dqs, dks, dvs, deps=()):
    def fn(*parts):
        return (jnp.concatenate([parts[0] + parts[1] + parts[2], parts[3] + parts[4] + parts[5],
                                 parts[6] + parts[7] + parts[8]], axis=1),)
    return _rowcall(fn, list(dqs) + list(dks) + list(dvs), [], [(3 * D_ATT, BF16)], [], name="attn_sum", tr=128,
                    deps=deps)[0]


def _attention_fwd(qkv):
    s_dim = qkv.shape[0]
    os_, lses = [], []
    for d in DILATIONS:
        o, lse = _attn_fwd(qkv.reshape(s_dim // d, d * 3 * D_ATT), d)
        os_.append(o.reshape(s_dim, D_ATT))
        lses.append(lse.reshape(s_dim, LANES))
    return _attn_combine(os_, lses)


def _attention_bwd(qkv, dymix, y_att, lse, sum_deps=()):
    s_dim = qkv.shape[0]
    dy, delta = _attn_delta(dymix, y_att)
    dqs, dks, dvs = [], [], []
    for d in DILATIONS:
        dq, dk, dv = _attn_bwd(qkv.reshape(s_dim // d, d * 3 * D_ATT), dy.reshape(s_dim // d, d * D_ATT),
                               lse.reshape(s_dim // d, d * LANES), delta.reshape(s_dim // d, d * LANES), d)
        dqs.append(dq.reshape(s_dim, D_ATT))
        dks.append(dk.reshape(s_dim, D_ATT))
        dvs.append(dv.reshape(s_dim, D_ATT))
    return _attn_sum(dqs, dks, dvs, sum_deps)


def _adamw(w, g, m, v, name):
    def fn(wb, gb, mb, vb):
        m2 = ADAM_B1 * mb + (1.0 - ADAM_B1) * gb
        v2 = ADAM_B2 * vb + (1.0 - ADAM_B2) * (gb * gb)
        m_hat = m2 / (1.0 - ADAM_B1 ** ADAM_STEP)
        v_hat = v2 / (1.0 - ADAM_B2 ** ADAM_STEP)
        delta = -ADAM_LR * (m_hat / (jnp.sqrt(v_hat) + ADAM_EPS) + ADAM_WD * wb)
        return delta, m2, v2
    cols = w.shape[1]
    tr = 128 if w.shape[0] % 128 == 0 else w.shape[0]
    return _rowcall(fn, [w, g, m, v], [], [(cols, F32)] * 3, [], name=name, tr=tr)


ANY = pl.BlockSpec(memory_space=pl.ANY)


def _position():
    x, y, c = lax.axis_index("x"), lax.axis_index("y"), lax.axis_index("c")
    chips = [(1 - x, y), (x, 1 - y), (1 - x, 1 - y)]
    return x, y, c, chips


def _remote(src, dst, send_sem, recv_sem, device):
    return pltpu.make_async_remote_copy(src_ref=src, dst_ref=dst, send_sem=send_sem, recv_sem=recv_sem,
                                        device_id=device, device_id_type=MESH)


def _gather_shards(shards):
    n = len(shards)

    def body(*refs):
        ins, outs = refs[:n], refs[n:2 * n]
        send_sems, recv_sems = refs[2 * n:]
        x, y, c, chips = _position()
        sibling = (x, y, 1 - c)

        def half(a, j, cc):
            h = ins[a].shape[0] // 2
            return outs[a].at[j, pl.ds(cc * h, h), :]

        sent = []
        for a in range(n):
            h = ins[a].shape[0] // 2
            for j, chip in enumerate(chips):
                cp = _remote(ins[a].at[pl.ds(c * h, h), :], half(a, j, c), send_sems.at[6 * a + j],
                             recv_sems.at[6 * a + j], (chip[0], chip[1], c))
                cp.start()
                sent.append(cp)
        for a in range(n):
            for j in range(3):
                landed = half(a, j, c)
                _remote(landed, landed, send_sems.at[6 * a + j], recv_sems.at[6 * a + j], (x, y, c)).wait_recv()
                cp = _remote(landed, landed, send_sems.at[6 * a + 3 + j], recv_sems.at[6 * a + 3 + j], sibling)
                cp.start()
                sent.append(cp)
        for a in range(n):
            for j in range(3):
                handed = half(a, j, 1 - c)
                _remote(handed, handed, send_sems.at[6 * a + 3 + j], recv_sems.at[6 * a + 3 + j], (x, y, c)).wait_recv()
        for cp in sent:
            cp.wait_send()

    return pl.pallas_call(
        body,
        in_specs=[ANY] * n,
        out_specs=[ANY] * n,
        out_shape=[jax.ShapeDtypeStruct((3,) + s.shape, s.dtype) for s in shards],
        scratch_shapes=[pltpu.SemaphoreType.DMA((6 * n,)), pltpu.SemaphoreType.DMA((6 * n,))],
        name="gather_shards",
    )(*shards)


def _handshake(peers):
    barrier = pltpu.get_barrier_semaphore()
    for p in peers:
        pl.semaphore_signal(barrier, inc=1, device_id=p, device_id_type=MESH)
    pl.semaphore_wait(barrier, len(peers))


def _gather_shards_async(shards, collective_id, name):
    n = len(shards)
    srcs = [jax.new_ref(s, memory_space=pltpu.MemorySpace.HBM) for s in shards]
    dsts = [jax.empty_ref(jax.ShapeDtypeStruct((3,) + s.shape, s.dtype), memory_space=pltpu.MemorySpace.HBM)
            for s in shards]

    @pl.kernel(mesh=plsc.ScalarSubcoreMesh(axis_name="seq", num_cores=1), name=name,
               scratch_types=(pltpu.SemaphoreType.DMA((6 * n,)), pltpu.SemaphoreType.DMA((6 * n,))),
               compiler_params=pltpu.CompilerParams(collective_id=collective_id))
    def launch(send_sems, recv_sems):
        x, y, c, chips = _position()
        sibling = (x, y, 1 - c)
        _handshake([(chip[0], chip[1], c) for chip in chips] + [sibling])

        def half(a, j, cc):
            h = shards[a].shape[0] // 2
            return dsts[a].at[j, pl.ds(cc * h, h), :]

        sent = []
        for a in range(n):
            h = shards[a].shape[0] // 2
            for j, chip in enumerate(chips):
                cp = _remote(srcs[a].at[pl.ds(c * h, h), :], half(a, j, c), send_sems.at[6 * a + j],
                             recv_sems.at[6 * a + j], (chip[0], chip[1], c))
                cp.start()
                sent.append(cp)
        for a in range(n):
            for j in range(3):
                landed = half(a, j, c)
                _remote(landed, landed, send_sems.at[6 * a + j], recv_sems.at[6 * a + j], (x, y, c)).wait_recv()
                cp = _remote(landed, landed, send_sems.at[6 * a + 3 + j], recv_sems.at[6 * a + 3 + j], sibling)
                cp.start()
                sent.append(cp)
        for a in range(n):
            for j in range(3):
                handed = half(a, j, 1 - c)
                _remote(handed, handed, send_sems.at[6 * a + 3 + j], recv_sems.at[6 * a + 3 + j], (x, y, c)).wait_recv()
        for cp in sent:
            cp.wait_send()

    launch()
    return [d[...] for d in dsts]


IN_COLS = {"z": (0, D_SSM), "xbc": (D_SSM, D_SSM + D_XBC), "dt": (D_SSM + D_XBC, D_SSM + D_XBC + SSM_HEADS),
           "qkv": (D_SSM + D_XBC + SSM_HEADS, D_IN_PROJ)}


def _cols_from_quarters(quarters, lo, hi):
    parts = []
    for q in range(N_CHIPS):
        a, b = max(lo, q * W_IN_SHARD), min(hi, (q + 1) * W_IN_SHARD)
        if a < b:
            parts.append(quarters[q][:, a - q * W_IN_SHARD:b - q * W_IN_SHARD])
    return parts[0] if len(parts) == 1 else jnp.concatenate(parts, axis=1)


def _quarters_from_cols(pieces):
    quarters = []
    for q in range(N_CHIPS):
        parts = []
        for name, (lo, hi) in IN_COLS.items():
            a, b = max(lo, q * W_IN_SHARD), min(hi, (q + 1) * W_IN_SHARD)
            if a < b:
                parts.append(pieces[name][:, a - lo:b - lo])
        quarters.append(jnp.concatenate(parts, axis=1))
    return jnp.stack(quarters)


def _by_chip(own, others):
    me = 2 * lax.axis_index("x") + lax.axis_index("y")
    rel = jnp.stack([own, others[1], others[0], others[2]])
    return jnp.stack([lax.dynamic_index_in_dim(rel, q ^ me, 0, keepdims=False) for q in range(N_CHIPS)])


def _add_sibling(grad, got, c_arr, name, deps=()):
    nq, rows, cols = grad.shape
    h = rows // 2
    tr = 128
    nb = h // tr

    def body(c_ref, a_ref, b_ref, *rest):
        o_ref, ob_ref = rest[len(deps):]
        total = a_ref[...] + b_ref[...]
        o_ref[...] = total
        ob_ref[...] = total.astype(BF16)

    out_spec = pl.BlockSpec((None, tr, cols), lambda q, i, c: (q, i, 0))
    return pl.pallas_call(
        body,
        grid_spec=pltpu.PrefetchScalarGridSpec(
            num_scalar_prefetch=1, grid=(nq, nb),
            in_specs=[pl.BlockSpec((None, tr, cols), lambda q, i, c: (q, c[0] * nb + i, 0)),
                      pl.BlockSpec((None, tr, cols), lambda q, i, c: (q, i, 0))] + [ANY] * len(deps),
            out_specs=[out_spec, out_spec]),
        out_shape=[jax.ShapeDtypeStruct((nq, h, cols), F32), jax.ShapeDtypeStruct((nq, h, cols), BF16)],
        compiler_params=_cparams("parallel", "parallel"),
        name=name,
    )(c_arr, grad, got, *deps)


def _add_chips(part, got, chip_arr, name, deps=()):
    _, h, cols = part.shape
    tr = 128

    def body(q_ref, p_ref, g0_ref, g1_ref, g2_ref, *rest):
        o_ref = rest[len(deps)]
        o_ref[...] = ((p_ref[...] + g0_ref[...].astype(F32)) + g1_ref[...].astype(F32)) + g2_ref[...].astype(F32)

    got_spec = lambda j: pl.BlockSpec((None, tr, cols), lambda i, q: (j, i, 0))
    return pl.pallas_call(
        body,
        grid_spec=pltpu.PrefetchScalarGridSpec(
            num_scalar_prefetch=1, grid=(h // tr,),
            in_specs=[pl.BlockSpec((None, tr, cols), lambda i, q: (q[0], i, 0)), got_spec(0), got_spec(1), got_spec(2)]
            + [ANY] * len(deps),
            out_specs=pl.BlockSpec((tr, cols), lambda i, q: (i, 0))),
        out_shape=jax.ShapeDtypeStruct((h, cols), F32),
        compiler_params=_cparams("parallel"),
        name=name,
    )(chip_arr, part, got, got, got, *deps)


def _sequencer_exchange(src, out_shape, collective_id, name, plan, n_copies):
    src_ref = jax.new_ref(src, memory_space=pltpu.MemorySpace.HBM)
    dst_ref = jax.empty_ref(out_shape, memory_space=pltpu.MemorySpace.HBM)

    @pl.kernel(mesh=plsc.ScalarSubcoreMesh(axis_name="seq", num_cores=1), name=name,
               scratch_types=(pltpu.SemaphoreType.DMA((n_copies,)), pltpu.SemaphoreType.DMA((n_copies,))),
               compiler_params=pltpu.CompilerParams(collective_id=collective_id))
    def launch(send_sems, recv_sems):
        x, y, c, chips = _position()
        copies = plan(src_ref, dst_ref, x, y, c, chips)
        _handshake([peer for _, _, peer in copies])
        started = []
        for k, (s, d, peer) in enumerate(copies):
            cp = _remote(s, d, send_sems.at[k], recv_sems.at[k], peer)
            cp.start()
            started.append(cp)
        for cp in started:
            cp.wait()

    launch()
    return dst_ref[...]


class _AsyncReduceScatter:
    def __init__(self, grad, nm, first_id):
        self.grad, self.nm, self.first_id = grad, nm, first_id
        nq, rows, cols = grad.shape
        h = self.h = rows // 2

        def to_sibling(s, d, x, y, c, chips):
            return [(s.at[:, pl.ds((1 - c) * h, h), :], d, (x, y, 1 - c))]

        self.from_sibling = _sequencer_exchange(grad, jax.ShapeDtypeStruct((nq, h, cols), F32), first_id,
                                                f"rs_sibling_{nm}", to_sibling, 1)

    def sibling_sum(self, not_before=()):
        cols = self.grad.shape[2]
        c_arr = lax.axis_index("c").astype(jnp.int32).reshape(1)
        self.part, self.part_b = _add_sibling(self.grad, self.from_sibling, c_arr, f"add_sibling_{self.nm}", not_before)

        def to_chips(s, d, x, y, c, chips):
            return [(s.at[2 * chip[0] + chip[1]], d.at[j], (chip[0], chip[1], c)) for j, chip in enumerate(chips)]

        self.from_chips = _sequencer_exchange(self.part_b, jax.ShapeDtypeStruct((3, self.h, cols), BF16),
                                              self.first_id + 1, f"rs_quarters_{self.nm}", to_chips, 3)
        return self.part_b

    def chip_sum(self, not_before=()):
        cols = self.grad.shape[2]
        chip_arr = (2 * lax.axis_index("x") + lax.axis_index("y")).astype(jnp.int32).reshape(1)
        self.half = _add_chips(self.part, self.from_chips, chip_arr, f"add_chips_{self.nm}", not_before)

        def whole_to_sibling(s, d, x, y, c, chips):
            return [(s, d, (x, y, 1 - c))]

        self.other = _sequencer_exchange(self.half, jax.ShapeDtypeStruct((self.h, cols), F32), self.first_id + 2,
                                         f"rs_share_{self.nm}", whole_to_sibling, 1)
        return self.half

    def share(self):
        return self.half, self.other


def _after(x, deps, name):
    def body(x_ref, *rest):
        rest[-1][...] = x_ref[...]

    vm = pl.BlockSpec(memory_space=pltpu.VMEM)
    return pl.pallas_call(body, in_specs=[vm] + [ANY] * len(deps), out_specs=vm,
                          out_shape=jax.ShapeDtypeStruct(x.shape, x.dtype), name=name)(x, *deps)


def _adamw_halves(w, mine, other, m, v, name):
    rows, cols = w.shape
    tr = 128
    nb = rows // 2 // tr
    c_arr = lax.axis_index("c").astype(jnp.int32).reshape(1)

    def body(c_ref, w_ref, a_ref, b_ref, m_ref, v_ref, g_out, d_out, m_out, v_out):
        is_mine = (pl.program_id(0) // nb) == c_ref[0]
        g = jnp.where(is_mine, a_ref[...], b_ref[...])
        wb, mb, vb = w_ref[...], m_ref[...], v_ref[...]
        m2 = ADAM_B1 * mb + (1.0 - ADAM_B1) * g
        v2 = ADAM_B2 * vb + (1.0 - ADAM_B2) * (g * g)
        m_hat = m2 / (1.0 - ADAM_B1 ** ADAM_STEP)
        v_hat = v2 / (1.0 - ADAM_B2 ** ADAM_STEP)
        g_out[...] = g
        d_out[...] = -ADAM_LR * (m_hat / (jnp.sqrt(v_hat) + ADAM_EPS) + ADAM_WD * wb)
        m_out[...] = m2
        v_out[...] = v2

    full = pl.BlockSpec((tr, cols), lambda i, c: (i, 0))
    half = pl.BlockSpec((tr, cols), lambda i, c: (i % nb, 0))
    return pl.pallas_call(
        body,
        grid_spec=pltpu.PrefetchScalarGridSpec(
            num_scalar_prefetch=1, grid=(rows // tr,),
            in_specs=[full, half, half, full, full], out_specs=[full] * 4),
        out_shape=[jax.ShapeDtypeStruct((rows, cols), F32)] * 4,
        compiler_params=_cparams("parallel"),
        name=name,
    )(c_arr, w, mine, other, m, v)


def _all_sum_small(v):
    n_dev = 8

    def body(v_ref, o_ref, gath, send_sems, recv_sems):
        x, y, c, _ = _position()
        me = 4 * x + 2 * y + c
        gath[me] = v_ref[...]
        copies = []
        for k in range(1, n_dev):
            peer = tuple(1 - p if (k >> s) & 1 else p for p, s in ((x, 2), (y, 1), (c, 0)))
            cp = _remote(v_ref, gath.at[me], send_sems.at[k - 1], recv_sems.at[k - 1], peer)
            cp.start()
            copies.append(cp)
        for cp in copies:
            cp.wait()
        acc = gath[0]
        for i in range(1, n_dev):
            acc = acc + gath[i]
        o_ref[...] = acc

    vm = pl.BlockSpec(memory_space=pltpu.VMEM)
    return pl.pallas_call(
        body,
        in_specs=[vm],
        out_specs=vm,
        out_shape=jax.ShapeDtypeStruct(v.shape, F32),
        scratch_shapes=[pltpu.VMEM((n_dev,) + v.shape, F32), pltpu.SemaphoreType.DMA((n_dev - 1,)),
                        pltpu.SemaphoreType.DMA((n_dev - 1,))],
        name="all_sum_small",
    )(v)


def _pack_rows(vectors):
    rows = []
    for v in vectors:
        flat = v.reshape(-1).astype(F32)
        rows.append(jnp.pad(flat, (0, (-flat.shape[0]) % LANES)).reshape(-1, LANES))
    out = jnp.concatenate(rows, axis=0)
    return jnp.pad(out, ((0, (-out.shape[0]) % 8), (0, 0)))


def _unpack_rows(packed, shapes):
    outs, r = [], 0
    for shp in shapes:
        size = math.prod(shp)
        nr = -(-size // LANES)
        outs.append(packed[r:r + nr].reshape(-1)[:size].reshape(shp))
        r += nr
    return outs


def _relu_sq(acc):
    r = jnp.maximum(acc, 0.0)
    return r, r * r


def _relu_sq_bwd(acc, r):
    return (acc * (2.0 * r.astype(F32)),)


def kernel(x, norm_mix_pre, w_in, conv_w, conv_b, dt_bias, a_log, d_skip, ssm_norm_w, w_out, norm_mix_post, norm_mlp_pre, w_up, w_down, norm_mlp_post, loss_target, m_norm_mix_pre, m_w_in, m_conv_w, m_conv_b, m_dt_bias, m_a_log, m_d_skip, m_ssm_norm_w, m_w_out, m_norm_mix_post, m_norm_mlp_pre, m_w_up, m_w_down, m_norm_mlp_post, v_norm_mix_pre, v_w_in, v_conv_w, v_conv_b, v_dt_bias, v_a_log, v_d_skip, v_ssm_norm_w, v_w_out, v_norm_mix_post, v_norm_mlp_pre, v_w_up, v_w_down, v_norm_mlp_post):
    s_dim = x.shape[1]
    xs, target = x[0], loss_target[0]
    chip = 2 * lax.axis_index("x") + lax.axis_index("y")

    own = [w_in[0].astype(BF16), w_out[0].astype(BF16), w_up[0].astype(BF16), w_down[0].astype(BF16)]
    fetched = list(_gather_shards(own[:1])) + _gather_shards_async(own[1:], 1, "gather_rest")
    g_in, g_out, g_up, g_down = [_by_chip(o, f) for o, f in zip(own, fetched)]
    w_z = _cols_from_quarters(g_in, *IN_COLS["z"])
    w_xbc = _perm_cols(_cols_from_quarters(g_in, *IN_COLS["xbc"]))
    w_dt = jnp.pad(_cols_from_quarters(g_in, *IN_COLS["dt"]), ((0, 0), (0, LANES - SSM_HEADS)))
    w_qkv = _cols_from_quarters(g_in, *IN_COLS["qkv"])
    w_out_full = g_out.reshape(D_MIX, D_MODEL)
    w_down_full = g_down.reshape(D_FF, D_MODEL)

    conv_cols = D_XBC // N_CHIPS
    conv_placed = lax.dynamic_update_slice(jnp.zeros((8, D_XBC), F32), 0.5 * conv_w[0], (0, chip * conv_cols))
    conv_full = _all_sum_small(conv_placed.reshape(-1, LANES)).reshape(8, D_XBC)
    w8 = _perm_cols(conv_full.at[CONV_WIDTH].set(conv_b[0]))

    u = _pre_norm(xs, norm_mix_pre)
    z = _matmul([(u, w_z, TK)], "nn", [F32], name="proj_z")
    xbc = _matmul([(u, w_xbc, TK)], "nn", [F32], name="proj_xbc")
    dt_raw = _matmul([(u, w_dt, TK)], "nn", [F32], name="proj_dt")
    qkv = _matmul([(u, w_qkv, TK)], "nn", [BF16], name="proj_qkv")
    xc = _conv_fwd(xbc, w8)
    dtg = _dt_to_groups(dt_raw)
    par = _pack_ssd_params(dt_bias[0], a_log[0], d_skip[0])
    y, y_ssm, states = _ssd_fwd(xc, z, dtg, par, ssm_norm_w)
    y_att, y_att_f32, lse = _attention_fwd(qkv)
    y_mix = jnp.concatenate([y_ssm, y_att], axis=1)
    mix = _matmul([(y_mix, w_out_full, TK)], "nn", [F32], name="out_proj")
    h1, u2 = _post_pre_norm(xs, mix, norm_mix_post, norm_mlp_pre)
    hid, act = _matmul([(u2, g_up, TK)], "nn", [BF16, BF16], name="mlp_up", epilogue=_relu_sq, b_quarters=True)
    ff = _matmul([(act, w_down_full, TK)], "nn", [F32], name="mlp_down")
    dh2, dff, d_g4, loss_part = _tail(ff, h1, target, norm_mlp_post)

    dhid = _matmul([(dff, w_down_full, TK)], "nt", [BF16], name="mlp_down_dx", epilogue=_relu_sq_bwd, extras=[hid])
    weights = {"norm_mix_pre": (norm_mix_pre, m_norm_mix_pre, v_norm_mix_pre), "w_in": (w_in, m_w_in, v_w_in),
               "conv_w": (conv_w, m_conv_w, v_conv_w), "conv_b": (conv_b, m_conv_b, v_conv_b),
               "dt_bias": (dt_bias, m_dt_bias, v_dt_bias), "a_log": (a_log, m_a_log, v_a_log),
               "d_skip": (d_skip, m_d_skip, v_d_skip), "ssm_norm_w": (ssm_norm_w, m_ssm_norm_w, v_ssm_norm_w),
               "w_out": (w_out, m_w_out, v_w_out), "norm_mix_post": (norm_mix_post, m_norm_mix_post, v_norm_mix_post),
               "norm_mlp_pre": (norm_mlp_pre, m_norm_mlp_pre, v_norm_mlp_pre), "w_up": (w_up, m_w_up, v_w_up),
               "w_down": (w_down, m_w_down, v_w_down),
               "norm_mlp_post": (norm_mlp_post, m_norm_mlp_post, v_norm_mlp_post)}
    grads, delta, new_m, new_v = {}, {}, {}, {}

    def adamw_big(n, halves):
        w, m, v = weights[n]
        g_, d_, m_, v_ = _adamw_halves(w[0], halves[0], halves[1], m[0], v[0], f"adamw_{n}")
        grads[n], delta[n], new_m[n], new_v[n] = g_[None], d_[None], m_[None], v_[None]

    dw_down = _matmul([(act, dff, TK)], "tn", [F32], name="mlp_down_dw")
    rs_down = _AsyncReduceScatter(dw_down.reshape(N_CHIPS, D_FF // N_CHIPS, D_MODEL), "w_down", 11)
    dw_up = _matmul([(u2, dhid, TK)], "tn", [F32], name="mlp_up_dw", deps=[dw_down], out_quarters=True)
    rs_up = _AsyncReduceScatter(dw_up, "w_up", 8)
    du2 = _matmul([(dhid, g_up, TK)], "nt", [F32], name="mlp_up_dx", b_quarters=True,
                  deps=[rs_down.sibling_sum(not_before=[dw_up])])
    dh1, dmix, d_g3, d_g2 = _mid_bwd(du2, h1, dh2, mix, norm_mix_post, norm_mlp_pre,
                                     deps=[rs_up.sibling_sum(not_before=[du2])])
    dymix = _matmul([(dmix, w_out_full, TK)], "nt", [F32], name="out_proj_dx")
    dw_out = _matmul([(y_mix, dmix, TK)], "tn", [F32], name="out_proj_dw")
    rs_out = _AsyncReduceScatter(dw_out.reshape(N_CHIPS, D_MIX // N_CHIPS, D_MODEL), "w_out", 5)
    dqkv = _attention_bwd(qkv, dymix, y_att_f32, lse,
                          sum_deps=[rs_down.chip_sum(not_before=[dymix]), rs_out.sibling_sum(not_before=[dymix])])
    g_down = rs_down.share()
    par_late = _after(par, [*g_down, rs_up.chip_sum(not_before=[dqkv])], "after_w_down")
    dxc, dz, ddtg, dpar, d_nw = _ssd_bwd(xc, z, dtg, par_late, ssm_norm_w, y, states, dymix)
    g_up = rs_up.share()
    dxbc, dw8 = _conv_bwd(xbc, _after(w8, [*g_up, rs_out.chip_sum(not_before=[dxc])], "after_w_up"), dxc)
    ddt = jnp.pad(_dt_from_groups(ddtg), ((0, 0), (0, LANES - SSM_HEADS))).astype(BF16)
    g_out = rs_out.share()
    dw_z = _matmul([(u, dz, TK)], "tn", [F32], name="proj_z_dw")
    dw_xbc = _matmul([(u, dxbc, TK)], "tn", [F32], name="proj_xbc_dw")
    dw_dt = _matmul([(u, ddt, TK)], "tn", [F32], name="proj_dt_dw")
    dw_qkv = _matmul([(u, dqkv, TK)], "tn", [F32], name="proj_qkv_dw")
    dw_in = _quarters_from_cols({"z": dw_z, "xbc": _unperm_cols(dw_xbc), "dt": dw_dt[:, :SSM_HEADS], "qkv": dw_qkv})
    rs_in = _AsyncReduceScatter(dw_in, "w_in", 2)
    adamw_big("w_down", g_down)
    adamw_big("w_up", g_up)
    rs_in.sibling_sum(not_before=[delta["w_up"]])
    du = _matmul([(dz, w_z, TK_MULTI), (dxbc, w_xbc, TK_MULTI), (dqkv, w_qkv, TK_MULTI), (ddt, w_dt, LANES)], "nt",
                 [F32], name="proj_dx", deps=[*g_out, rs_in.part_b])
    grad_x, d_g1 = _first_bwd(du, xs, dh1, norm_mix_pre)
    adamw_big("w_out", g_out)
    rs_in.chip_sum(not_before=[grad_x, delta["w_out"]])

    dconv = _unperm_cols(dw8)
    d_bias, d_alog, d_dskip = _unpack_ssd_params(dpar)
    small_shapes = [(1, D_MODEL), (CONV_WIDTH, D_XBC), (1, D_XBC), (1, SSM_HEADS), (1, SSM_HEADS), (1, SSM_HEADS),
                    (1, D_SSM), (1, D_MODEL), (1, D_MODEL), (1, D_MODEL), (1, LANES)]
    summed = _unpack_rows(
        _all_sum_small(_pack_rows([d_g1, dconv[:CONV_WIDTH], dconv[CONV_WIDTH:CONV_WIDTH + 1], d_bias, d_alog,
                                   d_dskip, d_nw, d_g2, d_g3, d_g4, loss_part])), small_shapes)
    (g_g1, g_conv_full, g_conv_b, g_bias, g_alog, g_dskip, g_nw, g_g2, g_g3, g_g4, loss_row) = summed
    loss = loss_row[0, 0]
    g_conv_w = lax.dynamic_slice(g_conv_full, (0, chip * conv_cols), (CONV_WIDTH, conv_cols))[None]

    grads.update({"norm_mix_pre": g_g1, "conv_w": g_conv_w, "conv_b": g_conv_b, "dt_bias": g_bias,
                  "a_log": g_alog, "d_skip": g_dskip, "ssm_norm_w": g_nw, "norm_mix_post": g_g2,
                  "norm_mlp_pre": g_g3, "norm_mlp_post": g_g4})
    order = list(weights)
    small_names = [n for n in order if n not in ("w_in", "w_out", "w_up", "w_down")]
    small_w_shapes = [weights[n][0].shape for n in small_names]
    packed = [_pack_rows([weights[n][k] for n in small_names]) for k in range(3)]
    packed_g = _pack_rows([grads[n].reshape(weights[n][0].shape) for n in small_names])
    sd, sm, sv = _adamw(packed[0], packed_g, packed[1], packed[2], "adamw_small")
    for k, n in enumerate(small_names):
        grads[n] = grads[n].reshape(weights[n][0].shape)
    for res, pk in ((delta, sd), (new_m, sm), (new_v, sv)):
        for n, val in zip(small_names, _unpack_rows(pk, small_w_shapes)):
            res[n] = val
    adamw_big("w_in", rs_in.share())

    return (loss, grad_x[None], *[grads[n] for n in order], *[delta[n] for n in order],
            *[new_m[n] for n in order], *[new_v[n] for n in order])
```

```python
import functools
import math

import numpy as np
import jax
import jax.numpy as jnp
from jax import lax
from jax.experimental import pallas as pl
from jax.experimental.pallas import tpu as pltpu
from jax.experimental.pallas import tpu_sc as plsc

F32 = jnp.float32
BF16 = jnp.bfloat16

D_MODEL = 2048
SSM_HEAD_DIM = 64
SSM_GROUPS = 8
HEADS_PER_GROUP = 4
SSM_HEADS = SSM_GROUPS * HEADS_PER_GROUP
D_SSM = SSM_HEADS * SSM_HEAD_DIM
D_STATE = 128
CONV_WIDTH = 4
SSD_CHUNK = 128
D_XBC = D_SSM + 2 * SSM_GROUPS * D_STATE
GROUP_X = HEADS_PER_GROUP * SSM_HEAD_DIM
GROUP_COLS = GROUP_X + 2 * D_STATE
ATT_HEAD_DIM = 128
ATT_HEADS = 16
D_ATT = ATT_HEADS * ATT_HEAD_DIM
DILATIONS = (1, 4, 16)
ATT_BLOCK = 128
D_MIX = D_SSM + D_ATT
D_IN_PROJ = D_SSM + D_XBC + SSM_HEADS + 3 * D_ATT
D_FF = 4 * D_MODEL
EPS = 1e-6
N_CHIPS = 4
W_IN_SHARD = D_IN_PROJ // N_CHIPS

ADAM_LR = 0.001
ADAM_B1 = 0.9
ADAM_B2 = 0.999
ADAM_EPS = 1e-08
ADAM_WD = 0.01
ADAM_STEP = 10

LANES = 128
VMEM_LIMIT = 48 * 1024 * 1024
MESH = pl.DeviceIdType.MESH

_NN = (((1,), (0,)), ((), ()))
_NT = (((1,), (1,)), ((), ()))
_TN = (((0,), (0,)), ((), ()))


def _dot(a, b, dims=_NN):
    return lax.dot_general(a, b, dims, preferred_element_type=F32)


def _cparams(*sem):
    return pltpu.CompilerParams(dimension_semantics=sem, vmem_limit_bytes=VMEM_LIMIT)


TK = 2048
TK_MULTI = 1024


def _matmul(pairs, mode, out_dtypes, *, name, tm=1024, tn=1024, epilogue=None, extras=(), deps=(), out_quarters=False):
    a0, b0, _ = pairs[0]
    m_dim = a0.shape[-1] if mode == "tn" else a0.shape[-2]
    if b0.ndim == 3:
        n_dim = b0.shape[1] if mode == "nt" else b0.shape[0] * b0.shape[2]
    else:
        n_dim = b0.shape[0] if mode == "nt" else b0.shape[1]
    tm, tn = min(tm, m_dim), min(tn, n_dim)
    nks, offs = [], []
    for a, _, tk in pairs:
        k_part = a.shape[0] if mode == "tn" else a.shape[-1]
        k_dim = k_part * (a.shape[0] if a.ndim == 3 else 1)
        assert k_part % tk == 0, (name, k_part, tk)
        offs.append(sum(nks))
        nks.append(k_dim // tk)
    nk_total = sum(nks)
    assert m_dim % tm == 0 and n_dim % tn == 0, (name, m_dim, n_dim)
    dims = {"nn": _NN, "nt": _NT, "tn": _TN}[mode]
    n_pairs, n_extra, n_out = len(pairs), len(extras), len(out_dtypes)

    in_specs, operands = [], []
    for (a, b, tk), off, nk in zip(pairs, offs, nks):
        def kidx(k, off=off, nk=nk):
            return k if n_pairs == 1 else jnp.clip(k - off, 0, nk - 1)
        if mode == "tn":
            assert a.ndim == 2
            in_specs.append(pl.BlockSpec((tk, tm), lambda m, n, k, f=kidx: (f(k), m)))
        elif a.ndim == 3:
            per = a.shape[2] // tk
            in_specs.append(pl.BlockSpec((None, tm, tk), lambda m, n, k, f=kidx, per=per: (f(k) // per, m, f(k) % per)))
        else:
            in_specs.append(pl.BlockSpec((tm, tk), lambda m, n, k, f=kidx: (m, f(k))))
        if b.ndim == 3 and mode == "nt":
            per = b.shape[2] // tk
            in_specs.append(pl.BlockSpec((None, tn, tk), lambda m, n, k, f=kidx, per=per: (f(k) // per, n, f(k) % per)))
        elif b.ndim == 3:
            per = b.shape[2] // tn
            in_specs.append(pl.BlockSpec((None, tk, tn), lambda m, n, k, f=kidx, per=per: (n // per, f(k), n % per)))
        elif mode == "nt":
            in_specs.append(pl.BlockSpec((tn, tk), lambda m, n, k, f=kidx: (n, f(k))))
        else:
            in_specs.append(pl.BlockSpec((tk, tn), lambda m, n, k, f=kidx: (f(k), n)))
        operands += [a, b]
    for e in extras:
        in_specs.append(pl.BlockSpec((tm, tn), lambda m, n, k: (m, n)))
        operands.append(e)
    in_specs += [pl.BlockSpec(memory_space=pl.ANY)] * len(deps)
    operands += list(deps)
    first_out = 2 * n_pairs + n_extra + len(deps)
    if out_quarters:
        out_per_q = n_dim // N_CHIPS // tn
        out_dims = (N_CHIPS, m_dim, n_dim // N_CHIPS)
        out_spec = pl.BlockSpec((None, tm, tn), lambda m, n, k: (n // out_per_q, m, n % out_per_q))
    else:
        out_dims = (m_dim, n_dim)
        out_spec = pl.BlockSpec((tm, tn), lambda m, n, k: (m, n))

    def body(*refs):
        ab = refs[:2 * n_pairs]
        e_refs = refs[2 * n_pairs:2 * n_pairs + n_extra]
        o_refs = refs[first_out:first_out + n_out]

        def finish(total):
            vals = (total,) if epilogue is None else epilogue(total, *[e[...] for e in e_refs])
            for o_ref, v in zip(o_refs, vals):
                o_ref[...] = v.astype(o_ref.dtype)

        if nk_total == 1:
            finish(_dot(ab[0][...], ab[1][...], dims))
            return
        acc = refs[-1]
        k = pl.program_id(2)

        @pl.when(k == 0)
        def _():
            acc[...] = jnp.zeros_like(acc)

        for i in range(n_pairs):
            def accumulate(i=i):
                acc[...] += _dot(ab[2 * i][...], ab[2 * i + 1][...], dims)
            if n_pairs == 1:
                accumulate()
            else:
                pl.when((k >= offs[i]) & (k < offs[i] + nks[i]))(accumulate)

        @pl.when(k == nk_total - 1)
        def _():
            finish(acc[...])

    outs = pl.pallas_call(
        body,
        grid=(m_dim // tm, n_dim // tn, nk_total),
        in_specs=in_specs,
        out_specs=[out_spec for _ in out_dtypes],
        out_shape=[jax.ShapeDtypeStruct(out_dims, dt) for dt in out_dtypes],
        scratch_shapes=[pltpu.VMEM((tm, tn), F32)] if nk_total > 1 else [],
        compiler_params=_cparams("parallel", "parallel", "arbitrary"),
        name=name,
    )(*operands)
    return outs[0] if n_out == 1 else outs


def _rowcall(fn, rows, vecs, row_outs, acc_widths, *, name, tr=256, row_cols=None, deps=()):
    s_dim = rows[0].shape[0]
    assert s_dim % tr == 0
    row_cols = row_cols or [None] * len(rows)
    n_r, n_v, n_ro, n_acc = len(rows), len(vecs), len(row_outs), len(acc_widths)
    in_specs = []
    for r, rc in zip(rows, row_cols):
        if rc is None:
            in_specs.append(pl.BlockSpec((tr, r.shape[1]), lambda i: (i, 0)))
        else:
            in_specs.append(pl.BlockSpec((tr, rc[0]), lambda i, c=rc[1]: (i, c)))
    for v in vecs:
        in_specs.append(pl.BlockSpec(v.shape, lambda i, nd=v.ndim: (0,) * nd))
    in_specs += [pl.BlockSpec(memory_space=pl.ANY)] * len(deps)
    n_d = len(deps)

    def body(*refs):
        ins = [r[...] for r in refs[:n_r + n_v]]
        ro = refs[n_r + n_v + n_d:n_r + n_v + n_d + n_ro]
        ao = refs[n_r + n_v + n_d + n_ro:]
        outs = fn(*ins)
        for ref, v in zip(ro, outs[:n_ro]):
            ref[...] = v.astype(ref.dtype)
        if n_acc:
            @pl.when(pl.program_id(0) == 0)
            def _():
                for ref in ao:
                    ref[...] = jnp.zeros_like(ref)
            for ref, v in zip(ao, outs[n_ro:]):
                ref[...] += v

    outs = pl.pallas_call(
        body,
        grid=(s_dim // tr,),
        in_specs=in_specs,
        out_specs=[pl.BlockSpec((tr, w), lambda i: (i, 0)) for w, _ in row_outs]
        + [pl.BlockSpec((1, w), lambda i: (0, 0)) for w in acc_widths],
        out_shape=[jax.ShapeDtypeStruct((s_dim, w), dt) for w, dt in row_outs]
        + [jax.ShapeDtypeStruct((1, w), F32) for w in acc_widths],
        compiler_params=_cparams("arbitrary"),
        name=name,
    )(*rows, *vecs, *deps)
    return outs


def _nrm(x, g):
    r = lax.rsqrt(jnp.mean(x * x, axis=-1, keepdims=True) + EPS)
    n = x * r
    return n * g, n, r


def _nrm_bwd(dy, n, r, g):
    dn = dy * g
    dx = r * (dn - n * jnp.mean(dn * n, axis=-1, keepdims=True))
    return dx, jnp.sum(dy * n, axis=0, keepdims=True)


def _sigmoid(x):
    return 1.0 / (1.0 + jnp.exp(-x))


def _softplus(x):
    return jnp.maximum(x, 0.0) + jnp.log(1.0 + jnp.exp(-jnp.abs(x)))


def _pre_norm(x, g1):
    def fn(xb, g):
        return (_nrm(xb, g)[0],)
    return _rowcall(fn, [x], [g1], [(D_MODEL, BF16)], [], name="pre_norm")[0]


def _post_pre_norm(x, mix, g2, g3):
    def fn(xb, mb, g2b, g3b):
        h1 = xb + _nrm(mb, g2b)[0]
        return h1, _nrm(h1, g3b)[0]
    return _rowcall(fn, [x, mix], [g2, g3], [(D_MODEL, F32), (D_MODEL, BF16)], [], name="post_pre_norm")


def _tail(ff, h1, target, g4):
    def fn(ffb, h1b, tb, g):
        y, n, r = _nrm(ffb, g)
        e = h1b + y - tb
        loss = 0.5 * jnp.sum(jnp.sum(e * e, axis=-1, keepdims=True) * (1.0 / D_MODEL), axis=0, keepdims=True)
        dh2 = e * (1.0 / D_MODEL)
        dff, dg = _nrm_bwd(dh2, n, r, g)
        return dh2, dff, dg, jnp.broadcast_to(loss, (1, LANES))
    return _rowcall(fn, [ff, h1, target], [g4], [(D_MODEL, F32), (D_MODEL, BF16)], [D_MODEL, LANES], name="tail")


def _mid_bwd(du2, h1, dh2, mix, g2, g3, deps=()):
    def fn(du2b, h1b, dh2b, mb, g2b, g3b):
        _, n3, r3 = _nrm(h1b, g3b)
        d3, dg3 = _nrm_bwd(du2b, n3, r3, g3b)
        dh1 = dh2b + d3
        _, n2, r2 = _nrm(mb, g2b)
        dmix, dg2 = _nrm_bwd(dh1, n2, r2, g2b)
        return dh1, dmix, dg3, dg2
    return _rowcall(fn, [du2, h1, dh2, mix], [g2, g3], [(D_MODEL, F32), (D_MODEL, BF16)], [D_MODEL, D_MODEL],
                    name="mid_bwd", deps=deps)


def _first_bwd(du, x, dh1, g1):
    def fn(dub, xb, dh1b, g):
        _, n, r = _nrm(xb, g)
        dx, dg = _nrm_bwd(dub, n, r, g)
        return dh1b + dx, dg
    return _rowcall(fn, [du, x, dh1], [g1], [(D_MODEL, F32)], [D_MODEL], name="first_bwd")


CONV_TILE = 256
CONV_ROWS = 256
PAD = 8


def _conv_taps(w):
    return [w[k:k + 1, :] for k in range(CONV_WIDTH)], w[CONV_WIDTH:CONV_WIDTH + 1, :]


def _conv_fwd(xbc, w8):
    s_dim, c_dim = xbc.shape
    n_steps = s_dim // CONV_ROWS

    def body(x_ref, w_ref, o_ref, xp):
        xp[0:PAD, :] = jnp.zeros((PAD, CONV_TILE), F32)
        xp[PAD:PAD + s_dim, :] = x_ref[...]
        taps, bias = _conv_taps(w_ref[...])

        def step(c, carry):
            base = pl.multiple_of(c * CONV_ROWS, CONV_ROWS)
            win = xp[pl.ds(base, CONV_ROWS + PAD), :]
            pre = bias + taps[3] * win[PAD:, :]
            for j in range(1, CONV_WIDTH):
                pre = pre + taps[3 - j] * pltpu.roll(win, j, axis=0)[PAD:, :]
            o_ref[pl.ds(base, CONV_ROWS), :] = pre * _sigmoid(pre)
            return carry

        lax.fori_loop(0, n_steps, step, 0)

    return pl.pallas_call(
        body,
        grid=(c_dim // CONV_TILE,),
        in_specs=[pl.BlockSpec((s_dim, CONV_TILE), lambda j: (0, j)), pl.BlockSpec((8, CONV_TILE), lambda j: (0, j))],
        out_specs=pl.BlockSpec((s_dim, CONV_TILE), lambda j: (0, j)),
        out_shape=jax.ShapeDtypeStruct((s_dim, c_dim), F32),
        scratch_shapes=[pltpu.VMEM((s_dim + 2 * PAD, CONV_TILE), F32)],
        compiler_params=_cparams("parallel"),
        name="conv_fwd",
    )(xbc, w8)


def _conv_bwd(xbc, w8, dxc):
    s_dim, c_dim = xbc.shape
    n_steps = s_dim // CONV_ROWS

    def body(x_ref, w_ref, d_ref, dx_ref, dw_ref, xp, dp):
        xp[0:PAD, :] = jnp.zeros((PAD, CONV_TILE), F32)
        xp[PAD:PAD + s_dim, :] = x_ref[...]
        dp[PAD + s_dim:, :] = jnp.zeros((PAD, CONV_TILE), F32)
        taps, bias = _conv_taps(w_ref[...])

        def step1(c, sums):
            base = pl.multiple_of(c * CONV_ROWS, CONV_ROWS)
            win = xp[pl.ds(base, CONV_ROWS + PAD), :]
            shifted = [win[PAD:, :]] + [pltpu.roll(win, j, axis=0)[PAD:, :] for j in range(1, CONV_WIDTH)]
            pre = bias
            for j in range(CONV_WIDTH):
                pre = pre + taps[3 - j] * shifted[j]
            sg = _sigmoid(pre)
            dpre = d_ref[pl.ds(base, CONV_ROWS), :] * (sg * (1.0 + pre * (1.0 - sg)))
            dp[pl.ds(base + PAD, CONV_ROWS), :] = dpre
            new = [sums[k] + jnp.sum(dpre * shifted[3 - k], axis=0, keepdims=True) for k in range(CONV_WIDTH)]
            new.append(sums[CONV_WIDTH] + jnp.sum(dpre, axis=0, keepdims=True))
            return tuple(new)

        zero = jnp.zeros((1, CONV_TILE), F32)
        sums = lax.fori_loop(0, n_steps, step1, (zero,) * (CONV_WIDTH + 1))
        dw_ref[...] = jnp.zeros((8, CONV_TILE), F32)
        for k in range(CONV_WIDTH + 1):
            dw_ref[k:k + 1, :] = sums[k]

        def step2(c, carry):
            base = pl.multiple_of(c * CONV_ROWS, CONV_ROWS)
            win = dp[pl.ds(base + PAD, CONV_ROWS + PAD), :]
            dx = taps[3] * win[:CONV_ROWS, :]
            for j in range(1, CONV_WIDTH):
                dx = dx + taps[3 - j] * pltpu.roll(win, CONV_ROWS + PAD - j, axis=0)[:CONV_ROWS, :]
            dx_ref[pl.ds(base, CONV_ROWS), :] = dx.astype(BF16)
            return carry

        lax.fori_loop(0, n_steps, step2, 0)

    col = lambda j: (0, j)
    return pl.pallas_call(
        body,
        grid=(c_dim // CONV_TILE,),
        in_specs=[pl.BlockSpec((s_dim, CONV_TILE), col), pl.BlockSpec((8, CONV_TILE), col),
                  pl.BlockSpec((s_dim, CONV_TILE), col)],
        out_specs=[pl.BlockSpec((s_dim, CONV_TILE), col), pl.BlockSpec((8, CONV_TILE), col)],
        out_shape=[jax.ShapeDtypeStruct((s_dim, c_dim), BF16), jax.ShapeDtypeStruct((8, c_dim), F32)],
        scratch_shapes=[pltpu.VMEM((s_dim + 2 * PAD, CONV_TILE), F32), pltpu.VMEM((s_dim + 2 * PAD, CONV_TILE), F32)],
        compiler_params=_cparams("parallel"),
        name="conv_bwd",
    )(xbc, w8, dxc)


def _perm_cols(a):
    parts = []
    for g in range(SSM_GROUPS):
        parts += [a[..., g * GROUP_X:(g + 1) * GROUP_X],
                  a[..., D_SSM + g * D_STATE:D_SSM + (g + 1) * D_STATE],
                  a[..., D_SSM + SSM_GROUPS * D_STATE + g * D_STATE:D_SSM + SSM_GROUPS * D_STATE + (g + 1) * D_STATE]]
    return jnp.concatenate(parts, axis=-1)


def _unperm_cols(a):
    xs = [a[..., g * GROUP_COLS:g * GROUP_COLS + GROUP_X] for g in range(SSM_GROUPS)]
    bs = [a[..., g * GROUP_COLS + GROUP_X:g * GROUP_COLS + GROUP_X + D_STATE] for g in range(SSM_GROUPS)]
    cs = [a[..., g * GROUP_COLS + GROUP_X + D_STATE:(g + 1) * GROUP_COLS] for g in range(SSM_GROUPS)]
    return jnp.concatenate(xs + bs + cs, axis=-1)


def _dt_to_groups(dt):
    s_dim = dt.shape[0]
    t = dt[:, :SSM_HEADS].reshape(s_dim, SSM_GROUPS, HEADS_PER_GROUP).transpose(1, 0, 2)
    return jnp.pad(t, ((0, 0), (0, 0), (0, LANES - HEADS_PER_GROUP)))


def _dt_from_groups(dtg):
    s_dim = dtg.shape[1]
    return dtg[:, :, :HEADS_PER_GROUP].transpose(1, 0, 2).reshape(s_dim, SSM_HEADS)


def _pack_ssd_params(dt_bias, a_log, d_skip):
    rows = jnp.stack([p.reshape(SSM_GROUPS, HEADS_PER_GROUP) for p in (dt_bias, a_log, d_skip)], axis=1)
    return jnp.pad(rows, ((0, 0), (0, 8 - 3), (0, LANES - HEADS_PER_GROUP)))


def _unpack_ssd_params(par):
    return tuple(par[:, k, :HEADS_PER_GROUP].reshape(SSM_HEADS) for k in range(3))


Q = SSD_CHUNK


def _split3(v):
    hi = v.astype(BF16)
    r1 = v - hi.astype(F32)
    mid = r1.astype(BF16)
    lo = (r1 - mid.astype(F32)).astype(BF16)
    return hi, mid, lo


def _dot_l01(t01, v):
    return sum(_dot(t01, p) for p in _split3(v))


def _dot_r01(v, e01):
    return sum(_dot(p, e01) for p in _split3(v))


def _ssd_consts():
    row = lax.broadcasted_iota(jnp.int32, (Q, Q), 0)
    col = lax.broadcasted_iota(jnp.int32, (Q, Q), 1)
    causal = row >= col
    tril = causal.astype(BF16)
    triu = (col >= row).astype(BF16)
    er = lax.broadcasted_iota(jnp.int32, (LANES, GROUP_X), 0)
    ec = lax.broadcasted_iota(jnp.int32, (LANES, GROUP_X), 1) // SSM_HEAD_DIM
    expand = (er == ec).astype(BF16)
    rr = lax.broadcasted_iota(jnp.int32, (GROUP_X, LANES), 0) // SSM_HEAD_DIM
    rc = lax.broadcasted_iota(jnp.int32, (GROUP_X, LANES), 1)
    reduce = (rr == rc).astype(BF16)
    lane_head = lax.broadcasted_iota(jnp.int32, (Q, GROUP_X), 1) // SSM_HEAD_DIM
    return causal, tril, triu, expand, reduce, lane_head


def _ssd_common(xc_ref, dt_ref, par_ref, consts):
    causal, tril, _, expand, _, _ = consts
    par = par_ref[...]
    bias, alog, dsk = par[0:1, :], par[1:2, :], par[2:3, :]
    a_neg = -jnp.exp(alog)
    dtr = dt_ref[...] + bias
    dt = _softplus(dtr)
    s = _dot_l01(tril, dt * a_neg)
    dt_x = _dot_r01(dt, expand)
    s_x = _dot_r01(s, expand)
    dsk_x = _dot_r01(jnp.broadcast_to(dsk, (8, LANES)), expand)[0:1, :]
    blk = xc_ref[...]
    x = blk[:, :GROUP_X]
    bm = blk[:, GROUP_X:GROUP_X + D_STATE].astype(BF16)
    cm = blk[:, GROUP_X + D_STATE:].astype(BF16)
    xdt = x * dt_x
    g = _dot(cm, bm, _NT)
    return dict(a_neg=a_neg, dtr=dtr, dt=dt, s=s, s_t=s.T, dt_x=dt_x, s_x=s_x, dsk_x=dsk_x, x=x, bm=bm, cm=cm,
                xdt=xdt, g=g)


def _decay(v, r, causal):
    diff = v["s"][:, r:r + 1] - v["s_t"][r:r + 1, :]
    return jnp.exp(jnp.where(causal, diff, -jnp.inf))


def _ssd_specs(n_chunks, rev):
    cidx = (lambda c: n_chunks - 1 - c) if rev else (lambda c: c)
    xc = pl.BlockSpec((Q, GROUP_COLS), lambda g, c: (cidx(c), g))
    gx = pl.BlockSpec((Q, GROUP_X), lambda g, c: (cidx(c), g))
    dt = pl.BlockSpec((None, Q, LANES), lambda g, c: (g, cidx(c), 0))
    par = pl.BlockSpec((None, 8, LANES), lambda g, c: (g, 0, 0))
    nw = pl.BlockSpec((1, GROUP_X), lambda g, c: (0, g))
    hs = pl.BlockSpec((None, None, D_STATE, GROUP_X), lambda g, c: (cidx(c), g, 0, 0))
    return xc, gx, dt, par, nw, hs


def _ssd_fwd(xc, z, dtg, par, nw):
    s_dim = xc.shape[0]
    n_chunks = s_dim // Q
    xc_s, gx_s, dt_s, par_s, nw_s, hs_s = _ssd_specs(n_chunks, False)

    def body(xc_ref, z_ref, dt_ref, par_ref, nw_ref, y_ref, ys_ref, hs_ref, ht):
        @pl.when(pl.program_id(1) == 0)
        def _():
            ht[...] = jnp.zeros_like(ht)

        consts = _ssd_consts()
        causal, lane_head = consts[0], consts[5]
        v = _ssd_common(xc_ref, dt_ref, par_ref, consts)
        xdt_b = v["xdt"].astype(BF16)
        yd = jnp.zeros((Q, GROUP_X), F32)
        for r in range(HEADS_PER_GROUP):
            m = (v["g"] * _decay(v, r, causal)).astype(BF16)
            yd = yd + _dot(m, jnp.where(lane_head == r, xdt_b, jnp.zeros_like(xdt_b)))
        h = ht[...]
        hs_ref[...] = h
        yo = jnp.exp(v["s_x"]) * _dot(v["cm"], h.astype(BF16))
        y = yd + yo + v["dsk_x"] * v["x"]
        s_last = v["s_x"][Q - 1:Q, :]
        snew = _dot(v["bm"], (v["xdt"] * jnp.exp(s_last - v["s_x"])).astype(BF16), _TN)
        ht[...] = jnp.exp(s_last) * h + snew
        zz = z_ref[...]
        yg = y * (zz * _sigmoid(zz))
        y_ref[...] = y
        ys_ref[...] = _nrm(yg, nw_ref[...])[0].astype(BF16)

    return pl.pallas_call(
        body,
        grid=(SSM_GROUPS, n_chunks),
        in_specs=[xc_s, gx_s, dt_s, par_s, nw_s],
        out_specs=[gx_s, gx_s, hs_s],
        out_shape=[jax.ShapeDtypeStruct((s_dim, D_SSM), F32), jax.ShapeDtypeStruct((s_dim, D_SSM), BF16),
                   jax.ShapeDtypeStruct((n_chunks, SSM_GROUPS, D_STATE, GROUP_X), F32)],
        scratch_shapes=[pltpu.VMEM((D_STATE, GROUP_X), F32)],
        compiler_params=_cparams("parallel", "arbitrary"),
        name="ssd_fwd",
    )(xc, z, dtg, par, nw)


def _ssd_bwd(xc, z, dtg, par, nw, y, hs, dymix):
    s_dim = xc.shape[0]
    n_chunks = s_dim // Q
    xc_s, gx_s, dt_s, par_s, nw_s, hs_s = _ssd_specs(n_chunks, True)

    def body(xc_ref, z_ref, dt_ref, par_ref, nw_ref, y_ref, hs_ref, dys_ref,
             dxc_ref, dz_ref, ddt_ref, dpar_ref, dnw_ref, dht):
        @pl.when(pl.program_id(1) == 0)
        def _():
            dht[...] = jnp.zeros_like(dht)
            dpar_ref[...] = jnp.zeros_like(dpar_ref)
            dnw_ref[...] = jnp.zeros_like(dnw_ref)

        consts = _ssd_consts()
        causal, _, triu, _, reduce, lane_head = consts
        v = _ssd_common(xc_ref, dt_ref, par_ref, consts)
        x, bm, cm, xdt, s_x = v["x"], v["bm"], v["cm"], v["xdt"], v["s_x"]
        h = hs_ref[...]
        hb = h.astype(BF16)
        es_x = jnp.exp(s_x)
        yo = es_x * _dot(cm, hb)
        s_last = s_x[Q - 1:Q, :]
        e_x = jnp.exp(s_last - s_x)
        es_last = jnp.exp(s_last)

        yv, zz, nw_v = y_ref[...], z_ref[...], nw_ref[...]
        sg = _sigmoid(zz)
        gz = zz * sg
        _, n, rstd = _nrm(yv * gz, nw_v)
        dout = dys_ref[...]
        dyg, dnw = _nrm_bwd(dout, n, rstd, nw_v)
        dnw_ref[...] += dnw
        dy = dyg * gz
        dz_ref[...] = (dyg * yv * (sg * (1.0 + zz * (1.0 - sg)))).astype(BF16)

        dyb = dy.astype(BF16)
        xdt_b = xdt.astype(BF16)
        dhp = dht[...]
        dhpb = dhp.astype(BF16)
        lane = lax.broadcasted_iota(jnp.int32, (Q, LANES), 1)
        sub = lax.broadcasted_iota(jnp.int32, (LANES, Q), 0)
        dxdt = jnp.zeros((Q, GROUP_X), F32)
        dg = jnp.zeros((Q, Q), F32)
        ds = jnp.zeros((Q, LANES), F32)
        ds_t = jnp.zeros((LANES, Q), F32)
        for r in range(HEADS_PER_GROUP):
            dec = _decay(v, r, causal)
            mf = v["g"] * dec
            dyr = jnp.where(lane_head == r, dyb, jnp.zeros_like(dyb))
            dm = _dot(dyr, xdt_b, _NT)
            dxdt = dxdt + _dot(mf.astype(BF16), dyr, _TN)
            dg = dg + dm * dec
            dd = dm * mf
            ds = ds + jnp.where(lane == r, jnp.sum(dd, axis=1, keepdims=True), 0.0)
            ds_t = ds_t + jnp.where(sub == r, jnp.sum(dd, axis=0, keepdims=True), 0.0)
        ds = ds - ds_t.T
        dgb = dg.astype(BF16)
        dwb = (es_x * dy).astype(BF16)
        dcm = _dot(dgb, bm) + _dot(dwb, hb, _NT)
        dh_prev = _dot(cm, dwb, _TN)
        zst = _dot(bm, dhpb)
        xe = xdt * e_x
        dxdt = dxdt + e_x * zst
        dee = xe * zst
        dbm = _dot(dgb, cm, _TN) + _dot(xe.astype(BF16), dhpb, _NT)
        v_last = jnp.sum(dee, axis=0, keepdims=True) + es_last * jnp.sum(dhp * h, axis=0, keepdims=True)
        row_x = lax.broadcasted_iota(jnp.int32, (Q, GROUP_X), 0)
        tx = dy * yo - dee + jnp.where(row_x == Q - 1, v_last, 0.0)
        ds = ds + _dot_r01(tx, reduce)
        ddta = _dot_l01(triu, ds)
        ddt = ddta * v["a_neg"] + _dot_r01(dxdt * x, reduce)
        dalog = jnp.sum(ddta * v["dt"], axis=0, keepdims=True) * v["a_neg"]
        draw = jnp.where(lane < HEADS_PER_GROUP, ddt * _sigmoid(v["dtr"]), 0.0)
        dbias = jnp.sum(draw, axis=0, keepdims=True)
        ddsk = _dot_r01(jnp.broadcast_to(jnp.sum(dy * x, axis=0, keepdims=True), (8, GROUP_X)), reduce)[0:1, :]
        dht[...] = es_last * dhp + dh_prev
        dxc_ref[:, :GROUP_X] = dxdt * v["dt_x"] + v["dsk_x"] * dy
        dxc_ref[:, GROUP_X:GROUP_X + D_STATE] = dbm
        dxc_ref[:, GROUP_X + D_STATE:] = dcm
        ddt_ref[...] = draw
        dpar_ref[0:1, :] += dbias
        dpar_ref[1:2, :] += dalog
        dpar_ref[2:3, :] += ddsk

    return pl.pallas_call(
        body,
        grid=(SSM_GROUPS, n_chunks),
        in_specs=[xc_s, gx_s, dt_s, par_s, nw_s, gx_s, hs_s, gx_s],
        out_specs=[xc_s, gx_s, dt_s, par_s, nw_s],
        out_shape=[jax.ShapeDtypeStruct((s_dim, SSM_GROUPS * GROUP_COLS), F32),
                   jax.ShapeDtypeStruct((s_dim, D_SSM), BF16),
                   jax.ShapeDtypeStruct((SSM_GROUPS, s_dim, LANES), F32),
                   jax.ShapeDtypeStruct((SSM_GROUPS, 8, LANES), F32),
                   jax.ShapeDtypeStruct((1, D_SSM), F32)],
        scratch_shapes=[pltpu.VMEM((D_STATE, GROUP_X), F32)],
        compiler_params=_cparams("parallel", "arbitrary"),
        name="ssd_bwd",
    )(xc, z, dtg, par, nw, y, hs, dymix)


ATT_SCALE = ATT_HEAD_DIM ** -0.5
NEG_INF = -jnp.inf


def _head(h):
    return slice(h * ATT_HEAD_DIM, (h + 1) * ATT_HEAD_DIM)


def _band_masks():
    qi = lax.broadcasted_iota(jnp.int32, (ATT_BLOCK, ATT_BLOCK), 0)
    kj = lax.broadcasted_iota(jnp.int32, (ATT_BLOCK, ATT_BLOCK), 1)
    return kj <= qi, kj >= qi


def _attn_fwd(qkv_v, d):
    rows = qkv_v.shape[0]
    nb = rows // ATT_BLOCK
    blk = (ATT_BLOCK, D_ATT)
    prev = lambda i: jnp.maximum(i - 1, 0)

    def body(q_ref, kc_ref, kp_ref, vc_ref, vp_ref, o_ref, lse_ref):
        own, before = _band_masks()
        before = before & (pl.program_id(1) > 0)
        lane = lax.broadcasted_iota(jnp.int32, (ATT_BLOCK, LANES), 1)
        lse_all = jnp.zeros((ATT_BLOCK, LANES), F32)
        for h in range(ATT_HEADS):
            q = q_ref[:, _head(h)]
            sc = jnp.where(own, _dot(q, kc_ref[:, _head(h)], _NT) * ATT_SCALE, NEG_INF)
            sp = jnp.where(before, _dot(q, kp_ref[:, _head(h)], _NT) * ATT_SCALE, NEG_INF)
            m = jnp.maximum(jnp.max(sc, axis=1, keepdims=True), jnp.max(sp, axis=1, keepdims=True))
            pc, pp = jnp.exp(sc - m), jnp.exp(sp - m)
            den = jnp.sum(pc, axis=1, keepdims=True) + jnp.sum(pp, axis=1, keepdims=True)
            o = _dot(pc.astype(BF16), vc_ref[:, _head(h)]) + _dot(pp.astype(BF16), vp_ref[:, _head(h)])
            o_ref[:, _head(h)] = o / den
            lse_all = jnp.where(lane == h, m + jnp.log(den), lse_all)
        lse_ref[...] = lse_all

    return pl.pallas_call(
        body,
        grid=(d, nb),
        in_specs=[pl.BlockSpec(blk, lambda r, i: (i, 3 * r)),
                  pl.BlockSpec(blk, lambda r, i: (i, 3 * r + 1)),
                  pl.BlockSpec(blk, lambda r, i: (prev(i), 3 * r + 1)),
                  pl.BlockSpec(blk, lambda r, i: (i, 3 * r + 2)),
                  pl.BlockSpec(blk, lambda r, i: (prev(i), 3 * r + 2))],
        out_specs=[pl.BlockSpec(blk, lambda r, i: (i, r)), pl.BlockSpec((ATT_BLOCK, LANES), lambda r, i: (i, r))],
        out_shape=[jax.ShapeDtypeStruct((rows, d * D_ATT), F32), jax.ShapeDtypeStruct((rows, d * LANES), F32)],
        compiler_params=_cparams("parallel", "arbitrary"),
        name=f"attn_fwd_d{d}",
    )(qkv_v, qkv_v, qkv_v, qkv_v, qkv_v)


def _attn_combine(os_, lses):
    def fn(o1, o2, o3, l1, l2, l3):
        m = jnp.maximum(jnp.maximum(l1, l2), l3)
        tot = m + jnp.log(jnp.exp(l1 - m) + jnp.exp(l2 - m) + jnp.exp(l3 - m))
        w1, w2, w3 = jnp.exp(l1 - tot), jnp.exp(l2 - tot), jnp.exp(l3 - tot)
        cols = []
        for h in range(ATT_HEADS):
            cols.append(w1[:, h:h + 1] * o1[:, _head(h)] + w2[:, h:h + 1] * o2[:, _head(h)]
                        + w3[:, h:h + 1] * o3[:, _head(h)])
        y = jnp.concatenate(cols, axis=1)
        return y, y, tot
    return _rowcall(fn, list(os_) + list(lses), [], [(D_ATT, BF16), (D_ATT, F32), (LANES, F32)], [],
                    name="attn_combine", tr=128)


def _attn_delta(dymix, y_att):
    def fn(dy, y):
        lane = lax.broadcasted_iota(jnp.int32, (dy.shape[0], LANES), 1)
        delta = jnp.zeros((dy.shape[0], LANES), F32)
        for h in range(ATT_HEADS):
            delta = jnp.where(lane == h, jnp.sum(dy[:, _head(h)] * y[:, _head(h)], axis=1, keepdims=True), delta)
        return dy, delta
    return _rowcall(fn, [dymix, y_att], [], [(D_ATT, BF16), (LANES, F32)], [], name="attn_delta",
                    row_cols=[(D_ATT, 1), None])


def _attn_bwd(qkv_v, dy_v, lse_v, delta_v, d):
    rows = qkv_v.shape[0]
    nb = rows // ATT_BLOCK
    blk = (ATT_BLOCK, D_ATT)
    sblk = (ATT_BLOCK, LANES)
    prev = lambda i: jnp.maximum(i - 1, 0)
    nxt = lambda i: jnp.minimum(i + 1, nb - 1)

    def body(qc_ref, qn_ref, kc_ref, kp_ref, vc_ref, vp_ref, dyc_ref, dyn_ref, lc_ref, ln_ref, dc_ref, dn_ref,
             dq_ref, dk_ref, dv_ref):
        i = pl.program_id(1)
        own, before = _band_masks()
        before_c = before & (i > 0)
        before_n = before & (i < nb - 1)
        lc, ln, dc, dn = lc_ref[...], ln_ref[...], dc_ref[...], dn_ref[...]
        for h in range(ATT_HEADS):
            hs = _head(h)
            q, qn, kc, kp, vc, vp = qc_ref[:, hs], qn_ref[:, hs], kc_ref[:, hs], kp_ref[:, hs], vc_ref[:, hs], vp_ref[:, hs]
            dy, dyn = dyc_ref[:, hs], dyn_ref[:, hs]
            lse, lse_n, dl, dl_n = lc[:, h:h + 1], ln[:, h:h + 1], dc[:, h:h + 1], dn[:, h:h + 1]
            pc = jnp.exp(jnp.where(own, _dot(q, kc, _NT) * ATT_SCALE - lse, NEG_INF))
            pp = jnp.exp(jnp.where(before_c, _dot(q, kp, _NT) * ATT_SCALE - lse, NEG_INF))
            pn = jnp.exp(jnp.where(before_n, _dot(qn, kc, _NT) * ATT_SCALE - lse_n, NEG_INF))
            dsc = (pc * (_dot(dy, vc, _NT) - dl)).astype(BF16)
            dsp = (pp * (_dot(dy, vp, _NT) - dl)).astype(BF16)
            dsn = (pn * (_dot(dyn, vc, _NT) - dl_n)).astype(BF16)
            dq_ref[:, hs] = (_dot(dsc, kc) + _dot(dsp, kp)) * ATT_SCALE
            dk_ref[:, hs] = (_dot(dsc, q, _TN) + _dot(dsn, qn, _TN)) * ATT_SCALE
            dv_ref[:, hs] = _dot(pc.astype(BF16), dy, _TN) + _dot(pn.astype(BF16), dyn, _TN)

    return pl.pallas_call(
        body,
        grid=(d, nb),
        in_specs=[pl.BlockSpec(blk, lambda r, i: (i, 3 * r)), pl.BlockSpec(blk, lambda r, i: (nxt(i), 3 * r)),
                  pl.BlockSpec(blk, lambda r, i: (i, 3 * r + 1)), pl.BlockSpec(blk, lambda r, i: (prev(i), 3 * r + 1)),
                  pl.BlockSpec(blk, lambda r, i: (i, 3 * r + 2)), pl.BlockSpec(blk, lambda r, i: (prev(i), 3 * r + 2)),
                  pl.BlockSpec(blk, lambda r, i: (i, r)), pl.BlockSpec(blk, lambda r, i: (nxt(i), r)),
                  pl.BlockSpec(sblk, lambda r, i: (i, r)), pl.BlockSpec(sblk, lambda r, i: (nxt(i), r)),
                  pl.BlockSpec(sblk, lambda r, i: (i, r)), pl.BlockSpec(sblk, lambda r, i: (nxt(i), r))],
        out_specs=[pl.BlockSpec(blk, lambda r, i: (i, r))] * 3,
        out_shape=[jax.ShapeDtypeStruct((rows, d * D_ATT), F32)] * 3,
        compiler_params=_cparams("parallel", "arbitrary"),
        name=f"attn_bwd_d{d}",
    )(qkv_v, qkv_v, qkv_v, qkv_v, qkv_v, qkv_v, dy_v, dy_v, lse_v, lse_v, delta_v, delta_v)


def _attn_sum(dqs, dks, dvs, deps=()):
    def fn(*parts):
        return (jnp.concatenate([parts[0] + parts[1] + parts[2], parts[3] + parts[4] + parts[5],
                                 parts[6] + parts[7] + parts[8]], axis=1),)
    return _rowcall(fn, list(dqs) + list(dks) + list(dvs), [], [(3 * D_ATT, BF16)], [], name="attn_sum", tr=128,
                    deps=deps)[0]


def _attention_fwd(qkv):
    s_dim = qkv.shape[0]
    os_, lses = [], []
    for d in DILATIONS:
        o, lse = _attn_fwd(qkv.reshape(s_dim // d, d * 3 * D_ATT), d)
        os_.append(o.reshape(s_dim, D_ATT))
        lses.append(lse.reshape(s_dim, LANES))
    return _attn_combine(os_, lses)


def _attention_bwd(qkv, dymix, y_att, lse, sum_deps=()):
    s_dim = qkv.shape[0]
    dy, delta = _attn_delta(dymix, y_att)
    dqs, dks, dvs = [], [], []
    for d in DILATIONS:
        dq, dk, dv = _attn_bwd(qkv.reshape(s_dim // d, d * 3 * D_ATT), dy.reshape(s_dim // d, d * D_ATT),
                               lse.reshape(s_dim // d, d * LANES), delta.reshape(s_dim // d, d * LANES), d)
        dqs.append(dq.reshape(s_dim, D_ATT))
        dks.append(dk.reshape(s_dim, D_ATT))
        dvs.append(dv.reshape(s_dim, D_ATT))
    return _attn_sum(dqs, dks, dvs, sum_deps)


WIN = ATT_BLOCK * DILATIONS[-1]
N_BLOCKS = WIN // ATT_BLOCK


def _rows(start, d):
    return pl.ds(start, ATT_BLOCK) if d == 1 else pl.ds(start, ATT_BLOCK, stride=d)


def _block_start(idx, d):
    return (idx // d) * (ATT_BLOCK * d) + idx % d


def _lane_bcast(col):
    return jnp.broadcast_to(col, (col.shape[0], LANES))


def _attn_fused_fwd(qkv):
    s_dim = qkv.shape[0]
    n_win = s_dim // WIN
    blk = (WIN, ATT_HEAD_DIM)
    prev = lambda w: jnp.maximum(w - 1, 0)

    def body(q_ref, kc_ref, kp_ref, vc_ref, vp_ref, y_ref, yf_ref, lse_ref, qf, kf, vf, acc, m_run, l_run):
        w, h = pl.program_id(0), pl.program_id(1)
        qf[...] = q_ref[...].astype(F32)
        kf[0:WIN, :] = kp_ref[...].astype(F32)
        kf[WIN:, :] = kc_ref[...].astype(F32)
        vf[0:WIN, :] = vp_ref[...].astype(F32)
        vf[WIN:, :] = vc_ref[...].astype(F32)
        own, before = _band_masks()

        for d in DILATIONS:
            def block(idx, carry, d=d):
                start = _block_start(idx, d)
                rows = _rows(start, d)
                q = qf[rows, :].astype(BF16)
                kc, vc = kf[_rows(WIN + start, d), :].astype(BF16), vf[_rows(WIN + start, d), :].astype(BF16)
                kp = kf[_rows(WIN + start - ATT_BLOCK * d, d), :].astype(BF16)
                vp = vf[_rows(WIN + start - ATT_BLOCK * d, d), :].astype(BF16)
                has_prev = (idx >= d) | (w > 0)
                sc = jnp.where(own, _dot(q, kc, _NT) * ATT_SCALE, NEG_INF)
                sp = jnp.where(before & has_prev, _dot(q, kp, _NT) * ATT_SCALE, NEG_INF)
                m_blk = jnp.maximum(jnp.max(sc, axis=1, keepdims=True), jnp.max(sp, axis=1, keepdims=True))
                if d == DILATIONS[0]:
                    m_new = m_blk
                else:
                    m_old = m_run[rows, :][:, 0:1]
                    m_new = jnp.maximum(m_old, m_blk)
                pc, pp = jnp.exp(sc - m_new), jnp.exp(sp - m_new)
                l_new = jnp.sum(pc, axis=1, keepdims=True) + jnp.sum(pp, axis=1, keepdims=True)
                o_new = _dot(pc.astype(BF16), vc) + _dot(pp.astype(BF16), vp)
                if d != DILATIONS[0]:
                    alpha = jnp.exp(m_old - m_new)
                    l_new = alpha * l_run[rows, :][:, 0:1] + l_new
                    o_new = alpha * acc[rows, :] + o_new
                m_run[rows, :] = _lane_bcast(m_new)
                l_run[rows, :] = _lane_bcast(l_new)
                acc[rows, :] = o_new
                return carry

            lax.fori_loop(0, N_BLOCKS, block, 0)

        l_all = l_run[...]
        y = acc[...] / l_all
        y_ref[...] = y.astype(BF16)
        yf_ref[...] = y
        @pl.when(h == 0)
        def _():
            lse_ref[...] = jnp.zeros_like(lse_ref)

        lane = lax.broadcasted_iota(jnp.int32, (WIN, LANES), 1)
        lse_ref[...] = jnp.where(lane == h, m_run[...] + jnp.log(l_all), lse_ref[...])

    win_scratch = lambda rows: pltpu.VMEM((rows, ATT_HEAD_DIM), F32)
    return pl.pallas_call(
        body,
        grid=(n_win, ATT_HEADS),
        in_specs=[pl.BlockSpec(blk, lambda w, h: (w, h)),
                  pl.BlockSpec(blk, lambda w, h: (w, ATT_HEADS + h)),
                  pl.BlockSpec(blk, lambda w, h: (prev(w), ATT_HEADS + h)),
                  pl.BlockSpec(blk, lambda w, h: (w, 2 * ATT_HEADS + h)),
                  pl.BlockSpec(blk, lambda w, h: (prev(w), 2 * ATT_HEADS + h))],
        out_specs=[pl.BlockSpec(blk, lambda w, h: (w, h)), pl.BlockSpec(blk, lambda w, h: (w, h)),
                   pl.BlockSpec((WIN, LANES), lambda w, h: (w, 0))],
        out_shape=[jax.ShapeDtypeStruct((s_dim, D_ATT), BF16), jax.ShapeDtypeStruct((s_dim, D_ATT), F32),
                   jax.ShapeDtypeStruct((s_dim, LANES), F32)],
        scratch_shapes=[win_scratch(WIN), win_scratch(2 * WIN), win_scratch(2 * WIN), win_scratch(WIN),
                        win_scratch(WIN), win_scratch(WIN)],
        compiler_params=_cparams("parallel", "arbitrary"),
        name="attn_fused_fwd",
    )(qkv, qkv, qkv, qkv, qkv)


def _attn_fused_bwd(qkv, dymix, y_att, lse, deps=()):
    s_dim = qkv.shape[0]
    n_win = s_dim // WIN
    blk = (WIN, ATT_HEAD_DIM)
    prev = lambda w: jnp.maximum(w - 1, 0)
    nxt = lambda w: jnp.minimum(w + 1, n_win - 1)
    n_dep = len(deps)

    def body(qc_ref, qn_ref, kc_ref, kp_ref, vc_ref, vp_ref, dyc_ref, dyn_ref, yc_ref, yn_ref, lc_ref, ln_ref, *rest):
        out_ref = rest[n_dep]
        qf, qnf, kf, vf, dq_acc, dk_acc, dv_acc, ls_c, dl_c, ls_n, dl_n = rest[n_dep + 1:]
        w, h = pl.program_id(0), pl.program_id(1)
        qf[...] = qc_ref[...].astype(F32)
        qnf[...] = qn_ref[...].astype(F32)
        kf[0:WIN, :] = kp_ref[...].astype(F32)
        kf[WIN:, :] = kc_ref[...].astype(F32)
        vf[0:WIN, :] = vp_ref[...].astype(F32)
        vf[WIN:, :] = vc_ref[...].astype(F32)
        lane = lax.broadcasted_iota(jnp.int32, (WIN, LANES), 1)
        pick = lambda ref: _lane_bcast(jnp.sum(jnp.where(lane == h, ref[...], 0.0), axis=1, keepdims=True))
        ls_c[...] = pick(lc_ref)
        ls_n[...] = pick(ln_ref)
        dl_c[...] = _lane_bcast(jnp.sum(dyc_ref[...] * yc_ref[...], axis=1, keepdims=True))
        dl_n[...] = _lane_bcast(jnp.sum(dyn_ref[...] * yn_ref[...], axis=1, keepdims=True))
        for ref in (dq_acc, dk_acc, dv_acc):
            ref[...] = jnp.zeros_like(ref)
        own, before = _band_masks()

        def probs(q, k, v, dy, lse_col, dl_col, mask):
            p = jnp.exp(jnp.where(mask, _dot(q, k, _NT) * ATT_SCALE - lse_col, NEG_INF))
            ds = p * (_dot(dy, v, _NT) - dl_col)
            return p.astype(BF16), ds.astype(BF16)

        for d in DILATIONS:
            def block(idx, carry, d=d):
                start = _block_start(idx, d)
                rows = _rows(start, d)
                prows = _rows(start - ATT_BLOCK * d, d)
                q, dy = qf[rows, :].astype(BF16), dyc_ref[rows, :].astype(BF16)
                lse_col, dl_col = ls_c[rows, :][:, 0:1], dl_c[rows, :][:, 0:1]
                kc, vc = kf[_rows(WIN + start, d), :].astype(BF16), vf[_rows(WIN + start, d), :].astype(BF16)
                kp = kf[_rows(WIN + start - ATT_BLOCK * d, d), :].astype(BF16)
                vp = vf[_rows(WIN + start - ATT_BLOCK * d, d), :].astype(BF16)
                pc, dsc = probs(q, kc, vc, dy, lse_col, dl_col, own)
                pp, dsp = probs(q, kp, vp, dy, lse_col, dl_col, before & ((idx >= d) | (w > 0)))
                dq_acc[rows, :] += (_dot(dsc, kc) + _dot(dsp, kp)) * ATT_SCALE
                dk_acc[rows, :] += _dot(dsc, q, _TN) * ATT_SCALE
                dv_acc[rows, :] += _dot(pc, dy, _TN)

                @pl.when(idx >= d)
                def _():
                    dk_acc[prows, :] += _dot(dsp, q, _TN) * ATT_SCALE
                    dv_acc[prows, :] += _dot(pp, dy, _TN)
                return carry

            lax.fori_loop(0, N_BLOCKS, block, 0)

            def next_window(r, carry, d=d):
                krows = _rows(WIN - ATT_BLOCK * d + r, d)
                rows = _rows(r, d)
                q, dy = qnf[rows, :].astype(BF16), dyn_ref[rows, :].astype(BF16)
                k, v = kf[_rows(2 * WIN - ATT_BLOCK * d + r, d), :].astype(BF16), vf[_rows(2 * WIN - ATT_BLOCK * d + r, d), :].astype(BF16)
                pn, dsn = probs(q, k, v, dy, ls_n[rows, :][:, 0:1], dl_n[rows, :][:, 0:1], before & (w < n_win - 1))
                dk_acc[krows, :] += _dot(dsn, q, _TN) * ATT_SCALE
                dv_acc[krows, :] += _dot(pn, dy, _TN)
                return carry

            lax.fori_loop(0, d, next_window, 0)

        for part, acc_ref in enumerate((dq_acc, dk_acc, dv_acc)):
            out_ref[part] = acc_ref[...].astype(BF16)

    win_scratch = lambda rows: pltpu.VMEM((rows, ATT_HEAD_DIM), F32)
    cur = lambda c: pl.BlockSpec(blk, lambda w, h: (w, c + h))
    return pl.pallas_call(
        body,
        grid=(n_win, ATT_HEADS),
        in_specs=[cur(0), pl.BlockSpec(blk, lambda w, h: (nxt(w), h)),
                  cur(ATT_HEADS), pl.BlockSpec(blk, lambda w, h: (prev(w), ATT_HEADS + h)),
                  cur(2 * ATT_HEADS), pl.BlockSpec(blk, lambda w, h: (prev(w), 2 * ATT_HEADS + h)),
                  cur(ATT_HEADS), pl.BlockSpec(blk, lambda w, h: (nxt(w), ATT_HEADS + h)),
                  cur(0), pl.BlockSpec(blk, lambda w, h: (nxt(w), h)),
                  pl.BlockSpec((WIN, LANES), lambda w, h: (w, 0)), pl.BlockSpec((WIN, LANES), lambda w, h: (nxt(w), 0))]
        + [ANY] * n_dep,
        out_specs=pl.BlockSpec((3, WIN, ATT_HEAD_DIM), lambda w, h: (0, w, h)),
        out_shape=jax.ShapeDtypeStruct((3, s_dim, D_ATT), BF16),
        scratch_shapes=[win_scratch(WIN), win_scratch(WIN), win_scratch(2 * WIN), win_scratch(2 * WIN)]
        + [win_scratch(WIN)] * 7,
        compiler_params=_cparams("parallel", "arbitrary"),
        name="attn_fused_bwd",
    )(qkv, qkv, qkv, qkv, qkv, qkv, dymix, dymix, y_att, y_att, lse, lse, *deps)


def _adamw(w, g, m, v, name):
    def fn(wb, gb, mb, vb):
        m2 = ADAM_B1 * mb + (1.0 - ADAM_B1) * gb
        v2 = ADAM_B2 * vb + (1.0 - ADAM_B2) * (gb * gb)
        m_hat = m2 / (1.0 - ADAM_B1 ** ADAM_STEP)
        v_hat = v2 / (1.0 - ADAM_B2 ** ADAM_STEP)
        delta = -ADAM_LR * (m_hat / (jnp.sqrt(v_hat) + ADAM_EPS) + ADAM_WD * wb)
        return delta, m2, v2
    cols = w.shape[1]
    tr = 128 if w.shape[0] % 128 == 0 else w.shape[0]
    return _rowcall(fn, [w, g, m, v], [], [(cols, F32)] * 3, [], name=name, tr=tr)


ANY = pl.BlockSpec(memory_space=pl.ANY)


def _position():
    x, y, c = lax.axis_index("x"), lax.axis_index("y"), lax.axis_index("c")
    chips = [(1 - x, y), (x, 1 - y), (1 - x, 1 - y)]
    return x, y, c, chips


def _remote(src, dst, send_sem, recv_sem, device):
    return pltpu.make_async_remote_copy(src_ref=src, dst_ref=dst, send_sem=send_sem, recv_sem=recv_sem,
                                        device_id=device, device_id_type=MESH)


def _gather_shards(shards):
    n = len(shards)

    def body(*refs):
        ins, outs = refs[:n], refs[n:2 * n]
        send_sems, recv_sems = refs[2 * n:]
        x, y, c, chips = _position()
        sibling = (x, y, 1 - c)

        def half(a, j, cc):
            h = ins[a].shape[0] // 2
            return outs[a].at[j, pl.ds(cc * h, h), :]

        sent = []
        for a in range(n):
            h = ins[a].shape[0] // 2
            for j, chip in enumerate(chips):
                cp = _remote(ins[a].at[pl.ds(c * h, h), :], half(a, j, c), send_sems.at[6 * a + j],
                             recv_sems.at[6 * a + j], (chip[0], chip[1], c))
                cp.start()
                sent.append(cp)
        for a in range(n):
            for j in range(3):
                landed = half(a, j, c)
                _remote(landed, landed, send_sems.at[6 * a + j], recv_sems.at[6 * a + j], (x, y, c)).wait_recv()
                cp = _remote(landed, landed, send_sems.at[6 * a + 3 + j], recv_sems.at[6 * a + 3 + j], sibling)
                cp.start()
                sent.append(cp)
        for a in range(n):
            for j in range(3):
                handed = half(a, j, 1 - c)
                _remote(handed, handed, send_sems.at[6 * a + 3 + j], recv_sems.at[6 * a + 3 + j], (x, y, c)).wait_recv()
        for cp in sent:
            cp.wait_send()

    return pl.pallas_call(
        body,
        in_specs=[ANY] * n,
        out_specs=[ANY] * n,
        out_shape=[jax.ShapeDtypeStruct((3,) + s.shape, s.dtype) for s in shards],
        scratch_shapes=[pltpu.SemaphoreType.DMA((6 * n,)), pltpu.SemaphoreType.DMA((6 * n,))],
        name="gather_shards",
    )(*shards)


def _handshake(peers):
    barrier = pltpu.get_barrier_semaphore()
    for p in peers:
        pl.semaphore_signal(barrier, inc=1, device_id=p, device_id_type=MESH)
    pl.semaphore_wait(barrier, len(peers))


def _gather_shards_async(shards, collective_id, name):
    n = len(shards)
    srcs = [jax.new_ref(s, memory_space=pltpu.MemorySpace.HBM) for s in shards]
    dsts = [jax.empty_ref(jax.ShapeDtypeStruct((3,) + s.shape, s.dtype), memory_space=pltpu.MemorySpace.HBM)
            for s in shards]

    @pl.kernel(mesh=plsc.ScalarSubcoreMesh(axis_name="seq", num_cores=1), name=name,
               scratch_types=(pltpu.SemaphoreType.DMA((6 * n,)), pltpu.SemaphoreType.DMA((6 * n,))),
               compiler_params=pltpu.CompilerParams(collective_id=collective_id))
    def launch(send_sems, recv_sems):
        x, y, c, chips = _position()
        sibling = (x, y, 1 - c)
        _handshake([(chip[0], chip[1], c) for chip in chips] + [sibling])

        def half(a, j, cc):
            h = shards[a].shape[0] // 2
            return dsts[a].at[j, pl.ds(cc * h, h), :]

        sent = []
        for a in range(n):
            h = shards[a].shape[0] // 2
            for j, chip in enumerate(chips):
                cp = _remote(srcs[a].at[pl.ds(c * h, h), :], half(a, j, c), send_sems.at[6 * a + j],
                             recv_sems.at[6 * a + j], (chip[0], chip[1], c))
                cp.start()
                sent.append(cp)
        for a in range(n):
            for j in range(3):
                landed = half(a, j, c)
                _remote(landed, landed, send_sems.at[6 * a + j], recv_sems.at[6 * a + j], (x, y, c)).wait_recv()
                cp = _remote(landed, landed, send_sems.at[6 * a + 3 + j], recv_sems.at[6 * a + 3 + j], sibling)
                cp.start()
                sent.append(cp)
        for a in range(n):
            for j in range(3):
                handed = half(a, j, 1 - c)
                _remote(handed, handed, send_sems.at[6 * a + 3 + j], recv_sems.at[6 * a + 3 + j], (x, y, c)).wait_recv()
        for cp in sent:
            cp.wait_send()

    launch()
    return [d[...] for d in dsts]


IN_COLS = {"z": (0, D_SSM), "xbc": (D_SSM, D_SSM + D_XBC), "dt": (D_SSM + D_XBC, D_SSM + D_XBC + SSM_HEADS),
           "qkv": (D_SSM + D_XBC + SSM_HEADS, D_IN_PROJ)}


def _cols_from_quarters(quarters, lo, hi):
    parts = []
    for q in range(N_CHIPS):
        a, b = max(lo, q * W_IN_SHARD), min(hi, (q + 1) * W_IN_SHARD)
        if a < b:
            parts.append(quarters[q][:, a - q * W_IN_SHARD:b - q * W_IN_SHARD])
    return parts[0] if len(parts) == 1 else jnp.concatenate(parts, axis=1)


def _quarters_from_cols(pieces):
    quarters = []
    for q in range(N_CHIPS):
        parts = []
        for name, (lo, hi) in IN_COLS.items():
            a, b = max(lo, q * W_IN_SHARD), min(hi, (q + 1) * W_IN_SHARD)
            if a < b:
                parts.append(pieces[name][:, a - lo:b - lo])
        quarters.append(jnp.concatenate(parts, axis=1))
    return jnp.stack(quarters)


def _by_chip(own, others):
    me = 2 * lax.axis_index("x") + lax.axis_index("y")
    rel = jnp.stack([own, others[1], others[0], others[2]])
    return jnp.stack([lax.dynamic_index_in_dim(rel, q ^ me, 0, keepdims=False) for q in range(N_CHIPS)])


def _add_sibling(grad, got, c_arr, name, deps=()):
    nq, rows, cols = grad.shape
    h = rows // 2
    tr = 128
    nb = h // tr

    def body(c_ref, a_ref, b_ref, *rest):
        o_ref, ob_ref = rest[len(deps):]
        total = a_ref[...] + b_ref[...]
        o_ref[...] = total
        ob_ref[...] = total.astype(BF16)

    out_spec = pl.BlockSpec((None, tr, cols), lambda q, i, c: (q, i, 0))
    return pl.pallas_call(
        body,
        grid_spec=pltpu.PrefetchScalarGridSpec(
            num_scalar_prefetch=1, grid=(nq, nb),
            in_specs=[pl.BlockSpec((None, tr, cols), lambda q, i, c: (q, c[0] * nb + i, 0)),
                      pl.BlockSpec((None, tr, cols), lambda q, i, c: (q, i, 0))] + [ANY] * len(deps),
            out_specs=[out_spec, out_spec]),
        out_shape=[jax.ShapeDtypeStruct((nq, h, cols), F32), jax.ShapeDtypeStruct((nq, h, cols), BF16)],
        compiler_params=_cparams("parallel", "parallel"),
        name=name,
    )(c_arr, grad, got, *deps)


def _add_chips(part, got, chip_arr, name, deps=()):
    _, h, cols = part.shape
    tr = 128

    def body(q_ref, p_ref, g0_ref, g1_ref, g2_ref, *rest):
        o_ref = rest[len(deps)]
        o_ref[...] = ((p_ref[...] + g0_ref[...].astype(F32)) + g1_ref[...].astype(F32)) + g2_ref[...].astype(F32)

    got_spec = lambda j: pl.BlockSpec((None, tr, cols), lambda i, q: (j, i, 0))
    return pl.pallas_call(
        body,
        grid_spec=pltpu.PrefetchScalarGridSpec(
            num_scalar_prefetch=1, grid=(h // tr,),
            in_specs=[pl.BlockSpec((None, tr, cols), lambda i, q: (q[0], i, 0)), got_spec(0), got_spec(1), got_spec(2)]
            + [ANY] * len(deps),
            out_specs=pl.BlockSpec((tr, cols), lambda i, q: (i, 0))),
        out_shape=jax.ShapeDtypeStruct((h, cols), F32),
        compiler_params=_cparams("parallel"),
        name=name,
    )(chip_arr, part, got, got, got, *deps)


def _sequencer_exchange(src, out_shape, collective_id, name, plan, n_copies):
    src_ref = jax.new_ref(src, memory_space=pltpu.MemorySpace.HBM)
    dst_ref = jax.empty_ref(out_shape, memory_space=pltpu.MemorySpace.HBM)

    @pl.kernel(mesh=plsc.ScalarSubcoreMesh(axis_name="seq", num_cores=1), name=name,
               scratch_types=(pltpu.SemaphoreType.DMA((n_copies,)), pltpu.SemaphoreType.DMA((n_copies,))),
               compiler_params=pltpu.CompilerParams(collective_id=collective_id))
    def launch(send_sems, recv_sems):
        x, y, c, chips = _position()
        copies = plan(src_ref, dst_ref, x, y, c, chips)
        _handshake([peer for _, _, peer in copies])
        started = []
        for k, (s, d, peer) in enumerate(copies):
            cp = _remote(s, d, send_sems.at[k], recv_sems.at[k], peer)
            cp.start()
            started.append(cp)
        for cp in started:
            cp.wait()

    launch()
    return dst_ref[...]


class _AsyncReduceScatter:
    def __init__(self, grad, nm, first_id):
        self.grad, self.nm, self.first_id = grad, nm, first_id
        nq, rows, cols = grad.shape
        h = self.h = rows // 2

        def to_sibling(s, d, x, y, c, chips):
            return [(s.at[:, pl.ds((1 - c) * h, h), :], d, (x, y, 1 - c))]

        self.from_sibling = _sequencer_exchange(grad, jax.ShapeDtypeStruct((nq, h, cols), F32), first_id,
                                                f"rs_sibling_{nm}", to_sibling, 1)

    def sibling_sum(self, not_before=()):
        cols = self.grad.shape[2]
        c_arr = lax.axis_index("c").astype(jnp.int32).reshape(1)
        self.part, self.part_b = _add_sibling(self.grad, self.from_sibling, c_arr, f"add_sibling_{self.nm}", not_before)

        def to_chips(s, d, x, y, c, chips):
            return [(s.at[2 * chip[0] + chip[1]], d.at[j], (chip[0], chip[1], c)) for j, chip in enumerate(chips)]

        self.from_chips = _sequencer_exchange(self.part_b, jax.ShapeDtypeStruct((3, self.h, cols), BF16),
                                              self.first_id + 1, f"rs_quarters_{self.nm}", to_chips, 3)
        return self.part_b

    def chip_sum(self, not_before=()):
        cols = self.grad.shape[2]
        chip_arr = (2 * lax.axis_index("x") + lax.axis_index("y")).astype(jnp.int32).reshape(1)
        self.half = _add_chips(self.part, self.from_chips, chip_arr, f"add_chips_{self.nm}", not_before)

        def whole_to_sibling(s, d, x, y, c, chips):
            return [(s, d, (x, y, 1 - c))]

        self.other = _sequencer_exchange(self.half, jax.ShapeDtypeStruct((self.h, cols), F32), self.first_id + 2,
                                         f"rs_share_{self.nm}", whole_to_sibling, 1)
        return self.half

    def share(self):
        return self.half, self.other


def _after(x, deps, name):
    def body(x_ref, *rest):
        rest[-1][...] = x_ref[...]

    vm = pl.BlockSpec(memory_space=pltpu.VMEM)
    return pl.pallas_call(body, in_specs=[vm] + [ANY] * len(deps), out_specs=vm,
                          out_shape=jax.ShapeDtypeStruct(x.shape, x.dtype), name=name)(x, *deps)


def _adamw_halves(w, mine, other, m, v, name):
    rows, cols = w.shape
    tr = 128
    nb = rows // 2 // tr
    c_arr = lax.axis_index("c").astype(jnp.int32).reshape(1)

    def body(c_ref, w_ref, a_ref, b_ref, m_ref, v_ref, g_out, d_out, m_out, v_out):
        is_mine = (pl.program_id(0) // nb) == c_ref[0]
        g = jnp.where(is_mine, a_ref[...], b_ref[...])
        wb, mb, vb = w_ref[...], m_ref[...], v_ref[...]
        m2 = ADAM_B1 * mb + (1.0 - ADAM_B1) * g
        v2 = ADAM_B2 * vb + (1.0 - ADAM_B2) * (g * g)
        m_hat = m2 / (1.0 - ADAM_B1 ** ADAM_STEP)
        v_hat = v2 / (1.0 - ADAM_B2 ** ADAM_STEP)
        g_out[...] = g
        d_out[...] = -ADAM_LR * (m_hat / (jnp.sqrt(v_hat) + ADAM_EPS) + ADAM_WD * wb)
        m_out[...] = m2
        v_out[...] = v2

    full = pl.BlockSpec((tr, cols), lambda i, c: (i, 0))
    half = pl.BlockSpec((tr, cols), lambda i, c: (i % nb, 0))
    return pl.pallas_call(
        body,
        grid_spec=pltpu.PrefetchScalarGridSpec(
            num_scalar_prefetch=1, grid=(rows // tr,),
            in_specs=[full, half, half, full, full], out_specs=[full] * 4),
        out_shape=[jax.ShapeDtypeStruct((rows, cols), F32)] * 4,
        compiler_params=_cparams("parallel"),
        name=name,
    )(c_arr, w, mine, other, m, v)


def _all_sum_small(v):
    n_dev = 8

    def body(v_ref, o_ref, gath, send_sems, recv_sems):
        x, y, c, _ = _position()
        me = 4 * x + 2 * y + c
        gath[me] = v_ref[...]
        copies = []
        for k in range(1, n_dev):
            peer = tuple(1 - p if (k >> s) & 1 else p for p, s in ((x, 2), (y, 1), (c, 0)))
            cp = _remote(v_ref, gath.at[me], send_sems.at[k - 1], recv_sems.at[k - 1], peer)
            cp.start()
            copies.append(cp)
        for cp in copies:
            cp.wait()
        acc = gath[0]
        for i in range(1, n_dev):
            acc = acc + gath[i]
        o_ref[...] = acc

    vm = pl.BlockSpec(memory_space=pltpu.VMEM)
    return pl.pallas_call(
        body,
        in_specs=[vm],
        out_specs=vm,
        out_shape=jax.ShapeDtypeStruct(v.shape, F32),
        scratch_shapes=[pltpu.VMEM((n_dev,) + v.shape, F32), pltpu.SemaphoreType.DMA((n_dev - 1,)),
                        pltpu.SemaphoreType.DMA((n_dev - 1,))],
        name="all_sum_small",
    )(v)


def _pack_rows(vectors):
    rows = []
    for v in vectors:
        flat = v.reshape(-1).astype(F32)
        rows.append(jnp.pad(flat, (0, (-flat.shape[0]) % LANES)).reshape(-1, LANES))
    out = jnp.concatenate(rows, axis=0)
    return jnp.pad(out, ((0, (-out.shape[0]) % 8), (0, 0)))


def _unpack_rows(packed, shapes):
    outs, r = [], 0
    for shp in shapes:
        size = math.prod(shp)
        nr = -(-size // LANES)
        outs.append(packed[r:r + nr].reshape(-1)[:size].reshape(shp))
        r += nr
    return outs


def _relu_sq(acc):
    r = jnp.maximum(acc, 0.0)
    return r, r * r


def _relu_sq_bwd(acc, r):
    return (acc * (2.0 * r.astype(F32)),)


def kernel(x, norm_mix_pre, w_in, conv_w, conv_b, dt_bias, a_log, d_skip, ssm_norm_w, w_out, norm_mix_post, norm_mlp_pre, w_up, w_down, norm_mlp_post, loss_target, m_norm_mix_pre, m_w_in, m_conv_w, m_conv_b, m_dt_bias, m_a_log, m_d_skip, m_ssm_norm_w, m_w_out, m_norm_mix_post, m_norm_mlp_pre, m_w_up, m_w_down, m_norm_mlp_post, v_norm_mix_pre, v_w_in, v_conv_w, v_conv_b, v_dt_bias, v_a_log, v_d_skip, v_ssm_norm_w, v_w_out, v_norm_mix_post, v_norm_mlp_pre, v_w_up, v_w_down, v_norm_mlp_post):
    s_dim = x.shape[1]
    xs, target = x[0], loss_target[0]
    chip = 2 * lax.axis_index("x") + lax.axis_index("y")

    own = [w_in[0].astype(BF16), w_out[0].astype(BF16), w_up[0].astype(BF16), w_down[0].astype(BF16)]
    fetched = list(_gather_shards(own[:1])) + _gather_shards_async(own[1:], 1, "gather_rest")
    g_in, g_out, g_up, g_down = [_by_chip(o, f) for o, f in zip(own, fetched)]
    w_z = _cols_from_quarters(g_in, *IN_COLS["z"])
    w_xbc = _perm_cols(_cols_from_quarters(g_in, *IN_COLS["xbc"]))
    w_dt = jnp.pad(_cols_from_quarters(g_in, *IN_COLS["dt"]), ((0, 0), (0, LANES - SSM_HEADS)))
    w_qkv = _cols_from_quarters(g_in, *IN_COLS["qkv"])
    w_out_full = g_out.reshape(D_MIX, D_MODEL)
    w_down_full = g_down.reshape(D_FF, D_MODEL)

    conv_cols = D_XBC // N_CHIPS
    conv_placed = lax.dynamic_update_slice(jnp.zeros((8, D_XBC), F32), 0.5 * conv_w[0], (0, chip * conv_cols))
    conv_full = _all_sum_small(conv_placed.reshape(-1, LANES)).reshape(8, D_XBC)
    w8 = _perm_cols(conv_full.at[CONV_WIDTH].set(conv_b[0]))

    u = _pre_norm(xs, norm_mix_pre)
    z = _matmul([(u, w_z, TK)], "nn", [F32], name="proj_z")
    xbc = _matmul([(u, w_xbc, TK)], "nn", [F32], name="proj_xbc")
    dt_raw = _matmul([(u, w_dt, TK)], "nn", [F32], name="proj_dt")
    qkv = _matmul([(u, w_qkv, TK)], "nn", [BF16], name="proj_qkv")
    xc = _conv_fwd(xbc, w8)
    dtg = _dt_to_groups(dt_raw)
    par = _pack_ssd_params(dt_bias[0], a_log[0], d_skip[0])
    y, y_ssm, states = _ssd_fwd(xc, z, dtg, par, ssm_norm_w)
    y_att, y_att_f32, lse = _attn_fused_fwd(qkv)
    y_mix = jnp.concatenate([y_ssm, y_att], axis=1)
    mix = _matmul([(y_mix, w_out_full, TK)], "nn", [F32], name="out_proj")
    h1, u2 = _post_pre_norm(xs, mix, norm_mix_post, norm_mlp_pre)
    hid, act = _matmul([(u2, g_up, TK)], "nn", [BF16, BF16], name="mlp_up", epilogue=_relu_sq)
    ff = _matmul([(act, w_down_full, TK)], "nn", [F32], name="mlp_down")
    dh2, dff, d_g4, loss_part = _tail(ff, h1, target, norm_mlp_post)

    dhid = _matmul([(dff, w_down_full, TK)], "nt", [BF16], name="mlp_down_dx", epilogue=_relu_sq_bwd, extras=[hid])
    weights = {"norm_mix_pre": (norm_mix_pre, m_norm_mix_pre, v_norm_mix_pre), "w_in": (w_in, m_w_in, v_w_in),
               "conv_w": (conv_w, m_conv_w, v_conv_w), "conv_b": (conv_b, m_conv_b, v_conv_b),
               "dt_bias": (dt_bias, m_dt_bias, v_dt_bias), "a_log": (a_log, m_a_log, v_a_log),
               "d_skip": (d_skip, m_d_skip, v_d_skip), "ssm_norm_w": (ssm_norm_w, m_ssm_norm_w, v_ssm_norm_w),
               "w_out": (w_out, m_w_out, v_w_out), "norm_mix_post": (norm_mix_post, m_norm_mix_post, v_norm_mix_post),
               "norm_mlp_pre": (norm_mlp_pre, m_norm_mlp_pre, v_norm_mlp_pre), "w_up": (w_up, m_w_up, v_w_up),
               "w_down": (w_down, m_w_down, v_w_down),
               "norm_mlp_post": (norm_mlp_post, m_norm_mlp_post, v_norm_mlp_post)}
    grads, delta, new_m, new_v = {}, {}, {}, {}

    def adamw_big(n, halves):
        w, m, v = weights[n]
        g_, d_, m_, v_ = _adamw_halves(w[0], halves[0], halves[1], m[0], v[0], f"adamw_{n}")
        grads[n], delta[n], new_m[n], new_v[n] = g_[None], d_[None], m_[None], v_[None]

    dw_down = _matmul([(act, dff, TK)], "tn", [F32], name="mlp_down_dw")
    rs_down = _AsyncReduceScatter(dw_down.reshape(N_CHIPS, D_FF // N_CHIPS, D_MODEL), "w_down", 11)
    dw_up = _matmul([(u2, dhid, TK)], "tn", [F32], name="mlp_up_dw", deps=[dw_down], out_quarters=True)
    rs_up = _AsyncReduceScatter(dw_up, "w_up", 8)
    du2 = _matmul([(dhid, g_up, TK)], "nt", [F32], name="mlp_up_dx",
                  deps=[rs_down.sibling_sum(not_before=[dw_up])])
    dh1, dmix, d_g3, d_g2 = _mid_bwd(du2, h1, dh2, mix, norm_mix_post, norm_mlp_pre,
                                     deps=[rs_up.sibling_sum(not_before=[du2])])
    dymix = _matmul([(dmix, w_out_full, TK)], "nt", [F32], name="out_proj_dx")
    dw_out = _matmul([(y_mix, dmix, TK)], "tn", [F32], name="out_proj_dw")
    rs_out = _AsyncReduceScatter(dw_out.reshape(N_CHIPS, D_MIX // N_CHIPS, D_MODEL), "w_out", 5)
    dqkv = _attn_fused_bwd(qkv, dymix, y_att_f32, lse)
    par_late = _after(par, [rs_down.chip_sum(not_before=[dqkv]), rs_out.sibling_sum(not_before=[dymix])],
                      "after_w_down")
    dxc, dz, ddtg, dpar, d_nw = _ssd_bwd(xc, z, dtg, par_late, ssm_norm_w, y, states, dymix)
    g_down = rs_down.share()
    dxbc, dw8 = _conv_bwd(xbc, _after(w8, [*g_down, rs_up.chip_sum(not_before=[dxc])], "after_w_up"), dxc)
    ddt = jnp.pad(_dt_from_groups(ddtg), ((0, 0), (0, LANES - SSM_HEADS))).astype(BF16)
    g_up = rs_up.share()
    dw_z = _matmul([(u, dz, TK)], "tn", [F32], name="proj_z_dw")
    dw_xbc = _matmul([(u, dxbc, TK)], "tn", [F32], name="proj_xbc_dw",
                     deps=[*g_up, rs_out.chip_sum(not_before=[dxbc])])
    g_out = rs_out.share()
    dw_dt = _matmul([(u, ddt, TK)], "tn", [F32], name="proj_dt_dw")
    dw_qkv = _matmul([(u, dqkv, TK)], "tn", [F32], name="proj_qkv_dw")
    dw_in = _quarters_from_cols({"z": dw_z, "xbc": _unperm_cols(dw_xbc), "dt": dw_dt[:, :SSM_HEADS], "qkv": dw_qkv})
    rs_in = _AsyncReduceScatter(dw_in, "w_in", 2)
    adamw_big("w_down", g_down)
    adamw_big("w_up", g_up)
    rs_in.sibling_sum(not_before=[delta["w_up"]])
    du = _matmul([(dz, w_z, TK_MULTI), (dxbc, w_xbc, TK_MULTI), (dqkv, w_qkv, TK_MULTI), (ddt, w_dt, LANES)], "nt",
                 [F32], name="proj_dx", deps=[*g_out, rs_in.part_b])
    grad_x, d_g1 = _first_bwd(du, xs, dh1, norm_mix_pre)
    adamw_big("w_out", g_out)
    rs_in.chip_sum(not_before=[grad_x, delta["w_out"]])

    dconv = _unperm_cols(dw8)
    d_bias, d_alog, d_dskip = _unpack_ssd_params(dpar)
    small_shapes = [(1, D_MODEL), (CONV_WIDTH, D_XBC), (1, D_XBC), (1, SSM_HEADS), (1, SSM_HEADS), (1, SSM_HEADS),
                    (1, D_SSM), (1, D_MODEL), (1, D_MODEL), (1, D_MODEL), (1, LANES)]
    summed = _unpack_rows(
        _all_sum_small(_pack_rows([d_g1, dconv[:CONV_WIDTH], dconv[CONV_WIDTH:CONV_WIDTH + 1], d_bias, d_alog,
                                   d_dskip, d_nw, d_g2, d_g3, d_g4, loss_part])), small_shapes)
    (g_g1, g_conv_full, g_conv_b, g_bias, g_alog, g_dskip, g_nw, g_g2, g_g3, g_g4, loss_row) = summed
    loss = loss_row[0, 0]
    g_conv_w = lax.dynamic_slice(g_conv_full, (0, chip * conv_cols), (CONV_WIDTH, conv_cols))[None]

    grads.update({"norm_mix_pre": g_g1, "conv_w": g_conv_w, "conv_b": g_conv_b, "dt_bias": g_bias,
                  "a_log": g_alog, "d_skip": g_dskip, "ssm_norm_w": g_nw, "norm_mix_post": g_g2,
                  "norm_mlp_pre": g_g3, "norm_mlp_post": g_g4})
    order = list(weights)
    small_names = [n for n in order if n not in ("w_in", "w_out", "w_up", "w_down")]
    small_w_shapes = [weights[n][0].shape for n in small_names]
    packed = [_pack_rows([weights[n][k] for n in small_names]) for k in range(3)]
    packed_g = _pack_rows([grads[n].reshape(weights[n][0].shape) for n in small_names])
    sd, sm, sv = _adamw(packed[0], packed_g, packed[1], packed[2], "adamw_small")
    for k, n in enumerate(small_names):
        grads[n] = grads[n].reshape(weights[n][0].shape)
    for res, pk in ((delta, sd), (new_m, sm), (new_v, sv)):
        for n, val in zip(small_names, _unpack_rows(pk, small_w_shapes)):
            res[n] = val
    adamw_big("w_in", rs_in.share())

    return (loss, grad_x[None], *[grads[n] for n in order], *[delta[n] for n in order],
            *[new_m[n] for n in order], *[new_v[n] for n in order])
```

```python
import functools
import math

import numpy as np
import jax
import jax.numpy as jnp
from jax import lax
from jax.experimental import pallas as pl
from jax.experimental.pallas import tpu as pltpu
from jax.experimental.pallas import tpu_sc as plsc

F32 = jnp.float32
BF16 = jnp.bfloat16

D_MODEL = 2048
SSM_HEAD_DIM = 64
SSM_GROUPS = 8
HEADS_PER_GROUP = 4
SSM_HEADS = SSM_GROUPS * HEADS_PER_GROUP
D_SSM = SSM_HEADS * SSM_HEAD_DIM
D_STATE = 128
CONV_WIDTH = 4
SSD_CHUNK = 128
D_XBC = D_SSM + 2 * SSM_GROUPS * D_STATE
GROUP_X = HEADS_PER_GROUP * SSM_HEAD_DIM
GROUP_COLS = GROUP_X + 2 * D_STATE
ATT_HEAD_DIM = 128
ATT_HEADS = 16
D_ATT = ATT_HEADS * ATT_HEAD_DIM
DILATIONS = (1, 4, 16)
ATT_BLOCK = 128
D_MIX = D_SSM + D_ATT
D_IN_PROJ = D_SSM + D_XBC + SSM_HEADS + 3 * D_ATT
D_FF = 4 * D_MODEL
EPS = 1e-6
N_CHIPS = 4
W_IN_SHARD = D_IN_PROJ // N_CHIPS

ADAM_LR = 0.001
ADAM_B1 = 0.9
ADAM_B2 = 0.999
ADAM_EPS = 1e-08
ADAM_WD = 0.01
ADAM_STEP = 10

LANES = 128
VMEM_LIMIT = 48 * 1024 * 1024
MESH = pl.DeviceIdType.MESH

_NN = (((1,), (0,)), ((), ()))
_NT = (((1,), (1,)), ((), ()))
_TN = (((0,), (0,)), ((), ()))


def _dot(a, b, dims=_NN):
    return lax.dot_general(a, b, dims, preferred_element_type=F32)


def _cparams(*sem):
    return pltpu.CompilerParams(dimension_semantics=sem, vmem_limit_bytes=VMEM_LIMIT)


TK = 2048
TK_MULTI = 1024


def _matmul(pairs, mode, out_dtypes, *, name, tm=1024, tn=1024, epilogue=None, extras=(), deps=(), out_quarters=False):
    a0, b0, _ = pairs[0]
    m_dim = a0.shape[-1] if mode == "tn" else a0.shape[-2]
    if b0.ndim == 3:
        n_dim = b0.shape[1] if mode == "nt" else b0.shape[0] * b0.shape[2]
    else:
        n_dim = b0.shape[0] if mode == "nt" else b0.shape[1]
    tm, tn = min(tm, m_dim), min(tn, n_dim)
    nks, offs = [], []
    for a, _, tk in pairs:
        k_part = a.shape[0] if mode == "tn" else a.shape[-1]
        k_dim = k_part * (a.shape[0] if a.ndim == 3 else 1)
        assert k_part % tk == 0, (name, k_part, tk)
        offs.append(sum(nks))
        nks.append(k_dim // tk)
    nk_total = sum(nks)
    assert m_dim % tm == 0 and n_dim % tn == 0, (name, m_dim, n_dim)
    dims = {"nn": _NN, "nt": _NT, "tn": _TN}[mode]
    n_pairs, n_extra, n_out = len(pairs), len(extras), len(out_dtypes)

    in_specs, operands = [], []
    for (a, b, tk), off, nk in zip(pairs, offs, nks):
        def kidx(k, off=off, nk=nk):
            return k if n_pairs == 1 else jnp.clip(k - off, 0, nk - 1)
        if mode == "tn":
            assert a.ndim == 2
            in_specs.append(pl.BlockSpec((tk, tm), lambda m, n, k, f=kidx: (f(k), m)))
        elif a.ndim == 3:
            per = a.shape[2] // tk
            in_specs.append(pl.BlockSpec((None, tm, tk), lambda m, n, k, f=kidx, per=per: (f(k) // per, m, f(k) % per)))
        else:
            in_specs.append(pl.BlockSpec((tm, tk), lambda m, n, k, f=kidx: (m, f(k))))
        if b.ndim == 3 and mode == "nt":
            per = b.shape[2] // tk
            in_specs.append(pl.BlockSpec((None, tn, tk), lambda m, n, k, f=kidx, per=per: (f(k) // per, n, f(k) % per)))
        elif b.ndim == 3:
            per = b.shape[2] // tn
            in_specs.append(pl.BlockSpec((None, tk, tn), lambda m, n, k, f=kidx, per=per: (n // per, f(k), n % per)))
        elif mode == "nt":
            in_specs.append(pl.BlockSpec((tn, tk), lambda m, n, k, f=kidx: (n, f(k))))
        else:
            in_specs.append(pl.BlockSpec((tk, tn), lambda m, n, k, f=kidx: (f(k), n)))
        operands += [a, b]
    for e in extras:
        in_specs.append(pl.BlockSpec((tm, tn), lambda m, n, k: (m, n)))
        operands.append(e)
    in_specs += [pl.BlockSpec(memory_space=pl.ANY)] * len(deps)
    operands += list(deps)
    first_out = 2 * n_pairs + n_extra + len(deps)
    if out_quarters:
        out_per_q = n_dim // N_CHIPS // tn
        out_dims = (N_CHIPS, m_dim, n_dim // N_CHIPS)
        out_spec = pl.BlockSpec((None, tm, tn), lambda m, n, k: (n // out_per_q, m, n % out_per_q))
    else:
        out_dims = (m_dim, n_dim)
        out_spec = pl.BlockSpec((tm, tn), lambda m, n, k: (m, n))

    def body(*refs):
        ab = refs[:2 * n_pairs]
        e_refs = refs[2 * n_pairs:2 * n_pairs + n_extra]
        o_refs = refs[first_out:first_out + n_out]

        def finish(total):
            vals = (total,) if epilogue is None else epilogue(total, *[e[...] for e in e_refs])
            for o_ref, v in zip(o_refs, vals):
                o_ref[...] = v.astype(o_ref.dtype)

        if nk_total == 1:
            finish(_dot(ab[0][...], ab[1][...], dims))
            return
        acc = refs[-1]
        k = pl.program_id(2)

        @pl.when(k == 0)
        def _():
            acc[...] = jnp.zeros_like(acc)

        for i in range(n_pairs):
            def accumulate(i=i):
                acc[...] += _dot(ab[2 * i][...], ab[2 * i + 1][...], dims)
            if n_pairs == 1:
                accumulate()
            else:
                pl.when((k >= offs[i]) & (k < offs[i] + nks[i]))(accumulate)

        @pl.when(k == nk_total - 1)
        def _():
            finish(acc[...])

    outs = pl.pallas_call(
        body,
        grid=(m_dim // tm, n_dim // tn, nk_total),
        in_specs=in_specs,
        out_specs=[out_spec for _ in out_dtypes],
        out_shape=[jax.ShapeDtypeStruct(out_dims, dt) for dt in out_dtypes],
        scratch_shapes=[pltpu.VMEM((tm, tn), F32)] if nk_total > 1 else [],
        compiler_params=_cparams("parallel", "parallel", "arbitrary"),
        name=name,
    )(*operands)
    return outs[0] if n_out == 1 else outs


def _rowcall(fn, rows, vecs, row_outs, acc_widths, *, name, tr=256, row_cols=None, deps=()):
    s_dim = rows[0].shape[0]
    assert s_dim % tr == 0
    row_cols = row_cols or [None] * len(rows)
    n_r, n_v, n_ro, n_acc = len(rows), len(vecs), len(row_outs), len(acc_widths)
    in_specs = []
    for r, rc in zip(rows, row_cols):
        if rc is None:
            in_specs.append(pl.BlockSpec((tr, r.shape[1]), lambda i: (i, 0)))
        else:
            in_specs.append(pl.BlockSpec((tr, rc[0]), lambda i, c=rc[1]: (i, c)))
    for v in vecs:
        in_specs.append(pl.BlockSpec(v.shape, lambda i, nd=v.ndim: (0,) * nd))
    in_specs += [pl.BlockSpec(memory_space=pl.ANY)] * len(deps)
    n_d = len(deps)

    def body(*refs):
        ins = [r[...] for r in refs[:n_r + n_v]]
        ro = refs[n_r + n_v + n_d:n_r + n_v + n_d + n_ro]
        ao = refs[n_r + n_v + n_d + n_ro:]
        outs = fn(*ins)
        for ref, v in zip(ro, outs[:n_ro]):
            ref[...] = v.astype(ref.dtype)
        if n_acc:
            @pl.when(pl.program_id(0) == 0)
            def _():
                for ref in ao:
                    ref[...] = jnp.zeros_like(ref)
            for ref, v in zip(ao, outs[n_ro:]):
                ref[...] += v

    outs = pl.pallas_call(
        body,
        grid=(s_dim // tr,),
        in_specs=in_specs,
        out_specs=[pl.BlockSpec((tr, w), lambda i: (i, 0)) for w, _ in row_outs]
        + [pl.BlockSpec((1, w), lambda i: (0, 0)) for w in acc_widths],
        out_shape=[jax.ShapeDtypeStruct((s_dim, w), dt) for w, dt in row_outs]
        + [jax.ShapeDtypeStruct((1, w), F32) for w in acc_widths],
        compiler_params=_cparams("arbitrary"),
        name=name,
    )(*rows, *vecs, *deps)
    return outs


def _nrm(x, g):
    r = lax.rsqrt(jnp.mean(x * x, axis=-1, keepdims=True) + EPS)
    n = x * r
    return n * g, n, r


def _nrm_bwd(dy, n, r, g):
    dn = dy * g
    dx = r * (dn - n * jnp.mean(dn * n, axis=-1, keepdims=True))
    return dx, jnp.sum(dy * n, axis=0, keepdims=True)


def _sigmoid(x):
    return 1.0 / (1.0 + jnp.exp(-x))


def _softplus(x):
    return jnp.maximum(x, 0.0) + jnp.log(1.0 + jnp.exp(-jnp.abs(x)))


def _pre_norm(x, g1):
    def fn(xb, g):
        return (_nrm(xb, g)[0],)
    return _rowcall(fn, [x], [g1], [(D_MODEL, BF16)], [], name="pre_norm")[0]


def _post_pre_norm(x, mix, g2, g3):
    def fn(xb, mb, g2b, g3b):
        h1 = xb + _nrm(mb, g2b)[0]
        return h1, _nrm(h1, g3b)[0]
    return _rowcall(fn, [x, mix], [g2, g3], [(D_MODEL, F32), (D_MODEL, BF16)], [], name="post_pre_norm")


def _tail(ff, h1, target, g4):
    def fn(ffb, h1b, tb, g):
        y, n, r = _nrm(ffb, g)
        e = h1b + y - tb
        loss = 0.5 * jnp.sum(jnp.sum(e * e, axis=-1, keepdims=True) * (1.0 / D_MODEL), axis=0, keepdims=True)
        dh2 = e * (1.0 / D_MODEL)
        dff, dg = _nrm_bwd(dh2, n, r, g)
        return dh2, dff, dg, jnp.broadcast_to(loss, (1, LANES))
    return _rowcall(fn, [ff, h1, target], [g4], [(D_MODEL, F32), (D_MODEL, BF16)], [D_MODEL, LANES], name="tail")


def _mid_bwd(du2, h1, dh2, mix, g2, g3, deps=()):
    def fn(du2b, h1b, dh2b, mb, g2b, g3b):
        _, n3, r3 = _nrm(h1b, g3b)
        d3, dg3 = _nrm_bwd(du2b, n3, r3, g3b)
        dh1 = dh2b + d3
        _, n2, r2 = _nrm(mb, g2b)
        dmix, dg2 = _nrm_bwd(dh1, n2, r2, g2b)
        return dh1, dmix, dg3, dg2
    return _rowcall(fn, [du2, h1, dh2, mix], [g2, g3], [(D_MODEL, F32), (D_MODEL, BF16)], [D_MODEL, D_MODEL],
                    name="mid_bwd", deps=deps)


def _first_bwd(du, x, dh1, g1):
    def fn(dub, xb, dh1b, g):
        _, n, r = _nrm(xb, g)
        dx, dg = _nrm_bwd(dub, n, r, g)
        return dh1b + dx, dg
    return _rowcall(fn, [du, x, dh1], [g1], [(D_MODEL, F32)], [D_MODEL], name="first_bwd")


CONV_TILE = 256
CONV_ROWS = 256
PAD = 8


def _conv_taps(w):
    return [w[k:k + 1, :] for k in range(CONV_WIDTH)], w[CONV_WIDTH:CONV_WIDTH + 1, :]


def _conv_fwd(xbc, w8):
    s_dim, c_dim = xbc.shape
    n_steps = s_dim // CONV_ROWS

    def body(x_ref, w_ref, o_ref, xp):
        xp[0:PAD, :] = jnp.zeros((PAD, CONV_TILE), F32)
        xp[PAD:PAD + s_dim, :] = x_ref[...]
        taps, bias = _conv_taps(w_ref[...])

        def step(c, carry):
            base = pl.multiple_of(c * CONV_ROWS, CONV_ROWS)
            win = xp[pl.ds(base, CONV_ROWS + PAD), :]
            pre = bias + taps[3] * win[PAD:, :]
            for j in range(1, CONV_WIDTH):
                pre = pre + taps[3 - j] * pltpu.roll(win, j, axis=0)[PAD:, :]
            o_ref[pl.ds(base, CONV_ROWS), :] = pre * _sigmoid(pre)
            return carry

        lax.fori_loop(0, n_steps, step, 0)

    return pl.pallas_call(
        body,
        grid=(c_dim // CONV_TILE,),
        in_specs=[pl.BlockSpec((s_dim, CONV_TILE), lambda j: (0, j)), pl.BlockSpec((8, CONV_TILE), lambda j: (0, j))],
        out_specs=pl.BlockSpec((s_dim, CONV_TILE), lambda j: (0, j)),
        out_shape=jax.ShapeDtypeStruct((s_dim, c_dim), F32),
        scratch_shapes=[pltpu.VMEM((s_dim + 2 * PAD, CONV_TILE), F32)],
        compiler_params=_cparams("parallel"),
        name="conv_fwd",
    )(xbc, w8)


def _conv_bwd(xbc, w8, dxc):
    s_dim, c_dim = xbc.shape
    n_steps = s_dim // CONV_ROWS

    def body(x_ref, w_ref, d_ref, dx_ref, dw_ref, xp, dp):
        xp[0:PAD, :] = jnp.zeros((PAD, CONV_TILE), F32)
        xp[PAD:PAD + s_dim, :] = x_ref[...]
        dp[PAD + s_dim:, :] = jnp.zeros((PAD, CONV_TILE), F32)
        taps, bias = _conv_taps(w_ref[...])

        def step1(c, sums):
            base = pl.multiple_of(c * CONV_ROWS, CONV_ROWS)
            win = xp[pl.ds(base, CONV_ROWS + PAD), :]
            shifted = [win[PAD:, :]] + [pltpu.roll(win, j, axis=0)[PAD:, :] for j in range(1, CONV_WIDTH)]
            pre = bias
            for j in range(CONV_WIDTH):
                pre = pre + taps[3 - j] * shifted[j]
            sg = _sigmoid(pre)
            dpre = d_ref[pl.ds(base, CONV_ROWS), :] * (sg * (1.0 + pre * (1.0 - sg)))
            dp[pl.ds(base + PAD, CONV_ROWS), :] = dpre
            new = [sums[k] + jnp.sum(dpre * shifted[3 - k], axis=0, keepdims=True) for k in range(CONV_WIDTH)]
            new.append(sums[CONV_WIDTH] + jnp.sum(dpre, axis=0, keepdims=True))
            return tuple(new)

        zero = jnp.zeros((1, CONV_TILE), F32)
        sums = lax.fori_loop(0, n_steps, step1, (zero,) * (CONV_WIDTH + 1))
        dw_ref[...] = jnp.zeros((8, CONV_TILE), F32)
        for k in range(CONV_WIDTH + 1):
            dw_ref[k:k + 1, :] = sums[k]

        def step2(c, carry):
            base = pl.multiple_of(c * CONV_ROWS, CONV_ROWS)
            win = dp[pl.ds(base + PAD, CONV_ROWS + PAD), :]
            dx = taps[3] * win[:CONV_ROWS, :]
            for j in range(1, CONV_WIDTH):
                dx = dx + taps[3 - j] * pltpu.roll(win, CONV_ROWS + PAD - j, axis=0)[:CONV_ROWS, :]
            dx_ref[pl.ds(base, CONV_ROWS), :] = dx.astype(BF16)
            return carry

        lax.fori_loop(0, n_steps, step2, 0)

    col = lambda j: (0, j)
    return pl.pallas_call(
        body,
        grid=(c_dim // CONV_TILE,),
        in_specs=[pl.BlockSpec((s_dim, CONV_TILE), col), pl.BlockSpec((8, CONV_TILE), col),
                  pl.BlockSpec((s_dim, CONV_TILE), col)],
        out_specs=[pl.BlockSpec((s_dim, CONV_TILE), col), pl.BlockSpec((8, CONV_TILE), col)],
        out_shape=[jax.ShapeDtypeStruct((s_dim, c_dim), BF16), jax.ShapeDtypeStruct((8, c_dim), F32)],
        scratch_shapes=[pltpu.VMEM((s_dim + 2 * PAD, CONV_TILE), F32), pltpu.VMEM((s_dim + 2 * PAD, CONV_TILE), F32)],
        compiler_params=_cparams("parallel"),
        name="conv_bwd",
    )(xbc, w8, dxc)


def _perm_cols(a):
    parts = []
    for g in range(SSM_GROUPS):
        parts += [a[..., g * GROUP_X:(g + 1) * GROUP_X],
                  a[..., D_SSM + g * D_STATE:D_SSM + (g + 1) * D_STATE],
                  a[..., D_SSM + SSM_GROUPS * D_STATE + g * D_STATE:D_SSM + SSM_GROUPS * D_STATE + (g + 1) * D_STATE]]
    return jnp.concatenate(parts, axis=-1)


def _unperm_cols(a):
    xs = [a[..., g * GROUP_COLS:g * GROUP_COLS + GROUP_X] for g in range(SSM_GROUPS)]
    bs = [a[..., g * GROUP_COLS + GROUP_X:g * GROUP_COLS + GROUP_X + D_STATE] for g in range(SSM_GROUPS)]
    cs = [a[..., g * GROUP_COLS + GROUP_X + D_STATE:(g + 1) * GROUP_COLS] for g in range(SSM_GROUPS)]
    return jnp.concatenate(xs + bs + cs, axis=-1)


def _dt_to_groups(dt):
    s_dim = dt.shape[0]
    t = dt[:, :SSM_HEADS].reshape(s_dim, SSM_GROUPS, HEADS_PER_GROUP).transpose(1, 0, 2)
    return jnp.pad(t, ((0, 0), (0, 0), (0, LANES - HEADS_PER_GROUP)))


def _dt_from_groups(dtg):
    s_dim = dtg.shape[1]
    return dtg[:, :, :HEADS_PER_GROUP].transpose(1, 0, 2).reshape(s_dim, SSM_HEADS)


def _pack_ssd_params(dt_bias, a_log, d_skip):
    rows = jnp.stack([p.reshape(SSM_GROUPS, HEADS_PER_GROUP) for p in (dt_bias, a_log, d_skip)], axis=1)
    return jnp.pad(rows, ((0, 0), (0, 8 - 3), (0, LANES - HEADS_PER_GROUP)))


def _unpack_ssd_params(par):
    return tuple(par[:, k, :HEADS_PER_GROUP].reshape(SSM_HEADS) for k in range(3))


Q = SSD_CHUNK


def _split3(v):
    hi = v.astype(BF16)
    r1 = v - hi.astype(F32)
    mid = r1.astype(BF16)
    lo = (r1 - mid.astype(F32)).astype(BF16)
    return hi, mid, lo


def _dot_l01(t01, v):
    return sum(_dot(t01, p) for p in _split3(v))


def _dot_r01(v, e01):
    return sum(_dot(p, e01) for p in _split3(v))


def _ssd_consts():
    row = lax.broadcasted_iota(jnp.int32, (Q, Q), 0)
    col = lax.broadcasted_iota(jnp.int32, (Q, Q), 1)
    causal = row >= col
    tril = causal.astype(BF16)
    triu = (col >= row).astype(BF16)
    er = lax.broadcasted_iota(jnp.int32, (LANES, GROUP_X), 0)
    ec = lax.broadcasted_iota(jnp.int32, (LANES, GROUP_X), 1) // SSM_HEAD_DIM
    expand = (er == ec).astype(BF16)
    rr = lax.broadcasted_iota(jnp.int32, (GROUP_X, LANES), 0) // SSM_HEAD_DIM
    rc = lax.broadcasted_iota(jnp.int32, (GROUP_X, LANES), 1)
    reduce = (rr == rc).astype(BF16)
    lane_head = lax.broadcasted_iota(jnp.int32, (Q, GROUP_X), 1) // SSM_HEAD_DIM
    return causal, tril, triu, expand, reduce, lane_head


def _ssd_common(xc_ref, dt_ref, par_ref, consts):
    causal, tril, _, expand, _, _ = consts
    par = par_ref[...]
    bias, alog, dsk = par[0:1, :], par[1:2, :], par[2:3, :]
    a_neg = -jnp.exp(alog)
    dtr = dt_ref[...] + bias
    dt = _softplus(dtr)
    s = _dot_l01(tril, dt * a_neg)
    dt_x = _dot_r01(dt, expand)
    s_x = _dot_r01(s, expand)
    dsk_x = _dot_r01(jnp.broadcast_to(dsk, (8, LANES)), expand)[0:1, :]
    blk = xc_ref[...]
    x = blk[:, :GROUP_X]
    bm = blk[:, GROUP_X:GROUP_X + D_STATE].astype(BF16)
    cm = blk[:, GROUP_X + D_STATE:].astype(BF16)
    xdt = x * dt_x
    g = _dot(cm, bm, _NT)
    return dict(a_neg=a_neg, dtr=dtr, dt=dt, s=s, s_t=s.T, dt_x=dt_x, s_x=s_x, dsk_x=dsk_x, x=x, bm=bm, cm=cm,
                xdt=xdt, g=g)


def _decay(v, r, causal):
    diff = v["s"][:, r:r + 1] - v["s_t"][r:r + 1, :]
    return jnp.exp(jnp.where(causal, diff, -jnp.inf))


def _ssd_specs(n_chunks, rev):
    cidx = (lambda c: n_chunks - 1 - c) if rev else (lambda c: c)
    xc = pl.BlockSpec((Q, GROUP_COLS), lambda g, c: (cidx(c), g))
    gx = pl.BlockSpec((Q, GROUP_X), lambda g, c: (cidx(c), g))
    dt = pl.BlockSpec((None, Q, LANES), lambda g, c: (g, cidx(c), 0))
    par = pl.BlockSpec((None, 8, LANES), lambda g, c: (g, 0, 0))
    nw = pl.BlockSpec((1, GROUP_X), lambda g, c: (0, g))
    hs = pl.BlockSpec((None, None, D_STATE, GROUP_X), lambda g, c: (cidx(c), g, 0, 0))
    return xc, gx, dt, par, nw, hs


def _ssd_fwd(xc, z, dtg, par, nw):
    s_dim = xc.shape[0]
    n_chunks = s_dim // Q
    xc_s, gx_s, dt_s, par_s, nw_s, hs_s = _ssd_specs(n_chunks, False)

    def body(xc_ref, z_ref, dt_ref, par_ref, nw_ref, y_ref, ys_ref, hs_ref, ht):
        @pl.when(pl.program_id(1) == 0)
        def _():
            ht[...] = jnp.zeros_like(ht)

        consts = _ssd_consts()
        causal, lane_head = consts[0], consts[5]
        v = _ssd_common(xc_ref, dt_ref, par_ref, consts)
        xdt_b = v["xdt"].astype(BF16)
        yd = jnp.zeros((Q, GROUP_X), F32)
        for r in range(HEADS_PER_GROUP):
            m = (v["g"] * _decay(v, r, causal)).astype(BF16)
            yd = yd + _dot(m, jnp.where(lane_head == r, xdt_b, jnp.zeros_like(xdt_b)))
        h = ht[...]
        hs_ref[...] = h
        yo = jnp.exp(v["s_x"]) * _dot(v["cm"], h.astype(BF16))
        y = yd + yo + v["dsk_x"] * v["x"]
        s_last = v["s_x"][Q - 1:Q, :]
        snew = _dot(v["bm"], (v["xdt"] * jnp.exp(s_last - v["s_x"])).astype(BF16), _TN)
        ht[...] = jnp.exp(s_last) * h + snew
        zz = z_ref[...]
        yg = y * (zz * _sigmoid(zz))
        y_ref[...] = y
        ys_ref[...] = _nrm(yg, nw_ref[...])[0].astype(BF16)

    return pl.pallas_call(
        body,
        grid=(SSM_GROUPS, n_chunks),
        in_specs=[xc_s, gx_s, dt_s, par_s, nw_s],
        out_specs=[gx_s, gx_s, hs_s],
        out_shape=[jax.ShapeDtypeStruct((s_dim, D_SSM), F32), jax.ShapeDtypeStruct((s_dim, D_SSM), BF16),
                   jax.ShapeDtypeStruct((n_chunks, SSM_GROUPS, D_STATE, GROUP_X), F32)],
        scratch_shapes=[pltpu.VMEM((D_STATE, GROUP_X), F32)],
        compiler_params=_cparams("parallel", "arbitrary"),
        name="ssd_fwd",
    )(xc, z, dtg, par, nw)


def _ssd_bwd(xc, z, dtg, par, nw, y, hs, dymix):
    s_dim = xc.shape[0]
    n_chunks = s_dim // Q
    xc_s, gx_s, dt_s, par_s, nw_s, hs_s = _ssd_specs(n_chunks, True)

    def body(xc_ref, z_ref, dt_ref, par_ref, nw_ref, y_ref, hs_ref, dys_ref,
             dxc_ref, dz_ref, ddt_ref, dpar_ref, dnw_ref, dht):
        @pl.when(pl.program_id(1) == 0)
        def _():
            dht[...] = jnp.zeros_like(dht)
            dpar_ref[...] = jnp.zeros_like(dpar_ref)
            dnw_ref[...] = jnp.zeros_like(dnw_ref)

        consts = _ssd_consts()
        causal, _, triu, _, reduce, lane_head = consts
        v = _ssd_common(xc_ref, dt_ref, par_ref, consts)
        x, bm, cm, xdt, s_x = v["x"], v["bm"], v["cm"], v["xdt"], v["s_x"]
        h = hs_ref[...]
        hb = h.astype(BF16)
        es_x = jnp.exp(s_x)
        yo = es_x * _dot(cm, hb)
        s_last = s_x[Q - 1:Q, :]
        e_x = jnp.exp(s_last - s_x)
        es_last = jnp.exp(s_last)

        yv, zz, nw_v = y_ref[...], z_ref[...], nw_ref[...]
        sg = _sigmoid(zz)
        gz = zz * sg
        _, n, rstd = _nrm(yv * gz, nw_v)
        dout = dys_ref[...]
        dyg, dnw = _nrm_bwd(dout, n, rstd, nw_v)
        dnw_ref[...] += dnw
        dy = dyg * gz
        dz_ref[...] = (dyg * yv * (sg * (1.0 + zz * (1.0 - sg)))).astype(BF16)

        dyb = dy.astype(BF16)
        xdt_b = xdt.astype(BF16)
        dhp = dht[...]
        dhpb = dhp.astype(BF16)
        lane = lax.broadcasted_iota(jnp.int32, (Q, LANES), 1)
        sub = lax.broadcasted_iota(jnp.int32, (LANES, Q), 0)
        dxdt = jnp.zeros((Q, GROUP_X), F32)
        dg = jnp.zeros((Q, Q), F32)
        ds = jnp.zeros((Q, LANES), F32)
        ds_t = jnp.zeros((LANES, Q), F32)
        for r in range(HEADS_PER_GROUP):
            dec = _decay(v, r, causal)
            mf = v["g"] * dec
            dyr = jnp.where(lane_head == r, dyb, jnp.zeros_like(dyb))
            dm = _dot(dyr, xdt_b, _NT)
            dxdt = dxdt + _dot(mf.astype(BF16), dyr, _TN)
            dg = dg + dm * dec
            dd = dm * mf
            ds = ds + jnp.where(lane == r, jnp.sum(dd, axis=1, keepdims=True), 0.0)
            ds_t = ds_t + jnp.where(sub == r, jnp.sum(dd, axis=0, keepdims=True), 0.0)
        ds = ds - ds_t.T
        dgb = dg.astype(BF16)
        dwb = (es_x * dy).astype(BF16)
        dcm = _dot(dgb, bm) + _dot(dwb, hb, _NT)
        dh_prev = _dot(cm, dwb, _TN)
        zst = _dot(bm, dhpb)
        xe = xdt * e_x
        dxdt = dxdt + e_x * zst
        dee = xe * zst
        dbm = _dot(dgb, cm, _TN) + _dot(xe.astype(BF16), dhpb, _NT)
        v_last = jnp.sum(dee, axis=0, keepdims=True) + es_last * jnp.sum(dhp * h, axis=0, keepdims=True)
        row_x = lax.broadcasted_iota(jnp.int32, (Q, GROUP_X), 0)
        tx = dy * yo - dee + jnp.where(row_x == Q - 1, v_last, 0.0)
        ds = ds + _dot_r01(tx, reduce)
        ddta = _dot_l01(triu, ds)
        ddt = ddta * v["a_neg"] + _dot_r01(dxdt * x, reduce)
        dalog = jnp.sum(ddta * v["dt"], axis=0, keepdims=True) * v["a_neg"]
        draw = jnp.where(lane < HEADS_PER_GROUP, ddt * _sigmoid(v["dtr"]), 0.0)
        dbias = jnp.sum(draw, axis=0, keepdims=True)
        ddsk = _dot_r01(jnp.broadcast_to(jnp.sum(dy * x, axis=0, keepdims=True), (8, GROUP_X)), reduce)[0:1, :]
        dht[...] = es_last * dhp + dh_prev
        dxc_ref[:, :GROUP_X] = dxdt * v["dt_x"] + v["dsk_x"] * dy
        dxc_ref[:, GROUP_X:GROUP_X + D_STATE] = dbm
        dxc_ref[:, GROUP_X + D_STATE:] = dcm
        ddt_ref[...] = draw
        dpar_ref[0:1, :] += dbias
        dpar_ref[1:2, :] += dalog
        dpar_ref[2:3, :] += ddsk

    return pl.pallas_call(
        body,
        grid=(SSM_GROUPS, n_chunks),
        in_specs=[xc_s, gx_s, dt_s, par_s, nw_s, gx_s, hs_s, gx_s],
        out_specs=[xc_s, gx_s, dt_s, par_s, nw_s],
        out_shape=[jax.ShapeDtypeStruct((s_dim, SSM_GROUPS * GROUP_COLS), F32),
                   jax.ShapeDtypeStruct((s_dim, D_SSM), BF16),
                   jax.ShapeDtypeStruct((SSM_GROUPS, s_dim, LANES), F32),
                   jax.ShapeDtypeStruct((SSM_GROUPS, 8, LANES), F32),
                   jax.ShapeDtypeStruct((1, D_SSM), F32)],
        scratch_shapes=[pltpu.VMEM((D_STATE, GROUP_X), F32)],
        compiler_params=_cparams("parallel", "arbitrary"),
        name="ssd_bwd",
    )(xc, z, dtg, par, nw, y, hs, dymix)


ATT_SCALE = ATT_HEAD_DIM ** -0.5
NEG_INF = -jnp.inf


def _head(h):
    return slice(h * ATT_HEAD_DIM, (h + 1) * ATT_HEAD_DIM)


def _band_masks():
    qi = lax.broadcasted_iota(jnp.int32, (ATT_BLOCK, ATT_BLOCK), 0)
    kj = lax.broadcasted_iota(jnp.int32, (ATT_BLOCK, ATT_BLOCK), 1)
    return kj <= qi, kj >= qi


def _attn_fwd(qkv_v, d):
    rows = qkv_v.shape[0]
    nb = rows // ATT_BLOCK
    blk = (ATT_BLOCK, D_ATT)
    prev = lambda i: jnp.maximum(i - 1, 0)

    def body(q_ref, kc_ref, kp_ref, vc_ref, vp_ref, o_ref, lse_ref):
        own, before = _band_masks()
        before = before & (pl.program_id(1) > 0)
        lane = lax.broadcasted_iota(jnp.int32, (ATT_BLOCK, LANES), 1)
        lse_all = jnp.zeros((ATT_BLOCK, LANES), F32)
        for h in range(ATT_HEADS):
            q = q_ref[:, _head(h)]
            sc = jnp.where(own, _dot(q, kc_ref[:, _head(h)], _NT) * ATT_SCALE, NEG_INF)
            sp = jnp.where(before, _dot(q, kp_ref[:, _head(h)], _NT) * ATT_SCALE, NEG_INF)
            m = jnp.maximum(jnp.max(sc, axis=1, keepdims=True), jnp.max(sp, axis=1, keepdims=True))
            pc, pp = jnp.exp(sc - m), jnp.exp(sp - m)
            den = jnp.sum(pc, axis=1, keepdims=True) + jnp.sum(pp, axis=1, keepdims=True)
            o = _dot(pc.astype(BF16), vc_ref[:, _head(h)]) + _dot(pp.astype(BF16), vp_ref[:, _head(h)])
            o_ref[:, _head(h)] = o / den
            lse_all = jnp.where(lane == h, m + jnp.log(den), lse_all)
        lse_ref[...] = lse_all

    return pl.pallas_call(
        body,
        grid=(d, nb),
        in_specs=[pl.BlockSpec(blk, lambda r, i: (i, 3 * r)),
                  pl.BlockSpec(blk, lambda r, i: (i, 3 * r + 1)),
                  pl.BlockSpec(blk, lambda r, i: (prev(i), 3 * r + 1)),
                  pl.BlockSpec(blk, lambda r, i: (i, 3 * r + 2)),
                  pl.BlockSpec(blk, lambda r, i: (prev(i), 3 * r + 2))],
        out_specs=[pl.BlockSpec(blk, lambda r, i: (i, r)), pl.BlockSpec((ATT_BLOCK, LANES), lambda r, i: (i, r))],
        out_shape=[jax.ShapeDtypeStruct((rows, d * D_ATT), F32), jax.ShapeDtypeStruct((rows, d * LANES), F32)],
        compiler_params=_cparams("parallel", "arbitrary"),
        name=f"attn_fwd_d{d}",
    )(qkv_v, qkv_v, qkv_v, qkv_v, qkv_v)


def _attn_combine(os_, lses):
    def fn(o1, o2, o3, l1, l2, l3):
        m = jnp.maximum(jnp.maximum(l1, l2), l3)
        tot = m + jnp.log(jnp.exp(l1 - m) + jnp.exp(l2 - m) + jnp.exp(l3 - m))
        w1, w2, w3 = jnp.exp(l1 - tot), jnp.exp(l2 - tot), jnp.exp(l3 - tot)
        cols = []
        for h in range(ATT_HEADS):
            cols.append(w1[:, h:h + 1] * o1[:, _head(h)] + w2[:, h:h + 1] * o2[:, _head(h)]
                        + w3[:, h:h + 1] * o3[:, _head(h)])
        y = jnp.concatenate(cols, axis=1)
        return y, y, tot
    return _rowcall(fn, list(os_) + list(lses), [], [(D_ATT, BF16), (D_ATT, F32), (LANES, F32)], [],
                    name="attn_combine", tr=128)


def _attn_delta(dymix, y_att):
    def fn(dy, y):
        lane = lax.broadcasted_iota(jnp.int32, (dy.shape[0], LANES), 1)
        delta = jnp.zeros((dy.shape[0], LANES), F32)
        for h in range(ATT_HEADS):
            delta = jnp.where(lane == h, jnp.sum(dy[:, _head(h)] * y[:, _head(h)], axis=1, keepdims=True), delta)
        return dy, delta
    return _rowcall(fn, [dymix, y_att], [], [(D_ATT, BF16), (LANES, F32)], [], name="attn_delta",
                    row_cols=[(D_ATT, 1), None])


def _attn_bwd(qkv_v, dy_v, lse_v, delta_v, d):
    rows = qkv_v.shape[0]
    nb = rows // ATT_BLOCK
    blk = (ATT_BLOCK, D_ATT)
    sblk = (ATT_BLOCK, LANES)
    prev = lambda i: jnp.maximum(i - 1, 0)
    nxt = lambda i: jnp.minimum(i + 1, nb - 1)

    def body(qc_ref, qn_ref, kc_ref, kp_ref, vc_ref, vp_ref, dyc_ref, dyn_ref, lc_ref, ln_ref, dc_ref, dn_ref,
             dq_ref, dk_ref, dv_ref):
        i = pl.program_id(1)
        own, before = _band_masks()
        before_c = before & (i > 0)
        before_n = before & (i < nb - 1)
        lc, ln, dc, dn = lc_ref[...], ln_ref[...], dc_ref[...], dn_ref[...]
        for h in range(ATT_HEADS):
            hs = _head(h)
            q, qn, kc, kp, vc, vp = qc_ref[:, hs], qn_ref[:, hs], kc_ref[:, hs], kp_ref[:, hs], vc_ref[:, hs], vp_ref[:, hs]
            dy, dyn = dyc_ref[:, hs], dyn_ref[:, hs]
            lse, lse_n, dl, dl_n = lc[:, h:h + 1], ln[:, h:h + 1], dc[:, h:h + 1], dn[:, h:h + 1]
            pc = jnp.exp(jnp.where(own, _dot(q, kc, _NT) * ATT_SCALE - lse, NEG_INF))
            pp = jnp.exp(jnp.where(before_c, _dot(q, kp, _NT) * ATT_SCALE - lse, NEG_INF))
            pn = jnp.exp(jnp.where(before_n, _dot(qn, kc, _NT) * ATT_SCALE - lse_n, NEG_INF))
            dsc = (pc * (_dot(dy, vc, _NT) - dl)).astype(BF16)
            dsp = (pp * (_dot(dy, vp, _NT) - dl)).astype(BF16)
            dsn = (pn * (_dot(dyn, vc, _NT) - dl_n)).astype(BF16)
            dq_ref[:, hs] = (_dot(dsc, kc) + _dot(dsp, kp)) * ATT_SCALE
            dk_ref[:, hs] = (_dot(dsc, q, _TN) + _dot(dsn, qn, _TN)) * ATT_SCALE
            dv_ref[:, hs] = _dot(pc.astype(BF16), dy, _TN) + _dot(pn.astype(BF16), dyn, _TN)

    return pl.pallas_call(
        body,
        grid=(d, nb),
        in_specs=[pl.BlockSpec(blk, lambda r, i: (i, 3 * r)), pl.BlockSpec(blk, lambda r, i: (nxt(i), 3 * r)),
                  pl.BlockSpec(blk, lambda r, i: (i, 3 * r + 1)), pl.BlockSpec(blk, lambda r, i: (prev(i), 3 * r + 1)),
                  pl.BlockSpec(blk, lambda r, i: (i, 3 * r + 2)), pl.BlockSpec(blk, lambda r, i: (prev(i), 3 * r + 2)),
                  pl.BlockSpec(blk, lambda r, i: (i, r)), pl.BlockSpec(blk, lambda r, i: (nxt(i), r)),
                  pl.BlockSpec(sblk, lambda r, i: (i, r)), pl.BlockSpec(sblk, lambda r, i: (nxt(i), r)),
                  pl.BlockSpec(sblk, lambda r, i: (i, r)), pl.BlockSpec(sblk, lambda r, i: (nxt(i), r))],
        out_specs=[pl.BlockSpec(blk, lambda r, i: (i, r))] * 3,
        out_shape=[jax.ShapeDtypeStruct((rows, d * D_ATT), F32)] * 3,
        compiler_params=_cparams("parallel", "arbitrary"),
        name=f"attn_bwd_d{d}",
    )(qkv_v, qkv_v, qkv_v, qkv_v, qkv_v, qkv_v, dy_v, dy_v, lse_v, lse_v, delta_v, delta_v)


def _attn_sum(dqs, dks, dvs, deps=()):
    def fn(*parts):
        return (jnp.concatenate([parts[0] + parts[1] + parts[2], parts[3] + parts[4] + parts[5],
                                 parts[6] + parts[7] + parts[8]], axis=1),)
    return _rowcall(fn, list(dqs) + list(dks) + list(dvs), [], [(3 * D_ATT, BF16)], [], name="attn_sum", tr=128,
                    deps=deps)[0]


def _attention_fwd(qkv):
    s_dim = qkv.shape[0]
    os_, lses = [], []
    for d in DILATIONS:
        o, lse = _attn_fwd(qkv.reshape(s_dim // d, d * 3 * D_ATT), d)
        os_.append(o.reshape(s_dim, D_ATT))
        lses.append(lse.reshape(s_dim, LANES))
    return _attn_combine(os_, lses)


def _attention_bwd(qkv, dymix, y_att, lse, sum_deps=()):
    s_dim = qkv.shape[0]
    dy, delta = _attn_delta(dymix, y_att)
    dqs, dks, dvs = [], [], []
    for d in DILATIONS:
        dq, dk, dv = _attn_bwd(qkv.reshape(s_dim // d, d * 3 * D_ATT), dy.reshape(s_dim // d, d * D_ATT),
                               lse.reshape(s_dim // d, d * LANES), delta.reshape(s_dim // d, d * LANES), d)
        dqs.append(dq.reshape(s_dim, D_ATT))
        dks.append(dk.reshape(s_dim, D_ATT))
        dvs.append(dv.reshape(s_dim, D_ATT))
    return _attn_sum(dqs, dks, dvs, sum_deps)


WIN = ATT_BLOCK * DILATIONS[-1]
N_BLOCKS = WIN // ATT_BLOCK


def _rows(start, d):
    return pl.ds(start, ATT_BLOCK) if d == 1 else pl.ds(start, ATT_BLOCK, stride=d)


def _block_start(idx, d):
    return (idx // d) * (ATT_BLOCK * d) + idx % d


def _lane_bcast(col):
    return jnp.broadcast_to(col, (col.shape[0], LANES))


def _attn_fused_fwd(qkv):
    s_dim = qkv.shape[0]
    n_win = s_dim // WIN
    blk = (WIN, ATT_HEAD_DIM)
    prev = lambda w: jnp.maximum(w - 1, 0)

    def body(q_ref, kc_ref, kp_ref, vc_ref, vp_ref, y_ref, yf_ref, lse_ref, qf, kf, vf, acc, m_run, l_run):
        w, h = pl.program_id(0), pl.program_id(1)
        qf[...] = q_ref[...].astype(F32)
        kf[0:WIN, :] = kp_ref[...].astype(F32)
        kf[WIN:, :] = kc_ref[...].astype(F32)
        vf[0:WIN, :] = vp_ref[...].astype(F32)
        vf[WIN:, :] = vc_ref[...].astype(F32)
        own, before = _band_masks()

        for d in DILATIONS:
            def block(idx, carry, d=d):
                start = _block_start(idx, d)
                rows = _rows(start, d)
                q = qf[rows, :].astype(BF16)
                kc, vc = kf[_rows(WIN + start, d), :].astype(BF16), vf[_rows(WIN + start, d), :].astype(BF16)
                kp = kf[_rows(WIN + start - ATT_BLOCK * d, d), :].astype(BF16)
                vp = vf[_rows(WIN + start - ATT_BLOCK * d, d), :].astype(BF16)
                has_prev = (idx >= d) | (w > 0)
                sc = jnp.where(own, _dot(q, kc, _NT) * ATT_SCALE, NEG_INF)
                sp = jnp.where(before & has_prev, _dot(q, kp, _NT) * ATT_SCALE, NEG_INF)
                m_blk = jnp.maximum(jnp.max(sc, axis=1, keepdims=True), jnp.max(sp, axis=1, keepdims=True))
                if d == DILATIONS[0]:
                    m_new = m_blk
                else:
                    m_old = m_run[rows, :][:, 0:1]
                    m_new = jnp.maximum(m_old, m_blk)
                pc, pp = jnp.exp(sc - m_new), jnp.exp(sp - m_new)
                l_new = jnp.sum(pc, axis=1, keepdims=True) + jnp.sum(pp, axis=1, keepdims=True)
                o_new = _dot(pc.astype(BF16), vc) + _dot(pp.astype(BF16), vp)
                if d != DILATIONS[0]:
                    alpha = jnp.exp(m_old - m_new)
                    l_new = alpha * l_run[rows, :][:, 0:1] + l_new
                    o_new = alpha * acc[rows, :] + o_new
                m_run[rows, :] = _lane_bcast(m_new)
                l_run[rows, :] = _lane_bcast(l_new)
                acc[rows, :] = o_new
                return carry

            for idx in range(N_BLOCKS):
                block(idx, 0)

        l_all = l_run[...]
        y = acc[...] / l_all
        y_ref[...] = y.astype(BF16)
        yf_ref[...] = y
        @pl.when(h == 0)
        def _():
            lse_ref[...] = jnp.zeros_like(lse_ref)

        lane = lax.broadcasted_iota(jnp.int32, (WIN, LANES), 1)
        lse_ref[...] = jnp.where(lane == h, m_run[...] + jnp.log(l_all), lse_ref[...])

    win_scratch = lambda rows: pltpu.VMEM((rows, ATT_HEAD_DIM), F32)
    return pl.pallas_call(
        body,
        grid=(n_win, ATT_HEADS),
        in_specs=[pl.BlockSpec(blk, lambda w, h: (w, h)),
                  pl.BlockSpec(blk, lambda w, h: (w, ATT_HEADS + h)),
                  pl.BlockSpec(blk, lambda w, h: (prev(w), ATT_HEADS + h)),
                  pl.BlockSpec(blk, lambda w, h: (w, 2 * ATT_HEADS + h)),
                  pl.BlockSpec(blk, lambda w, h: (prev(w), 2 * ATT_HEADS + h))],
        out_specs=[pl.BlockSpec(blk, lambda w, h: (w, h)), pl.BlockSpec(blk, lambda w, h: (w, h)),
                   pl.BlockSpec((WIN, LANES), lambda w, h: (w, 0))],
        out_shape=[jax.ShapeDtypeStruct((s_dim, D_ATT), BF16), jax.ShapeDtypeStruct((s_dim, D_ATT), F32),
                   jax.ShapeDtypeStruct((s_dim, LANES), F32)],
        scratch_shapes=[win_scratch(WIN), win_scratch(2 * WIN), win_scratch(2 * WIN), win_scratch(WIN),
                        win_scratch(WIN), win_scratch(WIN)],
        compiler_params=_cparams("parallel", "arbitrary"),
        name="attn_fused_fwd",
    )(qkv, qkv, qkv, qkv, qkv)


def _attn_fused_bwd(qkv, dymix, y_att, lse, deps=()):
    s_dim = qkv.shape[0]
    n_win = s_dim // WIN
    blk = (WIN, ATT_HEAD_DIM)
    prev = lambda w: jnp.maximum(w - 1, 0)
    nxt = lambda w: jnp.minimum(w + 1, n_win - 1)
    n_dep = len(deps)

    def body(qc_ref, qn_ref, kc_ref, kp_ref, vc_ref, vp_ref, dyc_ref, dyn_ref, yc_ref, yn_ref, lc_ref, ln_ref, *rest):
        out_ref = rest[n_dep]
        qf, qnf, kf, vf, dq_acc, dk_acc, dv_acc, ls_c, dl_c, ls_n, dl_n = rest[n_dep + 1:]
        w, h = pl.program_id(0), pl.program_id(1)
        qf[...] = qc_ref[...].astype(F32)
        qnf[...] = qn_ref[...].astype(F32)
        kf[0:WIN, :] = kp_ref[...].astype(F32)
        kf[WIN:, :] = kc_ref[...].astype(F32)
        vf[0:WIN, :] = vp_ref[...].astype(F32)
        vf[WIN:, :] = vc_ref[...].astype(F32)
        lane = lax.broadcasted_iota(jnp.int32, (WIN, LANES), 1)
        pick = lambda ref: _lane_bcast(jnp.sum(jnp.where(lane == h, ref[...], 0.0), axis=1, keepdims=True))
        ls_c[...] = pick(lc_ref)
        ls_n[...] = pick(ln_ref)
        dl_c[...] = _lane_bcast(jnp.sum(dyc_ref[...] * yc_ref[...], axis=1, keepdims=True))
        dl_n[...] = _lane_bcast(jnp.sum(dyn_ref[...] * yn_ref[...], axis=1, keepdims=True))
        for ref in (dq_acc, dk_acc, dv_acc):
            ref[...] = jnp.zeros_like(ref)
        own, before = _band_masks()

        def probs(q, k, v, dy, lse_col, dl_col, mask):
            p = jnp.exp(jnp.where(mask, _dot(q, k, _NT) * ATT_SCALE - lse_col, NEG_INF))
            ds = p * (_dot(dy, v, _NT) - dl_col)
            return p.astype(BF16), ds.astype(BF16)

        for d in DILATIONS:
            def block(idx, carry, d=d):
                start = _block_start(idx, d)
                rows = _rows(start, d)
                q, dy = qf[rows, :].astype(BF16), dyc_ref[rows, :].astype(BF16)
                lse_col, dl_col = ls_c[rows, :][:, 0:1], dl_c[rows, :][:, 0:1]
                kc, vc = kf[_rows(WIN + start, d), :].astype(BF16), vf[_rows(WIN + start, d), :].astype(BF16)
                kp = kf[_rows(WIN + start - ATT_BLOCK * d, d), :].astype(BF16)
                vp = vf[_rows(WIN + start - ATT_BLOCK * d, d), :].astype(BF16)
                pc, dsc = probs(q, kc, vc, dy, lse_col, dl_col, own)
                pp, dsp = probs(q, kp, vp, dy, lse_col, dl_col, before & ((idx >= d) | (w > 0)))
                dq_acc[rows, :] += (_dot(dsc, kc) + _dot(dsp, kp)) * ATT_SCALE
                dk_acc[rows, :] += _dot(dsc, q, _TN) * ATT_SCALE
                dv_acc[rows, :] += _dot(pc, dy, _TN)

                if idx >= d:
                    prows = _rows(start - ATT_BLOCK * d, d)
                    dk_acc[prows, :] += _dot(dsp, q, _TN) * ATT_SCALE
                    dv_acc[prows, :] += _dot(pp, dy, _TN)
                return carry

            for idx in range(N_BLOCKS):
                block(idx, 0)

            def next_window(r, carry, d=d):
                krows = _rows(WIN - ATT_BLOCK * d + r, d)
                rows = _rows(r, d)
                q, dy = qnf[rows, :].astype(BF16), dyn_ref[rows, :].astype(BF16)
                k, v = kf[_rows(2 * WIN - ATT_BLOCK * d + r, d), :].astype(BF16), vf[_rows(2 * WIN - ATT_BLOCK * d + r, d), :].astype(BF16)
                pn, dsn = probs(q, k, v, dy, ls_n[rows, :][:, 0:1], dl_n[rows, :][:, 0:1], before & (w < n_win - 1))
                dk_acc[krows, :] += _dot(dsn, q, _TN) * ATT_SCALE
                dv_acc[krows, :] += _dot(pn, dy, _TN)
                return carry

            for r in range(d):
                next_window(r, 0)

        for part, acc_ref in enumerate((dq_acc, dk_acc, dv_acc)):
            out_ref[part] = acc_ref[...].astype(BF16)

    win_scratch = lambda rows: pltpu.VMEM((rows, ATT_HEAD_DIM), F32)
    cur = lambda c: pl.BlockSpec(blk, lambda w, h: (w, c + h))
    return pl.pallas_call(
        body,
        grid=(n_win, ATT_HEADS),
        in_specs=[cur(0), pl.BlockSpec(blk, lambda w, h: (nxt(w), h)),
                  cur(ATT_HEADS), pl.BlockSpec(blk, lambda w, h: (prev(w), ATT_HEADS + h)),
                  cur(2 * ATT_HEADS), pl.BlockSpec(blk, lambda w, h: (prev(w), 2 * ATT_HEADS + h)),
                  cur(ATT_HEADS), pl.BlockSpec(blk, lambda w, h: (nxt(w), ATT_HEADS + h)),
                  cur(0), pl.BlockSpec(blk, lambda w, h: (nxt(w), h)),
                  pl.BlockSpec((WIN, LANES), lambda w, h: (w, 0)), pl.BlockSpec((WIN, LANES), lambda w, h: (nxt(w), 0))]
        + [ANY] * n_dep,
        out_specs=pl.BlockSpec((3, WIN, ATT_HEAD_DIM), lambda w, h: (0, w, h)),
        out_shape=jax.ShapeDtypeStruct((3, s_dim, D_ATT), BF16),
        scratch_shapes=[win_scratch(WIN), win_scratch(WIN), win_scratch(2 * WIN), win_scratch(2 * WIN)]
        + [win_scratch(WIN)] * 7,
        compiler_params=_cparams("parallel", "arbitrary"),
        name="attn_fused_bwd",
    )(qkv, qkv, qkv, qkv, qkv, qkv, dymix, dymix, y_att, y_att, lse, lse, *deps)


def _adamw(w, g, m, v, name):
    def fn(wb, gb, mb, vb):
        m2 = ADAM_B1 * mb + (1.0 - ADAM_B1) * gb
        v2 = ADAM_B2 * vb + (1.0 - ADAM_B2) * (gb * gb)
        m_hat = m2 / (1.0 - ADAM_B1 ** ADAM_STEP)
        v_hat = v2 / (1.0 - ADAM_B2 ** ADAM_STEP)
        delta = -ADAM_LR * (m_hat / (jnp.sqrt(v_hat) + ADAM_EPS) + ADAM_WD * wb)
        return delta, m2, v2
    cols = w.shape[1]
    tr = 128 if w.shape[0] % 128 == 0 else w.shape[0]
    return _rowcall(fn, [w, g, m, v], [], [(cols, F32)] * 3, [], name=name, tr=tr)


ANY = pl.BlockSpec(memory_space=pl.ANY)


def _position():
    x, y, c = lax.axis_index("x"), lax.axis_index("y"), lax.axis_index("c")
    chips = [(1 - x, y), (x, 1 - y), (1 - x, 1 - y)]
    return x, y, c, chips


def _remote(src, dst, send_sem, recv_sem, device):
    return pltpu.make_async_remote_copy(src_ref=src, dst_ref=dst, send_sem=send_sem, recv_sem=recv_sem,
                                        device_id=device, device_id_type=MESH)


def _gather_shards(shards):
    n = len(shards)

    def body(*refs):
        ins, outs = refs[:n], refs[n:2 * n]
        send_sems, recv_sems = refs[2 * n:]
        x, y, c, chips = _position()
        sibling = (x, y, 1 - c)

        def half(a, j, cc):
            h = ins[a].shape[0] // 2
            return outs[a].at[j, pl.ds(cc * h, h), :]

        sent = []
        for a in range(n):
            h = ins[a].shape[0] // 2
            for j, chip in enumerate(chips):
                cp = _remote(ins[a].at[pl.ds(c * h, h), :], half(a, j, c), send_sems.at[6 * a + j],
                             recv_sems.at[6 * a + j], (chip[0], chip[1], c))
                cp.start()
                sent.append(cp)
        for a in range(n):
            for j in range(3):
                landed = half(a, j, c)
                _remote(landed, landed, send_sems.at[6 * a + j], recv_sems.at[6 * a + j], (x, y, c)).wait_recv()
                cp = _remote(landed, landed, send_sems.at[6 * a + 3 + j], recv_sems.at[6 * a + 3 + j], sibling)
                cp.start()
                sent.append(cp)
        for a in range(n):
            for j in range(3):
                handed = half(a, j, 1 - c)
                _remote(handed, handed, send_sems.at[6 * a + 3 + j], recv_sems.at[6 * a + 3 + j], (x, y, c)).wait_recv()
        for cp in sent:
            cp.wait_send()

    return pl.pallas_call(
        body,
        in_specs=[ANY] * n,
        out_specs=[ANY] * n,
        out_shape=[jax.ShapeDtypeStruct((3,) + s.shape, s.dtype) for s in shards],
        scratch_shapes=[pltpu.SemaphoreType.DMA((6 * n,)), pltpu.SemaphoreType.DMA((6 * n,))],
        name="gather_shards",
    )(*shards)


def _handshake(peers):
    barrier = pltpu.get_barrier_semaphore()
    for p in peers:
        pl.semaphore_signal(barrier, inc=1, device_id=p, device_id_type=MESH)
    pl.semaphore_wait(barrier, len(peers))


def _gather_shards_async(shards, collective_id, name):
    n = len(shards)
    srcs = [jax.new_ref(s, memory_space=pltpu.MemorySpace.HBM) for s in shards]
    dsts = [jax.empty_ref(jax.ShapeDtypeStruct((3,) + s.shape, s.dtype), memory_space=pltpu.MemorySpace.HBM)
            for s in shards]

    @pl.kernel(mesh=plsc.ScalarSubcoreMesh(axis_name="seq", num_cores=1), name=name,
               scratch_types=(pltpu.SemaphoreType.DMA((6 * n,)), pltpu.SemaphoreType.DMA((6 * n,))),
               compiler_params=pltpu.CompilerParams(collective_id=collective_id))
    def launch(send_sems, recv_sems):
        x, y, c, chips = _position()
        sibling = (x, y, 1 - c)
        _handshake([(chip[0], chip[1], c) for chip in chips] + [sibling])

        def half(a, j, cc):
            h = shards[a].shape[0] // 2
            return dsts[a].at[j, pl.ds(cc * h, h), :]

        sent = []
        for a in range(n):
            h = shards[a].shape[0] // 2
            for j, chip in enumerate(chips):
                cp = _remote(srcs[a].at[pl.ds(c * h, h), :], half(a, j, c), send_sems.at[6 * a + j],
                             recv_sems.at[6 * a + j], (chip[0], chip[1], c))
                cp.start()
                sent.append(cp)
        for a in range(n):
            for j in range(3):
                landed = half(a, j, c)
                _remote(landed, landed, send_sems.at[6 * a + j], recv_sems.at[6 * a + j], (x, y, c)).wait_recv()
                cp = _remote(landed, landed, send_sems.at[6 * a + 3 + j], recv_sems.at[6 * a + 3 + j], sibling)
                cp.start()
                sent.append(cp)
        for a in range(n):
            for j in range(3):
                handed = half(a, j, 1 - c)
                _remote(handed, handed, send_sems.at[6 * a + 3 + j], recv_sems.at[6 * a + 3 + j], (x, y, c)).wait_recv()
        for cp in sent:
            cp.wait_send()

    launch()
    return [d[...] for d in dsts]


IN_COLS = {"z": (0, D_SSM), "xbc": (D_SSM, D_SSM + D_XBC), "dt": (D_SSM + D_XBC, D_SSM + D_XBC + SSM_HEADS),
           "qkv": (D_SSM + D_XBC + SSM_HEADS, D_IN_PROJ)}


def _cols_from_quarters(quarters, lo, hi):
    parts = []
    for q in range(N_CHIPS):
        a, b = max(lo, q * W_IN_SHARD), min(hi, (q + 1) * W_IN_SHARD)
        if a < b:
            parts.append(quarters[q][:, a - q * W_IN_SHARD:b - q * W_IN_SHARD])
    return parts[0] if len(parts) == 1 else jnp.concatenate(parts, axis=1)


def _quarters_from_cols(pieces):
    quarters = []
    for q in range(N_CHIPS):
        parts = []
        for name, (lo, hi) in IN_COLS.items():
            a, b = max(lo, q * W_IN_SHARD), min(hi, (q + 1) * W_IN_SHARD)
            if a < b:
                parts.append(pieces[name][:, a - lo:b - lo])
        quarters.append(jnp.concatenate(parts, axis=1))
    return jnp.stack(quarters)


def _by_chip(own, others):
    me = 2 * lax.axis_index("x") + lax.axis_index("y")
    rel = jnp.stack([own, others[1], others[0], others[2]])
    return jnp.stack([lax.dynamic_index_in_dim(rel, q ^ me, 0, keepdims=False) for q in range(N_CHIPS)])


def _add_sibling(grad, got, c_arr, name, deps=()):
    nq, rows, cols = grad.shape
    h = rows // 2
    tr = 128
    nb = h // tr

    def body(c_ref, a_ref, b_ref, *rest):
        o_ref, ob_ref = rest[len(deps):]
        total = a_ref[...] + b_ref[...]
        o_ref[...] = total
        ob_ref[...] = total.astype(BF16)

    out_spec = pl.BlockSpec((None, tr, cols), lambda q, i, c: (q, i, 0))
    return pl.pallas_call(
        body,
        grid_spec=pltpu.PrefetchScalarGridSpec(
            num_scalar_prefetch=1, grid=(nq, nb),
            in_specs=[pl.BlockSpec((None, tr, cols), lambda q, i, c: (q, c[0] * nb + i, 0)),
                      pl.BlockSpec((None, tr, cols), lambda q, i, c: (q, i, 0))] + [ANY] * len(deps),
            out_specs=[out_spec, out_spec]),
        out_shape=[jax.ShapeDtypeStruct((nq, h, cols), F32), jax.ShapeDtypeStruct((nq, h, cols), BF16)],
        compiler_params=_cparams("parallel", "parallel"),
        name=name,
    )(c_arr, grad, got, *deps)


def _add_chips(part, got, chip_arr, name, deps=()):
    _, h, cols = part.shape
    tr = 128

    def body(q_ref, p_ref, g0_ref, g1_ref, g2_ref, *rest):
        o_ref = rest[len(deps)]
        o_ref[...] = ((p_ref[...] + g0_ref[...].astype(F32)) + g1_ref[...].astype(F32)) + g2_ref[...].astype(F32)

    got_spec = lambda j: pl.BlockSpec((None, tr, cols), lambda i, q: (j, i, 0))
    return pl.pallas_call(
        body,
        grid_spec=pltpu.PrefetchScalarGridSpec(
            num_scalar_prefetch=1, grid=(h // tr,),
            in_specs=[pl.BlockSpec((None, tr, cols), lambda i, q: (q[0], i, 0)), got_spec(0), got_spec(1), got_spec(2)]
            + [ANY] * len(deps),
            out_specs=pl.BlockSpec((tr, cols), lambda i, q: (i, 0))),
        out_shape=jax.ShapeDtypeStruct((h, cols), F32),
        compiler_params=_cparams("parallel"),
        name=name,
    )(chip_arr, part, got, got, got, *deps)


def _sequencer_exchange(src, out_shape, collective_id, name, plan, n_copies):
    src_ref = jax.new_ref(src, memory_space=pltpu.MemorySpace.HBM)
    dst_ref = jax.empty_ref(out_shape, memory_space=pltpu.MemorySpace.HBM)

    @pl.kernel(mesh=plsc.ScalarSubcoreMesh(axis_name="seq", num_cores=1), name=name,
               scratch_types=(pltpu.SemaphoreType.DMA((n_copies,)), pltpu.SemaphoreType.DMA((n_copies,))),
               compiler_params=pltpu.CompilerParams(collective_id=collective_id))
    def launch(send_sems, recv_sems):
        x, y, c, chips = _position()
        copies = plan(src_ref, dst_ref, x, y, c, chips)
        _handshake([peer for _, _, peer in copies])
        started = []
        for k, (s, d, peer) in enumerate(copies):
            cp = _remote(s, d, send_sems.at[k], recv_sems.at[k], peer)
            cp.start()
            started.append(cp)
        for cp in started:
            cp.wait()

    launch()
    return dst_ref[...]


class _AsyncReduceScatter:
    def __init__(self, grad, nm, first_id):
        self.grad, self.nm, self.first_id = grad, nm, first_id
        nq, rows, cols = grad.shape
        h = self.h = rows // 2

        def to_sibling(s, d, x, y, c, chips):
            return [(s.at[:, pl.ds((1 - c) * h, h), :], d, (x, y, 1 - c))]

        self.from_sibling = _sequencer_exchange(grad, jax.ShapeDtypeStruct((nq, h, cols), F32), first_id,
                                                f"rs_sibling_{nm}", to_sibling, 1)

    def sibling_sum(self, not_before=()):
        cols = self.grad.shape[2]
        c_arr = lax.axis_index("c").astype(jnp.int32).reshape(1)
        self.part, self.part_b = _add_sibling(self.grad, self.from_sibling, c_arr, f"add_sibling_{self.nm}", not_before)

        def to_chips(s, d, x, y, c, chips):
            return [(s.at[2 * chip[0] + chip[1]], d.at[j], (chip[0], chip[1], c)) for j, chip in enumerate(chips)]

        self.from_chips = _sequencer_exchange(self.part_b, jax.ShapeDtypeStruct((3, self.h, cols), BF16),
                                              self.first_id + 1, f"rs_quarters_{self.nm}", to_chips, 3)
        return self.part_b

    def chip_sum(self, not_before=()):
        cols = self.grad.shape[2]
        chip_arr = (2 * lax.axis_index("x") + lax.axis_index("y")).astype(jnp.int32).reshape(1)
        self.half = _add_chips(self.part, self.from_chips, chip_arr, f"add_chips_{self.nm}", not_before)

        def whole_to_sibling(s, d, x, y, c, chips):
            return [(s, d, (x, y, 1 - c))]

        self.other = _sequencer_exchange(self.half, jax.ShapeDtypeStruct((self.h, cols), F32), self.first_id + 2,
                                         f"rs_share_{self.nm}", whole_to_sibling, 1)
        return self.half

    def share(self):
        return self.half, self.other


def _after(x, deps, name):
    def body(x_ref, *rest):
        rest[-1][...] = x_ref[...]

    vm = pl.BlockSpec(memory_space=pltpu.VMEM)
    return pl.pallas_call(body, in_specs=[vm] + [ANY] * len(deps), out_specs=vm,
                          out_shape=jax.ShapeDtypeStruct(x.shape, x.dtype), name=name)(x, *deps)


def _adamw_halves(w, mine, other, m, v, name):
    rows, cols = w.shape
    tr = 128
    nb = rows // 2 // tr
    c_arr = lax.axis_index("c").astype(jnp.int32).reshape(1)

    def body(c_ref, w_ref, a_ref, b_ref, m_ref, v_ref, g_out, d_out, m_out, v_out):
        is_mine = (pl.program_id(0) // nb) == c_ref[0]
        g = jnp.where(is_mine, a_ref[...], b_ref[...])
        wb, mb, vb = w_ref[...], m_ref[...], v_ref[...]
        m2 = ADAM_B1 * mb + (1.0 - ADAM_B1) * g
        v2 = ADAM_B2 * vb + (1.0 - ADAM_B2) * (g * g)
        m_hat = m2 / (1.0 - ADAM_B1 ** ADAM_STEP)
        v_hat = v2 / (1.0 - ADAM_B2 ** ADAM_STEP)
        g_out[...] = g
        d_out[...] = -ADAM_LR * (m_hat / (jnp.sqrt(v_hat) + ADAM_EPS) + ADAM_WD * wb)
        m_out[...] = m2
        v_out[...] = v2

    full = pl.BlockSpec((tr, cols), lambda i, c: (i, 0))
    half = pl.BlockSpec((tr, cols), lambda i, c: (i % nb, 0))
    return pl.pallas_call(
        body,
        grid_spec=pltpu.PrefetchScalarGridSpec(
            num_scalar_prefetch=1, grid=(rows // tr,),
            in_specs=[full, half, half, full, full], out_specs=[full] * 4),
        out_shape=[jax.ShapeDtypeStruct((rows, cols), F32)] * 4,
        compiler_params=_cparams("parallel"),
        name=name,
    )(c_arr, w, mine, other, m, v)


def _all_sum_small(v):
    n_dev = 8

    def body(v_ref, o_ref, gath, send_sems, recv_sems):
        x, y, c, _ = _position()
        me = 4 * x + 2 * y + c
        gath[me] = v_ref[...]
        copies = []
        for k in range(1, n_dev):
            peer = tuple(1 - p if (k >> s) & 1 else p for p, s in ((x, 2), (y, 1), (c, 0)))
            cp = _remote(v_ref, gath.at[me], send_sems.at[k - 1], recv_sems.at[k - 1], peer)
            cp.start()
            copies.append(cp)
        for cp in copies:
            cp.wait()
        acc = gath[0]
        for i in range(1, n_dev):
            acc = acc + gath[i]
        o_ref[...] = acc

    vm = pl.BlockSpec(memory_space=pltpu.VMEM)
    return pl.pallas_call(
        body,
        in_specs=[vm],
        out_specs=vm,
        out_shape=jax.ShapeDtypeStruct(v.shape, F32),
        scratch_shapes=[pltpu.VMEM((n_dev,) + v.shape, F32), pltpu.SemaphoreType.DMA((n_dev - 1,)),
                        pltpu.SemaphoreType.DMA((n_dev - 1,))],
        name="all_sum_small",
    )(v)


def _pack_rows(vectors):
    rows = []
    for v in vectors:
        flat = v.reshape(-1).astype(F32)
        rows.append(jnp.pad(flat, (0, (-flat.shape[0]) % LANES)).reshape(-1, LANES))
    out = jnp.concatenate(rows, axis=0)
    return jnp.pad(out, ((0, (-out.shape[0]) % 8), (0, 0)))


def _unpack_rows(packed, shapes):
    outs, r = [], 0
    for shp in shapes:
        size = math.prod(shp)
        nr = -(-size // LANES)
        outs.append(packed[r:r + nr].reshape(-1)[:size].reshape(shp))
        r += nr
    return outs


def _relu_sq(acc):
    r = jnp.maximum(acc, 0.0)
    return r, r * r


def _relu_sq_bwd(acc, r):
    return (acc * (2.0 * r.astype(F32)),)


def kernel(x, norm_mix_pre, w_in, conv_w, conv_b, dt_bias, a_log, d_skip, ssm_norm_w, w_out, norm_mix_post, norm_mlp_pre, w_up, w_down, norm_mlp_post, loss_target, m_norm_mix_pre, m_w_in, m_conv_w, m_conv_b, m_dt_bias, m_a_log, m_d_skip, m_ssm_norm_w, m_w_out, m_norm_mix_post, m_norm_mlp_pre, m_w_up, m_w_down, m_norm_mlp_post, v_norm_mix_pre, v_w_in, v_conv_w, v_conv_b, v_dt_bias, v_a_log, v_d_skip, v_ssm_norm_w, v_w_out, v_norm_mix_post, v_norm_mlp_pre, v_w_up, v_w_down, v_norm_mlp_post):
    s_dim = x.shape[1]
    xs, target = x[0], loss_target[0]
    chip = 2 * lax.axis_index("x") + lax.axis_index("y")

    own = [w_in[0].astype(BF16), w_out[0].astype(BF16), w_up[0].astype(BF16), w_down[0].astype(BF16)]
    fetched = list(_gather_shards(own[:1])) + _gather_shards_async(own[1:], 1, "gather_rest")
    g_in, g_out, g_up, g_down = [_by_chip(o, f) for o, f in zip(own, fetched)]
    w_z = _cols_from_quarters(g_in, *IN_COLS["z"])
    w_xbc = _perm_cols(_cols_from_quarters(g_in, *IN_COLS["xbc"]))
    w_dt = jnp.pad(_cols_from_quarters(g_in, *IN_COLS["dt"]), ((0, 0), (0, LANES - SSM_HEADS)))
    w_qkv = _cols_from_quarters(g_in, *IN_COLS["qkv"])
    w_out_full = g_out.reshape(D_MIX, D_MODEL)
    w_down_full = g_down.reshape(D_FF, D_MODEL)

    conv_cols = D_XBC // N_CHIPS
    conv_placed = lax.dynamic_update_slice(jnp.zeros((8, D_XBC), F32), 0.5 * conv_w[0], (0, chip * conv_cols))
    conv_full = _all_sum_small(conv_placed.reshape(-1, LANES)).reshape(8, D_XBC)
    w8 = _perm_cols(conv_full.at[CONV_WIDTH].set(conv_b[0]))

    u = _pre_norm(xs, norm_mix_pre)
    z = _matmul([(u, w_z, TK)], "nn", [F32], name="proj_z")
    xbc = _matmul([(u, w_xbc, TK)], "nn", [F32], name="proj_xbc")
    dt_raw = _matmul([(u, w_dt, TK)], "nn", [F32], name="proj_dt")
    qkv = _matmul([(u, w_qkv, TK)], "nn", [BF16], name="proj_qkv")
    xc = _conv_fwd(xbc, w8)
    dtg = _dt_to_groups(dt_raw)
    par = _pack_ssd_params(dt_bias[0], a_log[0], d_skip[0])
    y, y_ssm, states = _ssd_fwd(xc, z, dtg, par, ssm_norm_w)
    y_att, y_att_f32, lse = _attn_fused_fwd(qkv)
    y_mix = jnp.concatenate([y_ssm, y_att], axis=1)
    mix = _matmul([(y_mix, w_out_full, TK)], "nn", [F32], name="out_proj")
    h1, u2 = _post_pre_norm(xs, mix, norm_mix_post, norm_mlp_pre)
    hid, act = _matmul([(u2, g_up, TK)], "nn", [BF16, BF16], name="mlp_up", epilogue=_relu_sq)
    ff = _matmul([(act, w_down_full, TK)], "nn", [F32], name="mlp_down")
    dh2, dff, d_g4, loss_part = _tail(ff, h1, target, norm_mlp_post)

    dhid = _matmul([(dff, w_down_full, TK)], "nt", [BF16], name="mlp_down_dx", epilogue=_relu_sq_bwd, extras=[hid])
    weights = {"norm_mix_pre": (norm_mix_pre, m_norm_mix_pre, v_norm_mix_pre), "w_in": (w_in, m_w_in, v_w_in),
               "conv_w": (conv_w, m_conv_w, v_conv_w), "conv_b": (conv_b, m_conv_b, v_conv_b),
               "dt_bias": (dt_bias, m_dt_bias, v_dt_bias), "a_log": (a_log, m_a_log, v_a_log),
               "d_skip": (d_skip, m_d_skip, v_d_skip), "ssm_norm_w": (ssm_norm_w, m_ssm_norm_w, v_ssm_norm_w),
               "w_out": (w_out, m_w_out, v_w_out), "norm_mix_post": (norm_mix_post, m_norm_mix_post, v_norm_mix_post),
               "norm_mlp_pre": (norm_mlp_pre, m_norm_mlp_pre, v_norm_mlp_pre), "w_up": (w_up, m_w_up, v_w_up),
               "w_down": (w_down, m_w_down, v_w_down),
               "norm_mlp_post": (norm_mlp_post, m_norm_mlp_post, v_norm_mlp_post)}
    grads, delta, new_m, new_v = {}, {}, {}, {}

    def adamw_big(n, halves):
        w, m, v = weights[n]
        g_, d_, m_, v_ = _adamw_halves(w[0], halves[0], halves[1], m[0], v[0], f"adamw_{n}")
        grads[n], delta[n], new_m[n], new_v[n] = g_[None], d_[None], m_[None], v_[None]

    dw_down = _matmul([(act, dff, TK)], "tn", [F32], name="mlp_down_dw")
    rs_down = _AsyncReduceScatter(dw_down.reshape(N_CHIPS, D_FF // N_CHIPS, D_MODEL), "w_down", 11)
    dw_up = _matmul([(u2, dhid, TK)], "tn", [F32], name="mlp_up_dw", deps=[dw_down], out_quarters=True)
    rs_up = _AsyncReduceScatter(dw_up, "w_up", 8)
    du2 = _matmul([(dhid, g_up, TK)], "nt", [F32], name="mlp_up_dx",
                  deps=[rs_down.sibling_sum(not_before=[dw_up])])
    dh1, dmix, d_g3, d_g2 = _mid_bwd(du2, h1, dh2, mix, norm_mix_post, norm_mlp_pre,
                                     deps=[rs_up.sibling_sum(not_before=[du2])])
    dymix = _matmul([(dmix, w_out_full, TK)], "nt", [F32], name="out_proj_dx")
    dw_out = _matmul([(y_mix, dmix, TK)], "tn", [F32], name="out_proj_dw")
    rs_out = _AsyncReduceScatter(dw_out.reshape(N_CHIPS, D_MIX // N_CHIPS, D_MODEL), "w_out", 5)
    dqkv = _attn_fused_bwd(qkv, dymix, y_att_f32, lse)
    par_late = _after(par, [rs_down.chip_sum(not_before=[dqkv]), rs_out.sibling_sum(not_before=[dymix])],
                      "after_w_down")
    dxc, dz, ddtg, dpar, d_nw = _ssd_bwd(xc, z, dtg, par_late, ssm_norm_w, y, states, dymix)
    g_down = rs_down.share()
    dxbc, dw8 = _conv_bwd(xbc, _after(w8, [*g_down, rs_up.chip_sum(not_before=[dxc])], "after_w_up"), dxc)
    ddt = jnp.pad(_dt_from_groups(ddtg), ((0, 0), (0, LANES - SSM_HEADS))).astype(BF16)
    g_up = rs_up.share()
    dw_z = _matmul([(u, dz, TK)], "tn", [F32], name="proj_z_dw")
    dw_xbc = _matmul([(u, dxbc, TK)], "tn", [F32], name="proj_xbc_dw",
                     deps=[*g_up, rs_out.chip_sum(not_before=[dxbc])])
    g_out = rs_out.share()
    dw_dt = _matmul([(u, ddt, TK)], "tn", [F32], name="proj_dt_dw")
    dw_qkv = _matmul([(u, dqkv, TK)], "tn", [F32], name="proj_qkv_dw")
    dw_in = _quarters_from_cols({"z": dw_z, "xbc": _unperm_cols(dw_xbc), "dt": dw_dt[:, :SSM_HEADS], "qkv": dw_qkv})
    rs_in = _AsyncReduceScatter(dw_in, "w_in", 2)
    adamw_big("w_down", g_down)
    adamw_big("w_up", g_up)
    rs_in.sibling_sum(not_before=[delta["w_up"]])
    du = _matmul([(dz, w_z, TK_MULTI), (dxbc, w_xbc, TK_MULTI), (dqkv, w_qkv, TK_MULTI), (ddt, w_dt, LANES)], "nt",
                 [F32], name="proj_dx", deps=[*g_out, rs_in.part_b])
    grad_x, d_g1 = _first_bwd(du, xs, dh1, norm_mix_pre)
    adamw_big("w_out", g_out)
    rs_in.chip_sum(not_before=[grad_x, delta["w_out"]])

    dconv = _unperm_cols(dw8)
    d_bias, d_alog, d_dskip = _unpack_ssd_params(dpar)
    small_shapes = [(1, D_MODEL), (CONV_WIDTH, D_XBC), (1, D_XBC), (1, SSM_HEADS), (1, SSM_HEADS), (1, SSM_HEADS),
                    (1, D_SSM), (1, D_MODEL), (1, D_MODEL), (1, D_MODEL), (1, LANES)]
    summed = _unpack_rows(
        _all_sum_small(_pack_rows([d_g1, dconv[:CONV_WIDTH], dconv[CONV_WIDTH:CONV_WIDTH + 1], d_bias, d_alog,
                                   d_dskip, d_nw, d_g2, d_g3, d_g4, loss_part])), small_shapes)
    (g_g1, g_conv_full, g_conv_b, g_bias, g_alog, g_dskip, g_nw, g_g2, g_g3, g_g4, loss_row) = summed
    loss = loss_row[0, 0]
    g_conv_w = lax.dynamic_slice(g_conv_full, (0, chip * conv_cols), (CONV_WIDTH, conv_cols))[None]

    grads.update({"norm_mix_pre": g_g1, "conv_w": g_conv_w, "conv_b": g_conv_b, "dt_bias": g_bias,
                  "a_log": g_alog, "d_skip": g_dskip, "ssm_norm_w": g_nw, "norm_mix_post": g_g2,
                  "norm_mlp_pre": g_g3, "norm_mlp_post": g_g4})
    order = list(weights)
    small_names = [n for n in order if n not in ("w_in", "w_out", "w_up", "w_down")]
    small_w_shapes = [weights[n][0].shape for n in small_names]
    packed = [_pack_rows([weights[n][k] for n in small_names]) for k in range(3)]
    packed_g = _pack_rows([grads[n].reshape(weights[n][0].shape) for n in small_names])
    sd, sm, sv = _adamw(packed[0], packed_g, packed[1], packed[2], "adamw_small")
    for k, n in enumerate(small_names):
        grads[n] = grads[n].reshape(weights[n][0].shape)
    for res, pk in ((delta, sd), (new_m, sm), (new_v, sv)):
        for n, val in zip(small_names, _unpack_rows(pk, small_w_shapes)):
            res[n] = val
    adamw_big("w_in", rs_in.share())

    return (loss, grad_x[None], *[grads[n] for n in order], *[delta[n] for n in order],
            *[new_m[n] for n in order], *[new_v[n] for n in order])
```

```python
import functools
import math

import numpy as np
import jax
import jax.numpy as jnp
from jax import lax
from jax.experimental import pallas as pl
from jax.experimental.pallas import tpu as pltpu
from jax.experimental.pallas import tpu_sc as plsc

F32 = jnp.float32
BF16 = jnp.bfloat16

D_MODEL = 2048
SSM_HEAD_DIM = 64
SSM_GROUPS = 8
HEADS_PER_GROUP = 4
SSM_HEADS = SSM_GROUPS * HEADS_PER_GROUP
D_SSM = SSM_HEADS * SSM_HEAD_DIM
D_STATE = 128
CONV_WIDTH = 4
SSD_CHUNK = 128
D_XBC = D_SSM + 2 * SSM_GROUPS * D_STATE
GROUP_X = HEADS_PER_GROUP * SSM_HEAD_DIM
GROUP_COLS = GROUP_X + 2 * D_STATE
ATT_HEAD_DIM = 128
ATT_HEADS = 16
D_ATT = ATT_HEADS * ATT_HEAD_DIM
DILATIONS = (1, 4, 16)
ATT_BLOCK = 128
D_MIX = D_SSM + D_ATT
D_IN_PROJ = D_SSM + D_XBC + SSM_HEADS + 3 * D_ATT
D_FF = 4 * D_MODEL
EPS = 1e-6
N_CHIPS = 4
W_IN_SHARD = D_IN_PROJ // N_CHIPS

ADAM_LR = 0.001
ADAM_B1 = 0.9
ADAM_B2 = 0.999
ADAM_EPS = 1e-08
ADAM_WD = 0.01
ADAM_STEP = 10

LANES = 128
VMEM_LIMIT = 48 * 1024 * 1024
MESH = pl.DeviceIdType.MESH

_NN = (((1,), (0,)), ((), ()))
_NT = (((1,), (1,)), ((), ()))
_TN = (((0,), (0,)), ((), ()))


def _dot(a, b, dims=_NN):
    return lax.dot_general(a, b, dims, preferred_element_type=F32)


def _cparams(*sem):
    return pltpu.CompilerParams(dimension_semantics=sem, vmem_limit_bytes=VMEM_LIMIT)


TK = 2048
TK_MULTI = 1024


def _matmul(pairs, mode, out_dtypes, *, name, tm=1024, tn=1024, epilogue=None, extras=(), deps=(), out_quarters=False):
    a0, b0, _ = pairs[0]
    m_dim = a0.shape[-1] if mode == "tn" else a0.shape[-2]
    if b0.ndim == 3:
        n_dim = b0.shape[1] if mode == "nt" else b0.shape[0] * b0.shape[2]
    else:
        n_dim = b0.shape[0] if mode == "nt" else b0.shape[1]
    tm, tn = min(tm, m_dim), min(tn, n_dim)
    nks, offs = [], []
    for a, _, tk in pairs:
        k_part = a.shape[0] if mode == "tn" else a.shape[-1]
        k_dim = k_part * (a.shape[0] if a.ndim == 3 else 1)
        assert k_part % tk == 0, (name, k_part, tk)
        offs.append(sum(nks))
        nks.append(k_dim // tk)
    nk_total = sum(nks)
    assert m_dim % tm == 0 and n_dim % tn == 0, (name, m_dim, n_dim)
    dims = {"nn": _NN, "nt": _NT, "tn": _TN}[mode]
    n_pairs, n_extra, n_out = len(pairs), len(extras), len(out_dtypes)

    in_specs, operands = [], []
    for (a, b, tk), off, nk in zip(pairs, offs, nks):
        def kidx(k, off=off, nk=nk):
            return k if n_pairs == 1 else jnp.clip(k - off, 0, nk - 1)
        if mode == "tn":
            assert a.ndim == 2
            in_specs.append(pl.BlockSpec((tk, tm), lambda m, n, k, f=kidx: (f(k), m)))
        elif a.ndim == 3:
            per = a.shape[2] // tk
            in_specs.append(pl.BlockSpec((None, tm, tk), lambda m, n, k, f=kidx, per=per: (f(k) // per, m, f(k) % per)))
        else:
            in_specs.append(pl.BlockSpec((tm, tk), lambda m, n, k, f=kidx: (m, f(k))))
        if b.ndim == 3 and mode == "nt":
            per = b.shape[2] // tk
            in_specs.append(pl.BlockSpec((None, tn, tk), lambda m, n, k, f=kidx, per=per: (f(k) // per, n, f(k) % per)))
        elif b.ndim == 3:
            per = b.shape[2] // tn
            in_specs.append(pl.BlockSpec((None, tk, tn), lambda m, n, k, f=kidx, per=per: (n // per, f(k), n % per)))
        elif mode == "nt":
            in_specs.append(pl.BlockSpec((tn, tk), lambda m, n, k, f=kidx: (n, f(k))))
        else:
            in_specs.append(pl.BlockSpec((tk, tn), lambda m, n, k, f=kidx: (f(k), n)))
        operands += [a, b]
    for e in extras:
        in_specs.append(pl.BlockSpec((tm, tn), lambda m, n, k: (m, n)))
        operands.append(e)
    in_specs += [pl.BlockSpec(memory_space=pl.ANY)] * len(deps)
    operands += list(deps)
    first_out = 2 * n_pairs + n_extra + len(deps)
    if out_quarters:
        out_per_q = n_dim // N_CHIPS // tn
        out_dims = (N_CHIPS, m_dim, n_dim // N_CHIPS)
        out_spec = pl.BlockSpec((None, tm, tn), lambda m, n, k: (n // out_per_q, m, n % out_per_q))
    else:
        out_dims = (m_dim, n_dim)
        out_spec = pl.BlockSpec((tm, tn), lambda m, n, k: (m, n))

    def body(*refs):
        ab = refs[:2 * n_pairs]
        e_refs = refs[2 * n_pairs:2 * n_pairs + n_extra]
        o_refs = refs[first_out:first_out + n_out]

        def finish(total):
            vals = (total,) if epilogue is None else epilogue(total, *[e[...] for e in e_refs])
            for o_ref, v in zip(o_refs, vals):
                o_ref[...] = v.astype(o_ref.dtype)

        if nk_total == 1:
            finish(_dot(ab[0][...], ab[1][...], dims))
            return
        acc = refs[-1]
        k = pl.program_id(2)

        @pl.when(k == 0)
        def _():
            acc[...] = jnp.zeros_like(acc)

        for i in range(n_pairs):
            def accumulate(i=i):
                acc[...] += _dot(ab[2 * i][...], ab[2 * i + 1][...], dims)
            if n_pairs == 1:
                accumulate()
            else:
                pl.when((k >= offs[i]) & (k < offs[i] + nks[i]))(accumulate)

        @pl.when(k == nk_total - 1)
        def _():
            finish(acc[...])

    outs = pl.pallas_call(
        body,
        grid=(m_dim // tm, n_dim // tn, nk_total),
        in_specs=in_specs,
        out_specs=[out_spec for _ in out_dtypes],
        out_shape=[jax.ShapeDtypeStruct(out_dims, dt) for dt in out_dtypes],
        scratch_shapes=[pltpu.VMEM((tm, tn), F32)] if nk_total > 1 else [],
        compiler_params=_cparams("parallel", "parallel", "arbitrary"),
        name=name,
    )(*operands)
    return outs[0] if n_out == 1 else outs


def _rowcall(fn, rows, vecs, row_outs, acc_widths, *, name, tr=256, row_cols=None, deps=()):
    s_dim = rows[0].shape[0]
    assert s_dim % tr == 0
    row_cols = row_cols or [None] * len(rows)
    n_r, n_v, n_ro, n_acc = len(rows), len(vecs), len(row_outs), len(acc_widths)
    in_specs = []
    for r, rc in zip(rows, row_cols):
        if rc is None:
            in_specs.append(pl.BlockSpec((tr, r.shape[1]), lambda i: (i, 0)))
        else:
            in_specs.append(pl.BlockSpec((tr, rc[0]), lambda i, c=rc[1]: (i, c)))
    for v in vecs:
        in_specs.append(pl.BlockSpec(v.shape, lambda i, nd=v.ndim: (0,) * nd))
    in_specs += [pl.BlockSpec(memory_space=pl.ANY)] * len(deps)
    n_d = len(deps)

    def body(*refs):
        ins = [r[...] for r in refs[:n_r + n_v]]
        ro = refs[n_r + n_v + n_d:n_r + n_v + n_d + n_ro]
        ao = refs[n_r + n_v + n_d + n_ro:]
        outs = fn(*ins)
        for ref, v in zip(ro, outs[:n_ro]):
            ref[...] = v.astype(ref.dtype)
        if n_acc:
            @pl.when(pl.program_id(0) == 0)
            def _():
                for ref in ao:
                    ref[...] = jnp.zeros_like(ref)
            for ref, v in zip(ao, outs[n_ro:]):
                ref[...] += v

    outs = pl.pallas_call(
        body,
        grid=(s_dim // tr,),
        in_specs=in_specs,
        out_specs=[pl.BlockSpec((tr, w), lambda i: (i, 0)) for w, _ in row_outs]
        + [pl.BlockSpec((1, w), lambda i: (0, 0)) for w in acc_widths],
        out_shape=[jax.ShapeDtypeStruct((s_dim, w), dt) for w, dt in row_outs]
        + [jax.ShapeDtypeStruct((1, w), F32) for w in acc_widths],
        compiler_params=_cparams("arbitrary"),
        name=name,
    )(*rows, *vecs, *deps)
    return outs


def _nrm(x, g):
    r = lax.rsqrt(jnp.mean(x * x, axis=-1, keepdims=True) + EPS)
    n = x * r
    return n * g, n, r


def _nrm_bwd(dy, n, r, g):
    dn = dy * g
    dx = r * (dn - n * jnp.mean(dn * n, axis=-1, keepdims=True))
    return dx, jnp.sum(dy * n, axis=0, keepdims=True)


def _sigmoid(x):
    return 1.0 / (1.0 + jnp.exp(-x))


def _softplus(x):
    return jnp.maximum(x, 0.0) + jnp.log(1.0 + jnp.exp(-jnp.abs(x)))


def _pre_norm(x, g1):
    def fn(xb, g):
        return (_nrm(xb, g)[0],)
    return _rowcall(fn, [x], [g1], [(D_MODEL, BF16)], [], name="pre_norm")[0]


def _post_pre_norm(x, mix, g2, g3):
    def fn(xb, mb, g2b, g3b):
        h1 = xb + _nrm(mb, g2b)[0]
        return h1, _nrm(h1, g3b)[0]
    return _rowcall(fn, [x, mix], [g2, g3], [(D_MODEL, F32), (D_MODEL, BF16)], [], name="post_pre_norm")


def _tail(ff, h1, target, g4):
    def fn(ffb, h1b, tb, g):
        y, n, r = _nrm(ffb, g)
        e = h1b + y - tb
        loss = 0.5 * jnp.sum(jnp.sum(e * e, axis=-1, keepdims=True) * (1.0 / D_MODEL), axis=0, keepdims=True)
        dh2 = e * (1.0 / D_MODEL)
        dff, dg = _nrm_bwd(dh2, n, r, g)
        return dh2, dff, dg, jnp.broadcast_to(loss, (1, LANES))
    return _rowcall(fn, [ff, h1, target], [g4], [(D_MODEL, F32), (D_MODEL, BF16)], [D_MODEL, LANES], name="tail")


def _mid_bwd(du2, h1, dh2, mix, g2, g3, deps=()):
    def fn(du2b, h1b, dh2b, mb, g2b, g3b):
        _, n3, r3 = _nrm(h1b, g3b)
        d3, dg3 = _nrm_bwd(du2b, n3, r3, g3b)
        dh1 = dh2b + d3
        _, n2, r2 = _nrm(mb, g2b)
        dmix, dg2 = _nrm_bwd(dh1, n2, r2, g2b)
        return dh1, dmix, dg3, dg2
    return _rowcall(fn, [du2, h1, dh2, mix], [g2, g3], [(D_MODEL, F32), (D_MODEL, BF16)], [D_MODEL, D_MODEL],
                    name="mid_bwd", deps=deps)


def _first_bwd(du, x, dh1, g1):
    def fn(dub, xb, dh1b, g):
        _, n, r = _nrm(xb, g)
        dx, dg = _nrm_bwd(dub, n, r, g)
        return dh1b + dx, dg
    return _rowcall(fn, [du, x, dh1], [g1], [(D_MODEL, F32)], [D_MODEL], name="first_bwd")


CONV_TILE = 256
CONV_ROWS = 256
PAD = 8


def _conv_taps(w):
    return [w[k:k + 1, :] for k in range(CONV_WIDTH)], w[CONV_WIDTH:CONV_WIDTH + 1, :]


def _conv_fwd(xbc, w8):
    s_dim, c_dim = xbc.shape
    n_steps = s_dim // CONV_ROWS

    def body(x_ref, w_ref, o_ref, xp):
        xp[0:PAD, :] = jnp.zeros((PAD, CONV_TILE), F32)
        xp[PAD:PAD + s_dim, :] = x_ref[...]
        taps, bias = _conv_taps(w_ref[...])

        def step(c, carry):
            base = pl.multiple_of(c * CONV_ROWS, CONV_ROWS)
            win = xp[pl.ds(base, CONV_ROWS + PAD), :]
            pre = bias + taps[3] * win[PAD:, :]
            for j in range(1, CONV_WIDTH):
                pre = pre + taps[3 - j] * pltpu.roll(win, j, axis=0)[PAD:, :]
            o_ref[pl.ds(base, CONV_ROWS), :] = pre * _sigmoid(pre)
            return carry

        lax.fori_loop(0, n_steps, step, 0)

    return pl.pallas_call(
        body,
        grid=(c_dim // CONV_TILE,),
        in_specs=[pl.BlockSpec((s_dim, CONV_TILE), lambda j: (0, j)), pl.BlockSpec((8, CONV_TILE), lambda j: (0, j))],
        out_specs=pl.BlockSpec((s_dim, CONV_TILE), lambda j: (0, j)),
        out_shape=jax.ShapeDtypeStruct((s_dim, c_dim), F32),
        scratch_shapes=[pltpu.VMEM((s_dim + 2 * PAD, CONV_TILE), F32)],
        compiler_params=_cparams("parallel"),
        name="conv_fwd",
    )(xbc, w8)


def _conv_bwd(xbc, w8, dxc):
    s_dim, c_dim = xbc.shape
    n_steps = s_dim // CONV_ROWS

    def body(x_ref, w_ref, d_ref, dx_ref, dw_ref, xp, dp):
        xp[0:PAD, :] = jnp.zeros((PAD, CONV_TILE), F32)
        xp[PAD:PAD + s_dim, :] = x_ref[...]
        dp[PAD + s_dim:, :] = jnp.zeros((PAD, CONV_TILE), F32)
        taps, bias = _conv_taps(w_ref[...])

        def step1(c, sums):
            base = pl.multiple_of(c * CONV_ROWS, CONV_ROWS)
            win = xp[pl.ds(base, CONV_ROWS + PAD), :]
            shifted = [win[PAD:, :]] + [pltpu.roll(win, j, axis=0)[PAD:, :] for j in range(1, CONV_WIDTH)]
            pre = bias
            for j in range(CONV_WIDTH):
                pre = pre + taps[3 - j] * shifted[j]
            sg = _sigmoid(pre)
            dpre = d_ref[pl.ds(base, CONV_ROWS), :] * (sg * (1.0 + pre * (1.0 - sg)))
            dp[pl.ds(base + PAD, CONV_ROWS), :] = dpre
            new = [sums[k] + jnp.sum(dpre * shifted[3 - k], axis=0, keepdims=True) for k in range(CONV_WIDTH)]
            new.append(sums[CONV_WIDTH] + jnp.sum(dpre, axis=0, keepdims=True))
            return tuple(new)

        zero = jnp.zeros((1, CONV_TILE), F32)
        sums = lax.fori_loop(0, n_steps, step1, (zero,) * (CONV_WIDTH + 1))
        dw_ref[...] = jnp.zeros((8, CONV_TILE), F32)
        for k in range(CONV_WIDTH + 1):
            dw_ref[k:k + 1, :] = sums[k]

        def step2(c, carry):
            base = pl.multiple_of(c * CONV_ROWS, CONV_ROWS)
            win = dp[pl.ds(base + PAD, CONV_ROWS + PAD), :]
            dx = taps[3] * win[:CONV_ROWS, :]
            for j in range(1, CONV_WIDTH):
                dx = dx + taps[3 - j] * pltpu.roll(win, CONV_ROWS + PAD - j, axis=0)[:CONV_ROWS, :]
            dx_ref[pl.ds(base, CONV_ROWS), :] = dx.astype(BF16)
            return carry

        lax.fori_loop(0, n_steps, step2, 0)

    col = lambda j: (0, j)
    return pl.pallas_call(
        body,
        grid=(c_dim // CONV_TILE,),
        in_specs=[pl.BlockSpec((s_dim, CONV_TILE), col), pl.BlockSpec((8, CONV_TILE), col),
                  pl.BlockSpec((s_dim, CONV_TILE), col)],
        out_specs=[pl.BlockSpec((s_dim, CONV_TILE), col), pl.BlockSpec((8, CONV_TILE), col)],
        out_shape=[jax.ShapeDtypeStruct((s_dim, c_dim), BF16), jax.ShapeDtypeStruct((8, c_dim), F32)],
        scratch_shapes=[pltpu.VMEM((s_dim + 2 * PAD, CONV_TILE), F32), pltpu.VMEM((s_dim + 2 * PAD, CONV_TILE), F32)],
        compiler_params=_cparams("parallel"),
        name="conv_bwd",
    )(xbc, w8, dxc)


def _perm_cols(a):
    parts = []
    for g in range(SSM_GROUPS):
        parts += [a[..., g * GROUP_X:(g + 1) * GROUP_X],
                  a[..., D_SSM + g * D_STATE:D_SSM + (g + 1) * D_STATE],
                  a[..., D_SSM + SSM_GROUPS * D_STATE + g * D_STATE:D_SSM + SSM_GROUPS * D_STATE + (g + 1) * D_STATE]]
    return jnp.concatenate(parts, axis=-1)


def _unperm_cols(a):
    xs = [a[..., g * GROUP_COLS:g * GROUP_COLS + GROUP_X] for g in range(SSM_GROUPS)]
    bs = [a[..., g * GROUP_COLS + GROUP_X:g * GROUP_COLS + GROUP_X + D_STATE] for g in range(SSM_GROUPS)]
    cs = [a[..., g * GROUP_COLS + GROUP_X + D_STATE:(g + 1) * GROUP_COLS] for g in range(SSM_GROUPS)]
    return jnp.concatenate(xs + bs + cs, axis=-1)


def _dt_to_groups(dt):
    s_dim = dt.shape[0]
    t = dt[:, :SSM_HEADS].reshape(s_dim, SSM_GROUPS, HEADS_PER_GROUP).transpose(1, 0, 2)
    return jnp.pad(t, ((0, 0), (0, 0), (0, LANES - HEADS_PER_GROUP)))


def _dt_from_groups(dtg):
    s_dim = dtg.shape[1]
    return dtg[:, :, :HEADS_PER_GROUP].transpose(1, 0, 2).reshape(s_dim, SSM_HEADS)


def _pack_ssd_params(dt_bias, a_log, d_skip):
    rows = jnp.stack([p.reshape(SSM_GROUPS, HEADS_PER_GROUP) for p in (dt_bias, a_log, d_skip)], axis=1)
    return jnp.pad(rows, ((0, 0), (0, 8 - 3), (0, LANES - HEADS_PER_GROUP)))


def _unpack_ssd_params(par):
    return tuple(par[:, k, :HEADS_PER_GROUP].reshape(SSM_HEADS) for k in range(3))


Q = SSD_CHUNK


def _split3(v):
    hi = v.astype(BF16)
    r1 = v - hi.astype(F32)
    mid = r1.astype(BF16)
    lo = (r1 - mid.astype(F32)).astype(BF16)
    return hi, mid, lo


def _dot_l01(t01, v):
    return sum(_dot(t01, p) for p in _split3(v))


def _dot_r01(v, e01):
    return sum(_dot(p, e01) for p in _split3(v))


def _ssd_consts():
    row = lax.broadcasted_iota(jnp.int32, (Q, Q), 0)
    col = lax.broadcasted_iota(jnp.int32, (Q, Q), 1)
    causal = row >= col
    tril = causal.astype(BF16)
    triu = (col >= row).astype(BF16)
    er = lax.broadcasted_iota(jnp.int32, (LANES, GROUP_X), 0)
    ec = lax.broadcasted_iota(jnp.int32, (LANES, GROUP_X), 1) // SSM_HEAD_DIM
    expand = (er == ec).astype(BF16)
    rr = lax.broadcasted_iota(jnp.int32, (GROUP_X, LANES), 0) // SSM_HEAD_DIM
    rc = lax.broadcasted_iota(jnp.int32, (GROUP_X, LANES), 1)
    reduce = (rr == rc).astype(BF16)
    lane_head = lax.broadcasted_iota(jnp.int32, (Q, GROUP_X), 1) // SSM_HEAD_DIM
    return causal, tril, triu, expand, reduce, lane_head


def _ssd_common(xc_ref, dt_ref, par_ref, consts):
    causal, tril, _, expand, _, _ = consts
    par = par_ref[...]
    bias, alog, dsk = par[0:1, :], par[1:2, :], par[2:3, :]
    a_neg = -jnp.exp(alog)
    dtr = dt_ref[...] + bias
    dt = _softplus(dtr)
    s = _dot_l01(tril, dt * a_neg)
    dt_x = _dot_r01(dt, expand)
    s_x = _dot_r01(s, expand)
    dsk_x = _dot_r01(jnp.broadcast_to(dsk, (8, LANES)), expand)[0:1, :]
    blk = xc_ref[...]
    x = blk[:, :GROUP_X]
    bm = blk[:, GROUP_X:GROUP_X + D_STATE].astype(BF16)
    cm = blk[:, GROUP_X + D_STATE:].astype(BF16)
    xdt = x * dt_x
    g = _dot(cm, bm, _NT)
    return dict(a_neg=a_neg, dtr=dtr, dt=dt, s=s, s_t=s.T, dt_x=dt_x, s_x=s_x, dsk_x=dsk_x, x=x, bm=bm, cm=cm,
                xdt=xdt, g=g)


def _decay(v, r, causal):
    diff = v["s"][:, r:r + 1] - v["s_t"][r:r + 1, :]
    return jnp.exp(jnp.where(causal, diff, -jnp.inf))


def _ssd_specs(n_chunks, rev):
    cidx = (lambda c: n_chunks - 1 - c) if rev else (lambda c: c)
    xc = pl.BlockSpec((Q, GROUP_COLS), lambda g, c: (cidx(c), g))
    gx = pl.BlockSpec((Q, GROUP_X), lambda g, c: (cidx(c), g))
    dt = pl.BlockSpec((None, Q, LANES), lambda g, c: (g, cidx(c), 0))
    par = pl.BlockSpec((None, 8, LANES), lambda g, c: (g, 0, 0))
    nw = pl.BlockSpec((1, GROUP_X), lambda g, c: (0, g))
    hs = pl.BlockSpec((None, None, D_STATE, GROUP_X), lambda g, c: (cidx(c), g, 0, 0))
    return xc, gx, dt, par, nw, hs


def _ssd_fwd(xc, z, dtg, par, nw):
    s_dim = xc.shape[0]
    n_chunks = s_dim // Q
    xc_s, gx_s, dt_s, par_s, nw_s, hs_s = _ssd_specs(n_chunks, False)

    def body(xc_ref, z_ref, dt_ref, par_ref, nw_ref, y_ref, ys_ref, hs_ref, ht):
        @pl.when(pl.program_id(1) == 0)
        def _():
            ht[...] = jnp.zeros_like(ht)

        consts = _ssd_consts()
        causal, lane_head = consts[0], consts[5]
        v = _ssd_common(xc_ref, dt_ref, par_ref, consts)
        xdt_b = v["xdt"].astype(BF16)
        yd = jnp.zeros((Q, GROUP_X), F32)
        for r in range(HEADS_PER_GROUP):
            m = (v["g"] * _decay(v, r, causal)).astype(BF16)
            yd = yd + _dot(m, jnp.where(lane_head == r, xdt_b, jnp.zeros_like(xdt_b)))
        h = ht[...]
        hs_ref[...] = h
        yo = jnp.exp(v["s_x"]) * _dot(v["cm"], h.astype(BF16))
        y = yd + yo + v["dsk_x"] * v["x"]
        s_last = v["s_x"][Q - 1:Q, :]
        snew = _dot(v["bm"], (v["xdt"] * jnp.exp(s_last - v["s_x"])).astype(BF16), _TN)
        ht[...] = jnp.exp(s_last) * h + snew
        zz = z_ref[...]
        yg = y * (zz * _sigmoid(zz))
        y_ref[...] = y
        ys_ref[...] = _nrm(yg, nw_ref[...])[0].astype(BF16)

    return pl.pallas_call(
        body,
        grid=(SSM_GROUPS, n_chunks),
        in_specs=[xc_s, gx_s, dt_s, par_s, nw_s],
        out_specs=[gx_s, gx_s, hs_s],
        out_shape=[jax.ShapeDtypeStruct((s_dim, D_SSM), F32), jax.ShapeDtypeStruct((s_dim, D_SSM), BF16),
                   jax.ShapeDtypeStruct((n_chunks, SSM_GROUPS, D_STATE, GROUP_X), F32)],
        scratch_shapes=[pltpu.VMEM((D_STATE, GROUP_X), F32)],
        compiler_params=_cparams("parallel", "arbitrary"),
        name="ssd_fwd",
    )(xc, z, dtg, par, nw)


def _ssd_bwd(xc, z, dtg, par, nw, y, hs, dymix):
    s_dim = xc.shape[0]
    n_chunks = s_dim // Q
    xc_s, gx_s, dt_s, par_s, nw_s, hs_s = _ssd_specs(n_chunks, True)

    def body(xc_ref, z_ref, dt_ref, par_ref, nw_ref, y_ref, hs_ref, dys_ref,
             dxc_ref, dz_ref, ddt_ref, dpar_ref, dnw_ref, dht):
        @pl.when(pl.program_id(1) == 0)
        def _():
            dht[...] = jnp.zeros_like(dht)
            dpar_ref[...] = jnp.zeros_like(dpar_ref)
            dnw_ref[...] = jnp.zeros_like(dnw_ref)

        consts = _ssd_consts()
        causal, _, triu, _, reduce, lane_head = consts
        v = _ssd_common(xc_ref, dt_ref, par_ref, consts)
        x, bm, cm, xdt, s_x = v["x"], v["bm"], v["cm"], v["xdt"], v["s_x"]
        h = hs_ref[...]
        hb = h.astype(BF16)
        es_x = jnp.exp(s_x)
        yo = es_x * _dot(cm, hb)
        s_last = s_x[Q - 1:Q, :]
        e_x = jnp.exp(s_last - s_x)
        es_last = jnp.exp(s_last)

        yv, zz, nw_v = y_ref[...], z_ref[...], nw_ref[...]
        sg = _sigmoid(zz)
        gz = zz * sg
        _, n, rstd = _nrm(yv * gz, nw_v)
        dout = dys_ref[...]
        dyg, dnw = _nrm_bwd(dout, n, rstd, nw_v)
        dnw_ref[...] += dnw
        dy = dyg * gz
        dz_ref[...] = (dyg * yv * (sg * (1.0 + zz * (1.0 - sg)))).astype(BF16)

        dyb = dy.astype(BF16)
        xdt_b = xdt.astype(BF16)
        dhp = dht[...]
        dhpb = dhp.astype(BF16)
        lane = lax.broadcasted_iota(jnp.int32, (Q, LANES), 1)
        sub = lax.broadcasted_iota(jnp.int32, (LANES, Q), 0)
        dxdt = jnp.zeros((Q, GROUP_X), F32)
        dg = jnp.zeros((Q, Q), F32)
        ds = jnp.zeros((Q, LANES), F32)
        ds_t = jnp.zeros((LANES, Q), F32)
        for r in range(HEADS_PER_GROUP):
            dec = _decay(v, r, causal)
            mf = v["g"] * dec
            dyr = jnp.where(lane_head == r, dyb, jnp.zeros_like(dyb))
            dm = _dot(dyr, xdt_b, _NT)
            dxdt = dxdt + _dot(mf.astype(BF16), dyr, _TN)
            dg = dg + dm * dec
            dd = dm * mf
            ds = ds + jnp.where(lane == r, jnp.sum(dd, axis=1, keepdims=True), 0.0)
            ds_t = ds_t + jnp.where(sub == r, jnp.sum(dd, axis=0, keepdims=True), 0.0)
        ds = ds - ds_t.T
        dgb = dg.astype(BF16)
        dwb = (es_x * dy).astype(BF16)
        dcm = _dot(dgb, bm) + _dot(dwb, hb, _NT)
        dh_prev = _dot(cm, dwb, _TN)
        zst = _dot(bm, dhpb)
        xe = xdt * e_x
        dxdt = dxdt + e_x * zst
        dee = xe * zst
        dbm = _dot(dgb, cm, _TN) + _dot(xe.astype(BF16), dhpb, _NT)
        v_last = jnp.sum(dee, axis=0, keepdims=True) + es_last * jnp.sum(dhp * h, axis=0, keepdims=True)
        row_x = lax.broadcasted_iota(jnp.int32, (Q, GROUP_X), 0)
        tx = dy * yo - dee + jnp.where(row_x == Q - 1, v_last, 0.0)
        ds = ds + _dot_r01(tx, reduce)
        ddta = _dot_l01(triu, ds)
        ddt = ddta * v["a_neg"] + _dot_r01(dxdt * x, reduce)
        dalog = jnp.sum(ddta * v["dt"], axis=0, keepdims=True) * v["a_neg"]
        draw = jnp.where(lane < HEADS_PER_GROUP, ddt * _sigmoid(v["dtr"]), 0.0)
        dbias = jnp.sum(draw, axis=0, keepdims=True)
        ddsk = _dot_r01(jnp.broadcast_to(jnp.sum(dy * x, axis=0, keepdims=True), (8, GROUP_X)), reduce)[0:1, :]
        dht[...] = es_last * dhp + dh_prev
        dxc_ref[:, :GROUP_X] = dxdt * v["dt_x"] + v["dsk_x"] * dy
        dxc_ref[:, GROUP_X:GROUP_X + D_STATE] = dbm
        dxc_ref[:, GROUP_X + D_STATE:] = dcm
        ddt_ref[...] = draw
        dpar_ref[0:1, :] += dbias
        dpar_ref[1:2, :] += dalog
        dpar_ref[2:3, :] += ddsk

    return pl.pallas_call(
        body,
        grid=(SSM_GROUPS, n_chunks),
        in_specs=[xc_s, gx_s, dt_s, par_s, nw_s, gx_s, hs_s, gx_s],
        out_specs=[xc_s, gx_s, dt_s, par_s, nw_s],
        out_shape=[jax.ShapeDtypeStruct((s_dim, SSM_GROUPS * GROUP_COLS), F32),
                   jax.ShapeDtypeStruct((s_dim, D_SSM), BF16),
                   jax.ShapeDtypeStruct((SSM_GROUPS, s_dim, LANES), F32),
                   jax.ShapeDtypeStruct((SSM_GROUPS, 8, LANES), F32),
                   jax.ShapeDtypeStruct((1, D_SSM), F32)],
        scratch_shapes=[pltpu.VMEM((D_STATE, GROUP_X), F32)],
        compiler_params=_cparams("parallel", "arbitrary"),
        name="ssd_bwd",
    )(xc, z, dtg, par, nw, y, hs, dymix)


ATT_SCALE = ATT_HEAD_DIM ** -0.5
NEG_INF = -jnp.inf


def _head(h):
    return slice(h * ATT_HEAD_DIM, (h + 1) * ATT_HEAD_DIM)


def _band_masks():
    qi = lax.broadcasted_iota(jnp.int32, (ATT_BLOCK, ATT_BLOCK), 0)
    kj = lax.broadcasted_iota(jnp.int32, (ATT_BLOCK, ATT_BLOCK), 1)
    return kj <= qi, kj >= qi


def _attn_fwd(qkv_v, d):
    rows = qkv_v.shape[0]
    nb = rows // ATT_BLOCK
    blk = (ATT_BLOCK, D_ATT)
    prev = lambda i: jnp.maximum(i - 1, 0)

    def body(q_ref, kc_ref, kp_ref, vc_ref, vp_ref, o_ref, lse_ref):
        own, before = _band_masks()
        before = before & (pl.program_id(1) > 0)
        lane = lax.broadcasted_iota(jnp.int32, (ATT_BLOCK, LANES), 1)
        lse_all = jnp.zeros((ATT_BLOCK, LANES), F32)
        for h in range(ATT_HEADS):
            q = q_ref[:, _head(h)]
            sc = jnp.where(own, _dot(q, kc_ref[:, _head(h)], _NT) * ATT_SCALE, NEG_INF)
            sp = jnp.where(before, _dot(q, kp_ref[:, _head(h)], _NT) * ATT_SCALE, NEG_INF)
            m = jnp.maximum(jnp.max(sc, axis=1, keepdims=True), jnp.max(sp, axis=1, keepdims=True))
            pc, pp = jnp.exp(sc - m), jnp.exp(sp - m)
            den = jnp.sum(pc, axis=1, keepdims=True) + jnp.sum(pp, axis=1, keepdims=True)
            o = _dot(pc.astype(BF16), vc_ref[:, _head(h)]) + _dot(pp.astype(BF16), vp_ref[:, _head(h)])
            o_ref[:, _head(h)] = o / den
            lse_all = jnp.where(lane == h, m + jnp.log(den), lse_all)
        lse_ref[...] = lse_all

    return pl.pallas_call(
        body,
        grid=(d, nb),
        in_specs=[pl.BlockSpec(blk, lambda r, i: (i, 3 * r)),
                  pl.BlockSpec(blk, lambda r, i: (i, 3 * r + 1)),
                  pl.BlockSpec(blk, lambda r, i: (prev(i), 3 * r + 1)),
                  pl.BlockSpec(blk, lambda r, i: (i, 3 * r + 2)),
                  pl.BlockSpec(blk, lambda r, i: (prev(i), 3 * r + 2))],
        out_specs=[pl.BlockSpec(blk, lambda r, i: (i, r)), pl.BlockSpec((ATT_BLOCK, LANES), lambda r, i: (i, r))],
        out_shape=[jax.ShapeDtypeStruct((rows, d * D_ATT), F32), jax.ShapeDtypeStruct((rows, d * LANES), F32)],
        compiler_params=_cparams("parallel", "arbitrary"),
        name=f"attn_fwd_d{d}",
    )(qkv_v, qkv_v, qkv_v, qkv_v, qkv_v)


def _attn_combine(os_, lses):
    def fn(o1, o2, o3, l1, l2, l3):
        m = jnp.maximum(jnp.maximum(l1, l2), l3)
        tot = m + jnp.log(jnp.exp(l1 - m) + jnp.exp(l2 - m) + jnp.exp(l3 - m))
        w1, w2, w3 = jnp.exp(l1 - tot), jnp.exp(l2 - tot), jnp.exp(l3 - tot)
        cols = []
        for h in range(ATT_HEADS):
            cols.append(w1[:, h:h + 1] * o1[:, _head(h)] + w2[:, h:h + 1] * o2[:, _head(h)]
                        + w3[:, h:h + 1] * o3[:, _head(h)])
        y = jnp.concatenate(cols, axis=1)
        return y, y, tot
    return _rowcall(fn, list(os_) + list(lses), [], [(D_ATT, BF16), (D_ATT, F32), (LANES, F32)], [],
                    name="attn_combine", tr=128)


def _attn_delta(dymix, y_att):
    def fn(dy, y):
        lane = lax.broadcasted_iota(jnp.int32, (dy.shape[0], LANES), 1)
        delta = jnp.zeros((dy.shape[0], LANES), F32)
        for h in range(ATT_HEADS):
            delta = jnp.where(lane == h, jnp.sum(dy[:, _head(h)] * y[:, _head(h)], axis=1, keepdims=True), delta)
        return dy, delta
    return _rowcall(fn, [dymix, y_att], [], [(D_ATT, BF16), (LANES, F32)], [], name="attn_delta",
                    row_cols=[(D_ATT, 1), None])


def _attn_bwd(qkv_v, dy_v, lse_v, delta_v, d):
    rows = qkv_v.shape[0]
    nb = rows // ATT_BLOCK
    blk = (ATT_BLOCK, D_ATT)
    sblk = (ATT_BLOCK, LANES)
    prev = lambda i: jnp.maximum(i - 1, 0)
    nxt = lambda i: jnp.minimum(i + 1, nb - 1)

    def body(qc_ref, qn_ref, kc_ref, kp_ref, vc_ref, vp_ref, dyc_ref, dyn_ref, lc_ref, ln_ref, dc_ref, dn_ref,
             dq_ref, dk_ref, dv_ref):
        i = pl.program_id(1)
        own, before = _band_masks()
        before_c = before & (i > 0)
        before_n = before & (i < nb - 1)
        lc, ln, dc, dn = lc_ref[...], ln_ref[...], dc_ref[...], dn_ref[...]
        for h in range(ATT_HEADS):
            hs = _head(h)
            q, qn, kc, kp, vc, vp = qc_ref[:, hs], qn_ref[:, hs], kc_ref[:, hs], kp_ref[:, hs], vc_ref[:, hs], vp_ref[:, hs]
            dy, dyn = dyc_ref[:, hs], dyn_ref[:, hs]
            lse, lse_n, dl, dl_n = lc[:, h:h + 1], ln[:, h:h + 1], dc[:, h:h + 1], dn[:, h:h + 1]
            pc = jnp.exp(jnp.where(own, _dot(q, kc, _NT) * ATT_SCALE - lse, NEG_INF))
            pp = jnp.exp(jnp.where(before_c, _dot(q, kp, _NT) * ATT_SCALE - lse, NEG_INF))
            pn = jnp.exp(jnp.where(before_n, _dot(qn, kc, _NT) * ATT_SCALE - lse_n, NEG_INF))
            dsc = (pc * (_dot(dy, vc, _NT) - dl)).astype(BF16)
            dsp = (pp * (_dot(dy, vp, _NT) - dl)).astype(BF16)
            dsn = (pn * (_dot(dyn, vc, _NT) - dl_n)).astype(BF16)
            dq_ref[:, hs] = (_dot(dsc, kc) + _dot(dsp, kp)) * ATT_SCALE
            dk_ref[:, hs] = (_dot(dsc, q, _TN) + _dot(dsn, qn, _TN)) * ATT_SCALE
            dv_ref[:, hs] = _dot(pc.astype(BF16), dy, _TN) + _dot(pn.astype(BF16), dyn, _TN)

    return pl.pallas_call(
        body,
        grid=(d, nb),
        in_specs=[pl.BlockSpec(blk, lambda r, i: (i, 3 * r)), pl.BlockSpec(blk, lambda r, i: (nxt(i), 3 * r)),
                  pl.BlockSpec(blk, lambda r, i: (i, 3 * r + 1)), pl.BlockSpec(blk, lambda r, i: (prev(i), 3 * r + 1)),
                  pl.BlockSpec(blk, lambda r, i: (i, 3 * r + 2)), pl.BlockSpec(blk, lambda r, i: (prev(i), 3 * r + 2)),
                  pl.BlockSpec(blk, lambda r, i: (i, r)), pl.BlockSpec(blk, lambda r, i: (nxt(i), r)),
                  pl.BlockSpec(sblk, lambda r, i: (i, r)), pl.BlockSpec(sblk, lambda r, i: (nxt(i), r)),
                  pl.BlockSpec(sblk, lambda r, i: (i, r)), pl.BlockSpec(sblk, lambda r, i: (nxt(i), r))],
        out_specs=[pl.BlockSpec(blk, lambda r, i: (i, r))] * 3,
        out_shape=[jax.ShapeDtypeStruct((rows, d * D_ATT), F32)] * 3,
        compiler_params=_cparams("parallel", "arbitrary"),
        name=f"attn_bwd_d{d}",
    )(qkv_v, qkv_v, qkv_v, qkv_v, qkv_v, qkv_v, dy_v, dy_v, lse_v, lse_v, delta_v, delta_v)


def _attn_sum(dqs, dks, dvs, deps=()):
    def fn(*parts):
        return (jnp.concatenate([parts[0] + parts[1] + parts[2], parts[3] + parts[4] + parts[5],
                                 parts[6] + parts[7] + parts[8]], axis=1),)
    return _rowcall(fn, list(dqs) + list(dks) + list(dvs), [], [(3 * D_ATT, BF16)], [], name="attn_sum", tr=128,
                    deps=deps)[0]


def _attention_fwd(qkv):
    s_dim = qkv.shape[0]
    os_, lses = [], []
    for d in DILATIONS:
        o, lse = _attn_fwd(qkv.reshape(s_dim // d, d * 3 * D_ATT), d)
        os_.append(o.reshape(s_dim, D_ATT))
        lses.append(lse.reshape(s_dim, LANES))
    return _attn_combine(os_, lses)


def _attention_bwd(qkv, dymix, y_att, lse, sum_deps=()):
    s_dim = qkv.shape[0]
    dy, delta = _attn_delta(dymix, y_att)
    dqs, dks, dvs = [], [], []
    for d in DILATIONS:
        dq, dk, dv = _attn_bwd(qkv.reshape(s_dim // d, d * 3 * D_ATT), dy.reshape(s_dim // d, d * D_ATT),
                               lse.reshape(s_dim // d, d * LANES), delta.reshape(s_dim // d, d * LANES), d)
        dqs.append(dq.reshape(s_dim, D_ATT))
        dks.append(dk.reshape(s_dim, D_ATT))
        dvs.append(dv.reshape(s_dim, D_ATT))
    return _attn_sum(dqs, dks, dvs, sum_deps)


WIN = ATT_BLOCK * DILATIONS[-1]
N_BLOCKS = WIN // ATT_BLOCK


def _rows(start, d):
    return pl.ds(start, ATT_BLOCK) if d == 1 else pl.ds(start, ATT_BLOCK, stride=d)


def _block_start(idx, d):
    return (idx // d) * (ATT_BLOCK * d) + idx % d


def _lane_bcast(col):
    return jnp.broadcast_to(col, (col.shape[0], LANES))


def _attn_fused_fwd(qkv):
    s_dim = qkv.shape[0]
    n_win = s_dim // WIN
    blk = (WIN, ATT_HEAD_DIM)
    prev = lambda w: jnp.maximum(w - 1, 0)

    def body(q_ref, kc_ref, kp_ref, vc_ref, vp_ref, y_ref, yf_ref, lse_ref, qf, kf, vf, acc, m_run, l_run):
        w, h = pl.program_id(0), pl.program_id(1)
        qf[...] = q_ref[...].astype(F32)
        kf[0:WIN, :] = kp_ref[...].astype(F32)
        kf[WIN:, :] = kc_ref[...].astype(F32)
        vf[0:WIN, :] = vp_ref[...].astype(F32)
        vf[WIN:, :] = vc_ref[...].astype(F32)
        own, before = _band_masks()

        for d in DILATIONS:
            def block(idx, carry, d=d):
                start = _block_start(idx, d)
                rows = _rows(start, d)
                q = qf[rows, :].astype(BF16)
                kc, vc = kf[_rows(WIN + start, d), :].astype(BF16), vf[_rows(WIN + start, d), :].astype(BF16)
                kp = kf[_rows(WIN + start - ATT_BLOCK * d, d), :].astype(BF16)
                vp = vf[_rows(WIN + start - ATT_BLOCK * d, d), :].astype(BF16)
                has_prev = (idx >= d) | (w > 0)
                sc = jnp.where(own, _dot(q, kc, _NT) * ATT_SCALE, NEG_INF)
                sp = jnp.where(before & has_prev, _dot(q, kp, _NT) * ATT_SCALE, NEG_INF)
                m_blk = jnp.maximum(jnp.max(sc, axis=1, keepdims=True), jnp.max(sp, axis=1, keepdims=True))
                if d == DILATIONS[0]:
                    m_new = m_blk
                else:
                    m_old = m_run[rows, :][:, 0:1]
                    m_new = jnp.maximum(m_old, m_blk)
                pc, pp = jnp.exp(sc - m_new), jnp.exp(sp - m_new)
                l_new = jnp.sum(pc, axis=1, keepdims=True) + jnp.sum(pp, axis=1, keepdims=True)
                o_new = _dot(pc.astype(BF16), vc) + _dot(pp.astype(BF16), vp)
                if d != DILATIONS[0]:
                    alpha = jnp.exp(m_old - m_new)
                    l_new = alpha * l_run[rows, :][:, 0:1] + l_new
                    o_new = alpha * acc[rows, :] + o_new
                m_run[rows, :] = _lane_bcast(m_new)
                l_run[rows, :] = _lane_bcast(l_new)
                acc[rows, :] = o_new
                return carry

            for idx in range(N_BLOCKS):
                block(idx, 0)

        l_all = l_run[...]
        y = acc[...] / l_all
        y_ref[...] = y.astype(BF16)
        yf_ref[...] = y
        @pl.when(h == 0)
        def _():
            lse_ref[...] = jnp.zeros_like(lse_ref)

        lane = lax.broadcasted_iota(jnp.int32, (WIN, LANES), 1)
        lse_ref[...] = jnp.where(lane == h, m_run[...] + jnp.log(l_all), lse_ref[...])

    win_scratch = lambda rows: pltpu.VMEM((rows, ATT_HEAD_DIM), F32)
    return pl.pallas_call(
        body,
        grid=(n_win, ATT_HEADS),
        in_specs=[pl.BlockSpec(blk, lambda w, h: (w, h)),
                  pl.BlockSpec(blk, lambda w, h: (w, ATT_HEADS + h)),
                  pl.BlockSpec(blk, lambda w, h: (prev(w), ATT_HEADS + h)),
                  pl.BlockSpec(blk, lambda w, h: (w, 2 * ATT_HEADS + h)),
                  pl.BlockSpec(blk, lambda w, h: (prev(w), 2 * ATT_HEADS + h))],
        out_specs=[pl.BlockSpec(blk, lambda w, h: (w, h)), pl.BlockSpec(blk, lambda w, h: (w, h)),
                   pl.BlockSpec((WIN, LANES), lambda w, h: (w, 0))],
        out_shape=[jax.ShapeDtypeStruct((s_dim, D_ATT), BF16), jax.ShapeDtypeStruct((s_dim, D_ATT), F32),
                   jax.ShapeDtypeStruct((s_dim, LANES), F32)],
        scratch_shapes=[win_scratch(WIN), win_scratch(2 * WIN), win_scratch(2 * WIN), win_scratch(WIN),
                        win_scratch(WIN), win_scratch(WIN)],
        compiler_params=_cparams("parallel", "arbitrary"),
        name="attn_fused_fwd",
    )(qkv, qkv, qkv, qkv, qkv)


def _attn_fused_bwd(qkv, dymix, y_att, lse, deps=()):
    s_dim = qkv.shape[0]
    n_win = s_dim // WIN
    blk = (WIN, ATT_HEAD_DIM)
    prev = lambda w: jnp.maximum(w - 1, 0)
    nxt = lambda w: jnp.minimum(w + 1, n_win - 1)
    n_dep = len(deps)

    def body(qc_ref, qn_ref, kc_ref, kp_ref, vc_ref, vp_ref, dyc_ref, dyn_ref, yc_ref, yn_ref, lc_ref, ln_ref, *rest):
        out_ref = rest[n_dep]
        qf, qnf, kf, vf, dq_acc, dk_acc, dv_acc, ls_c, dl_c, ls_n, dl_n = rest[n_dep + 1:]
        w, h = pl.program_id(0), pl.program_id(1)
        qf[...] = qc_ref[...].astype(F32)
        qnf[...] = qn_ref[...].astype(F32)
        kf[0:WIN, :] = kp_ref[...].astype(F32)
        kf[WIN:, :] = kc_ref[...].astype(F32)
        vf[0:WIN, :] = vp_ref[...].astype(F32)
        vf[WIN:, :] = vc_ref[...].astype(F32)
        lane = lax.broadcasted_iota(jnp.int32, (WIN, LANES), 1)
        pick = lambda ref: _lane_bcast(jnp.sum(jnp.where(lane == h, ref[...], 0.0), axis=1, keepdims=True))
        ls_c[...] = pick(lc_ref)
        ls_n[...] = pick(ln_ref)
        dl_c[...] = _lane_bcast(jnp.sum(dyc_ref[...] * yc_ref[...], axis=1, keepdims=True))
        dl_n[...] = _lane_bcast(jnp.sum(dyn_ref[...] * yn_ref[...], axis=1, keepdims=True))
        for ref in (dq_acc, dk_acc, dv_acc):
            ref[...] = jnp.zeros_like(ref)
        own, before = _band_masks()

        def probs(q, k, v, dy, lse_col, dl_col, mask):
            p = jnp.exp(jnp.where(mask, _dot(q, k, _NT) * ATT_SCALE - lse_col, NEG_INF))
            ds = p * (_dot(dy, v, _NT) - dl_col)
            return p.astype(BF16), ds.astype(BF16)

        for d in DILATIONS:
            def block(idx, carry, d=d):
                start = _block_start(idx, d)
                rows = _rows(start, d)
                q, dy = qf[rows, :].astype(BF16), dyc_ref[rows, :].astype(BF16)
                lse_col, dl_col = ls_c[rows, :][:, 0:1], dl_c[rows, :][:, 0:1]
                kc, vc = kf[_rows(WIN + start, d), :].astype(BF16), vf[_rows(WIN + start, d), :].astype(BF16)
                kp = kf[_rows(WIN + start - ATT_BLOCK * d, d), :].astype(BF16)
                vp = vf[_rows(WIN + start - ATT_BLOCK * d, d), :].astype(BF16)
                pc, dsc = probs(q, kc, vc, dy, lse_col, dl_col, own)
                pp, dsp = probs(q, kp, vp, dy, lse_col, dl_col, before & ((idx >= d) | (w > 0)))
                dq_acc[rows, :] += (_dot(dsc, kc) + _dot(dsp, kp)) * ATT_SCALE
                dk_acc[rows, :] += _dot(dsc, q, _TN) * ATT_SCALE
                dv_acc[rows, :] += _dot(pc, dy, _TN)

                if idx >= d:
                    prows = _rows(start - ATT_BLOCK * d, d)
                    dk_acc[prows, :] += _dot(dsp, q, _TN) * ATT_SCALE
                    dv_acc[prows, :] += _dot(pp, dy, _TN)
                return carry

            for idx in range(N_BLOCKS):
                block(idx, 0)

            def next_window(r, carry, d=d):
                krows = _rows(WIN - ATT_BLOCK * d + r, d)
                rows = _rows(r, d)
                q, dy = qnf[rows, :].astype(BF16), dyn_ref[rows, :].astype(BF16)
                k, v = kf[_rows(2 * WIN - ATT_BLOCK * d + r, d), :].astype(BF16), vf[_rows(2 * WIN - ATT_BLOCK * d + r, d), :].astype(BF16)
                pn, dsn = probs(q, k, v, dy, ls_n[rows, :][:, 0:1], dl_n[rows, :][:, 0:1], before & (w < n_win - 1))
                dk_acc[krows, :] += _dot(dsn, q, _TN) * ATT_SCALE
                dv_acc[krows, :] += _dot(pn, dy, _TN)
                return carry

            for r in range(d):
                next_window(r, 0)

        for part, acc_ref in enumerate((dq_acc, dk_acc, dv_acc)):
            out_ref[part] = acc_ref[...].astype(BF16)

    win_scratch = lambda rows: pltpu.VMEM((rows, ATT_HEAD_DIM), F32)
    cur = lambda c: pl.BlockSpec(blk, lambda w, h: (w, c + h))
    return pl.pallas_call(
        body,
        grid=(n_win, ATT_HEADS),
        in_specs=[cur(0), pl.BlockSpec(blk, lambda w, h: (nxt(w), h)),
                  cur(ATT_HEADS), pl.BlockSpec(blk, lambda w, h: (prev(w), ATT_HEADS + h)),
                  cur(2 * ATT_HEADS), pl.BlockSpec(blk, lambda w, h: (prev(w), 2 * ATT_HEADS + h)),
                  cur(ATT_HEADS), pl.BlockSpec(blk, lambda w, h: (nxt(w), ATT_HEADS + h)),
                  cur(0), pl.BlockSpec(blk, lambda w, h: (nxt(w), h)),
                  pl.BlockSpec((WIN, LANES), lambda w, h: (w, 0)), pl.BlockSpec((WIN, LANES), lambda w, h: (nxt(w), 0))]
        + [ANY] * n_dep,
        out_specs=pl.BlockSpec((3, WIN, ATT_HEAD_DIM), lambda w, h: (0, w, h)),
        out_shape=jax.ShapeDtypeStruct((3, s_dim, D_ATT), BF16),
        scratch_shapes=[win_scratch(WIN), win_scratch(WIN), win_scratch(2 * WIN), win_scratch(2 * WIN)]
        + [win_scratch(WIN)] * 7,
        compiler_params=_cparams("parallel", "arbitrary"),
        name="attn_fused_bwd",
    )(qkv, qkv, qkv, qkv, qkv, qkv, dymix, dymix, y_att, y_att, lse, lse, *deps)


def _adamw(w, g, m, v, name):
    def fn(wb, gb, mb, vb):
        m2 = ADAM_B1 * mb + (1.0 - ADAM_B1) * gb
        v2 = ADAM_B2 * vb + (1.0 - ADAM_B2) * (gb * gb)
        m_hat = m2 / (1.0 - ADAM_B1 ** ADAM_STEP)
        v_hat = v2 / (1.0 - ADAM_B2 ** ADAM_STEP)
        delta = -ADAM_LR * (m_hat / (jnp.sqrt(v_hat) + ADAM_EPS) + ADAM_WD * wb)
        return delta, m2, v2
    cols = w.shape[1]
    tr = 128 if w.shape[0] % 128 == 0 else w.shape[0]
    return _rowcall(fn, [w, g, m, v], [], [(cols, F32)] * 3, [], name=name, tr=tr)


ANY = pl.BlockSpec(memory_space=pl.ANY)


def _position():
    x, y, c = lax.axis_index("x"), lax.axis_index("y"), lax.axis_index("c")
    chips = [(1 - x, y), (x, 1 - y), (1 - x, 1 - y)]
    return x, y, c, chips


def _remote(src, dst, send_sem, recv_sem, device):
    return pltpu.make_async_remote_copy(src_ref=src, dst_ref=dst, send_sem=send_sem, recv_sem=recv_sem,
                                        device_id=device, device_id_type=MESH)


def _gather_shards(shards):
    n = len(shards)

    def body(*refs):
        ins, outs = refs[:n], refs[n:2 * n]
        send_sems, recv_sems = refs[2 * n:]
        x, y, c, chips = _position()
        sibling = (x, y, 1 - c)

        def half(a, j, cc):
            h = ins[a].shape[0] // 2
            return outs[a].at[j, pl.ds(cc * h, h), :]

        sent = []
        for a in range(n):
            h = ins[a].shape[0] // 2
            for j, chip in enumerate(chips):
                cp = _remote(ins[a].at[pl.ds(c * h, h), :], half(a, j, c), send_sems.at[6 * a + j],
                             recv_sems.at[6 * a + j], (chip[0], chip[1], c))
                cp.start()
                sent.append(cp)
        for a in range(n):
            for j in range(3):
                landed = half(a, j, c)
                _remote(landed, landed, send_sems.at[6 * a + j], recv_sems.at[6 * a + j], (x, y, c)).wait_recv()
                cp = _remote(landed, landed, send_sems.at[6 * a + 3 + j], recv_sems.at[6 * a + 3 + j], sibling)
                cp.start()
                sent.append(cp)
        for a in range(n):
            for j in range(3):
                handed = half(a, j, 1 - c)
                _remote(handed, handed, send_sems.at[6 * a + 3 + j], recv_sems.at[6 * a + 3 + j], (x, y, c)).wait_recv()
        for cp in sent:
            cp.wait_send()

    return pl.pallas_call(
        body,
        in_specs=[ANY] * n,
        out_specs=[ANY] * n,
        out_shape=[jax.ShapeDtypeStruct((3,) + s.shape, s.dtype) for s in shards],
        scratch_shapes=[pltpu.SemaphoreType.DMA((6 * n,)), pltpu.SemaphoreType.DMA((6 * n,))],
        name="gather_shards",
    )(*shards)


def _handshake(peers):
    barrier = pltpu.get_barrier_semaphore()
    for p in peers:
        pl.semaphore_signal(barrier, inc=1, device_id=p, device_id_type=MESH)
    pl.semaphore_wait(barrier, len(peers))


def _gather_shards_async(shards, collective_id, name):
    n = len(shards)
    srcs = [jax.new_ref(s, memory_space=pltpu.MemorySpace.HBM) for s in shards]
    dsts = [jax.empty_ref(jax.ShapeDtypeStruct((3,) + s.shape, s.dtype), memory_space=pltpu.MemorySpace.HBM)
            for s in shards]

    @pl.kernel(mesh=plsc.ScalarSubcoreMesh(axis_name="seq", num_cores=1), name=name,
               scratch_types=(pltpu.SemaphoreType.DMA((6 * n,)), pltpu.SemaphoreType.DMA((6 * n,))),
               compiler_params=pltpu.CompilerParams(collective_id=collective_id))
    def launch(send_sems, recv_sems):
        x, y, c, chips = _position()
        sibling = (x, y, 1 - c)
        _handshake([(chip[0], chip[1], c) for chip in chips] + [sibling])

        def half(a, j, cc):
            h = shards[a].shape[0] // 2
            return dsts[a].at[j, pl.ds(cc * h, h), :]

        sent = []
        for a in range(n):
            h = shards[a].shape[0] // 2
            for j, chip in enumerate(chips):
                cp = _remote(srcs[a].at[pl.ds(c * h, h), :], half(a, j, c), send_sems.at[6 * a + j],
                             recv_sems.at[6 * a + j], (chip[0], chip[1], c))
                cp.start()
                sent.append(cp)
        for a in range(n):
            for j in range(3):
                landed = half(a, j, c)
                _remote(landed, landed, send_sems.at[6 * a + j], recv_sems.at[6 * a + j], (x, y, c)).wait_recv()
                cp = _remote(landed, landed, send_sems.at[6 * a + 3 + j], recv_sems.at[6 * a + 3 + j], sibling)
                cp.start()
                sent.append(cp)
        for a in range(n):
            for j in range(3):
                handed = half(a, j, 1 - c)
                _remote(handed, handed, send_sems.at[6 * a + 3 + j], recv_sems.at[6 * a + 3 + j], (x, y, c)).wait_recv()
        for cp in sent:
            cp.wait_send()

    launch()
    return [d[...] for d in dsts]


IN_COLS = {"z": (0, D_SSM), "xbc": (D_SSM, D_SSM + D_XBC), "dt": (D_SSM + D_XBC, D_SSM + D_XBC + SSM_HEADS),
           "qkv": (D_SSM + D_XBC + SSM_HEADS, D_IN_PROJ)}


def _cols_from_quarters(quarters, lo, hi):
    parts = []
    for q in range(N_CHIPS):
        a, b = max(lo, q * W_IN_SHARD), min(hi, (q + 1) * W_IN_SHARD)
        if a < b:
            parts.append(quarters[q][:, a - q * W_IN_SHARD:b - q * W_IN_SHARD])
    return parts[0] if len(parts) == 1 else jnp.concatenate(parts, axis=1)


def _quarters_from_cols(pieces):
    quarters = []
    for q in range(N_CHIPS):
        parts = []
        for name, (lo, hi) in IN_COLS.items():
            a, b = max(lo, q * W_IN_SHARD), min(hi, (q + 1) * W_IN_SHARD)
            if a < b:
                parts.append(pieces[name][:, a - lo:b - lo])
        quarters.append(jnp.concatenate(parts, axis=1))
    return jnp.stack(quarters)


def _by_chip(own, others):
    me = 2 * lax.axis_index("x") + lax.axis_index("y")
    rel = jnp.stack([own, others[1], others[0], others[2]])
    return jnp.stack([lax.dynamic_index_in_dim(rel, q ^ me, 0, keepdims=False) for q in range(N_CHIPS)])


def _add_sibling(grad, got, c_arr, name, deps=()):
    nq, rows, cols = grad.shape
    h = rows // 2
    tr = 128
    nb = h // tr

    def body(c_ref, a_ref, b_ref, *rest):
        o_ref, ob_ref = rest[len(deps):]
        total = a_ref[...] + b_ref[...]
        o_ref[...] = total
        ob_ref[...] = total.astype(BF16)

    out_spec = pl.BlockSpec((None, tr, cols), lambda q, i, c: (q, i, 0))
    return pl.pallas_call(
        body,
        grid_spec=pltpu.PrefetchScalarGridSpec(
            num_scalar_prefetch=1, grid=(nq, nb),
            in_specs=[pl.BlockSpec((None, tr, cols), lambda q, i, c: (q, c[0] * nb + i, 0)),
                      pl.BlockSpec((None, tr, cols), lambda q, i, c: (q, i, 0))] + [ANY] * len(deps),
            out_specs=[out_spec, out_spec]),
        out_shape=[jax.ShapeDtypeStruct((nq, h, cols), F32), jax.ShapeDtypeStruct((nq, h, cols), BF16)],
        compiler_params=_cparams("parallel", "parallel"),
        name=name,
    )(c_arr, grad, got, *deps)


def _add_chips(part, got, chip_arr, name, deps=()):
    _, h, cols = part.shape
    tr = 128

    def body(q_ref, p_ref, g0_ref, g1_ref, g2_ref, *rest):
        o_ref = rest[len(deps)]
        o_ref[...] = ((p_ref[...] + g0_ref[...].astype(F32)) + g1_ref[...].astype(F32)) + g2_ref[...].astype(F32)

    got_spec = lambda j: pl.BlockSpec((None, tr, cols), lambda i, q: (j, i, 0))
    return pl.pallas_call(
        body,
        grid_spec=pltpu.PrefetchScalarGridSpec(
            num_scalar_prefetch=1, grid=(h // tr,),
            in_specs=[pl.BlockSpec((None, tr, cols), lambda i, q: (q[0], i, 0)), got_spec(0), got_spec(1), got_spec(2)]
            + [ANY] * len(deps),
            out_specs=pl.BlockSpec((tr, cols), lambda i, q: (i, 0))),
        out_shape=jax.ShapeDtypeStruct((h, cols), F32),
        compiler_params=_cparams("parallel"),
        name=name,
    )(chip_arr, part, got, got, got, *deps)


def _sequencer_exchange(src, out_shape, collective_id, name, plan, n_copies):
    src_ref = jax.new_ref(src, memory_space=pltpu.MemorySpace.HBM)
    dst_ref = jax.empty_ref(out_shape, memory_space=pltpu.MemorySpace.HBM)

    @pl.kernel(mesh=plsc.ScalarSubcoreMesh(axis_name="seq", num_cores=1), name=name,
               scratch_types=(pltpu.SemaphoreType.DMA((n_copies,)), pltpu.SemaphoreType.DMA((n_copies,))),
               compiler_params=pltpu.CompilerParams(collective_id=collective_id))
    def launch(send_sems, recv_sems):
        x, y, c, chips = _position()
        copies = plan(src_ref, dst_ref, x, y, c, chips)
        _handshake([peer for _, _, peer in copies])
        started = []
        for k, (s, d, peer) in enumerate(copies):
            cp = _remote(s, d, send_sems.at[k], recv_sems.at[k], peer)
            cp.start()
            started.append(cp)
        for cp in started:
            cp.wait()

    launch()
    return dst_ref[...]


class _AsyncReduceScatter:
    def __init__(self, grad, nm, first_id):
        self.grad, self.nm, self.first_id = grad, nm, first_id
        nq, rows, cols = grad.shape
        h = self.h = rows // 2

        def to_sibling(s, d, x, y, c, chips):
            return [(s.at[:, pl.ds((1 - c) * h, h), :], d, (x, y, 1 - c))]

        self.from_sibling = _sequencer_exchange(grad, jax.ShapeDtypeStruct((nq, h, cols), F32), first_id,
                                                f"rs_sibling_{nm}", to_sibling, 1)

    def sibling_sum(self, not_before=()):
        cols = self.grad.shape[2]
        c_arr = lax.axis_index("c").astype(jnp.int32).reshape(1)
        self.part, self.part_b = _add_sibling(self.grad, self.from_sibling, c_arr, f"add_sibling_{self.nm}", not_before)

        def to_chips(s, d, x, y, c, chips):
            return [(s.at[2 * chip[0] + chip[1]], d.at[j], (chip[0], chip[1], c)) for j, chip in enumerate(chips)]

        self.from_chips = _sequencer_exchange(self.part_b, jax.ShapeDtypeStruct((3, self.h, cols), BF16),
                                              self.first_id + 1, f"rs_quarters_{self.nm}", to_chips, 3)
        return self.part_b

    def chip_sum(self, not_before=()):
        cols = self.grad.shape[2]
        chip_arr = (2 * lax.axis_index("x") + lax.axis_index("y")).astype(jnp.int32).reshape(1)
        self.half = _add_chips(self.part, self.from_chips, chip_arr, f"add_chips_{self.nm}", not_before)

        def whole_to_sibling(s, d, x, y, c, chips):
            return [(s, d, (x, y, 1 - c))]

        self.other = _sequencer_exchange(self.half, jax.ShapeDtypeStruct((self.h, cols), F32), self.first_id + 2,
                                         f"rs_share_{self.nm}", whole_to_sibling, 1)
        return self.half

    def share(self):
        return self.half, self.other


def _after(x, deps, name):
    def body(x_ref, *rest):
        rest[-1][...] = x_ref[...]

    vm = pl.BlockSpec(memory_space=pltpu.VMEM)
    return pl.pallas_call(body, in_specs=[vm] + [ANY] * len(deps), out_specs=vm,
                          out_shape=jax.ShapeDtypeStruct(x.shape, x.dtype), name=name)(x, *deps)


def _adamw_halves(w, mine, other, m, v, name):
    rows, cols = w.shape
    tr = 128
    nb = rows // 2 // tr
    c_arr = lax.axis_index("c").astype(jnp.int32).reshape(1)

    def body(c_ref, w_ref, a_ref, b_ref, m_ref, v_ref, g_out, d_out, m_out, v_out):
        is_mine = (pl.program_id(0) // nb) == c_ref[0]
        g = jnp.where(is_mine, a_ref[...], b_ref[...])
        wb, mb, vb = w_ref[...], m_ref[...], v_ref[...]
        m2 = ADAM_B1 * mb + (1.0 - ADAM_B1) * g
        v2 = ADAM_B2 * vb + (1.0 - ADAM_B2) * (g * g)
        m_hat = m2 / (1.0 - ADAM_B1 ** ADAM_STEP)
        v_hat = v2 / (1.0 - ADAM_B2 ** ADAM_STEP)
        g_out[...] = g
        d_out[...] = -ADAM_LR * (m_hat / (jnp.sqrt(v_hat) + ADAM_EPS) + ADAM_WD * wb)
        m_out[...] = m2
        v_out[...] = v2

    full = pl.BlockSpec((tr, cols), lambda i, c: (i, 0))
    half = pl.BlockSpec((tr, cols), lambda i, c: (i % nb, 0))
    return pl.pallas_call(
        body,
        grid_spec=pltpu.PrefetchScalarGridSpec(
            num_scalar_prefetch=1, grid=(rows // tr,),
            in_specs=[full, half, half, full, full], out_specs=[full] * 4),
        out_shape=[jax.ShapeDtypeStruct((rows, cols), F32)] * 4,
        compiler_params=_cparams("parallel"),
        name=name,
    )(c_arr, w, mine, other, m, v)


def _all_sum_small(v):
    n_dev = 8

    def body(v_ref, o_ref, gath, send_sems, recv_sems):
        x, y, c, _ = _position()
        me = 4 * x + 2 * y + c
        gath[me] = v_ref[...]
        copies = []
        for k in range(1, n_dev):
            peer = tuple(1 - p if (k >> s) & 1 else p for p, s in ((x, 2), (y, 1), (c, 0)))
            cp = _remote(v_ref, gath.at[me], send_sems.at[k - 1], recv_sems.at[k - 1], peer)
            cp.start()
            copies.append(cp)
        for cp in copies:
            cp.wait()
        acc = gath[0]
        for i in range(1, n_dev):
            acc = acc + gath[i]
        o_ref[...] = acc

    vm = pl.BlockSpec(memory_space=pltpu.VMEM)
    return pl.pallas_call(
        body,
        in_specs=[vm],
        out_specs=vm,
        out_shape=jax.ShapeDtypeStruct(v.shape, F32),
        scratch_shapes=[pltpu.VMEM((n_dev,) + v.shape, F32), pltpu.SemaphoreType.DMA((n_dev - 1,)),
                        pltpu.SemaphoreType.DMA((n_dev - 1,))],
        name="all_sum_small",
    )(v)


def _pack_rows(vectors):
    rows = []
    for v in vectors:
        flat = v.reshape(-1).astype(F32)
        rows.append(jnp.pad(flat, (0, (-flat.shape[0]) % LANES)).reshape(-1, LANES))
    out = jnp.concatenate(rows, axis=0)
    return jnp.pad(out, ((0, (-out.shape[0]) % 8), (0, 0)))


def _unpack_rows(packed, shapes):
    outs, r = [], 0
    for shp in shapes:
        size = math.prod(shp)
        nr = -(-size // LANES)
        outs.append(packed[r:r + nr].reshape(-1)[:size].reshape(shp))
        r += nr
    return outs


def _relu_sq(acc):
    r = jnp.maximum(acc, 0.0)
    return r, r * r


def _relu_sq_bwd(acc, r):
    return (acc * (2.0 * r.astype(F32)),)


def kernel(x, norm_mix_pre, w_in, conv_w, conv_b, dt_bias, a_log, d_skip, ssm_norm_w, w_out, norm_mix_post, norm_mlp_pre, w_up, w_down, norm_mlp_post, loss_target, m_norm_mix_pre, m_w_in, m_conv_w, m_conv_b, m_dt_bias, m_a_log, m_d_skip, m_ssm_norm_w, m_w_out, m_norm_mix_post, m_norm_mlp_pre, m_w_up, m_w_down, m_norm_mlp_post, v_norm_mix_pre, v_w_in, v_conv_w, v_conv_b, v_dt_bias, v_a_log, v_d_skip, v_ssm_norm_w, v_w_out, v_norm_mix_post, v_norm_mlp_pre, v_w_up, v_w_down, v_norm_mlp_post):
    s_dim = x.shape[1]
    xs, target = x[0], loss_target[0]
    chip = 2 * lax.axis_index("x") + lax.axis_index("y")

    own = [w_in[0].astype(BF16), w_out[0].astype(BF16), w_up[0].astype(BF16), w_down[0].astype(BF16)]
    fetched_in = _gather_shards(own[:1])[0]
    fetched_in, *rest = lax.optimization_barrier((fetched_in, *own[1:]))
    fetched = [fetched_in] + _gather_shards_async(rest, 1, "gather_rest")
    g_in, g_out, g_up, g_down = [_by_chip(o, f) for o, f in zip(own, fetched)]
    w_z = _cols_from_quarters(g_in, *IN_COLS["z"])
    w_xbc = _perm_cols(_cols_from_quarters(g_in, *IN_COLS["xbc"]))
    w_dt = jnp.pad(_cols_from_quarters(g_in, *IN_COLS["dt"]), ((0, 0), (0, LANES - SSM_HEADS)))
    w_qkv = _cols_from_quarters(g_in, *IN_COLS["qkv"])
    w_out_full = g_out.reshape(D_MIX, D_MODEL)
    w_down_full = g_down.reshape(D_FF, D_MODEL)

    conv_cols = D_XBC // N_CHIPS
    conv_placed = lax.dynamic_update_slice(jnp.zeros((8, D_XBC), F32), 0.5 * conv_w[0], (0, chip * conv_cols))
    conv_full = _all_sum_small(conv_placed.reshape(-1, LANES)).reshape(8, D_XBC)
    w8 = _perm_cols(conv_full.at[CONV_WIDTH].set(conv_b[0]))

    u = _pre_norm(xs, norm_mix_pre)
    z = _matmul([(u, w_z, TK)], "nn", [F32], name="proj_z")
    xbc = _matmul([(u, w_xbc, TK)], "nn", [F32], name="proj_xbc")
    dt_raw = _matmul([(u, w_dt, TK)], "nn", [F32], name="proj_dt")
    qkv = _matmul([(u, w_qkv, TK)], "nn", [BF16], name="proj_qkv")
    xc = _conv_fwd(xbc, w8)
    dtg = _dt_to_groups(dt_raw)
    par = _pack_ssd_params(dt_bias[0], a_log[0], d_skip[0])
    y, y_ssm, states = _ssd_fwd(xc, z, dtg, par, ssm_norm_w)
    y_att, y_att_f32, lse = _attn_fused_fwd(qkv)
    y_mix = jnp.concatenate([y_ssm, y_att], axis=1)
    mix = _matmul([(y_mix, w_out_full, TK)], "nn", [F32], name="out_proj")
    h1, u2 = _post_pre_norm(xs, mix, norm_mix_post, norm_mlp_pre)
    hid, act = _matmul([(u2, g_up, TK)], "nn", [BF16, BF16], name="mlp_up", epilogue=_relu_sq)
    ff = _matmul([(act, w_down_full, TK)], "nn", [F32], name="mlp_down")
    dh2, dff, d_g4, loss_part = _tail(ff, h1, target, norm_mlp_post)

    dhid = _matmul([(dff, w_down_full, TK)], "nt", [BF16], name="mlp_down_dx", epilogue=_relu_sq_bwd, extras=[hid])
    weights = {"norm_mix_pre": (norm_mix_pre, m_norm_mix_pre, v_norm_mix_pre), "w_in": (w_in, m_w_in, v_w_in),
               "conv_w": (conv_w, m_conv_w, v_conv_w), "conv_b": (conv_b, m_conv_b, v_conv_b),
               "dt_bias": (dt_bias, m_dt_bias, v_dt_bias), "a_log": (a_log, m_a_log, v_a_log),
               "d_skip": (d_skip, m_d_skip, v_d_skip), "ssm_norm_w": (ssm_norm_w, m_ssm_norm_w, v_ssm_norm_w),
               "w_out": (w_out, m_w_out, v_w_out), "norm_mix_post": (norm_mix_post, m_norm_mix_post, v_norm_mix_post),
               "norm_mlp_pre": (norm_mlp_pre, m_norm_mlp_pre, v_norm_mlp_pre), "w_up": (w_up, m_w_up, v_w_up),
               "w_down": (w_down, m_w_down, v_w_down),
               "norm_mlp_post": (norm_mlp_post, m_norm_mlp_post, v_norm_mlp_post)}
    grads, delta, new_m, new_v = {}, {}, {}, {}

    def adamw_big(n, halves):
        w, m, v = weights[n]
        g_, d_, m_, v_ = _adamw_halves(w[0], halves[0], halves[1], m[0], v[0], f"adamw_{n}")
        grads[n], delta[n], new_m[n], new_v[n] = g_[None], d_[None], m_[None], v_[None]

    dw_down = _matmul([(act, dff, TK)], "tn", [F32], name="mlp_down_dw")
    rs_down = _AsyncReduceScatter(dw_down.reshape(N_CHIPS, D_FF // N_CHIPS, D_MODEL), "w_down", 11)
    dw_up = _matmul([(u2, dhid, TK)], "tn", [F32], name="mlp_up_dw", deps=[dw_down], out_quarters=True)
    rs_up = _AsyncReduceScatter(dw_up, "w_up", 8)
    du2 = _matmul([(dhid, g_up, TK)], "nt", [F32], name="mlp_up_dx",
                  deps=[rs_down.sibling_sum(not_before=[dw_up])])
    dh1, dmix, d_g3, d_g2 = _mid_bwd(du2, h1, dh2, mix, norm_mix_post, norm_mlp_pre,
                                     deps=[rs_up.sibling_sum(not_before=[du2])])
    dymix = _matmul([(dmix, w_out_full, TK)], "nt", [F32], name="out_proj_dx")
    dw_out = _matmul([(y_mix, dmix, TK)], "tn", [F32], name="out_proj_dw")
    rs_out = _AsyncReduceScatter(dw_out.reshape(N_CHIPS, D_MIX // N_CHIPS, D_MODEL), "w_out", 5)
    dqkv = _attn_fused_bwd(qkv, dymix, y_att_f32, lse)
    par_late = _after(par, [rs_down.chip_sum(not_before=[dqkv]), rs_out.sibling_sum(not_before=[dymix])],
                      "after_w_down")
    dxc, dz, ddtg, dpar, d_nw = _ssd_bwd(xc, z, dtg, par_late, ssm_norm_w, y, states, dymix)
    g_down = rs_down.share()
    dxbc, dw8 = _conv_bwd(xbc, _after(w8, [*g_down, rs_up.chip_sum(not_before=[dxc])], "after_w_up"), dxc)
    ddt = jnp.pad(_dt_from_groups(ddtg), ((0, 0), (0, LANES - SSM_HEADS))).astype(BF16)
    g_up = rs_up.share()
    dw_z = _matmul([(u, dz, TK)], "tn", [F32], name="proj_z_dw")
    dw_xbc = _matmul([(u, dxbc, TK)], "tn", [F32], name="proj_xbc_dw",
                     deps=[*g_up, rs_out.chip_sum(not_before=[dxbc])])
    g_out = rs_out.share()
    dw_dt = _matmul([(u, ddt, TK)], "tn", [F32], name="proj_dt_dw")
    dw_qkv = _matmul([(u, dqkv, TK)], "tn", [F32], name="proj_qkv_dw")
    dw_in = _quarters_from_cols({"z": dw_z, "xbc": _unperm_cols(dw_xbc), "dt": dw_dt[:, :SSM_HEADS], "qkv": dw_qkv})
    rs_in = _AsyncReduceScatter(dw_in, "w_in", 2)
    adamw_big("w_down", g_down)
    adamw_big("w_up", g_up)
    rs_in.sibling_sum(not_before=[delta["w_up"]])
    du = _matmul([(dz, w_z, TK_MULTI), (dxbc, w_xbc, TK_MULTI), (dqkv, w_qkv, TK_MULTI), (ddt, w_dt, LANES)], "nt",
                 [F32], name="proj_dx", deps=[*g_out, rs_in.part_b])
    grad_x, d_g1 = _first_bwd(du, xs, dh1, norm_mix_pre)
    adamw_big("w_out", g_out)
    rs_in.chip_sum(not_before=[grad_x, delta["w_out"]])

    dconv = _unperm_cols(dw8)
    d_bias, d_alog, d_dskip = _unpack_ssd_params(dpar)
    small_shapes = [(1, D_MODEL), (CONV_WIDTH, D_XBC), (1, D_XBC), (1, SSM_HEADS), (1, SSM_HEADS), (1, SSM_HEADS),
                    (1, D_SSM), (1, D_MODEL), (1, D_MODEL), (1, D_MODEL), (1, LANES)]
    summed = _unpack_rows(
        _all_sum_small(_pack_rows([d_g1, dconv[:CONV_WIDTH], dconv[CONV_WIDTH:CONV_WIDTH + 1], d_bias, d_alog,
                                   d_dskip, d_nw, d_g2, d_g3, d_g4, loss_part])), small_shapes)
    (g_g1, g_conv_full, g_conv_b, g_bias, g_alog, g_dskip, g_nw, g_g2, g_g3, g_g4, loss_row) = summed
    loss = loss_row[0, 0]
    g_conv_w = lax.dynamic_slice(g_conv_full, (0, chip * conv_cols), (CONV_WIDTH, conv_cols))[None]

    grads.update({"norm_mix_pre": g_g1, "conv_w": g_conv_w, "conv_b": g_conv_b, "dt_bias": g_bias,
                  "a_log": g_alog, "d_skip": g_dskip, "ssm_norm_w": g_nw, "norm_mix_post": g_g2,
                  "norm_mlp_pre": g_g3, "norm_mlp_post": g_g4})
    order = list(weights)
    small_names = [n for n in order if n not in ("w_in", "w_out", "w_up", "w_down")]
    small_w_shapes = [weights[n][0].shape for n in small_names]
    packed = [_pack_rows([weights[n][k] for n in small_names]) for k in range(3)]
    packed_g = _pack_rows([grads[n].reshape(weights[n][0].shape) for n in small_names])
    sd, sm, sv = _adamw(packed[0], packed_g, packed[1], packed[2], "adamw_small")
    for k, n in enumerate(small_names):
        grads[n] = grads[n].reshape(weights[n][0].shape)
    for res, pk in ((delta, sd), (new_m, sm), (new_v, sv)):
        for n, val in zip(small_names, _unpack_rows(pk, small_w_shapes)):
            res[n] = val
    adamw_big("w_in", rs_in.share())

    return (loss, grad_x[None], *[grads[n] for n in order], *[delta[n] for n in order],
            *[new_m[n] for n in order], *[new_v[n] for n in order])
```

```python
import functools
import math

import numpy as np
import jax
import jax.numpy as jnp
from jax import lax
from jax.experimental import pallas as pl
from jax.experimental.pallas import tpu as pltpu
from jax.experimental.pallas import tpu_sc as plsc

F32 = jnp.float32
BF16 = jnp.bfloat16

D_MODEL = 2048
SSM_HEAD_DIM = 64
SSM_GROUPS = 8
HEADS_PER_GROUP = 4
SSM_HEADS = SSM_GROUPS * HEADS_PER_GROUP
D_SSM = SSM_HEADS * SSM_HEAD_DIM
D_STATE = 128
CONV_WIDTH = 4
SSD_CHUNK = 128
D_XBC = D_SSM + 2 * SSM_GROUPS * D_STATE
GROUP_X = HEADS_PER_GROUP * SSM_HEAD_DIM
GROUP_COLS = GROUP_X + 2 * D_STATE
ATT_HEAD_DIM = 128
ATT_HEADS = 16
D_ATT = ATT_HEADS * ATT_HEAD_DIM
DILATIONS = (1, 4, 16)
ATT_BLOCK = 128
D_MIX = D_SSM + D_ATT
D_IN_PROJ = D_SSM + D_XBC + SSM_HEADS + 3 * D_ATT
D_FF = 4 * D_MODEL
EPS = 1e-6
N_CHIPS = 4
W_IN_SHARD = D_IN_PROJ // N_CHIPS

ADAM_LR = 0.001
ADAM_B1 = 0.9
ADAM_B2 = 0.999
ADAM_EPS = 1e-08
ADAM_WD = 0.01
ADAM_STEP = 10

LANES = 128
VMEM_LIMIT = 48 * 1024 * 1024
MESH = pl.DeviceIdType.MESH

_NN = (((1,), (0,)), ((), ()))
_NT = (((1,), (1,)), ((), ()))
_TN = (((0,), (0,)), ((), ()))


def _dot(a, b, dims=_NN):
    return lax.dot_general(a, b, dims, preferred_element_type=F32)


def _cparams(*sem):
    return pltpu.CompilerParams(dimension_semantics=sem, vmem_limit_bytes=VMEM_LIMIT)


TK = 2048
TK_MULTI = 1024


def _matmul(pairs, mode, out_dtypes, *, name, tm=1024, tn=1024, epilogue=None, extras=(), deps=(), out_quarters=False):
    a0, b0, _ = pairs[0]
    m_dim = a0.shape[-1] if mode == "tn" else a0.shape[-2]
    if b0.ndim == 3:
        n_dim = b0.shape[1] if mode == "nt" else b0.shape[0] * b0.shape[2]
    else:
        n_dim = b0.shape[0] if mode == "nt" else b0.shape[1]
    tm, tn = min(tm, m_dim), min(tn, n_dim)
    nks, offs = [], []
    for a, _, tk in pairs:
        k_part = a.shape[0] if mode == "tn" else a.shape[-1]
        k_dim = k_part * (a.shape[0] if a.ndim == 3 else 1)
        assert k_part % tk == 0, (name, k_part, tk)
        offs.append(sum(nks))
        nks.append(k_dim // tk)
    nk_total = sum(nks)
    assert m_dim % tm == 0 and n_dim % tn == 0, (name, m_dim, n_dim)
    dims = {"nn": _NN, "nt": _NT, "tn": _TN}[mode]
    n_pairs, n_extra, n_out = len(pairs), len(extras), len(out_dtypes)

    in_specs, operands = [], []
    for (a, b, tk), off, nk in zip(pairs, offs, nks):
        def kidx(k, off=off, nk=nk):
            return k if n_pairs == 1 else jnp.clip(k - off, 0, nk - 1)
        if mode == "tn":
            assert a.ndim == 2
            in_specs.append(pl.BlockSpec((tk, tm), lambda m, n, k, f=kidx: (f(k), m)))
        elif a.ndim == 3:
            per = a.shape[2] // tk
            in_specs.append(pl.BlockSpec((None, tm, tk), lambda m, n, k, f=kidx, per=per: (f(k) // per, m, f(k) % per)))
        else:
            in_specs.append(pl.BlockSpec((tm, tk), lambda m, n, k, f=kidx: (m, f(k))))
        if b.ndim == 3 and mode == "nt":
            per = b.shape[2] // tk
            in_specs.append(pl.BlockSpec((None, tn, tk), lambda m, n, k, f=kidx, per=per: (f(k) // per, n, f(k) % per)))
        elif b.ndim == 3:
            per = b.shape[2] // tn
            in_specs.append(pl.BlockSpec((None, tk, tn), lambda m, n, k, f=kidx, per=per: (n // per, f(k), n % per)))
        elif mode == "nt":
            in_specs.append(pl.BlockSpec((tn, tk), lambda m, n, k, f=kidx: (n, f(k))))
        else:
            in_specs.append(pl.BlockSpec((tk, tn), lambda m, n, k, f=kidx: (f(k), n)))
        operands += [a, b]
    for e in extras:
        in_specs.append(pl.BlockSpec((tm, tn), lambda m, n, k: (m, n)))
        operands.append(e)
    in_specs += [pl.BlockSpec(memory_space=pl.ANY)] * len(deps)
    operands += list(deps)
    first_out = 2 * n_pairs + n_extra + len(deps)
    if out_quarters:
        out_per_q = n_dim // N_CHIPS // tn
        out_dims = (N_CHIPS, m_dim, n_dim // N_CHIPS)
        out_spec = pl.BlockSpec((None, tm, tn), lambda m, n, k: (n // out_per_q, m, n % out_per_q))
    else:
        out_dims = (m_dim, n_dim)
        out_spec = pl.BlockSpec((tm, tn), lambda m, n, k: (m, n))

    def body(*refs):
        ab = refs[:2 * n_pairs]
        e_refs = refs[2 * n_pairs:2 * n_pairs + n_extra]
        o_refs = refs[first_out:first_out + n_out]

        def finish(total):
            vals = (total,) if epilogue is None else epilogue(total, *[e[...] for e in e_refs])
            for o_ref, v in zip(o_refs, vals):
                o_ref[...] = v.astype(o_ref.dtype)

        if nk_total == 1:
            finish(_dot(ab[0][...], ab[1][...], dims))
            return
        acc = refs[-1]
        k = pl.program_id(2)

        @pl.when(k == 0)
        def _():
            acc[...] = jnp.zeros_like(acc)

        for i in range(n_pairs):
            def accumulate(i=i):
                acc[...] += _dot(ab[2 * i][...], ab[2 * i + 1][...], dims)
            if n_pairs == 1:
                accumulate()
            else:
                pl.when((k >= offs[i]) & (k < offs[i] + nks[i]))(accumulate)

        @pl.when(k == nk_total - 1)
        def _():
            finish(acc[...])

    outs = pl.pallas_call(
        body,
        grid=(m_dim // tm, n_dim // tn, nk_total),
        in_specs=in_specs,
        out_specs=[out_spec for _ in out_dtypes],
        out_shape=[jax.ShapeDtypeStruct(out_dims, dt) for dt in out_dtypes],
        scratch_shapes=[pltpu.VMEM((tm, tn), F32)] if nk_total > 1 else [],
        compiler_params=_cparams("parallel", "parallel", "arbitrary"),
        name=name,
    )(*operands)
    return outs[0] if n_out == 1 else outs


def _rowcall(fn, rows, vecs, row_outs, acc_widths, *, name, tr=256, row_cols=None, deps=()):
    s_dim = rows[0].shape[0]
    assert s_dim % tr == 0
    row_cols = row_cols or [None] * len(rows)
    n_r, n_v, n_ro, n_acc = len(rows), len(vecs), len(row_outs), len(acc_widths)
    in_specs = []
    for r, rc in zip(rows, row_cols):
        if rc is None:
            in_specs.append(pl.BlockSpec((tr, r.shape[1]), lambda i: (i, 0)))
        else:
            in_specs.append(pl.BlockSpec((tr, rc[0]), lambda i, c=rc[1]: (i, c)))
    for v in vecs:
        in_specs.append(pl.BlockSpec(v.shape, lambda i, nd=v.ndim: (0,) * nd))
    in_specs += [pl.BlockSpec(memory_space=pl.ANY)] * len(deps)
    n_d = len(deps)

    def body(*refs):
        ins = [r[...] for r in refs[:n_r + n_v]]
        ro = refs[n_r + n_v + n_d:n_r + n_v + n_d + n_ro]
        ao = refs[n_r + n_v + n_d + n_ro:]
        outs = fn(*ins)
        for ref, v in zip(ro, outs[:n_ro]):
            ref[...] = v.astype(ref.dtype)
        if n_acc:
            @pl.when(pl.program_id(0) == 0)
            def _():
                for ref in ao:
                    ref[...] = jnp.zeros_like(ref)
            for ref, v in zip(ao, outs[n_ro:]):
                ref[...] += v

    outs = pl.pallas_call(
        body,
        grid=(s_dim // tr,),
        in_specs=in_specs,
        out_specs=[pl.BlockSpec((tr, w), lambda i: (i, 0)) for w, _ in row_outs]
        + [pl.BlockSpec((1, w), lambda i: (0, 0)) for w in acc_widths],
        out_shape=[jax.ShapeDtypeStruct((s_dim, w), dt) for w, dt in row_outs]
        + [jax.ShapeDtypeStruct((1, w), F32) for w in acc_widths],
        compiler_params=_cparams("arbitrary"),
        name=name,
    )(*rows, *vecs, *deps)
    return outs


def _nrm(x, g):
    r = lax.rsqrt(jnp.mean(x * x, axis=-1, keepdims=True) + EPS)
    n = x * r
    return n * g, n, r


def _nrm_bwd(dy, n, r, g):
    dn = dy * g
    dx = r * (dn - n * jnp.mean(dn * n, axis=-1, keepdims=True))
    return dx, jnp.sum(dy * n, axis=0, keepdims=True)


def _sigmoid(x):
    return 1.0 / (1.0 + jnp.exp(-x))


def _softplus(x):
    return jnp.maximum(x, 0.0) + jnp.log(1.0 + jnp.exp(-jnp.abs(x)))


def _pre_norm(x, g1):
    def fn(xb, g):
        return (_nrm(xb, g)[0],)
    return _rowcall(fn, [x], [g1], [(D_MODEL, BF16)], [], name="pre_norm")[0]


def _post_pre_norm(x, mix, g2, g3):
    def fn(xb, mb, g2b, g3b):
        h1 = xb + _nrm(mb, g2b)[0]
        return h1, _nrm(h1, g3b)[0]
    return _rowcall(fn, [x, mix], [g2, g3], [(D_MODEL, F32), (D_MODEL, BF16)], [], name="post_pre_norm")


def _tail(ff, h1, target, g4):
    def fn(ffb, h1b, tb, g):
        y, n, r = _nrm(ffb, g)
        e = h1b + y - tb
        loss = 0.5 * jnp.sum(jnp.sum(e * e, axis=-1, keepdims=True) * (1.0 / D_MODEL), axis=0, keepdims=True)
        dh2 = e * (1.0 / D_MODEL)
        dff, dg = _nrm_bwd(dh2, n, r, g)
        return dh2, dff, dg, jnp.broadcast_to(loss, (1, LANES))
    return _rowcall(fn, [ff, h1, target], [g4], [(D_MODEL, F32), (D_MODEL, BF16)], [D_MODEL, LANES], name="tail")


def _mid_bwd(du2, h1, dh2, mix, g2, g3, deps=()):
    def fn(du2b, h1b, dh2b, mb, g2b, g3b):
        _, n3, r3 = _nrm(h1b, g3b)
        d3, dg3 = _nrm_bwd(du2b, n3, r3, g3b)
        dh1 = dh2b + d3
        _, n2, r2 = _nrm(mb, g2b)
        dmix, dg2 = _nrm_bwd(dh1, n2, r2, g2b)
        return dh1, dmix, dg3, dg2
    return _rowcall(fn, [du2, h1, dh2, mix], [g2, g3], [(D_MODEL, F32), (D_MODEL, BF16)], [D_MODEL, D_MODEL],
                    name="mid_bwd", deps=deps)


def _first_bwd(du, x, dh1, g1):
    def fn(dub, xb, dh1b, g):
        _, n, r = _nrm(xb, g)
        dx, dg = _nrm_bwd(dub, n, r, g)
        return dh1b + dx, dg
    return _rowcall(fn, [du, x, dh1], [g1], [(D_MODEL, F32)], [D_MODEL], name="first_bwd")


CONV_TILE = 256
CONV_ROWS = 256
PAD = 8


def _conv_taps(w):
    return [w[k:k + 1, :] for k in range(CONV_WIDTH)], w[CONV_WIDTH:CONV_WIDTH + 1, :]


def _conv_fwd(xbc, w8):
    s_dim, c_dim = xbc.shape
    n_steps = s_dim // CONV_ROWS

    def body(x_ref, w_ref, o_ref, xp):
        xp[0:PAD, :] = jnp.zeros((PAD, CONV_TILE), F32)
        xp[PAD:PAD + s_dim, :] = x_ref[...]
        taps, bias = _conv_taps(w_ref[...])

        def step(c, carry):
            base = pl.multiple_of(c * CONV_ROWS, CONV_ROWS)
            win = xp[pl.ds(base, CONV_ROWS + PAD), :]
            pre = bias + taps[3] * win[PAD:, :]
            for j in range(1, CONV_WIDTH):
                pre = pre + taps[3 - j] * pltpu.roll(win, j, axis=0)[PAD:, :]
            o_ref[pl.ds(base, CONV_ROWS), :] = pre * _sigmoid(pre)
            return carry

        lax.fori_loop(0, n_steps, step, 0)

    return pl.pallas_call(
        body,
        grid=(c_dim // CONV_TILE,),
        in_specs=[pl.BlockSpec((s_dim, CONV_TILE), lambda j: (0, j)), pl.BlockSpec((8, CONV_TILE), lambda j: (0, j))],
        out_specs=pl.BlockSpec((s_dim, CONV_TILE), lambda j: (0, j)),
        out_shape=jax.ShapeDtypeStruct((s_dim, c_dim), F32),
        scratch_shapes=[pltpu.VMEM((s_dim + 2 * PAD, CONV_TILE), F32)],
        compiler_params=_cparams("parallel"),
        name="conv_fwd",
    )(xbc, w8)


def _conv_bwd(xbc, w8, dxc):
    s_dim, c_dim = xbc.shape
    n_steps = s_dim // CONV_ROWS

    def body(x_ref, w_ref, d_ref, dx_ref, dw_ref, xp, dp):
        xp[0:PAD, :] = jnp.zeros((PAD, CONV_TILE), F32)
        xp[PAD:PAD + s_dim, :] = x_ref[...]
        dp[PAD + s_dim:, :] = jnp.zeros((PAD, CONV_TILE), F32)
        taps, bias = _conv_taps(w_ref[...])

        def step1(c, sums):
            base = pl.multiple_of(c * CONV_ROWS, CONV_ROWS)
            win = xp[pl.ds(base, CONV_ROWS + PAD), :]
            shifted = [win[PAD:, :]] + [pltpu.roll(win, j, axis=0)[PAD:, :] for j in range(1, CONV_WIDTH)]
            pre = bias
            for j in range(CONV_WIDTH):
                pre = pre + taps[3 - j] * shifted[j]
            sg = _sigmoid(pre)
            dpre = d_ref[pl.ds(base, CONV_ROWS), :] * (sg * (1.0 + pre * (1.0 - sg)))
            dp[pl.ds(base + PAD, CONV_ROWS), :] = dpre
            new = [sums[k] + jnp.sum(dpre * shifted[3 - k], axis=0, keepdims=True) for k in range(CONV_WIDTH)]
            new.append(sums[CONV_WIDTH] + jnp.sum(dpre, axis=0, keepdims=True))
            return tuple(new)

        zero = jnp.zeros((1, CONV_TILE), F32)
        sums = lax.fori_loop(0, n_steps, step1, (zero,) * (CONV_WIDTH + 1))
        dw_ref[...] = jnp.zeros((8, CONV_TILE), F32)
        for k in range(CONV_WIDTH + 1):
            dw_ref[k:k + 1, :] = sums[k]

        def step2(c, carry):
            base = pl.multiple_of(c * CONV_ROWS, CONV_ROWS)
            win = dp[pl.ds(base + PAD, CONV_ROWS + PAD), :]
            dx = taps[3] * win[:CONV_ROWS, :]
            for j in range(1, CONV_WIDTH):
                dx = dx + taps[3 - j] * pltpu.roll(win, CONV_ROWS + PAD - j, axis=0)[:CONV_ROWS, :]
            dx_ref[pl.ds(base, CONV_ROWS), :] = dx.astype(BF16)
            return carry

        lax.fori_loop(0, n_steps, step2, 0)

    col = lambda j: (0, j)
    return pl.pallas_call(
        body,
        grid=(c_dim // CONV_TILE,),
        in_specs=[pl.BlockSpec((s_dim, CONV_TILE), col), pl.BlockSpec((8, CONV_TILE), col),
                  pl.BlockSpec((s_dim, CONV_TILE), col)],
        out_specs=[pl.BlockSpec((s_dim, CONV_TILE), col), pl.BlockSpec((8, CONV_TILE), col)],
        out_shape=[jax.ShapeDtypeStruct((s_dim, c_dim), BF16), jax.ShapeDtypeStruct((8, c_dim), F32)],
        scratch_shapes=[pltpu.VMEM((s_dim + 2 * PAD, CONV_TILE), F32), pltpu.VMEM((s_dim + 2 * PAD, CONV_TILE), F32)],
        compiler_params=_cparams("parallel"),
        name="conv_bwd",
    )(xbc, w8, dxc)


def _perm_cols(a):
    parts = []
    for g in range(SSM_GROUPS):
        parts += [a[..., g * GROUP_X:(g + 1) * GROUP_X],
                  a[..., D_SSM + g * D_STATE:D_SSM + (g + 1) * D_STATE],
                  a[..., D_SSM + SSM_GROUPS * D_STATE + g * D_STATE:D_SSM + SSM_GROUPS * D_STATE + (g + 1) * D_STATE]]
    return jnp.concatenate(parts, axis=-1)


def _unperm_cols(a):
    xs = [a[..., g * GROUP_COLS:g * GROUP_COLS + GROUP_X] for g in range(SSM_GROUPS)]
    bs = [a[..., g * GROUP_COLS + GROUP_X:g * GROUP_COLS + GROUP_X + D_STATE] for g in range(SSM_GROUPS)]
    cs = [a[..., g * GROUP_COLS + GROUP_X + D_STATE:(g + 1) * GROUP_COLS] for g in range(SSM_GROUPS)]
    return jnp.concatenate(xs + bs + cs, axis=-1)


def _dt_to_groups(dt):
    s_dim = dt.shape[0]
    t = dt[:, :SSM_HEADS].reshape(s_dim, SSM_GROUPS, HEADS_PER_GROUP).transpose(1, 0, 2)
    return jnp.pad(t, ((0, 0), (0, 0), (0, LANES - HEADS_PER_GROUP)))


def _dt_from_groups(dtg):
    s_dim = dtg.shape[1]
    return dtg[:, :, :HEADS_PER_GROUP].transpose(1, 0, 2).reshape(s_dim, SSM_HEADS)


def _pack_ssd_params(dt_bias, a_log, d_skip):
    rows = jnp.stack([p.reshape(SSM_GROUPS, HEADS_PER_GROUP) for p in (dt_bias, a_log, d_skip)], axis=1)
    return jnp.pad(rows, ((0, 0), (0, 8 - 3), (0, LANES - HEADS_PER_GROUP)))


def _unpack_ssd_params(par):
    return tuple(par[:, k, :HEADS_PER_GROUP].reshape(SSM_HEADS) for k in range(3))


Q = SSD_CHUNK


def _split3(v):
    hi = v.astype(BF16)
    r1 = v - hi.astype(F32)
    mid = r1.astype(BF16)
    lo = (r1 - mid.astype(F32)).astype(BF16)
    return hi, mid, lo


def _dot_l01(t01, v):
    return sum(_dot(t01, p) for p in _split3(v))


def _dot_r01(v, e01):
    return sum(_dot(p, e01) for p in _split3(v))


def _ssd_consts():
    row = lax.broadcasted_iota(jnp.int32, (Q, Q), 0)
    col = lax.broadcasted_iota(jnp.int32, (Q, Q), 1)
    causal = row >= col
    tril = causal.astype(BF16)
    triu = (col >= row).astype(BF16)
    er = lax.broadcasted_iota(jnp.int32, (LANES, GROUP_X), 0)
    ec = lax.broadcasted_iota(jnp.int32, (LANES, GROUP_X), 1) // SSM_HEAD_DIM
    expand = (er == ec).astype(BF16)
    rr = lax.broadcasted_iota(jnp.int32, (GROUP_X, LANES), 0) // SSM_HEAD_DIM
    rc = lax.broadcasted_iota(jnp.int32, (GROUP_X, LANES), 1)
    reduce = (rr == rc).astype(BF16)
    lane_head = lax.broadcasted_iota(jnp.int32, (Q, GROUP_X), 1) // SSM_HEAD_DIM
    return causal, tril, triu, expand, reduce, lane_head


def _ssd_common(xc_ref, dt_ref, par_ref, consts):
    causal, tril, _, expand, _, _ = consts
    par = par_ref[...]
    bias, alog, dsk = par[0:1, :], par[1:2, :], par[2:3, :]
    a_neg = -jnp.exp(alog)
    dtr = dt_ref[...] + bias
    dt = _softplus(dtr)
    s = _dot_l01(tril, dt * a_neg)
    dt_x = _dot_r01(dt, expand)
    s_x = _dot_r01(s, expand)
    dsk_x = _dot_r01(jnp.broadcast_to(dsk, (8, LANES)), expand)[0:1, :]
    blk = xc_ref[...]
    x = blk[:, :GROUP_X]
    bm = blk[:, GROUP_X:GROUP_X + D_STATE].astype(BF16)
    cm = blk[:, GROUP_X + D_STATE:].astype(BF16)
    xdt = x * dt_x
    g = _dot(cm, bm, _NT)
    return dict(a_neg=a_neg, dtr=dtr, dt=dt, s=s, s_t=s.T, dt_x=dt_x, s_x=s_x, dsk_x=dsk_x, x=x, bm=bm, cm=cm,
                xdt=xdt, g=g)


def _decay(v, r, causal):
    diff = v["s"][:, r:r + 1] - v["s_t"][r:r + 1, :]
    return jnp.exp(jnp.where(causal, diff, -jnp.inf))


def _ssd_specs(n_chunks, rev):
    cidx = (lambda c: n_chunks - 1 - c) if rev else (lambda c: c)
    xc = pl.BlockSpec((Q, GROUP_COLS), lambda g, c: (cidx(c), g))
    gx = pl.BlockSpec((Q, GROUP_X), lambda g, c: (cidx(c), g))
    dt = pl.BlockSpec((None, Q, LANES), lambda g, c: (g, cidx(c), 0))
    par = pl.BlockSpec((None, 8, LANES), lambda g, c: (g, 0, 0))
    nw = pl.BlockSpec((1, GROUP_X), lambda g, c: (0, g))
    hs = pl.BlockSpec((None, None, D_STATE, GROUP_X), lambda g, c: (cidx(c), g, 0, 0))
    return xc, gx, dt, par, nw, hs


def _ssd_fwd(xc, z, dtg, par, nw):
    s_dim = xc.shape[0]
    n_chunks = s_dim // Q
    xc_s, gx_s, dt_s, par_s, nw_s, hs_s = _ssd_specs(n_chunks, False)

    def body(xc_ref, z_ref, dt_ref, par_ref, nw_ref, y_ref, ys_ref, hs_ref, ht):
        @pl.when(pl.program_id(1) == 0)
        def _():
            ht[...] = jnp.zeros_like(ht)

        consts = _ssd_consts()
        causal, lane_head = consts[0], consts[5]
        v = _ssd_common(xc_ref, dt_ref, par_ref, consts)
        xdt_b = v["xdt"].astype(BF16)
        yd = jnp.zeros((Q, GROUP_X), F32)
        for r in range(HEADS_PER_GROUP):
            m = (v["g"] * _decay(v, r, causal)).astype(BF16)
            yd = yd + _dot(m, jnp.where(lane_head == r, xdt_b, jnp.zeros_like(xdt_b)))
        h = ht[...]
        hs_ref[...] = h
        yo = jnp.exp(v["s_x"]) * _dot(v["cm"], h.astype(BF16))
        y = yd + yo + v["dsk_x"] * v["x"]
        s_last = v["s_x"][Q - 1:Q, :]
        snew = _dot(v["bm"], (v["xdt"] * jnp.exp(s_last - v["s_x"])).astype(BF16), _TN)
        ht[...] = jnp.exp(s_last) * h + snew
        zz = z_ref[...]
        yg = y * (zz * _sigmoid(zz))
        y_ref[...] = y
        ys_ref[...] = _nrm(yg, nw_ref[...])[0].astype(BF16)

    return pl.pallas_call(
        body,
        grid=(SSM_GROUPS, n_chunks),
        in_specs=[xc_s, gx_s, dt_s, par_s, nw_s],
        out_specs=[gx_s, gx_s, hs_s],
        out_shape=[jax.ShapeDtypeStruct((s_dim, D_SSM), F32), jax.ShapeDtypeStruct((s_dim, D_SSM), BF16),
                   jax.ShapeDtypeStruct((n_chunks, SSM_GROUPS, D_STATE, GROUP_X), F32)],
        scratch_shapes=[pltpu.VMEM((D_STATE, GROUP_X), F32)],
        compiler_params=_cparams("parallel", "arbitrary"),
        name="ssd_fwd",
    )(xc, z, dtg, par, nw)


def _ssd_bwd(xc, z, dtg, par, nw, y, hs, dymix):
    s_dim = xc.shape[0]
    n_chunks = s_dim // Q
    xc_s, gx_s, dt_s, par_s, nw_s, hs_s = _ssd_specs(n_chunks, True)

    def body(xc_ref, z_ref, dt_ref, par_ref, nw_ref, y_ref, hs_ref, dys_ref,
             dxc_ref, dz_ref, ddt_ref, dpar_ref, dnw_ref, dht):
        @pl.when(pl.program_id(1) == 0)
        def _():
            dht[...] = jnp.zeros_like(dht)
            dpar_ref[...] = jnp.zeros_like(dpar_ref)
            dnw_ref[...] = jnp.zeros_like(dnw_ref)

        consts = _ssd_consts()
        causal, _, triu, _, reduce, lane_head = consts
        v = _ssd_common(xc_ref, dt_ref, par_ref, consts)
        x, bm, cm, xdt, s_x = v["x"], v["bm"], v["cm"], v["xdt"], v["s_x"]
        h = hs_ref[...]
        hb = h.astype(BF16)
        es_x = jnp.exp(s_x)
        yo = es_x * _dot(cm, hb)
        s_last = s_x[Q - 1:Q, :]
        e_x = jnp.exp(s_last - s_x)
        es_last = jnp.exp(s_last)

        yv, zz, nw_v = y_ref[...], z_ref[...], nw_ref[...]
        sg = _sigmoid(zz)
        gz = zz * sg
        _, n, rstd = _nrm(yv * gz, nw_v)
        dout = dys_ref[...]
        dyg, dnw = _nrm_bwd(dout, n, rstd, nw_v)
        dnw_ref[...] += dnw
        dy = dyg * gz
        dz_ref[...] = (dyg * yv * (sg * (1.0 + zz * (1.0 - sg)))).astype(BF16)

        dyb = dy.astype(BF16)
        xdt_b = xdt.astype(BF16)
        dhp = dht[...]
        dhpb = dhp.astype(BF16)
        lane = lax.broadcasted_iota(jnp.int32, (Q, LANES), 1)
        sub = lax.broadcasted_iota(jnp.int32, (LANES, Q), 0)
        dxdt = jnp.zeros((Q, GROUP_X), F32)
        dg = jnp.zeros((Q, Q), F32)
        ds = jnp.zeros((Q, LANES), F32)
        ds_t = jnp.zeros((LANES, Q), F32)
        for r in range(HEADS_PER_GROUP):
            dec = _decay(v, r, causal)
            mf = v["g"] * dec
            dyr = jnp.where(lane_head == r, dyb, jnp.zeros_like(dyb))
            dm = _dot(dyr, xdt_b, _NT)
            dxdt = dxdt + _dot(mf.astype(BF16), dyr, _TN)
            dg = dg + dm * dec
            dd = dm * mf
            ds = ds + jnp.where(lane == r, jnp.sum(dd, axis=1, keepdims=True), 0.0)
            ds_t = ds_t + jnp.where(sub == r, jnp.sum(dd, axis=0, keepdims=True), 0.0)
        ds = ds - ds_t.T
        dgb = dg.astype(BF16)
        dwb = (es_x * dy).astype(BF16)
        dcm = _dot(dgb, bm) + _dot(dwb, hb, _NT)
        dh_prev = _dot(cm, dwb, _TN)
        zst = _dot(bm, dhpb)
        xe = xdt * e_x
        dxdt = dxdt + e_x * zst
        dee = xe * zst
        dbm = _dot(dgb, cm, _TN) + _dot(xe.astype(BF16), dhpb, _NT)
        v_last = jnp.sum(dee, axis=0, keepdims=True) + es_last * jnp.sum(dhp * h, axis=0, keepdims=True)
        row_x = lax.broadcasted_iota(jnp.int32, (Q, GROUP_X), 0)
        tx = dy * yo - dee + jnp.where(row_x == Q - 1, v_last, 0.0)
        ds = ds + _dot_r01(tx, reduce)
        ddta = _dot_l01(triu, ds)
        ddt = ddta * v["a_neg"] + _dot_r01(dxdt * x, reduce)
        dalog = jnp.sum(ddta * v["dt"], axis=0, keepdims=True) * v["a_neg"]
        draw = jnp.where(lane < HEADS_PER_GROUP, ddt * _sigmoid(v["dtr"]), 0.0)
        dbias = jnp.sum(draw, axis=0, keepdims=True)
        ddsk = _dot_r01(jnp.broadcast_to(jnp.sum(dy * x, axis=0, keepdims=True), (8, GROUP_X)), reduce)[0:1, :]
        dht[...] = es_last * dhp + dh_prev
        dxc_ref[:, :GROUP_X] = dxdt * v["dt_x"] + v["dsk_x"] * dy
        dxc_ref[:, GROUP_X:GROUP_X + D_STATE] = dbm
        dxc_ref[:, GROUP_X + D_STATE:] = dcm
        ddt_ref[...] = draw
        dpar_ref[0:1, :] += dbias
        dpar_ref[1:2, :] += dalog
        dpar_ref[2:3, :] += ddsk

    return pl.pallas_call(
        body,
        grid=(SSM_GROUPS, n_chunks),
        in_specs=[xc_s, gx_s, dt_s, par_s, nw_s, gx_s, hs_s, gx_s],
        out_specs=[xc_s, gx_s, dt_s, par_s, nw_s],
        out_shape=[jax.ShapeDtypeStruct((s_dim, SSM_GROUPS * GROUP_COLS), F32),
                   jax.ShapeDtypeStruct((s_dim, D_SSM), BF16),
                   jax.ShapeDtypeStruct((SSM_GROUPS, s_dim, LANES), F32),
                   jax.ShapeDtypeStruct((SSM_GROUPS, 8, LANES), F32),
                   jax.ShapeDtypeStruct((1, D_SSM), F32)],
        scratch_shapes=[pltpu.VMEM((D_STATE, GROUP_X), F32)],
        compiler_params=_cparams("parallel", "arbitrary"),
        name="ssd_bwd",
    )(xc, z, dtg, par, nw, y, hs, dymix)


ATT_SCALE = ATT_HEAD_DIM ** -0.5
NEG_INF = -jnp.inf


def _head(h):
    return slice(h * ATT_HEAD_DIM, (h + 1) * ATT_HEAD_DIM)


def _band_masks():
    qi = lax.broadcasted_iota(jnp.int32, (ATT_BLOCK, ATT_BLOCK), 0)
    kj = lax.broadcasted_iota(jnp.int32, (ATT_BLOCK, ATT_BLOCK), 1)
    return kj <= qi, kj >= qi


def _attn_fwd(qkv_v, d):
    rows = qkv_v.shape[0]
    nb = rows // ATT_BLOCK
    blk = (ATT_BLOCK, D_ATT)
    prev = lambda i: jnp.maximum(i - 1, 0)

    def body(q_ref, kc_ref, kp_ref, vc_ref, vp_ref, o_ref, lse_ref):
        own, before = _band_masks()
        before = before & (pl.program_id(1) > 0)
        lane = lax.broadcasted_iota(jnp.int32, (ATT_BLOCK, LANES), 1)
        lse_all = jnp.zeros((ATT_BLOCK, LANES), F32)
        for h in range(ATT_HEADS):
            q = q_ref[:, _head(h)]
            sc = jnp.where(own, _dot(q, kc_ref[:, _head(h)], _NT) * ATT_SCALE, NEG_INF)
            sp = jnp.where(before, _dot(q, kp_ref[:, _head(h)], _NT) * ATT_SCALE, NEG_INF)
            m = jnp.maximum(jnp.max(sc, axis=1, keepdims=True), jnp.max(sp, axis=1, keepdims=True))
            pc, pp = jnp.exp(sc - m), jnp.exp(sp - m)
            den = jnp.sum(pc, axis=1, keepdims=True) + jnp.sum(pp, axis=1, keepdims=True)
            o = _dot(pc.astype(BF16), vc_ref[:, _head(h)]) + _dot(pp.astype(BF16), vp_ref[:, _head(h)])
            o_ref[:, _head(h)] = o / den
            lse_all = jnp.where(lane == h, m + jnp.log(den), lse_all)
        lse_ref[...] = lse_all

    return pl.pallas_call(
        body,
        grid=(d, nb),
        in_specs=[pl.BlockSpec(blk, lambda r, i: (i, 3 * r)),
                  pl.BlockSpec(blk, lambda r, i: (i, 3 * r + 1)),
                  pl.BlockSpec(blk, lambda r, i: (prev(i), 3 * r + 1)),
                  pl.BlockSpec(blk, lambda r, i: (i, 3 * r + 2)),
                  pl.BlockSpec(blk, lambda r, i: (prev(i), 3 * r + 2))],
        out_specs=[pl.BlockSpec(blk, lambda r, i: (i, r)), pl.BlockSpec((ATT_BLOCK, LANES), lambda r, i: (i, r))],
        out_shape=[jax.ShapeDtypeStruct((rows, d * D_ATT), F32), jax.ShapeDtypeStruct((rows, d * LANES), F32)],
        compiler_params=_cparams("parallel", "arbitrary"),
        name=f"attn_fwd_d{d}",
    )(qkv_v, qkv_v, qkv_v, qkv_v, qkv_v)


def _attn_combine(os_, lses):
    def fn(o1, o2, o3, l1, l2, l3):
        m = jnp.maximum(jnp.maximum(l1, l2), l3)
        tot = m + jnp.log(jnp.exp(l1 - m) + jnp.exp(l2 - m) + jnp.exp(l3 - m))
        w1, w2, w3 = jnp.exp(l1 - tot), jnp.exp(l2 - tot), jnp.exp(l3 - tot)
        cols = []
        for h in range(ATT_HEADS):
            cols.append(w1[:, h:h + 1] * o1[:, _head(h)] + w2[:, h:h + 1] * o2[:, _head(h)]
                        + w3[:, h:h + 1] * o3[:, _head(h)])
        y = jnp.concatenate(cols, axis=1)
        return y, y, tot
    return _rowcall(fn, list(os_) + list(lses), [], [(D_ATT, BF16), (D_ATT, F32), (LANES, F32)], [],
                    name="attn_combine", tr=128)


def _attn_delta(dymix, y_att):
    def fn(dy, y):
        lane = lax.broadcasted_iota(jnp.int32, (dy.shape[0], LANES), 1)
        delta = jnp.zeros((dy.shape[0], LANES), F32)
        for h in range(ATT_HEADS):
            delta = jnp.where(lane == h, jnp.sum(dy[:, _head(h)] * y[:, _head(h)], axis=1, keepdims=True), delta)
        return dy, delta
    return _rowcall(fn, [dymix, y_att], [], [(D_ATT, BF16), (LANES, F32)], [], name="attn_delta",
                    row_cols=[(D_ATT, 1), None])


def _attn_bwd(qkv_v, dy_v, lse_v, delta_v, d):
    rows = qkv_v.shape[0]
    nb = rows // ATT_BLOCK
    blk = (ATT_BLOCK, D_ATT)
    sblk = (ATT_BLOCK, LANES)
    prev = lambda i: jnp.maximum(i - 1, 0)
    nxt = lambda i: jnp.minimum(i + 1, nb - 1)

    def body(qc_ref, qn_ref, kc_ref, kp_ref, vc_ref, vp_ref, dyc_ref, dyn_ref, lc_ref, ln_ref, dc_ref, dn_ref,
             dq_ref, dk_ref, dv_ref):
        i = pl.program_id(1)
        own, before = _band_masks()
        before_c = before & (i > 0)
        before_n = before & (i < nb - 1)
        lc, ln, dc, dn = lc_ref[...], ln_ref[...], dc_ref[...], dn_ref[...]
        for h in range(ATT_HEADS):
            hs = _head(h)
            q, qn, kc, kp, vc, vp = qc_ref[:, hs], qn_ref[:, hs], kc_ref[:, hs], kp_ref[:, hs], vc_ref[:, hs], vp_ref[:, hs]
            dy, dyn = dyc_ref[:, hs], dyn_ref[:, hs]
            lse, lse_n, dl, dl_n = lc[:, h:h + 1], ln[:, h:h + 1], dc[:, h:h + 1], dn[:, h:h + 1]
            pc = jnp.exp(jnp.where(own, _dot(q, kc, _NT) * ATT_SCALE - lse, NEG_INF))
            pp = jnp.exp(jnp.where(before_c, _dot(q, kp, _NT) * ATT_SCALE - lse, NEG_INF))
            pn = jnp.exp(jnp.where(before_n, _dot(qn, kc, _NT) * ATT_SCALE - lse_n, NEG_INF))
            dsc = (pc * (_dot(dy, vc, _NT) - dl)).astype(BF16)
            dsp = (pp * (_dot(dy, vp, _NT) - dl)).astype(BF16)
            dsn = (pn * (_dot(dyn, vc, _NT) - dl_n)).astype(BF16)
            dq_ref[:, hs] = (_dot(dsc, kc) + _dot(dsp, kp)) * ATT_SCALE
            dk_ref[:, hs] = (_dot(dsc, q, _TN) + _dot(dsn, qn, _TN)) * ATT_SCALE
            dv_ref[:, hs] = _dot(pc.astype(BF16), dy, _TN) + _dot(pn.astype(BF16), dyn, _TN)

    return pl.pallas_call(
        body,
        grid=(d, nb),
        in_specs=[pl.BlockSpec(blk, lambda r, i: (i, 3 * r)), pl.BlockSpec(blk, lambda r, i: (nxt(i), 3 * r)),
                  pl.BlockSpec(blk, lambda r, i: (i, 3 * r + 1)), pl.BlockSpec(blk, lambda r, i: (prev(i), 3 * r + 1)),
                  pl.BlockSpec(blk, lambda r, i: (i, 3 * r + 2)), pl.BlockSpec(blk, lambda r, i: (prev(i), 3 * r + 2)),
                  pl.BlockSpec(blk, lambda r, i: (i, r)), pl.BlockSpec(blk, lambda r, i: (nxt(i), r)),
                  pl.BlockSpec(sblk, lambda r, i: (i, r)), pl.BlockSpec(sblk, lambda r, i: (nxt(i), r)),
                  pl.BlockSpec(sblk, lambda r, i: (i, r)), pl.BlockSpec(sblk, lambda r, i: (nxt(i), r))],
        out_specs=[pl.BlockSpec(blk, lambda r, i: (i, r))] * 3,
        out_shape=[jax.ShapeDtypeStruct((rows, d * D_ATT), F32)] * 3,
        compiler_params=_cparams("parallel", "arbitrary"),
        name=f"attn_bwd_d{d}",
    )(qkv_v, qkv_v, qkv_v, qkv_v, qkv_v, qkv_v, dy_v, dy_v, lse_v, lse_v, delta_v, delta_v)


def _attn_sum(dqs, dks, dvs, deps=()):
    def fn(*parts):
        return (jnp.concatenate([parts[0] + parts[1] + parts[2], parts[3] + parts[4] + parts[5],
                                 parts[6] + parts[7] + parts[8]], axis=1),)
    return _rowcall(fn, list(dqs) + list(dks) + list(dvs), [], [(3 * D_ATT, BF16)], [], name="attn_sum", tr=128,
                    deps=deps)[0]


def _attention_fwd(qkv):
    s_dim = qkv.shape[0]
    os_, lses = [], []
    for d in DILATIONS:
        o, lse = _attn_fwd(qkv.reshape(s_dim // d, d * 3 * D_ATT), d)
        os_.append(o.reshape(s_dim, D_ATT))
        lses.append(lse.reshape(s_dim, LANES))
    return _attn_combine(os_, lses)


def _attention_bwd(qkv, dymix, y_att, lse, sum_deps=()):
    s_dim = qkv.shape[0]
    dy, delta = _attn_delta(dymix, y_att)
    dqs, dks, dvs = [], [], []
    for d in DILATIONS:
        dq, dk, dv = _attn_bwd(qkv.reshape(s_dim // d, d * 3 * D_ATT), dy.reshape(s_dim // d, d * D_ATT),
                               lse.reshape(s_dim // d, d * LANES), delta.reshape(s_dim // d, d * LANES), d)
        dqs.append(dq.reshape(s_dim, D_ATT))
        dks.append(dk.reshape(s_dim, D_ATT))
        dvs.append(dv.reshape(s_dim, D_ATT))
    return _attn_sum(dqs, dks, dvs, sum_deps)


WIN = ATT_BLOCK * DILATIONS[-1]
N_BLOCKS = WIN // ATT_BLOCK


def _rows(start, d):
    return pl.ds(start, ATT_BLOCK) if d == 1 else pl.ds(start, ATT_BLOCK, stride=d)


def _block_start(idx, d):
    return (idx // d) * (ATT_BLOCK * d) + idx % d


def _lane_bcast(col):
    return jnp.broadcast_to(col, (col.shape[0], LANES))


def _attn_fused_fwd(qkv):
    s_dim = qkv.shape[0]
    n_win = s_dim // WIN
    blk = (WIN, ATT_HEAD_DIM)
    prev = lambda w: jnp.maximum(w - 1, 0)

    def body(q_ref, kc_ref, kp_ref, vc_ref, vp_ref, y_ref, yf_ref, lse_ref, qf, kf, vf, acc, m_run, l_run):
        w, h = pl.program_id(0), pl.program_id(1)
        qf[...] = q_ref[...].astype(F32)
        kf[0:WIN, :] = kp_ref[...].astype(F32)
        kf[WIN:, :] = kc_ref[...].astype(F32)
        vf[0:WIN, :] = vp_ref[...].astype(F32)
        vf[WIN:, :] = vc_ref[...].astype(F32)
        own, before = _band_masks()

        for d in DILATIONS:
            def block(idx, carry, d=d):
                start = _block_start(idx, d)
                rows = _rows(start, d)
                q = qf[rows, :].astype(BF16)
                kc, vc = kf[_rows(WIN + start, d), :].astype(BF16), vf[_rows(WIN + start, d), :].astype(BF16)
                kp = kf[_rows(WIN + start - ATT_BLOCK * d, d), :].astype(BF16)
                vp = vf[_rows(WIN + start - ATT_BLOCK * d, d), :].astype(BF16)
                has_prev = (idx >= d) | (w > 0)
                sc = jnp.where(own, _dot(q, kc, _NT) * ATT_SCALE, NEG_INF)
                sp = jnp.where(before & has_prev, _dot(q, kp, _NT) * ATT_SCALE, NEG_INF)
                m_blk = jnp.maximum(jnp.max(sc, axis=1, keepdims=True), jnp.max(sp, axis=1, keepdims=True))
                if d == DILATIONS[0]:
                    m_new = m_blk
                else:
                    m_old = m_run[rows, :][:, 0:1]
                    m_new = jnp.maximum(m_old, m_blk)
                pc, pp = jnp.exp(sc - m_new), jnp.exp(sp - m_new)
                l_new = jnp.sum(pc, axis=1, keepdims=True) + jnp.sum(pp, axis=1, keepdims=True)
                o_new = _dot(pc.astype(BF16), vc) + _dot(pp.astype(BF16), vp)
                if d != DILATIONS[0]:
                    alpha = jnp.exp(m_old - m_new)
                    l_new = alpha * l_run[rows, :][:, 0:1] + l_new
                    o_new = alpha * acc[rows, :] + o_new
                m_run[rows, :] = _lane_bcast(m_new)
                l_run[rows, :] = _lane_bcast(l_new)
                acc[rows, :] = o_new
                return carry

            for idx in range(N_BLOCKS):
                block(idx, 0)

        l_all = l_run[...]
        y = acc[...] / l_all
        y_ref[...] = y.astype(BF16)
        yf_ref[...] = y
        @pl.when(h == 0)
        def _():
            lse_ref[...] = jnp.zeros_like(lse_ref)

        lane = lax.broadcasted_iota(jnp.int32, (WIN, LANES), 1)
        lse_ref[...] = jnp.where(lane == h, m_run[...] + jnp.log(l_all), lse_ref[...])

    win_scratch = lambda rows: pltpu.VMEM((rows, ATT_HEAD_DIM), F32)
    return pl.pallas_call(
        body,
        grid=(n_win, ATT_HEADS),
        in_specs=[pl.BlockSpec(blk, lambda w, h: (w, h)),
                  pl.BlockSpec(blk, lambda w, h: (w, ATT_HEADS + h)),
                  pl.BlockSpec(blk, lambda w, h: (prev(w), ATT_HEADS + h)),
                  pl.BlockSpec(blk, lambda w, h: (w, 2 * ATT_HEADS + h)),
                  pl.BlockSpec(blk, lambda w, h: (prev(w), 2 * ATT_HEADS + h))],
        out_specs=[pl.BlockSpec(blk, lambda w, h: (w, h)), pl.BlockSpec(blk, lambda w, h: (w, h)),
                   pl.BlockSpec((WIN, LANES), lambda w, h: (w, 0))],
        out_shape=[jax.ShapeDtypeStruct((s_dim, D_ATT), BF16), jax.ShapeDtypeStruct((s_dim, D_ATT), F32),
                   jax.ShapeDtypeStruct((s_dim, LANES), F32)],
        scratch_shapes=[win_scratch(WIN), win_scratch(2 * WIN), win_scratch(2 * WIN), win_scratch(WIN),
                        win_scratch(WIN), win_scratch(WIN)],
        compiler_params=_cparams("parallel", "arbitrary"),
        name="attn_fused_fwd",
    )(qkv, qkv, qkv, qkv, qkv)


def _attn_fused_bwd(qkv, dymix, y_att, lse, deps=()):
    s_dim = qkv.shape[0]
    n_win = s_dim // WIN
    blk = (WIN, ATT_HEAD_DIM)
    prev = lambda w: jnp.maximum(w - 1, 0)
    nxt = lambda w: jnp.minimum(w + 1, n_win - 1)
    n_dep = len(deps)

    def body(qc_ref, qn_ref, kc_ref, kp_ref, vc_ref, vp_ref, dyc_ref, dyn_ref, yc_ref, yn_ref, lc_ref, ln_ref, *rest):
        out_ref = rest[n_dep]
        qf, qnf, kf, vf, dq_acc, dk_acc, dv_acc, ls_c, dl_c, ls_n, dl_n = rest[n_dep + 1:]
        w, h = pl.program_id(0), pl.program_id(1)
        qf[...] = qc_ref[...].astype(F32)
        qnf[...] = qn_ref[...].astype(F32)
        kf[0:WIN, :] = kp_ref[...].astype(F32)
        kf[WIN:, :] = kc_ref[...].astype(F32)
        vf[0:WIN, :] = vp_ref[...].astype(F32)
        vf[WIN:, :] = vc_ref[...].astype(F32)
        lane = lax.broadcasted_iota(jnp.int32, (WIN, LANES), 1)
        pick = lambda ref: _lane_bcast(jnp.sum(jnp.where(lane == h, ref[...], 0.0), axis=1, keepdims=True))
        ls_c[...] = pick(lc_ref)
        ls_n[...] = pick(ln_ref)
        dl_c[...] = _lane_bcast(jnp.sum(dyc_ref[...] * yc_ref[...], axis=1, keepdims=True))
        dl_n[...] = _lane_bcast(jnp.sum(dyn_ref[...] * yn_ref[...], axis=1, keepdims=True))
        for ref in (dq_acc, dk_acc, dv_acc):
            ref[...] = jnp.zeros_like(ref)
        own, before = _band_masks()

        def probs(q, k, v, dy, lse_col, dl_col, mask):
            p = jnp.exp(jnp.where(mask, _dot(q, k, _NT) * ATT_SCALE - lse_col, NEG_INF))
            ds = p * (_dot(dy, v, _NT) - dl_col)
            return p.astype(BF16), ds.astype(BF16)

        for d in DILATIONS:
            def block(idx, carry, d=d):
                start = _block_start(idx, d)
                rows = _rows(start, d)
                q, dy = qf[rows, :].astype(BF16), dyc_ref[rows, :].astype(BF16)
                lse_col, dl_col = ls_c[rows, :][:, 0:1], dl_c[rows, :][:, 0:1]
                kc, vc = kf[_rows(WIN + start, d), :].astype(BF16), vf[_rows(WIN + start, d), :].astype(BF16)
                kp = kf[_rows(WIN + start - ATT_BLOCK * d, d), :].astype(BF16)
                vp = vf[_rows(WIN + start - ATT_BLOCK * d, d), :].astype(BF16)
                pc, dsc = probs(q, kc, vc, dy, lse_col, dl_col, own)
                pp, dsp = probs(q, kp, vp, dy, lse_col, dl_col, before & ((idx >= d) | (w > 0)))
                dq_acc[rows, :] += (_dot(dsc, kc) + _dot(dsp, kp)) * ATT_SCALE
                dk_acc[rows, :] += _dot(dsc, q, _TN) * ATT_SCALE
                dv_acc[rows, :] += _dot(pc, dy, _TN)

                if idx >= d:
                    prows = _rows(start - ATT_BLOCK * d, d)
                    dk_acc[prows, :] += _dot(dsp, q, _TN) * ATT_SCALE
                    dv_acc[prows, :] += _dot(pp, dy, _TN)
                return carry

            for idx in range(N_BLOCKS):
                block(idx, 0)

            def next_window(r, carry, d=d):
                krows = _rows(WIN - ATT_BLOCK * d + r, d)
                rows = _rows(r, d)
                q, dy = qnf[rows, :].astype(BF16), dyn_ref[rows, :].astype(BF16)
                k, v = kf[_rows(2 * WIN - ATT_BLOCK * d + r, d), :].astype(BF16), vf[_rows(2 * WIN - ATT_BLOCK * d + r, d), :].astype(BF16)
                pn, dsn = probs(q, k, v, dy, ls_n[rows, :][:, 0:1], dl_n[rows, :][:, 0:1], before & (w < n_win - 1))
                dk_acc[krows, :] += _dot(dsn, q, _TN) * ATT_SCALE
                dv_acc[krows, :] += _dot(pn, dy, _TN)
                return carry

            for r in range(d):
                next_window(r, 0)

        for part, acc_ref in enumerate((dq_acc, dk_acc, dv_acc)):
            out_ref[part] = acc_ref[...].astype(BF16)

    win_scratch = lambda rows: pltpu.VMEM((rows, ATT_HEAD_DIM), F32)
    cur = lambda c: pl.BlockSpec(blk, lambda w, h: (w, c + h))
    return pl.pallas_call(
        body,
        grid=(n_win, ATT_HEADS),
        in_specs=[cur(0), pl.BlockSpec(blk, lambda w, h: (nxt(w), h)),
                  cur(ATT_HEADS), pl.BlockSpec(blk, lambda w, h: (prev(w), ATT_HEADS + h)),
                  cur(2 * ATT_HEADS), pl.BlockSpec(blk, lambda w, h: (prev(w), 2 * ATT_HEADS + h)),
                  cur(ATT_HEADS), pl.BlockSpec(blk, lambda w, h: (nxt(w), ATT_HEADS + h)),
                  cur(0), pl.BlockSpec(blk, lambda w, h: (nxt(w), h)),
                  pl.BlockSpec((WIN, LANES), lambda w, h: (w, 0)), pl.BlockSpec((WIN, LANES), lambda w, h: (nxt(w), 0))]
        + [ANY] * n_dep,
        out_specs=pl.BlockSpec((3, WIN, ATT_HEAD_DIM), lambda w, h: (0, w, h)),
        out_shape=jax.ShapeDtypeStruct((3, s_dim, D_ATT), BF16),
        scratch_shapes=[win_scratch(WIN), win_scratch(WIN), win_scratch(2 * WIN), win_scratch(2 * WIN)]
        + [win_scratch(WIN)] * 7,
        compiler_params=_cparams("parallel", "arbitrary"),
        name="attn_fused_bwd",
    )(qkv, qkv, qkv, qkv, qkv, qkv, dymix, dymix, y_att, y_att, lse, lse, *deps)


def _adamw(w, g, m, v, name):
    def fn(wb, gb, mb, vb):
        m2 = ADAM_B1 * mb + (1.0 - ADAM_B1) * gb
        v2 = ADAM_B2 * vb + (1.0 - ADAM_B2) * (gb * gb)
        m_hat = m2 / (1.0 - ADAM_B1 ** ADAM_STEP)
        v_hat = v2 / (1.0 - ADAM_B2 ** ADAM_STEP)
        delta = -ADAM_LR * (m_hat / (jnp.sqrt(v_hat) + ADAM_EPS) + ADAM_WD * wb)
        return delta, m2, v2
    cols = w.shape[1]
    tr = 128 if w.shape[0] % 128 == 0 else w.shape[0]
    return _rowcall(fn, [w, g, m, v], [], [(cols, F32)] * 3, [], name=name, tr=tr)


ANY = pl.BlockSpec(memory_space=pl.ANY)


def _position():
    x, y, c = lax.axis_index("x"), lax.axis_index("y"), lax.axis_index("c")
    chips = [(1 - x, y), (x, 1 - y), (1 - x, 1 - y)]
    return x, y, c, chips


def _remote(src, dst, send_sem, recv_sem, device):
    return pltpu.make_async_remote_copy(src_ref=src, dst_ref=dst, send_sem=send_sem, recv_sem=recv_sem,
                                        device_id=device, device_id_type=MESH)


def _gather_shards(shards):
    n = len(shards)

    def body(*refs):
        ins, outs = refs[:n], refs[n:2 * n]
        send_sems, recv_sems = refs[2 * n:]
        x, y, c, chips = _position()
        sibling = (x, y, 1 - c)

        def half(a, j, cc):
            h = ins[a].shape[0] // 2
            return outs[a].at[j, pl.ds(cc * h, h), :]

        sent = []
        for a in range(n):
            h = ins[a].shape[0] // 2
            for j, chip in enumerate(chips):
                cp = _remote(ins[a].at[pl.ds(c * h, h), :], half(a, 2 * x + y, c), send_sems.at[6 * a + j],
                             recv_sems.at[6 * a + j], (chip[0], chip[1], c))
                cp.start()
                sent.append(cp)
        for a in range(n):
            for j, chip in enumerate(chips):
                landed = half(a, 2 * chip[0] + chip[1], c)
                _remote(landed, landed, send_sems.at[6 * a + j], recv_sems.at[6 * a + j], (x, y, c)).wait_recv()
                cp = _remote(landed, landed, send_sems.at[6 * a + 3 + j], recv_sems.at[6 * a + 3 + j], sibling)
                cp.start()
                sent.append(cp)
        for a in range(n):
            for j, chip in enumerate(chips):
                handed = half(a, 2 * chip[0] + chip[1], 1 - c)
                _remote(handed, handed, send_sems.at[6 * a + 3 + j], recv_sems.at[6 * a + 3 + j], (x, y, c)).wait_recv()
        for cp in sent:
            cp.wait_send()

    return pl.pallas_call(
        body,
        in_specs=[ANY] * n,
        out_specs=[ANY] * n,
        out_shape=[jax.ShapeDtypeStruct((N_CHIPS,) + s.shape, s.dtype) for s in shards],
        scratch_shapes=[pltpu.SemaphoreType.DMA((6 * n,)), pltpu.SemaphoreType.DMA((6 * n,))],
        name="gather_shards",
    )(*shards)


def _handshake(peers):
    barrier = pltpu.get_barrier_semaphore()
    for p in peers:
        pl.semaphore_signal(barrier, inc=1, device_id=p, device_id_type=MESH)
    pl.semaphore_wait(barrier, len(peers))


def _gather_shards_async(shards, collective_id, name):
    n = len(shards)
    srcs = [jax.new_ref(s, memory_space=pltpu.MemorySpace.HBM) for s in shards]
    dsts = [jax.empty_ref(jax.ShapeDtypeStruct((N_CHIPS,) + s.shape, s.dtype), memory_space=pltpu.MemorySpace.HBM)
            for s in shards]

    @pl.kernel(mesh=plsc.ScalarSubcoreMesh(axis_name="seq", num_cores=1), name=name,
               scratch_types=(pltpu.SemaphoreType.DMA((6 * n,)), pltpu.SemaphoreType.DMA((6 * n,))),
               compiler_params=pltpu.CompilerParams(collective_id=collective_id))
    def launch(send_sems, recv_sems):
        x, y, c, chips = _position()
        sibling = (x, y, 1 - c)
        _handshake([(chip[0], chip[1], c) for chip in chips] + [sibling])

        def half(a, j, cc):
            h = shards[a].shape[0] // 2
            return dsts[a].at[j, pl.ds(cc * h, h), :]

        sent = []
        for a in range(n):
            h = shards[a].shape[0] // 2
            for j, chip in enumerate(chips):
                cp = _remote(srcs[a].at[pl.ds(c * h, h), :], half(a, 2 * x + y, c), send_sems.at[6 * a + j],
                             recv_sems.at[6 * a + j], (chip[0], chip[1], c))
                cp.start()
                sent.append(cp)
        for a in range(n):
            for j, chip in enumerate(chips):
                landed = half(a, 2 * chip[0] + chip[1], c)
                _remote(landed, landed, send_sems.at[6 * a + j], recv_sems.at[6 * a + j], (x, y, c)).wait_recv()
                cp = _remote(landed, landed, send_sems.at[6 * a + 3 + j], recv_sems.at[6 * a + 3 + j], sibling)
                cp.start()
                sent.append(cp)
        for a in range(n):
            for j, chip in enumerate(chips):
                handed = half(a, 2 * chip[0] + chip[1], 1 - c)
                _remote(handed, handed, send_sems.at[6 * a + 3 + j], recv_sems.at[6 * a + 3 + j], (x, y, c)).wait_recv()
        for cp in sent:
            cp.wait_send()

    launch()
    return [d[...] for d in dsts]


IN_COLS = {"z": (0, D_SSM), "xbc": (D_SSM, D_SSM + D_XBC), "dt": (D_SSM + D_XBC, D_SSM + D_XBC + SSM_HEADS),
           "qkv": (D_SSM + D_XBC + SSM_HEADS, D_IN_PROJ)}


def _cols_from_quarters(quarters, lo, hi):
    parts = []
    for q in range(N_CHIPS):
        a, b = max(lo, q * W_IN_SHARD), min(hi, (q + 1) * W_IN_SHARD)
        if a < b:
            parts.append(quarters[q][:, a - q * W_IN_SHARD:b - q * W_IN_SHARD])
    return parts[0] if len(parts) == 1 else jnp.concatenate(parts, axis=1)


def _quarters_from_cols(pieces):
    quarters = []
    for q in range(N_CHIPS):
        parts = []
        for name, (lo, hi) in IN_COLS.items():
            a, b = max(lo, q * W_IN_SHARD), min(hi, (q + 1) * W_IN_SHARD)
            if a < b:
                parts.append(pieces[name][:, a - lo:b - lo])
        quarters.append(jnp.concatenate(parts, axis=1))
    return jnp.stack(quarters)


def _by_chip(own, fetched):
    me = 2 * lax.axis_index("x") + lax.axis_index("y")
    return lax.dynamic_update_slice(fetched, own[None], (me, 0, 0))


def _add_sibling(grad, got, c_arr, name, deps=()):
    nq, rows, cols = grad.shape
    h = rows // 2
    tr = 128
    nb = h // tr

    def body(c_ref, a_ref, b_ref, *rest):
        o_ref, ob_ref = rest[len(deps):]
        total = a_ref[...] + b_ref[...]
        o_ref[...] = total
        ob_ref[...] = total.astype(BF16)

    out_spec = pl.BlockSpec((None, tr, cols), lambda q, i, c: (q, i, 0))
    return pl.pallas_call(
        body,
        grid_spec=pltpu.PrefetchScalarGridSpec(
            num_scalar_prefetch=1, grid=(nq, nb),
            in_specs=[pl.BlockSpec((None, tr, cols), lambda q, i, c: (q, c[0] * nb + i, 0)),
                      pl.BlockSpec((None, tr, cols), lambda q, i, c: (q, i, 0))] + [ANY] * len(deps),
            out_specs=[out_spec, out_spec]),
        out_shape=[jax.ShapeDtypeStruct((nq, h, cols), F32), jax.ShapeDtypeStruct((nq, h, cols), BF16)],
        compiler_params=_cparams("parallel", "parallel"),
        name=name,
    )(c_arr, grad, got, *deps)


def _add_chips(part, got, chip_arr, name, deps=()):
    _, h, cols = part.shape
    tr = 128

    def body(q_ref, p_ref, g0_ref, g1_ref, g2_ref, *rest):
        o_ref = rest[len(deps)]
        o_ref[...] = ((p_ref[...] + g0_ref[...].astype(F32)) + g1_ref[...].astype(F32)) + g2_ref[...].astype(F32)

    got_spec = lambda j: pl.BlockSpec((None, tr, cols), lambda i, q: (j, i, 0))
    return pl.pallas_call(
        body,
        grid_spec=pltpu.PrefetchScalarGridSpec(
            num_scalar_prefetch=1, grid=(h // tr,),
            in_specs=[pl.BlockSpec((None, tr, cols), lambda i, q: (q[0], i, 0)), got_spec(0), got_spec(1), got_spec(2)]
            + [ANY] * len(deps),
            out_specs=pl.BlockSpec((tr, cols), lambda i, q: (i, 0))),
        out_shape=jax.ShapeDtypeStruct((h, cols), F32),
        compiler_params=_cparams("parallel"),
        name=name,
    )(chip_arr, part, got, got, got, *deps)


def _sequencer_exchange(src, out_shape, collective_id, name, plan, n_copies):
    src_ref = jax.new_ref(src, memory_space=pltpu.MemorySpace.HBM)
    dst_ref = jax.empty_ref(out_shape, memory_space=pltpu.MemorySpace.HBM)

    @pl.kernel(mesh=plsc.ScalarSubcoreMesh(axis_name="seq", num_cores=1), name=name,
               scratch_types=(pltpu.SemaphoreType.DMA((n_copies,)), pltpu.SemaphoreType.DMA((n_copies,))),
               compiler_params=pltpu.CompilerParams(collective_id=collective_id))
    def launch(send_sems, recv_sems):
        x, y, c, chips = _position()
        copies = plan(src_ref, dst_ref, x, y, c, chips)
        _handshake([peer for _, _, peer in copies])
        started = []
        for k, (s, d, peer) in enumerate(copies):
            cp = _remote(s, d, send_sems.at[k], recv_sems.at[k], peer)
            cp.start()
            started.append(cp)
        for cp in started:
            cp.wait()

    launch()
    return dst_ref[...]


class _AsyncReduceScatter:
    def __init__(self, grad, nm, first_id):
        self.grad, self.nm, self.first_id = grad, nm, first_id
        nq, rows, cols = grad.shape
        h = self.h = rows // 2

        def to_sibling(s, d, x, y, c, chips):
            return [(s.at[:, pl.ds((1 - c) * h, h), :], d, (x, y, 1 - c))]

        self.from_sibling = _sequencer_exchange(grad, jax.ShapeDtypeStruct((nq, h, cols), F32), first_id,
                                                f"rs_sibling_{nm}", to_sibling, 1)

    def sibling_sum(self, not_before=()):
        cols = self.grad.shape[2]
        c_arr = lax.axis_index("c").astype(jnp.int32).reshape(1)
        self.part, self.part_b = _add_sibling(self.grad, self.from_sibling, c_arr, f"add_sibling_{self.nm}", not_before)

        def to_chips(s, d, x, y, c, chips):
            return [(s.at[2 * chip[0] + chip[1]], d.at[j], (chip[0], chip[1], c)) for j, chip in enumerate(chips)]

        self.from_chips = _sequencer_exchange(self.part_b, jax.ShapeDtypeStruct((3, self.h, cols), BF16),
                                              self.first_id + 1, f"rs_quarters_{self.nm}", to_chips, 3)
        return self.part_b

    def chip_sum(self, not_before=()):
        cols = self.grad.shape[2]
        chip_arr = (2 * lax.axis_index("x") + lax.axis_index("y")).astype(jnp.int32).reshape(1)
        self.half = _add_chips(self.part, self.from_chips, chip_arr, f"add_chips_{self.nm}", not_before)

        def whole_to_sibling(s, d, x, y, c, chips):
            return [(s, d, (x, y, 1 - c))]

        self.other = _sequencer_exchange(self.half, jax.ShapeDtypeStruct((self.h, cols), F32), self.first_id + 2,
                                         f"rs_share_{self.nm}", whole_to_sibling, 1)
        return self.half

    def share(self):
        return self.half, self.other


def _after(x, deps, name):
    def body(x_ref, *rest):
        rest[-1][...] = x_ref[...]

    vm = pl.BlockSpec(memory_space=pltpu.VMEM)
    return pl.pallas_call(body, in_specs=[vm] + [ANY] * len(deps), out_specs=vm,
                          out_shape=jax.ShapeDtypeStruct(x.shape, x.dtype), name=name)(x, *deps)


def _adamw_halves(w, mine, other, m, v, name):
    rows, cols = w.shape
    tr = 128
    nb = rows // 2 // tr
    c_arr = lax.axis_index("c").astype(jnp.int32).reshape(1)

    def body(c_ref, w_ref, a_ref, b_ref, m_ref, v_ref, g_out, d_out, m_out, v_out):
        is_mine = (pl.program_id(0) // nb) == c_ref[0]
        g = jnp.where(is_mine, a_ref[...], b_ref[...])
        wb, mb, vb = w_ref[...], m_ref[...], v_ref[...]
        m2 = ADAM_B1 * mb + (1.0 - ADAM_B1) * g
        v2 = ADAM_B2 * vb + (1.0 - ADAM_B2) * (g * g)
        m_hat = m2 / (1.0 - ADAM_B1 ** ADAM_STEP)
        v_hat = v2 / (1.0 - ADAM_B2 ** ADAM_STEP)
        g_out[...] = g
        d_out[...] = -ADAM_LR * (m_hat / (jnp.sqrt(v_hat) + ADAM_EPS) + ADAM_WD * wb)
        m_out[...] = m2
        v_out[...] = v2

    full = pl.BlockSpec((tr, cols), lambda i, c: (i, 0))
    half = pl.BlockSpec((tr, cols), lambda i, c: (i % nb, 0))
    return pl.pallas_call(
        body,
        grid_spec=pltpu.PrefetchScalarGridSpec(
            num_scalar_prefetch=1, grid=(rows // tr,),
            in_specs=[full, half, half, full, full], out_specs=[full] * 4),
        out_shape=[jax.ShapeDtypeStruct((rows, cols), F32)] * 4,
        compiler_params=_cparams("parallel"),
        name=name,
    )(c_arr, w, mine, other, m, v)


def _all_sum_small(v):
    n_dev = 8

    def body(v_ref, o_ref, gath, send_sems, recv_sems):
        x, y, c, _ = _position()
        me = 4 * x + 2 * y + c
        gath[me] = v_ref[...]
        copies = []
        for k in range(1, n_dev):
            peer = tuple(1 - p if (k >> s) & 1 else p for p, s in ((x, 2), (y, 1), (c, 0)))
            cp = _remote(v_ref, gath.at[me], send_sems.at[k - 1], recv_sems.at[k - 1], peer)
            cp.start()
            copies.append(cp)
        for cp in copies:
            cp.wait()
        acc = gath[0]
        for i in range(1, n_dev):
            acc = acc + gath[i]
        o_ref[...] = acc

    vm = pl.BlockSpec(memory_space=pltpu.VMEM)
    return pl.pallas_call(
        body,
        in_specs=[vm],
        out_specs=vm,
        out_shape=jax.ShapeDtypeStruct(v.shape, F32),
        scratch_shapes=[pltpu.VMEM((n_dev,) + v.shape, F32), pltpu.SemaphoreType.DMA((n_dev - 1,)),
                        pltpu.SemaphoreType.DMA((n_dev - 1,))],
        name="all_sum_small",
    )(v)


def _pack_rows(vectors):
    rows = []
    for v in vectors:
        flat = v.reshape(-1).astype(F32)
        rows.append(jnp.pad(flat, (0, (-flat.shape[0]) % LANES)).reshape(-1, LANES))
    out = jnp.concatenate(rows, axis=0)
    return jnp.pad(out, ((0, (-out.shape[0]) % 8), (0, 0)))


def _unpack_rows(packed, shapes):
    outs, r = [], 0
    for shp in shapes:
        size = math.prod(shp)
        nr = -(-size // LANES)
        outs.append(packed[r:r + nr].reshape(-1)[:size].reshape(shp))
        r += nr
    return outs


def _relu_sq(acc):
    r = jnp.maximum(acc, 0.0)
    return r, r * r


def _relu_sq_bwd(acc, r):
    return (acc * (2.0 * r.astype(F32)),)


def kernel(x, norm_mix_pre, w_in, conv_w, conv_b, dt_bias, a_log, d_skip, ssm_norm_w, w_out, norm_mix_post, norm_mlp_pre, w_up, w_down, norm_mlp_post, loss_target, m_norm_mix_pre, m_w_in, m_conv_w, m_conv_b, m_dt_bias, m_a_log, m_d_skip, m_ssm_norm_w, m_w_out, m_norm_mix_post, m_norm_mlp_pre, m_w_up, m_w_down, m_norm_mlp_post, v_norm_mix_pre, v_w_in, v_conv_w, v_conv_b, v_dt_bias, v_a_log, v_d_skip, v_ssm_norm_w, v_w_out, v_norm_mix_post, v_norm_mlp_pre, v_w_up, v_w_down, v_norm_mlp_post):
    s_dim = x.shape[1]
    xs, target = x[0], loss_target[0]
    chip = 2 * lax.axis_index("x") + lax.axis_index("y")

    own = [w_in[0].astype(BF16), w_out[0].astype(BF16), w_up[0].astype(BF16), w_down[0].astype(BF16)]
    fetched_in = _gather_shards(own[:1])[0]
    fetched_in, *rest = lax.optimization_barrier((fetched_in, *own[1:]))
    fetched = [fetched_in] + _gather_shards_async(rest, 1, "gather_rest")
    g_in, g_out, g_up, g_down = [_by_chip(o, f) for o, f in zip(own, fetched)]
    w_z = _cols_from_quarters(g_in, *IN_COLS["z"])
    w_xbc = _perm_cols(_cols_from_quarters(g_in, *IN_COLS["xbc"]))
    w_dt = jnp.pad(_cols_from_quarters(g_in, *IN_COLS["dt"]), ((0, 0), (0, LANES - SSM_HEADS)))
    w_qkv = _cols_from_quarters(g_in, *IN_COLS["qkv"])
    w_out_full = g_out.reshape(D_MIX, D_MODEL)
    w_down_full = g_down.reshape(D_FF, D_MODEL)

    conv_cols = D_XBC // N_CHIPS
    conv_placed = lax.dynamic_update_slice(jnp.zeros((8, D_XBC), F32), 0.5 * conv_w[0], (0, chip * conv_cols))
    conv_full = _all_sum_small(conv_placed.reshape(-1, LANES)).reshape(8, D_XBC)
    w8 = _perm_cols(conv_full.at[CONV_WIDTH].set(conv_b[0]))

    u = _pre_norm(xs, norm_mix_pre)
    z = _matmul([(u, w_z, TK)], "nn", [F32], name="proj_z")
    xbc = _matmul([(u, w_xbc, TK)], "nn", [F32], name="proj_xbc")
    dt_raw = _matmul([(u, w_dt, TK)], "nn", [F32], name="proj_dt")
    qkv = _matmul([(u, w_qkv, TK)], "nn", [BF16], name="proj_qkv")
    xc = _conv_fwd(xbc, w8)
    dtg = _dt_to_groups(dt_raw)
    par = _pack_ssd_params(dt_bias[0], a_log[0], d_skip[0])
    y, y_ssm, states = _ssd_fwd(xc, z, dtg, par, ssm_norm_w)
    y_att, y_att_f32, lse = _attn_fused_fwd(qkv)
    y_mix = jnp.concatenate([y_ssm, y_att], axis=1)
    mix = _matmul([(y_mix, w_out_full, TK)], "nn", [F32], name="out_proj")
    h1, u2 = _post_pre_norm(xs, mix, norm_mix_post, norm_mlp_pre)
    hid, act = _matmul([(u2, g_up, TK)], "nn", [BF16, BF16], name="mlp_up", epilogue=_relu_sq)
    ff = _matmul([(act, w_down_full, TK)], "nn", [F32], name="mlp_down")
    dh2, dff, d_g4, loss_part = _tail(ff, h1, target, norm_mlp_post)

    dhid = _matmul([(dff, w_down_full, TK)], "nt", [BF16], name="mlp_down_dx", epilogue=_relu_sq_bwd, extras=[hid])
    weights = {"norm_mix_pre": (norm_mix_pre, m_norm_mix_pre, v_norm_mix_pre), "w_in": (w_in, m_w_in, v_w_in),
               "conv_w": (conv_w, m_conv_w, v_conv_w), "conv_b": (conv_b, m_conv_b, v_conv_b),
               "dt_bias": (dt_bias, m_dt_bias, v_dt_bias), "a_log": (a_log, m_a_log, v_a_log),
               "d_skip": (d_skip, m_d_skip, v_d_skip), "ssm_norm_w": (ssm_norm_w, m_ssm_norm_w, v_ssm_norm_w),
               "w_out": (w_out, m_w_out, v_w_out), "norm_mix_post": (norm_mix_post, m_norm_mix_post, v_norm_mix_post),
               "norm_mlp_pre": (norm_mlp_pre, m_norm_mlp_pre, v_norm_mlp_pre), "w_up": (w_up, m_w_up, v_w_up),
               "w_down": (w_down, m_w_down, v_w_down),
               "norm_mlp_post": (norm_mlp_post, m_norm_mlp_post, v_norm_mlp_post)}
    grads, delta, new_m, new_v = {}, {}, {}, {}

    def adamw_big(n, halves):
        w, m, v = weights[n]
        g_, d_, m_, v_ = _adamw_halves(w[0], halves[0], halves[1], m[0], v[0], f"adamw_{n}")
        grads[n], delta[n], new_m[n], new_v[n] = g_[None], d_[None], m_[None], v_[None]

    dw_down = _matmul([(act, dff, TK)], "tn", [F32], name="mlp_down_dw")
    rs_down = _AsyncReduceScatter(dw_down.reshape(N_CHIPS, D_FF // N_CHIPS, D_MODEL), "w_down", 11)
    dw_up = _matmul([(u2, dhid, TK)], "tn", [F32], name="mlp_up_dw", deps=[dw_down], out_quarters=True)
    rs_up = _AsyncReduceScatter(dw_up, "w_up", 8)
    du2 = _matmul([(dhid, g_up, TK)], "nt", [F32], name="mlp_up_dx",
                  deps=[rs_down.sibling_sum(not_before=[dw_up])])
    dh1, dmix, d_g3, d_g2 = _mid_bwd(du2, h1, dh2, mix, norm_mix_post, norm_mlp_pre,
                                     deps=[rs_up.sibling_sum(not_before=[du2])])
    dymix = _matmul([(dmix, w_out_full, TK)], "nt", [F32], name="out_proj_dx")
    dw_out = _matmul([(y_mix, dmix, TK)], "tn", [F32], name="out_proj_dw")
    rs_out = _AsyncReduceScatter(dw_out.reshape(N_CHIPS, D_MIX // N_CHIPS, D_MODEL), "w_out", 5)
    dqkv = _attn_fused_bwd(qkv, dymix, y_att_f32, lse)
    par_late = _after(par, [rs_down.chip_sum(not_before=[dqkv]), rs_out.sibling_sum(not_before=[dymix])],
                      "after_w_down")
    dxc, dz, ddtg, dpar, d_nw = _ssd_bwd(xc, z, dtg, par_late, ssm_norm_w, y, states, dymix)
    g_down = rs_down.share()
    dxbc, dw8 = _conv_bwd(xbc, _after(w8, [*g_down, rs_up.chip_sum(not_before=[dxc])], "after_w_up"), dxc)
    ddt = jnp.pad(_dt_from_groups(ddtg), ((0, 0), (0, LANES - SSM_HEADS))).astype(BF16)
    g_up = rs_up.share()
    dw_z = _matmul([(u, dz, TK)], "tn", [F32], name="proj_z_dw")
    dw_xbc = _matmul([(u, dxbc, TK)], "tn", [F32], name="proj_xbc_dw",
                     deps=[*g_up, rs_out.chip_sum(not_before=[dxbc])])
    g_out = rs_out.share()
    dw_dt = _matmul([(u, ddt, TK)], "tn", [F32], name="proj_dt_dw")
    dw_qkv = _matmul([(u, dqkv, TK)], "tn", [F32], name="proj_qkv_dw")
    dw_in = _quarters_from_cols({"z": dw_z, "xbc": _unperm_cols(dw_xbc), "dt": dw_dt[:, :SSM_HEADS], "qkv": dw_qkv})
    rs_in = _AsyncReduceScatter(dw_in, "w_in", 2)
    adamw_big("w_down", g_down)
    adamw_big("w_up", g_up)
    rs_in.sibling_sum(not_before=[delta["w_up"]])
    du = _matmul([(dz, w_z, TK_MULTI), (dxbc, w_xbc, TK_MULTI), (dqkv, w_qkv, TK_MULTI), (ddt, w_dt, LANES)], "nt",
                 [F32], name="proj_dx", deps=[*g_out, rs_in.part_b])
    grad_x, d_g1 = _first_bwd(du, xs, dh1, norm_mix_pre)
    adamw_big("w_out", g_out)
    rs_in.chip_sum(not_before=[grad_x, delta["w_out"]])

    dconv = _unperm_cols(dw8)
    d_bias, d_alog, d_dskip = _unpack_ssd_params(dpar)
    small_shapes = [(1, D_MODEL), (CONV_WIDTH, D_XBC), (1, D_XBC), (1, SSM_HEADS), (1, SSM_HEADS), (1, SSM_HEADS),
                    (1, D_SSM), (1, D_MODEL), (1, D_MODEL), (1, D_MODEL), (1, LANES)]
    summed = _unpack_rows(
        _all_sum_small(_pack_rows([d_g1, dconv[:CONV_WIDTH], dconv[CONV_WIDTH:CONV_WIDTH + 1], d_bias, d_alog,
                                   d_dskip, d_nw, d_g2, d_g3, d_g4, loss_part])), small_shapes)
    (g_g1, g_conv_full, g_conv_b, g_bias, g_alog, g_dskip, g_nw, g_g2, g_g3, g_g4, loss_row) = summed
    loss = loss_row[0, 0]
    g_conv_w = lax.dynamic_slice(g_conv_full, (0, chip * conv_cols), (CONV_WIDTH, conv_cols))[None]

    grads.update({"norm_mix_pre": g_g1, "conv_w": g_conv_w, "conv_b": g_conv_b, "dt_bias": g_bias,
                  "a_log": g_alog, "d_skip": g_dskip, "ssm_norm_w": g_nw, "norm_mix_post": g_g2,
                  "norm_mlp_pre": g_g3, "norm_mlp_post": g_g4})
    order = list(weights)
    small_names = [n for n in order if n not in ("w_in", "w_out", "w_up", "w_down")]
    small_w_shapes = [weights[n][0].shape for n in small_names]
    packed = [_pack_rows([weights[n][k] for n in small_names]) for k in range(3)]
    packed_g = _pack_rows([grads[n].reshape(weights[n][0].shape) for n in small_names])
    sd, sm, sv = _adamw(packed[0], packed_g, packed[1], packed[2], "adamw_small")
    for k, n in enumerate(small_names):
        grads[n] = grads[n].reshape(weights[n][0].shape)
    for res, pk in ((delta, sd), (new_m, sm), (new_v, sv)):
        for n, val in zip(small_names, _unpack_rows(pk, small_w_shapes)):
            res[n] = val
    adamw_big("w_in", rs_in.share())

    return (loss, grad_x[None], *[grads[n] for n in order], *[delta[n] for n in order],
            *[new_m[n] for n in order], *[new_v[n] for n in order])
```

```python
import functools
import math

import numpy as np
import jax
import jax.numpy as jnp
from jax import lax
from jax.experimental import pallas as pl
from jax.experimental.pallas import tpu as pltpu
from jax.experimental.pallas import tpu_sc as plsc

F32 = jnp.float32
BF16 = jnp.bfloat16

D_MODEL = 2048
SSM_HEAD_DIM = 64
SSM_GROUPS = 8
HEADS_PER_GROUP = 4
SSM_HEADS = SSM_GROUPS * HEADS_PER_GROUP
D_SSM = SSM_HEADS * SSM_HEAD_DIM
D_STATE = 128
CONV_WIDTH = 4
SSD_CHUNK = 128
D_XBC = D_SSM + 2 * SSM_GROUPS * D_STATE
GROUP_X = HEADS_PER_GROUP * SSM_HEAD_DIM
GROUP_COLS = GROUP_X + 2 * D_STATE
ATT_HEAD_DIM = 128
ATT_HEADS = 16
D_ATT = ATT_HEADS * ATT_HEAD_DIM
DILATIONS = (1, 4, 16)
ATT_BLOCK = 128
D_MIX = D_SSM + D_ATT
D_IN_PROJ = D_SSM + D_XBC + SSM_HEADS + 3 * D_ATT
D_FF = 4 * D_MODEL
EPS = 1e-6
N_CHIPS = 4
W_IN_SHARD = D_IN_PROJ // N_CHIPS

ADAM_LR = 0.001
ADAM_B1 = 0.9
ADAM_B2 = 0.999
ADAM_EPS = 1e-08
ADAM_WD = 0.01
ADAM_STEP = 10

LANES = 128
VMEM_LIMIT = 48 * 1024 * 1024
MESH = pl.DeviceIdType.MESH

_NN = (((1,), (0,)), ((), ()))
_NT = (((1,), (1,)), ((), ()))
_TN = (((0,), (0,)), ((), ()))


def _dot(a, b, dims=_NN):
    return lax.dot_general(a, b, dims, preferred_element_type=F32)


def _cparams(*sem):
    return pltpu.CompilerParams(dimension_semantics=sem, vmem_limit_bytes=VMEM_LIMIT)


TK = 2048
TK_MULTI = 1024


def _matmul(pairs, mode, out_dtypes, *, name, tm=1024, tn=1024, epilogue=None, extras=(), deps=(), out_quarters=False):
    a0, b0, _ = pairs[0]
    m_dim = a0.shape[-1] if mode == "tn" else a0.shape[-2]
    if b0.ndim == 3:
        n_dim = b0.shape[1] if mode == "nt" else b0.shape[0] * b0.shape[2]
    else:
        n_dim = b0.shape[0] if mode == "nt" else b0.shape[1]
    tm, tn = min(tm, m_dim), min(tn, n_dim)
    nks, offs = [], []
    for a, _, tk in pairs:
        k_part = a.shape[0] if mode == "tn" else a.shape[-1]
        k_dim = k_part * (a.shape[0] if a.ndim == 3 else 1)
        assert k_part % tk == 0, (name, k_part, tk)
        offs.append(sum(nks))
        nks.append(k_dim // tk)
    nk_total = sum(nks)
    assert m_dim % tm == 0 and n_dim % tn == 0, (name, m_dim, n_dim)
    dims = {"nn": _NN, "nt": _NT, "tn": _TN}[mode]
    n_pairs, n_extra, n_out = len(pairs), len(extras), len(out_dtypes)

    in_specs, operands = [], []
    for (a, b, tk), off, nk in zip(pairs, offs, nks):
        def kidx(k, off=off, nk=nk):
            return k if n_pairs == 1 else jnp.clip(k - off, 0, nk - 1)
        if mode == "tn":
            assert a.ndim == 2
            in_specs.append(pl.BlockSpec((tk, tm), lambda m, n, k, f=kidx: (f(k), m)))
        elif a.ndim == 3:
            per = a.shape[2] // tk
            in_specs.append(pl.BlockSpec((None, tm, tk), lambda m, n, k, f=kidx, per=per: (f(k) // per, m, f(k) % per)))
        else:
            in_specs.append(pl.BlockSpec((tm, tk), lambda m, n, k, f=kidx: (m, f(k))))
        if b.ndim == 3 and mode == "nt":
            per = b.shape[2] // tk
            in_specs.append(pl.BlockSpec((None, tn, tk), lambda m, n, k, f=kidx, per=per: (f(k) // per, n, f(k) % per)))
        elif b.ndim == 3:
            per = b.shape[2] // tn
            in_specs.append(pl.BlockSpec((None, tk, tn), lambda m, n, k, f=kidx, per=per: (n // per, f(k), n % per)))
        elif mode == "nt":
            in_specs.append(pl.BlockSpec((tn, tk), lambda m, n, k, f=kidx: (n, f(k))))
        else:
            in_specs.append(pl.BlockSpec((tk, tn), lambda m, n, k, f=kidx: (f(k), n)))
        operands += [a, b]
    for e in extras:
        in_specs.append(pl.BlockSpec((tm, tn), lambda m, n, k: (m, n)))
        operands.append(e)
    in_specs += [pl.BlockSpec(memory_space=pl.ANY)] * len(deps)
    operands += list(deps)
    first_out = 2 * n_pairs + n_extra + len(deps)
    if out_quarters:
        out_per_q = n_dim // N_CHIPS // tn
        out_dims = (N_CHIPS, m_dim, n_dim // N_CHIPS)
        out_spec = pl.BlockSpec((None, tm, tn), lambda m, n, k: (n // out_per_q, m, n % out_per_q))
    else:
        out_dims = (m_dim, n_dim)
        out_spec = pl.BlockSpec((tm, tn), lambda m, n, k: (m, n))

    def body(*refs):
        ab = refs[:2 * n_pairs]
        e_refs = refs[2 * n_pairs:2 * n_pairs + n_extra]
        o_refs = refs[first_out:first_out + n_out]

        def finish(total):
            vals = (total,) if epilogue is None else epilogue(total, *[e[...] for e in e_refs])
            for o_ref, v in zip(o_refs, vals):
                o_ref[...] = v.astype(o_ref.dtype)

        if nk_total == 1:
            finish(_dot(ab[0][...], ab[1][...], dims))
            return
        acc = refs[-1]
        k = pl.program_id(2)

        @pl.when(k == 0)
        def _():
            acc[...] = jnp.zeros_like(acc)

        for i in range(n_pairs):
            def accumulate(i=i):
                acc[...] += _dot(ab[2 * i][...], ab[2 * i + 1][...], dims)
            if n_pairs == 1:
                accumulate()
            else:
                pl.when((k >= offs[i]) & (k < offs[i] + nks[i]))(accumulate)

        @pl.when(k == nk_total - 1)
        def _():
            finish(acc[...])

    outs = pl.pallas_call(
        body,
        grid=(m_dim // tm, n_dim // tn, nk_total),
        in_specs=in_specs,
        out_specs=[out_spec for _ in out_dtypes],
        out_shape=[jax.ShapeDtypeStruct(out_dims, dt) for dt in out_dtypes],
        scratch_shapes=[pltpu.VMEM((tm, tn), F32)] if nk_total > 1 else [],
        compiler_params=_cparams("parallel", "parallel", "arbitrary"),
        name=name,
    )(*operands)
    return outs[0] if n_out == 1 else outs


def _rowcall(fn, rows, vecs, row_outs, acc_widths, *, name, tr=256, row_cols=None, deps=()):
    s_dim = rows[0].shape[0]
    assert s_dim % tr == 0
    row_cols = row_cols or [None] * len(rows)
    n_r, n_v, n_ro, n_acc = len(rows), len(vecs), len(row_outs), len(acc_widths)
    in_specs = []
    for r, rc in zip(rows, row_cols):
        if rc is None:
            in_specs.append(pl.BlockSpec((tr, r.shape[1]), lambda i: (i, 0)))
        else:
            in_specs.append(pl.BlockSpec((tr, rc[0]), lambda i, c=rc[1]: (i, c)))
    for v in vecs:
        in_specs.append(pl.BlockSpec(v.shape, lambda i, nd=v.ndim: (0,) * nd))
    in_specs += [pl.BlockSpec(memory_space=pl.ANY)] * len(deps)
    n_d = len(deps)

    def body(*refs):
        ins = [r[...] for r in refs[:n_r + n_v]]
        ro = refs[n_r + n_v + n_d:n_r + n_v + n_d + n_ro]
        ao = refs[n_r + n_v + n_d + n_ro:]
        outs = fn(*ins)
        for ref, v in zip(ro, outs[:n_ro]):
            ref[...] = v.astype(ref.dtype)
        if n_acc:
            @pl.when(pl.program_id(0) == 0)
            def _():
                for ref in ao:
                    ref[...] = jnp.zeros_like(ref)
            for ref, v in zip(ao, outs[n_ro:]):
                ref[...] += v

    outs = pl.pallas_call(
        body,
        grid=(s_dim // tr,),
        in_specs=in_specs,
        out_specs=[pl.BlockSpec((tr, w), lambda i: (i, 0)) for w, _ in row_outs]
        + [pl.BlockSpec((1, w), lambda i: (0, 0)) for w in acc_widths],
        out_shape=[jax.ShapeDtypeStruct((s_dim, w), dt) for w, dt in row_outs]
        + [jax.ShapeDtypeStruct((1, w), F32) for w in acc_widths],
        compiler_params=_cparams("arbitrary"),
        name=name,
    )(*rows, *vecs, *deps)
    return outs


def _nrm(x, g):
    r = lax.rsqrt(jnp.mean(x * x, axis=-1, keepdims=True) + EPS)
    n = x * r
    return n * g, n, r


def _nrm_bwd(dy, n, r, g):
    dn = dy * g
    dx = r * (dn - n * jnp.mean(dn * n, axis=-1, keepdims=True))
    return dx, jnp.sum(dy * n, axis=0, keepdims=True)


def _sigmoid(x):
    return 1.0 / (1.0 + jnp.exp(-x))


def _softplus(x):
    return jnp.maximum(x, 0.0) + jnp.log(1.0 + jnp.exp(-jnp.abs(x)))


def _pre_norm(x, g1):
    def fn(xb, g):
        return (_nrm(xb, g)[0],)
    return _rowcall(fn, [x], [g1], [(D_MODEL, BF16)], [], name="pre_norm")[0]


def _post_pre_norm(x, mix, g2, g3):
    def fn(xb, mb, g2b, g3b):
        h1 = xb + _nrm(mb, g2b)[0]
        return h1, _nrm(h1, g3b)[0]
    return _rowcall(fn, [x, mix], [g2, g3], [(D_MODEL, F32), (D_MODEL, BF16)], [], name="post_pre_norm")


def _tail(ff, h1, target, g4):
    def fn(ffb, h1b, tb, g):
        y, n, r = _nrm(ffb, g)
        e = h1b + y - tb
        loss = 0.5 * jnp.sum(jnp.sum(e * e, axis=-1, keepdims=True) * (1.0 / D_MODEL), axis=0, keepdims=True)
        dh2 = e * (1.0 / D_MODEL)
        dff, dg = _nrm_bwd(dh2, n, r, g)
        return dh2, dff, dg, jnp.broadcast_to(loss, (1, LANES))
    return _rowcall(fn, [ff, h1, target], [g4], [(D_MODEL, F32), (D_MODEL, BF16)], [D_MODEL, LANES], name="tail")


def _mid_bwd(du2, h1, dh2, mix, g2, g3, deps=()):
    def fn(du2b, h1b, dh2b, mb, g2b, g3b):
        _, n3, r3 = _nrm(h1b, g3b)
        d3, dg3 = _nrm_bwd(du2b, n3, r3, g3b)
        dh1 = dh2b + d3
        _, n2, r2 = _nrm(mb, g2b)
        dmix, dg2 = _nrm_bwd(dh1, n2, r2, g2b)
        return dh1, dmix, dg3, dg2
    return _rowcall(fn, [du2, h1, dh2, mix], [g2, g3], [(D_MODEL, F32), (D_MODEL, BF16)], [D_MODEL, D_MODEL],
                    name="mid_bwd", deps=deps)


def _first_bwd(du, x, dh1, g1):
    def fn(dub, xb, dh1b, g):
        _, n, r = _nrm(xb, g)
        dx, dg = _nrm_bwd(dub, n, r, g)
        return dh1b + dx, dg
    return _rowcall(fn, [du, x, dh1], [g1], [(D_MODEL, F32)], [D_MODEL], name="first_bwd")


CONV_TILE = 256
CONV_ROWS = 256
PAD = 8


def _conv_taps(w):
    return [w[k:k + 1, :] for k in range(CONV_WIDTH)], w[CONV_WIDTH:CONV_WIDTH + 1, :]


def _conv_fwd(xbc, w8):
    s_dim, c_dim = xbc.shape
    n_steps = s_dim // CONV_ROWS

    def body(x_ref, w_ref, o_ref, xp):
        xp[0:PAD, :] = jnp.zeros((PAD, CONV_TILE), F32)
        xp[PAD:PAD + s_dim, :] = x_ref[...]
        taps, bias = _conv_taps(w_ref[...])

        def step(c, carry):
            base = pl.multiple_of(c * CONV_ROWS, CONV_ROWS)
            win = xp[pl.ds(base, CONV_ROWS + PAD), :]
            pre = bias + taps[3] * win[PAD:, :]
            for j in range(1, CONV_WIDTH):
                pre = pre + taps[3 - j] * pltpu.roll(win, j, axis=0)[PAD:, :]
            o_ref[pl.ds(base, CONV_ROWS), :] = pre * _sigmoid(pre)
            return carry

        lax.fori_loop(0, n_steps, step, 0)

    return pl.pallas_call(
        body,
        grid=(c_dim // CONV_TILE,),
        in_specs=[pl.BlockSpec((s_dim, CONV_TILE), lambda j: (0, j)), pl.BlockSpec((8, CONV_TILE), lambda j: (0, j))],
        out_specs=pl.BlockSpec((s_dim, CONV_TILE), lambda j: (0, j)),
        out_shape=jax.ShapeDtypeStruct((s_dim, c_dim), F32),
        scratch_shapes=[pltpu.VMEM((s_dim + 2 * PAD, CONV_TILE), F32)],
        compiler_params=_cparams("parallel"),
        name="conv_fwd",
    )(xbc, w8)


def _conv_bwd(xbc, w8, dxc):
    s_dim, c_dim = xbc.shape
    n_steps = s_dim // CONV_ROWS

    def body(x_ref, w_ref, d_ref, dx_ref, dw_ref, xp, dp):
        xp[0:PAD, :] = jnp.zeros((PAD, CONV_TILE), F32)
        xp[PAD:PAD + s_dim, :] = x_ref[...]
        dp[PAD + s_dim:, :] = jnp.zeros((PAD, CONV_TILE), F32)
        taps, bias = _conv_taps(w_ref[...])

        def step1(c, sums):
            base = pl.multiple_of(c * CONV_ROWS, CONV_ROWS)
            win = xp[pl.ds(base, CONV_ROWS + PAD), :]
            shifted = [win[PAD:, :]] + [pltpu.roll(win, j, axis=0)[PAD:, :] for j in range(1, CONV_WIDTH)]
            pre = bias
            for j in range(CONV_WIDTH):
                pre = pre + taps[3 - j] * shifted[j]
            sg = _sigmoid(pre)
            dpre = d_ref[pl.ds(base, CONV_ROWS), :] * (sg * (1.0 + pre * (1.0 - sg)))
            dp[pl.ds(base + PAD, CONV_ROWS), :] = dpre
            new = [sums[k] + jnp.sum(dpre * shifted[3 - k], axis=0, keepdims=True) for k in range(CONV_WIDTH)]
            new.append(sums[CONV_WIDTH] + jnp.sum(dpre, axis=0, keepdims=True))
            return tuple(new)

        zero = jnp.zeros((1, CONV_TILE), F32)
        sums = lax.fori_loop(0, n_steps, step1, (zero,) * (CONV_WIDTH + 1))
        dw_ref[...] = jnp.zeros((8, CONV_TILE), F32)
        for k in range(CONV_WIDTH + 1):
            dw_ref[k:k + 1, :] = sums[k]

        def step2(c, carry):
            base = pl.multiple_of(c * CONV_ROWS, CONV_ROWS)
            win = dp[pl.ds(base + PAD, CONV_ROWS + PAD), :]
            dx = taps[3] * win[:CONV_ROWS, :]
            for j in range(1, CONV_WIDTH):
                dx = dx + taps[3 - j] * pltpu.roll(win, CONV_ROWS + PAD - j, axis=0)[:CONV_ROWS, :]
            dx_ref[pl.ds(base, CONV_ROWS), :] = dx.astype(BF16)
            return carry

        lax.fori_loop(0, n_steps, step2, 0)

    col = lambda j: (0, j)
    return pl.pallas_call(
        body,
        grid=(c_dim // CONV_TILE,),
        in_specs=[pl.BlockSpec((s_dim, CONV_TILE), col), pl.BlockSpec((8, CONV_TILE), col),
                  pl.BlockSpec((s_dim, CONV_TILE), col)],
        out_specs=[pl.BlockSpec((s_dim, CONV_TILE), col), pl.BlockSpec((8, CONV_TILE), col)],
        out_shape=[jax.ShapeDtypeStruct((s_dim, c_dim), BF16), jax.ShapeDtypeStruct((8, c_dim), F32)],
        scratch_shapes=[pltpu.VMEM((s_dim + 2 * PAD, CONV_TILE), F32), pltpu.VMEM((s_dim + 2 * PAD, CONV_TILE), F32)],
        compiler_params=_cparams("parallel"),
        name="conv_bwd",
    )(xbc, w8, dxc)


def _perm_cols(a):
    parts = []
    for g in range(SSM_GROUPS):
        parts += [a[..., g * GROUP_X:(g + 1) * GROUP_X],
                  a[..., D_SSM + g * D_STATE:D_SSM + (g + 1) * D_STATE],
                  a[..., D_SSM + SSM_GROUPS * D_STATE + g * D_STATE:D_SSM + SSM_GROUPS * D_STATE + (g + 1) * D_STATE]]
    return jnp.concatenate(parts, axis=-1)


def _unperm_cols(a):
    xs = [a[..., g * GROUP_COLS:g * GROUP_COLS + GROUP_X] for g in range(SSM_GROUPS)]
    bs = [a[..., g * GROUP_COLS + GROUP_X:g * GROUP_COLS + GROUP_X + D_STATE] for g in range(SSM_GROUPS)]
    cs = [a[..., g * GROUP_COLS + GROUP_X + D_STATE:(g + 1) * GROUP_COLS] for g in range(SSM_GROUPS)]
    return jnp.concatenate(xs + bs + cs, axis=-1)


def _dt_to_groups(dt):
    s_dim = dt.shape[0]
    t = dt[:, :SSM_HEADS].reshape(s_dim, SSM_GROUPS, HEADS_PER_GROUP).transpose(1, 0, 2)
    return jnp.pad(t, ((0, 0), (0, 0), (0, LANES - HEADS_PER_GROUP)))


def _dt_from_groups(dtg):
    s_dim = dtg.shape[1]
    return dtg[:, :, :HEADS_PER_GROUP].transpose(1, 0, 2).reshape(s_dim, SSM_HEADS)


def _pack_ssd_params(dt_bias, a_log, d_skip):
    rows = jnp.stack([p.reshape(SSM_GROUPS, HEADS_PER_GROUP) for p in (dt_bias, a_log, d_skip)], axis=1)
    return jnp.pad(rows, ((0, 0), (0, 8 - 3), (0, LANES - HEADS_PER_GROUP)))


def _unpack_ssd_params(par):
    return tuple(par[:, k, :HEADS_PER_GROUP].reshape(SSM_HEADS) for k in range(3))


Q = SSD_CHUNK


def _split3(v):
    hi = v.astype(BF16)
    r1 = v - hi.astype(F32)
    mid = r1.astype(BF16)
    lo = (r1 - mid.astype(F32)).astype(BF16)
    return hi, mid, lo


def _dot_l01(t01, v):
    return sum(_dot(t01, p) for p in _split3(v))


def _dot_r01(v, e01):
    return sum(_dot(p, e01) for p in _split3(v))


def _ssd_consts():
    row = lax.broadcasted_iota(jnp.int32, (Q, Q), 0)
    col = lax.broadcasted_iota(jnp.int32, (Q, Q), 1)
    causal = row >= col
    tril = causal.astype(BF16)
    triu = (col >= row).astype(BF16)
    er = lax.broadcasted_iota(jnp.int32, (LANES, GROUP_X), 0)
    ec = lax.broadcasted_iota(jnp.int32, (LANES, GROUP_X), 1) // SSM_HEAD_DIM
    expand = (er == ec).astype(BF16)
    rr = lax.broadcasted_iota(jnp.int32, (GROUP_X, LANES), 0) // SSM_HEAD_DIM
    rc = lax.broadcasted_iota(jnp.int32, (GROUP_X, LANES), 1)
    reduce = (rr == rc).astype(BF16)
    lane_head = lax.broadcasted_iota(jnp.int32, (Q, GROUP_X), 1) // SSM_HEAD_DIM
    return causal, tril, triu, expand, reduce, lane_head


def _ssd_common(xc_ref, dt_ref, par_ref, consts):
    causal, tril, _, expand, _, _ = consts
    par = par_ref[...]
    bias, alog, dsk = par[0:1, :], par[1:2, :], par[2:3, :]
    a_neg = -jnp.exp(alog)
    dtr = dt_ref[...] + bias
    dt = _softplus(dtr)
    s = _dot_l01(tril, dt * a_neg)
    dt_x = _dot_r01(dt, expand)
    s_x = _dot_r01(s, expand)
    dsk_x = _dot_r01(jnp.broadcast_to(dsk, (8, LANES)), expand)[0:1, :]
    blk = xc_ref[...]
    x = blk[:, :GROUP_X]
    bm = blk[:, GROUP_X:GROUP_X + D_STATE].astype(BF16)
    cm = blk[:, GROUP_X + D_STATE:].astype(BF16)
    xdt = x * dt_x
    g = _dot(cm, bm, _NT)
    return dict(a_neg=a_neg, dtr=dtr, dt=dt, s=s, s_t=s.T, dt_x=dt_x, s_x=s_x, dsk_x=dsk_x, x=x, bm=bm, cm=cm,
                xdt=xdt, g=g)


def _decay(v, r, causal):
    diff = v["s"][:, r:r + 1] - v["s_t"][r:r + 1, :]
    return jnp.exp(jnp.where(causal, diff, -jnp.inf))


def _ssd_specs(n_chunks, rev):
    cidx = (lambda c: n_chunks - 1 - c) if rev else (lambda c: c)
    xc = pl.BlockSpec((Q, GROUP_COLS), lambda g, c: (cidx(c), g))
    gx = pl.BlockSpec((Q, GROUP_X), lambda g, c: (cidx(c), g))
    dt = pl.BlockSpec((None, Q, LANES), lambda g, c: (g, cidx(c), 0))
    par = pl.BlockSpec((None, 8, LANES), lambda g, c: (g, 0, 0))
    nw = pl.BlockSpec((1, GROUP_X), lambda g, c: (0, g))
    hs = pl.BlockSpec((None, None, D_STATE, GROUP_X), lambda g, c: (cidx(c), g, 0, 0))
    return xc, gx, dt, par, nw, hs


def _ssd_fwd(xc, z, dtg, par, nw):
    s_dim = xc.shape[0]
    n_chunks = s_dim // Q
    xc_s, gx_s, dt_s, par_s, nw_s, hs_s = _ssd_specs(n_chunks, False)

    def body(xc_ref, z_ref, dt_ref, par_ref, nw_ref, y_ref, ys_ref, hs_ref, ht):
        @pl.when(pl.program_id(1) == 0)
        def _():
            ht[...] = jnp.zeros_like(ht)

        consts = _ssd_consts()
        causal, lane_head = consts[0], consts[5]
        v = _ssd_common(xc_ref, dt_ref, par_ref, consts)
        xdt_b = v["xdt"].astype(BF16)
        yd = jnp.zeros((Q, GROUP_X), F32)
        for r in range(HEADS_PER_GROUP):
            m = (v["g"] * _decay(v, r, causal)).astype(BF16)
            yd = yd + _dot(m, jnp.where(lane_head == r, xdt_b, jnp.zeros_like(xdt_b)))
        h = ht[...]
        hs_ref[...] = h
        yo = jnp.exp(v["s_x"]) * _dot(v["cm"], h.astype(BF16))
        y = yd + yo + v["dsk_x"] * v["x"]
        s_last = v["s_x"][Q - 1:Q, :]
        snew = _dot(v["bm"], (v["xdt"] * jnp.exp(s_last - v["s_x"])).astype(BF16), _TN)
        ht[...] = jnp.exp(s_last) * h + snew
        zz = z_ref[...]
        yg = y * (zz * _sigmoid(zz))
        y_ref[...] = y
        ys_ref[...] = _nrm(yg, nw_ref[...])[0].astype(BF16)

    return pl.pallas_call(
        body,
        grid=(SSM_GROUPS, n_chunks),
        in_specs=[xc_s, gx_s, dt_s, par_s, nw_s],
        out_specs=[gx_s, gx_s, hs_s],
        out_shape=[jax.ShapeDtypeStruct((s_dim, D_SSM), F32), jax.ShapeDtypeStruct((s_dim, D_SSM), BF16),
                   jax.ShapeDtypeStruct((n_chunks, SSM_GROUPS, D_STATE, GROUP_X), F32)],
        scratch_shapes=[pltpu.VMEM((D_STATE, GROUP_X), F32)],
        compiler_params=_cparams("parallel", "arbitrary"),
        name="ssd_fwd",
    )(xc, z, dtg, par, nw)


def _ssd_bwd(xc, z, dtg, par, nw, y, hs, dymix):
    s_dim = xc.shape[0]
    n_chunks = s_dim // Q
    xc_s, gx_s, dt_s, par_s, nw_s, hs_s = _ssd_specs(n_chunks, True)

    def body(xc_ref, z_ref, dt_ref, par_ref, nw_ref, y_ref, hs_ref, dys_ref,
             dxc_ref, dz_ref, ddt_ref, dpar_ref, dnw_ref, dht):
        @pl.when(pl.program_id(1) == 0)
        def _():
            dht[...] = jnp.zeros_like(dht)
            dpar_ref[...] = jnp.zeros_like(dpar_ref)
            dnw_ref[...] = jnp.zeros_like(dnw_ref)

        consts = _ssd_consts()
        causal, _, triu, _, reduce, lane_head = consts
        v = _ssd_common(xc_ref, dt_ref, par_ref, consts)
        x, bm, cm, xdt, s_x = v["x"], v["bm"], v["cm"], v["xdt"], v["s_x"]
        h = hs_ref[...]
        hb = h.astype(BF16)
        es_x = jnp.exp(s_x)
        yo = es_x * _dot(cm, hb)
        s_last = s_x[Q - 1:Q, :]
        e_x = jnp.exp(s_last - s_x)
        es_last = jnp.exp(s_last)

        yv, zz, nw_v = y_ref[...], z_ref[...], nw_ref[...]
        sg = _sigmoid(zz)
        gz = zz * sg
        _, n, rstd = _nrm(yv * gz, nw_v)
        dout = dys_ref[...]
        dyg, dnw = _nrm_bwd(dout, n, rstd, nw_v)
        dnw_ref[...] += dnw
        dy = dyg * gz
        dz_ref[...] = (dyg * yv * (sg * (1.0 + zz * (1.0 - sg)))).astype(BF16)

        dyb = dy.astype(BF16)
        xdt_b = xdt.astype(BF16)
        dhp = dht[...]
        dhpb = dhp.astype(BF16)
        lane = lax.broadcasted_iota(jnp.int32, (Q, LANES), 1)
        sub = lax.broadcasted_iota(jnp.int32, (LANES, Q), 0)
        dxdt = jnp.zeros((Q, GROUP_X), F32)
        dg = jnp.zeros((Q, Q), F32)
        ds = jnp.zeros((Q, LANES), F32)
        ds_t = jnp.zeros((LANES, Q), F32)
        for r in range(HEADS_PER_GROUP):
            dec = _decay(v, r, causal)
            mf = v["g"] * dec
            dyr = jnp.where(lane_head == r, dyb, jnp.zeros_like(dyb))
            dm = _dot(dyr, xdt_b, _NT)
            dxdt = dxdt + _dot(mf.astype(BF16), dyr, _TN)
            dg = dg + dm * dec
            dd = dm * mf
            ds = ds + jnp.where(lane == r, jnp.sum(dd, axis=1, keepdims=True), 0.0)
            ds_t = ds_t + jnp.where(sub == r, jnp.sum(dd, axis=0, keepdims=True), 0.0)
        ds = ds - ds_t.T
        dgb = dg.astype(BF16)
        dwb = (es_x * dy).astype(BF16)
        dcm = _dot(dgb, bm) + _dot(dwb, hb, _NT)
        dh_prev = _dot(cm, dwb, _TN)
        zst = _dot(bm, dhpb)
        xe = xdt * e_x
        dxdt = dxdt + e_x * zst
        dee = xe * zst
        dbm = _dot(dgb, cm, _TN) + _dot(xe.astype(BF16), dhpb, _NT)
        v_last = jnp.sum(dee, axis=0, keepdims=True) + es_last * jnp.sum(dhp * h, axis=0, keepdims=True)
        row_x = lax.broadcasted_iota(jnp.int32, (Q, GROUP_X), 0)
        tx = dy * yo - dee + jnp.where(row_x == Q - 1, v_last, 0.0)
        ds = ds + _dot_r01(tx, reduce)
        ddta = _dot_l01(triu, ds)
        ddt = ddta * v["a_neg"] + _dot_r01(dxdt * x, reduce)
        dalog = jnp.sum(ddta * v["dt"], axis=0, keepdims=True) * v["a_neg"]
        draw = jnp.where(lane < HEADS_PER_GROUP, ddt * _sigmoid(v["dtr"]), 0.0)
        dbias = jnp.sum(draw, axis=0, keepdims=True)
        ddsk = _dot_r01(jnp.broadcast_to(jnp.sum(dy * x, axis=0, keepdims=True), (8, GROUP_X)), reduce)[0:1, :]
        dht[...] = es_last * dhp + dh_prev
        dxc_ref[:, :GROUP_X] = dxdt * v["dt_x"] + v["dsk_x"] * dy
        dxc_ref[:, GROUP_X:GROUP_X + D_STATE] = dbm
        dxc_ref[:, GROUP_X + D_STATE:] = dcm
        ddt_ref[...] = draw
        dpar_ref[0:1, :] += dbias
        dpar_ref[1:2, :] += dalog
        dpar_ref[2:3, :] += ddsk

    return pl.pallas_call(
        body,
        grid=(SSM_GROUPS, n_chunks),
        in_specs=[xc_s, gx_s, dt_s, par_s, nw_s, gx_s, hs_s, gx_s],
        out_specs=[xc_s, gx_s, dt_s, par_s, nw_s],
        out_shape=[jax.ShapeDtypeStruct((s_dim, SSM_GROUPS * GROUP_COLS), F32),
                   jax.ShapeDtypeStruct((s_dim, D_SSM), BF16),
                   jax.ShapeDtypeStruct((SSM_GROUPS, s_dim, LANES), F32),
                   jax.ShapeDtypeStruct((SSM_GROUPS, 8, LANES), F32),
                   jax.ShapeDtypeStruct((1, D_SSM), F32)],
        scratch_shapes=[pltpu.VMEM((D_STATE, GROUP_X), F32)],
        compiler_params=_cparams("parallel", "arbitrary"),
        name="ssd_bwd",
    )(xc, z, dtg, par, nw, y, hs, dymix)


ATT_SCALE = ATT_HEAD_DIM ** -0.5
NEG_INF = -jnp.inf


def _head(h):
    return slice(h * ATT_HEAD_DIM, (h + 1) * ATT_HEAD_DIM)


def _band_masks():
    qi = lax.broadcasted_iota(jnp.int32, (ATT_BLOCK, ATT_BLOCK), 0)
    kj = lax.broadcasted_iota(jnp.int32, (ATT_BLOCK, ATT_BLOCK), 1)
    return kj <= qi, kj >= qi


def _attn_fwd(qkv_v, d):
    rows = qkv_v.shape[0]
    nb = rows // ATT_BLOCK
    blk = (ATT_BLOCK, D_ATT)
    prev = lambda i: jnp.maximum(i - 1, 0)

    def body(q_ref, kc_ref, kp_ref, vc_ref, vp_ref, o_ref, lse_ref):
        own, before = _band_masks()
        before = before & (pl.program_id(1) > 0)
        lane = lax.broadcasted_iota(jnp.int32, (ATT_BLOCK, LANES), 1)
        lse_all = jnp.zeros((ATT_BLOCK, LANES), F32)
        for h in range(ATT_HEADS):
            q = q_ref[:, _head(h)]
            sc = jnp.where(own, _dot(q, kc_ref[:, _head(h)], _NT) * ATT_SCALE, NEG_INF)
            sp = jnp.where(before, _dot(q, kp_ref[:, _head(h)], _NT) * ATT_SCALE, NEG_INF)
            m = jnp.maximum(jnp.max(sc, axis=1, keepdims=True), jnp.max(sp, axis=1, keepdims=True))
            pc, pp = jnp.exp(sc - m), jnp.exp(sp - m)
            den = jnp.sum(pc, axis=1, keepdims=True) + jnp.sum(pp, axis=1, keepdims=True)
            o = _dot(pc.astype(BF16), vc_ref[:, _head(h)]) + _dot(pp.astype(BF16), vp_ref[:, _head(h)])
            o_ref[:, _head(h)] = o / den
            lse_all = jnp.where(lane == h, m + jnp.log(den), lse_all)
        lse_ref[...] = lse_all

    return pl.pallas_call(
        body,
        grid=(d, nb),
        in_specs=[pl.BlockSpec(blk, lambda r, i: (i, 3 * r)),
                  pl.BlockSpec(blk, lambda r, i: (i, 3 * r + 1)),
                  pl.BlockSpec(blk, lambda r, i: (prev(i), 3 * r + 1)),
                  pl.BlockSpec(blk, lambda r, i: (i, 3 * r + 2)),
                  pl.BlockSpec(blk, lambda r, i: (prev(i), 3 * r + 2))],
        out_specs=[pl.BlockSpec(blk, lambda r, i: (i, r)), pl.BlockSpec((ATT_BLOCK, LANES), lambda r, i: (i, r))],
        out_shape=[jax.ShapeDtypeStruct((rows, d * D_ATT), F32), jax.ShapeDtypeStruct((rows, d * LANES), F32)],
        compiler_params=_cparams("parallel", "arbitrary"),
        name=f"attn_fwd_d{d}",
    )(qkv_v, qkv_v, qkv_v, qkv_v, qkv_v)


def _attn_combine(os_, lses):
    def fn(o1, o2, o3, l1, l2, l3):
        m = jnp.maximum(jnp.maximum(l1, l2), l3)
        tot = m + jnp.log(jnp.exp(l1 - m) + jnp.exp(l2 - m) + jnp.exp(l3 - m))
        w1, w2, w3 = jnp.exp(l1 - tot), jnp.exp(l2 - tot), jnp.exp(l3 - tot)
        cols = []
        for h in range(ATT_HEADS):
            cols.append(w1[:, h:h + 1] * o1[:, _head(h)] + w2[:, h:h + 1] * o2[:, _head(h)]
                        + w3[:, h:h + 1] * o3[:, _head(h)])
        y = jnp.concatenate(cols, axis=1)
        return y, y, tot
    return _rowcall(fn, list(os_) + list(lses), [], [(D_ATT, BF16), (D_ATT, F32), (LANES, F32)], [],
                    name="attn_combine", tr=128)


def _attn_delta(dymix, y_att):
    def fn(dy, y):
        lane = lax.broadcasted_iota(jnp.int32, (dy.shape[0], LANES), 1)
        delta = jnp.zeros((dy.shape[0], LANES), F32)
        for h in range(ATT_HEADS):
            delta = jnp.where(lane == h, jnp.sum(dy[:, _head(h)] * y[:, _head(h)], axis=1, keepdims=True), delta)
        return dy, delta
    return _rowcall(fn, [dymix, y_att], [], [(D_ATT, BF16), (LANES, F32)], [], name="attn_delta",
                    row_cols=[(D_ATT, 1), None])


def _attn_bwd(qkv_v, dy_v, lse_v, delta_v, d):
    rows = qkv_v.shape[0]
    nb = rows // ATT_BLOCK
    blk = (ATT_BLOCK, D_ATT)
    sblk = (ATT_BLOCK, LANES)
    prev = lambda i: jnp.maximum(i - 1, 0)
    nxt = lambda i: jnp.minimum(i + 1, nb - 1)

    def body(qc_ref, qn_ref, kc_ref, kp_ref, vc_ref, vp_ref, dyc_ref, dyn_ref, lc_ref, ln_ref, dc_ref, dn_ref,
             dq_ref, dk_ref, dv_ref):
        i = pl.program_id(1)
        own, before = _band_masks()
        before_c = before & (i > 0)
        before_n = before & (i < nb - 1)
        lc, ln, dc, dn = lc_ref[...], ln_ref[...], dc_ref[...], dn_ref[...]
        for h in range(ATT_HEADS):
            hs = _head(h)
            q, qn, kc, kp, vc, vp = qc_ref[:, hs], qn_ref[:, hs], kc_ref[:, hs], kp_ref[:, hs], vc_ref[:, hs], vp_ref[:, hs]
            dy, dyn = dyc_ref[:, hs], dyn_ref[:, hs]
            lse, lse_n, dl, dl_n = lc[:, h:h + 1], ln[:, h:h + 1], dc[:, h:h + 1], dn[:, h:h + 1]
            pc = jnp.exp(jnp.where(own, _dot(q, kc, _NT) * ATT_SCALE - lse, NEG_INF))
            pp = jnp.exp(jnp.where(before_c, _dot(q, kp, _NT) * ATT_SCALE - lse, NEG_INF))
            pn = jnp.exp(jnp.where(before_n, _dot(qn, kc, _NT) * ATT_SCALE - lse_n, NEG_INF))
            dsc = (pc * (_dot(dy, vc, _NT) - dl)).astype(BF16)
            dsp = (pp * (_dot(dy, vp, _NT) - dl)).astype(BF16)
            dsn = (pn * (_dot(dyn, vc, _NT) - dl_n)).astype(BF16)
            dq_ref[:, hs] = (_dot(dsc, kc) + _dot(dsp, kp)) * ATT_SCALE
            dk_ref[:, hs] = (_dot(dsc, q, _TN) + _dot(dsn, qn, _TN)) * ATT_SCALE
            dv_ref[:, hs] = _dot(pc.astype(BF16), dy, _TN) + _dot(pn.astype(BF16), dyn, _TN)

    return pl.pallas_call(
        body,
        grid=(d, nb),
        in_specs=[pl.BlockSpec(blk, lambda r, i: (i, 3 * r)), pl.BlockSpec(blk, lambda r, i: (nxt(i), 3 * r)),
                  pl.BlockSpec(blk, lambda r, i: (i, 3 * r + 1)), pl.BlockSpec(blk, lambda r, i: (prev(i), 3 * r + 1)),
                  pl.BlockSpec(blk, lambda r, i: (i, 3 * r + 2)), pl.BlockSpec(blk, lambda r, i: (prev(i), 3 * r + 2)),
                  pl.BlockSpec(blk, lambda r, i: (i, r)), pl.BlockSpec(blk, lambda r, i: (nxt(i), r)),
                  pl.BlockSpec(sblk, lambda r, i: (i, r)), pl.BlockSpec(sblk, lambda r, i: (nxt(i), r)),
                  pl.BlockSpec(sblk, lambda r, i: (i, r)), pl.BlockSpec(sblk, lambda r, i: (nxt(i), r))],
        out_specs=[pl.BlockSpec(blk, lambda r, i: (i, r))] * 3,
        out_shape=[jax.ShapeDtypeStruct((rows, d * D_ATT), F32)] * 3,
        compiler_params=_cparams("parallel", "arbitrary"),
        name=f"attn_bwd_d{d}",
    )(qkv_v, qkv_v, qkv_v, qkv_v, qkv_v, qkv_v, dy_v, dy_v, lse_v, lse_v, delta_v, delta_v)


def _attn_sum(dqs, dks, dvs, deps=()):
    def fn(*parts):
        return (jnp.concatenate([parts[0] + parts[1] + parts[2], parts[3] + parts[4] + parts[5],
                                 parts[6] + parts[7] + parts[8]], axis=1),)
    return _rowcall(fn, list(dqs) + list(dks) + list(dvs), [], [(3 * D_ATT, BF16)], [], name="attn_sum", tr=128,
                    deps=deps)[0]


def _attention_fwd(qkv):
    s_dim = qkv.shape[0]
    os_, lses = [], []
    for d in DILATIONS:
        o, lse = _attn_fwd(qkv.reshape(s_dim // d, d * 3 * D_ATT), d)
        os_.append(o.reshape(s_dim, D_ATT))
        lses.append(lse.reshape(s_dim, LANES))
    return _attn_combine(os_, lses)


def _attention_bwd(qkv, dymix, y_att, lse, sum_deps=()):
    s_dim = qkv.shape[0]
    dy, delta = _attn_delta(dymix, y_att)
    dqs, dks, dvs = [], [], []
    for d in DILATIONS:
        dq, dk, dv = _attn_bwd(qkv.reshape(s_dim // d, d * 3 * D_ATT), dy.reshape(s_dim // d, d * D_ATT),
                               lse.reshape(s_dim // d, d * LANES), delta.reshape(s_dim // d, d * LANES), d)
        dqs.append(dq.reshape(s_dim, D_ATT))
        dks.append(dk.reshape(s_dim, D_ATT))
        dvs.append(dv.reshape(s_dim, D_ATT))
    return _attn_sum(dqs, dks, dvs, sum_deps)


WIN = ATT_BLOCK * DILATIONS[-1]
N_BLOCKS = WIN // ATT_BLOCK


def _rows(start, d):
    return pl.ds(start, ATT_BLOCK) if d == 1 else pl.ds(start, ATT_BLOCK, stride=d)


def _block_start(idx, d):
    return (idx // d) * (ATT_BLOCK * d) + idx % d


def _lane_bcast(col):
    return jnp.broadcast_to(col, (col.shape[0], LANES))


def _attn_fused_fwd(qkv):
    s_dim = qkv.shape[0]
    n_win = s_dim // WIN
    blk = (WIN, ATT_HEAD_DIM)
    prev = lambda w: jnp.maximum(w - 1, 0)

    def body(q_ref, kc_ref, kp_ref, vc_ref, vp_ref, y_ref, yf_ref, lse_ref, qf, kf, vf, acc, m_run, l_run):
        w, h = pl.program_id(0), pl.program_id(1)
        qf[...] = q_ref[...].astype(F32)
        kf[0:WIN, :] = kp_ref[...].astype(F32)
        kf[WIN:, :] = kc_ref[...].astype(F32)
        vf[0:WIN, :] = vp_ref[...].astype(F32)
        vf[WIN:, :] = vc_ref[...].astype(F32)
        own, before = _band_masks()

        for d in DILATIONS:
            def block(idx, carry, d=d):
                start = _block_start(idx, d)
                rows = _rows(start, d)
                q = qf[rows, :].astype(BF16)
                kc, vc = kf[_rows(WIN + start, d), :].astype(BF16), vf[_rows(WIN + start, d), :].astype(BF16)
                kp = kf[_rows(WIN + start - ATT_BLOCK * d, d), :].astype(BF16)
                vp = vf[_rows(WIN + start - ATT_BLOCK * d, d), :].astype(BF16)
                has_prev = (idx >= d) | (w > 0)
                sc = jnp.where(own, _dot(q, kc, _NT) * ATT_SCALE, NEG_INF)
                sp = jnp.where(before & has_prev, _dot(q, kp, _NT) * ATT_SCALE, NEG_INF)
                m_blk = jnp.maximum(jnp.max(sc, axis=1, keepdims=True), jnp.max(sp, axis=1, keepdims=True))
                if d == DILATIONS[0]:
                    m_new = m_blk
                else:
                    m_old = m_run[rows, :][:, 0:1]
                    m_new = jnp.maximum(m_old, m_blk)
                pc, pp = jnp.exp(sc - m_new), jnp.exp(sp - m_new)
                l_new = jnp.sum(pc, axis=1, keepdims=True) + jnp.sum(pp, axis=1, keepdims=True)
                o_new = _dot(pc.astype(BF16), vc) + _dot(pp.astype(BF16), vp)
                if d != DILATIONS[0]:
                    alpha = jnp.exp(m_old - m_new)
                    l_new = alpha * l_run[rows, :][:, 0:1] + l_new
                    o_new = alpha * acc[rows, :] + o_new
                m_run[rows, :] = _lane_bcast(m_new)
                l_run[rows, :] = _lane_bcast(l_new)
                acc[rows, :] = o_new
                return carry

            for idx in range(N_BLOCKS):
                block(idx, 0)

        l_all = l_run[...]
        y = acc[...] / l_all
        y_ref[...] = y.astype(BF16)
        yf_ref[...] = y
        @pl.when(h == 0)
        def _():
            lse_ref[...] = jnp.zeros_like(lse_ref)

        lane = lax.broadcasted_iota(jnp.int32, (WIN, LANES), 1)
        lse_ref[...] = jnp.where(lane == h, m_run[...] + jnp.log(l_all), lse_ref[...])

    win_scratch = lambda rows: pltpu.VMEM((rows, ATT_HEAD_DIM), F32)
    return pl.pallas_call(
        body,
        grid=(n_win, ATT_HEADS),
        in_specs=[pl.BlockSpec(blk, lambda w, h: (w, h)),
                  pl.BlockSpec(blk, lambda w, h: (w, ATT_HEADS + h)),
                  pl.BlockSpec(blk, lambda w, h: (prev(w), ATT_HEADS + h)),
                  pl.BlockSpec(blk, lambda w, h: (w, 2 * ATT_HEADS + h)),
                  pl.BlockSpec(blk, lambda w, h: (prev(w), 2 * ATT_HEADS + h))],
        out_specs=[pl.BlockSpec(blk, lambda w, h: (w, h)), pl.BlockSpec(blk, lambda w, h: (w, h)),
                   pl.BlockSpec((WIN, LANES), lambda w, h: (w, 0))],
        out_shape=[jax.ShapeDtypeStruct((s_dim, D_ATT), BF16), jax.ShapeDtypeStruct((s_dim, D_ATT), F32),
                   jax.ShapeDtypeStruct((s_dim, LANES), F32)],
        scratch_shapes=[win_scratch(WIN), win_scratch(2 * WIN), win_scratch(2 * WIN), win_scratch(WIN),
                        win_scratch(WIN), win_scratch(WIN)],
        compiler_params=_cparams("parallel", "arbitrary"),
        name="attn_fused_fwd",
    )(qkv, qkv, qkv, qkv, qkv)


def _attn_fused_bwd(qkv, dymix, y_att, lse, deps=()):
    s_dim = qkv.shape[0]
    n_win = s_dim // WIN
    blk = (WIN, ATT_HEAD_DIM)
    prev = lambda w: jnp.maximum(w - 1, 0)
    nxt = lambda w: jnp.minimum(w + 1, n_win - 1)
    n_dep = len(deps)

    def body(qc_ref, qn_ref, kc_ref, kp_ref, vc_ref, vp_ref, dyc_ref, dyn_ref, yc_ref, yn_ref, lc_ref, ln_ref, *rest):
        out_ref = rest[n_dep]
        qf, qnf, kf, vf, dq_acc, dk_acc, dv_acc, ls_c, dl_c, ls_n, dl_n = rest[n_dep + 1:]
        w, h = pl.program_id(0), pl.program_id(1)
        qf[...] = qc_ref[...].astype(F32)
        qnf[...] = qn_ref[...].astype(F32)
        kf[0:WIN, :] = kp_ref[...].astype(F32)
        kf[WIN:, :] = kc_ref[...].astype(F32)
        vf[0:WIN, :] = vp_ref[...].astype(F32)
        vf[WIN:, :] = vc_ref[...].astype(F32)
        lane = lax.broadcasted_iota(jnp.int32, (WIN, LANES), 1)
        pick = lambda ref: _lane_bcast(jnp.sum(jnp.where(lane == h, ref[...], 0.0), axis=1, keepdims=True))
        ls_c[...] = pick(lc_ref)
        ls_n[...] = pick(ln_ref)
        dl_c[...] = _lane_bcast(jnp.sum(dyc_ref[...] * yc_ref[...], axis=1, keepdims=True))
        dl_n[...] = _lane_bcast(jnp.sum(dyn_ref[...] * yn_ref[...], axis=1, keepdims=True))
        for ref in (dq_acc, dk_acc, dv_acc):
            ref[...] = jnp.zeros_like(ref)
        own, before = _band_masks()

        def probs(q, k, v, dy, lse_col, dl_col, mask):
            p = jnp.exp(jnp.where(mask, _dot(q, k, _NT) * ATT_SCALE - lse_col, NEG_INF))
            ds = p * (_dot(dy, v, _NT) - dl_col)
            return p.astype(BF16), ds.astype(BF16)

        for d in DILATIONS:
            def block(idx, carry, d=d):
                start = _block_start(idx, d)
                rows = _rows(start, d)
                q, dy = qf[rows, :].astype(BF16), dyc_ref[rows, :].astype(BF16)
                lse_col, dl_col = ls_c[rows, :][:, 0:1], dl_c[rows, :][:, 0:1]
                kc, vc = kf[_rows(WIN + start, d), :].astype(BF16), vf[_rows(WIN + start, d), :].astype(BF16)
                kp = kf[_rows(WIN + start - ATT_BLOCK * d, d), :].astype(BF16)
                vp = vf[_rows(WIN + start - ATT_BLOCK * d, d), :].astype(BF16)
                pc, dsc = probs(q, kc, vc, dy, lse_col, dl_col, own)
                pp, dsp = probs(q, kp, vp, dy, lse_col, dl_col, before & ((idx >= d) | (w > 0)))
                dq_acc[rows, :] += (_dot(dsc, kc) + _dot(dsp, kp)) * ATT_SCALE
                dk_acc[rows, :] += _dot(dsc, q, _TN) * ATT_SCALE
                dv_acc[rows, :] += _dot(pc, dy, _TN)

                if idx >= d:
                    prows = _rows(start - ATT_BLOCK * d, d)
                    dk_acc[prows, :] += _dot(dsp, q, _TN) * ATT_SCALE
                    dv_acc[prows, :] += _dot(pp, dy, _TN)
                return carry

            for idx in range(N_BLOCKS):
                block(idx, 0)

            def next_window(r, carry, d=d):
                krows = _rows(WIN - ATT_BLOCK * d + r, d)
                rows = _rows(r, d)
                q, dy = qnf[rows, :].astype(BF16), dyn_ref[rows, :].astype(BF16)
                k, v = kf[_rows(2 * WIN - ATT_BLOCK * d + r, d), :].astype(BF16), vf[_rows(2 * WIN - ATT_BLOCK * d + r, d), :].astype(BF16)
                pn, dsn = probs(q, k, v, dy, ls_n[rows, :][:, 0:1], dl_n[rows, :][:, 0:1], before & (w < n_win - 1))
                dk_acc[krows, :] += _dot(dsn, q, _TN) * ATT_SCALE
                dv_acc[krows, :] += _dot(pn, dy, _TN)
                return carry

            for r in range(d):
                next_window(r, 0)

        for part, acc_ref in enumerate((dq_acc, dk_acc, dv_acc)):
            out_ref[part] = acc_ref[...].astype(BF16)

    win_scratch = lambda rows: pltpu.VMEM((rows, ATT_HEAD_DIM), F32)
    cur = lambda c: pl.BlockSpec(blk, lambda w, h: (w, c + h))
    return pl.pallas_call(
        body,
        grid=(n_win, ATT_HEADS),
        in_specs=[cur(0), pl.BlockSpec(blk, lambda w, h: (nxt(w), h)),
                  cur(ATT_HEADS), pl.BlockSpec(blk, lambda w, h: (prev(w), ATT_HEADS + h)),
                  cur(2 * ATT_HEADS), pl.BlockSpec(blk, lambda w, h: (prev(w), 2 * ATT_HEADS + h)),
                  cur(ATT_HEADS), pl.BlockSpec(blk, lambda w, h: (nxt(w), ATT_HEADS + h)),
                  cur(0), pl.BlockSpec(blk, lambda w, h: (nxt(w), h)),
                  pl.BlockSpec((WIN, LANES), lambda w, h: (w, 0)), pl.BlockSpec((WIN, LANES), lambda w, h: (nxt(w), 0))]
        + [ANY] * n_dep,
        out_specs=pl.BlockSpec((3, WIN, ATT_HEAD_DIM), lambda w, h: (0, w, h)),
        out_shape=jax.ShapeDtypeStruct((3, s_dim, D_ATT), BF16),
        scratch_shapes=[win_scratch(WIN), win_scratch(WIN), win_scratch(2 * WIN), win_scratch(2 * WIN)]
        + [win_scratch(WIN)] * 7,
        compiler_params=_cparams("parallel", "arbitrary"),
        name="attn_fused_bwd",
    )(qkv, qkv, qkv, qkv, qkv, qkv, dymix, dymix, y_att, y_att, lse, lse, *deps)


def _adamw(w, g, m, v, name):
    def fn(wb, gb, mb, vb):
        m2 = ADAM_B1 * mb + (1.0 - ADAM_B1) * gb
        v2 = ADAM_B2 * vb + (1.0 - ADAM_B2) * (gb * gb)
        m_hat = m2 / (1.0 - ADAM_B1 ** ADAM_STEP)
        v_hat = v2 / (1.0 - ADAM_B2 ** ADAM_STEP)
        delta = -ADAM_LR * (m_hat / (jnp.sqrt(v_hat) + ADAM_EPS) + ADAM_WD * wb)
        return delta, m2, v2
    cols = w.shape[1]
    tr = 128 if w.shape[0] % 128 == 0 else w.shape[0]
    return _rowcall(fn, [w, g, m, v], [], [(cols, F32)] * 3, [], name=name, tr=tr)


ANY = pl.BlockSpec(memory_space=pl.ANY)


def _position():
    x, y, c = lax.axis_index("x"), lax.axis_index("y"), lax.axis_index("c")
    chips = [(1 - x, y), (x, 1 - y), (1 - x, 1 - y)]
    return x, y, c, chips


def _remote(src, dst, send_sem, recv_sem, device):
    return pltpu.make_async_remote_copy(src_ref=src, dst_ref=dst, send_sem=send_sem, recv_sem=recv_sem,
                                        device_id=device, device_id_type=MESH)


def _gather_shards(shards):
    n = len(shards)

    def body(*refs):
        ins, outs = refs[:n], refs[n:2 * n]
        send_sems, recv_sems = refs[2 * n:]
        x, y, c, chips = _position()
        sibling = (x, y, 1 - c)

        def half(a, j, cc):
            h = ins[a].shape[0] // 2
            return outs[a].at[j, pl.ds(cc * h, h), :]

        sent = []
        for a in range(n):
            h = ins[a].shape[0] // 2
            for j, chip in enumerate(chips):
                cp = _remote(ins[a].at[pl.ds(c * h, h), :], half(a, 2 * x + y, c), send_sems.at[6 * a + j],
                             recv_sems.at[6 * a + j], (chip[0], chip[1], c))
                cp.start()
                sent.append(cp)
        for a in range(n):
            for j, chip in enumerate(chips):
                landed = half(a, 2 * chip[0] + chip[1], c)
                _remote(landed, landed, send_sems.at[6 * a + j], recv_sems.at[6 * a + j], (x, y, c)).wait_recv()
                cp = _remote(landed, landed, send_sems.at[6 * a + 3 + j], recv_sems.at[6 * a + 3 + j], sibling)
                cp.start()
                sent.append(cp)
        for a in range(n):
            for j, chip in enumerate(chips):
                handed = half(a, 2 * chip[0] + chip[1], 1 - c)
                _remote(handed, handed, send_sems.at[6 * a + 3 + j], recv_sems.at[6 * a + 3 + j], (x, y, c)).wait_recv()
        for cp in sent:
            cp.wait_send()

    return pl.pallas_call(
        body,
        in_specs=[ANY] * n,
        out_specs=[ANY] * n,
        out_shape=[jax.ShapeDtypeStruct((N_CHIPS,) + s.shape, s.dtype) for s in shards],
        scratch_shapes=[pltpu.SemaphoreType.DMA((6 * n,)), pltpu.SemaphoreType.DMA((6 * n,))],
        name="gather_shards",
    )(*shards)


def _handshake(peers):
    barrier = pltpu.get_barrier_semaphore()
    for p in peers:
        pl.semaphore_signal(barrier, inc=1, device_id=p, device_id_type=MESH)
    pl.semaphore_wait(barrier, len(peers))


def _gather_shards_async(shards, collective_id, name):
    n = len(shards)
    srcs = [jax.new_ref(s, memory_space=pltpu.MemorySpace.HBM) for s in shards]
    dsts = [jax.empty_ref(jax.ShapeDtypeStruct((N_CHIPS,) + s.shape, s.dtype), memory_space=pltpu.MemorySpace.HBM)
            for s in shards]

    @pl.kernel(mesh=plsc.ScalarSubcoreMesh(axis_name="seq", num_cores=1), name=name,
               scratch_types=(pltpu.SemaphoreType.DMA((6 * n,)), pltpu.SemaphoreType.DMA((6 * n,))),
               compiler_params=pltpu.CompilerParams(collective_id=collective_id))
    def launch(send_sems, recv_sems):
        x, y, c, chips = _position()
        sibling = (x, y, 1 - c)
        _handshake([(chip[0], chip[1], c) for chip in chips] + [sibling])

        def half(a, j, cc):
            h = shards[a].shape[0] // 2
            return dsts[a].at[j, pl.ds(cc * h, h), :]

        sent = []
        for a in range(n):
            h = shards[a].shape[0] // 2
            for j, chip in enumerate(chips):
                cp = _remote(srcs[a].at[pl.ds(c * h, h), :], half(a, 2 * x + y, c), send_sems.at[6 * a + j],
                             recv_sems.at[6 * a + j], (chip[0], chip[1], c))
                cp.start()
                sent.append(cp)
        for a in range(n):
            for j, chip in enumerate(chips):
                landed = half(a, 2 * chip[0] + chip[1], c)
                _remote(landed, landed, send_sems.at[6 * a + j], recv_sems.at[6 * a + j], (x, y, c)).wait_recv()
                cp = _remote(landed, landed, send_sems.at[6 * a + 3 + j], recv_sems.at[6 * a + 3 + j], sibling)
                cp.start()
                sent.append(cp)
        for a in range(n):
            for j, chip in enumerate(chips):
                handed = half(a, 2 * chip[0] + chip[1], 1 - c)
                _remote(handed, handed, send_sems.at[6 * a + 3 + j], recv_sems.at[6 * a + 3 + j], (x, y, c)).wait_recv()
        for cp in sent:
            cp.wait_send()

    launch()
    return [d[...] for d in dsts]


IN_COLS = {"z": (0, D_SSM), "xbc": (D_SSM, D_SSM + D_XBC), "dt": (D_SSM + D_XBC, D_SSM + D_XBC + SSM_HEADS),
           "qkv": (D_SSM + D_XBC + SSM_HEADS, D_IN_PROJ)}


def _cols_from_quarters(quarters, lo, hi):
    parts = []
    for q in range(N_CHIPS):
        a, b = max(lo, q * W_IN_SHARD), min(hi, (q + 1) * W_IN_SHARD)
        if a < b:
            parts.append(quarters[q][:, a - q * W_IN_SHARD:b - q * W_IN_SHARD])
    return parts[0] if len(parts) == 1 else jnp.concatenate(parts, axis=1)


def _quarters_from_cols(pieces):
    rows = next(iter(pieces.values())).shape[0]
    out = lax.empty((N_CHIPS, rows, W_IN_SHARD), next(iter(pieces.values())).dtype)
    for q in range(N_CHIPS):
        for name, (lo, hi) in IN_COLS.items():
            a, b = max(lo, q * W_IN_SHARD), min(hi, (q + 1) * W_IN_SHARD)
            if a < b:
                out = lax.dynamic_update_slice(out, pieces[name][None, :, a - lo:b - lo], (q, 0, a - q * W_IN_SHARD))
    return out


def _by_chip(own, fetched):
    me = 2 * lax.axis_index("x") + lax.axis_index("y")
    return lax.dynamic_update_slice(fetched, own[None], (me, 0, 0))


def _add_sibling(grad, got, c_arr, name, deps=()):
    nq, rows, cols = grad.shape
    h = rows // 2
    tr = 128
    nb = h // tr

    def body(c_ref, a_ref, b_ref, *rest):
        o_ref, ob_ref = rest[len(deps):]
        total = a_ref[...] + b_ref[...]
        o_ref[...] = total
        ob_ref[...] = total.astype(BF16)

    out_spec = pl.BlockSpec((None, tr, cols), lambda q, i, c: (q, i, 0))
    return pl.pallas_call(
        body,
        grid_spec=pltpu.PrefetchScalarGridSpec(
            num_scalar_prefetch=1, grid=(nq, nb),
            in_specs=[pl.BlockSpec((None, tr, cols), lambda q, i, c: (q, c[0] * nb + i, 0)),
                      pl.BlockSpec((None, tr, cols), lambda q, i, c: (q, i, 0))] + [ANY] * len(deps),
            out_specs=[out_spec, out_spec]),
        out_shape=[jax.ShapeDtypeStruct((nq, h, cols), F32), jax.ShapeDtypeStruct((nq, h, cols), BF16)],
        compiler_params=_cparams("parallel", "parallel"),
        name=name,
    )(c_arr, grad, got, *deps)


def _add_chips(part, got, chip_arr, name, deps=()):
    _, h, cols = part.shape
    tr = 128

    def body(q_ref, p_ref, g0_ref, g1_ref, g2_ref, *rest):
        o_ref = rest[len(deps)]
        o_ref[...] = ((p_ref[...] + g0_ref[...].astype(F32)) + g1_ref[...].astype(F32)) + g2_ref[...].astype(F32)

    got_spec = lambda j: pl.BlockSpec((None, tr, cols), lambda i, q: (j, i, 0))
    return pl.pallas_call(
        body,
        grid_spec=pltpu.PrefetchScalarGridSpec(
            num_scalar_prefetch=1, grid=(h // tr,),
            in_specs=[pl.BlockSpec((None, tr, cols), lambda i, q: (q[0], i, 0)), got_spec(0), got_spec(1), got_spec(2)]
            + [ANY] * len(deps),
            out_specs=pl.BlockSpec((tr, cols), lambda i, q: (i, 0))),
        out_shape=jax.ShapeDtypeStruct((h, cols), F32),
        compiler_params=_cparams("parallel"),
        name=name,
    )(chip_arr, part, got, got, got, *deps)


def _sequencer_exchange(src, out_shape, collective_id, name, plan, n_copies):
    src_ref = jax.new_ref(src, memory_space=pltpu.MemorySpace.HBM)
    dst_ref = jax.empty_ref(out_shape, memory_space=pltpu.MemorySpace.HBM)

    @pl.kernel(mesh=plsc.ScalarSubcoreMesh(axis_name="seq", num_cores=1), name=name,
               scratch_types=(pltpu.SemaphoreType.DMA((n_copies,)), pltpu.SemaphoreType.DMA((n_copies,))),
               compiler_params=pltpu.CompilerParams(collective_id=collective_id))
    def launch(send_sems, recv_sems):
        x, y, c, chips = _position()
        copies = plan(src_ref, dst_ref, x, y, c, chips)
        _handshake([peer for _, _, peer in copies])
        started = []
        for k, (s, d, peer) in enumerate(copies):
            cp = _remote(s, d, send_sems.at[k], recv_sems.at[k], peer)
            cp.start()
            started.append(cp)
        for cp in started:
            cp.wait()

    launch()
    return dst_ref[...]


class _AsyncReduceScatter:
    def __init__(self, grad, nm, first_id):
        self.grad, self.nm, self.first_id = grad, nm, first_id
        nq, rows, cols = grad.shape
        h = self.h = rows // 2

        def to_sibling(s, d, x, y, c, chips):
            return [(s.at[:, pl.ds((1 - c) * h, h), :], d, (x, y, 1 - c))]

        self.from_sibling = _sequencer_exchange(grad, jax.ShapeDtypeStruct((nq, h, cols), F32), first_id,
                                                f"rs_sibling_{nm}", to_sibling, 1)

    def sibling_sum(self, not_before=()):
        cols = self.grad.shape[2]
        c_arr = lax.axis_index("c").astype(jnp.int32).reshape(1)
        self.part, self.part_b = _add_sibling(self.grad, self.from_sibling, c_arr, f"add_sibling_{self.nm}", not_before)

        def to_chips(s, d, x, y, c, chips):
            return [(s.at[2 * chip[0] + chip[1]], d.at[j], (chip[0], chip[1], c)) for j, chip in enumerate(chips)]

        self.from_chips = _sequencer_exchange(self.part_b, jax.ShapeDtypeStruct((3, self.h, cols), BF16),
                                              self.first_id + 1, f"rs_quarters_{self.nm}", to_chips, 3)
        return self.part_b

    def chip_sum(self, not_before=()):
        cols = self.grad.shape[2]
        chip_arr = (2 * lax.axis_index("x") + lax.axis_index("y")).astype(jnp.int32).reshape(1)
        self.half = _add_chips(self.part, self.from_chips, chip_arr, f"add_chips_{self.nm}", not_before)

        def whole_to_sibling(s, d, x, y, c, chips):
            return [(s, d, (x, y, 1 - c))]

        self.other = _sequencer_exchange(self.half, jax.ShapeDtypeStruct((self.h, cols), F32), self.first_id + 2,
                                         f"rs_share_{self.nm}", whole_to_sibling, 1)
        return self.half

    def share(self):
        return self.half, self.other


def _after(x, deps, name):
    def body(x_ref, *rest):
        rest[-1][...] = x_ref[...]

    vm = pl.BlockSpec(memory_space=pltpu.VMEM)
    return pl.pallas_call(body, in_specs=[vm] + [ANY] * len(deps), out_specs=vm,
                          out_shape=jax.ShapeDtypeStruct(x.shape, x.dtype), name=name)(x, *deps)


def _adamw_halves(w, mine, other, m, v, name):
    rows, cols = w.shape
    tr = 128
    nb = rows // 2 // tr
    c_arr = lax.axis_index("c").astype(jnp.int32).reshape(1)

    def body(c_ref, w_ref, a_ref, b_ref, m_ref, v_ref, g_out, d_out, m_out, v_out):
        is_mine = (pl.program_id(0) // nb) == c_ref[0]
        g = jnp.where(is_mine, a_ref[...], b_ref[...])
        wb, mb, vb = w_ref[...], m_ref[...], v_ref[...]
        m2 = ADAM_B1 * mb + (1.0 - ADAM_B1) * g
        v2 = ADAM_B2 * vb + (1.0 - ADAM_B2) * (g * g)
        m_hat = m2 / (1.0 - ADAM_B1 ** ADAM_STEP)
        v_hat = v2 / (1.0 - ADAM_B2 ** ADAM_STEP)
        g_out[...] = g
        d_out[...] = -ADAM_LR * (m_hat / (jnp.sqrt(v_hat) + ADAM_EPS) + ADAM_WD * wb)
        m_out[...] = m2
        v_out[...] = v2

    full = pl.BlockSpec((tr, cols), lambda i, c: (i, 0))
    half = pl.BlockSpec((tr, cols), lambda i, c: (i % nb, 0))
    return pl.pallas_call(
        body,
        grid_spec=pltpu.PrefetchScalarGridSpec(
            num_scalar_prefetch=1, grid=(rows // tr,),
            in_specs=[full, half, half, full, full], out_specs=[full] * 4),
        out_shape=[jax.ShapeDtypeStruct((rows, cols), F32)] * 4,
        compiler_params=_cparams("parallel"),
        name=name,
    )(c_arr, w, mine, other, m, v)


def _all_sum_small(v):
    n_dev = 8

    def body(v_ref, o_ref, gath, send_sems, recv_sems):
        x, y, c, _ = _position()
        me = 4 * x + 2 * y + c
        gath[me] = v_ref[...]
        copies = []
        for k in range(1, n_dev):
            peer = tuple(1 - p if (k >> s) & 1 else p for p, s in ((x, 2), (y, 1), (c, 0)))
            cp = _remote(v_ref, gath.at[me], send_sems.at[k - 1], recv_sems.at[k - 1], peer)
            cp.start()
            copies.append(cp)
        for cp in copies:
            cp.wait()
        acc = gath[0]
        for i in range(1, n_dev):
            acc = acc + gath[i]
        o_ref[...] = acc

    vm = pl.BlockSpec(memory_space=pltpu.VMEM)
    return pl.pallas_call(
        body,
        in_specs=[vm],
        out_specs=vm,
        out_shape=jax.ShapeDtypeStruct(v.shape, F32),
        scratch_shapes=[pltpu.VMEM((n_dev,) + v.shape, F32), pltpu.SemaphoreType.DMA((n_dev - 1,)),
                        pltpu.SemaphoreType.DMA((n_dev - 1,))],
        name="all_sum_small",
    )(v)


def _pack_rows(vectors):
    rows = []
    for v in vectors:
        flat = v.reshape(-1).astype(F32)
        rows.append(jnp.pad(flat, (0, (-flat.shape[0]) % LANES)).reshape(-1, LANES))
    out = jnp.concatenate(rows, axis=0)
    return jnp.pad(out, ((0, (-out.shape[0]) % 8), (0, 0)))


def _unpack_rows(packed, shapes):
    outs, r = [], 0
    for shp in shapes:
        size = math.prod(shp)
        nr = -(-size // LANES)
        outs.append(packed[r:r + nr].reshape(-1)[:size].reshape(shp))
        r += nr
    return outs


def _relu_sq(acc):
    r = jnp.maximum(acc, 0.0)
    return r, r * r


def _relu_sq_bwd(acc, r):
    return (acc * (2.0 * r.astype(F32)),)


def kernel(x, norm_mix_pre, w_in, conv_w, conv_b, dt_bias, a_log, d_skip, ssm_norm_w, w_out, norm_mix_post, norm_mlp_pre, w_up, w_down, norm_mlp_post, loss_target, m_norm_mix_pre, m_w_in, m_conv_w, m_conv_b, m_dt_bias, m_a_log, m_d_skip, m_ssm_norm_w, m_w_out, m_norm_mix_post, m_norm_mlp_pre, m_w_up, m_w_down, m_norm_mlp_post, v_norm_mix_pre, v_w_in, v_conv_w, v_conv_b, v_dt_bias, v_a_log, v_d_skip, v_ssm_norm_w, v_w_out, v_norm_mix_post, v_norm_mlp_pre, v_w_up, v_w_down, v_norm_mlp_post):
    s_dim = x.shape[1]
    xs, target = x[0], loss_target[0]
    chip = 2 * lax.axis_index("x") + lax.axis_index("y")

    own = [w_in[0].astype(BF16), w_out[0].astype(BF16), w_up[0].astype(BF16), w_down[0].astype(BF16)]
    fetched_in = _gather_shards_async(own[:1], 14, "gather_w_in")[0]
    conv_cols = D_XBC // N_CHIPS
    conv_placed = lax.dynamic_update_slice(jnp.zeros((8, D_XBC), F32), 0.5 * conv_w[0], (0, chip * conv_cols))
    conv_full = _all_sum_small(conv_placed.reshape(-1, LANES)).reshape(8, D_XBC)
    w8 = _perm_cols(conv_full.at[CONV_WIDTH].set(conv_b[0]))
    u = _pre_norm(xs, norm_mix_pre)
    fetched_in, u, w8, *rest = lax.optimization_barrier((fetched_in, u, w8, *own[1:]))
    fetched = [fetched_in] + _gather_shards_async(rest, 1, "gather_rest")
    g_in, g_out, g_up, g_down = [_by_chip(o, f) for o, f in zip(own, fetched)]
    w_z = _cols_from_quarters(g_in, *IN_COLS["z"])
    w_xbc = _perm_cols(_cols_from_quarters(g_in, *IN_COLS["xbc"]))
    w_dt = jnp.pad(_cols_from_quarters(g_in, *IN_COLS["dt"]), ((0, 0), (0, LANES - SSM_HEADS)))
    w_qkv = _cols_from_quarters(g_in, *IN_COLS["qkv"])
    w_out_full = g_out.reshape(D_MIX, D_MODEL)
    w_down_full = g_down.reshape(D_FF, D_MODEL)

    z = _matmul([(u, w_z, TK)], "nn", [F32], name="proj_z")
    xbc = _matmul([(u, w_xbc, TK)], "nn", [F32], name="proj_xbc")
    dt_raw = _matmul([(u, w_dt, TK)], "nn", [F32], name="proj_dt")
    qkv = _matmul([(u, w_qkv, TK)], "nn", [BF16], name="proj_qkv")
    xc = _conv_fwd(xbc, w8)
    dtg = _dt_to_groups(dt_raw)
    par = _pack_ssd_params(dt_bias[0], a_log[0], d_skip[0])
    y, y_ssm, states = _ssd_fwd(xc, z, dtg, par, ssm_norm_w)
    y_att, y_att_f32, lse = _attn_fused_fwd(qkv)
    y_mix = jnp.concatenate([y_ssm, y_att], axis=1)
    mix = _matmul([(y_mix, w_out_full, TK)], "nn", [F32], name="out_proj")
    h1, u2 = _post_pre_norm(xs, mix, norm_mix_post, norm_mlp_pre)
    hid, act = _matmul([(u2, g_up, TK)], "nn", [BF16, BF16], name="mlp_up", epilogue=_relu_sq)
    ff = _matmul([(act, w_down_full, TK)], "nn", [F32], name="mlp_down")
    dh2, dff, d_g4, loss_part = _tail(ff, h1, target, norm_mlp_post)

    dhid = _matmul([(dff, w_down_full, TK)], "nt", [BF16], name="mlp_down_dx", epilogue=_relu_sq_bwd, extras=[hid])
    weights = {"norm_mix_pre": (norm_mix_pre, m_norm_mix_pre, v_norm_mix_pre), "w_in": (w_in, m_w_in, v_w_in),
               "conv_w": (conv_w, m_conv_w, v_conv_w), "conv_b": (conv_b, m_conv_b, v_conv_b),
               "dt_bias": (dt_bias, m_dt_bias, v_dt_bias), "a_log": (a_log, m_a_log, v_a_log),
               "d_skip": (d_skip, m_d_skip, v_d_skip), "ssm_norm_w": (ssm_norm_w, m_ssm_norm_w, v_ssm_norm_w),
               "w_out": (w_out, m_w_out, v_w_out), "norm_mix_post": (norm_mix_post, m_norm_mix_post, v_norm_mix_post),
               "norm_mlp_pre": (norm_mlp_pre, m_norm_mlp_pre, v_norm_mlp_pre), "w_up": (w_up, m_w_up, v_w_up),
               "w_down": (w_down, m_w_down, v_w_down),
               "norm_mlp_post": (norm_mlp_post, m_norm_mlp_post, v_norm_mlp_post)}
    grads, delta, new_m, new_v = {}, {}, {}, {}

    def adamw_big(n, halves):
        w, m, v = weights[n]
        g_, d_, m_, v_ = _adamw_halves(w[0], halves[0], halves[1], m[0], v[0], f"adamw_{n}")
        grads[n], delta[n], new_m[n], new_v[n] = g_[None], d_[None], m_[None], v_[None]

    dw_down = _matmul([(act, dff, TK)], "tn", [F32], name="mlp_down_dw")
    rs_down = _AsyncReduceScatter(dw_down.reshape(N_CHIPS, D_FF // N_CHIPS, D_MODEL), "w_down", 11)
    dw_up = _matmul([(u2, dhid, TK)], "tn", [F32], name="mlp_up_dw", deps=[dw_down], out_quarters=True)
    rs_up = _AsyncReduceScatter(dw_up, "w_up", 8)
    du2 = _matmul([(dhid, g_up, TK)], "nt", [F32], name="mlp_up_dx",
                  deps=[rs_down.sibling_sum(not_before=[dw_up])])
    dh1, dmix, d_g3, d_g2 = _mid_bwd(du2, h1, dh2, mix, norm_mix_post, norm_mlp_pre,
                                     deps=[rs_up.sibling_sum(not_before=[du2])])
    dymix = _matmul([(dmix, w_out_full, TK)], "nt", [F32], name="out_proj_dx")
    dw_out = _matmul([(y_mix, dmix, TK)], "tn", [F32], name="out_proj_dw")
    rs_out = _AsyncReduceScatter(dw_out.reshape(N_CHIPS, D_MIX // N_CHIPS, D_MODEL), "w_out", 5)
    dqkv = _attn_fused_bwd(qkv, dymix, y_att_f32, lse)
    par_late = _after(par, [rs_down.chip_sum(not_before=[dqkv]), rs_out.sibling_sum(not_before=[dymix])],
                      "after_w_down")
    dxc, dz, ddtg, dpar, d_nw = _ssd_bwd(xc, z, dtg, par_late, ssm_norm_w, y, states, dymix)
    g_down = rs_down.share()
    dxbc, dw8 = _conv_bwd(xbc, _after(w8, [*g_down, rs_up.chip_sum(not_before=[dxc])], "after_w_up"), dxc)
    ddt = jnp.pad(_dt_from_groups(ddtg), ((0, 0), (0, LANES - SSM_HEADS))).astype(BF16)
    g_up = rs_up.share()
    dw_z = _matmul([(u, dz, TK)], "tn", [F32], name="proj_z_dw")
    dw_xbc = _matmul([(u, dxbc, TK)], "tn", [F32], name="proj_xbc_dw",
                     deps=[*g_up, rs_out.chip_sum(not_before=[dxbc])])
    g_out = rs_out.share()
    dw_dt = _matmul([(u, ddt, TK)], "tn", [F32], name="proj_dt_dw")
    dw_qkv = _matmul([(u, dqkv, TK)], "tn", [F32], name="proj_qkv_dw")
    dw_in = _quarters_from_cols({"z": dw_z, "xbc": _unperm_cols(dw_xbc), "dt": dw_dt[:, :SSM_HEADS], "qkv": dw_qkv})
    rs_in = _AsyncReduceScatter(dw_in, "w_in", 2)
    adamw_big("w_down", g_down)
    adamw_big("w_up", g_up)
    rs_in.sibling_sum(not_before=[delta["w_up"]])
    du = _matmul([(dz, w_z, TK_MULTI), (dxbc, w_xbc, TK_MULTI), (dqkv, w_qkv, TK_MULTI), (ddt, w_dt, LANES)], "nt",
                 [F32], name="proj_dx", deps=[*g_out, rs_in.part_b])
    grad_x, d_g1 = _first_bwd(du, xs, dh1, norm_mix_pre)
    adamw_big("w_out", g_out)
    rs_in.chip_sum(not_before=[grad_x, delta["w_out"]])

    dconv = _unperm_cols(dw8)
    d_bias, d_alog, d_dskip = _unpack_ssd_params(dpar)
    small_shapes = [(1, D_MODEL), (CONV_WIDTH, D_XBC), (1, D_XBC), (1, SSM_HEADS), (1, SSM_HEADS), (1, SSM_HEADS),
                    (1, D_SSM), (1, D_MODEL), (1, D_MODEL), (1, D_MODEL), (1, LANES)]
    summed = _unpack_rows(
        _all_sum_small(_pack_rows([d_g1, dconv[:CONV_WIDTH], dconv[CONV_WIDTH:CONV_WIDTH + 1], d_bias, d_alog,
                                   d_dskip, d_nw, d_g2, d_g3, d_g4, loss_part])), small_shapes)
    (g_g1, g_conv_full, g_conv_b, g_bias, g_alog, g_dskip, g_nw, g_g2, g_g3, g_g4, loss_row) = summed
    loss = loss_row[0, 0]
    g_conv_w = lax.dynamic_slice(g_conv_full, (0, chip * conv_cols), (CONV_WIDTH, conv_cols))[None]

    grads.update({"norm_mix_pre": g_g1, "conv_w": g_conv_w, "conv_b": g_conv_b, "dt_bias": g_bias,
                  "a_log": g_alog, "d_skip": g_dskip, "ssm_norm_w": g_nw, "norm_mix_post": g_g2,
                  "norm_mlp_pre": g_g3, "norm_mlp_post": g_g4})
    order = list(weights)
    small_names = [n for n in order if n not in ("w_in", "w_out", "w_up", "w_down")]
    small_w_shapes = [weights[n][0].shape for n in small_names]
    packed = [_pack_rows([weights[n][k] for n in small_names]) for k in range(3)]
    packed_g = _pack_rows([grads[n].reshape(weights[n][0].shape) for n in small_names])
    sd, sm, sv = _adamw(packed[0], packed_g, packed[1], packed[2], "adamw_small")
    for k, n in enumerate(small_names):
        grads[n] = grads[n].reshape(weights[n][0].shape)
    for res, pk in ((delta, sd), (new_m, sm), (new_v, sv)):
        for n, val in zip(small_names, _unpack_rows(pk, small_w_shapes)):
            res[n] = val
    adamw_big("w_in", rs_in.share())

    return (loss, grad_x[None], *[grads[n] for n in order], *[delta[n] for n in order],
            *[new_m[n] for n in order], *[new_v[n] for n in order])
```

```python
import math

import numpy as np
import jax
import jax.numpy as jnp
from jax import lax
from jax.experimental import pallas as pl
from jax.experimental.pallas import tpu as pltpu
from jax.experimental.pallas import tpu_sc as plsc

F32 = jnp.float32
BF16 = jnp.bfloat16

D_MODEL = 2048
SSM_HEAD_DIM = 64
SSM_GROUPS = 8
HEADS_PER_GROUP = 4
SSM_HEADS = SSM_GROUPS * HEADS_PER_GROUP
D_SSM = SSM_HEADS * SSM_HEAD_DIM
D_STATE = 128
CONV_WIDTH = 4
SSD_CHUNK = 128
D_XBC = D_SSM + 2 * SSM_GROUPS * D_STATE
GROUP_X = HEADS_PER_GROUP * SSM_HEAD_DIM
GROUP_COLS = GROUP_X + 2 * D_STATE
ATT_HEAD_DIM = 128
ATT_HEADS = 16
D_ATT = ATT_HEADS * ATT_HEAD_DIM
DILATIONS = (1, 4, 16)
ATT_BLOCK = 128
D_MIX = D_SSM + D_ATT
D_IN_PROJ = D_SSM + D_XBC + SSM_HEADS + 3 * D_ATT
D_FF = 4 * D_MODEL
EPS = 1e-6
N_CHIPS = 4
W_IN_SHARD = D_IN_PROJ // N_CHIPS

ADAM_LR = 0.001
ADAM_B1 = 0.9
ADAM_B2 = 0.999
ADAM_EPS = 1e-08
ADAM_WD = 0.01
ADAM_STEP = 10

LANES = 128
VMEM_LIMIT = 48 * 1024 * 1024
MESH = pl.DeviceIdType.MESH

_NN = (((1,), (0,)), ((), ()))
_NT = (((1,), (1,)), ((), ()))
_TN = (((0,), (0,)), ((), ()))


def _dot(a, b, dims=_NN):
    return lax.dot_general(a, b, dims, preferred_element_type=F32)


def _cparams(*sem):
    return pltpu.CompilerParams(dimension_semantics=sem, vmem_limit_bytes=VMEM_LIMIT)


TK = 2048
TK_MULTI = 1024


def _matmul(pairs, mode, out_dtypes, *, name, tm=1024, tn=1024, epilogue=None, extras=(), deps=(), out_quarters=False):
    a0, b0, _ = pairs[0]
    m_dim = a0.shape[-1] if mode == "tn" else a0.shape[-2]
    if b0.ndim == 3:
        n_dim = b0.shape[1] if mode == "nt" else b0.shape[0] * b0.shape[2]
    else:
        n_dim = b0.shape[0] if mode == "nt" else b0.shape[1]
    tm, tn = min(tm, m_dim), min(tn, n_dim)
    nks, offs = [], []
    for a, _, tk in pairs:
        k_part = a.shape[0] if mode == "tn" else a.shape[-1]
        k_dim = k_part * (a.shape[0] if a.ndim == 3 else 1)
        assert k_part % tk == 0, (name, k_part, tk)
        offs.append(sum(nks))
        nks.append(k_dim // tk)
    nk_total = sum(nks)
    assert m_dim % tm == 0 and n_dim % tn == 0, (name, m_dim, n_dim)
    dims = {"nn": _NN, "nt": _NT, "tn": _TN}[mode]
    n_pairs, n_extra, n_out = len(pairs), len(extras), len(out_dtypes)

    in_specs, operands = [], []
    for (a, b, tk), off, nk in zip(pairs, offs, nks):
        def kidx(k, off=off, nk=nk):
            return k if n_pairs == 1 else jnp.clip(k - off, 0, nk - 1)
        if mode == "tn":
            assert a.ndim == 2
            in_specs.append(pl.BlockSpec((tk, tm), lambda m, n, k, f=kidx: (f(k), m)))
        elif a.ndim == 3:
            per = a.shape[2] // tk
            in_specs.append(pl.BlockSpec((None, tm, tk), lambda m, n, k, f=kidx, per=per: (f(k) // per, m, f(k) % per)))
        else:
            in_specs.append(pl.BlockSpec((tm, tk), lambda m, n, k, f=kidx: (m, f(k))))
        if b.ndim == 3 and mode == "nt":
            per = b.shape[2] // tk
            in_specs.append(pl.BlockSpec((None, tn, tk), lambda m, n, k, f=kidx, per=per: (f(k) // per, n, f(k) % per)))
        elif b.ndim == 3:
            per = b.shape[2] // tn
            in_specs.append(pl.BlockSpec((None, tk, tn), lambda m, n, k, f=kidx, per=per: (n // per, f(k), n % per)))
        elif mode == "nt":
            in_specs.append(pl.BlockSpec((tn, tk), lambda m, n, k, f=kidx: (n, f(k))))
        else:
            in_specs.append(pl.BlockSpec((tk, tn), lambda m, n, k, f=kidx: (f(k), n)))
        operands += [a, b]
    for e in extras:
        in_specs.append(pl.BlockSpec((tm, tn), lambda m, n, k: (m, n)))
        operands.append(e)
    in_specs += [pl.BlockSpec(memory_space=pl.ANY)] * len(deps)
    operands += list(deps)
    first_out = 2 * n_pairs + n_extra + len(deps)
    if out_quarters:
        out_per_q = n_dim // N_CHIPS // tn
        out_dims = (N_CHIPS, m_dim, n_dim // N_CHIPS)
        out_spec = pl.BlockSpec((None, tm, tn), lambda m, n, k: (n // out_per_q, m, n % out_per_q))
    else:
        out_dims = (m_dim, n_dim)
        out_spec = pl.BlockSpec((tm, tn), lambda m, n, k: (m, n))

    def body(*refs):
        ab = refs[:2 * n_pairs]
        e_refs = refs[2 * n_pairs:2 * n_pairs + n_extra]
        o_refs = refs[first_out:first_out + n_out]

        def finish(total):
            vals = (total,) if epilogue is None else epilogue(total, *[e[...] for e in e_refs])
            for o_ref, v in zip(o_refs, vals):
                o_ref[...] = v.astype(o_ref.dtype)

        if nk_total == 1:
            finish(_dot(ab[0][...], ab[1][...], dims))
            return
        acc = refs[-1]
        k = pl.program_id(2)

        @pl.when(k == 0)
        def _():
            acc[...] = jnp.zeros_like(acc)

        for i in range(n_pairs):
            def accumulate(i=i):
                acc[...] += _dot(ab[2 * i][...], ab[2 * i + 1][...], dims)
            if n_pairs == 1:
                accumulate()
            else:
                pl.when((k >= offs[i]) & (k < offs[i] + nks[i]))(accumulate)

        @pl.when(k == nk_total - 1)
        def _():
            finish(acc[...])

    outs = pl.pallas_call(
        body,
        grid=(m_dim // tm, n_dim // tn, nk_total),
        in_specs=in_specs,
        out_specs=[out_spec for _ in out_dtypes],
        out_shape=[jax.ShapeDtypeStruct(out_dims, dt) for dt in out_dtypes],
        scratch_shapes=[pltpu.VMEM((tm, tn), F32)] if nk_total > 1 else [],
        compiler_params=_cparams("parallel", "parallel", "arbitrary"),
        name=name,
    )(*operands)
    return outs[0] if n_out == 1 else outs


def _rowcall(fn, rows, vecs, row_outs, acc_widths, *, name, tr=256, row_cols=None, deps=()):
    s_dim = rows[0].shape[0]
    assert s_dim % tr == 0
    row_cols = row_cols or [None] * len(rows)
    n_r, n_v, n_ro, n_acc = len(rows), len(vecs), len(row_outs), len(acc_widths)
    in_specs = []
    for r, rc in zip(rows, row_cols):
        if rc is None:
            in_specs.append(pl.BlockSpec((tr, r.shape[1]), lambda i: (i, 0)))
        else:
            in_specs.append(pl.BlockSpec((tr, rc[0]), lambda i, c=rc[1]: (i, c)))
    for v in vecs:
        in_specs.append(pl.BlockSpec(v.shape, lambda i, nd=v.ndim: (0,) * nd))
    in_specs += [pl.BlockSpec(memory_space=pl.ANY)] * len(deps)
    n_d = len(deps)

    def body(*refs):
        ins = [r[...] for r in refs[:n_r + n_v]]
        ro = refs[n_r + n_v + n_d:n_r + n_v + n_d + n_ro]
        ao = refs[n_r + n_v + n_d + n_ro:]
        outs = fn(*ins)
        for ref, v in zip(ro, outs[:n_ro]):
            ref[...] = v.astype(ref.dtype)
        if n_acc:
            @pl.when(pl.program_id(0) == 0)
            def _():
                for ref in ao:
                    ref[...] = jnp.zeros_like(ref)
            for ref, v in zip(ao, outs[n_ro:]):
                ref[...] += v

    outs = pl.pallas_call(
        body,
        grid=(s_dim // tr,),
        in_specs=in_specs,
        out_specs=[pl.BlockSpec((tr, w), lambda i: (i, 0)) for w, _ in row_outs]
        + [pl.BlockSpec((1, w), lambda i: (0, 0)) for w in acc_widths],
        out_shape=[jax.ShapeDtypeStruct((s_dim, w), dt) for w, dt in row_outs]
        + [jax.ShapeDtypeStruct((1, w), F32) for w in acc_widths],
        compiler_params=_cparams("arbitrary"),
        name=name,
    )(*rows, *vecs, *deps)
    return outs


def _nrm(x, g):
    r = lax.rsqrt(jnp.mean(x * x, axis=-1, keepdims=True) + EPS)
    n = x * r
    return n * g, n, r


def _nrm_bwd(dy, n, r, g):
    dn = dy * g
    dx = r * (dn - n * jnp.mean(dn * n, axis=-1, keepdims=True))
    return dx, jnp.sum(dy * n, axis=0, keepdims=True)


def _sigmoid(x):
    return 1.0 / (1.0 + jnp.exp(-x))


def _softplus(x):
    return jnp.maximum(x, 0.0) + jnp.log(1.0 + jnp.exp(-jnp.abs(x)))


def _pre_norm(x, g1):
    def fn(xb, g):
        return (_nrm(xb, g)[0],)
    return _rowcall(fn, [x], [g1], [(D_MODEL, BF16)], [], name="pre_norm")[0]


def _post_pre_norm(x, mix, g2, g3):
    def fn(xb, mb, g2b, g3b):
        h1 = xb + _nrm(mb, g2b)[0]
        return h1, _nrm(h1, g3b)[0]
    return _rowcall(fn, [x, mix], [g2, g3], [(D_MODEL, F32), (D_MODEL, BF16)], [], name="post_pre_norm")


def _tail(ff, h1, target, g4):
    def fn(ffb, h1b, tb, g):
        y, n, r = _nrm(ffb, g)
        e = h1b + y - tb
        loss = 0.5 * jnp.sum(jnp.sum(e * e, axis=-1, keepdims=True) * (1.0 / D_MODEL), axis=0, keepdims=True)
        dh2 = e * (1.0 / D_MODEL)
        dff, dg = _nrm_bwd(dh2, n, r, g)
        return dh2, dff, dg, jnp.broadcast_to(loss, (1, LANES))
    return _rowcall(fn, [ff, h1, target], [g4], [(D_MODEL, F32), (D_MODEL, BF16)], [D_MODEL, LANES], name="tail")


def _mid_bwd(du2, h1, dh2, mix, g2, g3, deps=()):
    def fn(du2b, h1b, dh2b, mb, g2b, g3b):
        _, n3, r3 = _nrm(h1b, g3b)
        d3, dg3 = _nrm_bwd(du2b, n3, r3, g3b)
        dh1 = dh2b + d3
        _, n2, r2 = _nrm(mb, g2b)
        dmix, dg2 = _nrm_bwd(dh1, n2, r2, g2b)
        return dh1, dmix, dg3, dg2
    return _rowcall(fn, [du2, h1, dh2, mix], [g2, g3], [(D_MODEL, F32), (D_MODEL, BF16)], [D_MODEL, D_MODEL],
                    name="mid_bwd", deps=deps)


def _first_bwd(du, x, dh1, g1):
    def fn(dub, xb, dh1b, g):
        _, n, r = _nrm(xb, g)
        dx, dg = _nrm_bwd(dub, n, r, g)
        return dh1b + dx, dg
    return _rowcall(fn, [du, x, dh1], [g1], [(D_MODEL, F32)], [D_MODEL], name="first_bwd")


CONV_TILE = 256
CONV_ROWS = 256
PAD = 8


def _conv_taps(w):
    return [w[k:k + 1, :] for k in range(CONV_WIDTH)], w[CONV_WIDTH:CONV_WIDTH + 1, :]


def _conv_fwd(xbc, w8):
    s_dim, c_dim = xbc.shape
    n_steps = s_dim // CONV_ROWS

    def body(x_ref, w_ref, o_ref, xp):
        xp[0:PAD, :] = jnp.zeros((PAD, CONV_TILE), F32)
        xp[PAD:PAD + s_dim, :] = x_ref[...]
        taps, bias = _conv_taps(w_ref[...])

        def step(c, carry):
            base = pl.multiple_of(c * CONV_ROWS, CONV_ROWS)
            win = xp[pl.ds(base, CONV_ROWS + PAD), :]
            pre = bias + taps[3] * win[PAD:, :]
            for j in range(1, CONV_WIDTH):
                pre = pre + taps[3 - j] * pltpu.roll(win, j, axis=0)[PAD:, :]
            o_ref[pl.ds(base, CONV_ROWS), :] = pre * _sigmoid(pre)
            return carry

        lax.fori_loop(0, n_steps, step, 0)

    return pl.pallas_call(
        body,
        grid=(c_dim // CONV_TILE,),
        in_specs=[pl.BlockSpec((s_dim, CONV_TILE), lambda j: (0, j)), pl.BlockSpec((8, CONV_TILE), lambda j: (0, j))],
        out_specs=pl.BlockSpec((s_dim, CONV_TILE), lambda j: (0, j)),
        out_shape=jax.ShapeDtypeStruct((s_dim, c_dim), F32),
        scratch_shapes=[pltpu.VMEM((s_dim + 2 * PAD, CONV_TILE), F32)],
        compiler_params=_cparams("parallel"),
        name="conv_fwd",
    )(xbc, w8)


def _conv_bwd(xbc, w8, dxc):
    s_dim, c_dim = xbc.shape
    n_steps = s_dim // CONV_ROWS

    def body(x_ref, w_ref, d_ref, dx_ref, dw_ref, xp, dp):
        xp[0:PAD, :] = jnp.zeros((PAD, CONV_TILE), F32)
        xp[PAD:PAD + s_dim, :] = x_ref[...]
        dp[PAD + s_dim:, :] = jnp.zeros((PAD, CONV_TILE), F32)
        taps, bias = _conv_taps(w_ref[...])

        def step1(c, sums):
            base = pl.multiple_of(c * CONV_ROWS, CONV_ROWS)
            win = xp[pl.ds(base, CONV_ROWS + PAD), :]
            shifted = [win[PAD:, :]] + [pltpu.roll(win, j, axis=0)[PAD:, :] for j in range(1, CONV_WIDTH)]
            pre = bias
            for j in range(CONV_WIDTH):
                pre = pre + taps[3 - j] * shifted[j]
            sg = _sigmoid(pre)
            dpre = d_ref[pl.ds(base, CONV_ROWS), :] * (sg * (1.0 + pre * (1.0 - sg)))
            dp[pl.ds(base + PAD, CONV_ROWS), :] = dpre
            new = [sums[k] + jnp.sum(dpre * shifted[3 - k], axis=0, keepdims=True) for k in range(CONV_WIDTH)]
            new.append(sums[CONV_WIDTH] + jnp.sum(dpre, axis=0, keepdims=True))
            return tuple(new)

        zero = jnp.zeros((1, CONV_TILE), F32)
        sums = lax.fori_loop(0, n_steps, step1, (zero,) * (CONV_WIDTH + 1))
        dw_ref[...] = jnp.zeros((8, CONV_TILE), F32)
        for k in range(CONV_WIDTH + 1):
            dw_ref[k:k + 1, :] = sums[k]

        def step2(c, carry):
            base = pl.multiple_of(c * CONV_ROWS, CONV_ROWS)
            win = dp[pl.ds(base + PAD, CONV_ROWS + PAD), :]
            dx = taps[3] * win[:CONV_ROWS, :]
            for j in range(1, CONV_WIDTH):
                dx = dx + taps[3 - j] * pltpu.roll(win, CONV_ROWS + PAD - j, axis=0)[:CONV_ROWS, :]
            dx_ref[pl.ds(base, CONV_ROWS), :] = dx.astype(BF16)
            return carry

        lax.fori_loop(0, n_steps, step2, 0)

    col = lambda j: (0, j)
    return pl.pallas_call(
        body,
        grid=(c_dim // CONV_TILE,),
        in_specs=[pl.BlockSpec((s_dim, CONV_TILE), col), pl.BlockSpec((8, CONV_TILE), col),
                  pl.BlockSpec((s_dim, CONV_TILE), col)],
        out_specs=[pl.BlockSpec((s_dim, CONV_TILE), col), pl.BlockSpec((8, CONV_TILE), col)],
        out_shape=[jax.ShapeDtypeStruct((s_dim, c_dim), BF16), jax.ShapeDtypeStruct((8, c_dim), F32)],
        scratch_shapes=[pltpu.VMEM((s_dim + 2 * PAD, CONV_TILE), F32), pltpu.VMEM((s_dim + 2 * PAD, CONV_TILE), F32)],
        compiler_params=_cparams("parallel"),
        name="conv_bwd",
    )(xbc, w8, dxc)


def _perm_cols(a):
    parts = []
    for g in range(SSM_GROUPS):
        parts += [a[..., g * GROUP_X:(g + 1) * GROUP_X],
                  a[..., D_SSM + g * D_STATE:D_SSM + (g + 1) * D_STATE],
                  a[..., D_SSM + SSM_GROUPS * D_STATE + g * D_STATE:D_SSM + SSM_GROUPS * D_STATE + (g + 1) * D_STATE]]
    return jnp.concatenate(parts, axis=-1)


def _unperm_cols(a):
    xs = [a[..., g * GROUP_COLS:g * GROUP_COLS + GROUP_X] for g in range(SSM_GROUPS)]
    bs = [a[..., g * GROUP_COLS + GROUP_X:g * GROUP_COLS + GROUP_X + D_STATE] for g in range(SSM_GROUPS)]
    cs = [a[..., g * GROUP_COLS + GROUP_X + D_STATE:(g + 1) * GROUP_COLS] for g in range(SSM_GROUPS)]
    return jnp.concatenate(xs + bs + cs, axis=-1)


def _dt_to_groups(dt):
    s_dim = dt.shape[0]
    t = dt[:, :SSM_HEADS].reshape(s_dim, SSM_GROUPS, HEADS_PER_GROUP).transpose(1, 0, 2)
    return jnp.pad(t, ((0, 0), (0, 0), (0, LANES - HEADS_PER_GROUP)))


def _dt_from_groups(dtg):
    s_dim = dtg.shape[1]
    return dtg[:, :, :HEADS_PER_GROUP].transpose(1, 0, 2).reshape(s_dim, SSM_HEADS)


def _pack_ssd_params(dt_bias, a_log, d_skip):
    rows = jnp.stack([p.reshape(SSM_GROUPS, HEADS_PER_GROUP) for p in (dt_bias, a_log, d_skip)], axis=1)
    return jnp.pad(rows, ((0, 0), (0, 8 - 3), (0, LANES - HEADS_PER_GROUP)))


def _unpack_ssd_params(par):
    return tuple(par[:, k, :HEADS_PER_GROUP].reshape(SSM_HEADS) for k in range(3))


Q = SSD_CHUNK


def _split3(v):
    hi = v.astype(BF16)
    r1 = v - hi.astype(F32)
    mid = r1.astype(BF16)
    lo = (r1 - mid.astype(F32)).astype(BF16)
    return hi, mid, lo


def _dot_l01(t01, v):
    return sum(_dot(t01, p) for p in _split3(v))


def _dot_r01(v, e01):
    return sum(_dot(p, e01) for p in _split3(v))


def _ssd_consts():
    row = lax.broadcasted_iota(jnp.int32, (Q, Q), 0)
    col = lax.broadcasted_iota(jnp.int32, (Q, Q), 1)
    causal = row >= col
    tril = causal.astype(BF16)
    triu = (col >= row).astype(BF16)
    er = lax.broadcasted_iota(jnp.int32, (LANES, GROUP_X), 0)
    ec = lax.broadcasted_iota(jnp.int32, (LANES, GROUP_X), 1) // SSM_HEAD_DIM
    expand = (er == ec).astype(BF16)
    rr = lax.broadcasted_iota(jnp.int32, (GROUP_X, LANES), 0) // SSM_HEAD_DIM
    rc = lax.broadcasted_iota(jnp.int32, (GROUP_X, LANES), 1)
    reduce = (rr == rc).astype(BF16)
    lane_head = lax.broadcasted_iota(jnp.int32, (Q, GROUP_X), 1) // SSM_HEAD_DIM
    return causal, tril, triu, expand, reduce, lane_head


def _ssd_common(xc_ref, dt_ref, par_ref, consts):
    causal, tril, _, expand, _, _ = consts
    par = par_ref[...]
    bias, alog, dsk = par[0:1, :], par[1:2, :], par[2:3, :]
    a_neg = -jnp.exp(alog)
    dtr = dt_ref[...] + bias
    dt = _softplus(dtr)
    s = _dot_l01(tril, dt * a_neg)
    dt_x = _dot_r01(dt, expand)
    s_x = _dot_r01(s, expand)
    dsk_x = _dot_r01(jnp.broadcast_to(dsk, (8, LANES)), expand)[0:1, :]
    blk = xc_ref[...]
    x = blk[:, :GROUP_X]
    bm = blk[:, GROUP_X:GROUP_X + D_STATE].astype(BF16)
    cm = blk[:, GROUP_X + D_STATE:].astype(BF16)
    xdt = x * dt_x
    g = _dot(cm, bm, _NT)
    return dict(a_neg=a_neg, dtr=dtr, dt=dt, s=s, s_t=s.T, dt_x=dt_x, s_x=s_x, dsk_x=dsk_x, x=x, bm=bm, cm=cm,
                xdt=xdt, g=g)


def _decay(v, r, causal):
    diff = v["s"][:, r:r + 1] - v["s_t"][r:r + 1, :]
    return jnp.exp(jnp.where(causal, diff, -jnp.inf))


def _ssd_specs(n_chunks, rev):
    cidx = (lambda c: n_chunks - 1 - c) if rev else (lambda c: c)
    xc = pl.BlockSpec((Q, GROUP_COLS), lambda g, c: (cidx(c), g))
    gx = pl.BlockSpec((Q, GROUP_X), lambda g, c: (cidx(c), g))
    dt = pl.BlockSpec((None, Q, LANES), lambda g, c: (g, cidx(c), 0))
    par = pl.BlockSpec((None, 8, LANES), lambda g, c: (g, 0, 0))
    nw = pl.BlockSpec((1, GROUP_X), lambda g, c: (0, g))
    hs = pl.BlockSpec((None, None, D_STATE, GROUP_X), lambda g, c: (cidx(c), g, 0, 0))
    return xc, gx, dt, par, nw, hs


def _ssd_fwd(xc, z, dtg, par, nw):
    s_dim = xc.shape[0]
    n_chunks = s_dim // Q
    xc_s, gx_s, dt_s, par_s, nw_s, hs_s = _ssd_specs(n_chunks, False)

    def body(xc_ref, z_ref, dt_ref, par_ref, nw_ref, y_ref, ys_ref, hs_ref, ht):
        @pl.when(pl.program_id(1) == 0)
        def _():
            ht[...] = jnp.zeros_like(ht)

        consts = _ssd_consts()
        causal, lane_head = consts[0], consts[5]
        v = _ssd_common(xc_ref, dt_ref, par_ref, consts)
        xdt_b = v["xdt"].astype(BF16)
        yd = jnp.zeros((Q, GROUP_X), F32)
        for r in range(HEADS_PER_GROUP):
            m = (v["g"] * _decay(v, r, causal)).astype(BF16)
            yd = yd + _dot(m, jnp.where(lane_head == r, xdt_b, jnp.zeros_like(xdt_b)))
        h = ht[...]
        hs_ref[...] = h
        yo = jnp.exp(v["s_x"]) * _dot(v["cm"], h.astype(BF16))
        y = yd + yo + v["dsk_x"] * v["x"]
        s_last = v["s_x"][Q - 1:Q, :]
        snew = _dot(v["bm"], (v["xdt"] * jnp.exp(s_last - v["s_x"])).astype(BF16), _TN)
        ht[...] = jnp.exp(s_last) * h + snew
        zz = z_ref[...]
        yg = y * (zz * _sigmoid(zz))
        y_ref[...] = y
        ys_ref[...] = _nrm(yg, nw_ref[...])[0].astype(BF16)

    return pl.pallas_call(
        body,
        grid=(SSM_GROUPS, n_chunks),
        in_specs=[xc_s, gx_s, dt_s, par_s, nw_s],
        out_specs=[gx_s, gx_s, hs_s],
        out_shape=[jax.ShapeDtypeStruct((s_dim, D_SSM), F32), jax.ShapeDtypeStruct((s_dim, D_SSM), BF16),
                   jax.ShapeDtypeStruct((n_chunks, SSM_GROUPS, D_STATE, GROUP_X), F32)],
        scratch_shapes=[pltpu.VMEM((D_STATE, GROUP_X), F32)],
        compiler_params=_cparams("parallel", "arbitrary"),
        name="ssd_fwd",
    )(xc, z, dtg, par, nw)


def _ssd_bwd(xc, z, dtg, par, nw, y, hs, dymix):
    s_dim = xc.shape[0]
    n_chunks = s_dim // Q
    xc_s, gx_s, dt_s, par_s, nw_s, hs_s = _ssd_specs(n_chunks, True)

    def body(xc_ref, z_ref, dt_ref, par_ref, nw_ref, y_ref, hs_ref, dys_ref,
             dxc_ref, dz_ref, ddt_ref, dpar_ref, dnw_ref, dht):
        @pl.when(pl.program_id(1) == 0)
        def _():
            dht[...] = jnp.zeros_like(dht)
            dpar_ref[...] = jnp.zeros_like(dpar_ref)
            dnw_ref[...] = jnp.zeros_like(dnw_ref)

        consts = _ssd_consts()
        causal, _, triu, _, reduce, lane_head = consts
        v = _ssd_common(xc_ref, dt_ref, par_ref, consts)
        x, bm, cm, xdt, s_x = v["x"], v["bm"], v["cm"], v["xdt"], v["s_x"]
        h = hs_ref[...]
        hb = h.astype(BF16)
        es_x = jnp.exp(s_x)
        yo = es_x * _dot(cm, hb)
        s_last = s_x[Q - 1:Q, :]
        e_x = jnp.exp(s_last - s_x)
        es_last = jnp.exp(s_last)

        yv, zz, nw_v = y_ref[...], z_ref[...], nw_ref[...]
        sg = _sigmoid(zz)
        gz = zz * sg
        _, n, rstd = _nrm(yv * gz, nw_v)
        dout = dys_ref[...]
        dyg, dnw = _nrm_bwd(dout, n, rstd, nw_v)
        dnw_ref[...] += dnw
        dy = dyg * gz
        dz_ref[...] = (dyg * yv * (sg * (1.0 + zz * (1.0 - sg)))).astype(BF16)

        dyb = dy.astype(BF16)
        xdt_b = xdt.astype(BF16)
        dhp = dht[...]
        dhpb = dhp.astype(BF16)
        lane = lax.broadcasted_iota(jnp.int32, (Q, LANES), 1)
        sub = lax.broadcasted_iota(jnp.int32, (LANES, Q), 0)
        dxdt = jnp.zeros((Q, GROUP_X), F32)
        dg = jnp.zeros((Q, Q), F32)
        ds = jnp.zeros((Q, LANES), F32)
        ds_t = jnp.zeros((LANES, Q), F32)
        for r in range(HEADS_PER_GROUP):
            dec = _decay(v, r, causal)
            mf = v["g"] * dec
            dyr = jnp.where(lane_head == r, dyb, jnp.zeros_like(dyb))
            dm = _dot(dyr, xdt_b, _NT)
            dxdt = dxdt + _dot(mf.astype(BF16), dyr, _TN)
            dg = dg + dm * dec
            dd = dm * mf
            ds = ds + jnp.where(lane == r, jnp.sum(dd, axis=1, keepdims=True), 0.0)
            ds_t = ds_t + jnp.where(sub == r, jnp.sum(dd, axis=0, keepdims=True), 0.0)
        ds = ds - ds_t.T
        dgb = dg.astype(BF16)
        dwb = (es_x * dy).astype(BF16)
        dcm = _dot(dgb, bm) + _dot(dwb, hb, _NT)
        dh_prev = _dot(cm, dwb, _TN)
        zst = _dot(bm, dhpb)
        xe = xdt * e_x
        dxdt = dxdt + e_x * zst
        dee = xe * zst
        dbm = _dot(dgb, cm, _TN) + _dot(xe.astype(BF16), dhpb, _NT)
        v_last = jnp.sum(dee, axis=0, keepdims=True) + es_last * jnp.sum(dhp * h, axis=0, keepdims=True)
        row_x = lax.broadcasted_iota(jnp.int32, (Q, GROUP_X), 0)
        tx = dy * yo - dee + jnp.where(row_x == Q - 1, v_last, 0.0)
        ds = ds + _dot_r01(tx, reduce)
        ddta = _dot_l01(triu, ds)
        ddt = ddta * v["a_neg"] + _dot_r01(dxdt * x, reduce)
        dalog = jnp.sum(ddta * v["dt"], axis=0, keepdims=True) * v["a_neg"]
        draw = jnp.where(lane < HEADS_PER_GROUP, ddt * _sigmoid(v["dtr"]), 0.0)
        dbias = jnp.sum(draw, axis=0, keepdims=True)
        ddsk = _dot_r01(jnp.broadcast_to(jnp.sum(dy * x, axis=0, keepdims=True), (8, GROUP_X)), reduce)[0:1, :]
        dht[...] = es_last * dhp + dh_prev
        dxc_ref[:, :GROUP_X] = dxdt * v["dt_x"] + v["dsk_x"] * dy
        dxc_ref[:, GROUP_X:GROUP_X + D_STATE] = dbm
        dxc_ref[:, GROUP_X + D_STATE:] = dcm
        ddt_ref[...] = draw
        dpar_ref[0:1, :] += dbias
        dpar_ref[1:2, :] += dalog
        dpar_ref[2:3, :] += ddsk

    return pl.pallas_call(
        body,
        grid=(SSM_GROUPS, n_chunks),
        in_specs=[xc_s, gx_s, dt_s, par_s, nw_s, gx_s, hs_s, gx_s],
        out_specs=[xc_s, gx_s, dt_s, par_s, nw_s],
        out_shape=[jax.ShapeDtypeStruct((s_dim, SSM_GROUPS * GROUP_COLS), F32),
                   jax.ShapeDtypeStruct((s_dim, D_SSM), BF16),
                   jax.ShapeDtypeStruct((SSM_GROUPS, s_dim, LANES), F32),
                   jax.ShapeDtypeStruct((SSM_GROUPS, 8, LANES), F32),
                   jax.ShapeDtypeStruct((1, D_SSM), F32)],
        scratch_shapes=[pltpu.VMEM((D_STATE, GROUP_X), F32)],
        compiler_params=_cparams("parallel", "arbitrary"),
        name="ssd_bwd",
    )(xc, z, dtg, par, nw, y, hs, dymix)


ATT_SCALE = ATT_HEAD_DIM ** -0.5
NEG_INF = -jnp.inf


def _band_masks():
    qi = lax.broadcasted_iota(jnp.int32, (ATT_BLOCK, ATT_BLOCK), 0)
    kj = lax.broadcasted_iota(jnp.int32, (ATT_BLOCK, ATT_BLOCK), 1)
    return kj <= qi, kj >= qi


WIN = ATT_BLOCK * DILATIONS[-1]
N_BLOCKS = WIN // ATT_BLOCK


def _rows(start, d):
    return pl.ds(start, ATT_BLOCK) if d == 1 else pl.ds(start, ATT_BLOCK, stride=d)


def _block_start(idx, d):
    return (idx // d) * (ATT_BLOCK * d) + idx % d


def _lane_bcast(col):
    return jnp.broadcast_to(col, (col.shape[0], LANES))


def _attn_fused_fwd(qkv):
    s_dim = qkv.shape[0]
    n_win = s_dim // WIN
    blk = (WIN, ATT_HEAD_DIM)
    prev = lambda w: jnp.maximum(w - 1, 0)

    def body(q_ref, kc_ref, kp_ref, vc_ref, vp_ref, y_ref, yf_ref, lse_ref, qf, kf, vf, acc, m_run, l_run):
        w, h = pl.program_id(0), pl.program_id(1)
        qf[...] = q_ref[...].astype(F32)
        kf[0:WIN, :] = kp_ref[...].astype(F32)
        kf[WIN:, :] = kc_ref[...].astype(F32)
        vf[0:WIN, :] = vp_ref[...].astype(F32)
        vf[WIN:, :] = vc_ref[...].astype(F32)
        own, before = _band_masks()

        for d in DILATIONS:
            def block(idx, carry, d=d):
                start = _block_start(idx, d)
                rows = _rows(start, d)
                q = qf[rows, :].astype(BF16)
                kc, vc = kf[_rows(WIN + start, d), :].astype(BF16), vf[_rows(WIN + start, d), :].astype(BF16)
                kp = kf[_rows(WIN + start - ATT_BLOCK * d, d), :].astype(BF16)
                vp = vf[_rows(WIN + start - ATT_BLOCK * d, d), :].astype(BF16)
                has_prev = (idx >= d) | (w > 0)
                sc = jnp.where(own, _dot(q, kc, _NT) * ATT_SCALE, NEG_INF)
                sp = jnp.where(before & has_prev, _dot(q, kp, _NT) * ATT_SCALE, NEG_INF)
                m_blk = jnp.maximum(jnp.max(sc, axis=1, keepdims=True), jnp.max(sp, axis=1, keepdims=True))
                if d == DILATIONS[0]:
                    m_new = m_blk
                else:
                    m_old = m_run[rows, :][:, 0:1]
                    m_new = jnp.maximum(m_old, m_blk)
                pc, pp = jnp.exp(sc - m_new), jnp.exp(sp - m_new)
                l_new = jnp.sum(pc, axis=1, keepdims=True) + jnp.sum(pp, axis=1, keepdims=True)
                o_new = _dot(pc.astype(BF16), vc) + _dot(pp.astype(BF16), vp)
                if d != DILATIONS[0]:
                    alpha = jnp.exp(m_old - m_new)
                    l_new = alpha * l_run[rows, :][:, 0:1] + l_new
                    o_new = alpha * acc[rows, :] + o_new
                m_run[rows, :] = _lane_bcast(m_new)
                l_run[rows, :] = _lane_bcast(l_new)
                acc[rows, :] = o_new
                return carry

            for idx in range(N_BLOCKS):
                block(idx, 0)

        l_all = l_run[...]
        y = acc[...] / l_all
        y_ref[...] = y.astype(BF16)
        yf_ref[...] = y
        @pl.when(h == 0)
        def _():
            lse_ref[...] = jnp.zeros_like(lse_ref)

        lane = lax.broadcasted_iota(jnp.int32, (WIN, LANES), 1)
        lse_ref[...] = jnp.where(lane == h, m_run[...] + jnp.log(l_all), lse_ref[...])

    win_scratch = lambda rows: pltpu.VMEM((rows, ATT_HEAD_DIM), F32)
    return pl.pallas_call(
        body,
        grid=(n_win, ATT_HEADS),
        in_specs=[pl.BlockSpec(blk, lambda w, h: (w, h)),
                  pl.BlockSpec(blk, lambda w, h: (w, ATT_HEADS + h)),
                  pl.BlockSpec(blk, lambda w, h: (prev(w), ATT_HEADS + h)),
                  pl.BlockSpec(blk, lambda w, h: (w, 2 * ATT_HEADS + h)),
                  pl.BlockSpec(blk, lambda w, h: (prev(w), 2 * ATT_HEADS + h))],
        out_specs=[pl.BlockSpec(blk, lambda w, h: (w, h)), pl.BlockSpec(blk, lambda w, h: (w, h)),
                   pl.BlockSpec((WIN, LANES), lambda w, h: (w, 0))],
        out_shape=[jax.ShapeDtypeStruct((s_dim, D_ATT), BF16), jax.ShapeDtypeStruct((s_dim, D_ATT), F32),
                   jax.ShapeDtypeStruct((s_dim, LANES), F32)],
        scratch_shapes=[win_scratch(WIN), win_scratch(2 * WIN), win_scratch(2 * WIN), win_scratch(WIN),
                        win_scratch(WIN), win_scratch(WIN)],
        compiler_params=_cparams("parallel", "arbitrary"),
        name="attn_fused_fwd",
    )(qkv, qkv, qkv, qkv, qkv)


def _attn_fused_bwd(qkv, dymix, y_att, lse, deps=()):
    s_dim = qkv.shape[0]
    n_win = s_dim // WIN
    blk = (WIN, ATT_HEAD_DIM)
    prev = lambda w: jnp.maximum(w - 1, 0)
    nxt = lambda w: jnp.minimum(w + 1, n_win - 1)
    n_dep = len(deps)

    def body(qc_ref, qn_ref, kc_ref, kp_ref, vc_ref, vp_ref, dyc_ref, dyn_ref, yc_ref, yn_ref, lc_ref, ln_ref, *rest):
        out_ref = rest[n_dep]
        qf, qnf, kf, vf, dq_acc, dk_acc, dv_acc, ls_c, dl_c, ls_n, dl_n = rest[n_dep + 1:]
        w, h = pl.program_id(0), pl.program_id(1)
        qf[...] = qc_ref[...].astype(F32)
        qnf[...] = qn_ref[...].astype(F32)
        kf[0:WIN, :] = kp_ref[...].astype(F32)
        kf[WIN:, :] = kc_ref[...].astype(F32)
        vf[0:WIN, :] = vp_ref[...].astype(F32)
        vf[WIN:, :] = vc_ref[...].astype(F32)
        lane = lax.broadcasted_iota(jnp.int32, (WIN, LANES), 1)
        pick = lambda ref: _lane_bcast(jnp.sum(jnp.where(lane == h, ref[...], 0.0), axis=1, keepdims=True))
        ls_c[...] = pick(lc_ref)
        ls_n[...] = pick(ln_ref)
        dl_c[...] = _lane_bcast(jnp.sum(dyc_ref[...] * yc_ref[...], axis=1, keepdims=True))
        dl_n[...] = _lane_bcast(jnp.sum(dyn_ref[...] * yn_ref[...], axis=1, keepdims=True))
        for ref in (dq_acc, dk_acc, dv_acc):
            ref[...] = jnp.zeros_like(ref)
        own, before = _band_masks()

        def probs(q, k, v, dy, lse_col, dl_col, mask):
            p = jnp.exp(jnp.where(mask, _dot(q, k, _NT) * ATT_SCALE - lse_col, NEG_INF))
            ds = p * (_dot(dy, v, _NT) - dl_col)
            return p.astype(BF16), ds.astype(BF16)

        for d in DILATIONS:
            def block(idx, carry, d=d):
                start = _block_start(idx, d)
                rows = _rows(start, d)
                q, dy = qf[rows, :].astype(BF16), dyc_ref[rows, :].astype(BF16)
                lse_col, dl_col = ls_c[rows, :][:, 0:1], dl_c[rows, :][:, 0:1]
                kc, vc = kf[_rows(WIN + start, d), :].astype(BF16), vf[_rows(WIN + start, d), :].astype(BF16)
                kp = kf[_rows(WIN + start - ATT_BLOCK * d, d), :].astype(BF16)
                vp = vf[_rows(WIN + start - ATT_BLOCK * d, d), :].astype(BF16)
                pc, dsc = probs(q, kc, vc, dy, lse_col, dl_col, own)
                pp, dsp = probs(q, kp, vp, dy, lse_col, dl_col, before & ((idx >= d) | (w > 0)))
                dq_acc[rows, :] += (_dot(dsc, kc) + _dot(dsp, kp)) * ATT_SCALE
                dk_acc[rows, :] += _dot(dsc, q, _TN) * ATT_SCALE
                dv_acc[rows, :] += _dot(pc, dy, _TN)

                if idx >= d:
                    prows = _rows(start - ATT_BLOCK * d, d)
                    dk_acc[prows, :] += _dot(dsp, q, _TN) * ATT_SCALE
                    dv_acc[prows, :] += _dot(pp, dy, _TN)
                return carry

            for idx in range(N_BLOCKS):
                block(idx, 0)

            def next_window(r, carry, d=d):
                krows = _rows(WIN - ATT_BLOCK * d + r, d)
                rows = _rows(r, d)
                q, dy = qnf[rows, :].astype(BF16), dyn_ref[rows, :].astype(BF16)
                k, v = kf[_rows(2 * WIN - ATT_BLOCK * d + r, d), :].astype(BF16), vf[_rows(2 * WIN - ATT_BLOCK * d + r, d), :].astype(BF16)
                pn, dsn = probs(q, k, v, dy, ls_n[rows, :][:, 0:1], dl_n[rows, :][:, 0:1], before & (w < n_win - 1))
                dk_acc[krows, :] += _dot(dsn, q, _TN) * ATT_SCALE
                dv_acc[krows, :] += _dot(pn, dy, _TN)
                return carry

            for r in range(d):
                next_window(r, 0)

        for part, acc_ref in enumerate((dq_acc, dk_acc, dv_acc)):
            out_ref[part] = acc_ref[...].astype(BF16)

    win_scratch = lambda rows: pltpu.VMEM((rows, ATT_HEAD_DIM), F32)
    cur = lambda c: pl.BlockSpec(blk, lambda w, h: (w, c + h))
    return pl.pallas_call(
        body,
        grid=(n_win, ATT_HEADS),
        in_specs=[cur(0), pl.BlockSpec(blk, lambda w, h: (nxt(w), h)),
                  cur(ATT_HEADS), pl.BlockSpec(blk, lambda w, h: (prev(w), ATT_HEADS + h)),
                  cur(2 * ATT_HEADS), pl.BlockSpec(blk, lambda w, h: (prev(w), 2 * ATT_HEADS + h)),
                  cur(ATT_HEADS), pl.BlockSpec(blk, lambda w, h: (nxt(w), ATT_HEADS + h)),
                  cur(0), pl.BlockSpec(blk, lambda w, h: (nxt(w), h)),
                  pl.BlockSpec((WIN, LANES), lambda w, h: (w, 0)), pl.BlockSpec((WIN, LANES), lambda w, h: (nxt(w), 0))]
        + [ANY] * n_dep,
        out_specs=pl.BlockSpec((3, WIN, ATT_HEAD_DIM), lambda w, h: (0, w, h)),
        out_shape=jax.ShapeDtypeStruct((3, s_dim, D_ATT), BF16),
        scratch_shapes=[win_scratch(WIN), win_scratch(WIN), win_scratch(2 * WIN), win_scratch(2 * WIN)]
        + [win_scratch(WIN)] * 7,
        compiler_params=_cparams("parallel", "arbitrary"),
        name="attn_fused_bwd",
    )(qkv, qkv, qkv, qkv, qkv, qkv, dymix, dymix, y_att, y_att, lse, lse, *deps)


def _adamw(w, g, m, v, name):
    def fn(wb, gb, mb, vb):
        m2 = ADAM_B1 * mb + (1.0 - ADAM_B1) * gb
        v2 = ADAM_B2 * vb + (1.0 - ADAM_B2) * (gb * gb)
        m_hat = m2 / (1.0 - ADAM_B1 ** ADAM_STEP)
        v_hat = v2 / (1.0 - ADAM_B2 ** ADAM_STEP)
        delta = -ADAM_LR * (m_hat / (jnp.sqrt(v_hat) + ADAM_EPS) + ADAM_WD * wb)
        return delta, m2, v2
    cols = w.shape[1]
    tr = 128 if w.shape[0] % 128 == 0 else w.shape[0]
    return _rowcall(fn, [w, g, m, v], [], [(cols, F32)] * 3, [], name=name, tr=tr)


ANY = pl.BlockSpec(memory_space=pl.ANY)


def _position():
    x, y, c = lax.axis_index("x"), lax.axis_index("y"), lax.axis_index("c")
    chips = [(1 - x, y), (x, 1 - y), (1 - x, 1 - y)]
    return x, y, c, chips


def _remote(src, dst, send_sem, recv_sem, device):
    return pltpu.make_async_remote_copy(src_ref=src, dst_ref=dst, send_sem=send_sem, recv_sem=recv_sem,
                                        device_id=device, device_id_type=MESH)


def _handshake(peers):
    barrier = pltpu.get_barrier_semaphore()
    for p in peers:
        pl.semaphore_signal(barrier, inc=1, device_id=p, device_id_type=MESH)
    pl.semaphore_wait(barrier, len(peers))


def _gather_shards_async(shards, collective_id, name):
    n = len(shards)
    srcs = [jax.new_ref(s, memory_space=pltpu.MemorySpace.HBM) for s in shards]
    dsts = [jax.empty_ref(jax.ShapeDtypeStruct((N_CHIPS,) + s.shape, s.dtype), memory_space=pltpu.MemorySpace.HBM)
            for s in shards]

    @pl.kernel(mesh=plsc.ScalarSubcoreMesh(axis_name="seq", num_cores=1), name=name,
               scratch_types=(pltpu.SemaphoreType.DMA((6 * n,)), pltpu.SemaphoreType.DMA((6 * n,))),
               compiler_params=pltpu.CompilerParams(collective_id=collective_id))
    def launch(send_sems, recv_sems):
        x, y, c, chips = _position()
        sibling = (x, y, 1 - c)
        _handshake([(chip[0], chip[1], c) for chip in chips] + [sibling])

        def half(a, j, cc):
            h = shards[a].shape[0] // 2
            return dsts[a].at[j, pl.ds(cc * h, h), :]

        sent = []
        for a in range(n):
            h = shards[a].shape[0] // 2
            for j, chip in enumerate(chips):
                cp = _remote(srcs[a].at[pl.ds(c * h, h), :], half(a, 2 * x + y, c), send_sems.at[6 * a + j],
                             recv_sems.at[6 * a + j], (chip[0], chip[1], c))
                cp.start()
                sent.append(cp)
        for a in range(n):
            for j, chip in enumerate(chips):
                landed = half(a, 2 * chip[0] + chip[1], c)
                _remote(landed, landed, send_sems.at[6 * a + j], recv_sems.at[6 * a + j], (x, y, c)).wait_recv()
                cp = _remote(landed, landed, send_sems.at[6 * a + 3 + j], recv_sems.at[6 * a + 3 + j], sibling)
                cp.start()
                sent.append(cp)
        for a in range(n):
            for j, chip in enumerate(chips):
                handed = half(a, 2 * chip[0] + chip[1], 1 - c)
                _remote(handed, handed, send_sems.at[6 * a + 3 + j], recv_sems.at[6 * a + 3 + j], (x, y, c)).wait_recv()
        for cp in sent:
            cp.wait_send()

    launch()
    return [d[...] for d in dsts]


IN_COLS = {"z": (0, D_SSM), "xbc": (D_SSM, D_SSM + D_XBC), "dt": (D_SSM + D_XBC, D_SSM + D_XBC + SSM_HEADS),
           "qkv": (D_SSM + D_XBC + SSM_HEADS, D_IN_PROJ)}


def _cols_from_quarters(quarters, lo, hi):
    parts = []
    for q in range(N_CHIPS):
        a, b = max(lo, q * W_IN_SHARD), min(hi, (q + 1) * W_IN_SHARD)
        if a < b:
            parts.append(quarters[q][:, a - q * W_IN_SHARD:b - q * W_IN_SHARD])
    return parts[0] if len(parts) == 1 else jnp.concatenate(parts, axis=1)


def _quarters_from_cols(pieces):
    quarters = []
    for q in range(N_CHIPS):
        parts = []
        for name, (lo, hi) in IN_COLS.items():
            a, b = max(lo, q * W_IN_SHARD), min(hi, (q + 1) * W_IN_SHARD)
            if a < b:
                parts.append(pieces[name][:, a - lo:b - lo])
        quarters.append(jnp.concatenate(parts, axis=1))
    return jnp.stack(quarters)


def _by_chip(own, fetched):
    me = 2 * lax.axis_index("x") + lax.axis_index("y")
    return lax.dynamic_update_slice(fetched, own[None], (me, 0, 0))


def _add_sibling(grad, got, c_arr, name, deps=()):
    nq, rows, cols = grad.shape
    h = rows // 2
    tr = 128
    nb = h // tr

    def body(c_ref, a_ref, b_ref, *rest):
        o_ref, ob_ref = rest[len(deps):]
        total = a_ref[...] + b_ref[...]
        o_ref[...] = total
        ob_ref[...] = total.astype(BF16)

    out_spec = pl.BlockSpec((None, tr, cols), lambda q, i, c: (q, i, 0))
    return pl.pallas_call(
        body,
        grid_spec=pltpu.PrefetchScalarGridSpec(
            num_scalar_prefetch=1, grid=(nq, nb),
            in_specs=[pl.BlockSpec((None, tr, cols), lambda q, i, c: (q, c[0] * nb + i, 0)),
                      pl.BlockSpec((None, tr, cols), lambda q, i, c: (q, i, 0))] + [ANY] * len(deps),
            out_specs=[out_spec, out_spec]),
        out_shape=[jax.ShapeDtypeStruct((nq, h, cols), F32), jax.ShapeDtypeStruct((nq, h, cols), BF16)],
        compiler_params=_cparams("parallel", "parallel"),
        name=name,
    )(c_arr, grad, got, *deps)


def _add_chips(part, got, chip_arr, name, deps=()):
    _, h, cols = part.shape
    tr = 128

    def body(q_ref, p_ref, g0_ref, g1_ref, g2_ref, *rest):
        o_ref = rest[len(deps)]
        o_ref[...] = ((p_ref[...] + g0_ref[...].astype(F32)) + g1_ref[...].astype(F32)) + g2_ref[...].astype(F32)

    got_spec = lambda j: pl.BlockSpec((None, tr, cols), lambda i, q: (j, i, 0))
    return pl.pallas_call(
        body,
        grid_spec=pltpu.PrefetchScalarGridSpec(
            num_scalar_prefetch=1, grid=(h // tr,),
            in_specs=[pl.BlockSpec((None, tr, cols), lambda i, q: (q[0], i, 0)), got_spec(0), got_spec(1), got_spec(2)]
            + [ANY] * len(deps),
            out_specs=pl.BlockSpec((tr, cols), lambda i, q: (i, 0))),
        out_shape=jax.ShapeDtypeStruct((h, cols), F32),
        compiler_params=_cparams("parallel"),
        name=name,
    )(chip_arr, part, got, got, got, *deps)


def _sequencer_exchange(src, out_shape, collective_id, name, plan, n_copies):
    src_ref = jax.new_ref(src, memory_space=pltpu.MemorySpace.HBM)
    dst_ref = jax.empty_ref(out_shape, memory_space=pltpu.MemorySpace.HBM)

    @pl.kernel(mesh=plsc.ScalarSubcoreMesh(axis_name="seq", num_cores=1), name=name,
               scratch_types=(pltpu.SemaphoreType.DMA((n_copies,)), pltpu.SemaphoreType.DMA((n_copies,))),
               compiler_params=pltpu.CompilerParams(collective_id=collective_id))
    def launch(send_sems, recv_sems):
        x, y, c, chips = _position()
        copies = plan(src_ref, dst_ref, x, y, c, chips)
        _handshake([peer for _, _, peer in copies])
        started = []
        for k, (s, d, peer) in enumerate(copies):
            cp = _remote(s, d, send_sems.at[k], recv_sems.at[k], peer)
            cp.start()
            started.append(cp)
        for cp in started:
            cp.wait()

    launch()
    return dst_ref[...]


class _AsyncReduceScatter:
    def __init__(self, grad, nm, first_id):
        self.grad, self.nm, self.first_id = grad, nm, first_id
        nq, rows, cols = grad.shape
        h = self.h = rows // 2

        def to_sibling(s, d, x, y, c, chips):
            return [(s.at[:, pl.ds((1 - c) * h, h), :], d, (x, y, 1 - c))]

        self.from_sibling = _sequencer_exchange(grad, jax.ShapeDtypeStruct((nq, h, cols), F32), first_id,
                                                f"rs_sibling_{nm}", to_sibling, 1)

    def sibling_sum(self, not_before=()):
        cols = self.grad.shape[2]
        c_arr = lax.axis_index("c").astype(jnp.int32).reshape(1)
        self.part, self.part_b = _add_sibling(self.grad, self.from_sibling, c_arr, f"add_sibling_{self.nm}", not_before)

        def to_chips(s, d, x, y, c, chips):
            return [(s.at[2 * chip[0] + chip[1]], d.at[j], (chip[0], chip[1], c)) for j, chip in enumerate(chips)]

        self.from_chips = _sequencer_exchange(self.part_b, jax.ShapeDtypeStruct((3, self.h, cols), BF16),
                                              self.first_id + 1, f"rs_quarters_{self.nm}", to_chips, 3)
        return self.part_b

    def chip_sum(self, not_before=()):
        cols = self.grad.shape[2]
        chip_arr = (2 * lax.axis_index("x") + lax.axis_index("y")).astype(jnp.int32).reshape(1)
        self.half = _add_chips(self.part, self.from_chips, chip_arr, f"add_chips_{self.nm}", not_before)

        def whole_to_sibling(s, d, x, y, c, chips):
            return [(s, d, (x, y, 1 - c))]

        self.other = _sequencer_exchange(self.half, jax.ShapeDtypeStruct((self.h, cols), F32), self.first_id + 2,
                                         f"rs_share_{self.nm}", whole_to_sibling, 1)
        return self.half

    def share(self):
        return self.half, self.other


def _after(x, deps, name):
    def body(x_ref, *rest):
        rest[-1][...] = x_ref[...]

    vm = pl.BlockSpec(memory_space=pltpu.VMEM)
    return pl.pallas_call(body, in_specs=[vm] + [ANY] * len(deps), out_specs=vm,
                          out_shape=jax.ShapeDtypeStruct(x.shape, x.dtype), name=name)(x, *deps)


def _adamw_halves(w, mine, other, m, v, name):
    rows, cols = w.shape
    tr = 128
    nb = rows // 2 // tr
    c_arr = lax.axis_index("c").astype(jnp.int32).reshape(1)

    def body(c_ref, w_ref, a_ref, b_ref, m_ref, v_ref, g_out, d_out, m_out, v_out):
        is_mine = (pl.program_id(0) // nb) == c_ref[0]
        g = jnp.where(is_mine, a_ref[...], b_ref[...])
        wb, mb, vb = w_ref[...], m_ref[...], v_ref[...]
        m2 = ADAM_B1 * mb + (1.0 - ADAM_B1) * g
        v2 = ADAM_B2 * vb + (1.0 - ADAM_B2) * (g * g)
        m_hat = m2 / (1.0 - ADAM_B1 ** ADAM_STEP)
        v_hat = v2 / (1.0 - ADAM_B2 ** ADAM_STEP)
        g_out[...] = g
        d_out[...] = -ADAM_LR * (m_hat / (jnp.sqrt(v_hat) + ADAM_EPS) + ADAM_WD * wb)
        m_out[...] = m2
        v_out[...] = v2

    full = pl.BlockSpec((tr, cols), lambda i, c: (i, 0))
    half = pl.BlockSpec((tr, cols), lambda i, c: (i % nb, 0))
    return pl.pallas_call(
        body,
        grid_spec=pltpu.PrefetchScalarGridSpec(
            num_scalar_prefetch=1, grid=(rows // tr,),
            in_specs=[full, half, half, full, full], out_specs=[full] * 4),
        out_shape=[jax.ShapeDtypeStruct((rows, cols), F32)] * 4,
        compiler_params=_cparams("parallel"),
        name=name,
    )(c_arr, w, mine, other, m, v)


def _all_sum_small(v):
    n_dev = 8

    def body(v_ref, o_ref, gath, send_sems, recv_sems):
        x, y, c, _ = _position()
        me = 4 * x + 2 * y + c
        gath[me] = v_ref[...]
        copies = []
        for k in range(1, n_dev):
            peer = tuple(1 - p if (k >> s) & 1 else p for p, s in ((x, 2), (y, 1), (c, 0)))
            cp = _remote(v_ref, gath.at[me], send_sems.at[k - 1], recv_sems.at[k - 1], peer)
            cp.start()
            copies.append(cp)
        for cp in copies:
            cp.wait()
        acc = gath[0]
        for i in range(1, n_dev):
            acc = acc + gath[i]
        o_ref[...] = acc

    vm = pl.BlockSpec(memory_space=pltpu.VMEM)
    return pl.pallas_call(
        body,
        in_specs=[vm],
        out_specs=vm,
        out_shape=jax.ShapeDtypeStruct(v.shape, F32),
        scratch_shapes=[pltpu.VMEM((n_dev,) + v.shape, F32), pltpu.SemaphoreType.DMA((n_dev - 1,)),
                        pltpu.SemaphoreType.DMA((n_dev - 1,))],
        name="all_sum_small",
    )(v)


def _pack_rows(vectors):
    rows = []
    for v in vectors:
        flat = v.reshape(-1).astype(F32)
        rows.append(jnp.pad(flat, (0, (-flat.shape[0]) % LANES)).reshape(-1, LANES))
    out = jnp.concatenate(rows, axis=0)
    return jnp.pad(out, ((0, (-out.shape[0]) % 8), (0, 0)))


def _unpack_rows(packed, shapes):
    outs, r = [], 0
    for shp in shapes:
        size = math.prod(shp)
        nr = -(-size // LANES)
        outs.append(packed[r:r + nr].reshape(-1)[:size].reshape(shp))
        r += nr
    return outs


def _relu_sq(acc):
    r = jnp.maximum(acc, 0.0)
    return r, r * r


def _relu_sq_bwd(acc, r):
    return (acc * (2.0 * r.astype(F32)),)


def kernel(x, norm_mix_pre, w_in, conv_w, conv_b, dt_bias, a_log, d_skip, ssm_norm_w, w_out, norm_mix_post, norm_mlp_pre, w_up, w_down, norm_mlp_post, loss_target, m_norm_mix_pre, m_w_in, m_conv_w, m_conv_b, m_dt_bias, m_a_log, m_d_skip, m_ssm_norm_w, m_w_out, m_norm_mix_post, m_norm_mlp_pre, m_w_up, m_w_down, m_norm_mlp_post, v_norm_mix_pre, v_w_in, v_conv_w, v_conv_b, v_dt_bias, v_a_log, v_d_skip, v_ssm_norm_w, v_w_out, v_norm_mix_post, v_norm_mlp_pre, v_w_up, v_w_down, v_norm_mlp_post):
    s_dim = x.shape[1]
    xs, target = x[0], loss_target[0]
    chip = 2 * lax.axis_index("x") + lax.axis_index("y")

    own = [w_in[0].astype(BF16), w_out[0].astype(BF16), w_up[0].astype(BF16), w_down[0].astype(BF16)]
    fetched_in = _gather_shards_async(own[:1], 14, "gather_w_in")[0]
    conv_cols = D_XBC // N_CHIPS
    conv_placed = lax.dynamic_update_slice(jnp.zeros((8, D_XBC), F32), 0.5 * conv_w[0], (0, chip * conv_cols))
    conv_full = _all_sum_small(conv_placed.reshape(-1, LANES)).reshape(8, D_XBC)
    w8 = _perm_cols(conv_full.at[CONV_WIDTH].set(conv_b[0]))
    u = _pre_norm(xs, norm_mix_pre)
    fetched_in, u, w8, *rest = lax.optimization_barrier((fetched_in, u, w8, *own[1:]))
    fetched = [fetched_in] + _gather_shards_async(rest, 1, "gather_rest")
    g_in, g_out, g_up, g_down = [_by_chip(o, f) for o, f in zip(own, fetched)]
    w_z = _cols_from_quarters(g_in, *IN_COLS["z"])
    w_xbc = _perm_cols(_cols_from_quarters(g_in, *IN_COLS["xbc"]))
    w_dt = jnp.pad(_cols_from_quarters(g_in, *IN_COLS["dt"]), ((0, 0), (0, LANES - SSM_HEADS)))
    w_qkv = _cols_from_quarters(g_in, *IN_COLS["qkv"])
    w_out_full = g_out.reshape(D_MIX, D_MODEL)
    w_down_full = g_down.reshape(D_FF, D_MODEL)

    z = _matmul([(u, w_z, TK)], "nn", [F32], name="proj_z")
    xbc = _matmul([(u, w_xbc, TK)], "nn", [F32], name="proj_xbc")
    dt_raw = _matmul([(u, w_dt, TK)], "nn", [F32], name="proj_dt")
    qkv = _matmul([(u, w_qkv, TK)], "nn", [BF16], name="proj_qkv")
    xc = _conv_fwd(xbc, w8)
    dtg = _dt_to_groups(dt_raw)
    par = _pack_ssd_params(dt_bias[0], a_log[0], d_skip[0])
    y, y_ssm, states = _ssd_fwd(xc, z, dtg, par, ssm_norm_w)
    y_att, y_att_f32, lse = _attn_fused_fwd(qkv)
    y_mix = jnp.concatenate([y_ssm, y_att], axis=1)
    mix = _matmul([(y_mix, w_out_full, TK)], "nn", [F32], name="out_proj")
    h1, u2 = _post_pre_norm(xs, mix, norm_mix_post, norm_mlp_pre)
    hid, act = _matmul([(u2, g_up, TK)], "nn", [BF16, BF16], name="mlp_up", epilogue=_relu_sq)
    ff = _matmul([(act, w_down_full, TK)], "nn", [F32], name="mlp_down")
    dh2, dff, d_g4, loss_part = _tail(ff, h1, target, norm_mlp_post)

    dhid = _matmul([(dff, w_down_full, TK)], "nt", [BF16], name="mlp_down_dx", epilogue=_relu_sq_bwd, extras=[hid])
    weights = {"norm_mix_pre": (norm_mix_pre, m_norm_mix_pre, v_norm_mix_pre), "w_in": (w_in, m_w_in, v_w_in),
               "conv_w": (conv_w, m_conv_w, v_conv_w), "conv_b": (conv_b, m_conv_b, v_conv_b),
               "dt_bias": (dt_bias, m_dt_bias, v_dt_bias), "a_log": (a_log, m_a_log, v_a_log),
               "d_skip": (d_skip, m_d_skip, v_d_skip), "ssm_norm_w": (ssm_norm_w, m_ssm_norm_w, v_ssm_norm_w),
               "w_out": (w_out, m_w_out, v_w_out), "norm_mix_post": (norm_mix_post, m_norm_mix_post, v_norm_mix_post),
               "norm_mlp_pre": (norm_mlp_pre, m_norm_mlp_pre, v_norm_mlp_pre), "w_up": (w_up, m_w_up, v_w_up),
               "w_down": (w_down, m_w_down, v_w_down),
               "norm_mlp_post": (norm_mlp_post, m_norm_mlp_post, v_norm_mlp_post)}
    grads, delta, new_m, new_v = {}, {}, {}, {}

    def adamw_big(n, halves):
        w, m, v = weights[n]
        g_, d_, m_, v_ = _adamw_halves(w[0], halves[0], halves[1], m[0], v[0], f"adamw_{n}")
        grads[n], delta[n], new_m[n], new_v[n] = g_[None], d_[None], m_[None], v_[None]

    dw_down = _matmul([(act, dff, TK)], "tn", [F32], name="mlp_down_dw")
    rs_down = _AsyncReduceScatter(dw_down.reshape(N_CHIPS, D_FF // N_CHIPS, D_MODEL), "w_down", 11)
    dw_up = _matmul([(u2, dhid, TK)], "tn", [F32], name="mlp_up_dw", deps=[dw_down], out_quarters=True)
    rs_up = _AsyncReduceScatter(dw_up, "w_up", 8)
    du2 = _matmul([(dhid, g_up, TK)], "nt", [F32], name="mlp_up_dx",
                  deps=[rs_down.sibling_sum(not_before=[dw_up])])
    dh1, dmix, d_g3, d_g2 = _mid_bwd(du2, h1, dh2, mix, norm_mix_post, norm_mlp_pre,
                                     deps=[rs_up.sibling_sum(not_before=[du2])])
    dymix = _matmul([(dmix, w_out_full, TK)], "nt", [F32], name="out_proj_dx")
    dw_out = _matmul([(y_mix, dmix, TK)], "tn", [F32], name="out_proj_dw")
    rs_out = _AsyncReduceScatter(dw_out.reshape(N_CHIPS, D_MIX // N_CHIPS, D_MODEL), "w_out", 5)
    dqkv = _attn_fused_bwd(qkv, dymix, y_att_f32, lse)
    par_late = _after(par, [rs_down.chip_sum(not_before=[dqkv]), rs_out.sibling_sum(not_before=[dymix])],
                      "after_w_down")
    dxc, dz, ddtg, dpar, d_nw = _ssd_bwd(xc, z, dtg, par_late, ssm_norm_w, y, states, dymix)
    g_down = rs_down.share()
    dxbc, dw8 = _conv_bwd(xbc, _after(w8, [*g_down, rs_up.chip_sum(not_before=[dxc])], "after_w_up"), dxc)
    ddt = jnp.pad(_dt_from_groups(ddtg), ((0, 0), (0, LANES - SSM_HEADS))).astype(BF16)
    g_up = rs_up.share()
    dw_z = _matmul([(u, dz, TK)], "tn", [F32], name="proj_z_dw")
    dw_xbc = _matmul([(u, dxbc, TK)], "tn", [F32], name="proj_xbc_dw",
                     deps=[*g_up, rs_out.chip_sum(not_before=[dxbc])])
    g_out = rs_out.share()
    dw_dt = _matmul([(u, ddt, TK)], "tn", [F32], name="proj_dt_dw")
    dw_qkv = _matmul([(u, dqkv, TK)], "tn", [F32], name="proj_qkv_dw")
    dw_in = _quarters_from_cols({"z": dw_z, "xbc": _unperm_cols(dw_xbc), "dt": dw_dt[:, :SSM_HEADS], "qkv": dw_qkv})
    rs_in = _AsyncReduceScatter(dw_in, "w_in", 2)
    adamw_big("w_down", g_down)
    adamw_big("w_up", g_up)
    rs_in.sibling_sum(not_before=[delta["w_up"]])
    du = _matmul([(dz, w_z, TK_MULTI), (dxbc, w_xbc, TK_MULTI), (dqkv, w_qkv, TK_MULTI), (ddt, w_dt, LANES)], "nt",
                 [F32], name="proj_dx", deps=[*g_out, rs_in.part_b])
    grad_x, d_g1 = _first_bwd(du, xs, dh1, norm_mix_pre)
    adamw_big("w_out", g_out)
    rs_in.chip_sum(not_before=[grad_x, delta["w_out"]])

    dconv = _unperm_cols(dw8)
    d_bias, d_alog, d_dskip = _unpack_ssd_params(dpar)
    small_shapes = [(1, D_MODEL), (CONV_WIDTH, D_XBC), (1, D_XBC), (1, SSM_HEADS), (1, SSM_HEADS), (1, SSM_HEADS),
                    (1, D_SSM), (1, D_MODEL), (1, D_MODEL), (1, D_MODEL), (1, LANES)]
    summed = _unpack_rows(
        _all_sum_small(_pack_rows([d_g1, dconv[:CONV_WIDTH], dconv[CONV_WIDTH:CONV_WIDTH + 1], d_bias, d_alog,
                                   d_dskip, d_nw, d_g2, d_g3, d_g4, loss_part])), small_shapes)
    (g_g1, g_conv_full, g_conv_b, g_bias, g_alog, g_dskip, g_nw, g_g2, g_g3, g_g4, loss_row) = summed
    loss = loss_row[0, 0]
    g_conv_w = lax.dynamic_slice(g_conv_full, (0, chip * conv_cols), (CONV_WIDTH, conv_cols))[None]

    grads.update({"norm_mix_pre": g_g1, "conv_w": g_conv_w, "conv_b": g_conv_b, "dt_bias": g_bias,
                  "a_log": g_alog, "d_skip": g_dskip, "ssm_norm_w": g_nw, "norm_mix_post": g_g2,
                  "norm_mlp_pre": g_g3, "norm_mlp_post": g_g4})
    order = list(weights)
    small_names = [n for n in order if n not in ("w_in", "w_out", "w_up", "w_down")]
    small_w_shapes = [weights[n][0].shape for n in small_names]
    packed = [_pack_rows([weights[n][k] for n in small_names]) for k in range(3)]
    packed_g = _pack_rows([grads[n].reshape(weights[n][0].shape) for n in small_names])
    sd, sm, sv = _adamw(packed[0], packed_g, packed[1], packed[2], "adamw_small")
    for k, n in enumerate(small_names):
        grads[n] = grads[n].reshape(weights[n][0].shape)
    for res, pk in ((delta, sd), (new_m, sm), (new_v, sv)):
        for n, val in zip(small_names, _unpack_rows(pk, small_w_shapes)):
            res[n] = val
    adamw_big("w_in", rs_in.share())

    return (loss, grad_x[None], *[grads[n] for n in order], *[delta[n] for n in order],
            *[new_m[n] for n in order], *[new_v[n] for n in order])
```

```python
import math

import numpy as np
import jax
import jax.numpy as jnp
from jax import lax
from jax.experimental import pallas as pl
from jax.experimental.pallas import tpu as pltpu
from jax.experimental.pallas import tpu_sc as plsc

F32 = jnp.float32
BF16 = jnp.bfloat16

D_MODEL = 2048
SSM_HEAD_DIM = 64
SSM_GROUPS = 8
HEADS_PER_GROUP = 4
SSM_HEADS = SSM_GROUPS * HEADS_PER_GROUP
D_SSM = SSM_HEADS * SSM_HEAD_DIM
D_STATE = 128
CONV_WIDTH = 4
SSD_CHUNK = 128
D_XBC = D_SSM + 2 * SSM_GROUPS * D_STATE
GROUP_X = HEADS_PER_GROUP * SSM_HEAD_DIM
GROUP_COLS = GROUP_X + 2 * D_STATE
ATT_HEAD_DIM = 128
ATT_HEADS = 16
D_ATT = ATT_HEADS * ATT_HEAD_DIM
DILATIONS = (1, 4, 16)
ATT_BLOCK = 128
D_MIX = D_SSM + D_ATT
D_IN_PROJ = D_SSM + D_XBC + SSM_HEADS + 3 * D_ATT
D_FF = 4 * D_MODEL
EPS = 1e-6
N_CHIPS = 4
W_IN_SHARD = D_IN_PROJ // N_CHIPS

ADAM_LR = 0.001
ADAM_B1 = 0.9
ADAM_B2 = 0.999
ADAM_EPS = 1e-08
ADAM_WD = 0.01
ADAM_STEP = 10

LANES = 128
VMEM_LIMIT = 48 * 1024 * 1024
MESH = pl.DeviceIdType.MESH

_NN = (((1,), (0,)), ((), ()))
_NT = (((1,), (1,)), ((), ()))
_TN = (((0,), (0,)), ((), ()))


def _dot(a, b, dims=_NN):
    return lax.dot_general(a, b, dims, preferred_element_type=F32)


def _cparams(*sem):
    return pltpu.CompilerParams(dimension_semantics=sem, vmem_limit_bytes=VMEM_LIMIT)


TK = 2048
TK_MULTI = 1024


def _matmul(pairs, mode, out_dtypes, *, name, tm=1024, tn=1024, epilogue=None, extras=(), deps=(), out_quarters=False):
    a0, b0, _ = pairs[0]
    m_dim = a0.shape[-1] if mode == "tn" else a0.shape[-2]
    if b0.ndim == 3:
        n_dim = b0.shape[1] if mode == "nt" else b0.shape[0] * b0.shape[2]
    else:
        n_dim = b0.shape[0] if mode == "nt" else b0.shape[1]
    tm, tn = min(tm, m_dim), min(tn, n_dim)
    nks, offs = [], []
    for a, _, tk in pairs:
        k_part = a.shape[0] if mode == "tn" else a.shape[-1]
        k_dim = k_part * (a.shape[0] if a.ndim == 3 else 1)
        assert k_part % tk == 0, (name, k_part, tk)
        offs.append(sum(nks))
        nks.append(k_dim // tk)
    nk_total = sum(nks)
    assert m_dim % tm == 0 and n_dim % tn == 0, (name, m_dim, n_dim)
    dims = {"nn": _NN, "nt": _NT, "tn": _TN}[mode]
    n_pairs, n_extra, n_out = len(pairs), len(extras), len(out_dtypes)

    in_specs, operands = [], []
    for (a, b, tk), off, nk in zip(pairs, offs, nks):
        def kidx(k, off=off, nk=nk):
            return k if n_pairs == 1 else jnp.clip(k - off, 0, nk - 1)
        if mode == "tn":
            assert a.ndim == 2
            in_specs.append(pl.BlockSpec((tk, tm), lambda m, n, k, f=kidx: (f(k), m)))
        elif a.ndim == 3:
            per = a.shape[2] // tk
            in_specs.append(pl.BlockSpec((None, tm, tk), lambda m, n, k, f=kidx, per=per: (f(k) // per, m, f(k) % per)))
        else:
            in_specs.append(pl.BlockSpec((tm, tk), lambda m, n, k, f=kidx: (m, f(k))))
        if b.ndim == 3 and mode == "nt":
            per = b.shape[2] // tk
            in_specs.append(pl.BlockSpec((None, tn, tk), lambda m, n, k, f=kidx, per=per: (f(k) // per, n, f(k) % per)))
        elif b.ndim == 3:
            per = b.shape[2] // tn
            in_specs.append(pl.BlockSpec((None, tk, tn), lambda m, n, k, f=kidx, per=per: (n // per, f(k), n % per)))
        elif mode == "nt":
            in_specs.append(pl.BlockSpec((tn, tk), lambda m, n, k, f=kidx: (n, f(k))))
        else:
            in_specs.append(pl.BlockSpec((tk, tn), lambda m, n, k, f=kidx: (f(k), n)))
        operands += [a, b]
    for e in extras:
        in_specs.append(pl.BlockSpec((tm, tn), lambda m, n, k: (m, n)))
        operands.append(e)
    in_specs += [pl.BlockSpec(memory_space=pl.ANY)] * len(deps)
    operands += list(deps)
    first_out = 2 * n_pairs + n_extra + len(deps)
    if out_quarters:
        out_per_q = n_dim // N_CHIPS // tn
        out_dims = (N_CHIPS, m_dim, n_dim // N_CHIPS)
        out_spec = pl.BlockSpec((None, tm, tn), lambda m, n, k: (n // out_per_q, m, n % out_per_q))
    else:
        out_dims = (m_dim, n_dim)
        out_spec = pl.BlockSpec((tm, tn), lambda m, n, k: (m, n))

    def body(*refs):
        ab = refs[:2 * n_pairs]
        e_refs = refs[2 * n_pairs:2 * n_pairs + n_extra]
        o_refs = refs[first_out:first_out + n_out]

        def finish(total):
            vals = (total,) if epilogue is None else epilogue(total, *[e[...] for e in e_refs])
            for o_ref, v in zip(o_refs, vals):
                o_ref[...] = v.astype(o_ref.dtype)

        if nk_total == 1:
            finish(_dot(ab[0][...], ab[1][...], dims))
            return
        acc = refs[-1]
        k = pl.program_id(2)

        @pl.when(k == 0)
        def _():
            acc[...] = jnp.zeros_like(acc)

        for i in range(n_pairs):
            def accumulate(i=i):
                acc[...] += _dot(ab[2 * i][...], ab[2 * i + 1][...], dims)
            if n_pairs == 1:
                accumulate()
            else:
                pl.when((k >= offs[i]) & (k < offs[i] + nks[i]))(accumulate)

        @pl.when(k == nk_total - 1)
        def _():
            finish(acc[...])

    outs = pl.pallas_call(
        body,
        grid=(m_dim // tm, n_dim // tn, nk_total),
        in_specs=in_specs,
        out_specs=[out_spec for _ in out_dtypes],
        out_shape=[jax.ShapeDtypeStruct(out_dims, dt) for dt in out_dtypes],
        scratch_shapes=[pltpu.VMEM((tm, tn), F32)] if nk_total > 1 else [],
        compiler_params=_cparams("parallel", "parallel", "arbitrary"),
        name=name,
    )(*operands)
    return outs[0] if n_out == 1 else outs


def _rowcall(fn, rows, vecs, row_outs, acc_widths, *, name, tr=256, row_cols=None, deps=()):
    s_dim = rows[0].shape[0]
    assert s_dim % tr == 0
    row_cols = row_cols or [None] * len(rows)
    n_r, n_v, n_ro, n_acc = len(rows), len(vecs), len(row_outs), len(acc_widths)
    in_specs = []
    for r, rc in zip(rows, row_cols):
        if rc is None:
            in_specs.append(pl.BlockSpec((tr, r.shape[1]), lambda i: (i, 0)))
        else:
            in_specs.append(pl.BlockSpec((tr, rc[0]), lambda i, c=rc[1]: (i, c)))
    for v in vecs:
        in_specs.append(pl.BlockSpec(v.shape, lambda i, nd=v.ndim: (0,) * nd))
    in_specs += [pl.BlockSpec(memory_space=pl.ANY)] * len(deps)
    n_d = len(deps)

    def body(*refs):
        ins = [r[...] for r in refs[:n_r + n_v]]
        ro = refs[n_r + n_v + n_d:n_r + n_v + n_d + n_ro]
        ao = refs[n_r + n_v + n_d + n_ro:]
        outs = fn(*ins)
        for ref, v in zip(ro, outs[:n_ro]):
            ref[...] = v.astype(ref.dtype)
        if n_acc:
            @pl.when(pl.program_id(0) == 0)
            def _():
                for ref in ao:
                    ref[...] = jnp.zeros_like(ref)
            for ref, v in zip(ao, outs[n_ro:]):
                ref[...] += v

    outs = pl.pallas_call(
        body,
        grid=(s_dim // tr,),
        in_specs=in_specs,
        out_specs=[pl.BlockSpec((tr, w), lambda i: (i, 0)) for w, _ in row_outs]
        + [pl.BlockSpec((1, w), lambda i: (0, 0)) for w in acc_widths],
        out_shape=[jax.ShapeDtypeStruct((s_dim, w), dt) for w, dt in row_outs]
        + [jax.ShapeDtypeStruct((1, w), F32) for w in acc_widths],
        compiler_params=_cparams("arbitrary"),
        name=name,
    )(*rows, *vecs, *deps)
    return outs


def _nrm(x, g):
    r = lax.rsqrt(jnp.mean(x * x, axis=-1, keepdims=True) + EPS)
    n = x * r
    return n * g, n, r


def _nrm_bwd(dy, n, r, g):
    dn = dy * g
    dx = r * (dn - n * jnp.mean(dn * n, axis=-1, keepdims=True))
    return dx, jnp.sum(dy * n, axis=0, keepdims=True)


def _sigmoid(x):
    return 1.0 / (1.0 + jnp.exp(-x))


def _softplus(x):
    return jnp.maximum(x, 0.0) + jnp.log(1.0 + jnp.exp(-jnp.abs(x)))


def _pre_norm(x, g1):
    def fn(xb, g):
        return (_nrm(xb, g)[0],)
    return _rowcall(fn, [x], [g1], [(D_MODEL, BF16)], [], name="pre_norm")[0]


def _post_pre_norm(x, mix, g2, g3):
    def fn(xb, mb, g2b, g3b):
        h1 = xb + _nrm(mb, g2b)[0]
        return h1, _nrm(h1, g3b)[0]
    return _rowcall(fn, [x, mix], [g2, g3], [(D_MODEL, F32), (D_MODEL, BF16)], [], name="post_pre_norm")


def _tail(ff, h1, target, g4):
    def fn(ffb, h1b, tb, g):
        y, n, r = _nrm(ffb, g)
        e = h1b + y - tb
        loss = 0.5 * jnp.sum(jnp.sum(e * e, axis=-1, keepdims=True) * (1.0 / D_MODEL), axis=0, keepdims=True)
        dh2 = e * (1.0 / D_MODEL)
        dff, dg = _nrm_bwd(dh2, n, r, g)
        return dh2, dff, dg, jnp.broadcast_to(loss, (1, LANES))
    return _rowcall(fn, [ff, h1, target], [g4], [(D_MODEL, F32), (D_MODEL, BF16)], [D_MODEL, LANES], name="tail")


def _mid_bwd(du2, h1, dh2, mix, g2, g3, deps=()):
    def fn(du2b, h1b, dh2b, mb, g2b, g3b):
        _, n3, r3 = _nrm(h1b, g3b)
        d3, dg3 = _nrm_bwd(du2b, n3, r3, g3b)
        dh1 = dh2b + d3
        _, n2, r2 = _nrm(mb, g2b)
        dmix, dg2 = _nrm_bwd(dh1, n2, r2, g2b)
        return dh1, dmix, dg3, dg2
    return _rowcall(fn, [du2, h1, dh2, mix], [g2, g3], [(D_MODEL, F32), (D_MODEL, BF16)], [D_MODEL, D_MODEL],
                    name="mid_bwd", deps=deps)


def _first_bwd(du, x, dh1, g1):
    def fn(dub, xb, dh1b, g):
        _, n, r = _nrm(xb, g)
        dx, dg = _nrm_bwd(dub, n, r, g)
        return dh1b + dx, dg
    return _rowcall(fn, [du, x, dh1], [g1], [(D_MODEL, F32)], [D_MODEL], name="first_bwd")


CONV_TILE = 256
CONV_ROWS = 256
PAD = 8


def _conv_taps(w):
    return [w[k:k + 1, :] for k in range(CONV_WIDTH)], w[CONV_WIDTH:CONV_WIDTH + 1, :]


def _conv_fwd(xbc, w8):
    s_dim, c_dim = xbc.shape
    n_steps = s_dim // CONV_ROWS

    def body(x_ref, w_ref, o_ref, xp):
        xp[0:PAD, :] = jnp.zeros((PAD, CONV_TILE), F32)
        xp[PAD:PAD + s_dim, :] = x_ref[...]
        taps, bias = _conv_taps(w_ref[...])

        def step(c, carry):
            base = pl.multiple_of(c * CONV_ROWS, CONV_ROWS)
            win = xp[pl.ds(base, CONV_ROWS + PAD), :]
            pre = bias + taps[3] * win[PAD:, :]
            for j in range(1, CONV_WIDTH):
                pre = pre + taps[3 - j] * pltpu.roll(win, j, axis=0)[PAD:, :]
            o_ref[pl.ds(base, CONV_ROWS), :] = pre * _sigmoid(pre)
            return carry

        lax.fori_loop(0, n_steps, step, 0)

    return pl.pallas_call(
        body,
        grid=(c_dim // CONV_TILE,),
        in_specs=[pl.BlockSpec((s_dim, CONV_TILE), lambda j: (0, j)), pl.BlockSpec((8, CONV_TILE), lambda j: (0, j))],
        out_specs=pl.BlockSpec((s_dim, CONV_TILE), lambda j: (0, j)),
        out_shape=jax.ShapeDtypeStruct((s_dim, c_dim), F32),
        scratch_shapes=[pltpu.VMEM((s_dim + 2 * PAD, CONV_TILE), F32)],
        compiler_params=_cparams("parallel"),
        name="conv_fwd",
    )(xbc, w8)


def _conv_bwd(xbc, w8, dxc):
    s_dim, c_dim = xbc.shape
    n_steps = s_dim // CONV_ROWS

    def body(x_ref, w_ref, d_ref, dx_ref, dw_ref, xp, dp):
        xp[0:PAD, :] = jnp.zeros((PAD, CONV_TILE), F32)
        xp[PAD:PAD + s_dim, :] = x_ref[...]
        dp[PAD + s_dim:, :] = jnp.zeros((PAD, CONV_TILE), F32)
        taps, bias = _conv_taps(w_ref[...])

        def step1(c, sums):
            base = pl.multiple_of(c * CONV_ROWS, CONV_ROWS)
            win = xp[pl.ds(base, CONV_ROWS + PAD), :]
            shifted = [win[PAD:, :]] + [pltpu.roll(win, j, axis=0)[PAD:, :] for j in range(1, CONV_WIDTH)]
            pre = bias
            for j in range(CONV_WIDTH):
                pre = pre + taps[3 - j] * shifted[j]
            sg = _sigmoid(pre)
            dpre = d_ref[pl.ds(base, CONV_ROWS), :] * (sg * (1.0 + pre * (1.0 - sg)))
            dp[pl.ds(base + PAD, CONV_ROWS), :] = dpre
            new = [sums[k] + jnp.sum(dpre * shifted[3 - k], axis=0, keepdims=True) for k in range(CONV_WIDTH)]
            new.append(sums[CONV_WIDTH] + jnp.sum(dpre, axis=0, keepdims=True))
            return tuple(new)

        zero = jnp.zeros((1, CONV_TILE), F32)
        sums = lax.fori_loop(0, n_steps, step1, (zero,) * (CONV_WIDTH + 1))
        dw_ref[...] = jnp.zeros((8, CONV_TILE), F32)
        for k in range(CONV_WIDTH + 1):
            dw_ref[k:k + 1, :] = sums[k]

        def step2(c, carry):
            base = pl.multiple_of(c * CONV_ROWS, CONV_ROWS)
            win = dp[pl.ds(base + PAD, CONV_ROWS + PAD), :]
            dx = taps[3] * win[:CONV_ROWS, :]
            for j in range(1, CONV_WIDTH):
                dx = dx + taps[3 - j] * pltpu.roll(win, CONV_ROWS + PAD - j, axis=0)[:CONV_ROWS, :]
            dx_ref[pl.ds(base, CONV_ROWS), :] = dx.astype(BF16)
            return carry

        lax.fori_loop(0, n_steps, step2, 0)

    col = lambda j: (0, j)
    return pl.pallas_call(
        body,
        grid=(c_dim // CONV_TILE,),
        in_specs=[pl.BlockSpec((s_dim, CONV_TILE), col), pl.BlockSpec((8, CONV_TILE), col),
                  pl.BlockSpec((s_dim, CONV_TILE), col)],
        out_specs=[pl.BlockSpec((s_dim, CONV_TILE), col), pl.BlockSpec((8, CONV_TILE), col)],
        out_shape=[jax.ShapeDtypeStruct((s_dim, c_dim), BF16), jax.ShapeDtypeStruct((8, c_dim), F32)],
        scratch_shapes=[pltpu.VMEM((s_dim + 2 * PAD, CONV_TILE), F32), pltpu.VMEM((s_dim + 2 * PAD, CONV_TILE), F32)],
        compiler_params=_cparams("parallel"),
        name="conv_bwd",
    )(xbc, w8, dxc)


def _perm_cols(a):
    parts = []
    for g in range(SSM_GROUPS):
        parts += [a[..., g * GROUP_X:(g + 1) * GROUP_X],
                  a[..., D_SSM + g * D_STATE:D_SSM + (g + 1) * D_STATE],
                  a[..., D_SSM + SSM_GROUPS * D_STATE + g * D_STATE:D_SSM + SSM_GROUPS * D_STATE + (g + 1) * D_STATE]]
    return jnp.concatenate(parts, axis=-1)


def _unperm_cols(a):
    xs = [a[..., g * GROUP_COLS:g * GROUP_COLS + GROUP_X] for g in range(SSM_GROUPS)]
    bs = [a[..., g * GROUP_COLS + GROUP_X:g * GROUP_COLS + GROUP_X + D_STATE] for g in range(SSM_GROUPS)]
    cs = [a[..., g * GROUP_COLS + GROUP_X + D_STATE:(g + 1) * GROUP_COLS] for g in range(SSM_GROUPS)]
    return jnp.concatenate(xs + bs + cs, axis=-1)


def _dt_to_groups(dt):
    s_dim = dt.shape[0]
    t = dt[:, :SSM_HEADS].reshape(s_dim, SSM_GROUPS, HEADS_PER_GROUP).transpose(1, 0, 2)
    return jnp.pad(t, ((0, 0), (0, 0), (0, LANES - HEADS_PER_GROUP)))


def _dt_from_groups(dtg):
    s_dim = dtg.shape[1]
    return dtg[:, :, :HEADS_PER_GROUP].transpose(1, 0, 2).reshape(s_dim, SSM_HEADS)


def _pack_ssd_params(dt_bias, a_log, d_skip):
    rows = jnp.stack([p.reshape(SSM_GROUPS, HEADS_PER_GROUP) for p in (dt_bias, a_log, d_skip)], axis=1)
    return jnp.pad(rows, ((0, 0), (0, 8 - 3), (0, LANES - HEADS_PER_GROUP)))


def _unpack_ssd_params(par):
    return tuple(par[:, k, :HEADS_PER_GROUP].reshape(SSM_HEADS) for k in range(3))


Q = SSD_CHUNK


def _split3(v):
    hi = v.astype(BF16)
    r1 = v - hi.astype(F32)
    mid = r1.astype(BF16)
    lo = (r1 - mid.astype(F32)).astype(BF16)
    return hi, mid, lo


def _dot_l01(t01, v):
    return sum(_dot(t01, p) for p in _split3(v))


def _dot_r01(v, e01):
    return sum(_dot(p, e01) for p in _split3(v))


def _ssd_consts():
    row = lax.broadcasted_iota(jnp.int32, (Q, Q), 0)
    col = lax.broadcasted_iota(jnp.int32, (Q, Q), 1)
    causal = row >= col
    tril = causal.astype(BF16)
    triu = (col >= row).astype(BF16)
    er = lax.broadcasted_iota(jnp.int32, (LANES, GROUP_X), 0)
    ec = lax.broadcasted_iota(jnp.int32, (LANES, GROUP_X), 1) // SSM_HEAD_DIM
    expand = (er == ec).astype(BF16)
    rr = lax.broadcasted_iota(jnp.int32, (GROUP_X, LANES), 0) // SSM_HEAD_DIM
    rc = lax.broadcasted_iota(jnp.int32, (GROUP_X, LANES), 1)
    reduce = (rr == rc).astype(BF16)
    lane_head = lax.broadcasted_iota(jnp.int32, (Q, GROUP_X), 1) // SSM_HEAD_DIM
    return causal, tril, triu, expand, reduce, lane_head


def _ssd_common(xc_ref, dt_ref, par_ref, consts):
    causal, tril, _, expand, _, _ = consts
    par = par_ref[...]
    bias, alog, dsk = par[0:1, :], par[1:2, :], par[2:3, :]
    a_neg = -jnp.exp(alog)
    dtr = dt_ref[...] + bias
    dt = _softplus(dtr)
    s = _dot_l01(tril, dt * a_neg)
    dt_x = _dot_r01(dt, expand)
    s_x = _dot_r01(s, expand)
    dsk_x = _dot_r01(jnp.broadcast_to(dsk, (8, LANES)), expand)[0:1, :]
    blk = xc_ref[...]
    x = blk[:, :GROUP_X]
    bm = blk[:, GROUP_X:GROUP_X + D_STATE].astype(BF16)
    cm = blk[:, GROUP_X + D_STATE:].astype(BF16)
    xdt = x * dt_x
    g = _dot(cm, bm, _NT)
    return dict(a_neg=a_neg, dtr=dtr, dt=dt, s=s, s_t=s.T, dt_x=dt_x, s_x=s_x, dsk_x=dsk_x, x=x, bm=bm, cm=cm,
                xdt=xdt, g=g)


def _decay(v, r, causal):
    diff = v["s"][:, r:r + 1] - v["s_t"][r:r + 1, :]
    return jnp.exp(jnp.where(causal, diff, -jnp.inf))


def _ssd_specs(n_chunks, rev):
    cidx = (lambda c: n_chunks - 1 - c) if rev else (lambda c: c)
    xc = pl.BlockSpec((Q, GROUP_COLS), lambda g, c: (cidx(c), g))
    gx = pl.BlockSpec((Q, GROUP_X), lambda g, c: (cidx(c), g))
    dt = pl.BlockSpec((None, Q, LANES), lambda g, c: (g, cidx(c), 0))
    par = pl.BlockSpec((None, 8, LANES), lambda g, c: (g, 0, 0))
    nw = pl.BlockSpec((1, GROUP_X), lambda g, c: (0, g))
    hs = pl.BlockSpec((None, None, D_STATE, GROUP_X), lambda g, c: (cidx(c), g, 0, 0))
    return xc, gx, dt, par, nw, hs


def _ssd_fwd(xc, z, dtg, par, nw):
    s_dim = xc.shape[0]
    n_chunks = s_dim // Q
    xc_s, gx_s, dt_s, par_s, nw_s, hs_s = _ssd_specs(n_chunks, False)

    def body(xc_ref, z_ref, dt_ref, par_ref, nw_ref, y_ref, ys_ref, hs_ref, ht):
        @pl.when(pl.program_id(1) == 0)
        def _():
            ht[...] = jnp.zeros_like(ht)

        consts = _ssd_consts()
        causal, lane_head = consts[0], consts[5]
        v = _ssd_common(xc_ref, dt_ref, par_ref, consts)
        xdt_b = v["xdt"].astype(BF16)
        yd = jnp.zeros((Q, GROUP_X), F32)
        for r in range(HEADS_PER_GROUP):
            m = (v["g"] * _decay(v, r, causal)).astype(BF16)
            yd = yd + _dot(m, jnp.where(lane_head == r, xdt_b, jnp.zeros_like(xdt_b)))
        h = ht[...]
        hs_ref[...] = h
        yo = jnp.exp(v["s_x"]) * _dot(v["cm"], h.astype(BF16))
        y = yd + yo + v["dsk_x"] * v["x"]
        s_last = v["s_x"][Q - 1:Q, :]
        snew = _dot(v["bm"], (v["xdt"] * jnp.exp(s_last - v["s_x"])).astype(BF16), _TN)
        ht[...] = jnp.exp(s_last) * h + snew
        zz = z_ref[...]
        yg = y * (zz * _sigmoid(zz))
        y_ref[...] = y
        ys_ref[...] = _nrm(yg, nw_ref[...])[0].astype(BF16)

    return pl.pallas_call(
        body,
        grid=(SSM_GROUPS, n_chunks),
        in_specs=[xc_s, gx_s, dt_s, par_s, nw_s],
        out_specs=[gx_s, gx_s, hs_s],
        out_shape=[jax.ShapeDtypeStruct((s_dim, D_SSM), F32), jax.ShapeDtypeStruct((s_dim, D_SSM), BF16),
                   jax.ShapeDtypeStruct((n_chunks, SSM_GROUPS, D_STATE, GROUP_X), F32)],
        scratch_shapes=[pltpu.VMEM((D_STATE, GROUP_X), F32)],
        compiler_params=_cparams("parallel", "arbitrary"),
        name="ssd_fwd",
    )(xc, z, dtg, par, nw)


def _ssd_bwd(xc, z, dtg, par, nw, y, hs, dymix):
    s_dim = xc.shape[0]
    n_chunks = s_dim // Q
    xc_s, gx_s, dt_s, par_s, nw_s, hs_s = _ssd_specs(n_chunks, True)

    def body(xc_ref, z_ref, dt_ref, par_ref, nw_ref, y_ref, hs_ref, dys_ref,
             dxc_ref, dz_ref, ddt_ref, dpar_ref, dnw_ref, dht):
        @pl.when(pl.program_id(1) == 0)
        def _():
            dht[...] = jnp.zeros_like(dht)
            dpar_ref[...] = jnp.zeros_like(dpar_ref)
            dnw_ref[...] = jnp.zeros_like(dnw_ref)

        consts = _ssd_consts()
        causal, _, triu, _, reduce, lane_head = consts
        v = _ssd_common(xc_ref, dt_ref, par_ref, consts)
        x, bm, cm, xdt, s_x = v["x"], v["bm"], v["cm"], v["xdt"], v["s_x"]
        h = hs_ref[...]
        hb = h.astype(BF16)
        es_x = jnp.exp(s_x)
        yo = es_x * _dot(cm, hb)
        s_last = s_x[Q - 1:Q, :]
        e_x = jnp.exp(s_last - s_x)
        es_last = jnp.exp(s_last)

        yv, zz, nw_v = y_ref[...], z_ref[...], nw_ref[...]
        sg = _sigmoid(zz)
        gz = zz * sg
        _, n, rstd = _nrm(yv * gz, nw_v)
        dout = dys_ref[...]
        dyg, dnw = _nrm_bwd(dout, n, rstd, nw_v)
        dnw_ref[...] += dnw
        dy = dyg * gz
        dz_ref[...] = (dyg * yv * (sg * (1.0 + zz * (1.0 - sg)))).astype(BF16)

        dyb = dy.astype(BF16)
        xdt_b = xdt.astype(BF16)
        dhp = dht[...]
        dhpb = dhp.astype(BF16)
        lane = lax.broadcasted_iota(jnp.int32, (Q, LANES), 1)
        sub = lax.broadcasted_iota(jnp.int32, (LANES, Q), 0)
        dxdt = jnp.zeros((Q, GROUP_X), F32)
        dg = jnp.zeros((Q, Q), F32)
        ds = jnp.zeros((Q, LANES), F32)
        ds_t = jnp.zeros((LANES, Q), F32)
        for r in range(HEADS_PER_GROUP):
            dec = _decay(v, r, causal)
            mf = v["g"] * dec
            dyr = jnp.where(lane_head == r, dyb, jnp.zeros_like(dyb))
            dm = _dot(dyr, xdt_b, _NT)
            dxdt = dxdt + _dot(mf.astype(BF16), dyr, _TN)
            dg = dg + dm * dec
            dd = dm * mf
            ds = ds + jnp.where(lane == r, jnp.sum(dd, axis=1, keepdims=True), 0.0)
            ds_t = ds_t + jnp.where(sub == r, jnp.sum(dd, axis=0, keepdims=True), 0.0)
        ds = ds - ds_t.T
        dgb = dg.astype(BF16)
        dwb = (es_x * dy).astype(BF16)
        dcm = _dot(dgb, bm) + _dot(dwb, hb, _NT)
        dh_prev = _dot(cm, dwb, _TN)
        zst = _dot(bm, dhpb)
        xe = xdt * e_x
        dxdt = dxdt + e_x * zst
        dee = xe * zst
        dbm = _dot(dgb, cm, _TN) + _dot(xe.astype(BF16), dhpb, _NT)
        v_last = jnp.sum(dee, axis=0, keepdims=True) + es_last * jnp.sum(dhp * h, axis=0, keepdims=True)
        row_x = lax.broadcasted_iota(jnp.int32, (Q, GROUP_X), 0)
        tx = dy * yo - dee + jnp.where(row_x == Q - 1, v_last, 0.0)
        ds = ds + _dot_r01(tx, reduce)
        ddta = _dot_l01(triu, ds)
        ddt = ddta * v["a_neg"] + _dot_r01(dxdt * x, reduce)
        dalog = jnp.sum(ddta * v["dt"], axis=0, keepdims=True) * v["a_neg"]
        draw = jnp.where(lane < HEADS_PER_GROUP, ddt * _sigmoid(v["dtr"]), 0.0)
        dbias = jnp.sum(draw, axis=0, keepdims=True)
        ddsk = _dot_r01(jnp.broadcast_to(jnp.sum(dy * x, axis=0, keepdims=True), (8, GROUP_X)), reduce)[0:1, :]
        dht[...] = es_last * dhp + dh_prev
        dxc_ref[:, :GROUP_X] = dxdt * v["dt_x"] + v["dsk_x"] * dy
        dxc_ref[:, GROUP_X:GROUP_X + D_STATE] = dbm
        dxc_ref[:, GROUP_X + D_STATE:] = dcm
        ddt_ref[...] = draw
        dpar_ref[0:1, :] += dbias
        dpar_ref[1:2, :] += dalog
        dpar_ref[2:3, :] += ddsk

    return pl.pallas_call(
        body,
        grid=(SSM_GROUPS, n_chunks),
        in_specs=[xc_s, gx_s, dt_s, par_s, nw_s, gx_s, hs_s, gx_s],
        out_specs=[xc_s, gx_s, dt_s, par_s, nw_s],
        out_shape=[jax.ShapeDtypeStruct((s_dim, SSM_GROUPS * GROUP_COLS), F32),
                   jax.ShapeDtypeStruct((s_dim, D_SSM), BF16),
                   jax.ShapeDtypeStruct((SSM_GROUPS, s_dim, LANES), F32),
                   jax.ShapeDtypeStruct((SSM_GROUPS, 8, LANES), F32),
                   jax.ShapeDtypeStruct((1, D_SSM), F32)],
        scratch_shapes=[pltpu.VMEM((D_STATE, GROUP_X), F32)],
        compiler_params=_cparams("parallel", "arbitrary"),
        name="ssd_bwd",
    )(xc, z, dtg, par, nw, y, hs, dymix)


ATT_SCALE = ATT_HEAD_DIM ** -0.5
NEG_INF = -jnp.inf


def _band_masks():
    qi = lax.broadcasted_iota(jnp.int32, (ATT_BLOCK, ATT_BLOCK), 0)
    kj = lax.broadcasted_iota(jnp.int32, (ATT_BLOCK, ATT_BLOCK), 1)
    return kj <= qi, kj >= qi


WIN = ATT_BLOCK * DILATIONS[-1]
N_BLOCKS = WIN // ATT_BLOCK


def _rows(start, d):
    return pl.ds(start, ATT_BLOCK) if d == 1 else pl.ds(start, ATT_BLOCK, stride=d)


def _block_start(idx, d):
    return (idx // d) * (ATT_BLOCK * d) + idx % d


def _lane_bcast(col):
    return jnp.broadcast_to(col, (col.shape[0], LANES))


def _attn_fused_fwd(qkv):
    s_dim = qkv.shape[0]
    n_win = s_dim // WIN
    blk = (WIN, ATT_HEAD_DIM)
    prev = lambda w: jnp.maximum(w - 1, 0)

    def body(q_ref, kc_ref, kp_ref, vc_ref, vp_ref, y_ref, yf_ref, lse_ref, qf, kf, vf, acc, m_run, l_run):
        w, h = pl.program_id(0), pl.program_id(1)
        qf[...] = q_ref[...].astype(F32)
        kf[0:WIN, :] = kp_ref[...].astype(F32)
        kf[WIN:, :] = kc_ref[...].astype(F32)
        vf[0:WIN, :] = vp_ref[...].astype(F32)
        vf[WIN:, :] = vc_ref[...].astype(F32)
        own, before = _band_masks()

        for d in DILATIONS:
            def block(idx, carry, d=d):
                start = _block_start(idx, d)
                rows = _rows(start, d)
                q = qf[rows, :].astype(BF16)
                kc, vc = kf[_rows(WIN + start, d), :].astype(BF16), vf[_rows(WIN + start, d), :].astype(BF16)
                kp = kf[_rows(WIN + start - ATT_BLOCK * d, d), :].astype(BF16)
                vp = vf[_rows(WIN + start - ATT_BLOCK * d, d), :].astype(BF16)
                has_prev = (idx >= d) | (w > 0)
                sc = jnp.where(own, _dot(q, kc, _NT) * ATT_SCALE, NEG_INF)
                sp = jnp.where(before & has_prev, _dot(q, kp, _NT) * ATT_SCALE, NEG_INF)
                m_blk = jnp.maximum(jnp.max(sc, axis=1, keepdims=True), jnp.max(sp, axis=1, keepdims=True))
                if d == DILATIONS[0]:
                    m_new = m_blk
                else:
                    m_old = m_run[rows, :][:, 0:1]
                    m_new = jnp.maximum(m_old, m_blk)
                pc, pp = jnp.exp(sc - m_new), jnp.exp(sp - m_new)
                l_new = jnp.sum(pc, axis=1, keepdims=True) + jnp.sum(pp, axis=1, keepdims=True)
                o_new = _dot(pc.astype(BF16), vc) + _dot(pp.astype(BF16), vp)
                if d != DILATIONS[0]:
                    alpha = jnp.exp(m_old - m_new)
                    l_new = alpha * l_run[rows, :][:, 0:1] + l_new
                    o_new = alpha * acc[rows, :] + o_new
                m_run[rows, :] = _lane_bcast(m_new)
                l_run[rows, :] = _lane_bcast(l_new)
                acc[rows, :] = o_new
                return carry

            for idx in range(N_BLOCKS):
                block(idx, 0)

        l_all = l_run[...]
        y = acc[...] / l_all
        y_ref[...] = y.astype(BF16)
        yf_ref[...] = y
        @pl.when(h == 0)
        def _():
            lse_ref[...] = jnp.zeros_like(lse_ref)

        lane = lax.broadcasted_iota(jnp.int32, (WIN, LANES), 1)
        lse_ref[...] = jnp.where(lane == h, m_run[...] + jnp.log(l_all), lse_ref[...])

    win_scratch = lambda rows: pltpu.VMEM((rows, ATT_HEAD_DIM), F32)
    return pl.pallas_call(
        body,
        grid=(n_win, ATT_HEADS),
        in_specs=[pl.BlockSpec(blk, lambda w, h: (w, h)),
                  pl.BlockSpec(blk, lambda w, h: (w, ATT_HEADS + h)),
                  pl.BlockSpec(blk, lambda w, h: (prev(w), ATT_HEADS + h)),
                  pl.BlockSpec(blk, lambda w, h: (w, 2 * ATT_HEADS + h)),
                  pl.BlockSpec(blk, lambda w, h: (prev(w), 2 * ATT_HEADS + h))],
        out_specs=[pl.BlockSpec(blk, lambda w, h: (w, h)), pl.BlockSpec(blk, lambda w, h: (w, h)),
                   pl.BlockSpec((WIN, LANES), lambda w, h: (w, 0))],
        out_shape=[jax.ShapeDtypeStruct((s_dim, D_ATT), BF16), jax.ShapeDtypeStruct((s_dim, D_ATT), F32),
                   jax.ShapeDtypeStruct((s_dim, LANES), F32)],
        scratch_shapes=[win_scratch(WIN), win_scratch(2 * WIN), win_scratch(2 * WIN), win_scratch(WIN),
                        win_scratch(WIN), win_scratch(WIN)],
        compiler_params=_cparams("parallel", "arbitrary"),
        name="attn_fused_fwd",
    )(qkv, qkv, qkv, qkv, qkv)


def _attn_fused_bwd(qkv, dymix, y_att, lse, deps=()):
    s_dim = qkv.shape[0]
    n_win = s_dim // WIN
    blk = (WIN, ATT_HEAD_DIM)
    prev = lambda w: jnp.maximum(w - 1, 0)
    nxt = lambda w: jnp.minimum(w + 1, n_win - 1)
    n_dep = len(deps)

    def body(qc_ref, qn_ref, kc_ref, kp_ref, vc_ref, vp_ref, dyc_ref, dyn_ref, yc_ref, yn_ref, lc_ref, ln_ref, *rest):
        out_ref = rest[n_dep]
        qf, qnf, kf, vf, dq_acc, dk_acc, dv_acc, ls_c, dl_c, ls_n, dl_n = rest[n_dep + 1:]
        w, h = pl.program_id(0), pl.program_id(1)
        qf[...] = qc_ref[...].astype(F32)
        qnf[...] = qn_ref[...].astype(F32)
        kf[0:WIN, :] = kp_ref[...].astype(F32)
        kf[WIN:, :] = kc_ref[...].astype(F32)
        vf[0:WIN, :] = vp_ref[...].astype(F32)
        vf[WIN:, :] = vc_ref[...].astype(F32)
        lane = lax.broadcasted_iota(jnp.int32, (WIN, LANES), 1)
        pick = lambda ref: _lane_bcast(jnp.sum(jnp.where(lane == h, ref[...], 0.0), axis=1, keepdims=True))
        ls_c[...] = pick(lc_ref)
        ls_n[...] = pick(ln_ref)
        dl_c[...] = _lane_bcast(jnp.sum(dyc_ref[...] * yc_ref[...], axis=1, keepdims=True))
        dl_n[...] = _lane_bcast(jnp.sum(dyn_ref[...] * yn_ref[...], axis=1, keepdims=True))
        for ref in (dq_acc, dk_acc, dv_acc):
            ref[...] = jnp.zeros_like(ref)
        own, before = _band_masks()

        def probs(q, k, v, dy, lse_col, dl_col, mask):
            p = jnp.exp(jnp.where(mask, _dot(q, k, _NT) * ATT_SCALE - lse_col, NEG_INF))
            ds = p * (_dot(dy, v, _NT) - dl_col)
            return p.astype(BF16), ds.astype(BF16)

        for d in DILATIONS:
            def block(idx, carry, d=d):
                start = _block_start(idx, d)
                rows = _rows(start, d)
                q, dy = qf[rows, :].astype(BF16), dyc_ref[rows, :].astype(BF16)
                lse_col, dl_col = ls_c[rows, :][:, 0:1], dl_c[rows, :][:, 0:1]
                kc, vc = kf[_rows(WIN + start, d), :].astype(BF16), vf[_rows(WIN + start, d), :].astype(BF16)
                pc, dsc = probs(q, kc, vc, dy, lse_col, dl_col, own)
                dk_acc[rows, :] += _dot(dsc, q, _TN) * ATT_SCALE
                dv_acc[rows, :] += _dot(pc, dy, _TN)

                def block_before():
                    kp = kf[_rows(WIN + start - ATT_BLOCK * d, d), :].astype(BF16)
                    vp = vf[_rows(WIN + start - ATT_BLOCK * d, d), :].astype(BF16)
                    pp, dsp = probs(q, kp, vp, dy, lse_col, dl_col, before)
                    dq_acc[rows, :] += (_dot(dsc, kc) + _dot(dsp, kp)) * ATT_SCALE
                    return pp, dsp

                if idx >= d:
                    pp, dsp = block_before()
                    prows = _rows(start - ATT_BLOCK * d, d)
                    dk_acc[prows, :] += _dot(dsp, q, _TN) * ATT_SCALE
                    dv_acc[prows, :] += _dot(pp, dy, _TN)
                else:
                    @pl.when(w > 0)
                    def _():
                        block_before()

                    @pl.when(w == 0)
                    def _():
                        dq_acc[rows, :] += _dot(dsc, kc) * ATT_SCALE
                return carry

            for idx in range(N_BLOCKS):
                block(idx, 0)

            def next_window(r, carry, d=d):
                krows = _rows(WIN - ATT_BLOCK * d + r, d)
                rows = _rows(r, d)
                q, dy = qnf[rows, :].astype(BF16), dyn_ref[rows, :].astype(BF16)
                k, v = kf[_rows(2 * WIN - ATT_BLOCK * d + r, d), :].astype(BF16), vf[_rows(2 * WIN - ATT_BLOCK * d + r, d), :].astype(BF16)
                pn, dsn = probs(q, k, v, dy, ls_n[rows, :][:, 0:1], dl_n[rows, :][:, 0:1], before)
                dk_acc[krows, :] += _dot(dsn, q, _TN) * ATT_SCALE
                dv_acc[krows, :] += _dot(pn, dy, _TN)
                return carry

            @pl.when(w < n_win - 1)
            def _():
                for r in range(d):
                    next_window(r, 0)

        for part, acc_ref in enumerate((dq_acc, dk_acc, dv_acc)):
            out_ref[part] = acc_ref[...].astype(BF16)

    win_scratch = lambda rows: pltpu.VMEM((rows, ATT_HEAD_DIM), F32)
    cur = lambda c: pl.BlockSpec(blk, lambda w, h: (w, c + h))
    return pl.pallas_call(
        body,
        grid=(n_win, ATT_HEADS),
        in_specs=[cur(0), pl.BlockSpec(blk, lambda w, h: (nxt(w), h)),
                  cur(ATT_HEADS), pl.BlockSpec(blk, lambda w, h: (prev(w), ATT_HEADS + h)),
                  cur(2 * ATT_HEADS), pl.BlockSpec(blk, lambda w, h: (prev(w), 2 * ATT_HEADS + h)),
                  cur(ATT_HEADS), pl.BlockSpec(blk, lambda w, h: (nxt(w), ATT_HEADS + h)),
                  cur(0), pl.BlockSpec(blk, lambda w, h: (nxt(w), h)),
                  pl.BlockSpec((WIN, LANES), lambda w, h: (w, 0)), pl.BlockSpec((WIN, LANES), lambda w, h: (nxt(w), 0))]
        + [ANY] * n_dep,
        out_specs=pl.BlockSpec((3, WIN, ATT_HEAD_DIM), lambda w, h: (0, w, h)),
        out_shape=jax.ShapeDtypeStruct((3, s_dim, D_ATT), BF16),
        scratch_shapes=[win_scratch(WIN), win_scratch(WIN), win_scratch(2 * WIN), win_scratch(2 * WIN)]
        + [win_scratch(WIN)] * 7,
        compiler_params=_cparams("parallel", "arbitrary"),
        name="attn_fused_bwd",
    )(qkv, qkv, qkv, qkv, qkv, qkv, dymix, dymix, y_att, y_att, lse, lse, *deps)


def _adamw(w, g, m, v, name):
    def fn(wb, gb, mb, vb):
        m2 = ADAM_B1 * mb + (1.0 - ADAM_B1) * gb
        v2 = ADAM_B2 * vb + (1.0 - ADAM_B2) * (gb * gb)
        m_hat = m2 / (1.0 - ADAM_B1 ** ADAM_STEP)
        v_hat = v2 / (1.0 - ADAM_B2 ** ADAM_STEP)
        delta = -ADAM_LR * (m_hat / (jnp.sqrt(v_hat) + ADAM_EPS) + ADAM_WD * wb)
        return delta, m2, v2
    cols = w.shape[1]
    tr = 128 if w.shape[0] % 128 == 0 else w.shape[0]
    return _rowcall(fn, [w, g, m, v], [], [(cols, F32)] * 3, [], name=name, tr=tr)


ANY = pl.BlockSpec(memory_space=pl.ANY)


def _position():
    x, y, c = lax.axis_index("x"), lax.axis_index("y"), lax.axis_index("c")
    chips = [(1 - x, y), (x, 1 - y), (1 - x, 1 - y)]
    return x, y, c, chips


def _remote(src, dst, send_sem, recv_sem, device):
    return pltpu.make_async_remote_copy(src_ref=src, dst_ref=dst, send_sem=send_sem, recv_sem=recv_sem,
                                        device_id=device, device_id_type=MESH)


def _handshake(peers):
    barrier = pltpu.get_barrier_semaphore()
    for p in peers:
        pl.semaphore_signal(barrier, inc=1, device_id=p, device_id_type=MESH)
    pl.semaphore_wait(barrier, len(peers))


def _gather_shards_async(shards, collective_id, name):
    n = len(shards)
    srcs = [jax.new_ref(s, memory_space=pltpu.MemorySpace.HBM) for s in shards]
    dsts = [jax.empty_ref(jax.ShapeDtypeStruct((N_CHIPS,) + s.shape, s.dtype), memory_space=pltpu.MemorySpace.HBM)
            for s in shards]

    @pl.kernel(mesh=plsc.ScalarSubcoreMesh(axis_name="seq", num_cores=1), name=name,
               scratch_types=(pltpu.SemaphoreType.DMA((6 * n,)), pltpu.SemaphoreType.DMA((6 * n,))),
               compiler_params=pltpu.CompilerParams(collective_id=collective_id))
    def launch(send_sems, recv_sems):
        x, y, c, chips = _position()
        sibling = (x, y, 1 - c)
        _handshake([(chip[0], chip[1], c) for chip in chips] + [sibling])

        def half(a, j, cc):
            h = shards[a].shape[0] // 2
            return dsts[a].at[j, pl.ds(cc * h, h), :]

        sent = []
        for a in range(n):
            h = shards[a].shape[0] // 2
            for j, chip in enumerate(chips):
                cp = _remote(srcs[a].at[pl.ds(c * h, h), :], half(a, 2 * x + y, c), send_sems.at[6 * a + j],
                             recv_sems.at[6 * a + j], (chip[0], chip[1], c))
                cp.start()
                sent.append(cp)
        for a in range(n):
            for j, chip in enumerate(chips):
                landed = half(a, 2 * chip[0] + chip[1], c)
                _remote(landed, landed, send_sems.at[6 * a + j], recv_sems.at[6 * a + j], (x, y, c)).wait_recv()
                cp = _remote(landed, landed, send_sems.at[6 * a + 3 + j], recv_sems.at[6 * a + 3 + j], sibling)
                cp.start()
                sent.append(cp)
        for a in range(n):
            for j, chip in enumerate(chips):
                handed = half(a, 2 * chip[0] + chip[1], 1 - c)
                _remote(handed, handed, send_sems.at[6 * a + 3 + j], recv_sems.at[6 * a + 3 + j], (x, y, c)).wait_recv()
        for cp in sent:
            cp.wait_send()

    launch()
    return [d[...] for d in dsts]


IN_COLS = {"z": (0, D_SSM), "xbc": (D_SSM, D_SSM + D_XBC), "dt": (D_SSM + D_XBC, D_SSM + D_XBC + SSM_HEADS),
           "qkv": (D_SSM + D_XBC + SSM_HEADS, D_IN_PROJ)}


def _cols_from_quarters(quarters, lo, hi):
    parts = []
    for q in range(N_CHIPS):
        a, b = max(lo, q * W_IN_SHARD), min(hi, (q + 1) * W_IN_SHARD)
        if a < b:
            parts.append(quarters[q][:, a - q * W_IN_SHARD:b - q * W_IN_SHARD])
    return parts[0] if len(parts) == 1 else jnp.concatenate(parts, axis=1)


def _quarters_from_cols(pieces):
    quarters = []
    for q in range(N_CHIPS):
        parts = []
        for name, (lo, hi) in IN_COLS.items():
            a, b = max(lo, q * W_IN_SHARD), min(hi, (q + 1) * W_IN_SHARD)
            if a < b:
                parts.append(pieces[name][:, a - lo:b - lo])
        quarters.append(jnp.concatenate(parts, axis=1))
    return jnp.stack(quarters)


def _by_chip(own, fetched):
    me = 2 * lax.axis_index("x") + lax.axis_index("y")
    return lax.dynamic_update_slice(fetched, own[None], (me, 0, 0))


def _add_sibling(grad, got, place, name, deps=()):
    nq, rows, cols = grad.shape
    h = rows // 2
    tr = 128
    nb = h // tr

    def body(place_ref, a_ref, b_ref, *rest):
        own_ref, ob_ref = rest[len(deps):]
        total = a_ref[...] + b_ref[...]
        ob_ref[...] = total.astype(BF16)

        @pl.when(pl.program_id(1) == place_ref[1])
        def _():
            own_ref[...] = total

    return pl.pallas_call(
        body,
        grid_spec=pltpu.PrefetchScalarGridSpec(
            num_scalar_prefetch=1, grid=(nb, nq),
            in_specs=[pl.BlockSpec((None, tr, cols), lambda i, q, p: (q, p[0] * nb + i, 0)),
                      pl.BlockSpec((None, tr, cols), lambda i, q, p: (q, i, 0))] + [ANY] * len(deps),
            out_specs=[pl.BlockSpec((tr, cols), lambda i, q, p: (i, 0)),
                       pl.BlockSpec((None, tr, cols), lambda i, q, p: (q, i, 0))]),
        out_shape=[jax.ShapeDtypeStruct((h, cols), F32), jax.ShapeDtypeStruct((nq, h, cols), BF16)],
        compiler_params=_cparams("parallel", "arbitrary"),
        name=name,
    )(place, grad, got, *deps)


def _add_chips(part, got, name, deps=()):
    h, cols = part.shape
    tr = 128

    def body(p_ref, g0_ref, g1_ref, g2_ref, *rest):
        o_ref = rest[len(deps)]
        o_ref[...] = ((p_ref[...] + g0_ref[...].astype(F32)) + g1_ref[...].astype(F32)) + g2_ref[...].astype(F32)

    got_spec = lambda j: pl.BlockSpec((None, tr, cols), lambda i: (j, i, 0))
    row_spec = pl.BlockSpec((tr, cols), lambda i: (i, 0))
    return pl.pallas_call(
        body,
        grid=(h // tr,),
        in_specs=[row_spec, got_spec(0), got_spec(1), got_spec(2)] + [ANY] * len(deps),
        out_specs=row_spec,
        out_shape=jax.ShapeDtypeStruct((h, cols), F32),
        compiler_params=_cparams("parallel"),
        name=name,
    )(part, got, got, got, *deps)


def _sequencer_exchange(src, out_shape, collective_id, name, plan, n_copies):
    src_ref = jax.new_ref(src, memory_space=pltpu.MemorySpace.HBM)
    dst_ref = jax.empty_ref(out_shape, memory_space=pltpu.MemorySpace.HBM)

    @pl.kernel(mesh=plsc.ScalarSubcoreMesh(axis_name="seq", num_cores=1), name=name,
               scratch_types=(pltpu.SemaphoreType.DMA((n_copies,)), pltpu.SemaphoreType.DMA((n_copies,))),
               compiler_params=pltpu.CompilerParams(collective_id=collective_id))
    def launch(send_sems, recv_sems):
        x, y, c, chips = _position()
        copies = plan(src_ref, dst_ref, x, y, c, chips)
        _handshake([peer for _, _, peer in copies])
        started = []
        for k, (s, d, peer) in enumerate(copies):
            cp = _remote(s, d, send_sems.at[k], recv_sems.at[k], peer)
            cp.start()
            started.append(cp)
        for cp in started:
            cp.wait()

    launch()
    return dst_ref[...]


class _AsyncReduceScatter:
    def __init__(self, grad, nm, first_id):
        self.grad, self.nm, self.first_id = grad, nm, first_id
        nq, rows, cols = grad.shape
        h = self.h = rows // 2

        def to_sibling(s, d, x, y, c, chips):
            return [(s.at[:, pl.ds((1 - c) * h, h), :], d, (x, y, 1 - c))]

        self.from_sibling = _sequencer_exchange(grad, jax.ShapeDtypeStruct((nq, h, cols), F32), first_id,
                                                f"rs_sibling_{nm}", to_sibling, 1)

    def sibling_sum(self, not_before=()):
        cols = self.grad.shape[2]
        place = jnp.stack([lax.axis_index("c"), 2 * lax.axis_index("x") + lax.axis_index("y")]).astype(jnp.int32)
        self.part, self.part_b = _add_sibling(self.grad, self.from_sibling, place, f"add_sibling_{self.nm}", not_before)

        def to_chips(s, d, x, y, c, chips):
            return [(s.at[2 * chip[0] + chip[1]], d.at[j], (chip[0], chip[1], c)) for j, chip in enumerate(chips)]

        self.from_chips = _sequencer_exchange(self.part_b, jax.ShapeDtypeStruct((3, self.h, cols), BF16),
                                              self.first_id + 1, f"rs_quarters_{self.nm}", to_chips, 3)
        return self.part_b

    def chip_sum(self, not_before=()):
        cols = self.grad.shape[2]
        self.half = _add_chips(self.part, self.from_chips, f"add_chips_{self.nm}", not_before)

        def whole_to_sibling(s, d, x, y, c, chips):
            return [(s, d, (x, y, 1 - c))]

        self.other = _sequencer_exchange(self.half, jax.ShapeDtypeStruct((self.h, cols), F32), self.first_id + 2,
                                         f"rs_share_{self.nm}", whole_to_sibling, 1)
        return self.half

    def share(self):
        return self.half, self.other


def _after(x, deps, name):
    def body(x_ref, *rest):
        rest[-1][...] = x_ref[...]

    vm = pl.BlockSpec(memory_space=pltpu.VMEM)
    return pl.pallas_call(body, in_specs=[vm] + [ANY] * len(deps), out_specs=vm,
                          out_shape=jax.ShapeDtypeStruct(x.shape, x.dtype), name=name)(x, *deps)


def _adamw_halves(w, mine, other, m, v, name):
    rows, cols = w.shape
    tr = 128
    nb = rows // 2 // tr
    c_arr = lax.axis_index("c").astype(jnp.int32).reshape(1)

    def body(c_ref, w_ref, a_ref, b_ref, m_ref, v_ref, g_out, d_out, m_out, v_out):
        is_mine = (pl.program_id(0) // nb) == c_ref[0]
        g = jnp.where(is_mine, a_ref[...], b_ref[...])
        wb, mb, vb = w_ref[...], m_ref[...], v_ref[...]
        m2 = ADAM_B1 * mb + (1.0 - ADAM_B1) * g
        v2 = ADAM_B2 * vb + (1.0 - ADAM_B2) * (g * g)
        m_hat = m2 / (1.0 - ADAM_B1 ** ADAM_STEP)
        v_hat = v2 / (1.0 - ADAM_B2 ** ADAM_STEP)
        g_out[...] = g
        d_out[...] = -ADAM_LR * (m_hat / (jnp.sqrt(v_hat) + ADAM_EPS) + ADAM_WD * wb)
        m_out[...] = m2
        v_out[...] = v2

    full = pl.BlockSpec((tr, cols), lambda i, c: (i, 0))
    half = pl.BlockSpec((tr, cols), lambda i, c: (i % nb, 0))
    return pl.pallas_call(
        body,
        grid_spec=pltpu.PrefetchScalarGridSpec(
            num_scalar_prefetch=1, grid=(rows // tr,),
            in_specs=[full, half, half, full, full], out_specs=[full] * 4),
        out_shape=[jax.ShapeDtypeStruct((rows, cols), F32)] * 4,
        compiler_params=_cparams("parallel"),
        name=name,
    )(c_arr, w, mine, other, m, v)


def _all_sum_small(v):
    n_dev = 8

    def body(v_ref, o_ref, gath, send_sems, recv_sems):
        x, y, c, _ = _position()
        me = 4 * x + 2 * y + c
        gath[me] = v_ref[...]
        copies = []
        for k in range(1, n_dev):
            peer = tuple(1 - p if (k >> s) & 1 else p for p, s in ((x, 2), (y, 1), (c, 0)))
            cp = _remote(v_ref, gath.at[me], send_sems.at[k - 1], recv_sems.at[k - 1], peer)
            cp.start()
            copies.append(cp)
        for cp in copies:
            cp.wait()
        acc = gath[0]
        for i in range(1, n_dev):
            acc = acc + gath[i]
        o_ref[...] = acc

    vm = pl.BlockSpec(memory_space=pltpu.VMEM)
    return pl.pallas_call(
        body,
        in_specs=[vm],
        out_specs=vm,
        out_shape=jax.ShapeDtypeStruct(v.shape, F32),
        scratch_shapes=[pltpu.VMEM((n_dev,) + v.shape, F32), pltpu.SemaphoreType.DMA((n_dev - 1,)),
                        pltpu.SemaphoreType.DMA((n_dev - 1,))],
        name="all_sum_small",
    )(v)


def _pack_rows(vectors):
    rows = []
    for v in vectors:
        flat = v.reshape(-1).astype(F32)
        rows.append(jnp.pad(flat, (0, (-flat.shape[0]) % LANES)).reshape(-1, LANES))
    out = jnp.concatenate(rows, axis=0)
    return jnp.pad(out, ((0, (-out.shape[0]) % 8), (0, 0)))


def _unpack_rows(packed, shapes):
    outs, r = [], 0
    for shp in shapes:
        size = math.prod(shp)
        nr = -(-size // LANES)
        outs.append(packed[r:r + nr].reshape(-1)[:size].reshape(shp))
        r += nr
    return outs


def _relu_sq(acc):
    r = jnp.maximum(acc, 0.0)
    return r, r * r


def _relu_sq_bwd(acc, r):
    return (acc * (2.0 * r.astype(F32)),)


def kernel(x, norm_mix_pre, w_in, conv_w, conv_b, dt_bias, a_log, d_skip, ssm_norm_w, w_out, norm_mix_post, norm_mlp_pre, w_up, w_down, norm_mlp_post, loss_target, m_norm_mix_pre, m_w_in, m_conv_w, m_conv_b, m_dt_bias, m_a_log, m_d_skip, m_ssm_norm_w, m_w_out, m_norm_mix_post, m_norm_mlp_pre, m_w_up, m_w_down, m_norm_mlp_post, v_norm_mix_pre, v_w_in, v_conv_w, v_conv_b, v_dt_bias, v_a_log, v_d_skip, v_ssm_norm_w, v_w_out, v_norm_mix_post, v_norm_mlp_pre, v_w_up, v_w_down, v_norm_mlp_post):
    s_dim = x.shape[1]
    xs, target = x[0], loss_target[0]
    chip = 2 * lax.axis_index("x") + lax.axis_index("y")

    own = [w_in[0].astype(BF16), w_out[0].astype(BF16), w_up[0].astype(BF16), w_down[0].astype(BF16)]
    fetched_in = _gather_shards_async(own[:1], 14, "gather_w_in")[0]
    conv_cols = D_XBC // N_CHIPS
    conv_placed = lax.dynamic_update_slice(jnp.zeros((8, D_XBC), F32), 0.5 * conv_w[0], (0, chip * conv_cols))
    conv_full = _all_sum_small(conv_placed.reshape(-1, LANES)).reshape(8, D_XBC)
    w8 = _perm_cols(conv_full.at[CONV_WIDTH].set(conv_b[0]))
    u = _pre_norm(xs, norm_mix_pre)
    fetched_in, u, w8, *rest = lax.optimization_barrier((fetched_in, u, w8, *own[1:]))
    fetched = [fetched_in] + _gather_shards_async(rest, 1, "gather_rest")
    g_in, g_out, g_up, g_down = [_by_chip(o, f) for o, f in zip(own, fetched)]
    w_z = _cols_from_quarters(g_in, *IN_COLS["z"])
    w_xbc = _perm_cols(_cols_from_quarters(g_in, *IN_COLS["xbc"]))
    w_dt = jnp.pad(_cols_from_quarters(g_in, *IN_COLS["dt"]), ((0, 0), (0, LANES - SSM_HEADS)))
    w_qkv = _cols_from_quarters(g_in, *IN_COLS["qkv"])
    w_out_full = g_out.reshape(D_MIX, D_MODEL)
    w_down_full = g_down.reshape(D_FF, D_MODEL)

    z = _matmul([(u, w_z, TK)], "nn", [F32], name="proj_z")
    xbc = _matmul([(u, w_xbc, TK)], "nn", [F32], name="proj_xbc")
    dt_raw = _matmul([(u, w_dt, TK)], "nn", [F32], name="proj_dt")
    qkv = _matmul([(u, w_qkv, TK)], "nn", [BF16], name="proj_qkv")
    xc = _conv_fwd(xbc, w8)
    dtg = _dt_to_groups(dt_raw)
    par = _pack_ssd_params(dt_bias[0], a_log[0], d_skip[0])
    y, y_ssm, states = _ssd_fwd(xc, z, dtg, par, ssm_norm_w)
    y_att, y_att_f32, lse = _attn_fused_fwd(qkv)
    y_mix = jnp.concatenate([y_ssm, y_att], axis=1)
    mix = _matmul([(y_mix, w_out_full, TK)], "nn", [F32], name="out_proj")
    h1, u2 = _post_pre_norm(xs, mix, norm_mix_post, norm_mlp_pre)
    hid, act = _matmul([(u2, g_up, TK)], "nn", [BF16, BF16], name="mlp_up", epilogue=_relu_sq)
    ff = _matmul([(act, w_down_full, TK)], "nn", [F32], name="mlp_down")
    dh2, dff, d_g4, loss_part = _tail(ff, h1, target, norm_mlp_post)

    dhid = _matmul([(dff, w_down_full, TK)], "nt", [BF16], name="mlp_down_dx", epilogue=_relu_sq_bwd, extras=[hid])
    weights = {"norm_mix_pre": (norm_mix_pre, m_norm_mix_pre, v_norm_mix_pre), "w_in": (w_in, m_w_in, v_w_in),
               "conv_w": (conv_w, m_conv_w, v_conv_w), "conv_b": (conv_b, m_conv_b, v_conv_b),
               "dt_bias": (dt_bias, m_dt_bias, v_dt_bias), "a_log": (a_log, m_a_log, v_a_log),
               "d_skip": (d_skip, m_d_skip, v_d_skip), "ssm_norm_w": (ssm_norm_w, m_ssm_norm_w, v_ssm_norm_w),
               "w_out": (w_out, m_w_out, v_w_out), "norm_mix_post": (norm_mix_post, m_norm_mix_post, v_norm_mix_post),
               "norm_mlp_pre": (norm_mlp_pre, m_norm_mlp_pre, v_norm_mlp_pre), "w_up": (w_up, m_w_up, v_w_up),
               "w_down": (w_down, m_w_down, v_w_down),
               "norm_mlp_post": (norm_mlp_post, m_norm_mlp_post, v_norm_mlp_post)}
    grads, delta, new_m, new_v = {}, {}, {}, {}

    def adamw_big(n, halves):
        w, m, v = weights[n]
        g_, d_, m_, v_ = _adamw_halves(w[0], halves[0], halves[1], m[0], v[0], f"adamw_{n}")
        grads[n], delta[n], new_m[n], new_v[n] = g_[None], d_[None], m_[None], v_[None]

    dw_down = _matmul([(act, dff, TK)], "tn", [F32], name="mlp_down_dw")
    rs_down = _AsyncReduceScatter(dw_down.reshape(N_CHIPS, D_FF // N_CHIPS, D_MODEL), "w_down", 11)
    dw_up = _matmul([(u2, dhid, TK)], "tn", [F32], name="mlp_up_dw", deps=[dw_down], out_quarters=True)
    rs_up = _AsyncReduceScatter(dw_up, "w_up", 8)
    du2 = _matmul([(dhid, g_up, TK)], "nt", [F32], name="mlp_up_dx",
                  deps=[rs_down.sibling_sum(not_before=[dw_up])])
    dh1, dmix, d_g3, d_g2 = _mid_bwd(du2, h1, dh2, mix, norm_mix_post, norm_mlp_pre,
                                     deps=[rs_up.sibling_sum(not_before=[du2])])
    dymix = _matmul([(dmix, w_out_full, TK)], "nt", [F32], name="out_proj_dx")
    dw_out = _matmul([(y_mix, dmix, TK)], "tn", [F32], name="out_proj_dw")
    rs_out = _AsyncReduceScatter(dw_out.reshape(N_CHIPS, D_MIX // N_CHIPS, D_MODEL), "w_out", 5)
    dqkv = _attn_fused_bwd(qkv, dymix, y_att_f32, lse)
    par_late = _after(par, [rs_down.chip_sum(not_before=[dqkv]), rs_out.sibling_sum(not_before=[dymix])],
                      "after_w_down")
    dxc, dz, ddtg, dpar, d_nw = _ssd_bwd(xc, z, dtg, par_late, ssm_norm_w, y, states, dymix)
    g_down = rs_down.share()
    dxbc, dw8 = _conv_bwd(xbc, _after(w8, [*g_down, rs_up.chip_sum(not_before=[dxc])], "after_w_up"), dxc)
    ddt = jnp.pad(_dt_from_groups(ddtg), ((0, 0), (0, LANES - SSM_HEADS))).astype(BF16)
    g_up = rs_up.share()
    dw_z = _matmul([(u, dz, TK)], "tn", [F32], name="proj_z_dw")
    dw_xbc = _matmul([(u, dxbc, TK)], "tn", [F32], name="proj_xbc_dw",
                     deps=[*g_up, rs_out.chip_sum(not_before=[dxbc])])
    g_out = rs_out.share()
    dw_dt = _matmul([(u, ddt, TK)], "tn", [F32], name="proj_dt_dw")
    dw_qkv = _matmul([(u, dqkv, TK)], "tn", [F32], name="proj_qkv_dw")
    dw_in = _quarters_from_cols({"z": dw_z, "xbc": _unperm_cols(dw_xbc), "dt": dw_dt[:, :SSM_HEADS], "qkv": dw_qkv})
    rs_in = _AsyncReduceScatter(dw_in, "w_in", 2)
    adamw_big("w_down", g_down)
    adamw_big("w_up", g_up)
    rs_in.sibling_sum(not_before=[delta["w_up"]])
    du = _matmul([(dz, w_z, TK_MULTI), (dxbc, w_xbc, TK_MULTI), (dqkv, w_qkv, TK_MULTI), (ddt, w_dt, LANES)], "nt",
                 [F32], name="proj_dx", deps=[*g_out, rs_in.part_b])
    grad_x, d_g1 = _first_bwd(du, xs, dh1, norm_mix_pre)
    adamw_big("w_out", g_out)
    rs_in.chip_sum(not_before=[grad_x, delta["w_out"]])

    dconv = _unperm_cols(dw8)
    d_bias, d_alog, d_dskip = _unpack_ssd_params(dpar)
    small_shapes = [(1, D_MODEL), (CONV_WIDTH, D_XBC), (1, D_XBC), (1, SSM_HEADS), (1, SSM_HEADS), (1, SSM_HEADS),
                    (1, D_SSM), (1, D_MODEL), (1, D_MODEL), (1, D_MODEL), (1, LANES)]
    summed = _unpack_rows(
        _all_sum_small(_pack_rows([d_g1, dconv[:CONV_WIDTH], dconv[CONV_WIDTH:CONV_WIDTH + 1], d_bias, d_alog,
                                   d_dskip, d_nw, d_g2, d_g3, d_g4, loss_part])), small_shapes)
    (g_g1, g_conv_full, g_conv_b, g_bias, g_alog, g_dskip, g_nw, g_g2, g_g3, g_g4, loss_row) = summed
    loss = loss_row[0, 0]
    g_conv_w = lax.dynamic_slice(g_conv_full, (0, chip * conv_cols), (CONV_WIDTH, conv_cols))[None]

    grads.update({"norm_mix_pre": g_g1, "conv_w": g_conv_w, "conv_b": g_conv_b, "dt_bias": g_bias,
                  "a_log": g_alog, "d_skip": g_dskip, "ssm_norm_w": g_nw, "norm_mix_post": g_g2,
                  "norm_mlp_pre": g_g3, "norm_mlp_post": g_g4})
    order = list(weights)
    small_names = [n for n in order if n not in ("w_in", "w_out", "w_up", "w_down")]
    small_w_shapes = [weights[n][0].shape for n in small_names]
    packed = [_pack_rows([weights[n][k] for n in small_names]) for k in range(3)]
    packed_g = _pack_rows([grads[n].reshape(weights[n][0].shape) for n in small_names])
    sd, sm, sv = _adamw(packed[0], packed_g, packed[1], packed[2], "adamw_small")
    for k, n in enumerate(small_names):
        grads[n] = grads[n].reshape(weights[n][0].shape)
    for res, pk in ((delta, sd), (new_m, sm), (new_v, sv)):
        for n, val in zip(small_names, _unpack_rows(pk, small_w_shapes)):
            res[n] = val
    adamw_big("w_in", rs_in.share())

    return (loss, grad_x[None], *[grads[n] for n in order], *[delta[n] for n in order],
            *[new_m[n] for n in order], *[new_v[n] for n in order])
```

```python
import math

import numpy as np
import jax
import jax.numpy as jnp
from jax import lax
from jax.experimental import pallas as pl
from jax.experimental.pallas import tpu as pltpu
from jax.experimental.pallas import tpu_sc as plsc

F32 = jnp.float32
BF16 = jnp.bfloat16

D_MODEL = 2048
SSM_HEAD_DIM = 64
SSM_GROUPS = 8
HEADS_PER_GROUP = 4
SSM_HEADS = SSM_GROUPS * HEADS_PER_GROUP
D_SSM = SSM_HEADS * SSM_HEAD_DIM
D_STATE = 128
CONV_WIDTH = 4
SSD_CHUNK = 128
D_XBC = D_SSM + 2 * SSM_GROUPS * D_STATE
GROUP_X = HEADS_PER_GROUP * SSM_HEAD_DIM
GROUP_COLS = GROUP_X + 2 * D_STATE
ATT_HEAD_DIM = 128
ATT_HEADS = 16
D_ATT = ATT_HEADS * ATT_HEAD_DIM
DILATIONS = (1, 4, 16)
ATT_BLOCK = 128
D_MIX = D_SSM + D_ATT
D_IN_PROJ = D_SSM + D_XBC + SSM_HEADS + 3 * D_ATT
D_FF = 4 * D_MODEL
EPS = 1e-6
N_CHIPS = 4
W_IN_SHARD = D_IN_PROJ // N_CHIPS

ADAM_LR = 0.001
ADAM_B1 = 0.9
ADAM_B2 = 0.999
ADAM_EPS = 1e-08
ADAM_WD = 0.01
ADAM_STEP = 10

LANES = 128
VMEM_LIMIT = 48 * 1024 * 1024
MESH = pl.DeviceIdType.MESH

_NN = (((1,), (0,)), ((), ()))
_NT = (((1,), (1,)), ((), ()))
_TN = (((0,), (0,)), ((), ()))


def _dot(a, b, dims=_NN):
    return lax.dot_general(a, b, dims, preferred_element_type=F32)


def _cparams(*sem):
    return pltpu.CompilerParams(dimension_semantics=sem, vmem_limit_bytes=VMEM_LIMIT)


TK = 2048
TK_MULTI = 1024


def _matmul(pairs, mode, out_dtypes, *, name, tm=1024, tn=1024, epilogue=None, extras=(), deps=(), out_quarters=False):
    a0, b0, _ = pairs[0]
    m_dim = a0.shape[-1] if mode == "tn" else a0.shape[-2]
    if b0.ndim == 3:
        n_dim = b0.shape[1] if mode == "nt" else b0.shape[0] * b0.shape[2]
    else:
        n_dim = b0.shape[0] if mode == "nt" else b0.shape[1]
    tm, tn = min(tm, m_dim), min(tn, n_dim)
    nks, offs = [], []
    for a, _, tk in pairs:
        k_part = a.shape[0] if mode == "tn" else a.shape[-1]
        k_dim = k_part * (a.shape[0] if a.ndim == 3 else 1)
        assert k_part % tk == 0, (name, k_part, tk)
        offs.append(sum(nks))
        nks.append(k_dim // tk)
    nk_total = sum(nks)
    assert m_dim % tm == 0 and n_dim % tn == 0, (name, m_dim, n_dim)
    dims = {"nn": _NN, "nt": _NT, "tn": _TN}[mode]
    n_pairs, n_extra, n_out = len(pairs), len(extras), len(out_dtypes)

    in_specs, operands = [], []
    for (a, b, tk), off, nk in zip(pairs, offs, nks):
        def kidx(k, off=off, nk=nk):
            return k if n_pairs == 1 else jnp.clip(k - off, 0, nk - 1)
        if mode == "tn":
            assert a.ndim == 2
            in_specs.append(pl.BlockSpec((tk, tm), lambda m, n, k, f=kidx: (f(k), m)))
        elif a.ndim == 3:
            per = a.shape[2] // tk
            in_specs.append(pl.BlockSpec((None, tm, tk), lambda m, n, k, f=kidx, per=per: (f(k) // per, m, f(k) % per)))
        else:
            in_specs.append(pl.BlockSpec((tm, tk), lambda m, n, k, f=kidx: (m, f(k))))
        if b.ndim == 3 and mode == "nt":
            per = b.shape[2] // tk
            in_specs.append(pl.BlockSpec((None, tn, tk), lambda m, n, k, f=kidx, per=per: (f(k) // per, n, f(k) % per)))
        elif b.ndim == 3:
            per = b.shape[2] // tn
            in_specs.append(pl.BlockSpec((None, tk, tn), lambda m, n, k, f=kidx, per=per: (n // per, f(k), n % per)))
        elif mode == "nt":
            in_specs.append(pl.BlockSpec((tn, tk), lambda m, n, k, f=kidx: (n, f(k))))
        else:
            in_specs.append(pl.BlockSpec((tk, tn), lambda m, n, k, f=kidx: (f(k), n)))
        operands += [a, b]
    for e in extras:
        in_specs.append(pl.BlockSpec((tm, tn), lambda m, n, k: (m, n)))
        operands.append(e)
    in_specs += [pl.BlockSpec(memory_space=pl.ANY)] * len(deps)
    operands += list(deps)
    first_out = 2 * n_pairs + n_extra + len(deps)
    if out_quarters:
        out_per_q = n_dim // N_CHIPS // tn
        out_dims = (N_CHIPS, m_dim, n_dim // N_CHIPS)
        out_spec = pl.BlockSpec((None, tm, tn), lambda m, n, k: (n // out_per_q, m, n % out_per_q))
    else:
        out_dims = (m_dim, n_dim)
        out_spec = pl.BlockSpec((tm, tn), lambda m, n, k: (m, n))

    def body(*refs):
        ab = refs[:2 * n_pairs]
        e_refs = refs[2 * n_pairs:2 * n_pairs + n_extra]
        o_refs = refs[first_out:first_out + n_out]

        def finish(total):
            vals = (total,) if epilogue is None else epilogue(total, *[e[...] for e in e_refs])
            for o_ref, v in zip(o_refs, vals):
                o_ref[...] = v.astype(o_ref.dtype)

        if nk_total == 1:
            finish(_dot(ab[0][...], ab[1][...], dims))
            return
        acc = refs[-1]
        k = pl.program_id(2)

        @pl.when(k == 0)
        def _():
            acc[...] = jnp.zeros_like(acc)

        for i in range(n_pairs):
            def accumulate(i=i):
                acc[...] += _dot(ab[2 * i][...], ab[2 * i + 1][...], dims)
            if n_pairs == 1:
                accumulate()
            else:
                pl.when((k >= offs[i]) & (k < offs[i] + nks[i]))(accumulate)

        @pl.when(k == nk_total - 1)
        def _():
            finish(acc[...])

    outs = pl.pallas_call(
        body,
        grid=(m_dim // tm, n_dim // tn, nk_total),
        in_specs=in_specs,
        out_specs=[out_spec for _ in out_dtypes],
        out_shape=[jax.ShapeDtypeStruct(out_dims, dt) for dt in out_dtypes],
        scratch_shapes=[pltpu.VMEM((tm, tn), F32)] if nk_total > 1 else [],
        compiler_params=_cparams("parallel", "parallel", "arbitrary"),
        name=name,
    )(*operands)
    return outs[0] if n_out == 1 else outs


def _rowcall(fn, rows, vecs, row_outs, acc_widths, *, name, tr=256, row_cols=None, deps=()):
    s_dim = rows[0].shape[0]
    assert s_dim % tr == 0
    row_cols = row_cols or [None] * len(rows)
    n_r, n_v, n_ro, n_acc = len(rows), len(vecs), len(row_outs), len(acc_widths)
    in_specs = []
    for r, rc in zip(rows, row_cols):
        if rc is None:
            in_specs.append(pl.BlockSpec((tr, r.shape[1]), lambda i: (i, 0)))
        else:
            in_specs.append(pl.BlockSpec((tr, rc[0]), lambda i, c=rc[1]: (i, c)))
    for v in vecs:
        in_specs.append(pl.BlockSpec(v.shape, lambda i, nd=v.ndim: (0,) * nd))
    in_specs += [pl.BlockSpec(memory_space=pl.ANY)] * len(deps)
    n_d = len(deps)

    def body(*refs):
        ins = [r[...] for r in refs[:n_r + n_v]]
        ro = refs[n_r + n_v + n_d:n_r + n_v + n_d + n_ro]
        ao = refs[n_r + n_v + n_d + n_ro:]
        outs = fn(*ins)
        for ref, v in zip(ro, outs[:n_ro]):
            ref[...] = v.astype(ref.dtype)
        if n_acc:
            @pl.when(pl.program_id(0) == 0)
            def _():
                for ref in ao:
                    ref[...] = jnp.zeros_like(ref)
            for ref, v in zip(ao, outs[n_ro:]):
                ref[...] += v

    outs = pl.pallas_call(
        body,
        grid=(s_dim // tr,),
        in_specs=in_specs,
        out_specs=[pl.BlockSpec((tr, w), lambda i: (i, 0)) for w, _ in row_outs]
        + [pl.BlockSpec((1, w), lambda i: (0, 0)) for w in acc_widths],
        out_shape=[jax.ShapeDtypeStruct((s_dim, w), dt) for w, dt in row_outs]
        + [jax.ShapeDtypeStruct((1, w), F32) for w in acc_widths],
        compiler_params=_cparams("arbitrary"),
        name=name,
    )(*rows, *vecs, *deps)
    return outs


def _nrm(x, g):
    r = lax.rsqrt(jnp.mean(x * x, axis=-1, keepdims=True) + EPS)
    n = x * r
    return n * g, n, r


def _nrm_bwd(dy, n, r, g):
    dn = dy * g
    dx = r * (dn - n * jnp.mean(dn * n, axis=-1, keepdims=True))
    return dx, jnp.sum(dy * n, axis=0, keepdims=True)


def _sigmoid(x):
    return 1.0 / (1.0 + jnp.exp(-x))


def _softplus(x):
    return jnp.maximum(x, 0.0) + jnp.log(1.0 + jnp.exp(-jnp.abs(x)))


def _pre_norm(x, g1):
    def fn(xb, g):
        return (_nrm(xb, g)[0],)
    return _rowcall(fn, [x], [g1], [(D_MODEL, BF16)], [], name="pre_norm")[0]


def _post_pre_norm(x, mix, g2, g3):
    def fn(xb, mb, g2b, g3b):
        h1 = xb + _nrm(mb, g2b)[0]
        return h1, _nrm(h1, g3b)[0]
    return _rowcall(fn, [x, mix], [g2, g3], [(D_MODEL, F32), (D_MODEL, BF16)], [], name="post_pre_norm")


def _tail(ff, h1, target, g4):
    def fn(ffb, h1b, tb, g):
        y, n, r = _nrm(ffb, g)
        e = h1b + y - tb
        loss = 0.5 * jnp.sum(jnp.sum(e * e, axis=-1, keepdims=True) * (1.0 / D_MODEL), axis=0, keepdims=True)
        dh2 = e * (1.0 / D_MODEL)
        dff, dg = _nrm_bwd(dh2, n, r, g)
        return dh2, dff, dg, jnp.broadcast_to(loss, (1, LANES))
    return _rowcall(fn, [ff, h1, target], [g4], [(D_MODEL, F32), (D_MODEL, BF16)], [D_MODEL, LANES], name="tail")


def _mid_bwd(du2, h1, dh2, mix, g2, g3, deps=()):
    def fn(du2b, h1b, dh2b, mb, g2b, g3b):
        _, n3, r3 = _nrm(h1b, g3b)
        d3, dg3 = _nrm_bwd(du2b, n3, r3, g3b)
        dh1 = dh2b + d3
        _, n2, r2 = _nrm(mb, g2b)
        dmix, dg2 = _nrm_bwd(dh1, n2, r2, g2b)
        return dh1, dmix, dg3, dg2
    return _rowcall(fn, [du2, h1, dh2, mix], [g2, g3], [(D_MODEL, F32), (D_MODEL, BF16)], [D_MODEL, D_MODEL],
                    name="mid_bwd", deps=deps)


def _first_bwd(du, x, dh1, g1):
    def fn(dub, xb, dh1b, g):
        _, n, r = _nrm(xb, g)
        dx, dg = _nrm_bwd(dub, n, r, g)
        return dh1b + dx, dg
    return _rowcall(fn, [du, x, dh1], [g1], [(D_MODEL, F32)], [D_MODEL], name="first_bwd")


CONV_TILE = 256
CONV_ROWS = 256
PAD = 8


def _conv_taps(w):
    return [w[k:k + 1, :] for k in range(CONV_WIDTH)], w[CONV_WIDTH:CONV_WIDTH + 1, :]


def _conv_fwd(xbc, w8):
    s_dim, c_dim = xbc.shape
    n_steps = s_dim // CONV_ROWS

    def body(x_ref, w_ref, o_ref, xp):
        xp[0:PAD, :] = jnp.zeros((PAD, CONV_TILE), F32)
        xp[PAD:PAD + s_dim, :] = x_ref[...]
        taps, bias = _conv_taps(w_ref[...])

        def step(c, carry):
            base = pl.multiple_of(c * CONV_ROWS, CONV_ROWS)
            win = xp[pl.ds(base, CONV_ROWS + PAD), :]
            pre = bias + taps[3] * win[PAD:, :]
            for j in range(1, CONV_WIDTH):
                pre = pre + taps[3 - j] * pltpu.roll(win, j, axis=0)[PAD:, :]
            o_ref[pl.ds(base, CONV_ROWS), :] = pre * _sigmoid(pre)
            return carry

        lax.fori_loop(0, n_steps, step, 0)

    return pl.pallas_call(
        body,
        grid=(c_dim // CONV_TILE,),
        in_specs=[pl.BlockSpec((s_dim, CONV_TILE), lambda j: (0, j)), pl.BlockSpec((8, CONV_TILE), lambda j: (0, j))],
        out_specs=pl.BlockSpec((s_dim, CONV_TILE), lambda j: (0, j)),
        out_shape=jax.ShapeDtypeStruct((s_dim, c_dim), F32),
        scratch_shapes=[pltpu.VMEM((s_dim + 2 * PAD, CONV_TILE), F32)],
        compiler_params=_cparams("parallel"),
        name="conv_fwd",
    )(xbc, w8)


def _conv_bwd(xbc, w8, dxc):
    s_dim, c_dim = xbc.shape
    n_steps = s_dim // CONV_ROWS

    def body(x_ref, w_ref, d_ref, dx_ref, dw_ref, xp, dp):
        xp[0:PAD, :] = jnp.zeros((PAD, CONV_TILE), F32)
        xp[PAD:PAD + s_dim, :] = x_ref[...]
        dp[PAD + s_dim:, :] = jnp.zeros((PAD, CONV_TILE), F32)
        taps, bias = _conv_taps(w_ref[...])

        def step1(c, sums):
            base = pl.multiple_of(c * CONV_ROWS, CONV_ROWS)
            win = xp[pl.ds(base, CONV_ROWS + PAD), :]
            shifted = [win[PAD:, :]] + [pltpu.roll(win, j, axis=0)[PAD:, :] for j in range(1, CONV_WIDTH)]
            pre = bias
            for j in range(CONV_WIDTH):
                pre = pre + taps[3 - j] * shifted[j]
            sg = _sigmoid(pre)
            dpre = d_ref[pl.ds(base, CONV_ROWS), :] * (sg * (1.0 + pre * (1.0 - sg)))
            dp[pl.ds(base + PAD, CONV_ROWS), :] = dpre
            new = [sums[k] + jnp.sum(dpre * shifted[3 - k], axis=0, keepdims=True) for k in range(CONV_WIDTH)]
            new.append(sums[CONV_WIDTH] + jnp.sum(dpre, axis=0, keepdims=True))
            return tuple(new)

        zero = jnp.zeros((1, CONV_TILE), F32)
        sums = lax.fori_loop(0, n_steps, step1, (zero,) * (CONV_WIDTH + 1))
        dw_ref[...] = jnp.zeros((8, CONV_TILE), F32)
        for k in range(CONV_WIDTH + 1):
            dw_ref[k:k + 1, :] = sums[k]

        def step2(c, carry):
            base = pl.multiple_of(c * CONV_ROWS, CONV_ROWS)
            win = dp[pl.ds(base + PAD, CONV_ROWS + PAD), :]
            dx = taps[3] * win[:CONV_ROWS, :]
            for j in range(1, CONV_WIDTH):
                dx = dx + taps[3 - j] * pltpu.roll(win, CONV_ROWS + PAD - j, axis=0)[:CONV_ROWS, :]
            dx_ref[pl.ds(base, CONV_ROWS), :] = dx.astype(BF16)
            return carry

        lax.fori_loop(0, n_steps, step2, 0)

    col = lambda j: (0, j)
    return pl.pallas_call(
        body,
        grid=(c_dim // CONV_TILE,),
        in_specs=[pl.BlockSpec((s_dim, CONV_TILE), col), pl.BlockSpec((8, CONV_TILE), col),
                  pl.BlockSpec((s_dim, CONV_TILE), col)],
        out_specs=[pl.BlockSpec((s_dim, CONV_TILE), col), pl.BlockSpec((8, CONV_TILE), col)],
        out_shape=[jax.ShapeDtypeStruct((s_dim, c_dim), BF16), jax.ShapeDtypeStruct((8, c_dim), F32)],
        scratch_shapes=[pltpu.VMEM((s_dim + 2 * PAD, CONV_TILE), F32), pltpu.VMEM((s_dim + 2 * PAD, CONV_TILE), F32)],
        compiler_params=_cparams("parallel"),
        name="conv_bwd",
    )(xbc, w8, dxc)


def _perm_cols(a):
    parts = []
    for g in range(SSM_GROUPS):
        parts += [a[..., g * GROUP_X:(g + 1) * GROUP_X],
                  a[..., D_SSM + g * D_STATE:D_SSM + (g + 1) * D_STATE],
                  a[..., D_SSM + SSM_GROUPS * D_STATE + g * D_STATE:D_SSM + SSM_GROUPS * D_STATE + (g + 1) * D_STATE]]
    return jnp.concatenate(parts, axis=-1)


def _unperm_cols(a):
    xs = [a[..., g * GROUP_COLS:g * GROUP_COLS + GROUP_X] for g in range(SSM_GROUPS)]
    bs = [a[..., g * GROUP_COLS + GROUP_X:g * GROUP_COLS + GROUP_X + D_STATE] for g in range(SSM_GROUPS)]
    cs = [a[..., g * GROUP_COLS + GROUP_X + D_STATE:(g + 1) * GROUP_COLS] for g in range(SSM_GROUPS)]
    return jnp.concatenate(xs + bs + cs, axis=-1)


def _dt_to_groups(dt):
    s_dim = dt.shape[0]
    t = dt[:, :SSM_HEADS].reshape(s_dim, SSM_GROUPS, HEADS_PER_GROUP).transpose(1, 0, 2)
    return jnp.pad(t, ((0, 0), (0, 0), (0, LANES - HEADS_PER_GROUP)))


def _dt_from_groups(dtg):
    s_dim = dtg.shape[1]
    return dtg[:, :, :HEADS_PER_GROUP].transpose(1, 0, 2).reshape(s_dim, SSM_HEADS)


def _pack_ssd_params(dt_bias, a_log, d_skip):
    rows = jnp.stack([p.reshape(SSM_GROUPS, HEADS_PER_GROUP) for p in (dt_bias, a_log, d_skip)], axis=1)
    return jnp.pad(rows, ((0, 0), (0, 8 - 3), (0, LANES - HEADS_PER_GROUP)))


def _unpack_ssd_params(par):
    return tuple(par[:, k, :HEADS_PER_GROUP].reshape(SSM_HEADS) for k in range(3))


Q = SSD_CHUNK


def _split3(v):
    hi = v.astype(BF16)
    r1 = v - hi.astype(F32)
    mid = r1.astype(BF16)
    lo = (r1 - mid.astype(F32)).astype(BF16)
    return hi, mid, lo


def _dot_l01(t01, v):
    return sum(_dot(t01, p) for p in _split3(v))


def _dot_r01(v, e01):
    return sum(_dot(p, e01) for p in _split3(v))


def _ssd_consts():
    row = lax.broadcasted_iota(jnp.int32, (Q, Q), 0)
    col = lax.broadcasted_iota(jnp.int32, (Q, Q), 1)
    causal = row >= col
    tril = causal.astype(BF16)
    triu = (col >= row).astype(BF16)
    er = lax.broadcasted_iota(jnp.int32, (LANES, GROUP_X), 0)
    ec = lax.broadcasted_iota(jnp.int32, (LANES, GROUP_X), 1) // SSM_HEAD_DIM
    expand = (er == ec).astype(BF16)
    rr = lax.broadcasted_iota(jnp.int32, (GROUP_X, LANES), 0) // SSM_HEAD_DIM
    rc = lax.broadcasted_iota(jnp.int32, (GROUP_X, LANES), 1)
    reduce = (rr == rc).astype(BF16)
    lane_head = lax.broadcasted_iota(jnp.int32, (Q, GROUP_X), 1) // SSM_HEAD_DIM
    return causal, tril, triu, expand, reduce, lane_head


def _ssd_common(xc_ref, dt_ref, par_ref, consts):
    causal, tril, _, expand, _, _ = consts
    par = par_ref[...]
    bias, alog, dsk = par[0:1, :], par[1:2, :], par[2:3, :]
    a_neg = -jnp.exp(alog)
    dtr = dt_ref[...] + bias
    dt = _softplus(dtr)
    s = _dot_l01(tril, dt * a_neg)
    dt_x = _dot_r01(dt, expand)
    s_x = _dot_r01(s, expand)
    dsk_x = _dot_r01(jnp.broadcast_to(dsk, (8, LANES)), expand)[0:1, :]
    blk = xc_ref[...]
    x = blk[:, :GROUP_X]
    bm = blk[:, GROUP_X:GROUP_X + D_STATE].astype(BF16)
    cm = blk[:, GROUP_X + D_STATE:].astype(BF16)
    xdt = x * dt_x
    g = _dot(cm, bm, _NT)
    return dict(a_neg=a_neg, dtr=dtr, dt=dt, s=s, s_t=s.T, dt_x=dt_x, s_x=s_x, dsk_x=dsk_x, x=x, bm=bm, cm=cm,
                xdt=xdt, g=g)


def _decay(v, r, causal):
    diff = v["s"][:, r:r + 1] - v["s_t"][r:r + 1, :]
    return jnp.exp(jnp.where(causal, diff, -jnp.inf))


def _ssd_specs(n_chunks, rev):
    cidx = (lambda c: n_chunks - 1 - c) if rev else (lambda c: c)
    xc = pl.BlockSpec((Q, GROUP_COLS), lambda g, c: (cidx(c), g))
    gx = pl.BlockSpec((Q, GROUP_X), lambda g, c: (cidx(c), g))
    dt = pl.BlockSpec((None, Q, LANES), lambda g, c: (g, cidx(c), 0))
    par = pl.BlockSpec((None, 8, LANES), lambda g, c: (g, 0, 0))
    nw = pl.BlockSpec((1, GROUP_X), lambda g, c: (0, g))
    hs = pl.BlockSpec((None, None, D_STATE, GROUP_X), lambda g, c: (cidx(c), g, 0, 0))
    return xc, gx, dt, par, nw, hs


def _ssd_fwd(xc, z, dtg, par, nw):
    s_dim = xc.shape[0]
    n_chunks = s_dim // Q
    xc_s, gx_s, dt_s, par_s, nw_s, hs_s = _ssd_specs(n_chunks, False)

    def body(xc_ref, z_ref, dt_ref, par_ref, nw_ref, y_ref, ys_ref, hs_ref, ht):
        @pl.when(pl.program_id(1) == 0)
        def _():
            ht[...] = jnp.zeros_like(ht)

        consts = _ssd_consts()
        causal, lane_head = consts[0], consts[5]
        v = _ssd_common(xc_ref, dt_ref, par_ref, consts)
        xdt_b = v["xdt"].astype(BF16)
        yd = jnp.zeros((Q, GROUP_X), F32)
        for r in range(HEADS_PER_GROUP):
            m = (v["g"] * _decay(v, r, causal)).astype(BF16)
            yd = yd + _dot(m, jnp.where(lane_head == r, xdt_b, jnp.zeros_like(xdt_b)))
        h = ht[...]
        hs_ref[...] = h
        yo = jnp.exp(v["s_x"]) * _dot(v["cm"], h.astype(BF16))
        y = yd + yo + v["dsk_x"] * v["x"]
        s_last = v["s_x"][Q - 1:Q, :]
        snew = _dot(v["bm"], (v["xdt"] * jnp.exp(s_last - v["s_x"])).astype(BF16), _TN)
        ht[...] = jnp.exp(s_last) * h + snew
        zz = z_ref[...]
        yg = y * (zz * _sigmoid(zz))
        y_ref[...] = y
        ys_ref[...] = _nrm(yg, nw_ref[...])[0].astype(BF16)

    return pl.pallas_call(
        body,
        grid=(SSM_GROUPS, n_chunks),
        in_specs=[xc_s, gx_s, dt_s, par_s, nw_s],
        out_specs=[gx_s, gx_s, hs_s],
        out_shape=[jax.ShapeDtypeStruct((s_dim, D_SSM), F32), jax.ShapeDtypeStruct((s_dim, D_SSM), BF16),
                   jax.ShapeDtypeStruct((n_chunks, SSM_GROUPS, D_STATE, GROUP_X), F32)],
        scratch_shapes=[pltpu.VMEM((D_STATE, GROUP_X), F32)],
        compiler_params=_cparams("parallel", "arbitrary"),
        name="ssd_fwd",
    )(xc, z, dtg, par, nw)


def _ssd_bwd(xc, z, dtg, par, nw, y, hs, dymix):
    s_dim = xc.shape[0]
    n_chunks = s_dim // Q
    xc_s, gx_s, dt_s, par_s, nw_s, hs_s = _ssd_specs(n_chunks, True)

    def body(xc_ref, z_ref, dt_ref, par_ref, nw_ref, y_ref, hs_ref, dys_ref,
             dxc_ref, dz_ref, ddt_ref, dpar_ref, dnw_ref, dht):
        @pl.when(pl.program_id(1) == 0)
        def _():
            dht[...] = jnp.zeros_like(dht)
            dpar_ref[...] = jnp.zeros_like(dpar_ref)
            dnw_ref[...] = jnp.zeros_like(dnw_ref)

        consts = _ssd_consts()
        causal, _, triu, _, reduce, lane_head = consts
        v = _ssd_common(xc_ref, dt_ref, par_ref, consts)
        x, bm, cm, xdt, s_x = v["x"], v["bm"], v["cm"], v["xdt"], v["s_x"]
        h = hs_ref[...]
        hb = h.astype(BF16)
        es_x = jnp.exp(s_x)
        yo = es_x * _dot(cm, hb)
        s_last = s_x[Q - 1:Q, :]
        e_x = jnp.exp(s_last - s_x)
        es_last = jnp.exp(s_last)

        yv, zz, nw_v = y_ref[...], z_ref[...], nw_ref[...]
        sg = _sigmoid(zz)
        gz = zz * sg
        _, n, rstd = _nrm(yv * gz, nw_v)
        dout = dys_ref[...]
        dyg, dnw = _nrm_bwd(dout, n, rstd, nw_v)
        dnw_ref[...] += dnw
        dy = dyg * gz
        dz_ref[...] = (dyg * yv * (sg * (1.0 + zz * (1.0 - sg)))).astype(BF16)

        dyb = dy.astype(BF16)
        xdt_b = xdt.astype(BF16)
        dhp = dht[...]
        dhpb = dhp.astype(BF16)
        lane = lax.broadcasted_iota(jnp.int32, (Q, LANES), 1)
        sub = lax.broadcasted_iota(jnp.int32, (LANES, Q), 0)
        dxdt = jnp.zeros((Q, GROUP_X), F32)
        dg = jnp.zeros((Q, Q), F32)
        ds = jnp.zeros((Q, LANES), F32)
        ds_t = jnp.zeros((LANES, Q), F32)
        for r in range(HEADS_PER_GROUP):
            dec = _decay(v, r, causal)
            mf = v["g"] * dec
            dyr = jnp.where(lane_head == r, dyb, jnp.zeros_like(dyb))
            dm = _dot(dyr, xdt_b, _NT)
            dxdt = dxdt + _dot(mf.astype(BF16), dyr, _TN)
            dg = dg + dm * dec
            dd = dm * mf
            ds = ds + jnp.where(lane == r, jnp.sum(dd, axis=1, keepdims=True), 0.0)
            ds_t = ds_t + jnp.where(sub == r, jnp.sum(dd, axis=0, keepdims=True), 0.0)
        ds = ds - ds_t.T
        dgb = dg.astype(BF16)
        dwb = (es_x * dy).astype(BF16)
        dcm = _dot(dgb, bm) + _dot(dwb, hb, _NT)
        dh_prev = _dot(cm, dwb, _TN)
        zst = _dot(bm, dhpb)
        xe = xdt * e_x
        dxdt = dxdt + e_x * zst
        dee = xe * zst
        dbm = _dot(dgb, cm, _TN) + _dot(xe.astype(BF16), dhpb, _NT)
        v_last = jnp.sum(dee, axis=0, keepdims=True) + es_last * jnp.sum(dhp * h, axis=0, keepdims=True)
        row_x = lax.broadcasted_iota(jnp.int32, (Q, GROUP_X), 0)
        tx = dy * yo - dee + jnp.where(row_x == Q - 1, v_last, 0.0)
        ds = ds + _dot_r01(tx, reduce)
        ddta = _dot_l01(triu, ds)
        ddt = ddta * v["a_neg"] + _dot_r01(dxdt * x, reduce)
        dalog = jnp.sum(ddta * v["dt"], axis=0, keepdims=True) * v["a_neg"]
        draw = jnp.where(lane < HEADS_PER_GROUP, ddt * _sigmoid(v["dtr"]), 0.0)
        dbias = jnp.sum(draw, axis=0, keepdims=True)
        ddsk = _dot_r01(jnp.broadcast_to(jnp.sum(dy * x, axis=0, keepdims=True), (8, GROUP_X)), reduce)[0:1, :]
        dht[...] = es_last * dhp + dh_prev
        dxc_ref[:, :GROUP_X] = dxdt * v["dt_x"] + v["dsk_x"] * dy
        dxc_ref[:, GROUP_X:GROUP_X + D_STATE] = dbm
        dxc_ref[:, GROUP_X + D_STATE:] = dcm
        ddt_ref[...] = draw
        dpar_ref[0:1, :] += dbias
        dpar_ref[1:2, :] += dalog
        dpar_ref[2:3, :] += ddsk

    return pl.pallas_call(
        body,
        grid=(SSM_GROUPS, n_chunks),
        in_specs=[xc_s, gx_s, dt_s, par_s, nw_s, gx_s, hs_s, gx_s],
        out_specs=[xc_s, gx_s, dt_s, par_s, nw_s],
        out_shape=[jax.ShapeDtypeStruct((s_dim, SSM_GROUPS * GROUP_COLS), F32),
                   jax.ShapeDtypeStruct((s_dim, D_SSM), BF16),
                   jax.ShapeDtypeStruct((SSM_GROUPS, s_dim, LANES), F32),
                   jax.ShapeDtypeStruct((SSM_GROUPS, 8, LANES), F32),
                   jax.ShapeDtypeStruct((1, D_SSM), F32)],
        scratch_shapes=[pltpu.VMEM((D_STATE, GROUP_X), F32)],
        compiler_params=_cparams("parallel", "arbitrary"),
        name="ssd_bwd",
    )(xc, z, dtg, par, nw, y, hs, dymix)


ATT_SCALE = ATT_HEAD_DIM ** -0.5
NEG_INF = -jnp.inf


def _band_masks():
    qi = lax.broadcasted_iota(jnp.int32, (ATT_BLOCK, ATT_BLOCK), 0)
    kj = lax.broadcasted_iota(jnp.int32, (ATT_BLOCK, ATT_BLOCK), 1)
    return kj <= qi, kj >= qi


WIN = ATT_BLOCK * DILATIONS[-1]
N_BLOCKS = WIN // ATT_BLOCK


def _rows(start, d):
    return pl.ds(start, ATT_BLOCK) if d == 1 else pl.ds(start, ATT_BLOCK, stride=d)


def _block_start(idx, d):
    return (idx // d) * (ATT_BLOCK * d) + idx % d


def _lane_bcast(col):
    return jnp.broadcast_to(col, (col.shape[0], LANES))


def _attn_fused_fwd(qkv):
    s_dim = qkv.shape[0]
    n_win = s_dim // WIN
    blk = (WIN, ATT_HEAD_DIM)
    prev = lambda w: jnp.maximum(w - 1, 0)

    def body(q_ref, kc_ref, kp_ref, vc_ref, vp_ref, y_ref, yf_ref, lse_ref, qf, kf, vf, acc, m_run, l_run):
        w, h = pl.program_id(0), pl.program_id(1)
        qf[...] = q_ref[...].astype(F32)
        kf[0:WIN, :] = kp_ref[...].astype(F32)
        kf[WIN:, :] = kc_ref[...].astype(F32)
        vf[0:WIN, :] = vp_ref[...].astype(F32)
        vf[WIN:, :] = vc_ref[...].astype(F32)
        own, before = _band_masks()

        for d in DILATIONS:
            def block(idx, carry, d=d):
                start = _block_start(idx, d)
                rows = _rows(start, d)
                q = qf[rows, :].astype(BF16)
                kc, vc = kf[_rows(WIN + start, d), :].astype(BF16), vf[_rows(WIN + start, d), :].astype(BF16)
                kp = kf[_rows(WIN + start - ATT_BLOCK * d, d), :].astype(BF16)
                vp = vf[_rows(WIN + start - ATT_BLOCK * d, d), :].astype(BF16)
                has_prev = (idx >= d) | (w > 0)
                sc = jnp.where(own, _dot(q, kc, _NT) * ATT_SCALE, NEG_INF)
                sp = jnp.where(before & has_prev, _dot(q, kp, _NT) * ATT_SCALE, NEG_INF)
                m_blk = jnp.maximum(jnp.max(sc, axis=1, keepdims=True), jnp.max(sp, axis=1, keepdims=True))
                if d == DILATIONS[0]:
                    m_new = m_blk
                else:
                    m_old = m_run[rows, :][:, 0:1]
                    m_new = jnp.maximum(m_old, m_blk)
                pc, pp = jnp.exp(sc - m_new), jnp.exp(sp - m_new)
                l_new = jnp.sum(pc, axis=1, keepdims=True) + jnp.sum(pp, axis=1, keepdims=True)
                o_new = _dot(pc.astype(BF16), vc) + _dot(pp.astype(BF16), vp)
                if d != DILATIONS[0]:
                    alpha = jnp.exp(m_old - m_new)
                    l_new = alpha * l_run[rows, :][:, 0:1] + l_new
                    o_new = alpha * acc[rows, :] + o_new
                m_run[rows, :] = _lane_bcast(m_new)
                l_run[rows, :] = _lane_bcast(l_new)
                acc[rows, :] = o_new
                return carry

            for idx in range(N_BLOCKS):
                block(idx, 0)

        l_all = l_run[...]
        y = acc[...] / l_all
        y_ref[...] = y.astype(BF16)
        yf_ref[...] = y
        @pl.when(h == 0)
        def _():
            lse_ref[...] = jnp.zeros_like(lse_ref)

        lane = lax.broadcasted_iota(jnp.int32, (WIN, LANES), 1)
        lse_ref[...] = jnp.where(lane == h, m_run[...] + jnp.log(l_all), lse_ref[...])

    win_scratch = lambda rows: pltpu.VMEM((rows, ATT_HEAD_DIM), F32)
    return pl.pallas_call(
        body,
        grid=(n_win, ATT_HEADS),
        in_specs=[pl.BlockSpec(blk, lambda w, h: (w, h)),
                  pl.BlockSpec(blk, lambda w, h: (w, ATT_HEADS + h)),
                  pl.BlockSpec(blk, lambda w, h: (prev(w), ATT_HEADS + h)),
                  pl.BlockSpec(blk, lambda w, h: (w, 2 * ATT_HEADS + h)),
                  pl.BlockSpec(blk, lambda w, h: (prev(w), 2 * ATT_HEADS + h))],
        out_specs=[pl.BlockSpec(blk, lambda w, h: (w, h)), pl.BlockSpec(blk, lambda w, h: (w, h)),
                   pl.BlockSpec((WIN, LANES), lambda w, h: (w, 0))],
        out_shape=[jax.ShapeDtypeStruct((s_dim, D_ATT), BF16), jax.ShapeDtypeStruct((s_dim, D_ATT), F32),
                   jax.ShapeDtypeStruct((s_dim, LANES), F32)],
        scratch_shapes=[win_scratch(WIN), win_scratch(2 * WIN), win_scratch(2 * WIN), win_scratch(WIN),
                        win_scratch(WIN), win_scratch(WIN)],
        compiler_params=_cparams("parallel", "arbitrary"),
        name="attn_fused_fwd",
    )(qkv, qkv, qkv, qkv, qkv)


def _attn_fused_bwd(qkv, dymix, y_att, lse, deps=()):
    s_dim = qkv.shape[0]
    n_win = s_dim // WIN
    blk = (WIN, ATT_HEAD_DIM)
    prev = lambda w: jnp.maximum(w - 1, 0)
    nxt = lambda w: jnp.minimum(w + 1, n_win - 1)
    n_dep = len(deps)

    def body(qc_ref, qn_ref, kc_ref, kp_ref, vc_ref, vp_ref, dyc_ref, dyn_ref, yc_ref, yn_ref, lc_ref, ln_ref, *rest):
        out_ref = rest[n_dep]
        qf, qnf, kf, vf, dq_acc, dk_acc, dv_acc, ls_c, dl_c, ls_n, dl_n = rest[n_dep + 1:]
        w, h = pl.program_id(0), pl.program_id(1)
        qf[...] = qc_ref[...].astype(F32)
        qnf[...] = qn_ref[...].astype(F32)
        kf[0:WIN, :] = kp_ref[...].astype(F32)
        kf[WIN:, :] = kc_ref[...].astype(F32)
        vf[0:WIN, :] = vp_ref[...].astype(F32)
        vf[WIN:, :] = vc_ref[...].astype(F32)
        lane = lax.broadcasted_iota(jnp.int32, (WIN, LANES), 1)
        pick = lambda ref: _lane_bcast(jnp.sum(jnp.where(lane == h, ref[...], 0.0), axis=1, keepdims=True))
        ls_c[...] = pick(lc_ref)
        ls_n[...] = pick(ln_ref)
        dl_c[...] = _lane_bcast(jnp.sum(dyc_ref[...] * yc_ref[...], axis=1, keepdims=True))
        dl_n[...] = _lane_bcast(jnp.sum(dyn_ref[...] * yn_ref[...], axis=1, keepdims=True))
        for ref in (dq_acc, dk_acc, dv_acc):
            ref[...] = jnp.zeros_like(ref)
        own, before = _band_masks()

        def probs(q, k, v, dy, lse_col, dl_col, mask):
            p = jnp.exp(jnp.where(mask, _dot(q, k, _NT) * ATT_SCALE - lse_col, NEG_INF))
            ds = p * (_dot(dy, v, _NT) - dl_col)
            return p.astype(BF16), ds.astype(BF16)

        for d in DILATIONS:
            def block(idx, carry, d=d):
                start = _block_start(idx, d)
                rows = _rows(start, d)
                q, dy = qf[rows, :].astype(BF16), dyc_ref[rows, :].astype(BF16)
                lse_col, dl_col = ls_c[rows, :][:, 0:1], dl_c[rows, :][:, 0:1]
                kc, vc = kf[_rows(WIN + start, d), :].astype(BF16), vf[_rows(WIN + start, d), :].astype(BF16)
                kp = kf[_rows(WIN + start - ATT_BLOCK * d, d), :].astype(BF16)
                vp = vf[_rows(WIN + start - ATT_BLOCK * d, d), :].astype(BF16)
                pc, dsc = probs(q, kc, vc, dy, lse_col, dl_col, own)
                pp, dsp = probs(q, kp, vp, dy, lse_col, dl_col, before & ((idx >= d) | (w > 0)))
                dq_acc[rows, :] += (_dot(dsc, kc) + _dot(dsp, kp)) * ATT_SCALE
                dk_acc[rows, :] += _dot(dsc, q, _TN) * ATT_SCALE
                dv_acc[rows, :] += _dot(pc, dy, _TN)

                if idx >= d:
                    prows = _rows(start - ATT_BLOCK * d, d)
                    dk_acc[prows, :] += _dot(dsp, q, _TN) * ATT_SCALE
                    dv_acc[prows, :] += _dot(pp, dy, _TN)
                return carry

            for idx in range(N_BLOCKS):
                block(idx, 0)

            def next_window(r, carry, d=d):
                krows = _rows(WIN - ATT_BLOCK * d + r, d)
                rows = _rows(r, d)
                q, dy = qnf[rows, :].astype(BF16), dyn_ref[rows, :].astype(BF16)
                k, v = kf[_rows(2 * WIN - ATT_BLOCK * d + r, d), :].astype(BF16), vf[_rows(2 * WIN - ATT_BLOCK * d + r, d), :].astype(BF16)
                pn, dsn = probs(q, k, v, dy, ls_n[rows, :][:, 0:1], dl_n[rows, :][:, 0:1], before & (w < n_win - 1))
                dk_acc[krows, :] += _dot(dsn, q, _TN) * ATT_SCALE
                dv_acc[krows, :] += _dot(pn, dy, _TN)
                return carry

            for r in range(d):
                next_window(r, 0)

        for part, acc_ref in enumerate((dq_acc, dk_acc, dv_acc)):
            out_ref[part] = acc_ref[...].astype(BF16)

    win_scratch = lambda rows: pltpu.VMEM((rows, ATT_HEAD_DIM), F32)
    cur = lambda c: pl.BlockSpec(blk, lambda w, h: (w, c + h))
    return pl.pallas_call(
        body,
        grid=(n_win, ATT_HEADS),
        in_specs=[cur(0), pl.BlockSpec(blk, lambda w, h: (nxt(w), h)),
                  cur(ATT_HEADS), pl.BlockSpec(blk, lambda w, h: (prev(w), ATT_HEADS + h)),
                  cur(2 * ATT_HEADS), pl.BlockSpec(blk, lambda w, h: (prev(w), 2 * ATT_HEADS + h)),
                  cur(ATT_HEADS), pl.BlockSpec(blk, lambda w, h: (nxt(w), ATT_HEADS + h)),
                  cur(0), pl.BlockSpec(blk, lambda w, h: (nxt(w), h)),
                  pl.BlockSpec((WIN, LANES), lambda w, h: (w, 0)), pl.BlockSpec((WIN, LANES), lambda w, h: (nxt(w), 0))]
        + [ANY] * n_dep,
        out_specs=pl.BlockSpec((3, WIN, ATT_HEAD_DIM), lambda w, h: (0, w, h)),
        out_shape=jax.ShapeDtypeStruct((3, s_dim, D_ATT), BF16),
        scratch_shapes=[win_scratch(WIN), win_scratch(WIN), win_scratch(2 * WIN), win_scratch(2 * WIN)]
        + [win_scratch(WIN)] * 7,
        compiler_params=_cparams("parallel", "arbitrary"),
        name="attn_fused_bwd",
    )(qkv, qkv, qkv, qkv, qkv, qkv, dymix, dymix, y_att, y_att, lse, lse, *deps)


def _adamw(w, g, m, v, name):
    def fn(wb, gb, mb, vb):
        m2 = ADAM_B1 * mb + (1.0 - ADAM_B1) * gb
        v2 = ADAM_B2 * vb + (1.0 - ADAM_B2) * (gb * gb)
        m_hat = m2 / (1.0 - ADAM_B1 ** ADAM_STEP)
        v_hat = v2 / (1.0 - ADAM_B2 ** ADAM_STEP)
        delta = -ADAM_LR * (m_hat / (jnp.sqrt(v_hat) + ADAM_EPS) + ADAM_WD * wb)
        return delta, m2, v2
    cols = w.shape[1]
    tr = 128 if w.shape[0] % 128 == 0 else w.shape[0]
    return _rowcall(fn, [w, g, m, v], [], [(cols, F32)] * 3, [], name=name, tr=tr)


ANY = pl.BlockSpec(memory_space=pl.ANY)


def _position():
    x, y, c = lax.axis_index("x"), lax.axis_index("y"), lax.axis_index("c")
    chips = [(1 - x, y), (x, 1 - y), (1 - x, 1 - y)]
    return x, y, c, chips


def _remote(src, dst, send_sem, recv_sem, device):
    return pltpu.make_async_remote_copy(src_ref=src, dst_ref=dst, send_sem=send_sem, recv_sem=recv_sem,
                                        device_id=device, device_id_type=MESH)


def _handshake(peers):
    barrier = pltpu.get_barrier_semaphore()
    for p in peers:
        pl.semaphore_signal(barrier, inc=1, device_id=p, device_id_type=MESH)
    pl.semaphore_wait(barrier, len(peers))


def _gather_shards_async(shards, collective_id, name):
    n = len(shards)
    srcs = [jax.new_ref(s, memory_space=pltpu.MemorySpace.HBM) for s in shards]
    dsts = [jax.empty_ref(jax.ShapeDtypeStruct((N_CHIPS,) + s.shape, s.dtype), memory_space=pltpu.MemorySpace.HBM)
            for s in shards]

    @pl.kernel(mesh=plsc.ScalarSubcoreMesh(axis_name="seq", num_cores=1), name=name,
               scratch_types=(pltpu.SemaphoreType.DMA((6 * n,)), pltpu.SemaphoreType.DMA((6 * n,))),
               compiler_params=pltpu.CompilerParams(collective_id=collective_id))
    def launch(send_sems, recv_sems):
        x, y, c, chips = _position()
        sibling = (x, y, 1 - c)
        _handshake([(chip[0], chip[1], c) for chip in chips] + [sibling])

        def half(a, j, cc):
            h = shards[a].shape[0] // 2
            return dsts[a].at[j, pl.ds(cc * h, h), :]

        sent = []
        for a in range(n):
            h = shards[a].shape[0] // 2
            for j, chip in enumerate(chips):
                cp = _remote(srcs[a].at[pl.ds(c * h, h), :], half(a, 2 * x + y, c), send_sems.at[6 * a + j],
                             recv_sems.at[6 * a + j], (chip[0], chip[1], c))
                cp.start()
                sent.append(cp)
        for a in range(n):
            for j, chip in enumerate(chips):
                landed = half(a, 2 * chip[0] + chip[1], c)
                _remote(landed, landed, send_sems.at[6 * a + j], recv_sems.at[6 * a + j], (x, y, c)).wait_recv()
                cp = _remote(landed, landed, send_sems.at[6 * a + 3 + j], recv_sems.at[6 * a + 3 + j], sibling)
                cp.start()
                sent.append(cp)
        for a in range(n):
            for j, chip in enumerate(chips):
                handed = half(a, 2 * chip[0] + chip[1], 1 - c)
                _remote(handed, handed, send_sems.at[6 * a + 3 + j], recv_sems.at[6 * a + 3 + j], (x, y, c)).wait_recv()
        for cp in sent:
            cp.wait_send()

    launch()
    return [d[...] for d in dsts]


IN_COLS = {"z": (0, D_SSM), "xbc": (D_SSM, D_SSM + D_XBC), "dt": (D_SSM + D_XBC, D_SSM + D_XBC + SSM_HEADS),
           "qkv": (D_SSM + D_XBC + SSM_HEADS, D_IN_PROJ)}


def _cols_from_quarters(quarters, lo, hi):
    parts = []
    for q in range(N_CHIPS):
        a, b = max(lo, q * W_IN_SHARD), min(hi, (q + 1) * W_IN_SHARD)
        if a < b:
            parts.append(quarters[q][:, a - q * W_IN_SHARD:b - q * W_IN_SHARD])
    return parts[0] if len(parts) == 1 else jnp.concatenate(parts, axis=1)


def _quarters_from_cols(pieces):
    quarters = []
    for q in range(N_CHIPS):
        parts = []
        for name, (lo, hi) in IN_COLS.items():
            a, b = max(lo, q * W_IN_SHARD), min(hi, (q + 1) * W_IN_SHARD)
            if a < b:
                parts.append(pieces[name][:, a - lo:b - lo])
        quarters.append(jnp.concatenate(parts, axis=1))
    return jnp.stack(quarters)


def _by_chip(own, fetched):
    me = 2 * lax.axis_index("x") + lax.axis_index("y")
    return lax.dynamic_update_slice(fetched, own[None], (me, 0, 0))


def _add_sibling(grad, got, place, name, deps=()):
    nq, rows, cols = grad.shape
    h = rows // 2
    tr = 128
    nb = h // tr

    def body(place_ref, a_ref, b_ref, *rest):
        own_ref, ob_ref = rest[len(deps):]
        total = a_ref[...] + b_ref[...]
        ob_ref[...] = total.astype(BF16)

        @pl.when(pl.program_id(1) == place_ref[1])
        def _():
            own_ref[...] = total

    return pl.pallas_call(
        body,
        grid_spec=pltpu.PrefetchScalarGridSpec(
            num_scalar_prefetch=1, grid=(nb, nq),
            in_specs=[pl.BlockSpec((None, tr, cols), lambda i, q, p: (q, p[0] * nb + i, 0)),
                      pl.BlockSpec((None, tr, cols), lambda i, q, p: (q, i, 0))] + [ANY] * len(deps),
            out_specs=[pl.BlockSpec((tr, cols), lambda i, q, p: (i, 0)),
                       pl.BlockSpec((None, tr, cols), lambda i, q, p: (q, i, 0))]),
        out_shape=[jax.ShapeDtypeStruct((h, cols), F32), jax.ShapeDtypeStruct((nq, h, cols), BF16)],
        compiler_params=_cparams("parallel", "arbitrary"),
        name=name,
    )(place, grad, got, *deps)


def _add_chips(part, got, name, deps=()):
    h, cols = part.shape
    tr = 128

    def body(p_ref, g0_ref, g1_ref, g2_ref, *rest):
        o_ref = rest[len(deps)]
        o_ref[...] = ((p_ref[...] + g0_ref[...].astype(F32)) + g1_ref[...].astype(F32)) + g2_ref[...].astype(F32)

    got_spec = lambda j: pl.BlockSpec((None, tr, cols), lambda i: (j, i, 0))
    row_spec = pl.BlockSpec((tr, cols), lambda i: (i, 0))
    return pl.pallas_call(
        body,
        grid=(h // tr,),
        in_specs=[row_spec, got_spec(0), got_spec(1), got_spec(2)] + [ANY] * len(deps),
        out_specs=row_spec,
        out_shape=jax.ShapeDtypeStruct((h, cols), F32),
        compiler_params=_cparams("parallel"),
        name=name,
    )(part, got, got, got, *deps)


def _sequencer_exchange(src, out_shape, collective_id, name, plan, n_copies):
    src_ref = jax.new_ref(src, memory_space=pltpu.MemorySpace.HBM)
    dst_ref = jax.empty_ref(out_shape, memory_space=pltpu.MemorySpace.HBM)

    @pl.kernel(mesh=plsc.ScalarSubcoreMesh(axis_name="seq", num_cores=1), name=name,
               scratch_types=(pltpu.SemaphoreType.DMA((n_copies,)), pltpu.SemaphoreType.DMA((n_copies,))),
               compiler_params=pltpu.CompilerParams(collective_id=collective_id))
    def launch(send_sems, recv_sems):
        x, y, c, chips = _position()
        copies = plan(src_ref, dst_ref, x, y, c, chips)
        _handshake([peer for _, _, peer in copies])
        started = []
        for k, (s, d, peer) in enumerate(copies):
            cp = _remote(s, d, send_sems.at[k], recv_sems.at[k], peer)
            cp.start()
            started.append(cp)
        for cp in started:
            cp.wait()

    launch()
    return dst_ref[...]


class _AsyncReduceScatter:
    def __init__(self, grad, nm, first_id):
        self.grad, self.nm, self.first_id = grad, nm, first_id
        nq, rows, cols = grad.shape
        h = self.h = rows // 2

        def to_sibling(s, d, x, y, c, chips):
            return [(s.at[:, pl.ds((1 - c) * h, h), :], d, (x, y, 1 - c))]

        self.from_sibling = _sequencer_exchange(grad, jax.ShapeDtypeStruct((nq, h, cols), F32), first_id,
                                                f"rs_sibling_{nm}", to_sibling, 1)

    def sibling_sum(self, not_before=()):
        cols = self.grad.shape[2]
        place = jnp.stack([lax.axis_index("c"), 2 * lax.axis_index("x") + lax.axis_index("y")]).astype(jnp.int32)
        self.part, self.part_b = _add_sibling(self.grad, self.from_sibling, place, f"add_sibling_{self.nm}", not_before)

        def to_chips(s, d, x, y, c, chips):
            return [(s.at[2 * chip[0] + chip[1]], d.at[j], (chip[0], chip[1], c)) for j, chip in enumerate(chips)]

        self.from_chips = _sequencer_exchange(self.part_b, jax.ShapeDtypeStruct((3, self.h, cols), BF16),
                                              self.first_id + 1, f"rs_quarters_{self.nm}", to_chips, 3)
        return self.part_b

    def chip_sum(self, not_before=()):
        cols = self.grad.shape[2]
        self.half = _add_chips(self.part, self.from_chips, f"add_chips_{self.nm}", not_before)

        def whole_to_sibling(s, d, x, y, c, chips):
            return [(s, d, (x, y, 1 - c))]

        self.other = _sequencer_exchange(self.half, jax.ShapeDtypeStruct((self.h, cols), F32), self.first_id + 2,
                                         f"rs_share_{self.nm}", whole_to_sibling, 1)
        return self.half

    def share(self):
        return self.half, self.other


def _after(x, deps, name):
    def body(x_ref, *rest):
        rest[-1][...] = x_ref[...]

    vm = pl.BlockSpec(memory_space=pltpu.VMEM)
    return pl.pallas_call(body, in_specs=[vm] + [ANY] * len(deps), out_specs=vm,
                          out_shape=jax.ShapeDtypeStruct(x.shape, x.dtype), name=name)(x, *deps)


def _adamw_halves(w, mine, other, m, v, name):
    rows, cols = w.shape
    tr = 128
    nb = rows // 2 // tr
    c_arr = lax.axis_index("c").astype(jnp.int32).reshape(1)

    def body(c_ref, w_ref, a_ref, b_ref, m_ref, v_ref, g_out, d_out, m_out, v_out):
        is_mine = (pl.program_id(0) // nb) == c_ref[0]
        g = jnp.where(is_mine, a_ref[...], b_ref[...])
        wb, mb, vb = w_ref[...], m_ref[...], v_ref[...]
        m2 = ADAM_B1 * mb + (1.0 - ADAM_B1) * g
        v2 = ADAM_B2 * vb + (1.0 - ADAM_B2) * (g * g)
        m_hat = m2 / (1.0 - ADAM_B1 ** ADAM_STEP)
        v_hat = v2 / (1.0 - ADAM_B2 ** ADAM_STEP)
        g_out[...] = g
        d_out[...] = -ADAM_LR * (m_hat / (jnp.sqrt(v_hat) + ADAM_EPS) + ADAM_WD * wb)
        m_out[...] = m2
        v_out[...] = v2

    full = pl.BlockSpec((tr, cols), lambda i, c: (i, 0))
    half = pl.BlockSpec((tr, cols), lambda i, c: (i % nb, 0))
    return pl.pallas_call(
        body,
        grid_spec=pltpu.PrefetchScalarGridSpec(
            num_scalar_prefetch=1, grid=(rows // tr,),
            in_specs=[full, half, half, full, full], out_specs=[full] * 4),
        out_shape=[jax.ShapeDtypeStruct((rows, cols), F32)] * 4,
        compiler_params=_cparams("parallel"),
        name=name,
    )(c_arr, w, mine, other, m, v)


def _all_sum_small(v):
    n_dev = 8

    def body(v_ref, o_ref, gath, send_sems, recv_sems):
        x, y, c, _ = _position()
        me = 4 * x + 2 * y + c
        gath[me] = v_ref[...]
        copies = []
        for k in range(1, n_dev):
            peer = tuple(1 - p if (k >> s) & 1 else p for p, s in ((x, 2), (y, 1), (c, 0)))
            cp = _remote(v_ref, gath.at[me], send_sems.at[k - 1], recv_sems.at[k - 1], peer)
            cp.start()
            copies.append(cp)
        for cp in copies:
            cp.wait()
        acc = gath[0]
        for i in range(1, n_dev):
            acc = acc + gath[i]
        o_ref[...] = acc

    vm = pl.BlockSpec(memory_space=pltpu.VMEM)
    return pl.pallas_call(
        body,
        in_specs=[vm],
        out_specs=vm,
        out_shape=jax.ShapeDtypeStruct(v.shape, F32),
        scratch_shapes=[pltpu.VMEM((n_dev,) + v.shape, F32), pltpu.SemaphoreType.DMA((n_dev - 1,)),
                        pltpu.SemaphoreType.DMA((n_dev - 1,))],
        name="all_sum_small",
    )(v)


def _pack_rows(vectors):
    rows = []
    for v in vectors:
        flat = v.reshape(-1).astype(F32)
        rows.append(jnp.pad(flat, (0, (-flat.shape[0]) % LANES)).reshape(-1, LANES))
    out = jnp.concatenate(rows, axis=0)
    return jnp.pad(out, ((0, (-out.shape[0]) % 8), (0, 0)))


def _unpack_rows(packed, shapes):
    outs, r = [], 0
    for shp in shapes:
        size = math.prod(shp)
        nr = -(-size // LANES)
        outs.append(packed[r:r + nr].reshape(-1)[:size].reshape(shp))
        r += nr
    return outs


def _relu_sq(acc):
    r = jnp.maximum(acc, 0.0)
    return r, r * r


def _relu_sq_bwd(acc, r):
    return (acc * (2.0 * r.astype(F32)),)


def kernel(x, norm_mix_pre, w_in, conv_w, conv_b, dt_bias, a_log, d_skip, ssm_norm_w, w_out, norm_mix_post, norm_mlp_pre, w_up, w_down, norm_mlp_post, loss_target, m_norm_mix_pre, m_w_in, m_conv_w, m_conv_b, m_dt_bias, m_a_log, m_d_skip, m_ssm_norm_w, m_w_out, m_norm_mix_post, m_norm_mlp_pre, m_w_up, m_w_down, m_norm_mlp_post, v_norm_mix_pre, v_w_in, v_conv_w, v_conv_b, v_dt_bias, v_a_log, v_d_skip, v_ssm_norm_w, v_w_out, v_norm_mix_post, v_norm_mlp_pre, v_w_up, v_w_down, v_norm_mlp_post):
    s_dim = x.shape[1]
    xs, target = x[0], loss_target[0]
    chip = 2 * lax.axis_index("x") + lax.axis_index("y")

    own = [w_in[0].astype(BF16), w_out[0].astype(BF16), w_up[0].astype(BF16), w_down[0].astype(BF16)]
    fetched_in = _gather_shards_async(own[:1], 14, "gather_w_in")[0]
    conv_cols = D_XBC // N_CHIPS
    conv_placed = lax.dynamic_update_slice(jnp.zeros((8, D_XBC), F32), 0.5 * conv_w[0], (0, chip * conv_cols))
    conv_full = _all_sum_small(conv_placed.reshape(-1, LANES)).reshape(8, D_XBC)
    w8 = _perm_cols(conv_full.at[CONV_WIDTH].set(conv_b[0]))
    u = _pre_norm(xs, norm_mix_pre)
    fetched_in, u, w8, *rest = lax.optimization_barrier((fetched_in, u, w8, *own[1:]))
    fetched = [fetched_in] + _gather_shards_async(rest, 1, "gather_rest")
    g_in, g_out, g_up, g_down = [_by_chip(o, f) for o, f in zip(own, fetched)]
    w_z = _cols_from_quarters(g_in, *IN_COLS["z"])
    w_xbc = _perm_cols(_cols_from_quarters(g_in, *IN_COLS["xbc"]))
    w_dt = jnp.pad(_cols_from_quarters(g_in, *IN_COLS["dt"]), ((0, 0), (0, LANES - SSM_HEADS)))
    w_qkv = _cols_from_quarters(g_in, *IN_COLS["qkv"])
    w_out_full = g_out.reshape(D_MIX, D_MODEL)
    w_down_full = g_down.reshape(D_FF, D_MODEL)

    z = _matmul([(u, w_z, TK)], "nn", [F32], name="proj_z")
    xbc = _matmul([(u, w_xbc, TK)], "nn", [F32], name="proj_xbc")
    dt_raw = _matmul([(u, w_dt, TK)], "nn", [F32], name="proj_dt")
    qkv = _matmul([(u, w_qkv, TK)], "nn", [BF16], name="proj_qkv")
    xc = _conv_fwd(xbc, w8)
    dtg = _dt_to_groups(dt_raw)
    par = _pack_ssd_params(dt_bias[0], a_log[0], d_skip[0])
    y, y_ssm, states = _ssd_fwd(xc, z, dtg, par, ssm_norm_w)
    y_att, y_att_f32, lse = _attn_fused_fwd(qkv)
    y_mix = jnp.concatenate([y_ssm, y_att], axis=1)
    mix = _matmul([(y_mix, w_out_full, TK)], "nn", [F32], name="out_proj")
    h1, u2 = _post_pre_norm(xs, mix, norm_mix_post, norm_mlp_pre)
    hid, act = _matmul([(u2, g_up, TK)], "nn", [BF16, BF16], name="mlp_up", epilogue=_relu_sq)
    ff = _matmul([(act, w_down_full, TK)], "nn", [F32], name="mlp_down")
    dh2, dff, d_g4, loss_part = _tail(ff, h1, target, norm_mlp_post)

    dhid = _matmul([(dff, w_down_full, TK)], "nt", [BF16], name="mlp_down_dx", epilogue=_relu_sq_bwd, extras=[hid])
    weights = {"norm_mix_pre": (norm_mix_pre, m_norm_mix_pre, v_norm_mix_pre), "w_in": (w_in, m_w_in, v_w_in),
               "conv_w": (conv_w, m_conv_w, v_conv_w), "conv_b": (conv_b, m_conv_b, v_conv_b),
               "dt_bias": (dt_bias, m_dt_bias, v_dt_bias), "a_log": (a_log, m_a_log, v_a_log),
               "d_skip": (d_skip, m_d_skip, v_d_skip), "ssm_norm_w": (ssm_norm_w, m_ssm_norm_w, v_ssm_norm_w),
               "w_out": (w_out, m_w_out, v_w_out), "norm_mix_post": (norm_mix_post, m_norm_mix_post, v_norm_mix_post),
               "norm_mlp_pre": (norm_mlp_pre, m_norm_mlp_pre, v_norm_mlp_pre), "w_up": (w_up, m_w_up, v_w_up),
               "w_down": (w_down, m_w_down, v_w_down),
               "norm_mlp_post": (norm_mlp_post, m_norm_mlp_post, v_norm_mlp_post)}
    grads, delta, new_m, new_v = {}, {}, {}, {}

    def adamw_big(n, halves):
        w, m, v = weights[n]
        g_, d_, m_, v_ = _adamw_halves(w[0], halves[0], halves[1], m[0], v[0], f"adamw_{n}")
        grads[n], delta[n], new_m[n], new_v[n] = g_[None], d_[None], m_[None], v_[None]

    dw_down = _matmul([(act, dff, TK)], "tn", [F32], name="mlp_down_dw")
    rs_down = _AsyncReduceScatter(dw_down.reshape(N_CHIPS, D_FF // N_CHIPS, D_MODEL), "w_down", 11)
    dw_up = _matmul([(u2, dhid, TK)], "tn", [F32], name="mlp_up_dw", deps=[dw_down], out_quarters=True)
    rs_up = _AsyncReduceScatter(dw_up, "w_up", 8)
    du2 = _matmul([(dhid, g_up, TK)], "nt", [F32], name="mlp_up_dx",
                  deps=[rs_down.sibling_sum(not_before=[dw_up])])
    dh1, dmix, d_g3, d_g2 = _mid_bwd(du2, h1, dh2, mix, norm_mix_post, norm_mlp_pre,
                                     deps=[rs_up.sibling_sum(not_before=[du2])])
    dymix = _matmul([(dmix, w_out_full, TK)], "nt", [F32], name="out_proj_dx")
    dw_out = _matmul([(y_mix, dmix, TK)], "tn", [F32], name="out_proj_dw")
    rs_out = _AsyncReduceScatter(dw_out.reshape(N_CHIPS, D_MIX // N_CHIPS, D_MODEL), "w_out", 5)
    dqkv = _attn_fused_bwd(qkv, dymix, y_att_f32, lse)
    par_late = _after(par, [rs_down.chip_sum(not_before=[dqkv]), rs_out.sibling_sum(not_before=[dymix])],
                      "after_w_down")
    dxc, dz, ddtg, dpar, d_nw = _ssd_bwd(xc, z, dtg, par_late, ssm_norm_w, y, states, dymix)
    g_down = rs_down.share()
    dxbc, dw8 = _conv_bwd(xbc, _after(w8, [*g_down, rs_up.chip_sum(not_before=[dxc])], "after_w_up"), dxc)
    ddt = jnp.pad(_dt_from_groups(ddtg), ((0, 0), (0, LANES - SSM_HEADS))).astype(BF16)
    g_up = rs_up.share()
    dw_z = _matmul([(u, dz, TK)], "tn", [F32], name="proj_z_dw")
    dw_xbc = _matmul([(u, dxbc, TK)], "tn", [F32], name="proj_xbc_dw",
                     deps=[*g_up, rs_out.chip_sum(not_before=[dxbc])])
    g_out = rs_out.share()
    dw_dt = _matmul([(u, ddt, TK)], "tn", [F32], name="proj_dt_dw")
    dw_qkv = _matmul([(u, dqkv, TK)], "tn", [F32], name="proj_qkv_dw")
    dw_in = _quarters_from_cols({"z": dw_z, "xbc": _unperm_cols(dw_xbc), "dt": dw_dt[:, :SSM_HEADS], "qkv": dw_qkv})
    rs_in = _AsyncReduceScatter(dw_in, "w_in", 2)
    adamw_big("w_down", g_down)
    adamw_big("w_up", g_up)
    rs_in.sibling_sum(not_before=[delta["w_up"]])
    du = _matmul([(dz, w_z, TK_MULTI), (dxbc, w_xbc, TK_MULTI), (dqkv, w_qkv, TK_MULTI), (ddt, w_dt, LANES)], "nt",
                 [F32], name="proj_dx", deps=[*g_out, rs_in.part_b])
    grad_x, d_g1 = _first_bwd(du, xs, dh1, norm_mix_pre)
    adamw_big("w_out", g_out)
    rs_in.chip_sum(not_before=[grad_x, delta["w_out"]])

    dconv = _unperm_cols(dw8)
    d_bias, d_alog, d_dskip = _unpack_ssd_params(dpar)
    small_shapes = [(1, D_MODEL), (CONV_WIDTH, D_XBC), (1, D_XBC), (1, SSM_HEADS), (1, SSM_HEADS), (1, SSM_HEADS),
                    (1, D_SSM), (1, D_MODEL), (1, D_MODEL), (1, D_MODEL), (1, LANES)]
    summed = _unpack_rows(
        _all_sum_small(_pack_rows([d_g1, dconv[:CONV_WIDTH], dconv[CONV_WIDTH:CONV_WIDTH + 1], d_bias, d_alog,
                                   d_dskip, d_nw, d_g2, d_g3, d_g4, loss_part])), small_shapes)
    (g_g1, g_conv_full, g_conv_b, g_bias, g_alog, g_dskip, g_nw, g_g2, g_g3, g_g4, loss_row) = summed
    loss = loss_row[0, 0]
    g_conv_w = lax.dynamic_slice(g_conv_full, (0, chip * conv_cols), (CONV_WIDTH, conv_cols))[None]

    grads.update({"norm_mix_pre": g_g1, "conv_w": g_conv_w, "conv_b": g_conv_b, "dt_bias": g_bias,
                  "a_log": g_alog, "d_skip": g_dskip, "ssm_norm_w": g_nw, "norm_mix_post": g_g2,
                  "norm_mlp_pre": g_g3, "norm_mlp_post": g_g4})
    order = list(weights)
    small_names = [n for n in order if n not in ("w_in", "w_out", "w_up", "w_down")]
    small_w_shapes = [weights[n][0].shape for n in small_names]
    packed = [_pack_rows([weights[n][k] for n in small_names]) for k in range(3)]
    packed_g = _pack_rows([grads[n].reshape(weights[n][0].shape) for n in small_names])
    sd, sm, sv = _adamw(packed[0], packed_g, packed[1], packed[2], "adamw_small")
    for k, n in enumerate(small_names):
        grads[n] = grads[n].reshape(weights[n][0].shape)
    for res, pk in ((delta, sd), (new_m, sm), (new_v, sv)):
        for n, val in zip(small_names, _unpack_rows(pk, small_w_shapes)):
            res[n] = val
    adamw_big("w_in", rs_in.share())

    return (loss, grad_x[None], *[grads[n] for n in order], *[delta[n] for n in order],
            *[new_m[n] for n in order], *[new_v[n] for n in order])
```

```python
import math

import numpy as np
import jax
import jax.numpy as jnp
from jax import lax
from jax.experimental import pallas as pl
from jax.experimental.pallas import tpu as pltpu
from jax.experimental.pallas import tpu_sc as plsc

F32 = jnp.float32
BF16 = jnp.bfloat16

D_MODEL = 2048
SSM_HEAD_DIM = 64
SSM_GROUPS = 8
HEADS_PER_GROUP = 4
SSM_HEADS = SSM_GROUPS * HEADS_PER_GROUP
D_SSM = SSM_HEADS * SSM_HEAD_DIM
D_STATE = 128
CONV_WIDTH = 4
SSD_CHUNK = 128
D_XBC = D_SSM + 2 * SSM_GROUPS * D_STATE
GROUP_X = HEADS_PER_GROUP * SSM_HEAD_DIM
GROUP_COLS = GROUP_X + 2 * D_STATE
ATT_HEAD_DIM = 128
ATT_HEADS = 16
D_ATT = ATT_HEADS * ATT_HEAD_DIM
DILATIONS = (1, 4, 16)
ATT_BLOCK = 128
D_MIX = D_SSM + D_ATT
D_IN_PROJ = D_SSM + D_XBC + SSM_HEADS + 3 * D_ATT
D_FF = 4 * D_MODEL
EPS = 1e-6
N_CHIPS = 4
W_IN_SHARD = D_IN_PROJ // N_CHIPS

ADAM_LR = 0.001
ADAM_B1 = 0.9
ADAM_B2 = 0.999
ADAM_EPS = 1e-08
ADAM_WD = 0.01
ADAM_STEP = 10

LANES = 128
VMEM_LIMIT = 48 * 1024 * 1024
MESH = pl.DeviceIdType.MESH

_NN = (((1,), (0,)), ((), ()))
_NT = (((1,), (1,)), ((), ()))
_TN = (((0,), (0,)), ((), ()))


def _dot(a, b, dims=_NN):
    return lax.dot_general(a, b, dims, preferred_element_type=F32)


def _cparams(*sem):
    return pltpu.CompilerParams(dimension_semantics=sem, vmem_limit_bytes=VMEM_LIMIT)


TK = 2048
TK_MULTI = 1024


def _matmul(pairs, mode, out_dtypes, *, name, tm=1024, tn=1024, epilogue=None, extras=(), deps=(), out_quarters=False):
    a0, b0, _ = pairs[0]
    m_dim = a0.shape[-1] if mode == "tn" else a0.shape[-2]
    if b0.ndim == 3:
        n_dim = b0.shape[1] if mode == "nt" else b0.shape[0] * b0.shape[2]
    else:
        n_dim = b0.shape[0] if mode == "nt" else b0.shape[1]
    tm, tn = min(tm, m_dim), min(tn, n_dim)
    nks, offs = [], []
    for a, _, tk in pairs:
        k_part = a.shape[0] if mode == "tn" else a.shape[-1]
        k_dim = k_part * (a.shape[0] if a.ndim == 3 else 1)
        assert k_part % tk == 0, (name, k_part, tk)
        offs.append(sum(nks))
        nks.append(k_dim // tk)
    nk_total = sum(nks)
    assert m_dim % tm == 0 and n_dim % tn == 0, (name, m_dim, n_dim)
    dims = {"nn": _NN, "nt": _NT, "tn": _TN}[mode]
    n_pairs, n_extra, n_out = len(pairs), len(extras), len(out_dtypes)

    in_specs, operands = [], []
    for (a, b, tk), off, nk in zip(pairs, offs, nks):
        def kidx(k, off=off, nk=nk):
            return k if n_pairs == 1 else jnp.clip(k - off, 0, nk - 1)
        if mode == "tn":
            assert a.ndim == 2
            in_specs.append(pl.BlockSpec((tk, tm), lambda m, n, k, f=kidx: (f(k), m)))
        elif a.ndim == 3:
            per = a.shape[2] // tk
            in_specs.append(pl.BlockSpec((None, tm, tk), lambda m, n, k, f=kidx, per=per: (f(k) // per, m, f(k) % per)))
        else:
            in_specs.append(pl.BlockSpec((tm, tk), lambda m, n, k, f=kidx: (m, f(k))))
        if b.ndim == 3 and mode == "nt":
            per = b.shape[2] // tk
            in_specs.append(pl.BlockSpec((None, tn, tk), lambda m, n, k, f=kidx, per=per: (f(k) // per, n, f(k) % per)))
        elif b.ndim == 3:
            per = b.shape[2] // tn
            in_specs.append(pl.BlockSpec((None, tk, tn), lambda m, n, k, f=kidx, per=per: (n // per, f(k), n % per)))
        elif mode == "nt":
            in_specs.append(pl.BlockSpec((tn, tk), lambda m, n, k, f=kidx: (n, f(k))))
        else:
            in_specs.append(pl.BlockSpec((tk, tn), lambda m, n, k, f=kidx: (f(k), n)))
        operands += [a, b]
    for e in extras:
        in_specs.append(pl.BlockSpec((tm, tn), lambda m, n, k: (m, n)))
        operands.append(e)
    in_specs += [pl.BlockSpec(memory_space=pl.ANY)] * len(deps)
    operands += list(deps)
    first_out = 2 * n_pairs + n_extra + len(deps)
    if out_quarters:
        out_per_q = n_dim // N_CHIPS // tn
        out_dims = (N_CHIPS, m_dim, n_dim // N_CHIPS)
        out_spec = pl.BlockSpec((None, tm, tn), lambda m, n, k: (n // out_per_q, m, n % out_per_q))
    else:
        out_dims = (m_dim, n_dim)
        out_spec = pl.BlockSpec((tm, tn), lambda m, n, k: (m, n))

    def body(*refs):
        ab = refs[:2 * n_pairs]
        e_refs = refs[2 * n_pairs:2 * n_pairs + n_extra]
        o_refs = refs[first_out:first_out + n_out]

        def finish(total):
            vals = (total,) if epilogue is None else epilogue(total, *[e[...] for e in e_refs])
            for o_ref, v in zip(o_refs, vals):
                o_ref[...] = v.astype(o_ref.dtype)

        if nk_total == 1:
            finish(_dot(ab[0][...], ab[1][...], dims))
            return
        acc = refs[-1]
        k = pl.program_id(2)

        @pl.when(k == 0)
        def _():
            acc[...] = jnp.zeros_like(acc)

        for i in range(n_pairs):
            def accumulate(i=i):
                acc[...] += _dot(ab[2 * i][...], ab[2 * i + 1][...], dims)
            if n_pairs == 1:
                accumulate()
            else:
                pl.when((k >= offs[i]) & (k < offs[i] + nks[i]))(accumulate)

        @pl.when(k == nk_total - 1)
        def _():
            finish(acc[...])

    outs = pl.pallas_call(
        body,
        grid=(m_dim // tm, n_dim // tn, nk_total),
        in_specs=in_specs,
        out_specs=[out_spec for _ in out_dtypes],
        out_shape=[jax.ShapeDtypeStruct(out_dims, dt) for dt in out_dtypes],
        scratch_shapes=[pltpu.VMEM((tm, tn), F32)] if nk_total > 1 else [],
        compiler_params=_cparams("parallel", "parallel", "arbitrary"),
        name=name,
    )(*operands)
    return outs[0] if n_out == 1 else outs


def _rowcall(fn, rows, vecs, row_outs, acc_widths, *, name, tr=256, row_cols=None, deps=()):
    s_dim = rows[0].shape[0]
    assert s_dim % tr == 0
    row_cols = row_cols or [None] * len(rows)
    n_r, n_v, n_ro, n_acc = len(rows), len(vecs), len(row_outs), len(acc_widths)
    in_specs = []
    for r, rc in zip(rows, row_cols):
        if rc is None:
            in_specs.append(pl.BlockSpec((tr, r.shape[1]), lambda i: (i, 0)))
        else:
            in_specs.append(pl.BlockSpec((tr, rc[0]), lambda i, c=rc[1]: (i, c)))
    for v in vecs:
        in_specs.append(pl.BlockSpec(v.shape, lambda i, nd=v.ndim: (0,) * nd))
    in_specs += [pl.BlockSpec(memory_space=pl.ANY)] * len(deps)
    n_d = len(deps)

    def body(*refs):
        ins = [r[...] for r in refs[:n_r + n_v]]
        ro = refs[n_r + n_v + n_d:n_r + n_v + n_d + n_ro]
        ao = refs[n_r + n_v + n_d + n_ro:]
        outs = fn(*ins)
        for ref, v in zip(ro, outs[:n_ro]):
            ref[...] = v.astype(ref.dtype)
        if n_acc:
            @pl.when(pl.program_id(0) == 0)
            def _():
                for ref in ao:
                    ref[...] = jnp.zeros_like(ref)
            for ref, v in zip(ao, outs[n_ro:]):
                ref[...] += v

    outs = pl.pallas_call(
        body,
        grid=(s_dim // tr,),
        in_specs=in_specs,
        out_specs=[pl.BlockSpec((tr, w), lambda i: (i, 0)) for w, _ in row_outs]
        + [pl.BlockSpec((1, w), lambda i: (0, 0)) for w in acc_widths],
        out_shape=[jax.ShapeDtypeStruct((s_dim, w), dt) for w, dt in row_outs]
        + [jax.ShapeDtypeStruct((1, w), F32) for w in acc_widths],
        compiler_params=_cparams("arbitrary"),
        name=name,
    )(*rows, *vecs, *deps)
    return outs


def _nrm(x, g):
    r = lax.rsqrt(jnp.mean(x * x, axis=-1, keepdims=True) + EPS)
    n = x * r
    return n * g, n, r


def _nrm_bwd(dy, n, r, g):
    dn = dy * g
    dx = r * (dn - n * jnp.mean(dn * n, axis=-1, keepdims=True))
    return dx, jnp.sum(dy * n, axis=0, keepdims=True)


def _sigmoid(x):
    return 1.0 / (1.0 + jnp.exp(-x))


def _softplus(x):
    return jnp.maximum(x, 0.0) + jnp.log(1.0 + jnp.exp(-jnp.abs(x)))


def _pre_norm(x, g1):
    def fn(xb, g):
        return (_nrm(xb, g)[0],)
    return _rowcall(fn, [x], [g1], [(D_MODEL, BF16)], [], name="pre_norm")[0]


def _post_pre_norm(x, mix, g2, g3):
    def fn(xb, mb, g2b, g3b):
        h1 = xb + _nrm(mb, g2b)[0]
        return h1, _nrm(h1, g3b)[0]
    return _rowcall(fn, [x, mix], [g2, g3], [(D_MODEL, F32), (D_MODEL, BF16)], [], name="post_pre_norm")


def _tail(ff, h1, target, g4):
    def fn(ffb, h1b, tb, g):
        y, n, r = _nrm(ffb, g)
        e = h1b + y - tb
        loss = 0.5 * jnp.sum(jnp.sum(e * e, axis=-1, keepdims=True) * (1.0 / D_MODEL), axis=0, keepdims=True)
        dh2 = e * (1.0 / D_MODEL)
        dff, dg = _nrm_bwd(dh2, n, r, g)
        return dh2, dff, dg, jnp.broadcast_to(loss, (1, LANES))
    return _rowcall(fn, [ff, h1, target], [g4], [(D_MODEL, F32), (D_MODEL, BF16)], [D_MODEL, LANES], name="tail")


def _mid_bwd(du2, h1, dh2, mix, g2, g3, deps=()):
    def fn(du2b, h1b, dh2b, mb, g2b, g3b):
        _, n3, r3 = _nrm(h1b, g3b)
        d3, dg3 = _nrm_bwd(du2b, n3, r3, g3b)
        dh1 = dh2b + d3
        _, n2, r2 = _nrm(mb, g2b)
        dmix, dg2 = _nrm_bwd(dh1, n2, r2, g2b)
        return dh1, dmix, dg3, dg2
    return _rowcall(fn, [du2, h1, dh2, mix], [g2, g3], [(D_MODEL, F32), (D_MODEL, BF16)], [D_MODEL, D_MODEL],
                    name="mid_bwd", deps=deps)


def _first_bwd(du, x, dh1, g1):
    def fn(dub, xb, dh1b, g):
        _, n, r = _nrm(xb, g)
        dx, dg = _nrm_bwd(dub, n, r, g)
        return dh1b + dx, dg
    return _rowcall(fn, [du, x, dh1], [g1], [(D_MODEL, F32)], [D_MODEL], name="first_bwd")


CONV_TILE = 256
CONV_ROWS = 256
PAD = 8


def _conv_taps(w):
    return [w[k:k + 1, :] for k in range(CONV_WIDTH)], w[CONV_WIDTH:CONV_WIDTH + 1, :]


def _conv_fwd(xbc, w8):
    s_dim, c_dim = xbc.shape
    n_steps = s_dim // CONV_ROWS

    def body(x_ref, w_ref, o_ref, xp):
        xp[0:PAD, :] = jnp.zeros((PAD, CONV_TILE), F32)
        xp[PAD:PAD + s_dim, :] = x_ref[...]
        taps, bias = _conv_taps(w_ref[...])

        def step(c, carry):
            base = pl.multiple_of(c * CONV_ROWS, CONV_ROWS)
            win = xp[pl.ds(base, CONV_ROWS + PAD), :]
            pre = bias + taps[3] * win[PAD:, :]
            for j in range(1, CONV_WIDTH):
                pre = pre + taps[3 - j] * pltpu.roll(win, j, axis=0)[PAD:, :]
            o_ref[pl.ds(base, CONV_ROWS), :] = pre * _sigmoid(pre)
            return carry

        lax.fori_loop(0, n_steps, step, 0)

    return pl.pallas_call(
        body,
        grid=(c_dim // CONV_TILE,),
        in_specs=[pl.BlockSpec((s_dim, CONV_TILE), lambda j: (0, j)), pl.BlockSpec((8, CONV_TILE), lambda j: (0, j))],
        out_specs=pl.BlockSpec((s_dim, CONV_TILE), lambda j: (0, j)),
        out_shape=jax.ShapeDtypeStruct((s_dim, c_dim), F32),
        scratch_shapes=[pltpu.VMEM((s_dim + 2 * PAD, CONV_TILE), F32)],
        compiler_params=_cparams("parallel"),
        name="conv_fwd",
    )(xbc, w8)


def _conv_bwd(xbc, w8, dxc):
    s_dim, c_dim = xbc.shape
    n_steps = s_dim // CONV_ROWS

    def body(x_ref, w_ref, d_ref, dx_ref, dw_ref, xp, dp):
        xp[0:PAD, :] = jnp.zeros((PAD, CONV_TILE), F32)
        xp[PAD:PAD + s_dim, :] = x_ref[...]
        dp[PAD + s_dim:, :] = jnp.zeros((PAD, CONV_TILE), F32)
        taps, bias = _conv_taps(w_ref[...])

        def step1(c, sums):
            base = pl.multiple_of(c * CONV_ROWS, CONV_ROWS)
            win = xp[pl.ds(base, CONV_ROWS + PAD), :]
            shifted = [win[PAD:, :]] + [pltpu.roll(win, j, axis=0)[PAD:, :] for j in range(1, CONV_WIDTH)]
            pre = bias
            for j in range(CONV_WIDTH):
                pre = pre + taps[3 - j] * shifted[j]
            sg = _sigmoid(pre)
            dpre = d_ref[pl.ds(base, CONV_ROWS), :] * (sg * (1.0 + pre * (1.0 - sg)))
            dp[pl.ds(base + PAD, CONV_ROWS), :] = dpre
            new = [sums[k] + jnp.sum(dpre * shifted[3 - k], axis=0, keepdims=True) for k in range(CONV_WIDTH)]
            new.append(sums[CONV_WIDTH] + jnp.sum(dpre, axis=0, keepdims=True))
            return tuple(new)

        zero = jnp.zeros((1, CONV_TILE), F32)
        sums = lax.fori_loop(0, n_steps, step1, (zero,) * (CONV_WIDTH + 1))
        dw_ref[...] = jnp.zeros((8, CONV_TILE), F32)
        for k in range(CONV_WIDTH + 1):
            dw_ref[k:k + 1, :] = sums[k]

        def step2(c, carry):
            base = pl.multiple_of(c * CONV_ROWS, CONV_ROWS)
            win = dp[pl.ds(base + PAD, CONV_ROWS + PAD), :]
            dx = taps[3] * win[:CONV_ROWS, :]
            for j in range(1, CONV_WIDTH):
                dx = dx + taps[3 - j] * pltpu.roll(win, CONV_ROWS + PAD - j, axis=0)[:CONV_ROWS, :]
            dx_ref[pl.ds(base, CONV_ROWS), :] = dx.astype(BF16)
            return carry

        lax.fori_loop(0, n_steps, step2, 0)

    col = lambda j: (0, j)
    return pl.pallas_call(
        body,
        grid=(c_dim // CONV_TILE,),
        in_specs=[pl.BlockSpec((s_dim, CONV_TILE), col), pl.BlockSpec((8, CONV_TILE), col),
                  pl.BlockSpec((s_dim, CONV_TILE), col)],
        out_specs=[pl.BlockSpec((s_dim, CONV_TILE), col), pl.BlockSpec((8, CONV_TILE), col)],
        out_shape=[jax.ShapeDtypeStruct((s_dim, c_dim), BF16), jax.ShapeDtypeStruct((8, c_dim), F32)],
        scratch_shapes=[pltpu.VMEM((s_dim + 2 * PAD, CONV_TILE), F32), pltpu.VMEM((s_dim + 2 * PAD, CONV_TILE), F32)],
        compiler_params=_cparams("parallel"),
        name="conv_bwd",
    )(xbc, w8, dxc)


def _perm_cols(a):
    parts = []
    for g in range(SSM_GROUPS):
        parts += [a[..., g * GROUP_X:(g + 1) * GROUP_X],
                  a[..., D_SSM + g * D_STATE:D_SSM + (g + 1) * D_STATE],
                  a[..., D_SSM + SSM_GROUPS * D_STATE + g * D_STATE:D_SSM + SSM_GROUPS * D_STATE + (g + 1) * D_STATE]]
    return jnp.concatenate(parts, axis=-1)


def _unperm_cols(a):
    xs = [a[..., g * GROUP_COLS:g * GROUP_COLS + GROUP_X] for g in range(SSM_GROUPS)]
    bs = [a[..., g * GROUP_COLS + GROUP_X:g * GROUP_COLS + GROUP_X + D_STATE] for g in range(SSM_GROUPS)]
    cs = [a[..., g * GROUP_COLS + GROUP_X + D_STATE:(g + 1) * GROUP_COLS] for g in range(SSM_GROUPS)]
    return jnp.concatenate(xs + bs + cs, axis=-1)


def _dt_to_groups(dt):
    s_dim = dt.shape[0]
    t = dt[:, :SSM_HEADS].reshape(s_dim, SSM_GROUPS, HEADS_PER_GROUP).transpose(1, 0, 2)
    return jnp.pad(t, ((0, 0), (0, 0), (0, LANES - HEADS_PER_GROUP)))


def _dt_from_groups(dtg):
    s_dim = dtg.shape[1]
    return dtg[:, :, :HEADS_PER_GROUP].transpose(1, 0, 2).reshape(s_dim, SSM_HEADS)


def _pack_ssd_params(dt_bias, a_log, d_skip):
    rows = jnp.stack([p.reshape(SSM_GROUPS, HEADS_PER_GROUP) for p in (dt_bias, a_log, d_skip)], axis=1)
    return jnp.pad(rows, ((0, 0), (0, 8 - 3), (0, LANES - HEADS_PER_GROUP)))


def _unpack_ssd_params(par):
    return tuple(par[:, k, :HEADS_PER_GROUP].reshape(SSM_HEADS) for k in range(3))


Q = SSD_CHUNK


def _split3(v):
    hi = v.astype(BF16)
    r1 = v - hi.astype(F32)
    mid = r1.astype(BF16)
    lo = (r1 - mid.astype(F32)).astype(BF16)
    return hi, mid, lo


def _dot_l01(t01, v):
    return sum(_dot(t01, p) for p in _split3(v))


def _dot_r01(v, e01):
    return sum(_dot(p, e01) for p in _split3(v))


def _ssd_consts():
    row = lax.broadcasted_iota(jnp.int32, (Q, Q), 0)
    col = lax.broadcasted_iota(jnp.int32, (Q, Q), 1)
    causal = row >= col
    tril = causal.astype(BF16)
    triu = (col >= row).astype(BF16)
    er = lax.broadcasted_iota(jnp.int32, (LANES, GROUP_X), 0)
    ec = lax.broadcasted_iota(jnp.int32, (LANES, GROUP_X), 1) // SSM_HEAD_DIM
    expand = (er == ec).astype(BF16)
    rr = lax.broadcasted_iota(jnp.int32, (GROUP_X, LANES), 0) // SSM_HEAD_DIM
    rc = lax.broadcasted_iota(jnp.int32, (GROUP_X, LANES), 1)
    reduce = (rr == rc).astype(BF16)
    lane_head = lax.broadcasted_iota(jnp.int32, (Q, GROUP_X), 1) // SSM_HEAD_DIM
    return causal, tril, triu, expand, reduce, lane_head


def _ssd_common(xc_ref, dt_ref, par_ref, consts):
    causal, tril, _, expand, _, _ = consts
    par = par_ref[...]
    bias, alog, dsk = par[0:1, :], par[1:2, :], par[2:3, :]
    a_neg = -jnp.exp(alog)
    dtr = dt_ref[...] + bias
    dt = _softplus(dtr)
    s = _dot_l01(tril, dt * a_neg)
    dt_x = _dot_r01(dt, expand)
    s_x = _dot_r01(s, expand)
    dsk_x = _dot_r01(jnp.broadcast_to(dsk, (8, LANES)), expand)[0:1, :]
    blk = xc_ref[...]
    x = blk[:, :GROUP_X]
    bm = blk[:, GROUP_X:GROUP_X + D_STATE].astype(BF16)
    cm = blk[:, GROUP_X + D_STATE:].astype(BF16)
    xdt = x * dt_x
    g = _dot(cm, bm, _NT)
    return dict(a_neg=a_neg, dtr=dtr, dt=dt, s=s, s_t=s.T, dt_x=dt_x, s_x=s_x, dsk_x=dsk_x, x=x, bm=bm, cm=cm,
                xdt=xdt, g=g)


def _decay(v, r, causal):
    diff = v["s"][:, r:r + 1] - v["s_t"][r:r + 1, :]
    return jnp.exp(jnp.where(causal, diff, -jnp.inf))


def _ssd_specs(n_chunks, rev):
    cidx = (lambda c: n_chunks - 1 - c) if rev else (lambda c: c)
    xc = pl.BlockSpec((Q, GROUP_COLS), lambda g, c: (cidx(c), g))
    gx = pl.BlockSpec((Q, GROUP_X), lambda g, c: (cidx(c), g))
    dt = pl.BlockSpec((None, Q, LANES), lambda g, c: (g, cidx(c), 0))
    par = pl.BlockSpec((None, 8, LANES), lambda g, c: (g, 0, 0))
    nw = pl.BlockSpec((1, GROUP_X), lambda g, c: (0, g))
    hs = pl.BlockSpec((None, None, D_STATE, GROUP_X), lambda g, c: (cidx(c), g, 0, 0))
    return xc, gx, dt, par, nw, hs


def _ssd_fwd(xc, z, dtg, par, nw):
    s_dim = xc.shape[0]
    n_chunks = s_dim // Q
    xc_s, gx_s, dt_s, par_s, nw_s, hs_s = _ssd_specs(n_chunks, False)

    def body(xc_ref, z_ref, dt_ref, par_ref, nw_ref, y_ref, ys_ref, hs_ref, ht):
        @pl.when(pl.program_id(1) == 0)
        def _():
            ht[...] = jnp.zeros_like(ht)

        consts = _ssd_consts()
        causal, lane_head = consts[0], consts[5]
        v = _ssd_common(xc_ref, dt_ref, par_ref, consts)
        xdt_b = v["xdt"].astype(BF16)
        yd = jnp.zeros((Q, GROUP_X), F32)
        for r in range(HEADS_PER_GROUP):
            m = (v["g"] * _decay(v, r, causal)).astype(BF16)
            yd = yd + _dot(m, jnp.where(lane_head == r, xdt_b, jnp.zeros_like(xdt_b)))
        h = ht[...]
        hs_ref[...] = h
        yo = jnp.exp(v["s_x"]) * _dot(v["cm"], h.astype(BF16))
        y = yd + yo + v["dsk_x"] * v["x"]
        s_last = v["s_x"][Q - 1:Q, :]
        snew = _dot(v["bm"], (v["xdt"] * jnp.exp(s_last - v["s_x"])).astype(BF16), _TN)
        ht[...] = jnp.exp(s_last) * h + snew
        zz = z_ref[...]
        yg = y * (zz * _sigmoid(zz))
        y_ref[...] = y
        ys_ref[...] = _nrm(yg, nw_ref[...])[0].astype(BF16)

    return pl.pallas_call(
        body,
        grid=(SSM_GROUPS, n_chunks),
        in_specs=[xc_s, gx_s, dt_s, par_s, nw_s],
        out_specs=[gx_s, gx_s, hs_s],
        out_shape=[jax.ShapeDtypeStruct((s_dim, D_SSM), F32), jax.ShapeDtypeStruct((s_dim, D_SSM), BF16),
                   jax.ShapeDtypeStruct((n_chunks, SSM_GROUPS, D_STATE, GROUP_X), F32)],
        scratch_shapes=[pltpu.VMEM((D_STATE, GROUP_X), F32)],
        compiler_params=_cparams("parallel", "arbitrary"),
        name="ssd_fwd",
    )(xc, z, dtg, par, nw)


def _ssd_bwd(xc, z, dtg, par, nw, y, hs, dymix):
    s_dim = xc.shape[0]
    n_chunks = s_dim // Q
    xc_s, gx_s, dt_s, par_s, nw_s, hs_s = _ssd_specs(n_chunks, True)

    def body(xc_ref, z_ref, dt_ref, par_ref, nw_ref, y_ref, hs_ref, dys_ref,
             dxc_ref, dz_ref, ddt_ref, dpar_ref, dnw_ref, dht):
        @pl.when(pl.program_id(1) == 0)
        def _():
            dht[...] = jnp.zeros_like(dht)
            dpar_ref[...] = jnp.zeros_like(dpar_ref)
            dnw_ref[...] = jnp.zeros_like(dnw_ref)

        consts = _ssd_consts()
        causal, _, triu, _, reduce, lane_head = consts
        v = _ssd_common(xc_ref, dt_ref, par_ref, consts)
        x, bm, cm, xdt, s_x = v["x"], v["bm"], v["cm"], v["xdt"], v["s_x"]
        h = hs_ref[...]
        hb = h.astype(BF16)
        es_x = jnp.exp(s_x)
        yo = es_x * _dot(cm, hb)
        s_last = s_x[Q - 1:Q, :]
        e_x = jnp.exp(s_last - s_x)
        es_last = jnp.exp(s_last)

        yv, zz, nw_v = y_ref[...], z_ref[...], nw_ref[...]
        sg = _sigmoid(zz)
        gz = zz * sg
        _, n, rstd = _nrm(yv * gz, nw_v)
        dout = dys_ref[...]
        dyg, dnw = _nrm_bwd(dout, n, rstd, nw_v)
        dnw_ref[...] += dnw
        dy = dyg * gz
        dz_ref[...] = (dyg * yv * (sg * (1.0 + zz * (1.0 - sg)))).astype(BF16)

        dyb = dy.astype(BF16)
        xdt_b = xdt.astype(BF16)
        dhp = dht[...]
        dhpb = dhp.astype(BF16)
        lane = lax.broadcasted_iota(jnp.int32, (Q, LANES), 1)
        sub = lax.broadcasted_iota(jnp.int32, (LANES, Q), 0)
        dxdt = jnp.zeros((Q, GROUP_X), F32)
        dg = jnp.zeros((Q, Q), F32)
        ds = jnp.zeros((Q, LANES), F32)
        ds_t = jnp.zeros((LANES, Q), F32)
        for r in range(HEADS_PER_GROUP):
            dec = _decay(v, r, causal)
            mf = v["g"] * dec
            dyr = jnp.where(lane_head == r, dyb, jnp.zeros_like(dyb))
            dm = _dot(dyr, xdt_b, _NT)
            dxdt = dxdt + _dot(mf.astype(BF16), dyr, _TN)
            dg = dg + dm * dec
            dd = dm * mf
            ds = ds + jnp.where(lane == r, jnp.sum(dd, axis=1, keepdims=True), 0.0)
            ds_t = ds_t + jnp.where(sub == r, jnp.sum(dd, axis=0, keepdims=True), 0.0)
        ds = ds - ds_t.T
        dgb = dg.astype(BF16)
        dwb = (es_x * dy).astype(BF16)
        dcm = _dot(dgb, bm) + _dot(dwb, hb, _NT)
        dh_prev = _dot(cm, dwb, _TN)
        zst = _dot(bm, dhpb)
        xe = xdt * e_x
        dxdt = dxdt + e_x * zst
        dee = xe * zst
        dbm = _dot(dgb, cm, _TN) + _dot(xe.astype(BF16), dhpb, _NT)
        v_last = jnp.sum(dee, axis=0, keepdims=True) + es_last * jnp.sum(dhp * h, axis=0, keepdims=True)
        row_x = lax.broadcasted_iota(jnp.int32, (Q, GROUP_X), 0)
        tx = dy * yo - dee + jnp.where(row_x == Q - 1, v_last, 0.0)
        ds = ds + _dot_r01(tx, reduce)
        ddta = _dot_l01(triu, ds)
        ddt = ddta * v["a_neg"] + _dot_r01(dxdt * x, reduce)
        dalog = jnp.sum(ddta * v["dt"], axis=0, keepdims=True) * v["a_neg"]
        draw = jnp.where(lane < HEADS_PER_GROUP, ddt * _sigmoid(v["dtr"]), 0.0)
        dbias = jnp.sum(draw, axis=0, keepdims=True)
        ddsk = _dot_r01(jnp.broadcast_to(jnp.sum(dy * x, axis=0, keepdims=True), (8, GROUP_X)), reduce)[0:1, :]
        dht[...] = es_last * dhp + dh_prev
        dxc_ref[:, :GROUP_X] = dxdt * v["dt_x"] + v["dsk_x"] * dy
        dxc_ref[:, GROUP_X:GROUP_X + D_STATE] = dbm
        dxc_ref[:, GROUP_X + D_STATE:] = dcm
        ddt_ref[...] = draw
        dpar_ref[0:1, :] += dbias
        dpar_ref[1:2, :] += dalog
        dpar_ref[2:3, :] += ddsk

    return pl.pallas_call(
        body,
        grid=(SSM_GROUPS, n_chunks),
        in_specs=[xc_s, gx_s, dt_s, par_s, nw_s, gx_s, hs_s, gx_s],
        out_specs=[xc_s, gx_s, dt_s, par_s, nw_s],
        out_shape=[jax.ShapeDtypeStruct((s_dim, SSM_GROUPS * GROUP_COLS), F32),
                   jax.ShapeDtypeStruct((s_dim, D_SSM), BF16),
                   jax.ShapeDtypeStruct((SSM_GROUPS, s_dim, LANES), F32),
                   jax.ShapeDtypeStruct((SSM_GROUPS, 8, LANES), F32),
                   jax.ShapeDtypeStruct((1, D_SSM), F32)],
        scratch_shapes=[pltpu.VMEM((D_STATE, GROUP_X), F32)],
        compiler_params=_cparams("parallel", "arbitrary"),
        name="ssd_bwd",
    )(xc, z, dtg, par, nw, y, hs, dymix)


ATT_SCALE = ATT_HEAD_DIM ** -0.5
NEG_INF = -jnp.inf


def _band_masks():
    qi = lax.broadcasted_iota(jnp.int32, (ATT_BLOCK, ATT_BLOCK), 0)
    kj = lax.broadcasted_iota(jnp.int32, (ATT_BLOCK, ATT_BLOCK), 1)
    return kj <= qi, kj >= qi


WIN = ATT_BLOCK * DILATIONS[-1]
N_BLOCKS = WIN // ATT_BLOCK


def _rows(start, d):
    return pl.ds(start, ATT_BLOCK) if d == 1 else pl.ds(start, ATT_BLOCK, stride=d)


def _block_start(idx, d):
    return (idx // d) * (ATT_BLOCK * d) + idx % d


def _lane_bcast(col):
    return jnp.broadcast_to(col, (col.shape[0], LANES))


def _attn_fused_fwd(qkv):
    s_dim = qkv.shape[0]
    n_win = s_dim // WIN
    blk = (WIN, ATT_HEAD_DIM)
    prev = lambda w: jnp.maximum(w - 1, 0)

    def body(q_ref, kc_ref, kp_ref, vc_ref, vp_ref, y_ref, yf_ref, lse_ref, qf, kf, vf, acc, m_run, l_run):
        w, h = pl.program_id(0), pl.program_id(1)
        qf[...] = q_ref[...].astype(F32)
        kf[0:WIN, :] = kp_ref[...].astype(F32)
        kf[WIN:, :] = kc_ref[...].astype(F32)
        vf[0:WIN, :] = vp_ref[...].astype(F32)
        vf[WIN:, :] = vc_ref[...].astype(F32)
        own, before = _band_masks()

        for d in DILATIONS:
            def block(idx, carry, d=d):
                start = _block_start(idx, d)
                rows = _rows(start, d)
                q = qf[rows, :].astype(BF16)
                kc, vc = kf[_rows(WIN + start, d), :].astype(BF16), vf[_rows(WIN + start, d), :].astype(BF16)
                kp = kf[_rows(WIN + start - ATT_BLOCK * d, d), :].astype(BF16)
                vp = vf[_rows(WIN + start - ATT_BLOCK * d, d), :].astype(BF16)
                has_prev = (idx >= d) | (w > 0)
                sc = jnp.where(own, _dot(q, kc, _NT) * ATT_SCALE, NEG_INF)
                sp = jnp.where(before & has_prev, _dot(q, kp, _NT) * ATT_SCALE, NEG_INF)
                m_blk = jnp.maximum(jnp.max(sc, axis=1, keepdims=True), jnp.max(sp, axis=1, keepdims=True))
                if d == DILATIONS[0]:
                    m_new = m_blk
                else:
                    m_old = m_run[rows, :][:, 0:1]
                    m_new = jnp.maximum(m_old, m_blk)
                pc, pp = jnp.exp(sc - m_new), jnp.exp(sp - m_new)
                l_new = jnp.sum(pc, axis=1, keepdims=True) + jnp.sum(pp, axis=1, keepdims=True)
                o_new = _dot(pc.astype(BF16), vc) + _dot(pp.astype(BF16), vp)
                if d != DILATIONS[0]:
                    alpha = jnp.exp(m_old - m_new)
                    l_new = alpha * l_run[rows, :][:, 0:1] + l_new
                    o_new = alpha * acc[rows, :] + o_new
                m_run[rows, :] = _lane_bcast(m_new)
                l_run[rows, :] = _lane_bcast(l_new)
                acc[rows, :] = o_new
                return carry

            for idx in range(N_BLOCKS):
                block(idx, 0)

        l_all = l_run[...]
        y = acc[...] / l_all
        y_ref[...] = y.astype(BF16)
        yf_ref[...] = y
        @pl.when(h == 0)
        def _():
            lse_ref[...] = jnp.zeros_like(lse_ref)

        lane = lax.broadcasted_iota(jnp.int32, (WIN, LANES), 1)
        lse_ref[...] = jnp.where(lane == h, m_run[...] + jnp.log(l_all), lse_ref[...])

    win_scratch = lambda rows: pltpu.VMEM((rows, ATT_HEAD_DIM), F32)
    return pl.pallas_call(
        body,
        grid=(n_win, ATT_HEADS),
        in_specs=[pl.BlockSpec(blk, lambda w, h: (w, h)),
                  pl.BlockSpec(blk, lambda w, h: (w, ATT_HEADS + h)),
                  pl.BlockSpec(blk, lambda w, h: (prev(w), ATT_HEADS + h)),
                  pl.BlockSpec(blk, lambda w, h: (w, 2 * ATT_HEADS + h)),
                  pl.BlockSpec(blk, lambda w, h: (prev(w), 2 * ATT_HEADS + h))],
        out_specs=[pl.BlockSpec(blk, lambda w, h: (w, h)), pl.BlockSpec(blk, lambda w, h: (w, h)),
                   pl.BlockSpec((WIN, LANES), lambda w, h: (w, 0))],
        out_shape=[jax.ShapeDtypeStruct((s_dim, D_ATT), BF16), jax.ShapeDtypeStruct((s_dim, D_ATT), F32),
                   jax.ShapeDtypeStruct((s_dim, LANES), F32)],
        scratch_shapes=[win_scratch(WIN), win_scratch(2 * WIN), win_scratch(2 * WIN), win_scratch(WIN),
                        win_scratch(WIN), win_scratch(WIN)],
        compiler_params=_cparams("parallel", "arbitrary"),
        name="attn_fused_fwd",
    )(qkv, qkv, qkv, qkv, qkv)


def _attn_fused_bwd(qkv, dymix, y_att, lse, deps=()):
    s_dim = qkv.shape[0]
    n_win = s_dim // WIN
    blk = (WIN, ATT_HEAD_DIM)
    this = lambda w: jnp.minimum(w, n_win - 1)
    prev = lambda w: jnp.maximum(this(w) - 1, 0)
    n_dep = len(deps)

    def body(q_ref, kc_ref, kp_ref, vc_ref, vp_ref, dy_ref, y_ref, l_ref, *rest):
        dq_ref, dkv_ref = rest[n_dep:n_dep + 2]
        qf, kf, vf, dq_acc, dk_acc, dv_acc, ls_c, dl_c = rest[n_dep + 2:]
        h, w = pl.program_id(0), pl.program_id(1)
        slot, late = w % 2, 1 - w % 2

        @pl.when(w == 0)
        def _():
            dk_acc[...] = jnp.zeros_like(dk_acc)
            dv_acc[...] = jnp.zeros_like(dv_acc)

        @pl.when(w < n_win)
        def _():
            qf[...] = q_ref[...].astype(F32)
            kf[0:WIN, :] = kp_ref[...].astype(F32)
            kf[WIN:, :] = kc_ref[...].astype(F32)
            vf[0:WIN, :] = vp_ref[...].astype(F32)
            vf[WIN:, :] = vc_ref[...].astype(F32)
            lane = lax.broadcasted_iota(jnp.int32, (WIN, LANES), 1)
            ls_c[...] = _lane_bcast(jnp.sum(jnp.where(lane == h, l_ref[...], 0.0), axis=1, keepdims=True))
            dl_c[...] = _lane_bcast(jnp.sum(dy_ref[...] * y_ref[...], axis=1, keepdims=True))
            dq_acc[...] = jnp.zeros_like(dq_acc)
            dk_acc[slot] = jnp.zeros((WIN, ATT_HEAD_DIM), F32)
            dv_acc[slot] = jnp.zeros((WIN, ATT_HEAD_DIM), F32)
            own, before = _band_masks()

            def probs(q, k, v, dy, lse_col, dl_col, mask):
                p = jnp.exp(jnp.where(mask, _dot(q, k, _NT) * ATT_SCALE - lse_col, NEG_INF))
                ds = p * (_dot(dy, v, _NT) - dl_col)
                return p.astype(BF16), ds.astype(BF16)

            for d in DILATIONS:
                for idx in range(N_BLOCKS):
                    start = _block_start(idx, d)
                    rows = _rows(start, d)
                    q, dy = qf[rows, :].astype(BF16), dy_ref[rows, :].astype(BF16)
                    lse_col, dl_col = ls_c[rows, :][:, 0:1], dl_c[rows, :][:, 0:1]
                    kc, vc = kf[_rows(WIN + start, d), :].astype(BF16), vf[_rows(WIN + start, d), :].astype(BF16)
                    kp = kf[_rows(WIN + start - ATT_BLOCK * d, d), :].astype(BF16)
                    vp = vf[_rows(WIN + start - ATT_BLOCK * d, d), :].astype(BF16)
                    pc, dsc = probs(q, kc, vc, dy, lse_col, dl_col, own)
                    pp, dsp = probs(q, kp, vp, dy, lse_col, dl_col, before & ((idx >= d) | (w > 0)))
                    dq_acc[rows, :] += (_dot(dsc, kc) + _dot(dsp, kp)) * ATT_SCALE
                    dk_acc[slot, rows, :] += _dot(dsc, q, _TN) * ATT_SCALE
                    dv_acc[slot, rows, :] += _dot(pc, dy, _TN)
                    if idx >= d:
                        prows = _rows(start - ATT_BLOCK * d, d)
                        dk_acc[slot, prows, :] += _dot(dsp, q, _TN) * ATT_SCALE
                        dv_acc[slot, prows, :] += _dot(pp, dy, _TN)
                    else:
                        prows = _rows(WIN + start - ATT_BLOCK * d, d)
                        dk_acc[late, prows, :] += _dot(dsp, q, _TN) * ATT_SCALE
                        dv_acc[late, prows, :] += _dot(pp, dy, _TN)
            dq_ref[...] = dq_acc[...].astype(BF16)

        @pl.when(w > 0)
        def _():
            dkv_ref[0] = dk_acc[late].astype(BF16)
            dkv_ref[1] = dv_acc[late].astype(BF16)

    win_scratch = lambda *shape: pltpu.VMEM(shape + (ATT_HEAD_DIM,), F32)
    cur = lambda c: pl.BlockSpec(blk, lambda h, w: (this(w), c + h))
    before_spec = lambda c: pl.BlockSpec(blk, lambda h, w: (prev(w), c + h))
    return pl.pallas_call(
        body,
        grid=(ATT_HEADS, n_win + 1),
        in_specs=[cur(0), cur(ATT_HEADS), before_spec(ATT_HEADS), cur(2 * ATT_HEADS), before_spec(2 * ATT_HEADS),
                  cur(ATT_HEADS), cur(0), pl.BlockSpec((WIN, LANES), lambda h, w: (this(w), 0))] + [ANY] * n_dep,
        out_specs=[cur(0), pl.BlockSpec((2, WIN, ATT_HEAD_DIM), lambda h, w: (0, jnp.maximum(w - 1, 0), h))],
        out_shape=[jax.ShapeDtypeStruct((s_dim, D_ATT), BF16), jax.ShapeDtypeStruct((2, s_dim, D_ATT), BF16)],
        scratch_shapes=[win_scratch(WIN), win_scratch(2 * WIN), win_scratch(2 * WIN), win_scratch(WIN),
                        win_scratch(2, WIN), win_scratch(2, WIN), win_scratch(WIN), win_scratch(WIN)],
        compiler_params=_cparams("parallel", "arbitrary"),
        name="attn_fused_bwd",
    )(qkv, qkv, qkv, qkv, qkv, dymix, y_att, lse, *deps)


def _adamw(w, g, m, v, name):
    def fn(wb, gb, mb, vb):
        m2 = ADAM_B1 * mb + (1.0 - ADAM_B1) * gb
        v2 = ADAM_B2 * vb + (1.0 - ADAM_B2) * (gb * gb)
        m_hat = m2 / (1.0 - ADAM_B1 ** ADAM_STEP)
        v_hat = v2 / (1.0 - ADAM_B2 ** ADAM_STEP)
        delta = -ADAM_LR * (m_hat / (jnp.sqrt(v_hat) + ADAM_EPS) + ADAM_WD * wb)
        return delta, m2, v2
    cols = w.shape[1]
    tr = 128 if w.shape[0] % 128 == 0 else w.shape[0]
    return _rowcall(fn, [w, g, m, v], [], [(cols, F32)] * 3, [], name=name, tr=tr)


ANY = pl.BlockSpec(memory_space=pl.ANY)


def _position():
    x, y, c = lax.axis_index("x"), lax.axis_index("y"), lax.axis_index("c")
    chips = [(1 - x, y), (x, 1 - y), (1 - x, 1 - y)]
    return x, y, c, chips


def _remote(src, dst, send_sem, recv_sem, device):
    return pltpu.make_async_remote_copy(src_ref=src, dst_ref=dst, send_sem=send_sem, recv_sem=recv_sem,
                                        device_id=device, device_id_type=MESH)


def _handshake(peers):
    barrier = pltpu.get_barrier_semaphore()
    for p in peers:
        pl.semaphore_signal(barrier, inc=1, device_id=p, device_id_type=MESH)
    pl.semaphore_wait(barrier, len(peers))


def _gather_shards_async(shards, collective_id, name):
    n = len(shards)
    srcs = [jax.new_ref(s, memory_space=pltpu.MemorySpace.HBM) for s in shards]
    dsts = [jax.empty_ref(jax.ShapeDtypeStruct((N_CHIPS,) + s.shape, s.dtype), memory_space=pltpu.MemorySpace.HBM)
            for s in shards]

    @pl.kernel(mesh=plsc.ScalarSubcoreMesh(axis_name="seq", num_cores=1), name=name,
               scratch_types=(pltpu.SemaphoreType.DMA((6 * n,)), pltpu.SemaphoreType.DMA((6 * n,))),
               compiler_params=pltpu.CompilerParams(collective_id=collective_id))
    def launch(send_sems, recv_sems):
        x, y, c, chips = _position()
        sibling = (x, y, 1 - c)
        _handshake([(chip[0], chip[1], c) for chip in chips] + [sibling])

        def half(a, j, cc):
            h = shards[a].shape[0] // 2
            return dsts[a].at[j, pl.ds(cc * h, h), :]

        sent = []
        for a in range(n):
            h = shards[a].shape[0] // 2
            for j, chip in enumerate(chips):
                cp = _remote(srcs[a].at[pl.ds(c * h, h), :], half(a, 2 * x + y, c), send_sems.at[6 * a + j],
                             recv_sems.at[6 * a + j], (chip[0], chip[1], c))
                cp.start()
                sent.append(cp)
        for a in range(n):
            for j, chip in enumerate(chips):
                landed = half(a, 2 * chip[0] + chip[1], c)
                _remote(landed, landed, send_sems.at[6 * a + j], recv_sems.at[6 * a + j], (x, y, c)).wait_recv()
                cp = _remote(landed, landed, send_sems.at[6 * a + 3 + j], recv_sems.at[6 * a + 3 + j], sibling)
                cp.start()
                sent.append(cp)
        for a in range(n):
            for j, chip in enumerate(chips):
                handed = half(a, 2 * chip[0] + chip[1], 1 - c)
                _remote(handed, handed, send_sems.at[6 * a + 3 + j], recv_sems.at[6 * a + 3 + j], (x, y, c)).wait_recv()
        for cp in sent:
            cp.wait_send()

    launch()
    return [d[...] for d in dsts]


IN_COLS = {"z": (0, D_SSM), "xbc": (D_SSM, D_SSM + D_XBC), "dt": (D_SSM + D_XBC, D_SSM + D_XBC + SSM_HEADS),
           "qkv": (D_SSM + D_XBC + SSM_HEADS, D_IN_PROJ)}


def _cols_from_quarters(quarters, lo, hi):
    parts = []
    for q in range(N_CHIPS):
        a, b = max(lo, q * W_IN_SHARD), min(hi, (q + 1) * W_IN_SHARD)
        if a < b:
            parts.append(quarters[q][:, a - q * W_IN_SHARD:b - q * W_IN_SHARD])
    return parts[0] if len(parts) == 1 else jnp.concatenate(parts, axis=1)


def _quarters_from_cols(pieces):
    quarters = []
    for q in range(N_CHIPS):
        parts = []
        for name, (lo, hi) in IN_COLS.items():
            a, b = max(lo, q * W_IN_SHARD), min(hi, (q + 1) * W_IN_SHARD)
            if a < b:
                parts.append(pieces[name][:, a - lo:b - lo])
        quarters.append(jnp.concatenate(parts, axis=1))
    return jnp.stack(quarters)


def _by_chip(own, fetched):
    me = 2 * lax.axis_index("x") + lax.axis_index("y")
    return lax.dynamic_update_slice(fetched, own[None], (me, 0, 0))


def _add_sibling(grad, got, place, name, deps=()):
    nq, rows, cols = grad.shape
    h = rows // 2
    tr = 128
    nb = h // tr

    def body(place_ref, a_ref, b_ref, *rest):
        own_ref, ob_ref = rest[len(deps):]
        total = a_ref[...] + b_ref[...]
        ob_ref[...] = total.astype(BF16)

        @pl.when(pl.program_id(1) == place_ref[1])
        def _():
            own_ref[...] = total

    return pl.pallas_call(
        body,
        grid_spec=pltpu.PrefetchScalarGridSpec(
            num_scalar_prefetch=1, grid=(nb, nq),
            in_specs=[pl.BlockSpec((None, tr, cols), lambda i, q, p: (q, p[0] * nb + i, 0)),
                      pl.BlockSpec((None, tr, cols), lambda i, q, p: (q, i, 0))] + [ANY] * len(deps),
            out_specs=[pl.BlockSpec((tr, cols), lambda i, q, p: (i, 0)),
                       pl.BlockSpec((None, tr, cols), lambda i, q, p: (q, i, 0))]),
        out_shape=[jax.ShapeDtypeStruct((h, cols), F32), jax.ShapeDtypeStruct((nq, h, cols), BF16)],
        compiler_params=_cparams("parallel", "arbitrary"),
        name=name,
    )(place, grad, got, *deps)


def _add_chips(part, got, name, deps=()):
    h, cols = part.shape
    tr = 128

    def body(p_ref, g0_ref, g1_ref, g2_ref, *rest):
        o_ref = rest[len(deps)]
        o_ref[...] = ((p_ref[...] + g0_ref[...].astype(F32)) + g1_ref[...].astype(F32)) + g2_ref[...].astype(F32)

    got_spec = lambda j: pl.BlockSpec((None, tr, cols), lambda i: (j, i, 0))
    row_spec = pl.BlockSpec((tr, cols), lambda i: (i, 0))
    return pl.pallas_call(
        body,
        grid=(h // tr,),
        in_specs=[row_spec, got_spec(0), got_spec(1), got_spec(2)] + [ANY] * len(deps),
        out_specs=row_spec,
        out_shape=jax.ShapeDtypeStruct((h, cols), F32),
        compiler_params=_cparams("parallel"),
        name=name,
    )(part, got, got, got, *deps)


def _sequencer_exchange(src, out_shape, collective_id, name, plan, n_copies):
    src_ref = jax.new_ref(src, memory_space=pltpu.MemorySpace.HBM)
    dst_ref = jax.empty_ref(out_shape, memory_space=pltpu.MemorySpace.HBM)

    @pl.kernel(mesh=plsc.ScalarSubcoreMesh(axis_name="seq", num_cores=1), name=name,
               scratch_types=(pltpu.SemaphoreType.DMA((n_copies,)), pltpu.SemaphoreType.DMA((n_copies,))),
               compiler_params=pltpu.CompilerParams(collective_id=collective_id))
    def launch(send_sems, recv_sems):
        x, y, c, chips = _position()
        copies = plan(src_ref, dst_ref, x, y, c, chips)
        _handshake([peer for _, _, peer in copies])
        started = []
        for k, (s, d, peer) in enumerate(copies):
            cp = _remote(s, d, send_sems.at[k], recv_sems.at[k], peer)
            cp.start()
            started.append(cp)
        for cp in started:
            cp.wait()

    launch()
    return dst_ref[...]


class _AsyncReduceScatter:
    def __init__(self, grad, nm, first_id):
        self.grad, self.nm, self.first_id = grad, nm, first_id
        nq, rows, cols = grad.shape
        h = self.h = rows // 2

        def to_sibling(s, d, x, y, c, chips):
            return [(s.at[:, pl.ds((1 - c) * h, h), :], d, (x, y, 1 - c))]

        self.from_sibling = _sequencer_exchange(grad, jax.ShapeDtypeStruct((nq, h, cols), F32), first_id,
                                                f"rs_sibling_{nm}", to_sibling, 1)

    def sibling_sum(self, not_before=()):
        cols = self.grad.shape[2]
        place = jnp.stack([lax.axis_index("c"), 2 * lax.axis_index("x") + lax.axis_index("y")]).astype(jnp.int32)
        self.part, self.part_b = _add_sibling(self.grad, self.from_sibling, place, f"add_sibling_{self.nm}", not_before)

        def to_chips(s, d, x, y, c, chips):
            return [(s.at[2 * chip[0] + chip[1]], d.at[j], (chip[0], chip[1], c)) for j, chip in enumerate(chips)]

        self.from_chips = _sequencer_exchange(self.part_b, jax.ShapeDtypeStruct((3, self.h, cols), BF16),
                                              self.first_id + 1, f"rs_quarters_{self.nm}", to_chips, 3)
        return self.part_b

    def chip_sum(self, not_before=()):
        cols = self.grad.shape[2]
        self.half = _add_chips(self.part, self.from_chips, f"add_chips_{self.nm}", not_before)

        def whole_to_sibling(s, d, x, y, c, chips):
            return [(s, d, (x, y, 1 - c))]

        self.other = _sequencer_exchange(self.half, jax.ShapeDtypeStruct((self.h, cols), F32), self.first_id + 2,
                                         f"rs_share_{self.nm}", whole_to_sibling, 1)
        return self.half

    def share(self):
        return self.half, self.other


def _after(x, deps, name):
    def body(x_ref, *rest):
        rest[-1][...] = x_ref[...]

    vm = pl.BlockSpec(memory_space=pltpu.VMEM)
    return pl.pallas_call(body, in_specs=[vm] + [ANY] * len(deps), out_specs=vm,
                          out_shape=jax.ShapeDtypeStruct(x.shape, x.dtype), name=name)(x, *deps)


def _adamw_halves(w, mine, other, m, v, name):
    rows, cols = w.shape
    tr = 128
    nb = rows // 2 // tr
    c_arr = lax.axis_index("c").astype(jnp.int32).reshape(1)

    def body(c_ref, w_ref, a_ref, b_ref, m_ref, v_ref, g_out, d_out, m_out, v_out):
        is_mine = (pl.program_id(0) // nb) == c_ref[0]
        g = jnp.where(is_mine, a_ref[...], b_ref[...])
        wb, mb, vb = w_ref[...], m_ref[...], v_ref[...]
        m2 = ADAM_B1 * mb + (1.0 - ADAM_B1) * g
        v2 = ADAM_B2 * vb + (1.0 - ADAM_B2) * (g * g)
        m_hat = m2 / (1.0 - ADAM_B1 ** ADAM_STEP)
        v_hat = v2 / (1.0 - ADAM_B2 ** ADAM_STEP)
        g_out[...] = g
        d_out[...] = -ADAM_LR * (m_hat / (jnp.sqrt(v_hat) + ADAM_EPS) + ADAM_WD * wb)
        m_out[...] = m2
        v_out[...] = v2

    full = pl.BlockSpec((tr, cols), lambda i, c: (i, 0))
    half = pl.BlockSpec((tr, cols), lambda i, c: (i % nb, 0))
    return pl.pallas_call(
        body,
        grid_spec=pltpu.PrefetchScalarGridSpec(
            num_scalar_prefetch=1, grid=(rows // tr,),
            in_specs=[full, half, half, full, full], out_specs=[full] * 4),
        out_shape=[jax.ShapeDtypeStruct((rows, cols), F32)] * 4,
        compiler_params=_cparams("parallel"),
        name=name,
    )(c_arr, w, mine, other, m, v)


def _all_sum_small(v):
    n_dev = 8

    def body(v_ref, o_ref, gath, send_sems, recv_sems):
        x, y, c, _ = _position()
        me = 4 * x + 2 * y + c
        gath[me] = v_ref[...]
        copies = []
        for k in range(1, n_dev):
            peer = tuple(1 - p if (k >> s) & 1 else p for p, s in ((x, 2), (y, 1), (c, 0)))
            cp = _remote(v_ref, gath.at[me], send_sems.at[k - 1], recv_sems.at[k - 1], peer)
            cp.start()
            copies.append(cp)
        for cp in copies:
            cp.wait()
        acc = gath[0]
        for i in range(1, n_dev):
            acc = acc + gath[i]
        o_ref[...] = acc

    vm = pl.BlockSpec(memory_space=pltpu.VMEM)
    return pl.pallas_call(
        body,
        in_specs=[vm],
        out_specs=vm,
        out_shape=jax.ShapeDtypeStruct(v.shape, F32),
        scratch_shapes=[pltpu.VMEM((n_dev,) + v.shape, F32), pltpu.SemaphoreType.DMA((n_dev - 1,)),
                        pltpu.SemaphoreType.DMA((n_dev - 1,))],
        name="all_sum_small",
    )(v)


def _pack_rows(vectors):
    rows = []
    for v in vectors:
        flat = v.reshape(-1).astype(F32)
        rows.append(jnp.pad(flat, (0, (-flat.shape[0]) % LANES)).reshape(-1, LANES))
    out = jnp.concatenate(rows, axis=0)
    return jnp.pad(out, ((0, (-out.shape[0]) % 8), (0, 0)))


def _unpack_rows(packed, shapes):
    outs, r = [], 0
    for shp in shapes:
        size = math.prod(shp)
        nr = -(-size // LANES)
        outs.append(packed[r:r + nr].reshape(-1)[:size].reshape(shp))
        r += nr
    return outs


def _relu_sq(acc):
    r = jnp.maximum(acc, 0.0)
    return r, r * r


def _relu_sq_bwd(acc, r):
    return (acc * (2.0 * r.astype(F32)),)


def kernel(x, norm_mix_pre, w_in, conv_w, conv_b, dt_bias, a_log, d_skip, ssm_norm_w, w_out, norm_mix_post, norm_mlp_pre, w_up, w_down, norm_mlp_post, loss_target, m_norm_mix_pre, m_w_in, m_conv_w, m_conv_b, m_dt_bias, m_a_log, m_d_skip, m_ssm_norm_w, m_w_out, m_norm_mix_post, m_norm_mlp_pre, m_w_up, m_w_down, m_norm_mlp_post, v_norm_mix_pre, v_w_in, v_conv_w, v_conv_b, v_dt_bias, v_a_log, v_d_skip, v_ssm_norm_w, v_w_out, v_norm_mix_post, v_norm_mlp_pre, v_w_up, v_w_down, v_norm_mlp_post):
    s_dim = x.shape[1]
    xs, target = x[0], loss_target[0]
    chip = 2 * lax.axis_index("x") + lax.axis_index("y")

    own = [w_in[0].astype(BF16), w_out[0].astype(BF16), w_up[0].astype(BF16), w_down[0].astype(BF16)]
    fetched_in = _gather_shards_async(own[:1], 14, "gather_w_in")[0]
    conv_cols = D_XBC // N_CHIPS
    conv_placed = lax.dynamic_update_slice(jnp.zeros((8, D_XBC), F32), 0.5 * conv_w[0], (0, chip * conv_cols))
    conv_full = _all_sum_small(conv_placed.reshape(-1, LANES)).reshape(8, D_XBC)
    w8 = _perm_cols(conv_full.at[CONV_WIDTH].set(conv_b[0]))
    u = _pre_norm(xs, norm_mix_pre)
    fetched_in, u, w8, *rest = lax.optimization_barrier((fetched_in, u, w8, *own[1:]))
    fetched = [fetched_in] + _gather_shards_async(rest, 1, "gather_rest")
    g_in, g_out, g_up, g_down = [_by_chip(o, f) for o, f in zip(own, fetched)]
    w_z = _cols_from_quarters(g_in, *IN_COLS["z"])
    w_xbc = _perm_cols(_cols_from_quarters(g_in, *IN_COLS["xbc"]))
    w_dt = jnp.pad(_cols_from_quarters(g_in, *IN_COLS["dt"]), ((0, 0), (0, LANES - SSM_HEADS)))
    w_qkv = _cols_from_quarters(g_in, *IN_COLS["qkv"])
    w_out_full = g_out.reshape(D_MIX, D_MODEL)
    w_down_full = g_down.reshape(D_FF, D_MODEL)

    z = _matmul([(u, w_z, TK)], "nn", [F32], name="proj_z")
    xbc = _matmul([(u, w_xbc, TK)], "nn", [F32], name="proj_xbc")
    dt_raw = _matmul([(u, w_dt, TK)], "nn", [F32], name="proj_dt")
    qkv = _matmul([(u, w_qkv, TK)], "nn", [BF16], name="proj_qkv")
    xc = _conv_fwd(xbc, w8)
    dtg = _dt_to_groups(dt_raw)
    par = _pack_ssd_params(dt_bias[0], a_log[0], d_skip[0])
    y, y_ssm, states = _ssd_fwd(xc, z, dtg, par, ssm_norm_w)
    y_att, y_att_f32, lse = _attn_fused_fwd(qkv)
    y_mix = jnp.concatenate([y_ssm, y_att], axis=1)
    mix = _matmul([(y_mix, w_out_full, TK)], "nn", [F32], name="out_proj")
    h1, u2 = _post_pre_norm(xs, mix, norm_mix_post, norm_mlp_pre)
    hid, act = _matmul([(u2, g_up, TK)], "nn", [BF16, BF16], name="mlp_up", epilogue=_relu_sq)
    ff = _matmul([(act, w_down_full, TK)], "nn", [F32], name="mlp_down")
    dh2, dff, d_g4, loss_part = _tail(ff, h1, target, norm_mlp_post)

    dhid = _matmul([(dff, w_down_full, TK)], "nt", [BF16], name="mlp_down_dx", epilogue=_relu_sq_bwd, extras=[hid])
    weights = {"norm_mix_pre": (norm_mix_pre, m_norm_mix_pre, v_norm_mix_pre), "w_in": (w_in, m_w_in, v_w_in),
               "conv_w": (conv_w, m_conv_w, v_conv_w), "conv_b": (conv_b, m_conv_b, v_conv_b),
               "dt_bias": (dt_bias, m_dt_bias, v_dt_bias), "a_log": (a_log, m_a_log, v_a_log),
               "d_skip": (d_skip, m_d_skip, v_d_skip), "ssm_norm_w": (ssm_norm_w, m_ssm_norm_w, v_ssm_norm_w),
               "w_out": (w_out, m_w_out, v_w_out), "norm_mix_post": (norm_mix_post, m_norm_mix_post, v_norm_mix_post),
               "norm_mlp_pre": (norm_mlp_pre, m_norm_mlp_pre, v_norm_mlp_pre), "w_up": (w_up, m_w_up, v_w_up),
               "w_down": (w_down, m_w_down, v_w_down),
               "norm_mlp_post": (norm_mlp_post, m_norm_mlp_post, v_norm_mlp_post)}
    grads, delta, new_m, new_v = {}, {}, {}, {}

    def adamw_big(n, halves):
        w, m, v = weights[n]
        g_, d_, m_, v_ = _adamw_halves(w[0], halves[0], halves[1], m[0], v[0], f"adamw_{n}")
        grads[n], delta[n], new_m[n], new_v[n] = g_[None], d_[None], m_[None], v_[None]

    dw_down = _matmul([(act, dff, TK)], "tn", [F32], name="mlp_down_dw")
    rs_down = _AsyncReduceScatter(dw_down.reshape(N_CHIPS, D_FF // N_CHIPS, D_MODEL), "w_down", 11)
    dw_up = _matmul([(u2, dhid, TK)], "tn", [F32], name="mlp_up_dw", deps=[dw_down], out_quarters=True)
    rs_up = _AsyncReduceScatter(dw_up, "w_up", 8)
    du2 = _matmul([(dhid, g_up, TK)], "nt", [F32], name="mlp_up_dx",
                  deps=[rs_down.sibling_sum(not_before=[dw_up])])
    dh1, dmix, d_g3, d_g2 = _mid_bwd(du2, h1, dh2, mix, norm_mix_post, norm_mlp_pre,
                                     deps=[rs_up.sibling_sum(not_before=[du2])])
    dymix = _matmul([(dmix, w_out_full, TK)], "nt", [F32], name="out_proj_dx")
    dw_out = _matmul([(y_mix, dmix, TK)], "tn", [F32], name="out_proj_dw")
    rs_out = _AsyncReduceScatter(dw_out.reshape(N_CHIPS, D_MIX // N_CHIPS, D_MODEL), "w_out", 5)
    dq, dkv = _attn_fused_bwd(qkv, dymix, y_att_f32, lse)
    dqkv = jnp.concatenate([dq[None], dkv], axis=0)
    par_late = _after(par, [rs_down.chip_sum(not_before=[dqkv]), rs_out.sibling_sum(not_before=[dymix])],
                      "after_w_down")
    dxc, dz, ddtg, dpar, d_nw = _ssd_bwd(xc, z, dtg, par_late, ssm_norm_w, y, states, dymix)
    g_down = rs_down.share()
    dxbc, dw8 = _conv_bwd(xbc, _after(w8, [*g_down, rs_up.chip_sum(not_before=[dxc])], "after_w_up"), dxc)
    ddt = jnp.pad(_dt_from_groups(ddtg), ((0, 0), (0, LANES - SSM_HEADS))).astype(BF16)
    g_up = rs_up.share()
    dw_z = _matmul([(u, dz, TK)], "tn", [F32], name="proj_z_dw")
    dw_xbc = _matmul([(u, dxbc, TK)], "tn", [F32], name="proj_xbc_dw",
                     deps=[*g_up, rs_out.chip_sum(not_before=[dxbc])])
    g_out = rs_out.share()
    dw_dt = _matmul([(u, ddt, TK)], "tn", [F32], name="proj_dt_dw")
    dw_qkv = _matmul([(u, dqkv, TK)], "tn", [F32], name="proj_qkv_dw")
    dw_in = _quarters_from_cols({"z": dw_z, "xbc": _unperm_cols(dw_xbc), "dt": dw_dt[:, :SSM_HEADS], "qkv": dw_qkv})
    rs_in = _AsyncReduceScatter(dw_in, "w_in", 2)
    adamw_big("w_down", g_down)
    adamw_big("w_up", g_up)
    rs_in.sibling_sum(not_before=[delta["w_up"]])
    du = _matmul([(dz, w_z, TK_MULTI), (dxbc, w_xbc, TK_MULTI), (dqkv, w_qkv, TK_MULTI), (ddt, w_dt, LANES)], "nt",
                 [F32], name="proj_dx", deps=[*g_out, rs_in.part_b])
    grad_x, d_g1 = _first_bwd(du, xs, dh1, norm_mix_pre)
    adamw_big("w_out", g_out)
    rs_in.chip_sum(not_before=[grad_x, delta["w_out"]])

    dconv = _unperm_cols(dw8)
    d_bias, d_alog, d_dskip = _unpack_ssd_params(dpar)
    small_shapes = [(1, D_MODEL), (CONV_WIDTH, D_XBC), (1, D_XBC), (1, SSM_HEADS), (1, SSM_HEADS), (1, SSM_HEADS),
                    (1, D_SSM), (1, D_MODEL), (1, D_MODEL), (1, D_MODEL), (1, LANES)]
    summed = _unpack_rows(
        _all_sum_small(_pack_rows([d_g1, dconv[:CONV_WIDTH], dconv[CONV_WIDTH:CONV_WIDTH + 1], d_bias, d_alog,
                                   d_dskip, d_nw, d_g2, d_g3, d_g4, loss_part])), small_shapes)
    (g_g1, g_conv_full, g_conv_b, g_bias, g_alog, g_dskip, g_nw, g_g2, g_g3, g_g4, loss_row) = summed
    loss = loss_row[0, 0]
    g_conv_w = lax.dynamic_slice(g_conv_full, (0, chip * conv_cols), (CONV_WIDTH, conv_cols))[None]

    grads.update({"norm_mix_pre": g_g1, "conv_w": g_conv_w, "conv_b": g_conv_b, "dt_bias": g_bias,
                  "a_log": g_alog, "d_skip": g_dskip, "ssm_norm_w": g_nw, "norm_mix_post": g_g2,
                  "norm_mlp_pre": g_g3, "norm_mlp_post": g_g4})
    order = list(weights)
    small_names = [n for n in order if n not in ("w_in", "w_out", "w_up", "w_down")]
    small_w_shapes = [weights[n][0].shape for n in small_names]
    packed = [_pack_rows([weights[n][k] for n in small_names]) for k in range(3)]
    packed_g = _pack_rows([grads[n].reshape(weights[n][0].shape) for n in small_names])
    sd, sm, sv = _adamw(packed[0], packed_g, packed[1], packed[2], "adamw_small")
    for k, n in enumerate(small_names):
        grads[n] = grads[n].reshape(weights[n][0].shape)
    for res, pk in ((delta, sd), (new_m, sm), (new_v, sv)):
        for n, val in zip(small_names, _unpack_rows(pk, small_w_shapes)):
            res[n] = val
    adamw_big("w_in", rs_in.share())

    return (loss, grad_x[None], *[grads[n] for n in order], *[delta[n] for n in order],
            *[new_m[n] for n in order], *[new_v[n] for n in order])
```

```python
import math

import numpy as np
import jax
import jax.numpy as jnp
from jax import lax
from jax.experimental import pallas as pl
from jax.experimental.pallas import tpu as pltpu
from jax.experimental.pallas import tpu_sc as plsc

F32 = jnp.float32
BF16 = jnp.bfloat16

D_MODEL = 2048
SSM_HEAD_DIM = 64
SSM_GROUPS = 8
HEADS_PER_GROUP = 4
SSM_HEADS = SSM_GROUPS * HEADS_PER_GROUP
D_SSM = SSM_HEADS * SSM_HEAD_DIM
D_STATE = 128
CONV_WIDTH = 4
SSD_CHUNK = 128
D_XBC = D_SSM + 2 * SSM_GROUPS * D_STATE
GROUP_X = HEADS_PER_GROUP * SSM_HEAD_DIM
GROUP_COLS = GROUP_X + 2 * D_STATE
ATT_HEAD_DIM = 128
ATT_HEADS = 16
D_ATT = ATT_HEADS * ATT_HEAD_DIM
DILATIONS = (1, 4, 16)
ATT_BLOCK = 128
D_MIX = D_SSM + D_ATT
D_IN_PROJ = D_SSM + D_XBC + SSM_HEADS + 3 * D_ATT
D_FF = 4 * D_MODEL
EPS = 1e-6
N_CHIPS = 4
W_IN_SHARD = D_IN_PROJ // N_CHIPS

ADAM_LR = 0.001
ADAM_B1 = 0.9
ADAM_B2 = 0.999
ADAM_EPS = 1e-08
ADAM_WD = 0.01
ADAM_STEP = 10

LANES = 128
VMEM_LIMIT = 48 * 1024 * 1024
MESH = pl.DeviceIdType.MESH

_NN = (((1,), (0,)), ((), ()))
_NT = (((1,), (1,)), ((), ()))
_TN = (((0,), (0,)), ((), ()))


def _dot(a, b, dims=_NN):
    return lax.dot_general(a, b, dims, preferred_element_type=F32)


def _cparams(*sem):
    return pltpu.CompilerParams(dimension_semantics=sem, vmem_limit_bytes=VMEM_LIMIT)


TK = 2048
TK_MULTI = 1024


def _matmul(pairs, mode, out_dtypes, *, name, tm=1024, tn=1024, epilogue=None, extras=(), deps=(), out_quarters=False):
    a0, b0, _ = pairs[0]
    m_dim = a0.shape[-1] if mode == "tn" else a0.shape[-2]
    if b0.ndim == 3:
        n_dim = b0.shape[1] if mode == "nt" else b0.shape[0] * b0.shape[2]
    else:
        n_dim = b0.shape[0] if mode == "nt" else b0.shape[1]
    tm, tn = min(tm, m_dim), min(tn, n_dim)
    nks, offs = [], []
    for a, _, tk in pairs:
        k_part = a.shape[0] if mode == "tn" else a.shape[-1]
        k_dim = k_part * (a.shape[0] if a.ndim == 3 else 1)
        assert k_part % tk == 0, (name, k_part, tk)
        offs.append(sum(nks))
        nks.append(k_dim // tk)
    nk_total = sum(nks)
    assert m_dim % tm == 0 and n_dim % tn == 0, (name, m_dim, n_dim)
    dims = {"nn": _NN, "nt": _NT, "tn": _TN}[mode]
    n_pairs, n_extra, n_out = len(pairs), len(extras), len(out_dtypes)

    in_specs, operands = [], []
    for (a, b, tk), off, nk in zip(pairs, offs, nks):
        def kidx(k, off=off, nk=nk):
            return k if n_pairs == 1 else jnp.clip(k - off, 0, nk - 1)
        if mode == "tn":
            assert a.ndim == 2
            in_specs.append(pl.BlockSpec((tk, tm), lambda m, n, k, f=kidx: (f(k), m)))
        elif a.ndim == 3:
            per = a.shape[2] // tk
            in_specs.append(pl.BlockSpec((None, tm, tk), lambda m, n, k, f=kidx, per=per: (f(k) // per, m, f(k) % per)))
        else:
            in_specs.append(pl.BlockSpec((tm, tk), lambda m, n, k, f=kidx: (m, f(k))))
        if b.ndim == 3 and mode == "nt":
            per = b.shape[2] // tk
            in_specs.append(pl.BlockSpec((None, tn, tk), lambda m, n, k, f=kidx, per=per: (f(k) // per, n, f(k) % per)))
        elif b.ndim == 3:
            per = b.shape[2] // tn
            in_specs.append(pl.BlockSpec((None, tk, tn), lambda m, n, k, f=kidx, per=per: (n // per, f(k), n % per)))
        elif mode == "nt":
            in_specs.append(pl.BlockSpec((tn, tk), lambda m, n, k, f=kidx: (n, f(k))))
        else:
            in_specs.append(pl.BlockSpec((tk, tn), lambda m, n, k, f=kidx: (f(k), n)))
        operands += [a, b]
    for e in extras:
        in_specs.append(pl.BlockSpec((tm, tn), lambda m, n, k: (m, n)))
        operands.append(e)
    in_specs += [pl.BlockSpec(memory_space=pl.ANY)] * len(deps)
    operands += list(deps)
    first_out = 2 * n_pairs + n_extra + len(deps)
    if out_quarters:
        out_per_q = n_dim // N_CHIPS // tn
        out_dims = (N_CHIPS, m_dim, n_dim // N_CHIPS)
        out_spec = pl.BlockSpec((None, tm, tn), lambda m, n, k: (n // out_per_q, m, n % out_per_q))
    else:
        out_dims = (m_dim, n_dim)
        out_spec = pl.BlockSpec((tm, tn), lambda m, n, k: (m, n))

    def body(*refs):
        ab = refs[:2 * n_pairs]
        e_refs = refs[2 * n_pairs:2 * n_pairs + n_extra]
        o_refs = refs[first_out:first_out + n_out]

        def finish(total):
            vals = (total,) if epilogue is None else epilogue(total, *[e[...] for e in e_refs])
            for o_ref, v in zip(o_refs, vals):
                o_ref[...] = v.astype(o_ref.dtype)

        if nk_total == 1:
            finish(_dot(ab[0][...], ab[1][...], dims))
            return
        acc = refs[-1]
        k = pl.program_id(2)

        @pl.when(k == 0)
        def _():
            acc[...] = jnp.zeros_like(acc)

        for i in range(n_pairs):
            def accumulate(i=i):
                acc[...] += _dot(ab[2 * i][...], ab[2 * i + 1][...], dims)
            if n_pairs == 1:
                accumulate()
            else:
                pl.when((k >= offs[i]) & (k < offs[i] + nks[i]))(accumulate)

        @pl.when(k == nk_total - 1)
        def _():
            finish(acc[...])

    outs = pl.pallas_call(
        body,
        grid=(m_dim // tm, n_dim // tn, nk_total),
        in_specs=in_specs,
        out_specs=[out_spec for _ in out_dtypes],
        out_shape=[jax.ShapeDtypeStruct(out_dims, dt) for dt in out_dtypes],
        scratch_shapes=[pltpu.VMEM((tm, tn), F32)] if nk_total > 1 else [],
        compiler_params=_cparams("parallel", "parallel", "arbitrary"),
        name=name,
    )(*operands)
    return outs[0] if n_out == 1 else outs


def _rowcall(fn, rows, vecs, row_outs, acc_widths, *, name, tr=256, row_cols=None, deps=()):
    s_dim = rows[0].shape[0]
    assert s_dim % tr == 0
    row_cols = row_cols or [None] * len(rows)
    n_r, n_v, n_ro, n_acc = len(rows), len(vecs), len(row_outs), len(acc_widths)
    in_specs = []
    for r, rc in zip(rows, row_cols):
        if rc is None:
            in_specs.append(pl.BlockSpec((tr, r.shape[1]), lambda i: (i, 0)))
        else:
            in_specs.append(pl.BlockSpec((tr, rc[0]), lambda i, c=rc[1]: (i, c)))
    for v in vecs:
        in_specs.append(pl.BlockSpec(v.shape, lambda i, nd=v.ndim: (0,) * nd))
    in_specs += [pl.BlockSpec(memory_space=pl.ANY)] * len(deps)
    n_d = len(deps)

    def body(*refs):
        ins = [r[...] for r in refs[:n_r + n_v]]
        ro = refs[n_r + n_v + n_d:n_r + n_v + n_d + n_ro]
        ao = refs[n_r + n_v + n_d + n_ro:]
        outs = fn(*ins)
        for ref, v in zip(ro, outs[:n_ro]):
            ref[...] = v.astype(ref.dtype)
        if n_acc:
            @pl.when(pl.program_id(0) == 0)
            def _():
                for ref in ao:
                    ref[...] = jnp.zeros_like(ref)
            for ref, v in zip(ao, outs[n_ro:]):
                ref[...] += v

    outs = pl.pallas_call(
        body,
        grid=(s_dim // tr,),
        in_specs=in_specs,
        out_specs=[pl.BlockSpec((tr, w), lambda i: (i, 0)) for w, _ in row_outs]
        + [pl.BlockSpec((1, w), lambda i: (0, 0)) for w in acc_widths],
        out_shape=[jax.ShapeDtypeStruct((s_dim, w), dt) for w, dt in row_outs]
        + [jax.ShapeDtypeStruct((1, w), F32) for w in acc_widths],
        compiler_params=_cparams("arbitrary"),
        name=name,
    )(*rows, *vecs, *deps)
    return outs


def _nrm(x, g):
    r = lax.rsqrt(jnp.mean(x * x, axis=-1, keepdims=True) + EPS)
    n = x * r
    return n * g, n, r


def _nrm_bwd(dy, n, r, g):
    dn = dy * g
    dx = r * (dn - n * jnp.mean(dn * n, axis=-1, keepdims=True))
    return dx, jnp.sum(dy * n, axis=0, keepdims=True)


def _sigmoid(x):
    return 1.0 / (1.0 + jnp.exp(-x))


def _softplus(x):
    return jnp.maximum(x, 0.0) + jnp.log(1.0 + jnp.exp(-jnp.abs(x)))


def _pre_norm(x, g1):
    def fn(xb, g):
        return (_nrm(xb, g)[0],)
    return _rowcall(fn, [x], [g1], [(D_MODEL, BF16)], [], name="pre_norm")[0]


def _post_pre_norm(x, mix, g2, g3):
    def fn(xb, mb, g2b, g3b):
        h1 = xb + _nrm(mb, g2b)[0]
        return h1, _nrm(h1, g3b)[0]
    return _rowcall(fn, [x, mix], [g2, g3], [(D_MODEL, F32), (D_MODEL, BF16)], [], name="post_pre_norm")


def _tail(ff, h1, target, g4):
    def fn(ffb, h1b, tb, g):
        y, n, r = _nrm(ffb, g)
        e = h1b + y - tb
        loss = 0.5 * jnp.sum(jnp.sum(e * e, axis=-1, keepdims=True) * (1.0 / D_MODEL), axis=0, keepdims=True)
        dh2 = e * (1.0 / D_MODEL)
        dff, dg = _nrm_bwd(dh2, n, r, g)
        return dh2, dff, dg, jnp.broadcast_to(loss, (1, LANES))
    return _rowcall(fn, [ff, h1, target], [g4], [(D_MODEL, F32), (D_MODEL, BF16)], [D_MODEL, LANES], name="tail")


def _mid_bwd(du2, h1, dh2, mix, g2, g3, deps=()):
    def fn(du2b, h1b, dh2b, mb, g2b, g3b):
        _, n3, r3 = _nrm(h1b, g3b)
        d3, dg3 = _nrm_bwd(du2b, n3, r3, g3b)
        dh1 = dh2b + d3
        _, n2, r2 = _nrm(mb, g2b)
        dmix, dg2 = _nrm_bwd(dh1, n2, r2, g2b)
        return dh1, dmix, dg3, dg2
    return _rowcall(fn, [du2, h1, dh2, mix], [g2, g3], [(D_MODEL, F32), (D_MODEL, BF16)], [D_MODEL, D_MODEL],
                    name="mid_bwd", deps=deps)


def _first_bwd(du, x, dh1, g1):
    def fn(dub, xb, dh1b, g):
        _, n, r = _nrm(xb, g)
        dx, dg = _nrm_bwd(dub, n, r, g)
        return dh1b + dx, dg
    return _rowcall(fn, [du, x, dh1], [g1], [(D_MODEL, F32)], [D_MODEL], name="first_bwd")


CONV_TILE = 256
CONV_ROWS = 256
PAD = 8


def _conv_taps(w):
    return [w[k:k + 1, :] for k in range(CONV_WIDTH)], w[CONV_WIDTH:CONV_WIDTH + 1, :]


def _conv_fwd(xbc, w8):
    s_dim, c_dim = xbc.shape
    n_steps = s_dim // CONV_ROWS

    def body(x_ref, w_ref, o_ref, xp):
        xp[0:PAD, :] = jnp.zeros((PAD, CONV_TILE), F32)
        xp[PAD:PAD + s_dim, :] = x_ref[...]
        taps, bias = _conv_taps(w_ref[...])

        def step(c, carry):
            base = pl.multiple_of(c * CONV_ROWS, CONV_ROWS)
            win = xp[pl.ds(base, CONV_ROWS + PAD), :]
            pre = bias + taps[3] * win[PAD:, :]
            for j in range(1, CONV_WIDTH):
                pre = pre + taps[3 - j] * pltpu.roll(win, j, axis=0)[PAD:, :]
            o_ref[pl.ds(base, CONV_ROWS), :] = pre * _sigmoid(pre)
            return carry

        lax.fori_loop(0, n_steps, step, 0)

    return pl.pallas_call(
        body,
        grid=(c_dim // CONV_TILE,),
        in_specs=[pl.BlockSpec((s_dim, CONV_TILE), lambda j: (0, j)), pl.BlockSpec((8, CONV_TILE), lambda j: (0, j))],
        out_specs=pl.BlockSpec((s_dim, CONV_TILE), lambda j: (0, j)),
        out_shape=jax.ShapeDtypeStruct((s_dim, c_dim), F32),
        scratch_shapes=[pltpu.VMEM((s_dim + 2 * PAD, CONV_TILE), F32)],
        compiler_params=_cparams("parallel"),
        name="conv_fwd",
    )(xbc, w8)


def _conv_bwd(xbc, w8, dxc):
    s_dim, c_dim = xbc.shape
    n_steps = s_dim // CONV_ROWS

    def body(x_ref, w_ref, d_ref, dx_ref, dw_ref, xp, dp):
        xp[0:PAD, :] = jnp.zeros((PAD, CONV_TILE), F32)
        xp[PAD:PAD + s_dim, :] = x_ref[...]
        dp[PAD + s_dim:, :] = jnp.zeros((PAD, CONV_TILE), F32)
        taps, bias = _conv_taps(w_ref[...])

        def step1(c, sums):
            base = pl.multiple_of(c * CONV_ROWS, CONV_ROWS)
            win = xp[pl.ds(base, CONV_ROWS + PAD), :]
            shifted = [win[PAD:, :]] + [pltpu.roll(win, j, axis=0)[PAD:, :] for j in range(1, CONV_WIDTH)]
            pre = bias
            for j in range(CONV_WIDTH):
                pre = pre + taps[3 - j] * shifted[j]
            sg = _sigmoid(pre)
            dpre = d_ref[pl.ds(base, CONV_ROWS), :] * (sg * (1.0 + pre * (1.0 - sg)))
            dp[pl.ds(base + PAD, CONV_ROWS), :] = dpre
            new = [sums[k] + jnp.sum(dpre * shifted[3 - k], axis=0, keepdims=True) for k in range(CONV_WIDTH)]
            new.append(sums[CONV_WIDTH] + jnp.sum(dpre, axis=0, keepdims=True))
            return tuple(new)

        zero = jnp.zeros((1, CONV_TILE), F32)
        sums = lax.fori_loop(0, n_steps, step1, (zero,) * (CONV_WIDTH + 1))
        dw_ref[...] = jnp.zeros((8, CONV_TILE), F32)
        for k in range(CONV_WIDTH + 1):
            dw_ref[k:k + 1, :] = sums[k]

        def step2(c, carry):
            base = pl.multiple_of(c * CONV_ROWS, CONV_ROWS)
            win = dp[pl.ds(base + PAD, CONV_ROWS + PAD), :]
            dx = taps[3] * win[:CONV_ROWS, :]
            for j in range(1, CONV_WIDTH):
                dx = dx + taps[3 - j] * pltpu.roll(win, CONV_ROWS + PAD - j, axis=0)[:CONV_ROWS, :]
            dx_ref[pl.ds(base, CONV_ROWS), :] = dx.astype(BF16)
            return carry

        lax.fori_loop(0, n_steps, step2, 0)

    col = lambda j: (0, j)
    return pl.pallas_call(
        body,
        grid=(c_dim // CONV_TILE,),
        in_specs=[pl.BlockSpec((s_dim, CONV_TILE), col), pl.BlockSpec((8, CONV_TILE), col),
                  pl.BlockSpec((s_dim, CONV_TILE), col)],
        out_specs=[pl.BlockSpec((s_dim, CONV_TILE), col), pl.BlockSpec((8, CONV_TILE), col)],
        out_shape=[jax.ShapeDtypeStruct((s_dim, c_dim), BF16), jax.ShapeDtypeStruct((8, c_dim), F32)],
        scratch_shapes=[pltpu.VMEM((s_dim + 2 * PAD, CONV_TILE), F32), pltpu.VMEM((s_dim + 2 * PAD, CONV_TILE), F32)],
        compiler_params=_cparams("parallel"),
        name="conv_bwd",
    )(xbc, w8, dxc)


def _perm_cols(a):
    parts = []
    for g in range(SSM_GROUPS):
        parts += [a[..., g * GROUP_X:(g + 1) * GROUP_X],
                  a[..., D_SSM + g * D_STATE:D_SSM + (g + 1) * D_STATE],
                  a[..., D_SSM + SSM_GROUPS * D_STATE + g * D_STATE:D_SSM + SSM_GROUPS * D_STATE + (g + 1) * D_STATE]]
    return jnp.concatenate(parts, axis=-1)


def _unperm_cols(a):
    xs = [a[..., g * GROUP_COLS:g * GROUP_COLS + GROUP_X] for g in range(SSM_GROUPS)]
    bs = [a[..., g * GROUP_COLS + GROUP_X:g * GROUP_COLS + GROUP_X + D_STATE] for g in range(SSM_GROUPS)]
    cs = [a[..., g * GROUP_COLS + GROUP_X + D_STATE:(g + 1) * GROUP_COLS] for g in range(SSM_GROUPS)]
    return jnp.concatenate(xs + bs + cs, axis=-1)


def _dt_to_groups(dt):
    s_dim = dt.shape[0]
    t = dt[:, :SSM_HEADS].reshape(s_dim, SSM_GROUPS, HEADS_PER_GROUP).transpose(1, 0, 2)
    return jnp.pad(t, ((0, 0), (0, 0), (0, LANES - HEADS_PER_GROUP)))


def _dt_from_groups(dtg):
    s_dim = dtg.shape[1]
    return dtg[:, :, :HEADS_PER_GROUP].transpose(1, 0, 2).reshape(s_dim, SSM_HEADS)


def _pack_ssd_params(dt_bias, a_log, d_skip):
    rows = jnp.stack([p.reshape(SSM_GROUPS, HEADS_PER_GROUP) for p in (dt_bias, a_log, d_skip)], axis=1)
    return jnp.pad(rows, ((0, 0), (0, 8 - 3), (0, LANES - HEADS_PER_GROUP)))


def _unpack_ssd_params(par):
    return tuple(par[:, k, :HEADS_PER_GROUP].reshape(SSM_HEADS) for k in range(3))


Q = SSD_CHUNK


def _split3(v):
    hi = v.astype(BF16)
    r1 = v - hi.astype(F32)
    mid = r1.astype(BF16)
    lo = (r1 - mid.astype(F32)).astype(BF16)
    return hi, mid, lo


def _dot_l01(t01, v):
    return sum(_dot(t01, p) for p in _split3(v))


def _dot_r01(v, e01):
    return sum(_dot(p, e01) for p in _split3(v))


def _ssd_consts():
    row = lax.broadcasted_iota(jnp.int32, (Q, Q), 0)
    col = lax.broadcasted_iota(jnp.int32, (Q, Q), 1)
    causal = row >= col
    tril = causal.astype(BF16)
    triu = (col >= row).astype(BF16)
    er = lax.broadcasted_iota(jnp.int32, (LANES, GROUP_X), 0)
    ec = lax.broadcasted_iota(jnp.int32, (LANES, GROUP_X), 1) // SSM_HEAD_DIM
    expand = (er == ec).astype(BF16)
    rr = lax.broadcasted_iota(jnp.int32, (GROUP_X, LANES), 0) // SSM_HEAD_DIM
    rc = lax.broadcasted_iota(jnp.int32, (GROUP_X, LANES), 1)
    reduce = (rr == rc).astype(BF16)
    lane_head = lax.broadcasted_iota(jnp.int32, (Q, GROUP_X), 1) // SSM_HEAD_DIM
    return causal, tril, triu, expand, reduce, lane_head


def _ssd_common(xc_ref, dt_ref, par_ref, consts):
    causal, tril, _, expand, _, _ = consts
    par = par_ref[...]
    bias, alog, dsk = par[0:1, :], par[1:2, :], par[2:3, :]
    a_neg = -jnp.exp(alog)
    dtr = dt_ref[...] + bias
    dt = _softplus(dtr)
    s = _dot_l01(tril, dt * a_neg)
    dt_x = _dot_r01(dt, expand)
    s_x = _dot_r01(s, expand)
    dsk_x = _dot_r01(jnp.broadcast_to(dsk, (8, LANES)), expand)[0:1, :]
    blk = xc_ref[...]
    x = blk[:, :GROUP_X]
    bm = blk[:, GROUP_X:GROUP_X + D_STATE].astype(BF16)
    cm = blk[:, GROUP_X + D_STATE:].astype(BF16)
    xdt = x * dt_x
    g = _dot(cm, bm, _NT)
    return dict(a_neg=a_neg, dtr=dtr, dt=dt, s=s, s_t=s.T, dt_x=dt_x, s_x=s_x, dsk_x=dsk_x, x=x, bm=bm, cm=cm,
                xdt=xdt, g=g)


def _decay(v, r, causal):
    diff = v["s"][:, r:r + 1] - v["s_t"][r:r + 1, :]
    return jnp.exp(jnp.where(causal, diff, -jnp.inf))


def _ssd_specs(n_chunks, rev):
    cidx = (lambda c: n_chunks - 1 - c) if rev else (lambda c: c)
    xc = pl.BlockSpec((Q, GROUP_COLS), lambda g, c: (cidx(c), g))
    gx = pl.BlockSpec((Q, GROUP_X), lambda g, c: (cidx(c), g))
    dt = pl.BlockSpec((None, Q, LANES), lambda g, c: (g, cidx(c), 0))
    par = pl.BlockSpec((None, 8, LANES), lambda g, c: (g, 0, 0))
    nw = pl.BlockSpec((1, GROUP_X), lambda g, c: (0, g))
    hs = pl.BlockSpec((None, None, D_STATE, GROUP_X), lambda g, c: (cidx(c), g, 0, 0))
    return xc, gx, dt, par, nw, hs


def _ssd_fwd(xc, z, dtg, par, nw):
    s_dim = xc.shape[0]
    n_chunks = s_dim // Q
    xc_s, gx_s, dt_s, par_s, nw_s, hs_s = _ssd_specs(n_chunks, False)

    def body(xc_ref, z_ref, dt_ref, par_ref, nw_ref, y_ref, ys_ref, hs_ref, ht):
        @pl.when(pl.program_id(1) == 0)
        def _():
            ht[...] = jnp.zeros_like(ht)

        consts = _ssd_consts()
        causal, lane_head = consts[0], consts[5]
        v = _ssd_common(xc_ref, dt_ref, par_ref, consts)
        xdt_b = v["xdt"].astype(BF16)
        yd = jnp.zeros((Q, GROUP_X), F32)
        for r in range(HEADS_PER_GROUP):
            m = (v["g"] * _decay(v, r, causal)).astype(BF16)
            yd = yd + _dot(m, jnp.where(lane_head == r, xdt_b, jnp.zeros_like(xdt_b)))
        h = ht[...]
        hs_ref[...] = h
        yo = jnp.exp(v["s_x"]) * _dot(v["cm"], h.astype(BF16))
        y = yd + yo + v["dsk_x"] * v["x"]
        s_last = v["s_x"][Q - 1:Q, :]
        snew = _dot(v["bm"], (v["xdt"] * jnp.exp(s_last - v["s_x"])).astype(BF16), _TN)
        ht[...] = jnp.exp(s_last) * h + snew
        zz = z_ref[...]
        yg = y * (zz * _sigmoid(zz))
        y_ref[...] = y
        ys_ref[...] = _nrm(yg, nw_ref[...])[0].astype(BF16)

    return pl.pallas_call(
        body,
        grid=(SSM_GROUPS, n_chunks),
        in_specs=[xc_s, gx_s, dt_s, par_s, nw_s],
        out_specs=[gx_s, gx_s, hs_s],
        out_shape=[jax.ShapeDtypeStruct((s_dim, D_SSM), F32), jax.ShapeDtypeStruct((s_dim, D_SSM), BF16),
                   jax.ShapeDtypeStruct((n_chunks, SSM_GROUPS, D_STATE, GROUP_X), F32)],
        scratch_shapes=[pltpu.VMEM((D_STATE, GROUP_X), F32)],
        compiler_params=_cparams("parallel", "arbitrary"),
        name="ssd_fwd",
    )(xc, z, dtg, par, nw)


def _ssd_bwd(xc, z, dtg, par, nw, y, hs, dymix):
    s_dim = xc.shape[0]
    n_chunks = s_dim // Q
    xc_s, gx_s, dt_s, par_s, nw_s, hs_s = _ssd_specs(n_chunks, True)

    def body(xc_ref, z_ref, dt_ref, par_ref, nw_ref, y_ref, hs_ref, dys_ref,
             dxc_ref, dz_ref, ddt_ref, dpar_ref, dnw_ref, dht):
        @pl.when(pl.program_id(1) == 0)
        def _():
            dht[...] = jnp.zeros_like(dht)
            dpar_ref[...] = jnp.zeros_like(dpar_ref)
            dnw_ref[...] = jnp.zeros_like(dnw_ref)

        consts = _ssd_consts()
        causal, _, triu, _, reduce, lane_head = consts
        v = _ssd_common(xc_ref, dt_ref, par_ref, consts)
        x, bm, cm, xdt, s_x = v["x"], v["bm"], v["cm"], v["xdt"], v["s_x"]
        h = hs_ref[...]
        hb = h.astype(BF16)
        es_x = jnp.exp(s_x)
        yo = es_x * _dot(cm, hb)
        s_last = s_x[Q - 1:Q, :]
        e_x = jnp.exp(s_last - s_x)
        es_last = jnp.exp(s_last)

        yv, zz, nw_v = y_ref[...], z_ref[...], nw_ref[...]
        sg = _sigmoid(zz)
        gz = zz * sg
        _, n, rstd = _nrm(yv * gz, nw_v)
        dout = dys_ref[...]
        dyg, dnw = _nrm_bwd(dout, n, rstd, nw_v)
        dnw_ref[...] += dnw
        dy = dyg * gz
        dz_ref[...] = (dyg * yv * (sg * (1.0 + zz * (1.0 - sg)))).astype(BF16)

        dyb = dy.astype(BF16)
        xdt_b = xdt.astype(BF16)
        dhp = dht[...]
        dhpb = dhp.astype(BF16)
        lane = lax.broadcasted_iota(jnp.int32, (Q, LANES), 1)
        sub = lax.broadcasted_iota(jnp.int32, (LANES, Q), 0)
        dxdt = jnp.zeros((Q, GROUP_X), F32)
        dg = jnp.zeros((Q, Q), F32)
        ds = jnp.zeros((Q, LANES), F32)
        ds_t = jnp.zeros((LANES, Q), F32)
        for r in range(HEADS_PER_GROUP):
            dec = _decay(v, r, causal)
            mf = v["g"] * dec
            dyr = jnp.where(lane_head == r, dyb, jnp.zeros_like(dyb))
            dm = _dot(dyr, xdt_b, _NT)
            dxdt = dxdt + _dot(mf.astype(BF16), dyr, _TN)
            dg = dg + dm * dec
            dd = dm * mf
            ds = ds + jnp.where(lane == r, jnp.sum(dd, axis=1, keepdims=True), 0.0)
            ds_t = ds_t + jnp.where(sub == r, jnp.sum(dd, axis=0, keepdims=True), 0.0)
        ds = ds - ds_t.T
        dgb = dg.astype(BF16)
        dwb = (es_x * dy).astype(BF16)
        dcm = _dot(dgb, bm) + _dot(dwb, hb, _NT)
        dh_prev = _dot(cm, dwb, _TN)
        zst = _dot(bm, dhpb)
        xe = xdt * e_x
        dxdt = dxdt + e_x * zst
        dee = xe * zst
        dbm = _dot(dgb, cm, _TN) + _dot(xe.astype(BF16), dhpb, _NT)
        v_last = jnp.sum(dee, axis=0, keepdims=True) + es_last * jnp.sum(dhp * h, axis=0, keepdims=True)
        row_x = lax.broadcasted_iota(jnp.int32, (Q, GROUP_X), 0)
        tx = dy * yo - dee + jnp.where(row_x == Q - 1, v_last, 0.0)
        ds = ds + _dot_r01(tx, reduce)
        ddta = _dot_l01(triu, ds)
        ddt = ddta * v["a_neg"] + _dot_r01(dxdt * x, reduce)
        dalog = jnp.sum(ddta * v["dt"], axis=0, keepdims=True) * v["a_neg"]
        draw = jnp.where(lane < HEADS_PER_GROUP, ddt * _sigmoid(v["dtr"]), 0.0)
        dbias = jnp.sum(draw, axis=0, keepdims=True)
        ddsk = _dot_r01(jnp.broadcast_to(jnp.sum(dy * x, axis=0, keepdims=True), (8, GROUP_X)), reduce)[0:1, :]
        dht[...] = es_last * dhp + dh_prev
        dxc_ref[:, :GROUP_X] = dxdt * v["dt_x"] + v["dsk_x"] * dy
        dxc_ref[:, GROUP_X:GROUP_X + D_STATE] = dbm
        dxc_ref[:, GROUP_X + D_STATE:] = dcm
        ddt_ref[...] = draw
        dpar_ref[0:1, :] += dbias
        dpar_ref[1:2, :] += dalog
        dpar_ref[2:3, :] += ddsk

    return pl.pallas_call(
        body,
        grid=(SSM_GROUPS, n_chunks),
        in_specs=[xc_s, gx_s, dt_s, par_s, nw_s, gx_s, hs_s, gx_s],
        out_specs=[xc_s, gx_s, dt_s, par_s, nw_s],
        out_shape=[jax.ShapeDtypeStruct((s_dim, SSM_GROUPS * GROUP_COLS), F32),
                   jax.ShapeDtypeStruct((s_dim, D_SSM), BF16),
                   jax.ShapeDtypeStruct((SSM_GROUPS, s_dim, LANES), F32),
                   jax.ShapeDtypeStruct((SSM_GROUPS, 8, LANES), F32),
                   jax.ShapeDtypeStruct((1, D_SSM), F32)],
        scratch_shapes=[pltpu.VMEM((D_STATE, GROUP_X), F32)],
        compiler_params=_cparams("parallel", "arbitrary"),
        name="ssd_bwd",
    )(xc, z, dtg, par, nw, y, hs, dymix)


ATT_SCALE = ATT_HEAD_DIM ** -0.5
NEG_INF = -jnp.inf


def _band_masks():
    qi = lax.broadcasted_iota(jnp.int32, (ATT_BLOCK, ATT_BLOCK), 0)
    kj = lax.broadcasted_iota(jnp.int32, (ATT_BLOCK, ATT_BLOCK), 1)
    return kj <= qi, kj >= qi


WIN = ATT_BLOCK * DILATIONS[-1]
N_BLOCKS = WIN // ATT_BLOCK


def _rows(start, d):
    return pl.ds(start, ATT_BLOCK) if d == 1 else pl.ds(start, ATT_BLOCK, stride=d)


def _block_start(idx, d):
    return (idx // d) * (ATT_BLOCK * d) + idx % d


def _lane_bcast(col):
    return jnp.broadcast_to(col, (col.shape[0], LANES))


def _attn_fused_fwd(qkv):
    s_dim = qkv.shape[0]
    n_win = s_dim // WIN
    blk = (WIN, ATT_HEAD_DIM)
    prev = lambda w: jnp.maximum(w - 1, 0)

    def body(q_ref, kc_ref, kp_ref, vc_ref, vp_ref, y_ref, yf_ref, lse_ref, qf, kf, vf, acc, m_run, l_run):
        w, h = pl.program_id(0), pl.program_id(1)
        qf[...] = q_ref[...].astype(F32)
        kf[0:WIN, :] = kp_ref[...].astype(F32)
        kf[WIN:, :] = kc_ref[...].astype(F32)
        vf[0:WIN, :] = vp_ref[...].astype(F32)
        vf[WIN:, :] = vc_ref[...].astype(F32)
        own, before = _band_masks()

        for d in DILATIONS:
            def block(idx, carry, d=d):
                start = _block_start(idx, d)
                rows = _rows(start, d)
                q = qf[rows, :].astype(BF16)
                kc, vc = kf[_rows(WIN + start, d), :].astype(BF16), vf[_rows(WIN + start, d), :].astype(BF16)
                kp = kf[_rows(WIN + start - ATT_BLOCK * d, d), :].astype(BF16)
                vp = vf[_rows(WIN + start - ATT_BLOCK * d, d), :].astype(BF16)
                has_prev = (idx >= d) | (w > 0)
                sc = jnp.where(own, _dot(q, kc, _NT) * ATT_SCALE, NEG_INF)
                sp = jnp.where(before & has_prev, _dot(q, kp, _NT) * ATT_SCALE, NEG_INF)
                m_blk = jnp.maximum(jnp.max(sc, axis=1, keepdims=True), jnp.max(sp, axis=1, keepdims=True))
                if d == DILATIONS[0]:
                    m_new = m_blk
                else:
                    m_old = m_run[rows, :][:, 0:1]
                    m_new = jnp.maximum(m_old, m_blk)
                pc, pp = jnp.exp(sc - m_new), jnp.exp(sp - m_new)
                l_new = jnp.sum(pc, axis=1, keepdims=True) + jnp.sum(pp, axis=1, keepdims=True)
                o_new = _dot(pc.astype(BF16), vc) + _dot(pp.astype(BF16), vp)
                if d != DILATIONS[0]:
                    alpha = jnp.exp(m_old - m_new)
                    l_new = alpha * l_run[rows, :][:, 0:1] + l_new
                    o_new = alpha * acc[rows, :] + o_new
                m_run[rows, :] = _lane_bcast(m_new)
                l_run[rows, :] = _lane_bcast(l_new)
                acc[rows, :] = o_new
                return carry

            for idx in range(N_BLOCKS):
                block(idx, 0)

        l_all = l_run[...]
        y = acc[...] / l_all
        y_ref[...] = y.astype(BF16)
        yf_ref[...] = y
        @pl.when(h == 0)
        def _():
            lse_ref[...] = jnp.zeros_like(lse_ref)

        lane = lax.broadcasted_iota(jnp.int32, (WIN, LANES), 1)
        lse_ref[...] = jnp.where(lane == h, m_run[...] + jnp.log(l_all), lse_ref[...])

    win_scratch = lambda rows: pltpu.VMEM((rows, ATT_HEAD_DIM), F32)
    return pl.pallas_call(
        body,
        grid=(n_win, ATT_HEADS),
        in_specs=[pl.BlockSpec(blk, lambda w, h: (w, h)),
                  pl.BlockSpec(blk, lambda w, h: (w, ATT_HEADS + h)),
                  pl.BlockSpec(blk, lambda w, h: (prev(w), ATT_HEADS + h)),
                  pl.BlockSpec(blk, lambda w, h: (w, 2 * ATT_HEADS + h)),
                  pl.BlockSpec(blk, lambda w, h: (prev(w), 2 * ATT_HEADS + h))],
        out_specs=[pl.BlockSpec(blk, lambda w, h: (w, h)), pl.BlockSpec(blk, lambda w, h: (w, h)),
                   pl.BlockSpec((WIN, LANES), lambda w, h: (w, 0))],
        out_shape=[jax.ShapeDtypeStruct((s_dim, D_ATT), BF16), jax.ShapeDtypeStruct((s_dim, D_ATT), F32),
                   jax.ShapeDtypeStruct((s_dim, LANES), F32)],
        scratch_shapes=[win_scratch(WIN), win_scratch(2 * WIN), win_scratch(2 * WIN), win_scratch(WIN),
                        win_scratch(WIN), win_scratch(WIN)],
        compiler_params=_cparams("parallel", "arbitrary"),
        name="attn_fused_fwd",
    )(qkv, qkv, qkv, qkv, qkv)


def _attn_fused_bwd(qkv, dymix, y_att, lse, deps=()):
    s_dim = qkv.shape[0]
    n_win = s_dim // WIN
    blk = (WIN, ATT_HEAD_DIM)
    this = lambda w: jnp.minimum(w, n_win - 1)
    prev = lambda w: jnp.maximum(this(w) - 1, 0)
    n_dep = len(deps)

    def body(q_ref, kc_ref, kp_ref, vc_ref, vp_ref, dy_ref, y_ref, l_ref, *rest):
        dq_ref, dkv_ref = rest[n_dep:n_dep + 2]
        qf, kf, vf, dq_acc, dk_acc, dv_acc, ls_c, dl_c = rest[n_dep + 2:]
        h, w = pl.program_id(0), pl.program_id(1)
        slot, late = w % 2, 1 - w % 2

        @pl.when(w == 0)
        def _():
            dk_acc[...] = jnp.zeros_like(dk_acc)
            dv_acc[...] = jnp.zeros_like(dv_acc)

        @pl.when(w < n_win)
        def _():
            qf[...] = q_ref[...].astype(F32)
            kf[0:WIN, :] = kp_ref[...].astype(F32)
            kf[WIN:, :] = kc_ref[...].astype(F32)
            vf[0:WIN, :] = vp_ref[...].astype(F32)
            vf[WIN:, :] = vc_ref[...].astype(F32)
            lane = lax.broadcasted_iota(jnp.int32, (WIN, LANES), 1)
            ls_c[...] = _lane_bcast(jnp.sum(jnp.where(lane == h, l_ref[...], 0.0), axis=1, keepdims=True))
            dl_c[...] = _lane_bcast(jnp.sum(dy_ref[...] * y_ref[...], axis=1, keepdims=True))
            dq_acc[...] = jnp.zeros_like(dq_acc)
            dk_acc[slot] = jnp.zeros((WIN, ATT_HEAD_DIM), F32)
            dv_acc[slot] = jnp.zeros((WIN, ATT_HEAD_DIM), F32)
            own, before = _band_masks()

            def probs(q, k, v, dy, lse_col, dl_col, mask):
                p = jnp.exp(jnp.where(mask, _dot(q, k, _NT) * ATT_SCALE - lse_col, NEG_INF))
                ds = p * (_dot(dy, v, _NT) - dl_col)
                return p.astype(BF16), ds.astype(BF16)

            for d in DILATIONS:
                for idx in range(N_BLOCKS):
                    start = _block_start(idx, d)
                    rows = _rows(start, d)
                    q, dy = qf[rows, :].astype(BF16), dy_ref[rows, :].astype(BF16)
                    lse_col, dl_col = ls_c[rows, :][:, 0:1], dl_c[rows, :][:, 0:1]
                    kc, vc = kf[_rows(WIN + start, d), :].astype(BF16), vf[_rows(WIN + start, d), :].astype(BF16)
                    kp = kf[_rows(WIN + start - ATT_BLOCK * d, d), :].astype(BF16)
                    vp = vf[_rows(WIN + start - ATT_BLOCK * d, d), :].astype(BF16)
                    pc, dsc = probs(q, kc, vc, dy, lse_col, dl_col, own)
                    pp, dsp = probs(q, kp, vp, dy, lse_col, dl_col, before & ((idx >= d) | (w > 0)))
                    dq_acc[rows, :] += (_dot(dsc, kc) + _dot(dsp, kp)) * ATT_SCALE
                    dk_acc[slot, rows, :] += _dot(dsc, q, _TN) * ATT_SCALE
                    dv_acc[slot, rows, :] += _dot(pc, dy, _TN)
                    if idx >= d:
                        prows = _rows(start - ATT_BLOCK * d, d)
                        dk_acc[slot, prows, :] += _dot(dsp, q, _TN) * ATT_SCALE
                        dv_acc[slot, prows, :] += _dot(pp, dy, _TN)
                    else:
                        prows = _rows(WIN + start - ATT_BLOCK * d, d)
                        dk_acc[late, prows, :] += _dot(dsp, q, _TN) * ATT_SCALE
                        dv_acc[late, prows, :] += _dot(pp, dy, _TN)
            dq_ref[...] = dq_acc[...].astype(BF16)

        @pl.when(w > 0)
        def _():
            dkv_ref[0] = dk_acc[late].astype(BF16)
            dkv_ref[1] = dv_acc[late].astype(BF16)

    win_scratch = lambda *shape: pltpu.VMEM(shape + (ATT_HEAD_DIM,), F32)
    cur = lambda c: pl.BlockSpec(blk, lambda h, w: (this(w), c + h))
    before_spec = lambda c: pl.BlockSpec(blk, lambda h, w: (prev(w), c + h))
    return pl.pallas_call(
        body,
        grid=(ATT_HEADS, n_win + 1),
        in_specs=[cur(0), cur(ATT_HEADS), before_spec(ATT_HEADS), cur(2 * ATT_HEADS), before_spec(2 * ATT_HEADS),
                  cur(ATT_HEADS), cur(0), pl.BlockSpec((WIN, LANES), lambda h, w: (this(w), 0))] + [ANY] * n_dep,
        out_specs=[cur(0), pl.BlockSpec((2, WIN, ATT_HEAD_DIM), lambda h, w: (0, jnp.maximum(w - 1, 0), h))],
        out_shape=[jax.ShapeDtypeStruct((s_dim, D_ATT), BF16), jax.ShapeDtypeStruct((2, s_dim, D_ATT), BF16)],
        scratch_shapes=[win_scratch(WIN), win_scratch(2 * WIN), win_scratch(2 * WIN), win_scratch(WIN),
                        win_scratch(2, WIN), win_scratch(2, WIN), win_scratch(WIN), win_scratch(WIN)],
        compiler_params=_cparams("parallel", "arbitrary"),
        name="attn_fused_bwd",
    )(qkv, qkv, qkv, qkv, qkv, dymix, y_att, lse, *deps)


def _adamw(w, g, m, v, name):
    def fn(wb, gb, mb, vb):
        m2 = ADAM_B1 * mb + (1.0 - ADAM_B1) * gb
        v2 = ADAM_B2 * vb + (1.0 - ADAM_B2) * (gb * gb)
        m_hat = m2 / (1.0 - ADAM_B1 ** ADAM_STEP)
        v_hat = v2 / (1.0 - ADAM_B2 ** ADAM_STEP)
        delta = -ADAM_LR * (m_hat / (jnp.sqrt(v_hat) + ADAM_EPS) + ADAM_WD * wb)
        return delta, m2, v2
    cols = w.shape[1]
    tr = 128 if w.shape[0] % 128 == 0 else w.shape[0]
    return _rowcall(fn, [w, g, m, v], [], [(cols, F32)] * 3, [], name=name, tr=tr)


ANY = pl.BlockSpec(memory_space=pl.ANY)


def _position():
    x, y, c = lax.axis_index("x"), lax.axis_index("y"), lax.axis_index("c")
    chips = [(1 - x, y), (x, 1 - y), (1 - x, 1 - y)]
    return x, y, c, chips


def _remote(src, dst, send_sem, recv_sem, device):
    return pltpu.make_async_remote_copy(src_ref=src, dst_ref=dst, send_sem=send_sem, recv_sem=recv_sem,
                                        device_id=device, device_id_type=MESH)


def _handshake(peers):
    barrier = pltpu.get_barrier_semaphore()
    for p in peers:
        pl.semaphore_signal(barrier, inc=1, device_id=p, device_id_type=MESH)
    pl.semaphore_wait(barrier, len(peers))


def _gather_shards_async(shards, collective_id, name):
    n = len(shards)
    srcs = [jax.new_ref(s, memory_space=pltpu.MemorySpace.HBM) for s in shards]
    dsts = [jax.empty_ref(jax.ShapeDtypeStruct((N_CHIPS,) + s.shape, s.dtype), memory_space=pltpu.MemorySpace.HBM)
            for s in shards]

    @pl.kernel(mesh=plsc.ScalarSubcoreMesh(axis_name="seq", num_cores=1), name=name,
               scratch_types=(pltpu.SemaphoreType.DMA((6 * n,)), pltpu.SemaphoreType.DMA((6 * n,))),
               compiler_params=pltpu.CompilerParams(collective_id=collective_id))
    def launch(send_sems, recv_sems):
        x, y, c, chips = _position()
        sibling = (x, y, 1 - c)
        _handshake([(chip[0], chip[1], c) for chip in chips] + [sibling])

        def half(a, j, cc):
            h = shards[a].shape[0] // 2
            return dsts[a].at[j, pl.ds(cc * h, h), :]

        sent = []
        for a in range(n):
            h = shards[a].shape[0] // 2
            for j, chip in enumerate(chips):
                cp = _remote(srcs[a].at[pl.ds(c * h, h), :], half(a, 2 * x + y, c), send_sems.at[6 * a + j],
                             recv_sems.at[6 * a + j], (chip[0], chip[1], c))
                cp.start()
                sent.append(cp)
        for a in range(n):
            for j, chip in enumerate(chips):
                landed = half(a, 2 * chip[0] + chip[1], c)
                _remote(landed, landed, send_sems.at[6 * a + j], recv_sems.at[6 * a + j], (x, y, c)).wait_recv()
                cp = _remote(landed, landed, send_sems.at[6 * a + 3 + j], recv_sems.at[6 * a + 3 + j], sibling)
                cp.start()
                sent.append(cp)
        for a in range(n):
            for j, chip in enumerate(chips):
                handed = half(a, 2 * chip[0] + chip[1], 1 - c)
                _remote(handed, handed, send_sems.at[6 * a + 3 + j], recv_sems.at[6 * a + 3 + j], (x, y, c)).wait_recv()
        for cp in sent:
            cp.wait_send()

    launch()
    return [d[...] for d in dsts]


IN_COLS = {"z": (0, D_SSM), "xbc": (D_SSM, D_SSM + D_XBC), "dt": (D_SSM + D_XBC, D_SSM + D_XBC + SSM_HEADS),
           "qkv": (D_SSM + D_XBC + SSM_HEADS, D_IN_PROJ)}


def _cols_from_quarters(quarters, lo, hi):
    parts = []
    for q in range(N_CHIPS):
        a, b = max(lo, q * W_IN_SHARD), min(hi, (q + 1) * W_IN_SHARD)
        if a < b:
            parts.append(quarters[q][:, a - q * W_IN_SHARD:b - q * W_IN_SHARD])
    return parts[0] if len(parts) == 1 else jnp.concatenate(parts, axis=1)


def _quarters_from_cols(pieces):
    quarters = []
    for q in range(N_CHIPS):
        parts = []
        for name, (lo, hi) in IN_COLS.items():
            a, b = max(lo, q * W_IN_SHARD), min(hi, (q + 1) * W_IN_SHARD)
            if a < b:
                parts.append(pieces[name][:, a - lo:b - lo])
        quarters.append(jnp.concatenate(parts, axis=1))
    return jnp.stack(quarters)


def _by_chip(own, fetched):
    me = 2 * lax.axis_index("x") + lax.axis_index("y")
    return lax.dynamic_update_slice(fetched, own[None], (me, 0, 0))


def _add_sibling(grad, got, place, name, deps=()):
    nq, rows, cols = grad.shape
    h = rows // 2
    tr = 128
    nb = h // tr

    def body(place_ref, a_ref, b_ref, *rest):
        own_ref, ob_ref = rest[len(deps):]
        total = a_ref[...] + b_ref[...]
        ob_ref[...] = total.astype(BF16)

        @pl.when(pl.program_id(1) == place_ref[1])
        def _():
            own_ref[...] = total

    return pl.pallas_call(
        body,
        grid_spec=pltpu.PrefetchScalarGridSpec(
            num_scalar_prefetch=1, grid=(nb, nq),
            in_specs=[pl.BlockSpec((None, tr, cols), lambda i, q, p: (q, p[0] * nb + i, 0)),
                      pl.BlockSpec((None, tr, cols), lambda i, q, p: (q, i, 0))] + [ANY] * len(deps),
            out_specs=[pl.BlockSpec((tr, cols), lambda i, q, p: (i, 0)),
                       pl.BlockSpec((None, tr, cols), lambda i, q, p: (q, i, 0))]),
        out_shape=[jax.ShapeDtypeStruct((h, cols), F32), jax.ShapeDtypeStruct((nq, h, cols), BF16)],
        compiler_params=_cparams("parallel", "arbitrary"),
        name=name,
    )(place, grad, got, *deps)


def _add_chips(part, got, name, deps=()):
    h, cols = part.shape
    tr = 128

    def body(p_ref, g0_ref, g1_ref, g2_ref, *rest):
        o_ref = rest[len(deps)]
        o_ref[...] = ((p_ref[...] + g0_ref[...].astype(F32)) + g1_ref[...].astype(F32)) + g2_ref[...].astype(F32)

    got_spec = lambda j: pl.BlockSpec((None, tr, cols), lambda i: (j, i, 0))
    row_spec = pl.BlockSpec((tr, cols), lambda i: (i, 0))
    return pl.pallas_call(
        body,
        grid=(h // tr,),
        in_specs=[row_spec, got_spec(0), got_spec(1), got_spec(2)] + [ANY] * len(deps),
        out_specs=row_spec,
        out_shape=jax.ShapeDtypeStruct((h, cols), F32),
        compiler_params=_cparams("parallel"),
        name=name,
    )(part, got, got, got, *deps)


def _sequencer_exchange(src, out_shape, collective_id, name, plan, n_copies):
    src_ref = jax.new_ref(src, memory_space=pltpu.MemorySpace.HBM)
    dst_ref = jax.empty_ref(out_shape, memory_space=pltpu.MemorySpace.HBM)

    @pl.kernel(mesh=plsc.ScalarSubcoreMesh(axis_name="seq", num_cores=1), name=name,
               scratch_types=(pltpu.SemaphoreType.DMA((n_copies,)), pltpu.SemaphoreType.DMA((n_copies,))),
               compiler_params=pltpu.CompilerParams(collective_id=collective_id))
    def launch(send_sems, recv_sems):
        x, y, c, chips = _position()
        copies = plan(src_ref, dst_ref, x, y, c, chips)
        _handshake([peer for _, _, peer in copies])
        started = []
        for k, (s, d, peer) in enumerate(copies):
            cp = _remote(s, d, send_sems.at[k], recv_sems.at[k], peer)
            cp.start()
            started.append(cp)
        for cp in started:
            cp.wait()

    launch()
    return dst_ref[...]


class _AsyncReduceScatter:
    def __init__(self, grad, nm, first_id):
        self.grad, self.nm, self.first_id = grad, nm, first_id
        nq, rows, cols = grad.shape
        h = self.h = rows // 2

        def to_sibling(s, d, x, y, c, chips):
            return [(s.at[:, pl.ds((1 - c) * h, h), :], d, (x, y, 1 - c))]

        self.from_sibling = _sequencer_exchange(grad, jax.ShapeDtypeStruct((nq, h, cols), F32), first_id,
                                                f"rs_sibling_{nm}", to_sibling, 1)

    def sibling_sum(self, not_before=()):
        cols = self.grad.shape[2]
        place = jnp.stack([lax.axis_index("c"), 2 * lax.axis_index("x") + lax.axis_index("y")]).astype(jnp.int32)
        self.part, self.part_b = _add_sibling(self.grad, self.from_sibling, place, f"add_sibling_{self.nm}", not_before)

        def to_chips(s, d, x, y, c, chips):
            return [(s.at[2 * chip[0] + chip[1]], d.at[j], (chip[0], chip[1], c)) for j, chip in enumerate(chips)]

        self.from_chips = _sequencer_exchange(self.part_b, jax.ShapeDtypeStruct((3, self.h, cols), BF16),
                                              self.first_id + 1, f"rs_quarters_{self.nm}", to_chips, 3)
        return self.part_b

    def chip_sum(self, not_before=()):
        cols = self.grad.shape[2]
        self.half = _add_chips(self.part, self.from_chips, f"add_chips_{self.nm}", not_before)

        def whole_to_sibling(s, d, x, y, c, chips):
            return [(s, d, (x, y, 1 - c))]

        self.other = _sequencer_exchange(self.half, jax.ShapeDtypeStruct((self.h, cols), F32), self.first_id + 2,
                                         f"rs_share_{self.nm}", whole_to_sibling, 1)
        return self.half

    def share(self):
        return self.half, self.other


def _after(x, deps, name):
    def body(x_ref, *rest):
        rest[-1][...] = x_ref[...]

    vm = pl.BlockSpec(memory_space=pltpu.VMEM)
    return pl.pallas_call(body, in_specs=[vm] + [ANY] * len(deps), out_specs=vm,
                          out_shape=jax.ShapeDtypeStruct(x.shape, x.dtype), name=name)(x, *deps)


def _adamw_halves(w, mine, other, m, v, name):
    rows, cols = w.shape
    tr = 128
    nb = rows // 2 // tr
    c_arr = lax.axis_index("c").astype(jnp.int32).reshape(1)

    def body(c_ref, w_ref, a_ref, b_ref, m_ref, v_ref, g_out, d_out, m_out, v_out):
        is_mine = (pl.program_id(0) // nb) == c_ref[0]
        g = jnp.where(is_mine, a_ref[...], b_ref[...])
        wb, mb, vb = w_ref[...], m_ref[...], v_ref[...]
        m2 = ADAM_B1 * mb + (1.0 - ADAM_B1) * g
        v2 = ADAM_B2 * vb + (1.0 - ADAM_B2) * (g * g)
        m_hat = m2 / (1.0 - ADAM_B1 ** ADAM_STEP)
        v_hat = v2 / (1.0 - ADAM_B2 ** ADAM_STEP)
        g_out[...] = g
        d_out[...] = -ADAM_LR * (m_hat / (jnp.sqrt(v_hat) + ADAM_EPS) + ADAM_WD * wb)
        m_out[...] = m2
        v_out[...] = v2

    full = pl.BlockSpec((tr, cols), lambda i, c: (i, 0))
    half = pl.BlockSpec((tr, cols), lambda i, c: (i % nb, 0))
    return pl.pallas_call(
        body,
        grid_spec=pltpu.PrefetchScalarGridSpec(
            num_scalar_prefetch=1, grid=(rows // tr,),
            in_specs=[full, half, half, full, full], out_specs=[full] * 4),
        out_shape=[jax.ShapeDtypeStruct((rows, cols), F32)] * 4,
        compiler_params=_cparams("parallel"),
        name=name,
    )(c_arr, w, mine, other, m, v)


def _adamw_halves_t(w_t, mine_t, other_t, m_t, v_t, name):
    cols, rows = w_t.shape
    tr = cols // 11
    assert tr * 11 == cols and tr % 8 == 0
    c_arr = lax.axis_index("c").astype(jnp.int32).reshape(1)

    def body(c_ref, w_ref, a_ref, b_ref, m_ref, v_ref, g_out, d_out, m_out, v_out):
        first = c_ref[0] == 0
        a, b = a_ref[...], b_ref[...]
        g = jnp.concatenate([jnp.where(first, a, b), jnp.where(first, b, a)], axis=1)
        wb, mb, vb = w_ref[...], m_ref[...], v_ref[...]
        m2 = ADAM_B1 * mb + (1.0 - ADAM_B1) * g
        v2 = ADAM_B2 * vb + (1.0 - ADAM_B2) * (g * g)
        m_hat = m2 / (1.0 - ADAM_B1 ** ADAM_STEP)
        v_hat = v2 / (1.0 - ADAM_B2 ** ADAM_STEP)
        g_out[...] = g
        d_out[...] = -ADAM_LR * (m_hat / (jnp.sqrt(v_hat) + ADAM_EPS) + ADAM_WD * wb)
        m_out[...] = m2
        v_out[...] = v2

    full = pl.BlockSpec((tr, rows), lambda i, c: (i, 0))
    half = pl.BlockSpec((tr, rows // 2), lambda i, c: (i, 0))
    return pl.pallas_call(
        body,
        grid_spec=pltpu.PrefetchScalarGridSpec(
            num_scalar_prefetch=1, grid=(cols // tr,),
            in_specs=[full, half, half, full, full], out_specs=[full] * 4),
        out_shape=[jax.ShapeDtypeStruct((cols, rows), F32)] * 4,
        compiler_params=_cparams("parallel"),
        name=name,
    )(c_arr, w_t, mine_t, other_t, m_t, v_t)


def _all_sum_small(v):
    n_dev = 8

    def body(v_ref, o_ref, gath, send_sems, recv_sems):
        x, y, c, _ = _position()
        me = 4 * x + 2 * y + c
        gath[me] = v_ref[...]
        copies = []
        for k in range(1, n_dev):
            peer = tuple(1 - p if (k >> s) & 1 else p for p, s in ((x, 2), (y, 1), (c, 0)))
            cp = _remote(v_ref, gath.at[me], send_sems.at[k - 1], recv_sems.at[k - 1], peer)
            cp.start()
            copies.append(cp)
        for cp in copies:
            cp.wait()
        acc = gath[0]
        for i in range(1, n_dev):
            acc = acc + gath[i]
        o_ref[...] = acc

    vm = pl.BlockSpec(memory_space=pltpu.VMEM)
    return pl.pallas_call(
        body,
        in_specs=[vm],
        out_specs=vm,
        out_shape=jax.ShapeDtypeStruct(v.shape, F32),
        scratch_shapes=[pltpu.VMEM((n_dev,) + v.shape, F32), pltpu.SemaphoreType.DMA((n_dev - 1,)),
                        pltpu.SemaphoreType.DMA((n_dev - 1,))],
        name="all_sum_small",
    )(v)


def _pack_rows(vectors):
    rows = []
    for v in vectors:
        flat = v.reshape(-1).astype(F32)
        rows.append(jnp.pad(flat, (0, (-flat.shape[0]) % LANES)).reshape(-1, LANES))
    out = jnp.concatenate(rows, axis=0)
    return jnp.pad(out, ((0, (-out.shape[0]) % 8), (0, 0)))


def _unpack_rows(packed, shapes):
    outs, r = [], 0
    for shp in shapes:
        size = math.prod(shp)
        nr = -(-size // LANES)
        outs.append(packed[r:r + nr].reshape(-1)[:size].reshape(shp))
        r += nr
    return outs


def _relu_sq(acc):
    r = jnp.maximum(acc, 0.0)
    return r, r * r


def _relu_sq_bwd(acc, r):
    return (acc * (2.0 * r.astype(F32)),)


def kernel(x, norm_mix_pre, w_in, conv_w, conv_b, dt_bias, a_log, d_skip, ssm_norm_w, w_out, norm_mix_post, norm_mlp_pre, w_up, w_down, norm_mlp_post, loss_target, m_norm_mix_pre, m_w_in, m_conv_w, m_conv_b, m_dt_bias, m_a_log, m_d_skip, m_ssm_norm_w, m_w_out, m_norm_mix_post, m_norm_mlp_pre, m_w_up, m_w_down, m_norm_mlp_post, v_norm_mix_pre, v_w_in, v_conv_w, v_conv_b, v_dt_bias, v_a_log, v_d_skip, v_ssm_norm_w, v_w_out, v_norm_mix_post, v_norm_mlp_pre, v_w_up, v_w_down, v_norm_mlp_post):
    s_dim = x.shape[1]
    xs, target = x[0], loss_target[0]
    chip = 2 * lax.axis_index("x") + lax.axis_index("y")

    own = [w_in[0].astype(BF16), w_out[0].astype(BF16), w_up[0].astype(BF16), w_down[0].astype(BF16)]
    fetched_in = _gather_shards_async(own[:1], 14, "gather_w_in")[0]
    conv_cols = D_XBC // N_CHIPS
    conv_placed = lax.dynamic_update_slice(jnp.zeros((8, D_XBC), F32), 0.5 * conv_w[0], (0, chip * conv_cols))
    conv_full = _all_sum_small(conv_placed.reshape(-1, LANES)).reshape(8, D_XBC)
    w8 = _perm_cols(conv_full.at[CONV_WIDTH].set(conv_b[0]))
    u = _pre_norm(xs, norm_mix_pre)
    fetched_in, u, w8, *rest = lax.optimization_barrier((fetched_in, u, w8, *own[1:]))
    fetched = [fetched_in] + _gather_shards_async(rest, 1, "gather_rest")
    g_in, g_out, g_up, g_down = [_by_chip(o, f) for o, f in zip(own, fetched)]
    w_z = _cols_from_quarters(g_in, *IN_COLS["z"])
    w_xbc = _perm_cols(_cols_from_quarters(g_in, *IN_COLS["xbc"]))
    w_dt = jnp.pad(_cols_from_quarters(g_in, *IN_COLS["dt"]), ((0, 0), (0, LANES - SSM_HEADS)))
    w_qkv = _cols_from_quarters(g_in, *IN_COLS["qkv"])
    w_out_full = g_out.reshape(D_MIX, D_MODEL)
    w_down_full = g_down.reshape(D_FF, D_MODEL)

    z = _matmul([(u, w_z, TK)], "nn", [F32], name="proj_z")
    xbc = _matmul([(u, w_xbc, TK)], "nn", [F32], name="proj_xbc")
    dt_raw = _matmul([(u, w_dt, TK)], "nn", [F32], name="proj_dt")
    qkv = _matmul([(u, w_qkv, TK)], "nn", [BF16], name="proj_qkv")
    xc = _conv_fwd(xbc, w8)
    dtg = _dt_to_groups(dt_raw)
    par = _pack_ssd_params(dt_bias[0], a_log[0], d_skip[0])
    y, y_ssm, states = _ssd_fwd(xc, z, dtg, par, ssm_norm_w)
    y_att, y_att_f32, lse = _attn_fused_fwd(qkv)
    y_mix = jnp.concatenate([y_ssm, y_att], axis=1)
    mix = _matmul([(y_mix, w_out_full, TK)], "nn", [F32], name="out_proj")
    h1, u2 = _post_pre_norm(xs, mix, norm_mix_post, norm_mlp_pre)
    hid, act = _matmul([(u2, g_up, TK)], "nn", [BF16, BF16], name="mlp_up", epilogue=_relu_sq)
    ff = _matmul([(act, w_down_full, TK)], "nn", [F32], name="mlp_down")
    dh2, dff, d_g4, loss_part = _tail(ff, h1, target, norm_mlp_post)

    dhid = _matmul([(dff, w_down_full, TK)], "nt", [BF16], name="mlp_down_dx", epilogue=_relu_sq_bwd, extras=[hid])
    weights = {"norm_mix_pre": (norm_mix_pre, m_norm_mix_pre, v_norm_mix_pre), "w_in": (w_in, m_w_in, v_w_in),
               "conv_w": (conv_w, m_conv_w, v_conv_w), "conv_b": (conv_b, m_conv_b, v_conv_b),
               "dt_bias": (dt_bias, m_dt_bias, v_dt_bias), "a_log": (a_log, m_a_log, v_a_log),
               "d_skip": (d_skip, m_d_skip, v_d_skip), "ssm_norm_w": (ssm_norm_w, m_ssm_norm_w, v_ssm_norm_w),
               "w_out": (w_out, m_w_out, v_w_out), "norm_mix_post": (norm_mix_post, m_norm_mix_post, v_norm_mix_post),
               "norm_mlp_pre": (norm_mlp_pre, m_norm_mlp_pre, v_norm_mlp_pre), "w_up": (w_up, m_w_up, v_w_up),
               "w_down": (w_down, m_w_down, v_w_down),
               "norm_mlp_post": (norm_mlp_post, m_norm_mlp_post, v_norm_mlp_post)}
    grads, delta, new_m, new_v = {}, {}, {}, {}

    def adamw_big(n, halves):
        w, m, v = weights[n]
        g_, d_, m_, v_ = _adamw_halves(w[0], halves[0], halves[1], m[0], v[0], f"adamw_{n}")
        grads[n], delta[n], new_m[n], new_v[n] = g_[None], d_[None], m_[None], v_[None]

    dw_down = _matmul([(act, dff, TK)], "tn", [F32], name="mlp_down_dw")
    rs_down = _AsyncReduceScatter(dw_down.reshape(N_CHIPS, D_FF // N_CHIPS, D_MODEL), "w_down", 11)
    dw_up = _matmul([(u2, dhid, TK)], "tn", [F32], name="mlp_up_dw", deps=[dw_down], out_quarters=True)
    rs_up = _AsyncReduceScatter(dw_up, "w_up", 8)
    du2 = _matmul([(dhid, g_up, TK)], "nt", [F32], name="mlp_up_dx",
                  deps=[rs_down.sibling_sum(not_before=[dw_up])])
    dh1, dmix, d_g3, d_g2 = _mid_bwd(du2, h1, dh2, mix, norm_mix_post, norm_mlp_pre,
                                     deps=[rs_up.sibling_sum(not_before=[du2])])
    dymix = _matmul([(dmix, w_out_full, TK)], "nt", [F32], name="out_proj_dx")
    dw_out = _matmul([(y_mix, dmix, TK)], "tn", [F32], name="out_proj_dw")
    rs_out = _AsyncReduceScatter(dw_out.reshape(N_CHIPS, D_MIX // N_CHIPS, D_MODEL), "w_out", 5)
    dq, dkv = _attn_fused_bwd(qkv, dymix, y_att_f32, lse)
    dqkv = jnp.concatenate([dq[None], dkv], axis=0)
    par_late = _after(par, [rs_down.chip_sum(not_before=[dqkv]), rs_out.sibling_sum(not_before=[dymix])],
                      "after_w_down")
    dxc, dz, ddtg, dpar, d_nw = _ssd_bwd(xc, z, dtg, par_late, ssm_norm_w, y, states, dymix)
    g_down = rs_down.share()
    dxbc, dw8 = _conv_bwd(xbc, _after(w8, [*g_down, rs_up.chip_sum(not_before=[dxc])], "after_w_up"), dxc)
    ddt = jnp.pad(_dt_from_groups(ddtg), ((0, 0), (0, LANES - SSM_HEADS))).astype(BF16)
    g_up = rs_up.share()
    dw_z = _matmul([(u, dz, TK)], "tn", [F32], name="proj_z_dw")
    dw_xbc = _matmul([(u, dxbc, TK)], "tn", [F32], name="proj_xbc_dw",
                     deps=[*g_up, rs_out.chip_sum(not_before=[dxbc])])
    g_out = rs_out.share()
    dw_dt = _matmul([(u, ddt, TK)], "tn", [F32], name="proj_dt_dw")
    dw_qkv = _matmul([(u, dqkv, TK)], "tn", [F32], name="proj_qkv_dw")
    dw_in = _quarters_from_cols({"z": dw_z, "xbc": _unperm_cols(dw_xbc), "dt": dw_dt[:, :SSM_HEADS], "qkv": dw_qkv})
    rs_in = _AsyncReduceScatter(dw_in, "w_in", 2)
    adamw_big("w_down", g_down)
    adamw_big("w_up", g_up)
    rs_in.sibling_sum(not_before=[delta["w_up"]])
    du = _matmul([(dz, w_z, TK_MULTI), (dxbc, w_xbc, TK_MULTI), (dqkv, w_qkv, TK_MULTI), (ddt, w_dt, LANES)], "nt",
                 [F32], name="proj_dx", deps=[*g_out, rs_in.part_b])
    grad_x, d_g1 = _first_bwd(du, xs, dh1, norm_mix_pre)
    adamw_big("w_out", g_out)
    rs_in.chip_sum(not_before=[grad_x, delta["w_out"]])

    dconv = _unperm_cols(dw8)
    d_bias, d_alog, d_dskip = _unpack_ssd_params(dpar)
    small_shapes = [(1, D_MODEL), (CONV_WIDTH, D_XBC), (1, D_XBC), (1, SSM_HEADS), (1, SSM_HEADS), (1, SSM_HEADS),
                    (1, D_SSM), (1, D_MODEL), (1, D_MODEL), (1, D_MODEL), (1, LANES)]
    summed = _unpack_rows(
        _all_sum_small(_pack_rows([d_g1, dconv[:CONV_WIDTH], dconv[CONV_WIDTH:CONV_WIDTH + 1], d_bias, d_alog,
                                   d_dskip, d_nw, d_g2, d_g3, d_g4, loss_part])), small_shapes)
    (g_g1, g_conv_full, g_conv_b, g_bias, g_alog, g_dskip, g_nw, g_g2, g_g3, g_g4, loss_row) = summed
    loss = loss_row[0, 0]
    g_conv_w = lax.dynamic_slice(g_conv_full, (0, chip * conv_cols), (CONV_WIDTH, conv_cols))[None]

    grads.update({"norm_mix_pre": g_g1, "conv_w": g_conv_w, "conv_b": g_conv_b, "dt_bias": g_bias,
                  "a_log": g_alog, "d_skip": g_dskip, "ssm_norm_w": g_nw, "norm_mix_post": g_g2,
                  "norm_mlp_pre": g_g3, "norm_mlp_post": g_g4})
    order = list(weights)
    small_names = [n for n in order if n not in ("w_in", "w_out", "w_up", "w_down")]
    small_w_shapes = [weights[n][0].shape for n in small_names]
    packed = [_pack_rows([weights[n][k] for n in small_names]) for k in range(3)]
    packed_g = _pack_rows([grads[n].reshape(weights[n][0].shape) for n in small_names])
    sd, sm, sv = _adamw(packed[0], packed_g, packed[1], packed[2], "adamw_small")
    for k, n in enumerate(small_names):
        grads[n] = grads[n].reshape(weights[n][0].shape)
    for res, pk in ((delta, sd), (new_m, sm), (new_v, sv)):
        for n, val in zip(small_names, _unpack_rows(pk, small_w_shapes)):
            res[n] = val
    mine, other = rs_in.share()
    w_t, m_t, v_t = [jnp.swapaxes(a[0], 0, 1) for a in weights["w_in"]]
    results_t = _adamw_halves_t(w_t, mine.T, other.T, m_t, v_t, "adamw_w_in")
    grads["w_in"], delta["w_in"], new_m["w_in"], new_v["w_in"] = [jnp.swapaxes(r, 0, 1)[None] for r in results_t]

    return (loss, grad_x[None], *[grads[n] for n in order], *[delta[n] for n in order],
            *[new_m[n] for n in order], *[new_v[n] for n in order])
```

```python
import math

import numpy as np
import jax
import jax.numpy as jnp
from jax import lax
from jax.experimental import pallas as pl
from jax.experimental.pallas import tpu as pltpu
from jax.experimental.pallas import tpu_sc as plsc

F32 = jnp.float32
BF16 = jnp.bfloat16

D_MODEL = 2048
SSM_HEAD_DIM = 64
SSM_GROUPS = 8
HEADS_PER_GROUP = 4
SSM_HEADS = SSM_GROUPS * HEADS_PER_GROUP
D_SSM = SSM_HEADS * SSM_HEAD_DIM
D_STATE = 128
CONV_WIDTH = 4
SSD_CHUNK = 128
D_XBC = D_SSM + 2 * SSM_GROUPS * D_STATE
GROUP_X = HEADS_PER_GROUP * SSM_HEAD_DIM
GROUP_COLS = GROUP_X + 2 * D_STATE
ATT_HEAD_DIM = 128
ATT_HEADS = 16
D_ATT = ATT_HEADS * ATT_HEAD_DIM
DILATIONS = (1, 4, 16)
ATT_BLOCK = 128
D_MIX = D_SSM + D_ATT
D_IN_PROJ = D_SSM + D_XBC + SSM_HEADS + 3 * D_ATT
D_FF = 4 * D_MODEL
EPS = 1e-6
N_CHIPS = 4
W_IN_SHARD = D_IN_PROJ // N_CHIPS

ADAM_LR = 0.001
ADAM_B1 = 0.9
ADAM_B2 = 0.999
ADAM_EPS = 1e-08
ADAM_WD = 0.01
ADAM_STEP = 10

LANES = 128
VMEM_LIMIT = 48 * 1024 * 1024
MESH = pl.DeviceIdType.MESH

_NN = (((1,), (0,)), ((), ()))
_NT = (((1,), (1,)), ((), ()))
_TN = (((0,), (0,)), ((), ()))


def _dot(a, b, dims=_NN):
    return lax.dot_general(a, b, dims, preferred_element_type=F32)


def _cparams(*sem):
    return pltpu.CompilerParams(dimension_semantics=sem, vmem_limit_bytes=VMEM_LIMIT)


TK = 2048
TK_MULTI = 1024


def _matmul(pairs, mode, out_dtypes, *, name, tm=1024, tn=1024, epilogue=None, extras=(), deps=(), out_quarters=False):
    a0, b0, _ = pairs[0]
    m_dim = a0.shape[-1] if mode == "tn" else a0.shape[-2]
    if b0.ndim == 3:
        n_dim = b0.shape[1] if mode == "nt" else b0.shape[0] * b0.shape[2]
    else:
        n_dim = b0.shape[0] if mode == "nt" else b0.shape[1]
    tm, tn = min(tm, m_dim), min(tn, n_dim)
    nks, offs = [], []
    for a, _, tk in pairs:
        k_part = a.shape[0] if mode == "tn" else a.shape[-1]
        k_dim = k_part * (a.shape[0] if a.ndim == 3 else 1)
        assert k_part % tk == 0, (name, k_part, tk)
        offs.append(sum(nks))
        nks.append(k_dim // tk)
    nk_total = sum(nks)
    assert m_dim % tm == 0 and n_dim % tn == 0, (name, m_dim, n_dim)
    dims = {"nn": _NN, "nt": _NT, "tn": _TN}[mode]
    n_pairs, n_extra, n_out = len(pairs), len(extras), len(out_dtypes)

    in_specs, operands = [], []
    for (a, b, tk), off, nk in zip(pairs, offs, nks):
        def kidx(k, off=off, nk=nk):
            return k if n_pairs == 1 else jnp.clip(k - off, 0, nk - 1)
        if mode == "tn":
            assert a.ndim == 2
            in_specs.append(pl.BlockSpec((tk, tm), lambda m, n, k, f=kidx: (f(k), m)))
        elif a.ndim == 3:
            per = a.shape[2] // tk
            in_specs.append(pl.BlockSpec((None, tm, tk), lambda m, n, k, f=kidx, per=per: (f(k) // per, m, f(k) % per)))
        else:
            in_specs.append(pl.BlockSpec((tm, tk), lambda m, n, k, f=kidx: (m, f(k))))
        if b.ndim == 3 and mode == "nt":
            per = b.shape[2] // tk
            in_specs.append(pl.BlockSpec((None, tn, tk), lambda m, n, k, f=kidx, per=per: (f(k) // per, n, f(k) % per)))
        elif b.ndim == 3:
            per = b.shape[2] // tn
            in_specs.append(pl.BlockSpec((None, tk, tn), lambda m, n, k, f=kidx, per=per: (n // per, f(k), n % per)))
        elif mode == "nt":
            in_specs.append(pl.BlockSpec((tn, tk), lambda m, n, k, f=kidx: (n, f(k))))
        else:
            in_specs.append(pl.BlockSpec((tk, tn), lambda m, n, k, f=kidx: (f(k), n)))
        operands += [a, b]
    for e in extras:
        in_specs.append(pl.BlockSpec((tm, tn), lambda m, n, k: (m, n)))
        operands.append(e)
    in_specs += [pl.BlockSpec(memory_space=pl.ANY)] * len(deps)
    operands += list(deps)
    first_out = 2 * n_pairs + n_extra + len(deps)
    if out_quarters:
        out_per_q = n_dim // N_CHIPS // tn
        out_dims = (N_CHIPS, m_dim, n_dim // N_CHIPS)
        out_spec = pl.BlockSpec((None, tm, tn), lambda m, n, k: (n // out_per_q, m, n % out_per_q))
    else:
        out_dims = (m_dim, n_dim)
        out_spec = pl.BlockSpec((tm, tn), lambda m, n, k: (m, n))

    def body(*refs):
        ab = refs[:2 * n_pairs]
        e_refs = refs[2 * n_pairs:2 * n_pairs + n_extra]
        o_refs = refs[first_out:first_out + n_out]

        def finish(total):
            vals = (total,) if epilogue is None else epilogue(total, *[e[...] for e in e_refs])
            for o_ref, v in zip(o_refs, vals):
                o_ref[...] = v.astype(o_ref.dtype)

        if nk_total == 1:
            finish(_dot(ab[0][...], ab[1][...], dims))
            return
        acc = refs[-1]
        k = pl.program_id(2)

        @pl.when(k == 0)
        def _():
            acc[...] = jnp.zeros_like(acc)

        for i in range(n_pairs):
            def accumulate(i=i):
                acc[...] += _dot(ab[2 * i][...], ab[2 * i + 1][...], dims)
            if n_pairs == 1:
                accumulate()
            else:
                pl.when((k >= offs[i]) & (k < offs[i] + nks[i]))(accumulate)

        @pl.when(k == nk_total - 1)
        def _():
            finish(acc[...])

    outs = pl.pallas_call(
        body,
        grid=(m_dim // tm, n_dim // tn, nk_total),
        in_specs=in_specs,
        out_specs=[out_spec for _ in out_dtypes],
        out_shape=[jax.ShapeDtypeStruct(out_dims, dt) for dt in out_dtypes],
        scratch_shapes=[pltpu.VMEM((tm, tn), F32)] if nk_total > 1 else [],
        compiler_params=_cparams("parallel", "parallel", "arbitrary"),
        name=name,
    )(*operands)
    return outs[0] if n_out == 1 else outs


def _rowcall(fn, rows, vecs, row_outs, acc_widths, *, name, tr=256, row_cols=None, deps=()):
    s_dim = rows[0].shape[0]
    assert s_dim % tr == 0
    row_cols = row_cols or [None] * len(rows)
    n_r, n_v, n_ro, n_acc = len(rows), len(vecs), len(row_outs), len(acc_widths)
    in_specs = []
    for r, rc in zip(rows, row_cols):
        if rc is None:
            in_specs.append(pl.BlockSpec((tr, r.shape[1]), lambda i: (i, 0)))
        else:
            in_specs.append(pl.BlockSpec((tr, rc[0]), lambda i, c=rc[1]: (i, c)))
    for v in vecs:
        in_specs.append(pl.BlockSpec(v.shape, lambda i, nd=v.ndim: (0,) * nd))
    in_specs += [pl.BlockSpec(memory_space=pl.ANY)] * len(deps)
    n_d = len(deps)

    def body(*refs):
        ins = [r[...] for r in refs[:n_r + n_v]]
        ro = refs[n_r + n_v + n_d:n_r + n_v + n_d + n_ro]
        ao = refs[n_r + n_v + n_d + n_ro:]
        outs = fn(*ins)
        for ref, v in zip(ro, outs[:n_ro]):
            ref[...] = v.astype(ref.dtype)
        if n_acc:
            @pl.when(pl.program_id(0) == 0)
            def _():
                for ref in ao:
                    ref[...] = jnp.zeros_like(ref)
            for ref, v in zip(ao, outs[n_ro:]):
                ref[...] += v

    outs = pl.pallas_call(
        body,
        grid=(s_dim // tr,),
        in_specs=in_specs,
        out_specs=[pl.BlockSpec((tr, w), lambda i: (i, 0)) for w, _ in row_outs]
        + [pl.BlockSpec((1, w), lambda i: (0, 0)) for w in acc_widths],
        out_shape=[jax.ShapeDtypeStruct((s_dim, w), dt) for w, dt in row_outs]
        + [jax.ShapeDtypeStruct((1, w), F32) for w in acc_widths],
        compiler_params=_cparams("arbitrary"),
        name=name,
    )(*rows, *vecs, *deps)
    return outs


def _nrm(x, g):
    r = lax.rsqrt(jnp.mean(x * x, axis=-1, keepdims=True) + EPS)
    n = x * r
    return n * g, n, r


def _nrm_bwd(dy, n, r, g):
    dn = dy * g
    dx = r * (dn - n * jnp.mean(dn * n, axis=-1, keepdims=True))
    return dx, jnp.sum(dy * n, axis=0, keepdims=True)


def _sigmoid(x):
    return 1.0 / (1.0 + jnp.exp(-x))


def _softplus(x):
    return jnp.maximum(x, 0.0) + jnp.log(1.0 + jnp.exp(-jnp.abs(x)))


def _pre_norm(x, g1):
    def fn(xb, g):
        return (_nrm(xb, g)[0],)
    return _rowcall(fn, [x], [g1], [(D_MODEL, BF16)], [], name="pre_norm")[0]


def _post_pre_norm(x, mix, g2, g3):
    def fn(xb, mb, g2b, g3b):
        h1 = xb + _nrm(mb, g2b)[0]
        return h1, _nrm(h1, g3b)[0]
    return _rowcall(fn, [x, mix], [g2, g3], [(D_MODEL, F32), (D_MODEL, BF16)], [], name="post_pre_norm")


def _tail(ff, h1, target, g4):
    def fn(ffb, h1b, tb, g):
        y, n, r = _nrm(ffb, g)
        e = h1b + y - tb
        loss = 0.5 * jnp.sum(jnp.sum(e * e, axis=-1, keepdims=True) * (1.0 / D_MODEL), axis=0, keepdims=True)
        dh2 = e * (1.0 / D_MODEL)
        dff, dg = _nrm_bwd(dh2, n, r, g)
        return dh2, dff, dg, jnp.broadcast_to(loss, (1, LANES))
    return _rowcall(fn, [ff, h1, target], [g4], [(D_MODEL, F32), (D_MODEL, BF16)], [D_MODEL, LANES], name="tail")


def _mid_bwd(du2, h1, dh2, mix, g2, g3, deps=()):
    def fn(du2b, h1b, dh2b, mb, g2b, g3b):
        _, n3, r3 = _nrm(h1b, g3b)
        d3, dg3 = _nrm_bwd(du2b, n3, r3, g3b)
        dh1 = dh2b + d3
        _, n2, r2 = _nrm(mb, g2b)
        dmix, dg2 = _nrm_bwd(dh1, n2, r2, g2b)
        return dh1, dmix, dg3, dg2
    return _rowcall(fn, [du2, h1, dh2, mix], [g2, g3], [(D_MODEL, F32), (D_MODEL, BF16)], [D_MODEL, D_MODEL],
                    name="mid_bwd", deps=deps)


def _first_bwd(du, x, dh1, g1):
    def fn(dub, xb, dh1b, g):
        _, n, r = _nrm(xb, g)
        dx, dg = _nrm_bwd(dub, n, r, g)
        return dh1b + dx, dg
    return _rowcall(fn, [du, x, dh1], [g1], [(D_MODEL, F32)], [D_MODEL], name="first_bwd")


CONV_TILE = 256
CONV_ROWS = 256
PAD = 8


def _conv_taps(w):
    return [w[k:k + 1, :] for k in range(CONV_WIDTH)], w[CONV_WIDTH:CONV_WIDTH + 1, :]


def _conv_fwd(xbc, w8):
    s_dim, c_dim = xbc.shape
    n_steps = s_dim // CONV_ROWS

    def body(x_ref, w_ref, o_ref, xp):
        xp[0:PAD, :] = jnp.zeros((PAD, CONV_TILE), F32)
        xp[PAD:PAD + s_dim, :] = x_ref[...]
        taps, bias = _conv_taps(w_ref[...])

        def step(c, carry):
            base = pl.multiple_of(c * CONV_ROWS, CONV_ROWS)
            win = xp[pl.ds(base, CONV_ROWS + PAD), :]
            pre = bias + taps[3] * win[PAD:, :]
            for j in range(1, CONV_WIDTH):
                pre = pre + taps[3 - j] * pltpu.roll(win, j, axis=0)[PAD:, :]
            o_ref[pl.ds(base, CONV_ROWS), :] = pre * _sigmoid(pre)
            return carry

        lax.fori_loop(0, n_steps, step, 0)

    return pl.pallas_call(
        body,
        grid=(c_dim // CONV_TILE,),
        in_specs=[pl.BlockSpec((s_dim, CONV_TILE), lambda j: (0, j)), pl.BlockSpec((8, CONV_TILE), lambda j: (0, j))],
        out_specs=pl.BlockSpec((s_dim, CONV_TILE), lambda j: (0, j)),
        out_shape=jax.ShapeDtypeStruct((s_dim, c_dim), F32),
        scratch_shapes=[pltpu.VMEM((s_dim + 2 * PAD, CONV_TILE), F32)],
        compiler_params=_cparams("parallel"),
        name="conv_fwd",
    )(xbc, w8)


def _conv_bwd(xbc, w8, dxc):
    s_dim, c_dim = xbc.shape
    n_steps = s_dim // CONV_ROWS

    def body(x_ref, w_ref, d_ref, dx_ref, dw_ref, xp, dp):
        xp[0:PAD, :] = jnp.zeros((PAD, CONV_TILE), F32)
        xp[PAD:PAD + s_dim, :] = x_ref[...]
        dp[PAD + s_dim:, :] = jnp.zeros((PAD, CONV_TILE), F32)
        taps, bias = _conv_taps(w_ref[...])

        def step1(c, sums):
            base = pl.multiple_of(c * CONV_ROWS, CONV_ROWS)
            win = xp[pl.ds(base, CONV_ROWS + PAD), :]
            shifted = [win[PAD:, :]] + [pltpu.roll(win, j, axis=0)[PAD:, :] for j in range(1, CONV_WIDTH)]
            pre = bias
            for j in range(CONV_WIDTH):
                pre = pre + taps[3 - j] * shifted[j]
            sg = _sigmoid(pre)
            dpre = d_ref[pl.ds(base, CONV_ROWS), :] * (sg * (1.0 + pre * (1.0 - sg)))
            dp[pl.ds(base + PAD, CONV_ROWS), :] = dpre
            new = [sums[k] + jnp.sum(dpre * shifted[3 - k], axis=0, keepdims=True) for k in range(CONV_WIDTH)]
            new.append(sums[CONV_WIDTH] + jnp.sum(dpre, axis=0, keepdims=True))
            return tuple(new)

        zero = jnp.zeros((1, CONV_TILE), F32)
        sums = lax.fori_loop(0, n_steps, step1, (zero,) * (CONV_WIDTH + 1))
        dw_ref[...] = jnp.zeros((8, CONV_TILE), F32)
        for k in range(CONV_WIDTH + 1):
            dw_ref[k:k + 1, :] = sums[k]

        def step2(c, carry):
            base = pl.multiple_of(c * CONV_ROWS, CONV_ROWS)
            win = dp[pl.ds(base + PAD, CONV_ROWS + PAD), :]
            dx = taps[3] * win[:CONV_ROWS, :]
            for j in range(1, CONV_WIDTH):
                dx = dx + taps[3 - j] * pltpu.roll(win, CONV_ROWS + PAD - j, axis=0)[:CONV_ROWS, :]
            dx_ref[pl.ds(base, CONV_ROWS), :] = dx.astype(BF16)
            return carry

        lax.fori_loop(0, n_steps, step2, 0)

    col = lambda j: (0, j)
    return pl.pallas_call(
        body,
        grid=(c_dim // CONV_TILE,),
        in_specs=[pl.BlockSpec((s_dim, CONV_TILE), col), pl.BlockSpec((8, CONV_TILE), col),
                  pl.BlockSpec((s_dim, CONV_TILE), col)],
        out_specs=[pl.BlockSpec((s_dim, CONV_TILE), col), pl.BlockSpec((8, CONV_TILE), col)],
        out_shape=[jax.ShapeDtypeStruct((s_dim, c_dim), BF16), jax.ShapeDtypeStruct((8, c_dim), F32)],
        scratch_shapes=[pltpu.VMEM((s_dim + 2 * PAD, CONV_TILE), F32), pltpu.VMEM((s_dim + 2 * PAD, CONV_TILE), F32)],
        compiler_params=_cparams("parallel"),
        name="conv_bwd",
    )(xbc, w8, dxc)


def _perm_cols(a):
    parts = []
    for g in range(SSM_GROUPS):
        parts += [a[..., g * GROUP_X:(g + 1) * GROUP_X],
                  a[..., D_SSM + g * D_STATE:D_SSM + (g + 1) * D_STATE],
                  a[..., D_SSM + SSM_GROUPS * D_STATE + g * D_STATE:D_SSM + SSM_GROUPS * D_STATE + (g + 1) * D_STATE]]
    return jnp.concatenate(parts, axis=-1)


def _unperm_cols(a):
    xs = [a[..., g * GROUP_COLS:g * GROUP_COLS + GROUP_X] for g in range(SSM_GROUPS)]
    bs = [a[..., g * GROUP_COLS + GROUP_X:g * GROUP_COLS + GROUP_X + D_STATE] for g in range(SSM_GROUPS)]
    cs = [a[..., g * GROUP_COLS + GROUP_X + D_STATE:(g + 1) * GROUP_COLS] for g in range(SSM_GROUPS)]
    return jnp.concatenate(xs + bs + cs, axis=-1)


def _dt_to_groups(dt):
    s_dim = dt.shape[0]
    t = dt[:, :SSM_HEADS].reshape(s_dim, SSM_GROUPS, HEADS_PER_GROUP).transpose(1, 0, 2)
    return jnp.pad(t, ((0, 0), (0, 0), (0, LANES - HEADS_PER_GROUP)))


def _dt_from_groups(dtg):
    s_dim = dtg.shape[1]
    return dtg[:, :, :HEADS_PER_GROUP].transpose(1, 0, 2).reshape(s_dim, SSM_HEADS)


def _pack_ssd_params(dt_bias, a_log, d_skip):
    rows = jnp.stack([p.reshape(SSM_GROUPS, HEADS_PER_GROUP) for p in (dt_bias, a_log, d_skip)], axis=1)
    return jnp.pad(rows, ((0, 0), (0, 8 - 3), (0, LANES - HEADS_PER_GROUP)))


def _unpack_ssd_params(par):
    return tuple(par[:, k, :HEADS_PER_GROUP].reshape(SSM_HEADS) for k in range(3))


Q = SSD_CHUNK


def _split3(v):
    hi = v.astype(BF16)
    r1 = v - hi.astype(F32)
    mid = r1.astype(BF16)
    lo = (r1 - mid.astype(F32)).astype(BF16)
    return hi, mid, lo


def _dot_l01(t01, v):
    return sum(_dot(t01, p) for p in _split3(v))


def _dot_r01(v, e01):
    return sum(_dot(p, e01) for p in _split3(v))


def _ssd_consts():
    row = lax.broadcasted_iota(jnp.int32, (Q, Q), 0)
    col = lax.broadcasted_iota(jnp.int32, (Q, Q), 1)
    causal = row >= col
    tril = causal.astype(BF16)
    triu = (col >= row).astype(BF16)
    er = lax.broadcasted_iota(jnp.int32, (LANES, GROUP_X), 0)
    ec = lax.broadcasted_iota(jnp.int32, (LANES, GROUP_X), 1) // SSM_HEAD_DIM
    expand = (er == ec).astype(BF16)
    rr = lax.broadcasted_iota(jnp.int32, (GROUP_X, LANES), 0) // SSM_HEAD_DIM
    rc = lax.broadcasted_iota(jnp.int32, (GROUP_X, LANES), 1)
    reduce = (rr == rc).astype(BF16)
    lane_head = lax.broadcasted_iota(jnp.int32, (Q, GROUP_X), 1) // SSM_HEAD_DIM
    return causal, tril, triu, expand, reduce, lane_head


def _ssd_common(xc_ref, dt_ref, par_ref, consts):
    causal, tril, _, expand, _, _ = consts
    par = par_ref[...]
    bias, alog, dsk = par[0:1, :], par[1:2, :], par[2:3, :]
    a_neg = -jnp.exp(alog)
    dtr = dt_ref[...] + bias
    dt = _softplus(dtr)
    s = _dot_l01(tril, dt * a_neg)
    dt_x = _dot_r01(dt, expand)
    s_x = _dot_r01(s, expand)
    dsk_x = _dot_r01(jnp.broadcast_to(dsk, (8, LANES)), expand)[0:1, :]
    blk = xc_ref[...]
    x = blk[:, :GROUP_X]
    bm = blk[:, GROUP_X:GROUP_X + D_STATE].astype(BF16)
    cm = blk[:, GROUP_X + D_STATE:].astype(BF16)
    xdt = x * dt_x
    g = _dot(cm, bm, _NT)
    return dict(a_neg=a_neg, dtr=dtr, dt=dt, s=s, s_t=s.T, dt_x=dt_x, s_x=s_x, dsk_x=dsk_x, x=x, bm=bm, cm=cm,
                xdt=xdt, g=g)


def _decay(v, r, causal):
    diff = v["s"][:, r:r + 1] - v["s_t"][r:r + 1, :]
    return jnp.exp(jnp.where(causal, diff, -jnp.inf))


def _ssd_specs(n_chunks, rev):
    cidx = (lambda c: n_chunks - 1 - c) if rev else (lambda c: c)
    xc = pl.BlockSpec((Q, GROUP_COLS), lambda g, c: (cidx(c), g))
    gx = pl.BlockSpec((Q, GROUP_X), lambda g, c: (cidx(c), g))
    dt = pl.BlockSpec((None, Q, LANES), lambda g, c: (g, cidx(c), 0))
    par = pl.BlockSpec((None, 8, LANES), lambda g, c: (g, 0, 0))
    nw = pl.BlockSpec((1, GROUP_X), lambda g, c: (0, g))
    hs = pl.BlockSpec((None, None, D_STATE, GROUP_X), lambda g, c: (cidx(c), g, 0, 0))
    return xc, gx, dt, par, nw, hs


def _ssd_fwd(xc, z, dtg, par, nw):
    s_dim = xc.shape[0]
    n_chunks = s_dim // Q
    xc_s, gx_s, dt_s, par_s, nw_s, hs_s = _ssd_specs(n_chunks, False)

    def body(xc_ref, z_ref, dt_ref, par_ref, nw_ref, y_ref, ys_ref, hs_ref, ht):
        @pl.when(pl.program_id(1) == 0)
        def _():
            ht[...] = jnp.zeros_like(ht)

        consts = _ssd_consts()
        causal, lane_head = consts[0], consts[5]
        v = _ssd_common(xc_ref, dt_ref, par_ref, consts)
        xdt_b = v["xdt"].astype(BF16)
        yd = jnp.zeros((Q, GROUP_X), F32)
        for r in range(HEADS_PER_GROUP):
            m = (v["g"] * _decay(v, r, causal)).astype(BF16)
            yd = yd + _dot(m, jnp.where(lane_head == r, xdt_b, jnp.zeros_like(xdt_b)))
        h = ht[...]
        hs_ref[...] = h
        yo = jnp.exp(v["s_x"]) * _dot(v["cm"], h.astype(BF16))
        y = yd + yo + v["dsk_x"] * v["x"]
        s_last = v["s_x"][Q - 1:Q, :]
        snew = _dot(v["bm"], (v["xdt"] * jnp.exp(s_last - v["s_x"])).astype(BF16), _TN)
        ht[...] = jnp.exp(s_last) * h + snew
        zz = z_ref[...]
        yg = y * (zz * _sigmoid(zz))
        y_ref[...] = y
        ys_ref[...] = _nrm(yg, nw_ref[...])[0].astype(BF16)

    return pl.pallas_call(
        body,
        grid=(SSM_GROUPS, n_chunks),
        in_specs=[xc_s, gx_s, dt_s, par_s, nw_s],
        out_specs=[gx_s, gx_s, hs_s],
        out_shape=[jax.ShapeDtypeStruct((s_dim, D_SSM), F32), jax.ShapeDtypeStruct((s_dim, D_SSM), BF16),
                   jax.ShapeDtypeStruct((n_chunks, SSM_GROUPS, D_STATE, GROUP_X), F32)],
        scratch_shapes=[pltpu.VMEM((D_STATE, GROUP_X), F32)],
        compiler_params=_cparams("parallel", "arbitrary"),
        name="ssd_fwd",
    )(xc, z, dtg, par, nw)


def _ssd_bwd(xc, z, dtg, par, nw, y, hs, dymix):
    s_dim = xc.shape[0]
    n_chunks = s_dim // Q
    xc_s, gx_s, dt_s, par_s, nw_s, hs_s = _ssd_specs(n_chunks, True)

    def body(xc_ref, z_ref, dt_ref, par_ref, nw_ref, y_ref, hs_ref, dys_ref,
             dxc_ref, dz_ref, ddt_ref, dpar_ref, dnw_ref, dht):
        @pl.when(pl.program_id(1) == 0)
        def _():
            dht[...] = jnp.zeros_like(dht)
            dpar_ref[...] = jnp.zeros_like(dpar_ref)
            dnw_ref[...] = jnp.zeros_like(dnw_ref)

        consts = _ssd_consts()
        causal, _, triu, _, reduce, lane_head = consts
        v = _ssd_common(xc_ref, dt_ref, par_ref, consts)
        x, bm, cm, xdt, s_x = v["x"], v["bm"], v["cm"], v["xdt"], v["s_x"]
        h = hs_ref[...]
        hb = h.astype(BF16)
        es_x = jnp.exp(s_x)
        yo = es_x * _dot(cm, hb)
        s_last = s_x[Q - 1:Q, :]
        e_x = jnp.exp(s_last - s_x)
        es_last = jnp.exp(s_last)

        yv, zz, nw_v = y_ref[...], z_ref[...], nw_ref[...]
        sg = _sigmoid(zz)
        gz = zz * sg
        _, n, rstd = _nrm(yv * gz, nw_v)
        dout = dys_ref[...]
        dyg, dnw = _nrm_bwd(dout, n, rstd, nw_v)
        dnw_ref[...] += dnw
        dy = dyg * gz
        dz_ref[...] = (dyg * yv * (sg * (1.0 + zz * (1.0 - sg)))).astype(BF16)

        dyb = dy.astype(BF16)
        xdt_b = xdt.astype(BF16)
        dhp = dht[...]
        dhpb = dhp.astype(BF16)
        lane = lax.broadcasted_iota(jnp.int32, (Q, LANES), 1)
        sub = lax.broadcasted_iota(jnp.int32, (LANES, Q), 0)
        dxdt = jnp.zeros((Q, GROUP_X), F32)
        dg = jnp.zeros((Q, Q), F32)
        ds = jnp.zeros((Q, LANES), F32)
        ds_t = jnp.zeros((LANES, Q), F32)
        for r in range(HEADS_PER_GROUP):
            dec = _decay(v, r, causal)
            mf = v["g"] * dec
            dyr = jnp.where(lane_head == r, dyb, jnp.zeros_like(dyb))
            dm = _dot(dyr, xdt_b, _NT)
            dxdt = dxdt + _dot(mf.astype(BF16), dyr, _TN)
            dg = dg + dm * dec
            dd = dm * mf
            ds = ds + jnp.where(lane == r, jnp.sum(dd, axis=1, keepdims=True), 0.0)
            ds_t = ds_t + jnp.where(sub == r, jnp.sum(dd, axis=0, keepdims=True), 0.0)
        ds = ds - ds_t.T
        dgb = dg.astype(BF16)
        dwb = (es_x * dy).astype(BF16)
        dcm = _dot(dgb, bm) + _dot(dwb, hb, _NT)
        dh_prev = _dot(cm, dwb, _TN)
        zst = _dot(bm, dhpb)
        xe = xdt * e_x
        dxdt = dxdt + e_x * zst
        dee = xe * zst
        dbm = _dot(dgb, cm, _TN) + _dot(xe.astype(BF16), dhpb, _NT)
        v_last = jnp.sum(dee, axis=0, keepdims=True) + es_last * jnp.sum(dhp * h, axis=0, keepdims=True)
        row_x = lax.broadcasted_iota(jnp.int32, (Q, GROUP_X), 0)
        tx = dy * yo - dee + jnp.where(row_x == Q - 1, v_last, 0.0)
        ds = ds + _dot_r01(tx, reduce)
        ddta = _dot_l01(triu, ds)
        ddt = ddta * v["a_neg"] + _dot_r01(dxdt * x, reduce)
        dalog = jnp.sum(ddta * v["dt"], axis=0, keepdims=True) * v["a_neg"]
        draw = jnp.where(lane < HEADS_PER_GROUP, ddt * _sigmoid(v["dtr"]), 0.0)
        dbias = jnp.sum(draw, axis=0, keepdims=True)
        ddsk = _dot_r01(jnp.broadcast_to(jnp.sum(dy * x, axis=0, keepdims=True), (8, GROUP_X)), reduce)[0:1, :]
        dht[...] = es_last * dhp + dh_prev
        dxc_ref[:, :GROUP_X] = dxdt * v["dt_x"] + v["dsk_x"] * dy
        dxc_ref[:, GROUP_X:GROUP_X + D_STATE] = dbm
        dxc_ref[:, GROUP_X + D_STATE:] = dcm
        ddt_ref[...] = draw
        dpar_ref[0:1, :] += dbias
        dpar_ref[1:2, :] += dalog
        dpar_ref[2:3, :] += ddsk

    return pl.pallas_call(
        body,
        grid=(SSM_GROUPS, n_chunks),
        in_specs=[xc_s, gx_s, dt_s, par_s, nw_s, gx_s, hs_s, gx_s],
        out_specs=[xc_s, gx_s, dt_s, par_s, nw_s],
        out_shape=[jax.ShapeDtypeStruct((s_dim, SSM_GROUPS * GROUP_COLS), F32),
                   jax.ShapeDtypeStruct((s_dim, D_SSM), BF16),
                   jax.ShapeDtypeStruct((SSM_GROUPS, s_dim, LANES), F32),
                   jax.ShapeDtypeStruct((SSM_GROUPS, 8, LANES), F32),
                   jax.ShapeDtypeStruct((1, D_SSM), F32)],
        scratch_shapes=[pltpu.VMEM((D_STATE, GROUP_X), F32)],
        compiler_params=_cparams("parallel", "arbitrary"),
        name="ssd_bwd",
    )(xc, z, dtg, par, nw, y, hs, dymix)


ATT_SCALE = ATT_HEAD_DIM ** -0.5
NEG_INF = -jnp.inf


def _band_masks():
    qi = lax.broadcasted_iota(jnp.int32, (ATT_BLOCK, ATT_BLOCK), 0)
    kj = lax.broadcasted_iota(jnp.int32, (ATT_BLOCK, ATT_BLOCK), 1)
    return kj <= qi, kj >= qi


WIN = ATT_BLOCK * DILATIONS[-1]
N_BLOCKS = WIN // ATT_BLOCK


def _rows(start, d):
    return pl.ds(start, ATT_BLOCK) if d == 1 else pl.ds(start, ATT_BLOCK, stride=d)


def _block_start(idx, d):
    return (idx // d) * (ATT_BLOCK * d) + idx % d


def _lane_bcast(col):
    return jnp.broadcast_to(col, (col.shape[0], LANES))


def _attn_fused_fwd(qkv):
    s_dim = qkv.shape[0]
    n_win = s_dim // WIN
    blk = (WIN, ATT_HEAD_DIM)
    prev = lambda w: jnp.maximum(w - 1, 0)

    def body(q_ref, kc_ref, kp_ref, vc_ref, vp_ref, y_ref, yf_ref, lse_ref, qf, kf, vf, acc, m_run, l_run):
        w, h = pl.program_id(0), pl.program_id(1)
        qf[...] = q_ref[...].astype(F32)
        kf[0:WIN, :] = kp_ref[...].astype(F32)
        kf[WIN:, :] = kc_ref[...].astype(F32)
        vf[0:WIN, :] = vp_ref[...].astype(F32)
        vf[WIN:, :] = vc_ref[...].astype(F32)
        own, before = _band_masks()

        for d in DILATIONS:
            def block(idx, carry, d=d):
                start = _block_start(idx, d)
                rows = _rows(start, d)
                q = qf[rows, :].astype(BF16)
                kc, vc = kf[_rows(WIN + start, d), :].astype(BF16), vf[_rows(WIN + start, d), :].astype(BF16)
                kp = kf[_rows(WIN + start - ATT_BLOCK * d, d), :].astype(BF16)
                vp = vf[_rows(WIN + start - ATT_BLOCK * d, d), :].astype(BF16)
                has_prev = (idx >= d) | (w > 0)
                sc = jnp.where(own, _dot(q, kc, _NT) * ATT_SCALE, NEG_INF)
                sp = jnp.where(before & has_prev, _dot(q, kp, _NT) * ATT_SCALE, NEG_INF)
                m_blk = jnp.maximum(jnp.max(sc, axis=1, keepdims=True), jnp.max(sp, axis=1, keepdims=True))
                if d == DILATIONS[0]:
                    m_new = m_blk
                else:
                    m_old = m_run[rows, :][:, 0:1]
                    m_new = jnp.maximum(m_old, m_blk)
                pc, pp = jnp.exp(sc - m_new), jnp.exp(sp - m_new)
                l_new = jnp.sum(pc, axis=1, keepdims=True) + jnp.sum(pp, axis=1, keepdims=True)
                o_new = _dot(pc.astype(BF16), vc) + _dot(pp.astype(BF16), vp)
                if d != DILATIONS[0]:
                    alpha = jnp.exp(m_old - m_new)
                    l_new = alpha * l_run[rows, :][:, 0:1] + l_new
                    o_new = alpha * acc[rows, :] + o_new
                m_run[rows, :] = _lane_bcast(m_new)
                l_run[rows, :] = _lane_bcast(l_new)
                acc[rows, :] = o_new
                return carry

            for idx in range(N_BLOCKS):
                block(idx, 0)

        l_all = l_run[...]
        y = acc[...] / l_all
        y_ref[...] = y.astype(BF16)
        yf_ref[...] = y
        @pl.when(h == 0)
        def _():
            lse_ref[...] = jnp.zeros_like(lse_ref)

        lane = lax.broadcasted_iota(jnp.int32, (WIN, LANES), 1)
        lse_ref[...] = jnp.where(lane == h, m_run[...] + jnp.log(l_all), lse_ref[...])

    win_scratch = lambda rows: pltpu.VMEM((rows, ATT_HEAD_DIM), F32)
    return pl.pallas_call(
        body,
        grid=(n_win, ATT_HEADS),
        in_specs=[pl.BlockSpec(blk, lambda w, h: (w, h)),
                  pl.BlockSpec(blk, lambda w, h: (w, ATT_HEADS + h)),
                  pl.BlockSpec(blk, lambda w, h: (prev(w), ATT_HEADS + h)),
                  pl.BlockSpec(blk, lambda w, h: (w, 2 * ATT_HEADS + h)),
                  pl.BlockSpec(blk, lambda w, h: (prev(w), 2 * ATT_HEADS + h))],
        out_specs=[pl.BlockSpec(blk, lambda w, h: (w, h)), pl.BlockSpec(blk, lambda w, h: (w, h)),
                   pl.BlockSpec((WIN, LANES), lambda w, h: (w, 0))],
        out_shape=[jax.ShapeDtypeStruct((s_dim, D_ATT), BF16), jax.ShapeDtypeStruct((s_dim, D_ATT), F32),
                   jax.ShapeDtypeStruct((s_dim, LANES), F32)],
        scratch_shapes=[win_scratch(WIN), win_scratch(2 * WIN), win_scratch(2 * WIN), win_scratch(WIN),
                        win_scratch(WIN), win_scratch(WIN)],
        compiler_params=_cparams("parallel", "arbitrary"),
        name="attn_fused_fwd",
    )(qkv, qkv, qkv, qkv, qkv)


def _attn_fused_bwd(qkv, dymix, y_att, lse, deps=()):
    s_dim = qkv.shape[0]
    n_win = s_dim // WIN
    blk = (s_dim, ATT_HEAD_DIM)
    n_dep = len(deps)

    def body(q_ref, k_ref, v_ref, dy_ref, y_ref, l_ref, *rest):
        out_ref = rest[n_dep]
        qf, kf, vf, dq_acc, dk_acc, dv_acc, ls, dl = rest[n_dep + 1:]
        h = pl.program_id(0)
        qf[...] = q_ref[...].astype(F32)
        kf[...] = k_ref[...].astype(F32)
        vf[...] = v_ref[...].astype(F32)
        lane = lax.broadcasted_iota(jnp.int32, (s_dim, LANES), 1)
        ls[...] = _lane_bcast(jnp.sum(jnp.where(lane == h, l_ref[...], 0.0), axis=1, keepdims=True))
        dl[...] = _lane_bcast(jnp.sum(dy_ref[...] * y_ref[...], axis=1, keepdims=True))
        for ref in (dq_acc, dk_acc, dv_acc):
            ref[...] = jnp.zeros_like(ref)
        own, before = _band_masks()

        def probs(q, k, v, dy, lse_col, dl_col, mask):
            p = jnp.exp(jnp.where(mask, _dot(q, k, _NT) * ATT_SCALE - lse_col, NEG_INF))
            ds = p * (_dot(dy, v, _NT) - dl_col)
            return p.astype(BF16), ds.astype(BF16)

        for win in range(n_win):
            for d in DILATIONS:
                for idx in range(N_BLOCKS):
                    start = win * WIN + _block_start(idx, d)
                    rows = _rows(start, d)
                    q, dy = qf[rows, :].astype(BF16), dy_ref[rows, :].astype(BF16)
                    lse_col, dl_col = ls[rows, :][:, 0:1], dl[rows, :][:, 0:1]
                    kc, vc = kf[rows, :].astype(BF16), vf[rows, :].astype(BF16)
                    pc, dsc = probs(q, kc, vc, dy, lse_col, dl_col, own)
                    dk_acc[rows, :] += _dot(dsc, q, _TN) * ATT_SCALE
                    dv_acc[rows, :] += _dot(pc, dy, _TN)
                    if start < ATT_BLOCK * d:
                        dq_acc[rows, :] += _dot(dsc, kc) * ATT_SCALE
                        continue
                    prows = _rows(start - ATT_BLOCK * d, d)
                    kp, vp = kf[prows, :].astype(BF16), vf[prows, :].astype(BF16)
                    pp, dsp = probs(q, kp, vp, dy, lse_col, dl_col, before)
                    dq_acc[rows, :] += (_dot(dsc, kc) + _dot(dsp, kp)) * ATT_SCALE
                    dk_acc[prows, :] += _dot(dsp, q, _TN) * ATT_SCALE
                    dv_acc[prows, :] += _dot(pp, dy, _TN)

        for part, acc_ref in enumerate((dq_acc, dk_acc, dv_acc)):
            out_ref[part] = acc_ref[...].astype(BF16)

    col = lambda c: pl.BlockSpec(blk, lambda h: (0, c + h))
    return pl.pallas_call(
        body,
        grid=(ATT_HEADS,),
        in_specs=[col(0), col(ATT_HEADS), col(2 * ATT_HEADS), col(ATT_HEADS), col(0),
                  pl.BlockSpec((s_dim, LANES), lambda h: (0, 0))] + [ANY] * n_dep,
        out_specs=pl.BlockSpec((3, s_dim, ATT_HEAD_DIM), lambda h: (0, 0, h)),
        out_shape=jax.ShapeDtypeStruct((3, s_dim, D_ATT), BF16),
        scratch_shapes=[pltpu.VMEM((s_dim, ATT_HEAD_DIM), F32)] * 8,
        compiler_params=_cparams("parallel"),
        name="attn_fused_bwd",
    )(qkv, qkv, qkv, dymix, y_att, lse, *deps)


def _adamw(w, g, m, v, name):
    def fn(wb, gb, mb, vb):
        m2 = ADAM_B1 * mb + (1.0 - ADAM_B1) * gb
        v2 = ADAM_B2 * vb + (1.0 - ADAM_B2) * (gb * gb)
        m_hat = m2 / (1.0 - ADAM_B1 ** ADAM_STEP)
        v_hat = v2 / (1.0 - ADAM_B2 ** ADAM_STEP)
        delta = -ADAM_LR * (m_hat / (jnp.sqrt(v_hat) + ADAM_EPS) + ADAM_WD * wb)
        return delta, m2, v2
    cols = w.shape[1]
    tr = 128 if w.shape[0] % 128 == 0 else w.shape[0]
    return _rowcall(fn, [w, g, m, v], [], [(cols, F32)] * 3, [], name=name, tr=tr)


ANY = pl.BlockSpec(memory_space=pl.ANY)


def _position():
    x, y, c = lax.axis_index("x"), lax.axis_index("y"), lax.axis_index("c")
    chips = [(1 - x, y), (x, 1 - y), (1 - x, 1 - y)]
    return x, y, c, chips


def _remote(src, dst, send_sem, recv_sem, device):
    return pltpu.make_async_remote_copy(src_ref=src, dst_ref=dst, send_sem=send_sem, recv_sem=recv_sem,
                                        device_id=device, device_id_type=MESH)


def _handshake(peers):
    barrier = pltpu.get_barrier_semaphore()
    for p in peers:
        pl.semaphore_signal(barrier, inc=1, device_id=p, device_id_type=MESH)
    pl.semaphore_wait(barrier, len(peers))


def _gather_shards_async(shards, collective_id, name):
    n = len(shards)
    srcs = [jax.new_ref(s, memory_space=pltpu.MemorySpace.HBM) for s in shards]
    dsts = [jax.empty_ref(jax.ShapeDtypeStruct((N_CHIPS,) + s.shape, s.dtype), memory_space=pltpu.MemorySpace.HBM)
            for s in shards]

    @pl.kernel(mesh=plsc.ScalarSubcoreMesh(axis_name="seq", num_cores=1), name=name,
               scratch_types=(pltpu.SemaphoreType.DMA((6 * n,)), pltpu.SemaphoreType.DMA((6 * n,))),
               compiler_params=pltpu.CompilerParams(collective_id=collective_id))
    def launch(send_sems, recv_sems):
        x, y, c, chips = _position()
        sibling = (x, y, 1 - c)
        _handshake([(chip[0], chip[1], c) for chip in chips] + [sibling])

        def half(a, j, cc):
            h = shards[a].shape[0] // 2
            return dsts[a].at[j, pl.ds(cc * h, h), :]

        sent = []
        for a in range(n):
            h = shards[a].shape[0] // 2
            for j, chip in enumerate(chips):
                cp = _remote(srcs[a].at[pl.ds(c * h, h), :], half(a, 2 * x + y, c), send_sems.at[6 * a + j],
                             recv_sems.at[6 * a + j], (chip[0], chip[1], c))
                cp.start()
                sent.append(cp)
        for a in range(n):
            for j, chip in enumerate(chips):
                landed = half(a, 2 * chip[0] + chip[1], c)
                _remote(landed, landed, send_sems.at[6 * a + j], recv_sems.at[6 * a + j], (x, y, c)).wait_recv()
                cp = _remote(landed, landed, send_sems.at[6 * a + 3 + j], recv_sems.at[6 * a + 3 + j], sibling)
                cp.start()
                sent.append(cp)
        for a in range(n):
            for j, chip in enumerate(chips):
                handed = half(a, 2 * chip[0] + chip[1], 1 - c)
                _remote(handed, handed, send_sems.at[6 * a + 3 + j], recv_sems.at[6 * a + 3 + j], (x, y, c)).wait_recv()
        for cp in sent:
            cp.wait_send()

    launch()
    return [d[...] for d in dsts]


IN_COLS = {"z": (0, D_SSM), "xbc": (D_SSM, D_SSM + D_XBC), "dt": (D_SSM + D_XBC, D_SSM + D_XBC + SSM_HEADS),
           "qkv": (D_SSM + D_XBC + SSM_HEADS, D_IN_PROJ)}


def _cols_from_quarters(quarters, lo, hi):
    parts = []
    for q in range(N_CHIPS):
        a, b = max(lo, q * W_IN_SHARD), min(hi, (q + 1) * W_IN_SHARD)
        if a < b:
            parts.append(quarters[q][:, a - q * W_IN_SHARD:b - q * W_IN_SHARD])
    return parts[0] if len(parts) == 1 else jnp.concatenate(parts, axis=1)


def _quarters_from_cols(pieces):
    quarters = []
    for q in range(N_CHIPS):
        parts = []
        for name, (lo, hi) in IN_COLS.items():
            a, b = max(lo, q * W_IN_SHARD), min(hi, (q + 1) * W_IN_SHARD)
            if a < b:
                parts.append(pieces[name][:, a - lo:b - lo])
        quarters.append(jnp.concatenate(parts, axis=1))
    return jnp.stack(quarters)


def _by_chip(own, fetched):
    me = 2 * lax.axis_index("x") + lax.axis_index("y")
    return lax.dynamic_update_slice(fetched, own[None], (me, 0, 0))


def _add_sibling(grad, got, place, name, deps=()):
    nq, rows, cols = grad.shape
    h = rows // 2
    tr = 128
    nb = h // tr

    def body(place_ref, a_ref, b_ref, *rest):
        own_ref, ob_ref = rest[len(deps):]
        total = a_ref[...] + b_ref[...]
        ob_ref[...] = total.astype(BF16)

        @pl.when(pl.program_id(1) == place_ref[1])
        def _():
            own_ref[...] = total

    return pl.pallas_call(
        body,
        grid_spec=pltpu.PrefetchScalarGridSpec(
            num_scalar_prefetch=1, grid=(nb, nq),
            in_specs=[pl.BlockSpec((None, tr, cols), lambda i, q, p: (q, p[0] * nb + i, 0)),
                      pl.BlockSpec((None, tr, cols), lambda i, q, p: (q, i, 0))] + [ANY] * len(deps),
            out_specs=[pl.BlockSpec((tr, cols), lambda i, q, p: (i, 0)),
                       pl.BlockSpec((None, tr, cols), lambda i, q, p: (q, i, 0))]),
        out_shape=[jax.ShapeDtypeStruct((h, cols), F32), jax.ShapeDtypeStruct((nq, h, cols), BF16)],
        compiler_params=_cparams("parallel", "arbitrary"),
        name=name,
    )(place, grad, got, *deps)


def _add_chips(part, got, name, deps=()):
    h, cols = part.shape
    tr = 128

    def body(p_ref, g0_ref, g1_ref, g2_ref, *rest):
        o_ref = rest[len(deps)]
        o_ref[...] = ((p_ref[...] + g0_ref[...].astype(F32)) + g1_ref[...].astype(F32)) + g2_ref[...].astype(F32)

    got_spec = lambda j: pl.BlockSpec((None, tr, cols), lambda i: (j, i, 0))
    row_spec = pl.BlockSpec((tr, cols), lambda i: (i, 0))
    return pl.pallas_call(
        body,
        grid=(h // tr,),
        in_specs=[row_spec, got_spec(0), got_spec(1), got_spec(2)] + [ANY] * len(deps),
        out_specs=row_spec,
        out_shape=jax.ShapeDtypeStruct((h, cols), F32),
        compiler_params=_cparams("parallel"),
        name=name,
    )(part, got, got, got, *deps)


def _sequencer_exchange(src, out_shape, collective_id, name, plan, n_copies):
    src_ref = jax.new_ref(src, memory_space=pltpu.MemorySpace.HBM)
    dst_ref = jax.empty_ref(out_shape, memory_space=pltpu.MemorySpace.HBM)

    @pl.kernel(mesh=plsc.ScalarSubcoreMesh(axis_name="seq", num_cores=1), name=name,
               scratch_types=(pltpu.SemaphoreType.DMA((n_copies,)), pltpu.SemaphoreType.DMA((n_copies,))),
               compiler_params=pltpu.CompilerParams(collective_id=collective_id))
    def launch(send_sems, recv_sems):
        x, y, c, chips = _position()
        copies = plan(src_ref, dst_ref, x, y, c, chips)
        _handshake([peer for _, _, peer in copies])
        started = []
        for k, (s, d, peer) in enumerate(copies):
            cp = _remote(s, d, send_sems.at[k], recv_sems.at[k], peer)
            cp.start()
            started.append(cp)
        for cp in started:
            cp.wait()

    launch()
    return dst_ref[...]


class _AsyncReduceScatter:
    def __init__(self, grad, nm, first_id):
        self.grad, self.nm, self.first_id = grad, nm, first_id
        nq, rows, cols = grad.shape
        h = self.h = rows // 2

        def to_sibling(s, d, x, y, c, chips):
            return [(s.at[:, pl.ds((1 - c) * h, h), :], d, (x, y, 1 - c))]

        self.from_sibling = _sequencer_exchange(grad, jax.ShapeDtypeStruct((nq, h, cols), F32), first_id,
                                                f"rs_sibling_{nm}", to_sibling, 1)

    def sibling_sum(self, not_before=()):
        cols = self.grad.shape[2]
        place = jnp.stack([lax.axis_index("c"), 2 * lax.axis_index("x") + lax.axis_index("y")]).astype(jnp.int32)
        self.part, self.part_b = _add_sibling(self.grad, self.from_sibling, place, f"add_sibling_{self.nm}", not_before)

        def to_chips(s, d, x, y, c, chips):
            return [(s.at[2 * chip[0] + chip[1]], d.at[j], (chip[0], chip[1], c)) for j, chip in enumerate(chips)]

        self.from_chips = _sequencer_exchange(self.part_b, jax.ShapeDtypeStruct((3, self.h, cols), BF16),
                                              self.first_id + 1, f"rs_quarters_{self.nm}", to_chips, 3)
        return self.part_b

    def chip_sum(self, not_before=()):
        cols = self.grad.shape[2]
        self.half = _add_chips(self.part, self.from_chips, f"add_chips_{self.nm}", not_before)

        def whole_to_sibling(s, d, x, y, c, chips):
            return [(s, d, (x, y, 1 - c))]

        self.other = _sequencer_exchange(self.half, jax.ShapeDtypeStruct((self.h, cols), F32), self.first_id + 2,
                                         f"rs_share_{self.nm}", whole_to_sibling, 1)
        return self.half

    def share(self):
        return self.half, self.other


def _after(x, deps, name):
    def body(x_ref, *rest):
        rest[-1][...] = x_ref[...]

    vm = pl.BlockSpec(memory_space=pltpu.VMEM)
    return pl.pallas_call(body, in_specs=[vm] + [ANY] * len(deps), out_specs=vm,
                          out_shape=jax.ShapeDtypeStruct(x.shape, x.dtype), name=name)(x, *deps)


def _adamw_halves(w, mine, other, m, v, name):
    rows, cols = w.shape
    tr = 128
    nb = rows // 2 // tr
    c_arr = lax.axis_index("c").astype(jnp.int32).reshape(1)

    def body(c_ref, w_ref, a_ref, b_ref, m_ref, v_ref, g_out, d_out, m_out, v_out):
        is_mine = (pl.program_id(0) // nb) == c_ref[0]
        g = jnp.where(is_mine, a_ref[...], b_ref[...])
        wb, mb, vb = w_ref[...], m_ref[...], v_ref[...]
        m2 = ADAM_B1 * mb + (1.0 - ADAM_B1) * g
        v2 = ADAM_B2 * vb + (1.0 - ADAM_B2) * (g * g)
        m_hat = m2 / (1.0 - ADAM_B1 ** ADAM_STEP)
        v_hat = v2 / (1.0 - ADAM_B2 ** ADAM_STEP)
        g_out[...] = g
        d_out[...] = -ADAM_LR * (m_hat / (jnp.sqrt(v_hat) + ADAM_EPS) + ADAM_WD * wb)
        m_out[...] = m2
        v_out[...] = v2

    full = pl.BlockSpec((tr, cols), lambda i, c: (i, 0))
    half = pl.BlockSpec((tr, cols), lambda i, c: (i % nb, 0))
    return pl.pallas_call(
        body,
        grid_spec=pltpu.PrefetchScalarGridSpec(
            num_scalar_prefetch=1, grid=(rows // tr,),
            in_specs=[full, half, half, full, full], out_specs=[full] * 4),
        out_shape=[jax.ShapeDtypeStruct((rows, cols), F32)] * 4,
        compiler_params=_cparams("parallel"),
        name=name,
    )(c_arr, w, mine, other, m, v)


def _adamw_halves_t(w_t, mine_t, other_t, m_t, v_t, name):
    cols, rows = w_t.shape
    tr = cols // 11
    assert tr * 11 == cols and tr % 8 == 0
    c_arr = lax.axis_index("c").astype(jnp.int32).reshape(1)

    def body(c_ref, w_ref, a_ref, b_ref, m_ref, v_ref, g_out, d_out, m_out, v_out):
        first = c_ref[0] == 0
        a, b = a_ref[...], b_ref[...]
        g = jnp.concatenate([jnp.where(first, a, b), jnp.where(first, b, a)], axis=1)
        wb, mb, vb = w_ref[...], m_ref[...], v_ref[...]
        m2 = ADAM_B1 * mb + (1.0 - ADAM_B1) * g
        v2 = ADAM_B2 * vb + (1.0 - ADAM_B2) * (g * g)
        m_hat = m2 / (1.0 - ADAM_B1 ** ADAM_STEP)
        v_hat = v2 / (1.0 - ADAM_B2 ** ADAM_STEP)
        g_out[...] = g
        d_out[...] = -ADAM_LR * (m_hat / (jnp.sqrt(v_hat) + ADAM_EPS) + ADAM_WD * wb)
        m_out[...] = m2
        v_out[...] = v2

    full = pl.BlockSpec((tr, rows), lambda i, c: (i, 0))
    half = pl.BlockSpec((tr, rows // 2), lambda i, c: (i, 0))
    return pl.pallas_call(
        body,
        grid_spec=pltpu.PrefetchScalarGridSpec(
            num_scalar_prefetch=1, grid=(cols // tr,),
            in_specs=[full, half, half, full, full], out_specs=[full] * 4),
        out_shape=[jax.ShapeDtypeStruct((cols, rows), F32)] * 4,
        compiler_params=_cparams("parallel"),
        name=name,
    )(c_arr, w_t, mine_t, other_t, m_t, v_t)


def _all_sum_small(v):
    n_dev = 8

    def body(v_ref, o_ref, gath, send_sems, recv_sems):
        x, y, c, _ = _position()
        me = 4 * x + 2 * y + c
        gath[me] = v_ref[...]
        copies = []
        for k in range(1, n_dev):
            peer = tuple(1 - p if (k >> s) & 1 else p for p, s in ((x, 2), (y, 1), (c, 0)))
            cp = _remote(v_ref, gath.at[me], send_sems.at[k - 1], recv_sems.at[k - 1], peer)
            cp.start()
            copies.append(cp)
        for cp in copies:
            cp.wait()
        acc = gath[0]
        for i in range(1, n_dev):
            acc = acc + gath[i]
        o_ref[...] = acc

    vm = pl.BlockSpec(memory_space=pltpu.VMEM)
    return pl.pallas_call(
        body,
        in_specs=[vm],
        out_specs=vm,
        out_shape=jax.ShapeDtypeStruct(v.shape, F32),
        scratch_shapes=[pltpu.VMEM((n_dev,) + v.shape, F32), pltpu.SemaphoreType.DMA((n_dev - 1,)),
                        pltpu.SemaphoreType.DMA((n_dev - 1,))],
        name="all_sum_small",
    )(v)


def _pack_rows(vectors):
    rows = []
    for v in vectors:
        flat = v.reshape(-1).astype(F32)
        rows.append(jnp.pad(flat, (0, (-flat.shape[0]) % LANES)).reshape(-1, LANES))
    out = jnp.concatenate(rows, axis=0)
    return jnp.pad(out, ((0, (-out.shape[0]) % 8), (0, 0)))


def _unpack_rows(packed, shapes):
    outs, r = [], 0
    for shp in shapes:
        size = math.prod(shp)
        nr = -(-size // LANES)
        outs.append(packed[r:r + nr].reshape(-1)[:size].reshape(shp))
        r += nr
    return outs


def _relu_sq(acc):
    r = jnp.maximum(acc, 0.0)
    return r, r * r


def _relu_sq_bwd(acc, r):
    return (acc * (2.0 * r.astype(F32)),)


def kernel(x, norm_mix_pre, w_in, conv_w, conv_b, dt_bias, a_log, d_skip, ssm_norm_w, w_out, norm_mix_post, norm_mlp_pre, w_up, w_down, norm_mlp_post, loss_target, m_norm_mix_pre, m_w_in, m_conv_w, m_conv_b, m_dt_bias, m_a_log, m_d_skip, m_ssm_norm_w, m_w_out, m_norm_mix_post, m_norm_mlp_pre, m_w_up, m_w_down, m_norm_mlp_post, v_norm_mix_pre, v_w_in, v_conv_w, v_conv_b, v_dt_bias, v_a_log, v_d_skip, v_ssm_norm_w, v_w_out, v_norm_mix_post, v_norm_mlp_pre, v_w_up, v_w_down, v_norm_mlp_post):
    s_dim = x.shape[1]
    xs, target = x[0], loss_target[0]
    chip = 2 * lax.axis_index("x") + lax.axis_index("y")

    own = [w_in[0].astype(BF16), w_out[0].astype(BF16), w_up[0].astype(BF16), w_down[0].astype(BF16)]
    fetched_in = _gather_shards_async(own[:1], 14, "gather_w_in")[0]
    conv_cols = D_XBC // N_CHIPS
    conv_placed = lax.dynamic_update_slice(jnp.zeros((8, D_XBC), F32), 0.5 * conv_w[0], (0, chip * conv_cols))
    conv_full = _all_sum_small(conv_placed.reshape(-1, LANES)).reshape(8, D_XBC)
    w8 = _perm_cols(conv_full.at[CONV_WIDTH].set(conv_b[0]))
    u = _pre_norm(xs, norm_mix_pre)
    fetched_in, u, w8, *rest = lax.optimization_barrier((fetched_in, u, w8, *own[1:]))
    fetched = [fetched_in] + _gather_shards_async(rest, 1, "gather_rest")
    g_in, g_out, g_up, g_down = [_by_chip(o, f) for o, f in zip(own, fetched)]
    w_z = _cols_from_quarters(g_in, *IN_COLS["z"])
    w_xbc = _perm_cols(_cols_from_quarters(g_in, *IN_COLS["xbc"]))
    w_dt = jnp.pad(_cols_from_quarters(g_in, *IN_COLS["dt"]), ((0, 0), (0, LANES - SSM_HEADS)))
    w_qkv = _cols_from_quarters(g_in, *IN_COLS["qkv"])
    w_out_full = g_out.reshape(D_MIX, D_MODEL)
    w_down_full = g_down.reshape(D_FF, D_MODEL)

    z = _matmul([(u, w_z, TK)], "nn", [F32], name="proj_z")
    xbc = _matmul([(u, w_xbc, TK)], "nn", [F32], name="proj_xbc")
    dt_raw = _matmul([(u, w_dt, TK)], "nn", [F32], name="proj_dt")
    qkv = _matmul([(u, w_qkv, TK)], "nn", [BF16], name="proj_qkv")
    xc = _conv_fwd(xbc, w8)
    dtg = _dt_to_groups(dt_raw)
    par = _pack_ssd_params(dt_bias[0], a_log[0], d_skip[0])
    y, y_ssm, states = _ssd_fwd(xc, z, dtg, par, ssm_norm_w)
    y_att, y_att_f32, lse = _attn_fused_fwd(qkv)
    y_mix = jnp.concatenate([y_ssm, y_att], axis=1)
    mix = _matmul([(y_mix, w_out_full, TK)], "nn", [F32], name="out_proj")
    h1, u2 = _post_pre_norm(xs, mix, norm_mix_post, norm_mlp_pre)
    hid, act = _matmul([(u2, g_up, TK)], "nn", [BF16, BF16], name="mlp_up", epilogue=_relu_sq)
    ff = _matmul([(act, w_down_full, TK)], "nn", [F32], name="mlp_down")
    dh2, dff, d_g4, loss_part = _tail(ff, h1, target, norm_mlp_post)

    dhid = _matmul([(dff, w_down_full, TK)], "nt", [BF16], name="mlp_down_dx", epilogue=_relu_sq_bwd, extras=[hid])
    weights = {"norm_mix_pre": (norm_mix_pre, m_norm_mix_pre, v_norm_mix_pre), "w_in": (w_in, m_w_in, v_w_in),
               "conv_w": (conv_w, m_conv_w, v_conv_w), "conv_b": (conv_b, m_conv_b, v_conv_b),
               "dt_bias": (dt_bias, m_dt_bias, v_dt_bias), "a_log": (a_log, m_a_log, v_a_log),
               "d_skip": (d_skip, m_d_skip, v_d_skip), "ssm_norm_w": (ssm_norm_w, m_ssm_norm_w, v_ssm_norm_w),
               "w_out": (w_out, m_w_out, v_w_out), "norm_mix_post": (norm_mix_post, m_norm_mix_post, v_norm_mix_post),
               "norm_mlp_pre": (norm_mlp_pre, m_norm_mlp_pre, v_norm_mlp_pre), "w_up": (w_up, m_w_up, v_w_up),
               "w_down": (w_down, m_w_down, v_w_down),
               "norm_mlp_post": (norm_mlp_post, m_norm_mlp_post, v_norm_mlp_post)}
    grads, delta, new_m, new_v = {}, {}, {}, {}

    def adamw_big(n, halves):
        w, m, v = weights[n]
        g_, d_, m_, v_ = _adamw_halves(w[0], halves[0], halves[1], m[0], v[0], f"adamw_{n}")
        grads[n], delta[n], new_m[n], new_v[n] = g_[None], d_[None], m_[None], v_[None]

    dw_down = _matmul([(act, dff, TK)], "tn", [F32], name="mlp_down_dw")
    rs_down = _AsyncReduceScatter(dw_down.reshape(N_CHIPS, D_FF // N_CHIPS, D_MODEL), "w_down", 11)
    dw_up = _matmul([(u2, dhid, TK)], "tn", [F32], name="mlp_up_dw", deps=[dw_down], out_quarters=True)
    rs_up = _AsyncReduceScatter(dw_up, "w_up", 8)
    du2 = _matmul([(dhid, g_up, TK)], "nt", [F32], name="mlp_up_dx",
                  deps=[rs_down.sibling_sum(not_before=[dw_up])])
    dh1, dmix, d_g3, d_g2 = _mid_bwd(du2, h1, dh2, mix, norm_mix_post, norm_mlp_pre,
                                     deps=[rs_up.sibling_sum(not_before=[du2])])
    dymix = _matmul([(dmix, w_out_full, TK)], "nt", [F32], name="out_proj_dx")
    dw_out = _matmul([(y_mix, dmix, TK)], "tn", [F32], name="out_proj_dw")
    rs_out = _AsyncReduceScatter(dw_out.reshape(N_CHIPS, D_MIX // N_CHIPS, D_MODEL), "w_out", 5)
    dqkv = _attn_fused_bwd(qkv, dymix, y_att_f32, lse)
    par_late = _after(par, [rs_down.chip_sum(not_before=[dqkv]), rs_out.sibling_sum(not_before=[dymix])],
                      "after_w_down")
    dxc, dz, ddtg, dpar, d_nw = _ssd_bwd(xc, z, dtg, par_late, ssm_norm_w, y, states, dymix)
    g_down = rs_down.share()
    dxbc, dw8 = _conv_bwd(xbc, _after(w8, [*g_down, rs_up.chip_sum(not_before=[dxc])], "after_w_up"), dxc)
    ddt = jnp.pad(_dt_from_groups(ddtg), ((0, 0), (0, LANES - SSM_HEADS))).astype(BF16)
    g_up = rs_up.share()
    dw_z = _matmul([(u, dz, TK)], "tn", [F32], name="proj_z_dw")
    dw_xbc = _matmul([(u, dxbc, TK)], "tn", [F32], name="proj_xbc_dw",
                     deps=[*g_up, rs_out.chip_sum(not_before=[dxbc])])
    g_out = rs_out.share()
    dw_dt = _matmul([(u, ddt, TK)], "tn", [F32], name="proj_dt_dw")
    dw_qkv = _matmul([(u, dqkv, TK)], "tn", [F32], name="proj_qkv_dw")
    dw_in = _quarters_from_cols({"z": dw_z, "xbc": _unperm_cols(dw_xbc), "dt": dw_dt[:, :SSM_HEADS], "qkv": dw_qkv})
    rs_in = _AsyncReduceScatter(dw_in, "w_in", 2)
    adamw_big("w_down", g_down)
    adamw_big("w_up", g_up)
    rs_in.sibling_sum(not_before=[delta["w_up"]])
    du = _matmul([(dz, w_z, TK_MULTI), (dxbc, w_xbc, TK_MULTI), (dqkv, w_qkv, TK_MULTI), (ddt, w_dt, LANES)], "nt",
                 [F32], name="proj_dx", deps=[*g_out, rs_in.part_b])
    grad_x, d_g1 = _first_bwd(du, xs, dh1, norm_mix_pre)
    adamw_big("w_out", g_out)
    rs_in.chip_sum(not_before=[grad_x, delta["w_out"]])

    dconv = _unperm_cols(dw8)
    d_bias, d_alog, d_dskip = _unpack_ssd_params(dpar)
    small_shapes = [(1, D_MODEL), (CONV_WIDTH, D_XBC), (1, D_XBC), (1, SSM_HEADS), (1, SSM_HEADS), (1, SSM_HEADS),
                    (1, D_SSM), (1, D_MODEL), (1, D_MODEL), (1, D_MODEL), (1, LANES)]
    summed = _unpack_rows(
        _all_sum_small(_pack_rows([d_g1, dconv[:CONV_WIDTH], dconv[CONV_WIDTH:CONV_WIDTH + 1], d_bias, d_alog,
                                   d_dskip, d_nw, d_g2, d_g3, d_g4, loss_part])), small_shapes)
    (g_g1, g_conv_full, g_conv_b, g_bias, g_alog, g_dskip, g_nw, g_g2, g_g3, g_g4, loss_row) = summed
    loss = loss_row[0, 0]
    g_conv_w = lax.dynamic_slice(g_conv_full, (0, chip * conv_cols), (CONV_WIDTH, conv_cols))[None]

    grads.update({"norm_mix_pre": g_g1, "conv_w": g_conv_w, "conv_b": g_conv_b, "dt_bias": g_bias,
                  "a_log": g_alog, "d_skip": g_dskip, "ssm_norm_w": g_nw, "norm_mix_post": g_g2,
                  "norm_mlp_pre": g_g3, "norm_mlp_post": g_g4})
    order = list(weights)
    small_names = [n for n in order if n not in ("w_in", "w_out", "w_up", "w_down")]
    small_w_shapes = [weights[n][0].shape for n in small_names]
    packed = [_pack_rows([weights[n][k] for n in small_names]) for k in range(3)]
    packed_g = _pack_rows([grads[n].reshape(weights[n][0].shape) for n in small_names])
    sd, sm, sv = _adamw(packed[0], packed_g, packed[1], packed[2], "adamw_small")
    for k, n in enumerate(small_names):
        grads[n] = grads[n].reshape(weights[n][0].shape)
    for res, pk in ((delta, sd), (new_m, sm), (new_v, sv)):
        for n, val in zip(small_names, _unpack_rows(pk, small_w_shapes)):
            res[n] = val
    mine, other = rs_in.share()
    w_t, m_t, v_t = [jnp.swapaxes(a[0], 0, 1) for a in weights["w_in"]]
    results_t = _adamw_halves_t(w_t, mine.T, other.T, m_t, v_t, "adamw_w_in")
    grads["w_in"], delta["w_in"], new_m["w_in"], new_v["w_in"] = [jnp.swapaxes(r, 0, 1)[None] for r in results_t]

    return (loss, grad_x[None], *[grads[n] for n in order], *[delta[n] for n in order],
            *[new_m[n] for n in order], *[new_v[n] for n in order])
```

```python
import math

import numpy as np
import jax
import jax.numpy as jnp
from jax import lax
from jax.experimental import pallas as pl
from jax.experimental.pallas import tpu as pltpu
from jax.experimental.pallas import tpu_sc as plsc

F32 = jnp.float32
BF16 = jnp.bfloat16

D_MODEL = 2048
SSM_HEAD_DIM = 64
SSM_GROUPS = 8
HEADS_PER_GROUP = 4
SSM_HEADS = SSM_GROUPS * HEADS_PER_GROUP
D_SSM = SSM_HEADS * SSM_HEAD_DIM
D_STATE = 128
CONV_WIDTH = 4
SSD_CHUNK = 128
D_XBC = D_SSM + 2 * SSM_GROUPS * D_STATE
GROUP_X = HEADS_PER_GROUP * SSM_HEAD_DIM
GROUP_COLS = GROUP_X + 2 * D_STATE
ATT_HEAD_DIM = 128
ATT_HEADS = 16
D_ATT = ATT_HEADS * ATT_HEAD_DIM
DILATIONS = (1, 4, 16)
ATT_BLOCK = 128
D_MIX = D_SSM + D_ATT
D_IN_PROJ = D_SSM + D_XBC + SSM_HEADS + 3 * D_ATT
D_FF = 4 * D_MODEL
EPS = 1e-6
N_CHIPS = 4
W_IN_SHARD = D_IN_PROJ // N_CHIPS

ADAM_LR = 0.001
ADAM_B1 = 0.9
ADAM_B2 = 0.999
ADAM_EPS = 1e-08
ADAM_WD = 0.01
ADAM_STEP = 10

LANES = 128
VMEM_LIMIT = 48 * 1024 * 1024
MESH = pl.DeviceIdType.MESH

_NN = (((1,), (0,)), ((), ()))
_NT = (((1,), (1,)), ((), ()))
_TN = (((0,), (0,)), ((), ()))


def _dot(a, b, dims=_NN):
    return lax.dot_general(a, b, dims, preferred_element_type=F32)


def _cparams(*sem):
    return pltpu.CompilerParams(dimension_semantics=sem, vmem_limit_bytes=VMEM_LIMIT)


TK = 2048
TK_MULTI = 2048
TN_MULTI = 512


def _matmul(pairs, mode, out_dtypes, *, name, tm=1024, tn=1024, epilogue=None, extras=(), deps=(), out_quarters=False):
    a0, b0, _ = pairs[0]
    m_dim = a0.shape[-1] if mode == "tn" else a0.shape[-2]
    if b0.ndim == 3:
        n_dim = b0.shape[1] if mode == "nt" else b0.shape[0] * b0.shape[2]
    else:
        n_dim = b0.shape[0] if mode == "nt" else b0.shape[1]
    tm, tn = min(tm, m_dim), min(tn, n_dim)
    nks, offs = [], []
    for a, _, tk in pairs:
        k_part = a.shape[0] if mode == "tn" else a.shape[-1]
        k_dim = k_part * (a.shape[0] if a.ndim == 3 else 1)
        assert k_part % tk == 0, (name, k_part, tk)
        offs.append(sum(nks))
        nks.append(k_dim // tk)
    nk_total = sum(nks)
    assert m_dim % tm == 0 and n_dim % tn == 0, (name, m_dim, n_dim)
    dims = {"nn": _NN, "nt": _NT, "tn": _TN}[mode]
    n_pairs, n_extra, n_out = len(pairs), len(extras), len(out_dtypes)

    in_specs, operands = [], []
    for (a, b, tk), off, nk in zip(pairs, offs, nks):
        def kidx(k, off=off, nk=nk):
            return k if n_pairs == 1 else jnp.clip(k - off, 0, nk - 1)
        if mode == "tn":
            assert a.ndim == 2
            in_specs.append(pl.BlockSpec((tk, tm), lambda m, n, k, f=kidx: (f(k), m)))
        elif a.ndim == 3:
            per = a.shape[2] // tk
            in_specs.append(pl.BlockSpec((None, tm, tk), lambda m, n, k, f=kidx, per=per: (f(k) // per, m, f(k) % per)))
        else:
            in_specs.append(pl.BlockSpec((tm, tk), lambda m, n, k, f=kidx: (m, f(k))))
        if b.ndim == 3 and mode == "nt":
            per = b.shape[2] // tk
            in_specs.append(pl.BlockSpec((None, tn, tk), lambda m, n, k, f=kidx, per=per: (f(k) // per, n, f(k) % per)))
        elif b.ndim == 3:
            per = b.shape[2] // tn
            in_specs.append(pl.BlockSpec((None, tk, tn), lambda m, n, k, f=kidx, per=per: (n // per, f(k), n % per)))
        elif mode == "nt":
            in_specs.append(pl.BlockSpec((tn, tk), lambda m, n, k, f=kidx: (n, f(k))))
        else:
            in_specs.append(pl.BlockSpec((tk, tn), lambda m, n, k, f=kidx: (f(k), n)))
        operands += [a, b]
    for e in extras:
        in_specs.append(pl.BlockSpec((tm, tn), lambda m, n, k: (m, n)))
        operands.append(e)
    in_specs += [pl.BlockSpec(memory_space=pl.ANY)] * len(deps)
    operands += list(deps)
    first_out = 2 * n_pairs + n_extra + len(deps)
    if out_quarters:
        out_per_q = n_dim // N_CHIPS // tn
        out_dims = (N_CHIPS, m_dim, n_dim // N_CHIPS)
        out_spec = pl.BlockSpec((None, tm, tn), lambda m, n, k: (n // out_per_q, m, n % out_per_q))
    else:
        out_dims = (m_dim, n_dim)
        out_spec = pl.BlockSpec((tm, tn), lambda m, n, k: (m, n))

    def body(*refs):
        ab = refs[:2 * n_pairs]
        e_refs = refs[2 * n_pairs:2 * n_pairs + n_extra]
        o_refs = refs[first_out:first_out + n_out]

        def finish(total):
            vals = (total,) if epilogue is None else epilogue(total, *[e[...] for e in e_refs])
            for o_ref, v in zip(o_refs, vals):
                o_ref[...] = v.astype(o_ref.dtype)

        if nk_total == 1:
            finish(_dot(ab[0][...], ab[1][...], dims))
            return
        acc = refs[-1]
        k = pl.program_id(2)

        @pl.when(k == 0)
        def _():
            acc[...] = jnp.zeros_like(acc)

        for i in range(n_pairs):
            def accumulate(i=i):
                acc[...] += _dot(ab[2 * i][...], ab[2 * i + 1][...], dims)
            if n_pairs == 1:
                accumulate()
            else:
                pl.when((k >= offs[i]) & (k < offs[i] + nks[i]))(accumulate)

        @pl.when(k == nk_total - 1)
        def _():
            finish(acc[...])

    outs = pl.pallas_call(
        body,
        grid=(m_dim // tm, n_dim // tn, nk_total),
        in_specs=in_specs,
        out_specs=[out_spec for _ in out_dtypes],
        out_shape=[jax.ShapeDtypeStruct(out_dims, dt) for dt in out_dtypes],
        scratch_shapes=[pltpu.VMEM((tm, tn), F32)] if nk_total > 1 else [],
        compiler_params=_cparams("parallel", "parallel", "arbitrary"),
        name=name,
    )(*operands)
    return outs[0] if n_out == 1 else outs


def _rowcall(fn, rows, vecs, row_outs, acc_widths, *, name, tr=256, row_cols=None, deps=()):
    s_dim = rows[0].shape[0]
    assert s_dim % tr == 0
    row_cols = row_cols or [None] * len(rows)
    n_r, n_v, n_ro, n_acc = len(rows), len(vecs), len(row_outs), len(acc_widths)
    in_specs = []
    for r, rc in zip(rows, row_cols):
        if rc is None:
            in_specs.append(pl.BlockSpec((tr, r.shape[1]), lambda i: (i, 0)))
        else:
            in_specs.append(pl.BlockSpec((tr, rc[0]), lambda i, c=rc[1]: (i, c)))
    for v in vecs:
        in_specs.append(pl.BlockSpec(v.shape, lambda i, nd=v.ndim: (0,) * nd))
    in_specs += [pl.BlockSpec(memory_space=pl.ANY)] * len(deps)
    n_d = len(deps)

    def body(*refs):
        ins = [r[...] for r in refs[:n_r + n_v]]
        ro = refs[n_r + n_v + n_d:n_r + n_v + n_d + n_ro]
        ao = refs[n_r + n_v + n_d + n_ro:]
        outs = fn(*ins)
        for ref, v in zip(ro, outs[:n_ro]):
            ref[...] = v.astype(ref.dtype)
        if n_acc:
            @pl.when(pl.program_id(0) == 0)
            def _():
                for ref in ao:
                    ref[...] = jnp.zeros_like(ref)
            for ref, v in zip(ao, outs[n_ro:]):
                ref[...] += v

    outs = pl.pallas_call(
        body,
        grid=(s_dim // tr,),
        in_specs=in_specs,
        out_specs=[pl.BlockSpec((tr, w), lambda i: (i, 0)) for w, _ in row_outs]
        + [pl.BlockSpec((1, w), lambda i: (0, 0)) for w in acc_widths],
        out_shape=[jax.ShapeDtypeStruct((s_dim, w), dt) for w, dt in row_outs]
        + [jax.ShapeDtypeStruct((1, w), F32) for w in acc_widths],
        compiler_params=_cparams("arbitrary"),
        name=name,
    )(*rows, *vecs, *deps)
    return outs


def _nrm(x, g):
    r = lax.rsqrt(jnp.mean(x * x, axis=-1, keepdims=True) + EPS)
    n = x * r
    return n * g, n, r


def _nrm_bwd(dy, n, r, g):
    dn = dy * g
    dx = r * (dn - n * jnp.mean(dn * n, axis=-1, keepdims=True))
    return dx, jnp.sum(dy * n, axis=0, keepdims=True)


def _sigmoid(x):
    return 1.0 / (1.0 + jnp.exp(-x))


def _softplus(x):
    return jnp.maximum(x, 0.0) + jnp.log(1.0 + jnp.exp(-jnp.abs(x)))


def _pre_norm(x, g1):
    def fn(xb, g):
        return (_nrm(xb, g)[0],)
    return _rowcall(fn, [x], [g1], [(D_MODEL, BF16)], [], name="pre_norm")[0]


def _post_pre_norm(x, mix, g2, g3):
    def fn(xb, mb, g2b, g3b):
        h1 = xb + _nrm(mb, g2b)[0]
        return h1, _nrm(h1, g3b)[0]
    return _rowcall(fn, [x, mix], [g2, g3], [(D_MODEL, F32), (D_MODEL, BF16)], [], name="post_pre_norm")


def _tail(ff, h1, target, g4):
    def fn(ffb, h1b, tb, g):
        y, n, r = _nrm(ffb, g)
        e = h1b + y - tb
        loss = 0.5 * jnp.sum(jnp.sum(e * e, axis=-1, keepdims=True) * (1.0 / D_MODEL), axis=0, keepdims=True)
        dh2 = e * (1.0 / D_MODEL)
        dff, dg = _nrm_bwd(dh2, n, r, g)
        return dh2, dff, dg, jnp.broadcast_to(loss, (1, LANES))
    return _rowcall(fn, [ff, h1, target], [g4], [(D_MODEL, F32), (D_MODEL, BF16)], [D_MODEL, LANES], name="tail")


def _mid_bwd(du2, h1, dh2, mix, g2, g3, deps=()):
    def fn(du2b, h1b, dh2b, mb, g2b, g3b):
        _, n3, r3 = _nrm(h1b, g3b)
        d3, dg3 = _nrm_bwd(du2b, n3, r3, g3b)
        dh1 = dh2b + d3
        _, n2, r2 = _nrm(mb, g2b)
        dmix, dg2 = _nrm_bwd(dh1, n2, r2, g2b)
        return dh1, dmix, dg3, dg2
    return _rowcall(fn, [du2, h1, dh2, mix], [g2, g3], [(D_MODEL, F32), (D_MODEL, BF16)], [D_MODEL, D_MODEL],
                    name="mid_bwd", deps=deps)


def _first_bwd(du, x, dh1, g1):
    def fn(dub, xb, dh1b, g):
        _, n, r = _nrm(xb, g)
        dx, dg = _nrm_bwd(dub, n, r, g)
        return dh1b + dx, dg
    return _rowcall(fn, [du, x, dh1], [g1], [(D_MODEL, F32)], [D_MODEL], name="first_bwd")


CONV_TILE = 256
CONV_ROWS = 256
PAD = 8


def _conv_taps(w):
    return [w[k:k + 1, :] for k in range(CONV_WIDTH)], w[CONV_WIDTH:CONV_WIDTH + 1, :]


def _conv_fwd(xbc, w8):
    s_dim, c_dim = xbc.shape
    n_steps = s_dim // CONV_ROWS

    def body(x_ref, w_ref, o_ref, xp):
        xp[0:PAD, :] = jnp.zeros((PAD, CONV_TILE), F32)
        xp[PAD:PAD + s_dim, :] = x_ref[...]
        taps, bias = _conv_taps(w_ref[...])

        def step(c, carry):
            base = pl.multiple_of(c * CONV_ROWS, CONV_ROWS)
            win = xp[pl.ds(base, CONV_ROWS + PAD), :]
            pre = bias + taps[3] * win[PAD:, :]
            for j in range(1, CONV_WIDTH):
                pre = pre + taps[3 - j] * pltpu.roll(win, j, axis=0)[PAD:, :]
            o_ref[pl.ds(base, CONV_ROWS), :] = pre * _sigmoid(pre)
            return carry

        lax.fori_loop(0, n_steps, step, 0)

    return pl.pallas_call(
        body,
        grid=(c_dim // CONV_TILE,),
        in_specs=[pl.BlockSpec((s_dim, CONV_TILE), lambda j: (0, j)), pl.BlockSpec((8, CONV_TILE), lambda j: (0, j))],
        out_specs=pl.BlockSpec((s_dim, CONV_TILE), lambda j: (0, j)),
        out_shape=jax.ShapeDtypeStruct((s_dim, c_dim), F32),
        scratch_shapes=[pltpu.VMEM((s_dim + 2 * PAD, CONV_TILE), F32)],
        compiler_params=_cparams("parallel"),
        name="conv_fwd",
    )(xbc, w8)


def _conv_bwd(xbc, w8, dxc):
    s_dim, c_dim = xbc.shape
    n_steps = s_dim // CONV_ROWS

    def body(x_ref, w_ref, d_ref, dx_ref, dw_ref, xp, dp):
        xp[0:PAD, :] = jnp.zeros((PAD, CONV_TILE), F32)
        xp[PAD:PAD + s_dim, :] = x_ref[...]
        dp[PAD + s_dim:, :] = jnp.zeros((PAD, CONV_TILE), F32)
        taps, bias = _conv_taps(w_ref[...])

        def step1(c, sums):
            base = pl.multiple_of(c * CONV_ROWS, CONV_ROWS)
            win = xp[pl.ds(base, CONV_ROWS + PAD), :]
            shifted = [win[PAD:, :]] + [pltpu.roll(win, j, axis=0)[PAD:, :] for j in range(1, CONV_WIDTH)]
            pre = bias
            for j in range(CONV_WIDTH):
                pre = pre + taps[3 - j] * shifted[j]
            sg = _sigmoid(pre)
            dpre = d_ref[pl.ds(base, CONV_ROWS), :] * (sg * (1.0 + pre * (1.0 - sg)))
            dp[pl.ds(base + PAD, CONV_ROWS), :] = dpre
            new = [sums[k] + jnp.sum(dpre * shifted[3 - k], axis=0, keepdims=True) for k in range(CONV_WIDTH)]
            new.append(sums[CONV_WIDTH] + jnp.sum(dpre, axis=0, keepdims=True))
            return tuple(new)

        zero = jnp.zeros((1, CONV_TILE), F32)
        sums = lax.fori_loop(0, n_steps, step1, (zero,) * (CONV_WIDTH + 1))
        dw_ref[...] = jnp.zeros((8, CONV_TILE), F32)
        for k in range(CONV_WIDTH + 1):
            dw_ref[k:k + 1, :] = sums[k]

        def step2(c, carry):
            base = pl.multiple_of(c * CONV_ROWS, CONV_ROWS)
            win = dp[pl.ds(base + PAD, CONV_ROWS + PAD), :]
            dx = taps[3] * win[:CONV_ROWS, :]
            for j in range(1, CONV_WIDTH):
                dx = dx + taps[3 - j] * pltpu.roll(win, CONV_ROWS + PAD - j, axis=0)[:CONV_ROWS, :]
            dx_ref[pl.ds(base, CONV_ROWS), :] = dx.astype(BF16)
            return carry

        lax.fori_loop(0, n_steps, step2, 0)

    col = lambda j: (0, j)
    return pl.pallas_call(
        body,
        grid=(c_dim // CONV_TILE,),
        in_specs=[pl.BlockSpec((s_dim, CONV_TILE), col), pl.BlockSpec((8, CONV_TILE), col),
                  pl.BlockSpec((s_dim, CONV_TILE), col)],
        out_specs=[pl.BlockSpec((s_dim, CONV_TILE), col), pl.BlockSpec((8, CONV_TILE), col)],
        out_shape=[jax.ShapeDtypeStruct((s_dim, c_dim), BF16), jax.ShapeDtypeStruct((8, c_dim), F32)],
        scratch_shapes=[pltpu.VMEM((s_dim + 2 * PAD, CONV_TILE), F32), pltpu.VMEM((s_dim + 2 * PAD, CONV_TILE), F32)],
        compiler_params=_cparams("parallel"),
        name="conv_bwd",
    )(xbc, w8, dxc)


def _perm_cols(a):
    parts = []
    for g in range(SSM_GROUPS):
        parts += [a[..., g * GROUP_X:(g + 1) * GROUP_X],
                  a[..., D_SSM + g * D_STATE:D_SSM + (g + 1) * D_STATE],
                  a[..., D_SSM + SSM_GROUPS * D_STATE + g * D_STATE:D_SSM + SSM_GROUPS * D_STATE + (g + 1) * D_STATE]]
    return jnp.concatenate(parts, axis=-1)


def _unperm_cols(a):
    xs = [a[..., g * GROUP_COLS:g * GROUP_COLS + GROUP_X] for g in range(SSM_GROUPS)]
    bs = [a[..., g * GROUP_COLS + GROUP_X:g * GROUP_COLS + GROUP_X + D_STATE] for g in range(SSM_GROUPS)]
    cs = [a[..., g * GROUP_COLS + GROUP_X + D_STATE:(g + 1) * GROUP_COLS] for g in range(SSM_GROUPS)]
    return jnp.concatenate(xs + bs + cs, axis=-1)


def _dt_to_groups(dt):
    s_dim = dt.shape[0]
    t = dt[:, :SSM_HEADS].reshape(s_dim, SSM_GROUPS, HEADS_PER_GROUP).transpose(1, 0, 2)
    return jnp.pad(t, ((0, 0), (0, 0), (0, LANES - HEADS_PER_GROUP)))


def _dt_from_groups(dtg):
    s_dim = dtg.shape[1]
    return dtg[:, :, :HEADS_PER_GROUP].transpose(1, 0, 2).reshape(s_dim, SSM_HEADS)


def _pack_ssd_params(dt_bias, a_log, d_skip):
    rows = jnp.stack([p.reshape(SSM_GROUPS, HEADS_PER_GROUP) for p in (dt_bias, a_log, d_skip)], axis=1)
    return jnp.pad(rows, ((0, 0), (0, 8 - 3), (0, LANES - HEADS_PER_GROUP)))


def _unpack_ssd_params(par):
    return tuple(par[:, k, :HEADS_PER_GROUP].reshape(SSM_HEADS) for k in range(3))


Q = SSD_CHUNK


def _split3(v):
    hi = v.astype(BF16)
    r1 = v - hi.astype(F32)
    mid = r1.astype(BF16)
    lo = (r1 - mid.astype(F32)).astype(BF16)
    return hi, mid, lo


def _dot_l01(t01, v):
    return sum(_dot(t01, p) for p in _split3(v))


def _dot_r01(v, e01):
    return sum(_dot(p, e01) for p in _split3(v))


def _ssd_consts():
    row = lax.broadcasted_iota(jnp.int32, (Q, Q), 0)
    col = lax.broadcasted_iota(jnp.int32, (Q, Q), 1)
    causal = row >= col
    tril = causal.astype(BF16)
    triu = (col >= row).astype(BF16)
    er = lax.broadcasted_iota(jnp.int32, (LANES, GROUP_X), 0)
    ec = lax.broadcasted_iota(jnp.int32, (LANES, GROUP_X), 1) // SSM_HEAD_DIM
    expand = (er == ec).astype(BF16)
    rr = lax.broadcasted_iota(jnp.int32, (GROUP_X, LANES), 0) // SSM_HEAD_DIM
    rc = lax.broadcasted_iota(jnp.int32, (GROUP_X, LANES), 1)
    reduce = (rr == rc).astype(BF16)
    lane_head = lax.broadcasted_iota(jnp.int32, (Q, GROUP_X), 1) // SSM_HEAD_DIM
    return causal, tril, triu, expand, reduce, lane_head


def _ssd_common(xc_ref, dt_ref, par_ref, consts):
    causal, tril, _, expand, _, _ = consts
    par = par_ref[...]
    bias, alog, dsk = par[0:1, :], par[1:2, :], par[2:3, :]
    a_neg = -jnp.exp(alog)
    dtr = dt_ref[...] + bias
    dt = _softplus(dtr)
    s = _dot_l01(tril, dt * a_neg)
    dt_x = _dot_r01(dt, expand)
    s_x = _dot_r01(s, expand)
    dsk_x = _dot_r01(jnp.broadcast_to(dsk, (8, LANES)), expand)[0:1, :]
    blk = xc_ref[...]
    x = blk[:, :GROUP_X]
    bm = blk[:, GROUP_X:GROUP_X + D_STATE].astype(BF16)
    cm = blk[:, GROUP_X + D_STATE:].astype(BF16)
    xdt = x * dt_x
    g = _dot(cm, bm, _NT)
    return dict(a_neg=a_neg, dtr=dtr, dt=dt, s=s, s_t=s.T, dt_x=dt_x, s_x=s_x, dsk_x=dsk_x, x=x, bm=bm, cm=cm,
                xdt=xdt, g=g)


def _decay(v, r, causal):
    diff = v["s"][:, r:r + 1] - v["s_t"][r:r + 1, :]
    return jnp.exp(jnp.where(causal, diff, -jnp.inf))


def _ssd_specs(n_chunks, rev):
    cidx = (lambda c: n_chunks - 1 - c) if rev else (lambda c: c)
    xc = pl.BlockSpec((Q, GROUP_COLS), lambda g, c: (cidx(c), g))
    gx = pl.BlockSpec((Q, GROUP_X), lambda g, c: (cidx(c), g))
    dt = pl.BlockSpec((None, Q, LANES), lambda g, c: (g, cidx(c), 0))
    par = pl.BlockSpec((None, 8, LANES), lambda g, c: (g, 0, 0))
    nw = pl.BlockSpec((1, GROUP_X), lambda g, c: (0, g))
    hs = pl.BlockSpec((None, None, D_STATE, GROUP_X), lambda g, c: (cidx(c), g, 0, 0))
    return xc, gx, dt, par, nw, hs


def _ssd_fwd(xc, z, dtg, par, nw):
    s_dim = xc.shape[0]
    n_chunks = s_dim // Q
    xc_s, gx_s, dt_s, par_s, nw_s, hs_s = _ssd_specs(n_chunks, False)

    def body(xc_ref, z_ref, dt_ref, par_ref, nw_ref, y_ref, ys_ref, hs_ref, ht):
        @pl.when(pl.program_id(1) == 0)
        def _():
            ht[...] = jnp.zeros_like(ht)

        consts = _ssd_consts()
        causal, lane_head = consts[0], consts[5]
        v = _ssd_common(xc_ref, dt_ref, par_ref, consts)
        xdt_b = v["xdt"].astype(BF16)
        yd = jnp.zeros((Q, GROUP_X), F32)
        for r in range(HEADS_PER_GROUP):
            m = (v["g"] * _decay(v, r, causal)).astype(BF16)
            yd = yd + _dot(m, jnp.where(lane_head == r, xdt_b, jnp.zeros_like(xdt_b)))
        h = ht[...]
        hs_ref[...] = h
        yo = jnp.exp(v["s_x"]) * _dot(v["cm"], h.astype(BF16))
        y = yd + yo + v["dsk_x"] * v["x"]
        s_last = v["s_x"][Q - 1:Q, :]
        snew = _dot(v["bm"], (v["xdt"] * jnp.exp(s_last - v["s_x"])).astype(BF16), _TN)
        ht[...] = jnp.exp(s_last) * h + snew
        zz = z_ref[...]
        yg = y * (zz * _sigmoid(zz))
        y_ref[...] = y
        ys_ref[...] = _nrm(yg, nw_ref[...])[0].astype(BF16)

    return pl.pallas_call(
        body,
        grid=(SSM_GROUPS, n_chunks),
        in_specs=[xc_s, gx_s, dt_s, par_s, nw_s],
        out_specs=[gx_s, gx_s, hs_s],
        out_shape=[jax.ShapeDtypeStruct((s_dim, D_SSM), F32), jax.ShapeDtypeStruct((s_dim, D_SSM), BF16),
                   jax.ShapeDtypeStruct((n_chunks, SSM_GROUPS, D_STATE, GROUP_X), F32)],
        scratch_shapes=[pltpu.VMEM((D_STATE, GROUP_X), F32)],
        compiler_params=_cparams("parallel", "arbitrary"),
        name="ssd_fwd",
    )(xc, z, dtg, par, nw)


def _ssd_bwd(xc, z, dtg, par, nw, y, hs, dymix):
    s_dim = xc.shape[0]
    n_chunks = s_dim // Q
    xc_s, gx_s, dt_s, par_s, nw_s, hs_s = _ssd_specs(n_chunks, True)

    def body(xc_ref, z_ref, dt_ref, par_ref, nw_ref, y_ref, hs_ref, dys_ref,
             dxc_ref, dz_ref, ddt_ref, dpar_ref, dnw_ref, dht):
        @pl.when(pl.program_id(1) == 0)
        def _():
            dht[...] = jnp.zeros_like(dht)
            dpar_ref[...] = jnp.zeros_like(dpar_ref)
            dnw_ref[...] = jnp.zeros_like(dnw_ref)

        consts = _ssd_consts()
        causal, _, triu, _, reduce, lane_head = consts
        v = _ssd_common(xc_ref, dt_ref, par_ref, consts)
        x, bm, cm, xdt, s_x = v["x"], v["bm"], v["cm"], v["xdt"], v["s_x"]
        h = hs_ref[...]
        hb = h.astype(BF16)
        es_x = jnp.exp(s_x)
        yo = es_x * _dot(cm, hb)
        s_last = s_x[Q - 1:Q, :]
        e_x = jnp.exp(s_last - s_x)
        es_last = jnp.exp(s_last)

        yv, zz, nw_v = y_ref[...], z_ref[...], nw_ref[...]
        sg = _sigmoid(zz)
        gz = zz * sg
        _, n, rstd = _nrm(yv * gz, nw_v)
        dout = dys_ref[...]
        dyg, dnw = _nrm_bwd(dout, n, rstd, nw_v)
        dnw_ref[...] += dnw
        dy = dyg * gz
        dz_ref[...] = (dyg * yv * (sg * (1.0 + zz * (1.0 - sg)))).astype(BF16)

        dyb = dy.astype(BF16)
        xdt_b = xdt.astype(BF16)
        dhp = dht[...]
        dhpb = dhp.astype(BF16)
        lane = lax.broadcasted_iota(jnp.int32, (Q, LANES), 1)
        sub = lax.broadcasted_iota(jnp.int32, (LANES, Q), 0)
        dxdt = jnp.zeros((Q, GROUP_X), F32)
        dg = jnp.zeros((Q, Q), F32)
        ds = jnp.zeros((Q, LANES), F32)
        ds_t = jnp.zeros((LANES, Q), F32)
        for r in range(HEADS_PER_GROUP):
            dec = _decay(v, r, causal)
            mf = v["g"] * dec
            dyr = jnp.where(lane_head == r, dyb, jnp.zeros_like(dyb))
            dm = _dot(dyr, xdt_b, _NT)
            dxdt = dxdt + _dot(mf.astype(BF16), dyr, _TN)
            dg = dg + dm * dec
            dd = dm * mf
            ds = ds + jnp.where(lane == r, jnp.sum(dd, axis=1, keepdims=True), 0.0)
            ds_t = ds_t + jnp.where(sub == r, jnp.sum(dd, axis=0, keepdims=True), 0.0)
        ds = ds - ds_t.T
        dgb = dg.astype(BF16)
        dwb = (es_x * dy).astype(BF16)
        dcm = _dot(dgb, bm) + _dot(dwb, hb, _NT)
        dh_prev = _dot(cm, dwb, _TN)
        zst = _dot(bm, dhpb)
        xe = xdt * e_x
        dxdt = dxdt + e_x * zst
        dee = xe * zst
        dbm = _dot(dgb, cm, _TN) + _dot(xe.astype(BF16), dhpb, _NT)
        v_last = jnp.sum(dee, axis=0, keepdims=True) + es_last * jnp.sum(dhp * h, axis=0, keepdims=True)
        row_x = lax.broadcasted_iota(jnp.int32, (Q, GROUP_X), 0)
        tx = dy * yo - dee + jnp.where(row_x == Q - 1, v_last, 0.0)
        ds = ds + _dot_r01(tx, reduce)
        ddta = _dot_l01(triu, ds)
        ddt = ddta * v["a_neg"] + _dot_r01(dxdt * x, reduce)
        dalog = jnp.sum(ddta * v["dt"], axis=0, keepdims=True) * v["a_neg"]
        draw = jnp.where(lane < HEADS_PER_GROUP, ddt * _sigmoid(v["dtr"]), 0.0)
        dbias = jnp.sum(draw, axis=0, keepdims=True)
        ddsk = _dot_r01(jnp.broadcast_to(jnp.sum(dy * x, axis=0, keepdims=True), (8, GROUP_X)), reduce)[0:1, :]
        dht[...] = es_last * dhp + dh_prev
        dxc_ref[:, :GROUP_X] = dxdt * v["dt_x"] + v["dsk_x"] * dy
        dxc_ref[:, GROUP_X:GROUP_X + D_STATE] = dbm
        dxc_ref[:, GROUP_X + D_STATE:] = dcm
        ddt_ref[...] = draw
        dpar_ref[0:1, :] += dbias
        dpar_ref[1:2, :] += dalog
        dpar_ref[2:3, :] += ddsk

    return pl.pallas_call(
        body,
        grid=(SSM_GROUPS, n_chunks),
        in_specs=[xc_s, gx_s, dt_s, par_s, nw_s, gx_s, hs_s, gx_s],
        out_specs=[xc_s, gx_s, dt_s, par_s, nw_s],
        out_shape=[jax.ShapeDtypeStruct((s_dim, SSM_GROUPS * GROUP_COLS), F32),
                   jax.ShapeDtypeStruct((s_dim, D_SSM), BF16),
                   jax.ShapeDtypeStruct((SSM_GROUPS, s_dim, LANES), F32),
                   jax.ShapeDtypeStruct((SSM_GROUPS, 8, LANES), F32),
                   jax.ShapeDtypeStruct((1, D_SSM), F32)],
        scratch_shapes=[pltpu.VMEM((D_STATE, GROUP_X), F32)],
        compiler_params=_cparams("parallel", "arbitrary"),
        name="ssd_bwd",
    )(xc, z, dtg, par, nw, y, hs, dymix)


ATT_SCALE = ATT_HEAD_DIM ** -0.5
NEG_INF = -jnp.inf


def _band_masks():
    qi = lax.broadcasted_iota(jnp.int32, (ATT_BLOCK, ATT_BLOCK), 0)
    kj = lax.broadcasted_iota(jnp.int32, (ATT_BLOCK, ATT_BLOCK), 1)
    return kj <= qi, kj >= qi


WIN = ATT_BLOCK * DILATIONS[-1]
N_BLOCKS = WIN // ATT_BLOCK


def _rows(start, d):
    return pl.ds(start, ATT_BLOCK) if d == 1 else pl.ds(start, ATT_BLOCK, stride=d)


def _block_start(idx, d):
    return (idx // d) * (ATT_BLOCK * d) + idx % d


def _lane_bcast(col):
    return jnp.broadcast_to(col, (col.shape[0], LANES))


def _attn_fused_fwd(qkv):
    s_dim = qkv.shape[0]
    n_win = s_dim // WIN
    blk = (WIN, ATT_HEAD_DIM)
    prev = lambda w: jnp.maximum(w - 1, 0)

    def body(q_ref, kc_ref, kp_ref, vc_ref, vp_ref, y_ref, yf_ref, lse_ref, qf, kf, vf, acc, m_run, l_run):
        w, h = pl.program_id(0), pl.program_id(1)
        qf[...] = q_ref[...].astype(F32)
        kf[0:WIN, :] = kp_ref[...].astype(F32)
        kf[WIN:, :] = kc_ref[...].astype(F32)
        vf[0:WIN, :] = vp_ref[...].astype(F32)
        vf[WIN:, :] = vc_ref[...].astype(F32)
        own, before = _band_masks()

        for d in DILATIONS:
            def block(idx, carry, d=d):
                start = _block_start(idx, d)
                rows = _rows(start, d)
                q = qf[rows, :].astype(BF16)
                kc, vc = kf[_rows(WIN + start, d), :].astype(BF16), vf[_rows(WIN + start, d), :].astype(BF16)
                kp = kf[_rows(WIN + start - ATT_BLOCK * d, d), :].astype(BF16)
                vp = vf[_rows(WIN + start - ATT_BLOCK * d, d), :].astype(BF16)
                has_prev = (idx >= d) | (w > 0)
                sc = jnp.where(own, _dot(q, kc, _NT) * ATT_SCALE, NEG_INF)
                sp = jnp.where(before & has_prev, _dot(q, kp, _NT) * ATT_SCALE, NEG_INF)
                m_blk = jnp.maximum(jnp.max(sc, axis=1, keepdims=True), jnp.max(sp, axis=1, keepdims=True))
                if d == DILATIONS[0]:
                    m_new = m_blk
                else:
                    m_old = m_run[rows, :][:, 0:1]
                    m_new = jnp.maximum(m_old, m_blk)
                pc, pp = jnp.exp(sc - m_new), jnp.exp(sp - m_new)
                l_new = jnp.sum(pc, axis=1, keepdims=True) + jnp.sum(pp, axis=1, keepdims=True)
                o_new = _dot(pc.astype(BF16), vc) + _dot(pp.astype(BF16), vp)
                if d != DILATIONS[0]:
                    alpha = jnp.exp(m_old - m_new)
                    l_new = alpha * l_run[rows, :][:, 0:1] + l_new
                    o_new = alpha * acc[rows, :] + o_new
                m_run[rows, :] = _lane_bcast(m_new)
                l_run[rows, :] = _lane_bcast(l_new)
                acc[rows, :] = o_new
                return carry

            for idx in range(N_BLOCKS):
                block(idx, 0)

        l_all = l_run[...]
        y = acc[...] / l_all
        y_ref[...] = y.astype(BF16)
        yf_ref[...] = y
        @pl.when(h == 0)
        def _():
            lse_ref[...] = jnp.zeros_like(lse_ref)

        lane = lax.broadcasted_iota(jnp.int32, (WIN, LANES), 1)
        lse_ref[...] = jnp.where(lane == h, m_run[...] + jnp.log(l_all), lse_ref[...])

    win_scratch = lambda rows: pltpu.VMEM((rows, ATT_HEAD_DIM), F32)
    return pl.pallas_call(
        body,
        grid=(n_win, ATT_HEADS),
        in_specs=[pl.BlockSpec(blk, lambda w, h: (w, h)),
                  pl.BlockSpec(blk, lambda w, h: (w, ATT_HEADS + h)),
                  pl.BlockSpec(blk, lambda w, h: (prev(w), ATT_HEADS + h)),
                  pl.BlockSpec(blk, lambda w, h: (w, 2 * ATT_HEADS + h)),
                  pl.BlockSpec(blk, lambda w, h: (prev(w), 2 * ATT_HEADS + h))],
        out_specs=[pl.BlockSpec(blk, lambda w, h: (w, h)), pl.BlockSpec(blk, lambda w, h: (w, h)),
                   pl.BlockSpec((WIN, LANES), lambda w, h: (w, 0))],
        out_shape=[jax.ShapeDtypeStruct((s_dim, D_ATT), BF16), jax.ShapeDtypeStruct((s_dim, D_ATT), F32),
                   jax.ShapeDtypeStruct((s_dim, LANES), F32)],
        scratch_shapes=[win_scratch(WIN), win_scratch(2 * WIN), win_scratch(2 * WIN), win_scratch(WIN),
                        win_scratch(WIN), win_scratch(WIN)],
        compiler_params=_cparams("parallel", "arbitrary"),
        name="attn_fused_fwd",
    )(qkv, qkv, qkv, qkv, qkv)


def _attn_fused_bwd(qkv, dymix, y_att, lse, deps=()):
    s_dim = qkv.shape[0]
    n_win = s_dim // WIN
    blk = (WIN, ATT_HEAD_DIM)
    this = lambda w: jnp.minimum(w, n_win - 1)
    prev = lambda w: jnp.maximum(this(w) - 1, 0)
    n_dep = len(deps)

    def body(q_ref, kc_ref, kp_ref, vc_ref, vp_ref, dy_ref, y_ref, l_ref, *rest):
        dq_ref, dkv_ref = rest[n_dep:n_dep + 2]
        qf, kf, vf, dq_acc, dk_acc, dv_acc, ls_c, dl_c = rest[n_dep + 2:]
        h, w = pl.program_id(0), pl.program_id(1)
        slot, late = w % 2, 1 - w % 2

        @pl.when(w == 0)
        def _():
            dk_acc[...] = jnp.zeros_like(dk_acc)
            dv_acc[...] = jnp.zeros_like(dv_acc)

        @pl.when(w < n_win)
        def _():
            qf[...] = q_ref[...].astype(F32)
            kf[0:WIN, :] = kp_ref[...].astype(F32)
            kf[WIN:, :] = kc_ref[...].astype(F32)
            vf[0:WIN, :] = vp_ref[...].astype(F32)
            vf[WIN:, :] = vc_ref[...].astype(F32)
            lane = lax.broadcasted_iota(jnp.int32, (WIN, LANES), 1)
            ls_c[...] = _lane_bcast(jnp.sum(jnp.where(lane == h, l_ref[...], 0.0), axis=1, keepdims=True))
            dl_c[...] = _lane_bcast(jnp.sum(dy_ref[...] * y_ref[...], axis=1, keepdims=True))
            dq_acc[...] = jnp.zeros_like(dq_acc)
            dk_acc[slot] = jnp.zeros((WIN, ATT_HEAD_DIM), F32)
            dv_acc[slot] = jnp.zeros((WIN, ATT_HEAD_DIM), F32)
            own, before = _band_masks()

            def probs(q, k, v, dy, lse_col, dl_col, mask):
                p = jnp.exp(jnp.where(mask, _dot(q, k, _NT) * ATT_SCALE - lse_col, NEG_INF))
                ds = p * (_dot(dy, v, _NT) - dl_col)
                return p.astype(BF16), ds.astype(BF16)

            for d in DILATIONS:
                for idx in range(N_BLOCKS):
                    start = _block_start(idx, d)
                    rows = _rows(start, d)
                    q, dy = qf[rows, :].astype(BF16), dy_ref[rows, :].astype(BF16)
                    lse_col, dl_col = ls_c[rows, :][:, 0:1], dl_c[rows, :][:, 0:1]
                    kc, vc = kf[_rows(WIN + start, d), :].astype(BF16), vf[_rows(WIN + start, d), :].astype(BF16)
                    kp = kf[_rows(WIN + start - ATT_BLOCK * d, d), :].astype(BF16)
                    vp = vf[_rows(WIN + start - ATT_BLOCK * d, d), :].astype(BF16)
                    pc, dsc = probs(q, kc, vc, dy, lse_col, dl_col, own)
                    pp, dsp = probs(q, kp, vp, dy, lse_col, dl_col, before & ((idx >= d) | (w > 0)))
                    dq_acc[rows, :] += (_dot(dsc, kc) + _dot(dsp, kp)) * ATT_SCALE
                    dk_acc[slot, rows, :] += _dot(dsc, q, _TN) * ATT_SCALE
                    dv_acc[slot, rows, :] += _dot(pc, dy, _TN)
                    if idx >= d:
                        prows = _rows(start - ATT_BLOCK * d, d)
                        dk_acc[slot, prows, :] += _dot(dsp, q, _TN) * ATT_SCALE
                        dv_acc[slot, prows, :] += _dot(pp, dy, _TN)
                    else:
                        prows = _rows(WIN + start - ATT_BLOCK * d, d)
                        dk_acc[late, prows, :] += _dot(dsp, q, _TN) * ATT_SCALE
                        dv_acc[late, prows, :] += _dot(pp, dy, _TN)
            dq_ref[...] = dq_acc[...].astype(BF16)

        @pl.when(w > 0)
        def _():
            dkv_ref[0] = dk_acc[late].astype(BF16)
            dkv_ref[1] = dv_acc[late].astype(BF16)

    win_scratch = lambda *shape: pltpu.VMEM(shape + (ATT_HEAD_DIM,), F32)
    cur = lambda c: pl.BlockSpec(blk, lambda h, w: (this(w), c + h))
    before_spec = lambda c: pl.BlockSpec(blk, lambda h, w: (prev(w), c + h))
    return pl.pallas_call(
        body,
        grid=(ATT_HEADS, n_win + 1),
        in_specs=[cur(0), cur(ATT_HEADS), before_spec(ATT_HEADS), cur(2 * ATT_HEADS), before_spec(2 * ATT_HEADS),
                  cur(ATT_HEADS), cur(0), pl.BlockSpec((WIN, LANES), lambda h, w: (this(w), 0))] + [ANY] * n_dep,
        out_specs=[cur(0), pl.BlockSpec((2, WIN, ATT_HEAD_DIM), lambda h, w: (0, jnp.maximum(w - 1, 0), h))],
        out_shape=[jax.ShapeDtypeStruct((s_dim, D_ATT), BF16), jax.ShapeDtypeStruct((2, s_dim, D_ATT), BF16)],
        scratch_shapes=[win_scratch(WIN), win_scratch(2 * WIN), win_scratch(2 * WIN), win_scratch(WIN),
                        win_scratch(2, WIN), win_scratch(2, WIN), win_scratch(WIN), win_scratch(WIN)],
        compiler_params=_cparams("parallel", "arbitrary"),
        name="attn_fused_bwd",
    )(qkv, qkv, qkv, qkv, qkv, dymix, y_att, lse, *deps)


def _adamw(w, g, m, v, name):
    def fn(wb, gb, mb, vb):
        m2 = ADAM_B1 * mb + (1.0 - ADAM_B1) * gb
        v2 = ADAM_B2 * vb + (1.0 - ADAM_B2) * (gb * gb)
        m_hat = m2 / (1.0 - ADAM_B1 ** ADAM_STEP)
        v_hat = v2 / (1.0 - ADAM_B2 ** ADAM_STEP)
        delta = -ADAM_LR * (m_hat / (jnp.sqrt(v_hat) + ADAM_EPS) + ADAM_WD * wb)
        return delta, m2, v2
    cols = w.shape[1]
    tr = 128 if w.shape[0] % 128 == 0 else w.shape[0]
    return _rowcall(fn, [w, g, m, v], [], [(cols, F32)] * 3, [], name=name, tr=tr)


ANY = pl.BlockSpec(memory_space=pl.ANY)


def _position():
    x, y, c = lax.axis_index("x"), lax.axis_index("y"), lax.axis_index("c")
    chips = [(1 - x, y), (x, 1 - y), (1 - x, 1 - y)]
    return x, y, c, chips


def _remote(src, dst, send_sem, recv_sem, device):
    return pltpu.make_async_remote_copy(src_ref=src, dst_ref=dst, send_sem=send_sem, recv_sem=recv_sem,
                                        device_id=device, device_id_type=MESH)


def _handshake(peers):
    barrier = pltpu.get_barrier_semaphore()
    for p in peers:
        pl.semaphore_signal(barrier, inc=1, device_id=p, device_id_type=MESH)
    pl.semaphore_wait(barrier, len(peers))


def _gather_shards_async(shards, collective_id, name):
    n = len(shards)
    srcs = [jax.new_ref(s, memory_space=pltpu.MemorySpace.HBM) for s in shards]
    dsts = [jax.empty_ref(jax.ShapeDtypeStruct((N_CHIPS,) + s.shape, s.dtype), memory_space=pltpu.MemorySpace.HBM)
            for s in shards]

    @pl.kernel(mesh=plsc.ScalarSubcoreMesh(axis_name="seq", num_cores=1), name=name,
               scratch_types=(pltpu.SemaphoreType.DMA((6 * n,)), pltpu.SemaphoreType.DMA((6 * n,))),
               compiler_params=pltpu.CompilerParams(collective_id=collective_id))
    def launch(send_sems, recv_sems):
        x, y, c, chips = _position()
        sibling = (x, y, 1 - c)
        _handshake([(chip[0], chip[1], c) for chip in chips] + [sibling])

        def half(a, j, cc):
            h = shards[a].shape[0] // 2
            return dsts[a].at[j, pl.ds(cc * h, h), :]

        sent = []
        for a in range(n):
            h = shards[a].shape[0] // 2
            for j, chip in enumerate(chips):
                cp = _remote(srcs[a].at[pl.ds(c * h, h), :], half(a, 2 * x + y, c), send_sems.at[6 * a + j],
                             recv_sems.at[6 * a + j], (chip[0], chip[1], c))
                cp.start()
                sent.append(cp)
        for a in range(n):
            for j, chip in enumerate(chips):
                landed = half(a, 2 * chip[0] + chip[1], c)
                _remote(landed, landed, send_sems.at[6 * a + j], recv_sems.at[6 * a + j], (x, y, c)).wait_recv()
                cp = _remote(landed, landed, send_sems.at[6 * a + 3 + j], recv_sems.at[6 * a + 3 + j], sibling)
                cp.start()
                sent.append(cp)
        for a in range(n):
            for j, chip in enumerate(chips):
                handed = half(a, 2 * chip[0] + chip[1], 1 - c)
                _remote(handed, handed, send_sems.at[6 * a + 3 + j], recv_sems.at[6 * a + 3 + j], (x, y, c)).wait_recv()
        for cp in sent:
            cp.wait_send()

    launch()
    return [d[...] for d in dsts]


IN_COLS = {"z": (0, D_SSM), "xbc": (D_SSM, D_SSM + D_XBC), "dt": (D_SSM + D_XBC, D_SSM + D_XBC + SSM_HEADS),
           "qkv": (D_SSM + D_XBC + SSM_HEADS, D_IN_PROJ)}


def _cols_from_quarters(quarters, lo, hi):
    parts = []
    for q in range(N_CHIPS):
        a, b = max(lo, q * W_IN_SHARD), min(hi, (q + 1) * W_IN_SHARD)
        if a < b:
            parts.append(quarters[q][:, a - q * W_IN_SHARD:b - q * W_IN_SHARD])
    return parts[0] if len(parts) == 1 else jnp.concatenate(parts, axis=1)


def _quarters_from_cols(pieces):
    quarters = []
    for q in range(N_CHIPS):
        parts = []
        for name, (lo, hi) in IN_COLS.items():
            a, b = max(lo, q * W_IN_SHARD), min(hi, (q + 1) * W_IN_SHARD)
            if a < b:
                parts.append(pieces[name][:, a - lo:b - lo])
        quarters.append(jnp.concatenate(parts, axis=1))
    return jnp.stack(quarters)


def _by_chip(own, fetched):
    me = 2 * lax.axis_index("x") + lax.axis_index("y")
    return lax.dynamic_update_slice(fetched, own[None], (me, 0, 0))


def _add_sibling(grad, got, place, name, deps=()):
    nq, rows, cols = grad.shape
    h = rows // 2
    tr = 128
    nb = h // tr

    def body(place_ref, a_ref, b_ref, *rest):
        own_ref, ob_ref = rest[len(deps):]
        total = a_ref[...] + b_ref[...]
        ob_ref[...] = total.astype(BF16)

        @pl.when(pl.program_id(1) == place_ref[1])
        def _():
            own_ref[...] = total

    return pl.pallas_call(
        body,
        grid_spec=pltpu.PrefetchScalarGridSpec(
            num_scalar_prefetch=1, grid=(nb, nq),
            in_specs=[pl.BlockSpec((None, tr, cols), lambda i, q, p: (q, p[0] * nb + i, 0)),
                      pl.BlockSpec((None, tr, cols), lambda i, q, p: (q, i, 0))] + [ANY] * len(deps),
            out_specs=[pl.BlockSpec((tr, cols), lambda i, q, p: (i, 0)),
                       pl.BlockSpec((None, tr, cols), lambda i, q, p: (q, i, 0))]),
        out_shape=[jax.ShapeDtypeStruct((h, cols), F32), jax.ShapeDtypeStruct((nq, h, cols), BF16)],
        compiler_params=_cparams("parallel", "arbitrary"),
        name=name,
    )(place, grad, got, *deps)


def _add_chips(part, got, name, deps=()):
    h, cols = part.shape
    tr = 128

    def body(p_ref, g0_ref, g1_ref, g2_ref, *rest):
        o_ref = rest[len(deps)]
        o_ref[...] = ((p_ref[...] + g0_ref[...].astype(F32)) + g1_ref[...].astype(F32)) + g2_ref[...].astype(F32)

    got_spec = lambda j: pl.BlockSpec((None, tr, cols), lambda i: (j, i, 0))
    row_spec = pl.BlockSpec((tr, cols), lambda i: (i, 0))
    return pl.pallas_call(
        body,
        grid=(h // tr,),
        in_specs=[row_spec, got_spec(0), got_spec(1), got_spec(2)] + [ANY] * len(deps),
        out_specs=row_spec,
        out_shape=jax.ShapeDtypeStruct((h, cols), F32),
        compiler_params=_cparams("parallel"),
        name=name,
    )(part, got, got, got, *deps)


def _sequencer_exchange(src, out_shape, collective_id, name, plan, n_copies):
    src_ref = jax.new_ref(src, memory_space=pltpu.MemorySpace.HBM)
    dst_ref = jax.empty_ref(out_shape, memory_space=pltpu.MemorySpace.HBM)

    @pl.kernel(mesh=plsc.ScalarSubcoreMesh(axis_name="seq", num_cores=1), name=name,
               scratch_types=(pltpu.SemaphoreType.DMA((n_copies,)), pltpu.SemaphoreType.DMA((n_copies,))),
               compiler_params=pltpu.CompilerParams(collective_id=collective_id))
    def launch(send_sems, recv_sems):
        x, y, c, chips = _position()
        copies = plan(src_ref, dst_ref, x, y, c, chips)
        _handshake([peer for _, _, peer in copies])
        started = []
        for k, (s, d, peer) in enumerate(copies):
            cp = _remote(s, d, send_sems.at[k], recv_sems.at[k], peer)
            cp.start()
            started.append(cp)
        for cp in started:
            cp.wait()

    launch()
    return dst_ref[...]


class _AsyncReduceScatter:
    def __init__(self, grad, nm, first_id):
        self.grad, self.nm, self.first_id = grad, nm, first_id
        nq, rows, cols = grad.shape
        h = self.h = rows // 2

        def to_sibling(s, d, x, y, c, chips):
            return [(s.at[:, pl.ds((1 - c) * h, h), :], d, (x, y, 1 - c))]

        self.from_sibling = _sequencer_exchange(grad, jax.ShapeDtypeStruct((nq, h, cols), F32), first_id,
                                                f"rs_sibling_{nm}", to_sibling, 1)

    def sibling_sum(self, not_before=()):
        cols = self.grad.shape[2]
        place = jnp.stack([lax.axis_index("c"), 2 * lax.axis_index("x") + lax.axis_index("y")]).astype(jnp.int32)
        self.part, self.part_b = _add_sibling(self.grad, self.from_sibling, place, f"add_sibling_{self.nm}", not_before)

        def to_chips(s, d, x, y, c, chips):
            return [(s.at[2 * chip[0] + chip[1]], d.at[j], (chip[0], chip[1], c)) for j, chip in enumerate(chips)]

        self.from_chips = _sequencer_exchange(self.part_b, jax.ShapeDtypeStruct((3, self.h, cols), BF16),
                                              self.first_id + 1, f"rs_quarters_{self.nm}", to_chips, 3)
        return self.part_b

    def chip_sum(self, not_before=()):
        cols = self.grad.shape[2]
        self.half = _add_chips(self.part, self.from_chips, f"add_chips_{self.nm}", not_before)

        def whole_to_sibling(s, d, x, y, c, chips):
            return [(s, d, (x, y, 1 - c))]

        self.other = _sequencer_exchange(self.half, jax.ShapeDtypeStruct((self.h, cols), F32), self.first_id + 2,
                                         f"rs_share_{self.nm}", whole_to_sibling, 1)
        return self.half

    def share(self):
        return self.half, self.other


def _after(x, deps, name):
    def body(x_ref, *rest):
        rest[-1][...] = x_ref[...]

    vm = pl.BlockSpec(memory_space=pltpu.VMEM)
    return pl.pallas_call(body, in_specs=[vm] + [ANY] * len(deps), out_specs=vm,
                          out_shape=jax.ShapeDtypeStruct(x.shape, x.dtype), name=name)(x, *deps)


def _adamw_halves(w, mine, other, m, v, name):
    rows, cols = w.shape
    tr = 128
    nb = rows // 2 // tr
    c_arr = lax.axis_index("c").astype(jnp.int32).reshape(1)

    def body(c_ref, w_ref, a_ref, b_ref, m_ref, v_ref, g_out, d_out, m_out, v_out):
        is_mine = (pl.program_id(0) // nb) == c_ref[0]
        g = jnp.where(is_mine, a_ref[...], b_ref[...])
        wb, mb, vb = w_ref[...], m_ref[...], v_ref[...]
        m2 = ADAM_B1 * mb + (1.0 - ADAM_B1) * g
        v2 = ADAM_B2 * vb + (1.0 - ADAM_B2) * (g * g)
        m_hat = m2 / (1.0 - ADAM_B1 ** ADAM_STEP)
        v_hat = v2 / (1.0 - ADAM_B2 ** ADAM_STEP)
        g_out[...] = g
        d_out[...] = -ADAM_LR * (m_hat / (jnp.sqrt(v_hat) + ADAM_EPS) + ADAM_WD * wb)
        m_out[...] = m2
        v_out[...] = v2

    full = pl.BlockSpec((tr, cols), lambda i, c: (i, 0))
    mine_spec = pl.BlockSpec((tr, cols), lambda i, c: (jnp.where(i // nb == c[0], i % nb, 0), 0))
    other_spec = pl.BlockSpec((tr, cols), lambda i, c: (jnp.where(i // nb == c[0], 0, i % nb), 0))
    return pl.pallas_call(
        body,
        grid_spec=pltpu.PrefetchScalarGridSpec(
            num_scalar_prefetch=1, grid=(rows // tr,),
            in_specs=[full, mine_spec, other_spec, full, full], out_specs=[full] * 4),
        out_shape=[jax.ShapeDtypeStruct((rows, cols), F32)] * 4,
        compiler_params=_cparams("parallel"),
        name=name,
    )(c_arr, w, mine, other, m, v)


def _adamw_halves_t(w_t, mine_t, other_t, m_t, v_t, name):
    cols, rows = w_t.shape
    tr = cols // 11
    assert tr * 11 == cols and tr % 8 == 0
    c_arr = lax.axis_index("c").astype(jnp.int32).reshape(1)

    def body(c_ref, w_ref, a_ref, b_ref, m_ref, v_ref, g_out, d_out, m_out, v_out):
        first = c_ref[0] == 0
        a, b = a_ref[...], b_ref[...]
        g = jnp.concatenate([jnp.where(first, a, b), jnp.where(first, b, a)], axis=1)
        wb, mb, vb = w_ref[...], m_ref[...], v_ref[...]
        m2 = ADAM_B1 * mb + (1.0 - ADAM_B1) * g
        v2 = ADAM_B2 * vb + (1.0 - ADAM_B2) * (g * g)
        m_hat = m2 / (1.0 - ADAM_B1 ** ADAM_STEP)
        v_hat = v2 / (1.0 - ADAM_B2 ** ADAM_STEP)
        g_out[...] = g
        d_out[...] = -ADAM_LR * (m_hat / (jnp.sqrt(v_hat) + ADAM_EPS) + ADAM_WD * wb)
        m_out[...] = m2
        v_out[...] = v2

    full = pl.BlockSpec((tr, rows), lambda i, c: (i, 0))
    half = pl.BlockSpec((tr, rows // 2), lambda i, c: (i, 0))
    return pl.pallas_call(
        body,
        grid_spec=pltpu.PrefetchScalarGridSpec(
            num_scalar_prefetch=1, grid=(cols // tr,),
            in_specs=[full, half, half, full, full], out_specs=[full] * 4),
        out_shape=[jax.ShapeDtypeStruct((cols, rows), F32)] * 4,
        compiler_params=_cparams("parallel"),
        name=name,
    )(c_arr, w_t, mine_t, other_t, m_t, v_t)


def _all_sum_small(v):
    n_dev = 8

    def body(v_ref, o_ref, gath, send_sems, recv_sems):
        x, y, c, _ = _position()
        me = 4 * x + 2 * y + c
        gath[me] = v_ref[...]
        copies = []
        for k in range(1, n_dev):
            peer = tuple(1 - p if (k >> s) & 1 else p for p, s in ((x, 2), (y, 1), (c, 0)))
            cp = _remote(v_ref, gath.at[me], send_sems.at[k - 1], recv_sems.at[k - 1], peer)
            cp.start()
            copies.append(cp)
        for cp in copies:
            cp.wait()
        acc = gath[0]
        for i in range(1, n_dev):
            acc = acc + gath[i]
        o_ref[...] = acc

    vm = pl.BlockSpec(memory_space=pltpu.VMEM)
    return pl.pallas_call(
        body,
        in_specs=[vm],
        out_specs=vm,
        out_shape=jax.ShapeDtypeStruct(v.shape, F32),
        scratch_shapes=[pltpu.VMEM((n_dev,) + v.shape, F32), pltpu.SemaphoreType.DMA((n_dev - 1,)),
                        pltpu.SemaphoreType.DMA((n_dev - 1,))],
        name="all_sum_small",
    )(v)


def _pack_rows(vectors):
    rows = []
    for v in vectors:
        flat = v.reshape(-1).astype(F32)
        rows.append(jnp.pad(flat, (0, (-flat.shape[0]) % LANES)).reshape(-1, LANES))
    out = jnp.concatenate(rows, axis=0)
    return jnp.pad(out, ((0, (-out.shape[0]) % 8), (0, 0)))


def _unpack_rows(packed, shapes):
    outs, r = [], 0
    for shp in shapes:
        size = math.prod(shp)
        nr = -(-size // LANES)
        outs.append(packed[r:r + nr].reshape(-1)[:size].reshape(shp))
        r += nr
    return outs


def _relu_sq(acc):
    r = jnp.maximum(acc, 0.0)
    return r, r * r


def _relu_sq_bwd(acc, r):
    return (acc * (2.0 * r.astype(F32)),)


def kernel(x, norm_mix_pre, w_in, conv_w, conv_b, dt_bias, a_log, d_skip, ssm_norm_w, w_out, norm_mix_post, norm_mlp_pre, w_up, w_down, norm_mlp_post, loss_target, m_norm_mix_pre, m_w_in, m_conv_w, m_conv_b, m_dt_bias, m_a_log, m_d_skip, m_ssm_norm_w, m_w_out, m_norm_mix_post, m_norm_mlp_pre, m_w_up, m_w_down, m_norm_mlp_post, v_norm_mix_pre, v_w_in, v_conv_w, v_conv_b, v_dt_bias, v_a_log, v_d_skip, v_ssm_norm_w, v_w_out, v_norm_mix_post, v_norm_mlp_pre, v_w_up, v_w_down, v_norm_mlp_post):
    s_dim = x.shape[1]
    xs, target = x[0], loss_target[0]
    chip = 2 * lax.axis_index("x") + lax.axis_index("y")

    own = [w_in[0].astype(BF16), w_out[0].astype(BF16), w_up[0].astype(BF16), w_down[0].astype(BF16)]
    fetched_in = _gather_shards_async(own[:1], 14, "gather_w_in")[0]
    conv_cols = D_XBC // N_CHIPS
    conv_placed = lax.dynamic_update_slice(jnp.zeros((8, D_XBC), F32), 0.5 * conv_w[0], (0, chip * conv_cols))
    conv_full = _all_sum_small(conv_placed.reshape(-1, LANES)).reshape(8, D_XBC)
    w8 = _perm_cols(conv_full.at[CONV_WIDTH].set(conv_b[0]))
    u = _pre_norm(xs, norm_mix_pre)
    fetched_in, u, w8, *rest = lax.optimization_barrier((fetched_in, u, w8, *own[1:]))
    fetched = [fetched_in] + _gather_shards_async(rest, 1, "gather_rest")
    g_in, g_out, g_up, g_down = [_by_chip(o, f) for o, f in zip(own, fetched)]
    w_z = _cols_from_quarters(g_in, *IN_COLS["z"])
    w_xbc = _perm_cols(_cols_from_quarters(g_in, *IN_COLS["xbc"]))
    w_dt = jnp.pad(_cols_from_quarters(g_in, *IN_COLS["dt"]), ((0, 0), (0, LANES - SSM_HEADS)))
    w_qkv = _cols_from_quarters(g_in, *IN_COLS["qkv"])
    w_out_full = g_out.reshape(D_MIX, D_MODEL)
    w_down_full = g_down.reshape(D_FF, D_MODEL)

    z = _matmul([(u, w_z, TK)], "nn", [F32], name="proj_z")
    xbc = _matmul([(u, w_xbc, TK)], "nn", [F32], name="proj_xbc")
    dt_raw = _matmul([(u, w_dt, TK)], "nn", [F32], name="proj_dt")
    qkv = _matmul([(u, w_qkv, TK)], "nn", [BF16], name="proj_qkv")
    xc = _conv_fwd(xbc, w8)
    dtg = _dt_to_groups(dt_raw)
    par = _pack_ssd_params(dt_bias[0], a_log[0], d_skip[0])
    y, y_ssm, states = _ssd_fwd(xc, z, dtg, par, ssm_norm_w)
    y_att, y_att_f32, lse = _attn_fused_fwd(qkv)
    y_mix = jnp.concatenate([y_ssm, y_att], axis=1)
    mix = _matmul([(y_mix, w_out_full, TK)], "nn", [F32], name="out_proj")
    h1, u2 = _post_pre_norm(xs, mix, norm_mix_post, norm_mlp_pre)
    hid, act = _matmul([(u2, g_up, TK)], "nn", [BF16, BF16], name="mlp_up", epilogue=_relu_sq)
    ff = _matmul([(act, w_down_full, TK)], "nn", [F32], name="mlp_down")
    dh2, dff, d_g4, loss_part = _tail(ff, h1, target, norm_mlp_post)

    dhid = _matmul([(dff, w_down_full, TK)], "nt", [BF16], name="mlp_down_dx", epilogue=_relu_sq_bwd, extras=[hid])
    weights = {"norm_mix_pre": (norm_mix_pre, m_norm_mix_pre, v_norm_mix_pre), "w_in": (w_in, m_w_in, v_w_in),
               "conv_w": (conv_w, m_conv_w, v_conv_w), "conv_b": (conv_b, m_conv_b, v_conv_b),
               "dt_bias": (dt_bias, m_dt_bias, v_dt_bias), "a_log": (a_log, m_a_log, v_a_log),
               "d_skip": (d_skip, m_d_skip, v_d_skip), "ssm_norm_w": (ssm_norm_w, m_ssm_norm_w, v_ssm_norm_w),
               "w_out": (w_out, m_w_out, v_w_out), "norm_mix_post": (norm_mix_post, m_norm_mix_post, v_norm_mix_post),
               "norm_mlp_pre": (norm_mlp_pre, m_norm_mlp_pre, v_norm_mlp_pre), "w_up": (w_up, m_w_up, v_w_up),
               "w_down": (w_down, m_w_down, v_w_down),
               "norm_mlp_post": (norm_mlp_post, m_norm_mlp_post, v_norm_mlp_post)}
    grads, delta, new_m, new_v = {}, {}, {}, {}

    def adamw_big(n, halves):
        w, m, v = weights[n]
        g_, d_, m_, v_ = _adamw_halves(w[0], halves[0], halves[1], m[0], v[0], f"adamw_{n}")
        grads[n], delta[n], new_m[n], new_v[n] = g_[None], d_[None], m_[None], v_[None]

    dw_down = _matmul([(act, dff, TK)], "tn", [F32], name="mlp_down_dw")
    rs_down = _AsyncReduceScatter(dw_down.reshape(N_CHIPS, D_FF // N_CHIPS, D_MODEL), "w_down", 11)
    dw_up = _matmul([(u2, dhid, TK)], "tn", [F32], name="mlp_up_dw", deps=[dw_down], out_quarters=True)
    rs_up = _AsyncReduceScatter(dw_up, "w_up", 8)
    du2 = _matmul([(dhid, g_up, TK)], "nt", [F32], name="mlp_up_dx",
                  deps=[rs_down.sibling_sum(not_before=[dw_up])])
    dh1, dmix, d_g3, d_g2 = _mid_bwd(du2, h1, dh2, mix, norm_mix_post, norm_mlp_pre,
                                     deps=[rs_up.sibling_sum(not_before=[du2])])
    dymix = _matmul([(dmix, w_out_full, TK)], "nt", [F32], name="out_proj_dx")
    dw_out = _matmul([(y_mix, dmix, TK)], "tn", [F32], name="out_proj_dw")
    rs_out = _AsyncReduceScatter(dw_out.reshape(N_CHIPS, D_MIX // N_CHIPS, D_MODEL), "w_out", 5)
    dq, dkv = _attn_fused_bwd(qkv, dymix, y_att_f32, lse)
    dqkv = jnp.concatenate([dq[None], dkv], axis=0)
    par_late = _after(par, [rs_down.chip_sum(not_before=[dqkv]), rs_out.sibling_sum(not_before=[dymix])],
                      "after_w_down")
    dxc, dz, ddtg, dpar, d_nw = _ssd_bwd(xc, z, dtg, par_late, ssm_norm_w, y, states, dymix)
    g_down = rs_down.share()
    dxbc, dw8 = _conv_bwd(xbc, _after(w8, [*g_down, rs_up.chip_sum(not_before=[dxc])], "after_w_up"), dxc)
    ddt = jnp.pad(_dt_from_groups(ddtg), ((0, 0), (0, LANES - SSM_HEADS))).astype(BF16)
    g_up = rs_up.share()
    dw_z = _matmul([(u, dz, TK)], "tn", [F32], name="proj_z_dw")
    dw_xbc = _matmul([(u, dxbc, TK)], "tn", [F32], name="proj_xbc_dw",
                     deps=[*g_up, rs_out.chip_sum(not_before=[dxbc])])
    g_out = rs_out.share()
    dw_dt = _matmul([(u, ddt, TK)], "tn", [F32], name="proj_dt_dw")
    dw_qkv = _matmul([(u, dqkv, TK)], "tn", [F32], name="proj_qkv_dw")
    dw_in = _quarters_from_cols({"z": dw_z, "xbc": _unperm_cols(dw_xbc), "dt": dw_dt[:, :SSM_HEADS], "qkv": dw_qkv})
    rs_in = _AsyncReduceScatter(dw_in, "w_in", 2)
    adamw_big("w_down", g_down)
    adamw_big("w_up", g_up)
    rs_in.sibling_sum(not_before=[delta["w_up"]])
    du = _matmul([(dz, w_z, TK_MULTI), (dxbc, w_xbc, TK_MULTI), (dqkv, w_qkv, TK_MULTI), (ddt, w_dt, LANES)], "nt",
                 [F32], name="proj_dx", tn=TN_MULTI, deps=[*g_out, rs_in.part_b])
    grad_x, d_g1 = _first_bwd(du, xs, dh1, norm_mix_pre)
    adamw_big("w_out", g_out)
    rs_in.chip_sum(not_before=[grad_x, delta["w_out"]])

    dconv = _unperm_cols(dw8)
    d_bias, d_alog, d_dskip = _unpack_ssd_params(dpar)
    small_shapes = [(1, D_MODEL), (CONV_WIDTH, D_XBC), (1, D_XBC), (1, SSM_HEADS), (1, SSM_HEADS), (1, SSM_HEADS),
                    (1, D_SSM), (1, D_MODEL), (1, D_MODEL), (1, D_MODEL), (1, LANES)]
    summed = _unpack_rows(
        _all_sum_small(_pack_rows([d_g1, dconv[:CONV_WIDTH], dconv[CONV_WIDTH:CONV_WIDTH + 1], d_bias, d_alog,
                                   d_dskip, d_nw, d_g2, d_g3, d_g4, loss_part])), small_shapes)
    (g_g1, g_conv_full, g_conv_b, g_bias, g_alog, g_dskip, g_nw, g_g2, g_g3, g_g4, loss_row) = summed
    loss = loss_row[0, 0]
    g_conv_w = lax.dynamic_slice(g_conv_full, (0, chip * conv_cols), (CONV_WIDTH, conv_cols))[None]

    grads.update({"norm_mix_pre": g_g1, "conv_w": g_conv_w, "conv_b": g_conv_b, "dt_bias": g_bias,
                  "a_log": g_alog, "d_skip": g_dskip, "ssm_norm_w": g_nw, "norm_mix_post": g_g2,
                  "norm_mlp_pre": g_g3, "norm_mlp_post": g_g4})
    order = list(weights)
    small_names = [n for n in order if n not in ("w_in", "w_out", "w_up", "w_down")]
    small_w_shapes = [weights[n][0].shape for n in small_names]
    packed = [_pack_rows([weights[n][k] for n in small_names]) for k in range(3)]
    packed_g = _pack_rows([grads[n].reshape(weights[n][0].shape) for n in small_names])
    sd, sm, sv = _adamw(packed[0], packed_g, packed[1], packed[2], "adamw_small")
    for k, n in enumerate(small_names):
        grads[n] = grads[n].reshape(weights[n][0].shape)
    for res, pk in ((delta, sd), (new_m, sm), (new_v, sv)):
        for n, val in zip(small_names, _unpack_rows(pk, small_w_shapes)):
            res[n] = val
    mine, other = rs_in.share()
    w_t, m_t, v_t = [jnp.swapaxes(a[0], 0, 1) for a in weights["w_in"]]
    results_t = _adamw_halves_t(w_t, mine.T, other.T, m_t, v_t, "adamw_w_in")
    grads["w_in"], delta["w_in"], new_m["w_in"], new_v["w_in"] = [jnp.swapaxes(r, 0, 1)[None] for r in results_t]

    return (loss, grad_x[None], *[grads[n] for n in order], *[delta[n] for n in order],
            *[new_m[n] for n in order], *[new_v[n] for n in order])
```

```python
import math

import numpy as np
import jax
import jax.numpy as jnp
from jax import lax
from jax.experimental import pallas as pl
from jax.experimental.pallas import tpu as pltpu
from jax.experimental.pallas import tpu_sc as plsc

F32 = jnp.float32
BF16 = jnp.bfloat16

D_MODEL = 2048
SSM_HEAD_DIM = 64
SSM_GROUPS = 8
HEADS_PER_GROUP = 4
SSM_HEADS = SSM_GROUPS * HEADS_PER_GROUP
D_SSM = SSM_HEADS * SSM_HEAD_DIM
D_STATE = 128
CONV_WIDTH = 4
SSD_CHUNK = 128
D_XBC = D_SSM + 2 * SSM_GROUPS * D_STATE
GROUP_X = HEADS_PER_GROUP * SSM_HEAD_DIM
GROUP_COLS = GROUP_X + 2 * D_STATE
ATT_HEAD_DIM = 128
ATT_HEADS = 16
D_ATT = ATT_HEADS * ATT_HEAD_DIM
DILATIONS = (1, 4, 16)
ATT_BLOCK = 128
D_MIX = D_SSM + D_ATT
D_IN_PROJ = D_SSM + D_XBC + SSM_HEADS + 3 * D_ATT
D_FF = 4 * D_MODEL
EPS = 1e-6
N_CHIPS = 4
W_IN_SHARD = D_IN_PROJ // N_CHIPS

ADAM_LR = 0.001
ADAM_B1 = 0.9
ADAM_B2 = 0.999
ADAM_EPS = 1e-08
ADAM_WD = 0.01
ADAM_STEP = 10

LANES = 128
VMEM_LIMIT = 48 * 1024 * 1024
MESH = pl.DeviceIdType.MESH

_NN = (((1,), (0,)), ((), ()))
_NT = (((1,), (1,)), ((), ()))
_TN = (((0,), (0,)), ((), ()))


def _dot(a, b, dims=_NN):
    return lax.dot_general(a, b, dims, preferred_element_type=F32)


def _cparams(*sem):
    return pltpu.CompilerParams(dimension_semantics=sem, vmem_limit_bytes=VMEM_LIMIT)


TK = 2048
TK_MULTI = 1024


def _matmul(pairs, mode, out_dtypes, *, name, tm=1024, tn=1024, epilogue=None, extras=(), deps=(), out_quarters=False):
    a0, b0, _ = pairs[0]
    m_dim = a0.shape[-1] if mode == "tn" else a0.shape[-2]
    if b0.ndim == 3:
        n_dim = b0.shape[1] if mode == "nt" else b0.shape[0] * b0.shape[2]
    else:
        n_dim = b0.shape[0] if mode == "nt" else b0.shape[1]
    tm, tn = min(tm, m_dim), min(tn, n_dim)
    nks, offs = [], []
    for a, _, tk in pairs:
        k_part = a.shape[0] if mode == "tn" else a.shape[-1]
        k_dim = k_part * (a.shape[0] if a.ndim == 3 else 1)
        assert k_part % tk == 0, (name, k_part, tk)
        offs.append(sum(nks))
        nks.append(k_dim // tk)
    nk_total = sum(nks)
    assert m_dim % tm == 0 and n_dim % tn == 0, (name, m_dim, n_dim)
    dims = {"nn": _NN, "nt": _NT, "tn": _TN}[mode]
    n_pairs, n_extra, n_out = len(pairs), len(extras), len(out_dtypes)

    in_specs, operands = [], []
    for (a, b, tk), off, nk in zip(pairs, offs, nks):
        def kidx(k, off=off, nk=nk):
            return k if n_pairs == 1 else jnp.clip(k - off, 0, nk - 1)
        if mode == "tn":
            assert a.ndim == 2
            in_specs.append(pl.BlockSpec((tk, tm), lambda m, n, k, f=kidx: (f(k), m)))
        elif a.ndim == 3:
            per = a.shape[2] // tk
            in_specs.append(pl.BlockSpec((None, tm, tk), lambda m, n, k, f=kidx, per=per: (f(k) // per, m, f(k) % per)))
        else:
            in_specs.append(pl.BlockSpec((tm, tk), lambda m, n, k, f=kidx: (m, f(k))))
        if b.ndim == 3 and mode == "nt":
            per = b.shape[2] // tk
            in_specs.append(pl.BlockSpec((None, tn, tk), lambda m, n, k, f=kidx, per=per: (f(k) // per, n, f(k) % per)))
        elif b.ndim == 3:
            per = b.shape[2] // tn
            in_specs.append(pl.BlockSpec((None, tk, tn), lambda m, n, k, f=kidx, per=per: (n // per, f(k), n % per)))
        elif mode == "nt":
            in_specs.append(pl.BlockSpec((tn, tk), lambda m, n, k, f=kidx: (n, f(k))))
        else:
            in_specs.append(pl.BlockSpec((tk, tn), lambda m, n, k, f=kidx: (f(k), n)))
        operands += [a, b]
    for e in extras:
        in_specs.append(pl.BlockSpec((tm, tn), lambda m, n, k: (m, n)))
        operands.append(e)
    in_specs += [pl.BlockSpec(memory_space=pl.ANY)] * len(deps)
    operands += list(deps)
    first_out = 2 * n_pairs + n_extra + len(deps)
    if out_quarters:
        out_per_q = n_dim // N_CHIPS // tn
        out_dims = (N_CHIPS, m_dim, n_dim // N_CHIPS)
        out_spec = pl.BlockSpec((None, tm, tn), lambda m, n, k: (n // out_per_q, m, n % out_per_q))
    else:
        out_dims = (m_dim, n_dim)
        out_spec = pl.BlockSpec((tm, tn), lambda m, n, k: (m, n))

    def body(*refs):
        ab = refs[:2 * n_pairs]
        e_refs = refs[2 * n_pairs:2 * n_pairs + n_extra]
        o_refs = refs[first_out:first_out + n_out]

        def finish(total):
            vals = (total,) if epilogue is None else epilogue(total, *[e[...] for e in e_refs])
            for o_ref, v in zip(o_refs, vals):
                o_ref[...] = v.astype(o_ref.dtype)

        if nk_total == 1:
            finish(_dot(ab[0][...], ab[1][...], dims))
            return
        acc = refs[-1]
        k = pl.program_id(2)

        @pl.when(k == 0)
        def _():
            acc[...] = jnp.zeros_like(acc)

        for i in range(n_pairs):
            def accumulate(i=i):
                acc[...] += _dot(ab[2 * i][...], ab[2 * i + 1][...], dims)
            if n_pairs == 1:
                accumulate()
            else:
                pl.when((k >= offs[i]) & (k < offs[i] + nks[i]))(accumulate)

        @pl.when(k == nk_total - 1)
        def _():
            finish(acc[...])

    outs = pl.pallas_call(
        body,
        grid=(m_dim // tm, n_dim // tn, nk_total),
        in_specs=in_specs,
        out_specs=[out_spec for _ in out_dtypes],
        out_shape=[jax.ShapeDtypeStruct(out_dims, dt) for dt in out_dtypes],
        scratch_shapes=[pltpu.VMEM((tm, tn), F32)] if nk_total > 1 else [],
        compiler_params=_cparams("parallel", "parallel", "arbitrary"),
        name=name,
    )(*operands)
    return outs[0] if n_out == 1 else outs


def _rowcall(fn, rows, vecs, row_outs, acc_widths, *, name, tr=256, row_cols=None, deps=()):
    s_dim = rows[0].shape[0]
    assert s_dim % tr == 0
    row_cols = row_cols or [None] * len(rows)
    n_r, n_v, n_ro, n_acc = len(rows), len(vecs), len(row_outs), len(acc_widths)
    in_specs = []
    for r, rc in zip(rows, row_cols):
        if rc is None:
            in_specs.append(pl.BlockSpec((tr, r.shape[1]), lambda i: (i, 0)))
        else:
            in_specs.append(pl.BlockSpec((tr, rc[0]), lambda i, c=rc[1]: (i, c)))
    for v in vecs:
        in_specs.append(pl.BlockSpec(v.shape, lambda i, nd=v.ndim: (0,) * nd))
    in_specs += [pl.BlockSpec(memory_space=pl.ANY)] * len(deps)
    n_d = len(deps)

    def body(*refs):
        ins = [r[...] for r in refs[:n_r + n_v]]
        ro = refs[n_r + n_v + n_d:n_r + n_v + n_d + n_ro]
        ao = refs[n_r + n_v + n_d + n_ro:]
        outs = fn(*ins)
        for ref, v in zip(ro, outs[:n_ro]):
            ref[...] = v.astype(ref.dtype)
        if n_acc:
            @pl.when(pl.program_id(0) == 0)
            def _():
                for ref in ao:
                    ref[...] = jnp.zeros_like(ref)
            for ref, v in zip(ao, outs[n_ro:]):
                ref[...] += v

    outs = pl.pallas_call(
        body,
        grid=(s_dim // tr,),
        in_specs=in_specs,
        out_specs=[pl.BlockSpec((tr, w), lambda i: (i, 0)) for w, _ in row_outs]
        + [pl.BlockSpec((1, w), lambda i: (0, 0)) for w in acc_widths],
        out_shape=[jax.ShapeDtypeStruct((s_dim, w), dt) for w, dt in row_outs]
        + [jax.ShapeDtypeStruct((1, w), F32) for w in acc_widths],
        compiler_params=_cparams("arbitrary"),
        name=name,
    )(*rows, *vecs, *deps)
    return outs


def _nrm(x, g):
    r = lax.rsqrt(jnp.mean(x * x, axis=-1, keepdims=True) + EPS)
    n = x * r
    return n * g, n, r


def _nrm_bwd(dy, n, r, g):
    dn = dy * g
    dx = r * (dn - n * jnp.mean(dn * n, axis=-1, keepdims=True))
    return dx, jnp.sum(dy * n, axis=0, keepdims=True)


def _sigmoid(x):
    return 1.0 / (1.0 + jnp.exp(-x))


def _softplus(x):
    return jnp.maximum(x, 0.0) + jnp.log(1.0 + jnp.exp(-jnp.abs(x)))


def _pre_norm(x, g1):
    def fn(xb, g):
        return (_nrm(xb, g)[0],)
    return _rowcall(fn, [x], [g1], [(D_MODEL, BF16)], [], name="pre_norm")[0]


def _post_pre_norm(x, mix, g2, g3):
    def fn(xb, mb, g2b, g3b):
        h1 = xb + _nrm(mb, g2b)[0]
        return h1, _nrm(h1, g3b)[0]
    return _rowcall(fn, [x, mix], [g2, g3], [(D_MODEL, F32), (D_MODEL, BF16)], [], name="post_pre_norm")


def _tail(ff, h1, target, g4):
    def fn(ffb, h1b, tb, g):
        y, n, r = _nrm(ffb, g)
        e = h1b + y - tb
        loss = 0.5 * jnp.sum(jnp.sum(e * e, axis=-1, keepdims=True) * (1.0 / D_MODEL), axis=0, keepdims=True)
        dh2 = e * (1.0 / D_MODEL)
        dff, dg = _nrm_bwd(dh2, n, r, g)
        return dh2, dff, dg, jnp.broadcast_to(loss, (1, LANES))
    return _rowcall(fn, [ff, h1, target], [g4], [(D_MODEL, F32), (D_MODEL, BF16)], [D_MODEL, LANES], name="tail")


def _mid_bwd(du2, h1, dh2, mix, g2, g3, deps=()):
    def fn(du2b, h1b, dh2b, mb, g2b, g3b):
        _, n3, r3 = _nrm(h1b, g3b)
        d3, dg3 = _nrm_bwd(du2b, n3, r3, g3b)
        dh1 = dh2b + d3
        _, n2, r2 = _nrm(mb, g2b)
        dmix, dg2 = _nrm_bwd(dh1, n2, r2, g2b)
        return dh1, dmix, dg3, dg2
    return _rowcall(fn, [du2, h1, dh2, mix], [g2, g3], [(D_MODEL, F32), (D_MODEL, BF16)], [D_MODEL, D_MODEL],
                    name="mid_bwd", deps=deps)


def _first_bwd(du, x, dh1, g1):
    def fn(dub, xb, dh1b, g):
        _, n, r = _nrm(xb, g)
        dx, dg = _nrm_bwd(dub, n, r, g)
        return dh1b + dx, dg
    return _rowcall(fn, [du, x, dh1], [g1], [(D_MODEL, F32)], [D_MODEL], name="first_bwd")


CONV_TILE = 256
CONV_ROWS = 256
PAD = 8


def _conv_taps(w):
    return [w[k:k + 1, :] for k in range(CONV_WIDTH)], w[CONV_WIDTH:CONV_WIDTH + 1, :]


def _conv_fwd(xbc, w8):
    s_dim, c_dim = xbc.shape
    n_steps = s_dim // CONV_ROWS

    def body(x_ref, w_ref, o_ref, xp):
        xp[0:PAD, :] = jnp.zeros((PAD, CONV_TILE), F32)
        xp[PAD:PAD + s_dim, :] = x_ref[...]
        taps, bias = _conv_taps(w_ref[...])

        def step(c, carry):
            base = pl.multiple_of(c * CONV_ROWS, CONV_ROWS)
            win = xp[pl.ds(base, CONV_ROWS + PAD), :]
            pre = bias + taps[3] * win[PAD:, :]
            for j in range(1, CONV_WIDTH):
                pre = pre + taps[3 - j] * pltpu.roll(win, j, axis=0)[PAD:, :]
            o_ref[pl.ds(base, CONV_ROWS), :] = pre * _sigmoid(pre)
            return carry

        lax.fori_loop(0, n_steps, step, 0)

    return pl.pallas_call(
        body,
        grid=(c_dim // CONV_TILE,),
        in_specs=[pl.BlockSpec((s_dim, CONV_TILE), lambda j: (0, j)), pl.BlockSpec((8, CONV_TILE), lambda j: (0, j))],
        out_specs=pl.BlockSpec((s_dim, CONV_TILE), lambda j: (0, j)),
        out_shape=jax.ShapeDtypeStruct((s_dim, c_dim), F32),
        scratch_shapes=[pltpu.VMEM((s_dim + 2 * PAD, CONV_TILE), F32)],
        compiler_params=_cparams("parallel"),
        name="conv_fwd",
    )(xbc, w8)


def _conv_bwd(xbc, w8, dxc):
    s_dim, c_dim = xbc.shape
    n_steps = s_dim // CONV_ROWS

    def body(x_ref, w_ref, d_ref, dx_ref, dw_ref, xp, dp):
        xp[0:PAD, :] = jnp.zeros((PAD, CONV_TILE), F32)
        xp[PAD:PAD + s_dim, :] = x_ref[...]
        dp[PAD + s_dim:, :] = jnp.zeros((PAD, CONV_TILE), F32)
        taps, bias = _conv_taps(w_ref[...])

        def step1(c, sums):
            base = pl.multiple_of(c * CONV_ROWS, CONV_ROWS)
            win = xp[pl.ds(base, CONV_ROWS + PAD), :]
            shifted = [win[PAD:, :]] + [pltpu.roll(win, j, axis=0)[PAD:, :] for j in range(1, CONV_WIDTH)]
            pre = bias
            for j in range(CONV_WIDTH):
                pre = pre + taps[3 - j] * shifted[j]
            sg = _sigmoid(pre)
            dpre = d_ref[pl.ds(base, CONV_ROWS), :] * (sg * (1.0 + pre * (1.0 - sg)))
            dp[pl.ds(base + PAD, CONV_ROWS), :] = dpre
            new = [sums[k] + jnp.sum(dpre * shifted[3 - k], axis=0, keepdims=True) for k in range(CONV_WIDTH)]
            new.append(sums[CONV_WIDTH] + jnp.sum(dpre, axis=0, keepdims=True))
            return tuple(new)

        zero = jnp.zeros((1, CONV_TILE), F32)
        sums = lax.fori_loop(0, n_steps, step1, (zero,) * (CONV_WIDTH + 1))
        dw_ref[...] = jnp.zeros((8, CONV_TILE), F32)
        for k in range(CONV_WIDTH + 1):
            dw_ref[k:k + 1, :] = sums[k]

        def step2(c, carry):
            base = pl.multiple_of(c * CONV_ROWS, CONV_ROWS)
            win = dp[pl.ds(base + PAD, CONV_ROWS + PAD), :]
            dx = taps[3] * win[:CONV_ROWS, :]
            for j in range(1, CONV_WIDTH):
                dx = dx + taps[3 - j] * pltpu.roll(win, CONV_ROWS + PAD - j, axis=0)[:CONV_ROWS, :]
            dx_ref[pl.ds(base, CONV_ROWS), :] = dx.astype(BF16)
            return carry

        lax.fori_loop(0, n_steps, step2, 0)

    col = lambda j: (0, j)
    return pl.pallas_call(
        body,
        grid=(c_dim // CONV_TILE,),
        in_specs=[pl.BlockSpec((s_dim, CONV_TILE), col), pl.BlockSpec((8, CONV_TILE), col),
                  pl.BlockSpec((s_dim, CONV_TILE), col)],
        out_specs=[pl.BlockSpec((s_dim, CONV_TILE), col), pl.BlockSpec((8, CONV_TILE), col)],
        out_shape=[jax.ShapeDtypeStruct((s_dim, c_dim), BF16), jax.ShapeDtypeStruct((8, c_dim), F32)],
        scratch_shapes=[pltpu.VMEM((s_dim + 2 * PAD, CONV_TILE), F32), pltpu.VMEM((s_dim + 2 * PAD, CONV_TILE), F32)],
        compiler_params=_cparams("parallel"),
        name="conv_bwd",
    )(xbc, w8, dxc)


def _perm_cols(a):
    parts = []
    for g in range(SSM_GROUPS):
        parts += [a[..., g * GROUP_X:(g + 1) * GROUP_X],
                  a[..., D_SSM + g * D_STATE:D_SSM + (g + 1) * D_STATE],
                  a[..., D_SSM + SSM_GROUPS * D_STATE + g * D_STATE:D_SSM + SSM_GROUPS * D_STATE + (g + 1) * D_STATE]]
    return jnp.concatenate(parts, axis=-1)


def _unperm_cols(a):
    xs = [a[..., g * GROUP_COLS:g * GROUP_COLS + GROUP_X] for g in range(SSM_GROUPS)]
    bs = [a[..., g * GROUP_COLS + GROUP_X:g * GROUP_COLS + GROUP_X + D_STATE] for g in range(SSM_GROUPS)]
    cs = [a[..., g * GROUP_COLS + GROUP_X + D_STATE:(g + 1) * GROUP_COLS] for g in range(SSM_GROUPS)]
    return jnp.concatenate(xs + bs + cs, axis=-1)


def _dt_to_groups(dt):
    s_dim = dt.shape[0]
    t = dt[:, :SSM_HEADS].reshape(s_dim, SSM_GROUPS, HEADS_PER_GROUP).transpose(1, 0, 2)
    return jnp.pad(t, ((0, 0), (0, 0), (0, LANES - HEADS_PER_GROUP)))


def _dt_from_groups(dtg):
    s_dim = dtg.shape[1]
    return dtg[:, :, :HEADS_PER_GROUP].transpose(1, 0, 2).reshape(s_dim, SSM_HEADS)


def _pack_ssd_params(dt_bias, a_log, d_skip):
    rows = jnp.stack([p.reshape(SSM_GROUPS, HEADS_PER_GROUP) for p in (dt_bias, a_log, d_skip)], axis=1)
    return jnp.pad(rows, ((0, 0), (0, 8 - 3), (0, LANES - HEADS_PER_GROUP)))


def _unpack_ssd_params(par):
    return tuple(par[:, k, :HEADS_PER_GROUP].reshape(SSM_HEADS) for k in range(3))


Q = SSD_CHUNK


def _split3(v):
    hi = v.astype(BF16)
    r1 = v - hi.astype(F32)
    mid = r1.astype(BF16)
    lo = (r1 - mid.astype(F32)).astype(BF16)
    return hi, mid, lo


def _dot_l01(t01, v):
    return sum(_dot(t01, p) for p in _split3(v))


def _dot_r01(v, e01):
    return sum(_dot(p, e01) for p in _split3(v))


def _ssd_consts():
    row = lax.broadcasted_iota(jnp.int32, (Q, Q), 0)
    col = lax.broadcasted_iota(jnp.int32, (Q, Q), 1)
    causal = row >= col
    tril = causal.astype(BF16)
    triu = (col >= row).astype(BF16)
    er = lax.broadcasted_iota(jnp.int32, (LANES, GROUP_X), 0)
    ec = lax.broadcasted_iota(jnp.int32, (LANES, GROUP_X), 1) // SSM_HEAD_DIM
    expand = (er == ec).astype(BF16)
    rr = lax.broadcasted_iota(jnp.int32, (GROUP_X, LANES), 0) // SSM_HEAD_DIM
    rc = lax.broadcasted_iota(jnp.int32, (GROUP_X, LANES), 1)
    reduce = (rr == rc).astype(BF16)
    lane_head = lax.broadcasted_iota(jnp.int32, (Q, GROUP_X), 1) // SSM_HEAD_DIM
    return causal, tril, triu, expand, reduce, lane_head


def _ssd_common(xc_ref, dt_ref, par_ref, consts):
    causal, tril, _, expand, _, _ = consts
    par = par_ref[...]
    bias, alog, dsk = par[0:1, :], par[1:2, :], par[2:3, :]
    a_neg = -jnp.exp(alog)
    dtr = dt_ref[...] + bias
    dt = _softplus(dtr)
    s = _dot_l01(tril, dt * a_neg)
    dt_x = _dot_r01(dt, expand)
    s_x = _dot_r01(s, expand)
    dsk_x = _dot_r01(jnp.broadcast_to(dsk, (8, LANES)), expand)[0:1, :]
    blk = xc_ref[...]
    x = blk[:, :GROUP_X]
    bm = blk[:, GROUP_X:GROUP_X + D_STATE].astype(BF16)
    cm = blk[:, GROUP_X + D_STATE:].astype(BF16)
    xdt = x * dt_x
    g = _dot(cm, bm, _NT)
    return dict(a_neg=a_neg, dtr=dtr, dt=dt, s=s, s_t=s.T, dt_x=dt_x, s_x=s_x, dsk_x=dsk_x, x=x, bm=bm, cm=cm,
                xdt=xdt, g=g)


def _decay(v, r, causal):
    diff = v["s"][:, r:r + 1] - v["s_t"][r:r + 1, :]
    return jnp.exp(jnp.where(causal, diff, -jnp.inf))


def _ssd_specs(n_chunks, rev):
    cidx = (lambda c: n_chunks - 1 - c) if rev else (lambda c: c)
    xc = pl.BlockSpec((Q, GROUP_COLS), lambda g, c: (cidx(c), g))
    gx = pl.BlockSpec((Q, GROUP_X), lambda g, c: (cidx(c), g))
    dt = pl.BlockSpec((None, Q, LANES), lambda g, c: (g, cidx(c), 0))
    par = pl.BlockSpec((None, 8, LANES), lambda g, c: (g, 0, 0))
    nw = pl.BlockSpec((1, GROUP_X), lambda g, c: (0, g))
    hs = pl.BlockSpec((None, None, D_STATE, GROUP_X), lambda g, c: (cidx(c), g, 0, 0))
    return xc, gx, dt, par, nw, hs


def _ssd_fwd(xc, z, dtg, par, nw):
    s_dim = xc.shape[0]
    n_chunks = s_dim // Q
    xc_s, gx_s, dt_s, par_s, nw_s, hs_s = _ssd_specs(n_chunks, False)

    def body(xc_ref, z_ref, dt_ref, par_ref, nw_ref, y_ref, ys_ref, hs_ref, ht):
        @pl.when(pl.program_id(1) == 0)
        def _():
            ht[...] = jnp.zeros_like(ht)

        consts = _ssd_consts()
        causal, lane_head = consts[0], consts[5]
        v = _ssd_common(xc_ref, dt_ref, par_ref, consts)
        xdt_b = v["xdt"].astype(BF16)
        yd = jnp.zeros((Q, GROUP_X), F32)
        for r in range(HEADS_PER_GROUP):
            m = (v["g"] * _decay(v, r, causal)).astype(BF16)
            yd = yd + _dot(m, jnp.where(lane_head == r, xdt_b, jnp.zeros_like(xdt_b)))
        h = ht[...]
        hs_ref[...] = h
        yo = jnp.exp(v["s_x"]) * _dot(v["cm"], h.astype(BF16))
        y = yd + yo + v["dsk_x"] * v["x"]
        s_last = v["s_x"][Q - 1:Q, :]
        snew = _dot(v["bm"], (v["xdt"] * jnp.exp(s_last - v["s_x"])).astype(BF16), _TN)
        ht[...] = jnp.exp(s_last) * h + snew
        zz = z_ref[...]
        yg = y * (zz * _sigmoid(zz))
        y_ref[...] = y
        ys_ref[...] = _nrm(yg, nw_ref[...])[0].astype(BF16)

    return pl.pallas_call(
        body,
        grid=(SSM_GROUPS, n_chunks),
        in_specs=[xc_s, gx_s, dt_s, par_s, nw_s],
        out_specs=[gx_s, gx_s, hs_s],
        out_shape=[jax.ShapeDtypeStruct((s_dim, D_SSM), F32), jax.ShapeDtypeStruct((s_dim, D_SSM), BF16),
                   jax.ShapeDtypeStruct((n_chunks, SSM_GROUPS, D_STATE, GROUP_X), F32)],
        scratch_shapes=[pltpu.VMEM((D_STATE, GROUP_X), F32)],
        compiler_params=_cparams("parallel", "arbitrary"),
        name="ssd_fwd",
    )(xc, z, dtg, par, nw)


def _ssd_bwd(xc, z, dtg, par, nw, y, hs, dymix):
    s_dim = xc.shape[0]
    n_chunks = s_dim // Q
    xc_s, gx_s, dt_s, par_s, nw_s, hs_s = _ssd_specs(n_chunks, True)

    def body(xc_ref, z_ref, dt_ref, par_ref, nw_ref, y_ref, hs_ref, dys_ref,
             dxc_ref, dz_ref, ddt_ref, dpar_ref, dnw_ref, dht):
        @pl.when(pl.program_id(1) == 0)
        def _():
            dht[...] = jnp.zeros_like(dht)
            dpar_ref[...] = jnp.zeros_like(dpar_ref)
            dnw_ref[...] = jnp.zeros_like(dnw_ref)

        consts = _ssd_consts()
        causal, _, triu, _, reduce, lane_head = consts
        v = _ssd_common(xc_ref, dt_ref, par_ref, consts)
        x, bm, cm, xdt, s_x = v["x"], v["bm"], v["cm"], v["xdt"], v["s_x"]
        h = hs_ref[...]
        hb = h.astype(BF16)
        es_x = jnp.exp(s_x)
        yo = es_x * _dot(cm, hb)
        s_last = s_x[Q - 1:Q, :]
        e_x = jnp.exp(s_last - s_x)
        es_last = jnp.exp(s_last)

        yv, zz, nw_v = y_ref[...], z_ref[...], nw_ref[...]
        sg = _sigmoid(zz)
        gz = zz * sg
        _, n, rstd = _nrm(yv * gz, nw_v)
        dout = dys_ref[...]
        dyg, dnw = _nrm_bwd(dout, n, rstd, nw_v)
        dnw_ref[...] += dnw
        dy = dyg * gz
        dz_ref[...] = (dyg * yv * (sg * (1.0 + zz * (1.0 - sg)))).astype(BF16)

        dyb = dy.astype(BF16)
        xdt_b = xdt.astype(BF16)
        dhp = dht[...]
        dhpb = dhp.astype(BF16)
        lane = lax.broadcasted_iota(jnp.int32, (Q, LANES), 1)
        sub = lax.broadcasted_iota(jnp.int32, (LANES, Q), 0)
        dxdt = jnp.zeros((Q, GROUP_X), F32)
        dg = jnp.zeros((Q, Q), F32)
        ds = jnp.zeros((Q, LANES), F32)
        ds_t = jnp.zeros((LANES, Q), F32)
        for r in range(HEADS_PER_GROUP):
            dec = _decay(v, r, causal)
            mf = v["g"] * dec
            dyr = jnp.where(lane_head == r, dyb, jnp.zeros_like(dyb))
            dm = _dot(dyr, xdt_b, _NT)
            dxdt = dxdt + _dot(mf.astype(BF16), dyr, _TN)
            dg = dg + dm * dec
            dd = dm * mf
            ds = ds + jnp.where(lane == r, jnp.sum(dd, axis=1, keepdims=True), 0.0)
            ds_t = ds_t + jnp.where(sub == r, jnp.sum(dd, axis=0, keepdims=True), 0.0)
        ds = ds - ds_t.T
        dgb = dg.astype(BF16)
        dwb = (es_x * dy).astype(BF16)
        dcm = _dot(dgb, bm) + _dot(dwb, hb, _NT)
        dh_prev = _dot(cm, dwb, _TN)
        zst = _dot(bm, dhpb)
        xe = xdt * e_x
        dxdt = dxdt + e_x * zst
        dee = xe * zst
        dbm = _dot(dgb, cm, _TN) + _dot(xe.astype(BF16), dhpb, _NT)
        v_last = jnp.sum(dee, axis=0, keepdims=True) + es_last * jnp.sum(dhp * h, axis=0, keepdims=True)
        row_x = lax.broadcasted_iota(jnp.int32, (Q, GROUP_X), 0)
        tx = dy * yo - dee + jnp.where(row_x == Q - 1, v_last, 0.0)
        ds = ds + _dot_r01(tx, reduce)
        ddta = _dot_l01(triu, ds)
        ddt = ddta * v["a_neg"] + _dot_r01(dxdt * x, reduce)
        dalog = jnp.sum(ddta * v["dt"], axis=0, keepdims=True) * v["a_neg"]
        draw = jnp.where(lane < HEADS_PER_GROUP, ddt * _sigmoid(v["dtr"]), 0.0)
        dbias = jnp.sum(draw, axis=0, keepdims=True)
        ddsk = _dot_r01(jnp.broadcast_to(jnp.sum(dy * x, axis=0, keepdims=True), (8, GROUP_X)), reduce)[0:1, :]
        dht[...] = es_last * dhp + dh_prev
        dxc_ref[:, :GROUP_X] = dxdt * v["dt_x"] + v["dsk_x"] * dy
        dxc_ref[:, GROUP_X:GROUP_X + D_STATE] = dbm
        dxc_ref[:, GROUP_X + D_STATE:] = dcm
        ddt_ref[...] = draw
        dpar_ref[0:1, :] += dbias
        dpar_ref[1:2, :] += dalog
        dpar_ref[2:3, :] += ddsk

    return pl.pallas_call(
        body,
        grid=(SSM_GROUPS, n_chunks),
        in_specs=[xc_s, gx_s, dt_s, par_s, nw_s, gx_s, hs_s, gx_s],
        out_specs=[xc_s, gx_s, dt_s, par_s, nw_s],
        out_shape=[jax.ShapeDtypeStruct((s_dim, SSM_GROUPS * GROUP_COLS), F32),
                   jax.ShapeDtypeStruct((s_dim, D_SSM), BF16),
                   jax.ShapeDtypeStruct((SSM_GROUPS, s_dim, LANES), F32),
                   jax.ShapeDtypeStruct((SSM_GROUPS, 8, LANES), F32),
                   jax.ShapeDtypeStruct((1, D_SSM), F32)],
        scratch_shapes=[pltpu.VMEM((D_STATE, GROUP_X), F32)],
        compiler_params=_cparams("parallel", "arbitrary"),
        name="ssd_bwd",
    )(xc, z, dtg, par, nw, y, hs, dymix)


ATT_SCALE = ATT_HEAD_DIM ** -0.5
NEG_INF = -jnp.inf


def _band_masks():
    qi = lax.broadcasted_iota(jnp.int32, (ATT_BLOCK, ATT_BLOCK), 0)
    kj = lax.broadcasted_iota(jnp.int32, (ATT_BLOCK, ATT_BLOCK), 1)
    return kj <= qi, kj >= qi


WIN = ATT_BLOCK * DILATIONS[-1]
N_BLOCKS = WIN // ATT_BLOCK


def _rows(start, d):
    return pl.ds(start, ATT_BLOCK) if d == 1 else pl.ds(start, ATT_BLOCK, stride=d)


def _block_start(idx, d):
    return (idx // d) * (ATT_BLOCK * d) + idx % d


def _lane_bcast(col):
    return jnp.broadcast_to(col, (col.shape[0], LANES))


def _attn_fused_fwd(qkv):
    s_dim = qkv.shape[0]
    n_win = s_dim // WIN
    blk = (WIN, ATT_HEAD_DIM)
    prev = lambda w: jnp.maximum(w - 1, 0)

    def body(q_ref, kc_ref, kp_ref, vc_ref, vp_ref, y_ref, yf_ref, lse_ref, qf, kf, vf, acc, m_run, l_run):
        w, h = pl.program_id(0), pl.program_id(1)
        qf[...] = q_ref[...].astype(F32)
        kf[0:WIN, :] = kp_ref[...].astype(F32)
        kf[WIN:, :] = kc_ref[...].astype(F32)
        vf[0:WIN, :] = vp_ref[...].astype(F32)
        vf[WIN:, :] = vc_ref[...].astype(F32)
        own, before = _band_masks()

        for d in DILATIONS:
            def block(idx, carry, d=d):
                start = _block_start(idx, d)
                rows = _rows(start, d)
                q = qf[rows, :].astype(BF16)
                kc, vc = kf[_rows(WIN + start, d), :].astype(BF16), vf[_rows(WIN + start, d), :].astype(BF16)
                kp = kf[_rows(WIN + start - ATT_BLOCK * d, d), :].astype(BF16)
                vp = vf[_rows(WIN + start - ATT_BLOCK * d, d), :].astype(BF16)
                has_prev = (idx >= d) | (w > 0)
                sc = jnp.where(own, _dot(q, kc, _NT) * ATT_SCALE, NEG_INF)
                sp = jnp.where(before & has_prev, _dot(q, kp, _NT) * ATT_SCALE, NEG_INF)
                m_blk = jnp.maximum(jnp.max(sc, axis=1, keepdims=True), jnp.max(sp, axis=1, keepdims=True))
                if d == DILATIONS[0]:
                    m_new = m_blk
                else:
                    m_old = m_run[rows, :][:, 0:1]
                    m_new = jnp.maximum(m_old, m_blk)
                pc, pp = jnp.exp(sc - m_new), jnp.exp(sp - m_new)
                l_new = jnp.sum(pc, axis=1, keepdims=True) + jnp.sum(pp, axis=1, keepdims=True)
                o_new = _dot(pc.astype(BF16), vc) + _dot(pp.astype(BF16), vp)
                if d != DILATIONS[0]:
                    alpha = jnp.exp(m_old - m_new)
                    l_new = alpha * l_run[rows, :][:, 0:1] + l_new
                    o_new = alpha * acc[rows, :] + o_new
                m_run[rows, :] = _lane_bcast(m_new)
                l_run[rows, :] = _lane_bcast(l_new)
                acc[rows, :] = o_new
                return carry

            for idx in range(N_BLOCKS):
                block(idx, 0)

        l_all = l_run[...]
        y = acc[...] / l_all
        y_ref[...] = y.astype(BF16)
        yf_ref[...] = y
        @pl.when(h == 0)
        def _():
            lse_ref[...] = jnp.zeros_like(lse_ref)

        lane = lax.broadcasted_iota(jnp.int32, (WIN, LANES), 1)
        lse_ref[...] = jnp.where(lane == h, m_run[...] + jnp.log(l_all), lse_ref[...])

    win_scratch = lambda rows: pltpu.VMEM((rows, ATT_HEAD_DIM), F32)
    return pl.pallas_call(
        body,
        grid=(n_win, ATT_HEADS),
        in_specs=[pl.BlockSpec(blk, lambda w, h: (w, h)),
                  pl.BlockSpec(blk, lambda w, h: (w, ATT_HEADS + h)),
                  pl.BlockSpec(blk, lambda w, h: (prev(w), ATT_HEADS + h)),
                  pl.BlockSpec(blk, lambda w, h: (w, 2 * ATT_HEADS + h)),
                  pl.BlockSpec(blk, lambda w, h: (prev(w), 2 * ATT_HEADS + h))],
        out_specs=[pl.BlockSpec(blk, lambda w, h: (w, h)), pl.BlockSpec(blk, lambda w, h: (w, h)),
                   pl.BlockSpec((WIN, LANES), lambda w, h: (w, 0))],
        out_shape=[jax.ShapeDtypeStruct((s_dim, D_ATT), BF16), jax.ShapeDtypeStruct((s_dim, D_ATT), F32),
                   jax.ShapeDtypeStruct((s_dim, LANES), F32)],
        scratch_shapes=[win_scratch(WIN), win_scratch(2 * WIN), win_scratch(2 * WIN), win_scratch(WIN),
                        win_scratch(WIN), win_scratch(WIN)],
        compiler_params=_cparams("parallel", "arbitrary"),
        name="attn_fused_fwd",
    )(qkv, qkv, qkv, qkv, qkv)


def _attn_fused_bwd(qkv, dymix, y_att, lse, deps=()):
    s_dim = qkv.shape[0]
    n_win = s_dim // WIN
    blk = (WIN, ATT_HEAD_DIM)
    this = lambda w: jnp.minimum(w, n_win - 1)
    prev = lambda w: jnp.maximum(this(w) - 1, 0)
    n_dep = len(deps)

    def body(q_ref, kc_ref, kp_ref, vc_ref, vp_ref, dy_ref, y_ref, l_ref, *rest):
        dq_ref, dkv_ref = rest[n_dep:n_dep + 2]
        qf, kf, vf, dq_acc, dk_acc, dv_acc, ls_c, dl_c = rest[n_dep + 2:]
        h, w = pl.program_id(0), pl.program_id(1)
        slot, late = w % 2, 1 - w % 2

        @pl.when(w == 0)
        def _():
            dk_acc[...] = jnp.zeros_like(dk_acc)
            dv_acc[...] = jnp.zeros_like(dv_acc)

        @pl.when(w < n_win)
        def _():
            qf[...] = q_ref[...].astype(F32)
            kf[0:WIN, :] = kp_ref[...].astype(F32)
            kf[WIN:, :] = kc_ref[...].astype(F32)
            vf[0:WIN, :] = vp_ref[...].astype(F32)
            vf[WIN:, :] = vc_ref[...].astype(F32)
            lane = lax.broadcasted_iota(jnp.int32, (WIN, LANES), 1)
            ls_c[...] = _lane_bcast(jnp.sum(jnp.where(lane == h, l_ref[...], 0.0), axis=1, keepdims=True))
            dl_c[...] = _lane_bcast(jnp.sum(dy_ref[...] * y_ref[...], axis=1, keepdims=True))
            dq_acc[...] = jnp.zeros_like(dq_acc)
            dk_acc[slot] = jnp.zeros((WIN, ATT_HEAD_DIM), F32)
            dv_acc[slot] = jnp.zeros((WIN, ATT_HEAD_DIM), F32)
            own, before = _band_masks()

            def probs(q, k, v, dy, lse_col, dl_col, mask):
                p = jnp.exp(jnp.where(mask, _dot(q, k, _NT) * ATT_SCALE - lse_col, NEG_INF))
                ds = p * (_dot(dy, v, _NT) - dl_col)
                return p.astype(BF16), ds.astype(BF16)

            for d in DILATIONS:
                for idx in range(N_BLOCKS):
                    start = _block_start(idx, d)
                    rows = _rows(start, d)
                    q, dy = qf[rows, :].astype(BF16), dy_ref[rows, :].astype(BF16)
                    lse_col, dl_col = ls_c[rows, :][:, 0:1], dl_c[rows, :][:, 0:1]
                    kc, vc = kf[_rows(WIN + start, d), :].astype(BF16), vf[_rows(WIN + start, d), :].astype(BF16)
                    kp = kf[_rows(WIN + start - ATT_BLOCK * d, d), :].astype(BF16)
                    vp = vf[_rows(WIN + start - ATT_BLOCK * d, d), :].astype(BF16)
                    pc, dsc = probs(q, kc, vc, dy, lse_col, dl_col, own)
                    pp, dsp = probs(q, kp, vp, dy, lse_col, dl_col, before & ((idx >= d) | (w > 0)))
                    dq_acc[rows, :] += (_dot(dsc, kc) + _dot(dsp, kp)) * ATT_SCALE
                    dk_acc[slot, rows, :] += _dot(dsc, q, _TN) * ATT_SCALE
                    dv_acc[slot, rows, :] += _dot(pc, dy, _TN)
                    if idx >= d:
                        prows = _rows(start - ATT_BLOCK * d, d)
                        dk_acc[slot, prows, :] += _dot(dsp, q, _TN) * ATT_SCALE
                        dv_acc[slot, prows, :] += _dot(pp, dy, _TN)
                    else:
                        prows = _rows(WIN + start - ATT_BLOCK * d, d)
                        dk_acc[late, prows, :] += _dot(dsp, q, _TN) * ATT_SCALE
                        dv_acc[late, prows, :] += _dot(pp, dy, _TN)
            dq_ref[...] = dq_acc[...].astype(BF16)

        @pl.when(w > 0)
        def _():
            dkv_ref[0] = dk_acc[late].astype(BF16)
            dkv_ref[1] = dv_acc[late].astype(BF16)

    win_scratch = lambda *shape: pltpu.VMEM(shape + (ATT_HEAD_DIM,), F32)
    cur = lambda c: pl.BlockSpec(blk, lambda h, w: (this(w), c + h))
    before_spec = lambda c: pl.BlockSpec(blk, lambda h, w: (prev(w), c + h))
    return pl.pallas_call(
        body,
        grid=(ATT_HEADS, n_win + 1),
        in_specs=[cur(0), cur(ATT_HEADS), before_spec(ATT_HEADS), cur(2 * ATT_HEADS), before_spec(2 * ATT_HEADS),
                  cur(ATT_HEADS), cur(0), pl.BlockSpec((WIN, LANES), lambda h, w: (this(w), 0))] + [ANY] * n_dep,
        out_specs=[cur(0), pl.BlockSpec((2, WIN, ATT_HEAD_DIM), lambda h, w: (0, jnp.maximum(w - 1, 0), h))],
        out_shape=[jax.ShapeDtypeStruct((s_dim, D_ATT), BF16), jax.ShapeDtypeStruct((2, s_dim, D_ATT), BF16)],
        scratch_shapes=[win_scratch(WIN), win_scratch(2 * WIN), win_scratch(2 * WIN), win_scratch(WIN),
                        win_scratch(2, WIN), win_scratch(2, WIN), win_scratch(WIN), win_scratch(WIN)],
        compiler_params=_cparams("parallel", "arbitrary"),
        name="attn_fused_bwd",
    )(qkv, qkv, qkv, qkv, qkv, dymix, y_att, lse, *deps)


def _adamw(w, g, m, v, name):
    def fn(wb, gb, mb, vb):
        m2 = ADAM_B1 * mb + (1.0 - ADAM_B1) * gb
        v2 = ADAM_B2 * vb + (1.0 - ADAM_B2) * (gb * gb)
        m_hat = m2 / (1.0 - ADAM_B1 ** ADAM_STEP)
        v_hat = v2 / (1.0 - ADAM_B2 ** ADAM_STEP)
        delta = -ADAM_LR * (m_hat / (jnp.sqrt(v_hat) + ADAM_EPS) + ADAM_WD * wb)
        return delta, m2, v2
    cols = w.shape[1]
    tr = 128 if w.shape[0] % 128 == 0 else w.shape[0]
    return _rowcall(fn, [w, g, m, v], [], [(cols, F32)] * 3, [], name=name, tr=tr)


ANY = pl.BlockSpec(memory_space=pl.ANY)
SUM_ROWS = 256


def _position():
    x, y, c = lax.axis_index("x"), lax.axis_index("y"), lax.axis_index("c")
    chips = [(1 - x, y), (x, 1 - y), (1 - x, 1 - y)]
    return x, y, c, chips


def _remote(src, dst, send_sem, recv_sem, device):
    return pltpu.make_async_remote_copy(src_ref=src, dst_ref=dst, send_sem=send_sem, recv_sem=recv_sem,
                                        device_id=device, device_id_type=MESH)


def _handshake(peers):
    barrier = pltpu.get_barrier_semaphore()
    for p in peers:
        pl.semaphore_signal(barrier, inc=1, device_id=p, device_id_type=MESH)
    pl.semaphore_wait(barrier, len(peers))


def _gather_shards_async(shards, collective_id, name):
    n = len(shards)
    srcs = [jax.new_ref(s, memory_space=pltpu.MemorySpace.HBM) for s in shards]
    dsts = [jax.empty_ref(jax.ShapeDtypeStruct((N_CHIPS,) + s.shape, s.dtype), memory_space=pltpu.MemorySpace.HBM)
            for s in shards]

    @pl.kernel(mesh=plsc.ScalarSubcoreMesh(axis_name="seq", num_cores=1), name=name,
               scratch_types=(pltpu.SemaphoreType.DMA((6 * n,)), pltpu.SemaphoreType.DMA((6 * n,))),
               compiler_params=pltpu.CompilerParams(collective_id=collective_id))
    def launch(send_sems, recv_sems):
        x, y, c, chips = _position()
        sibling = (x, y, 1 - c)
        _handshake([(chip[0], chip[1], c) for chip in chips] + [sibling])

        def half(a, j, cc):
            h = shards[a].shape[0] // 2
            return dsts[a].at[j, pl.ds(cc * h, h), :]

        sent = []
        for a in range(n):
            h = shards[a].shape[0] // 2
            for j, chip in enumerate(chips):
                cp = _remote(srcs[a].at[pl.ds(c * h, h), :], half(a, 2 * x + y, c), send_sems.at[6 * a + j],
                             recv_sems.at[6 * a + j], (chip[0], chip[1], c))
                cp.start()
                sent.append(cp)
        for a in range(n):
            for j, chip in enumerate(chips):
                landed = half(a, 2 * chip[0] + chip[1], c)
                _remote(landed, landed, send_sems.at[6 * a + j], recv_sems.at[6 * a + j], (x, y, c)).wait_recv()
                cp = _remote(landed, landed, send_sems.at[6 * a + 3 + j], recv_sems.at[6 * a + 3 + j], sibling)
                cp.start()
                sent.append(cp)
        for a in range(n):
            for j, chip in enumerate(chips):
                handed = half(a, 2 * chip[0] + chip[1], 1 - c)
                _remote(handed, handed, send_sems.at[6 * a + 3 + j], recv_sems.at[6 * a + 3 + j], (x, y, c)).wait_recv()
        for cp in sent:
            cp.wait_send()

    launch()
    return [d[...] for d in dsts]


IN_COLS = {"z": (0, D_SSM), "xbc": (D_SSM, D_SSM + D_XBC), "dt": (D_SSM + D_XBC, D_SSM + D_XBC + SSM_HEADS),
           "qkv": (D_SSM + D_XBC + SSM_HEADS, D_IN_PROJ)}


def _cols_from_quarters(quarters, lo, hi):
    parts = []
    for q in range(N_CHIPS):
        a, b = max(lo, q * W_IN_SHARD), min(hi, (q + 1) * W_IN_SHARD)
        if a < b:
            parts.append(quarters[q][:, a - q * W_IN_SHARD:b - q * W_IN_SHARD])
    return parts[0] if len(parts) == 1 else jnp.concatenate(parts, axis=1)


def _quarters_from_cols(pieces):
    quarters = []
    for q in range(N_CHIPS):
        parts = []
        for name, (lo, hi) in IN_COLS.items():
            a, b = max(lo, q * W_IN_SHARD), min(hi, (q + 1) * W_IN_SHARD)
            if a < b:
                parts.append(pieces[name][:, a - lo:b - lo])
        quarters.append(jnp.concatenate(parts, axis=1))
    return jnp.stack(quarters)


def _by_chip(own, fetched):
    me = 2 * lax.axis_index("x") + lax.axis_index("y")
    return lax.dynamic_update_slice(fetched, own[None], (me, 0, 0))


def _add_sibling(grad, got, place, name, deps=()):
    nq, rows, cols = grad.shape
    h = rows // 2
    tr = SUM_ROWS
    nb = h // tr

    def body(place_ref, a_ref, b_ref, *rest):
        own_ref, ob_ref = rest[len(deps):]
        total = a_ref[...] + b_ref[...]
        ob_ref[...] = total.astype(BF16)

        @pl.when(pl.program_id(1) == place_ref[1])
        def _():
            own_ref[...] = total

    return pl.pallas_call(
        body,
        grid_spec=pltpu.PrefetchScalarGridSpec(
            num_scalar_prefetch=1, grid=(nb, nq),
            in_specs=[pl.BlockSpec((None, tr, cols), lambda i, q, p: (q, p[0] * nb + i, 0)),
                      pl.BlockSpec((None, tr, cols), lambda i, q, p: (q, i, 0))] + [ANY] * len(deps),
            out_specs=[pl.BlockSpec((tr, cols), lambda i, q, p: (i, 0)),
                       pl.BlockSpec((None, tr, cols), lambda i, q, p: (q, i, 0))]),
        out_shape=[jax.ShapeDtypeStruct((h, cols), F32), jax.ShapeDtypeStruct((nq, h, cols), BF16)],
        compiler_params=_cparams("parallel", "arbitrary"),
        name=name,
    )(place, grad, got, *deps)


def _add_chips(part, got, name, deps=()):
    h, cols = part.shape
    tr = SUM_ROWS

    def body(p_ref, g0_ref, g1_ref, g2_ref, *rest):
        o_ref = rest[len(deps)]
        o_ref[...] = ((p_ref[...] + g0_ref[...].astype(F32)) + g1_ref[...].astype(F32)) + g2_ref[...].astype(F32)

    got_spec = lambda j: pl.BlockSpec((None, tr, cols), lambda i: (j, i, 0))
    row_spec = pl.BlockSpec((tr, cols), lambda i: (i, 0))
    return pl.pallas_call(
        body,
        grid=(h // tr,),
        in_specs=[row_spec, got_spec(0), got_spec(1), got_spec(2)] + [ANY] * len(deps),
        out_specs=row_spec,
        out_shape=jax.ShapeDtypeStruct((h, cols), F32),
        compiler_params=_cparams("parallel"),
        name=name,
    )(part, got, got, got, *deps)


def _sequencer_exchange(src, out_shape, collective_id, name, plan, n_copies):
    src_ref = jax.new_ref(src, memory_space=pltpu.MemorySpace.HBM)
    dst_ref = jax.empty_ref(out_shape, memory_space=pltpu.MemorySpace.HBM)

    @pl.kernel(mesh=plsc.ScalarSubcoreMesh(axis_name="seq", num_cores=1), name=name,
               scratch_types=(pltpu.SemaphoreType.DMA((n_copies,)), pltpu.SemaphoreType.DMA((n_copies,))),
               compiler_params=pltpu.CompilerParams(collective_id=collective_id))
    def launch(send_sems, recv_sems):
        x, y, c, chips = _position()
        copies = plan(src_ref, dst_ref, x, y, c, chips)
        _handshake([peer for _, _, peer in copies])
        started = []
        for k, (s, d, peer) in enumerate(copies):
            cp = _remote(s, d, send_sems.at[k], recv_sems.at[k], peer)
            cp.start()
            started.append(cp)
        for cp in started:
            cp.wait()

    launch()
    return dst_ref[...]


class _AsyncReduceScatter:
    def __init__(self, grad, nm, first_id):
        self.grad, self.nm, self.first_id = grad, nm, first_id
        nq, rows, cols = grad.shape
        h = self.h = rows // 2

        def to_sibling(s, d, x, y, c, chips):
            return [(s.at[:, pl.ds((1 - c) * h, h), :], d, (x, y, 1 - c))]

        self.from_sibling = _sequencer_exchange(grad, jax.ShapeDtypeStruct((nq, h, cols), F32), first_id,
                                                f"rs_sibling_{nm}", to_sibling, 1)

    def sibling_sum(self, not_before=()):
        cols = self.grad.shape[2]
        place = jnp.stack([lax.axis_index("c"), 2 * lax.axis_index("x") + lax.axis_index("y")]).astype(jnp.int32)
        self.part, self.part_b = _add_sibling(self.grad, self.from_sibling, place, f"add_sibling_{self.nm}", not_before)

        def to_chips(s, d, x, y, c, chips):
            return [(s.at[2 * chip[0] + chip[1]], d.at[j], (chip[0], chip[1], c)) for j, chip in enumerate(chips)]

        self.from_chips = _sequencer_exchange(self.part_b, jax.ShapeDtypeStruct((3, self.h, cols), BF16),
                                              self.first_id + 1, f"rs_quarters_{self.nm}", to_chips, 3)
        return self.part_b

    def chip_sum(self, not_before=()):
        cols = self.grad.shape[2]
        self.half = _add_chips(self.part, self.from_chips, f"add_chips_{self.nm}", not_before)

        def whole_to_sibling(s, d, x, y, c, chips):
            return [(s, d, (x, y, 1 - c))]

        self.other = _sequencer_exchange(self.half, jax.ShapeDtypeStruct((self.h, cols), F32), self.first_id + 2,
                                         f"rs_share_{self.nm}", whole_to_sibling, 1)
        return self.half

    def share(self):
        return self.half, self.other


def _after(x, deps, name):
    def body(x_ref, *rest):
        rest[-1][...] = x_ref[...]

    vm = pl.BlockSpec(memory_space=pltpu.VMEM)
    return pl.pallas_call(body, in_specs=[vm] + [ANY] * len(deps), out_specs=vm,
                          out_shape=jax.ShapeDtypeStruct(x.shape, x.dtype), name=name)(x, *deps)


def _adamw_halves(w, mine, other, m, v, name):
    rows, cols = w.shape
    tr = SUM_ROWS
    nb = rows // 2 // tr
    c_arr = lax.axis_index("c").astype(jnp.int32).reshape(1)

    def body(c_ref, w_ref, a_ref, b_ref, m_ref, v_ref, g_out, d_out, m_out, v_out):
        is_mine = (pl.program_id(0) // nb) == c_ref[0]
        g = jnp.where(is_mine, a_ref[...], b_ref[...])
        wb, mb, vb = w_ref[...], m_ref[...], v_ref[...]
        m2 = ADAM_B1 * mb + (1.0 - ADAM_B1) * g
        v2 = ADAM_B2 * vb + (1.0 - ADAM_B2) * (g * g)
        m_hat = m2 / (1.0 - ADAM_B1 ** ADAM_STEP)
        v_hat = v2 / (1.0 - ADAM_B2 ** ADAM_STEP)
        g_out[...] = g
        d_out[...] = -ADAM_LR * (m_hat / (jnp.sqrt(v_hat) + ADAM_EPS) + ADAM_WD * wb)
        m_out[...] = m2
        v_out[...] = v2

    full = pl.BlockSpec((tr, cols), lambda i, c: (i, 0))
    mine_spec = pl.BlockSpec((tr, cols), lambda i, c: (jnp.where(i // nb == c[0], i % nb, 0), 0))
    other_spec = pl.BlockSpec((tr, cols), lambda i, c: (jnp.where(i // nb == c[0], 0, i % nb), 0))
    return pl.pallas_call(
        body,
        grid_spec=pltpu.PrefetchScalarGridSpec(
            num_scalar_prefetch=1, grid=(rows // tr,),
            in_specs=[full, mine_spec, other_spec, full, full], out_specs=[full] * 4),
        out_shape=[jax.ShapeDtypeStruct((rows, cols), F32)] * 4,
        compiler_params=_cparams("parallel"),
        name=name,
    )(c_arr, w, mine, other, m, v)


def _adamw_halves_t(w_t, mine_t, other_t, m_t, v_t, name):
    cols, rows = w_t.shape
    tr = cols // 11
    assert tr * 11 == cols and tr % 8 == 0
    c_arr = lax.axis_index("c").astype(jnp.int32).reshape(1)

    def body(c_ref, w_ref, a_ref, b_ref, m_ref, v_ref, g_out, d_out, m_out, v_out):
        first = c_ref[0] == 0
        a, b = a_ref[...], b_ref[...]
        g = jnp.concatenate([jnp.where(first, a, b), jnp.where(first, b, a)], axis=1)
        wb, mb, vb = w_ref[...], m_ref[...], v_ref[...]
        m2 = ADAM_B1 * mb + (1.0 - ADAM_B1) * g
        v2 = ADAM_B2 * vb + (1.0 - ADAM_B2) * (g * g)
        m_hat = m2 / (1.0 - ADAM_B1 ** ADAM_STEP)
        v_hat = v2 / (1.0 - ADAM_B2 ** ADAM_STEP)
        g_out[...] = g
        d_out[...] = -ADAM_LR * (m_hat / (jnp.sqrt(v_hat) + ADAM_EPS) + ADAM_WD * wb)
        m_out[...] = m2
        v_out[...] = v2

    full = pl.BlockSpec((tr, rows), lambda i, c: (i, 0))
    half = pl.BlockSpec((tr, rows // 2), lambda i, c: (i, 0))
    return pl.pallas_call(
        body,
        grid_spec=pltpu.PrefetchScalarGridSpec(
            num_scalar_prefetch=1, grid=(cols // tr,),
            in_specs=[full, half, half, full, full], out_specs=[full] * 4),
        out_shape=[jax.ShapeDtypeStruct((cols, rows), F32)] * 4,
        compiler_params=_cparams("parallel"),
        name=name,
    )(c_arr, w_t, mine_t, other_t, m_t, v_t)


def _all_sum_small(v):
    n_dev = 8

    def body(v_ref, o_ref, gath, send_sems, recv_sems):
        x, y, c, _ = _position()
        me = 4 * x + 2 * y + c
        gath[me] = v_ref[...]
        copies = []
        for k in range(1, n_dev):
            peer = tuple(1 - p if (k >> s) & 1 else p for p, s in ((x, 2), (y, 1), (c, 0)))
            cp = _remote(v_ref, gath.at[me], send_sems.at[k - 1], recv_sems.at[k - 1], peer)
            cp.start()
            copies.append(cp)
        for cp in copies:
            cp.wait()
        acc = gath[0]
        for i in range(1, n_dev):
            acc = acc + gath[i]
        o_ref[...] = acc

    vm = pl.BlockSpec(memory_space=pltpu.VMEM)
    return pl.pallas_call(
        body,
        in_specs=[vm],
        out_specs=vm,
        out_shape=jax.ShapeDtypeStruct(v.shape, F32),
        scratch_shapes=[pltpu.VMEM((n_dev,) + v.shape, F32), pltpu.SemaphoreType.DMA((n_dev - 1,)),
                        pltpu.SemaphoreType.DMA((n_dev - 1,))],
        name="all_sum_small",
    )(v)


def _pack_rows(vectors):
    rows = []
    for v in vectors:
        flat = v.reshape(-1).astype(F32)
        rows.append(jnp.pad(flat, (0, (-flat.shape[0]) % LANES)).reshape(-1, LANES))
    out = jnp.concatenate(rows, axis=0)
    return jnp.pad(out, ((0, (-out.shape[0]) % 8), (0, 0)))


def _unpack_rows(packed, shapes):
    outs, r = [], 0
    for shp in shapes:
        size = math.prod(shp)
        nr = -(-size // LANES)
        outs.append(packed[r:r + nr].reshape(-1)[:size].reshape(shp))
        r += nr
    return outs


def _relu_sq(acc):
    r = jnp.maximum(acc, 0.0)
    return r, r * r


def _relu_sq_bwd(acc, r):
    return (acc * (2.0 * r.astype(F32)),)


def kernel(x, norm_mix_pre, w_in, conv_w, conv_b, dt_bias, a_log, d_skip, ssm_norm_w, w_out, norm_mix_post, norm_mlp_pre, w_up, w_down, norm_mlp_post, loss_target, m_norm_mix_pre, m_w_in, m_conv_w, m_conv_b, m_dt_bias, m_a_log, m_d_skip, m_ssm_norm_w, m_w_out, m_norm_mix_post, m_norm_mlp_pre, m_w_up, m_w_down, m_norm_mlp_post, v_norm_mix_pre, v_w_in, v_conv_w, v_conv_b, v_dt_bias, v_a_log, v_d_skip, v_ssm_norm_w, v_w_out, v_norm_mix_post, v_norm_mlp_pre, v_w_up, v_w_down, v_norm_mlp_post):
    s_dim = x.shape[1]
    xs, target = x[0], loss_target[0]
    chip = 2 * lax.axis_index("x") + lax.axis_index("y")

    own = [w_in[0].astype(BF16), w_out[0].astype(BF16), w_up[0].astype(BF16), w_down[0].astype(BF16)]
    fetched_in = _gather_shards_async(own[:1], 14, "gather_w_in")[0]
    conv_cols = D_XBC // N_CHIPS
    conv_placed = lax.dynamic_update_slice(jnp.zeros((8, D_XBC), F32), 0.5 * conv_w[0], (0, chip * conv_cols))
    conv_full = _all_sum_small(conv_placed.reshape(-1, LANES)).reshape(8, D_XBC)
    w8 = _perm_cols(conv_full.at[CONV_WIDTH].set(conv_b[0]))
    u = _pre_norm(xs, norm_mix_pre)
    fetched_in, u, w8, *rest = lax.optimization_barrier((fetched_in, u, w8, *own[1:]))
    fetched = [fetched_in] + _gather_shards_async(rest, 1, "gather_rest")
    g_in, g_out, g_up, g_down = [_by_chip(o, f) for o, f in zip(own, fetched)]
    w_z = _cols_from_quarters(g_in, *IN_COLS["z"])
    w_xbc = _perm_cols(_cols_from_quarters(g_in, *IN_COLS["xbc"]))
    w_dt = jnp.pad(_cols_from_quarters(g_in, *IN_COLS["dt"]), ((0, 0), (0, LANES - SSM_HEADS)))
    w_qkv = _cols_from_quarters(g_in, *IN_COLS["qkv"])
    w_out_full = g_out.reshape(D_MIX, D_MODEL)
    w_down_full = g_down.reshape(D_FF, D_MODEL)

    z = _matmul([(u, w_z, TK)], "nn", [F32], name="proj_z")
    xbc = _matmul([(u, w_xbc, TK)], "nn", [F32], name="proj_xbc")
    dt_raw = _matmul([(u, w_dt, TK)], "nn", [F32], name="proj_dt")
    qkv = _matmul([(u, w_qkv, TK)], "nn", [BF16], name="proj_qkv")
    xc = _conv_fwd(xbc, w8)
    dtg = _dt_to_groups(dt_raw)
    par = _pack_ssd_params(dt_bias[0], a_log[0], d_skip[0])
    y, y_ssm, states = _ssd_fwd(xc, z, dtg, par, ssm_norm_w)
    y_att, y_att_f32, lse = _attn_fused_fwd(qkv)
    y_mix = jnp.concatenate([y_ssm, y_att], axis=1)
    mix = _matmul([(y_mix, w_out_full, TK)], "nn", [F32], name="out_proj")
    h1, u2 = _post_pre_norm(xs, mix, norm_mix_post, norm_mlp_pre)
    hid, act = _matmul([(u2, g_up, TK)], "nn", [BF16, BF16], name="mlp_up", epilogue=_relu_sq)
    ff = _matmul([(act, w_down_full, TK)], "nn", [F32], name="mlp_down")
    dh2, dff, d_g4, loss_part = _tail(ff, h1, target, norm_mlp_post)

    dhid = _matmul([(dff, w_down_full, TK)], "nt", [BF16], name="mlp_down_dx", epilogue=_relu_sq_bwd, extras=[hid])
    weights = {"norm_mix_pre": (norm_mix_pre, m_norm_mix_pre, v_norm_mix_pre), "w_in": (w_in, m_w_in, v_w_in),
               "conv_w": (conv_w, m_conv_w, v_conv_w), "conv_b": (conv_b, m_conv_b, v_conv_b),
               "dt_bias": (dt_bias, m_dt_bias, v_dt_bias), "a_log": (a_log, m_a_log, v_a_log),
               "d_skip": (d_skip, m_d_skip, v_d_skip), "ssm_norm_w": (ssm_norm_w, m_ssm_norm_w, v_ssm_norm_w),
               "w_out": (w_out, m_w_out, v_w_out), "norm_mix_post": (norm_mix_post, m_norm_mix_post, v_norm_mix_post),
               "norm_mlp_pre": (norm_mlp_pre, m_norm_mlp_pre, v_norm_mlp_pre), "w_up": (w_up, m_w_up, v_w_up),
               "w_down": (w_down, m_w_down, v_w_down),
               "norm_mlp_post": (norm_mlp_post, m_norm_mlp_post, v_norm_mlp_post)}
    grads, delta, new_m, new_v = {}, {}, {}, {}

    def adamw_big(n, halves):
        w, m, v = weights[n]
        g_, d_, m_, v_ = _adamw_halves(w[0], halves[0], halves[1], m[0], v[0], f"adamw_{n}")
        grads[n], delta[n], new_m[n], new_v[n] = g_[None], d_[None], m_[None], v_[None]

    dw_down = _matmul([(act, dff, TK)], "tn", [F32], name="mlp_down_dw")
    rs_down = _AsyncReduceScatter(dw_down.reshape(N_CHIPS, D_FF // N_CHIPS, D_MODEL), "w_down", 11)
    dw_up = _matmul([(u2, dhid, TK)], "tn", [F32], name="mlp_up_dw", deps=[dw_down], out_quarters=True)
    rs_up = _AsyncReduceScatter(dw_up, "w_up", 8)
    du2 = _matmul([(dhid, g_up, TK)], "nt", [F32], name="mlp_up_dx",
                  deps=[rs_down.sibling_sum(not_before=[dw_up])])
    dh1, dmix, d_g3, d_g2 = _mid_bwd(du2, h1, dh2, mix, norm_mix_post, norm_mlp_pre,
                                     deps=[rs_up.sibling_sum(not_before=[du2])])
    dymix = _matmul([(dmix, w_out_full, TK)], "nt", [F32], name="out_proj_dx")
    dw_out = _matmul([(y_mix, dmix, TK)], "tn", [F32], name="out_proj_dw")
    rs_out = _AsyncReduceScatter(dw_out.reshape(N_CHIPS, D_MIX // N_CHIPS, D_MODEL), "w_out", 5)
    dq, dkv = _attn_fused_bwd(qkv, dymix, y_att_f32, lse)
    dqkv = jnp.concatenate([dq[None], dkv], axis=0)
    par_late = _after(par, [rs_down.chip_sum(not_before=[dqkv]), rs_out.sibling_sum(not_before=[dymix])],
                      "after_w_down")
    dxc, dz, ddtg, dpar, d_nw = _ssd_bwd(xc, z, dtg, par_late, ssm_norm_w, y, states, dymix)
    g_down = rs_down.share()
    dxbc, dw8 = _conv_bwd(xbc, _after(w8, [*g_down, rs_up.chip_sum(not_before=[dxc])], "after_w_up"), dxc)
    ddt = jnp.pad(_dt_from_groups(ddtg), ((0, 0), (0, LANES - SSM_HEADS))).astype(BF16)
    g_up = rs_up.share()
    dw_z = _matmul([(u, dz, TK)], "tn", [F32], name="proj_z_dw")
    dw_xbc = _matmul([(u, dxbc, TK)], "tn", [F32], name="proj_xbc_dw",
                     deps=[*g_up, rs_out.chip_sum(not_before=[dxbc])])
    g_out = rs_out.share()
    dw_dt = _matmul([(u, ddt, TK)], "tn", [F32], name="proj_dt_dw")
    dw_qkv = _matmul([(u, dqkv, TK)], "tn", [F32], name="proj_qkv_dw")
    dw_in = _quarters_from_cols({"z": dw_z, "xbc": _unperm_cols(dw_xbc), "dt": dw_dt[:, :SSM_HEADS], "qkv": dw_qkv})
    rs_in = _AsyncReduceScatter(dw_in, "w_in", 2)
    adamw_big("w_down", g_down)
    adamw_big("w_up", g_up)
    rs_in.sibling_sum(not_before=[delta["w_up"]])
    du = _matmul([(dz, w_z, TK_MULTI), (dxbc, w_xbc, TK_MULTI), (dqkv, w_qkv, TK_MULTI), (ddt, w_dt, LANES)], "nt",
                 [F32], name="proj_dx", deps=[*g_out, rs_in.part_b])
    grad_x, d_g1 = _first_bwd(du, xs, dh1, norm_mix_pre)
    adamw_big("w_out", g_out)
    rs_in.chip_sum(not_before=[grad_x, delta["w_out"]])

    dconv = _unperm_cols(dw8)
    d_bias, d_alog, d_dskip = _unpack_ssd_params(dpar)
    small_shapes = [(1, D_MODEL), (CONV_WIDTH, D_XBC), (1, D_XBC), (1, SSM_HEADS), (1, SSM_HEADS), (1, SSM_HEADS),
                    (1, D_SSM), (1, D_MODEL), (1, D_MODEL), (1, D_MODEL), (1, LANES)]
    summed = _unpack_rows(
        _all_sum_small(_pack_rows([d_g1, dconv[:CONV_WIDTH], dconv[CONV_WIDTH:CONV_WIDTH + 1], d_bias, d_alog,
                                   d_dskip, d_nw, d_g2, d_g3, d_g4, loss_part])), small_shapes)
    (g_g1, g_conv_full, g_conv_b, g_bias, g_alog, g_dskip, g_nw, g_g2, g_g3, g_g4, loss_row) = summed
    loss = loss_row[0, 0]
    g_conv_w = lax.dynamic_slice(g_conv_full, (0, chip * conv_cols), (CONV_WIDTH, conv_cols))[None]

    grads.update({"norm_mix_pre": g_g1, "conv_w": g_conv_w, "conv_b": g_conv_b, "dt_bias": g_bias,
                  "a_log": g_alog, "d_skip": g_dskip, "ssm_norm_w": g_nw, "norm_mix_post": g_g2,
                  "norm_mlp_pre": g_g3, "norm_mlp_post": g_g4})
    order = list(weights)
    small_names = [n for n in order if n not in ("w_in", "w_out", "w_up", "w_down")]
    small_w_shapes = [weights[n][0].shape for n in small_names]
    packed = [_pack_rows([weights[n][k] for n in small_names]) for k in range(3)]
    packed_g = _pack_rows([grads[n].reshape(weights[n][0].shape) for n in small_names])
    sd, sm, sv = _adamw(packed[0], packed_g, packed[1], packed[2], "adamw_small")
    for k, n in enumerate(small_names):
        grads[n] = grads[n].reshape(weights[n][0].shape)
    for res, pk in ((delta, sd), (new_m, sm), (new_v, sv)):
        for n, val in zip(small_names, _unpack_rows(pk, small_w_shapes)):
            res[n] = val
    mine, other = rs_in.share()
    w_t, m_t, v_t = [jnp.swapaxes(a[0], 0, 1) for a in weights["w_in"]]
    results_t = _adamw_halves_t(w_t, mine.T, other.T, m_t, v_t, "adamw_w_in")
    grads["w_in"], delta["w_in"], new_m["w_in"], new_v["w_in"] = [jnp.swapaxes(r, 0, 1)[None] for r in results_t]

    return (loss, grad_x[None], *[grads[n] for n in order], *[delta[n] for n in order],
            *[new_m[n] for n in order], *[new_v[n] for n in order])
```

```python
import math

import numpy as np
import jax
import jax.numpy as jnp
from jax import lax
from jax.experimental import pallas as pl
from jax.experimental.pallas import tpu as pltpu
from jax.experimental.pallas import tpu_sc as plsc

F32 = jnp.float32
BF16 = jnp.bfloat16

D_MODEL = 2048
SSM_HEAD_DIM = 64
SSM_GROUPS = 8
HEADS_PER_GROUP = 4
SSM_HEADS = SSM_GROUPS * HEADS_PER_GROUP
D_SSM = SSM_HEADS * SSM_HEAD_DIM
D_STATE = 128
CONV_WIDTH = 4
SSD_CHUNK = 128
D_XBC = D_SSM + 2 * SSM_GROUPS * D_STATE
GROUP_X = HEADS_PER_GROUP * SSM_HEAD_DIM
GROUP_COLS = GROUP_X + 2 * D_STATE
ATT_HEAD_DIM = 128
ATT_HEADS = 16
D_ATT = ATT_HEADS * ATT_HEAD_DIM
DILATIONS = (1, 4, 16)
ATT_BLOCK = 128
D_MIX = D_SSM + D_ATT
D_IN_PROJ = D_SSM + D_XBC + SSM_HEADS + 3 * D_ATT
D_FF = 4 * D_MODEL
EPS = 1e-6
N_CHIPS = 4
W_IN_SHARD = D_IN_PROJ // N_CHIPS

ADAM_LR = 0.001
ADAM_B1 = 0.9
ADAM_B2 = 0.999
ADAM_EPS = 1e-08
ADAM_WD = 0.01
ADAM_STEP = 10

LANES = 128
VMEM_LIMIT = 48 * 1024 * 1024
MESH = pl.DeviceIdType.MESH

_NN = (((1,), (0,)), ((), ()))
_NT = (((1,), (1,)), ((), ()))
_TN = (((0,), (0,)), ((), ()))


def _dot(a, b, dims=_NN):
    return lax.dot_general(a, b, dims, preferred_element_type=F32)


def _cparams(*sem):
    return pltpu.CompilerParams(dimension_semantics=sem, vmem_limit_bytes=VMEM_LIMIT)


TK = 2048
TK_MULTI = 1024


def _matmul(pairs, mode, out_dtypes, *, name, tm=1024, tn=1024, epilogue=None, extras=(), deps=(), out_quarters=False):
    a0, b0, _ = pairs[0]
    m_dim = a0.shape[-1] if mode == "tn" else a0.shape[-2]
    if b0.ndim == 3:
        n_dim = b0.shape[1] if mode == "nt" else b0.shape[0] * b0.shape[2]
    else:
        n_dim = b0.shape[0] if mode == "nt" else b0.shape[1]
    tm, tn = min(tm, m_dim), min(tn, n_dim)
    nks, offs = [], []
    for a, _, tk in pairs:
        k_part = a.shape[0] if mode == "tn" else a.shape[-1]
        k_dim = k_part * (a.shape[0] if a.ndim == 3 else 1)
        assert k_part % tk == 0, (name, k_part, tk)
        offs.append(sum(nks))
        nks.append(k_dim // tk)
    nk_total = sum(nks)
    assert m_dim % tm == 0 and n_dim % tn == 0, (name, m_dim, n_dim)
    dims = {"nn": _NN, "nt": _NT, "tn": _TN}[mode]
    n_pairs, n_extra, n_out = len(pairs), len(extras), len(out_dtypes)

    in_specs, operands = [], []
    for (a, b, tk), off, nk in zip(pairs, offs, nks):
        def kidx(k, off=off, nk=nk):
            return k if n_pairs == 1 else jnp.clip(k - off, 0, nk - 1)
        if mode == "tn":
            assert a.ndim == 2
            in_specs.append(pl.BlockSpec((tk, tm), lambda m, n, k, f=kidx: (f(k), m)))
        elif a.ndim == 3:
            per = a.shape[2] // tk
            in_specs.append(pl.BlockSpec((None, tm, tk), lambda m, n, k, f=kidx, per=per: (f(k) // per, m, f(k) % per)))
        else:
            in_specs.append(pl.BlockSpec((tm, tk), lambda m, n, k, f=kidx: (m, f(k))))
        if b.ndim == 3 and mode == "nt":
            per = b.shape[2] // tk
            in_specs.append(pl.BlockSpec((None, tn, tk), lambda m, n, k, f=kidx, per=per: (f(k) // per, n, f(k) % per)))
        elif b.ndim == 3:
            per = b.shape[2] // tn
            in_specs.append(pl.BlockSpec((None, tk, tn), lambda m, n, k, f=kidx, per=per: (n // per, f(k), n % per)))
        elif mode == "nt":
            in_specs.append(pl.BlockSpec((tn, tk), lambda m, n, k, f=kidx: (n, f(k))))
        else:
            in_specs.append(pl.BlockSpec((tk, tn), lambda m, n, k, f=kidx: (f(k), n)))
        operands += [a, b]
    for e in extras:
        in_specs.append(pl.BlockSpec((tm, tn), lambda m, n, k: (m, n)))
        operands.append(e)
    in_specs += [pl.BlockSpec(memory_space=pl.ANY)] * len(deps)
    operands += list(deps)
    first_out = 2 * n_pairs + n_extra + len(deps)
    if out_quarters:
        out_per_q = n_dim // N_CHIPS // tn
        out_dims = (N_CHIPS, m_dim, n_dim // N_CHIPS)
        out_spec = pl.BlockSpec((None, tm, tn), lambda m, n, k: (n // out_per_q, m, n % out_per_q))
    else:
        out_dims = (m_dim, n_dim)
        out_spec = pl.BlockSpec((tm, tn), lambda m, n, k: (m, n))

    def body(*refs):
        ab = refs[:2 * n_pairs]
        e_refs = refs[2 * n_pairs:2 * n_pairs + n_extra]
        o_refs = refs[first_out:first_out + n_out]

        def finish(total):
            vals = (total,) if epilogue is None else epilogue(total, *[e[...] for e in e_refs])
            for o_ref, v in zip(o_refs, vals):
                o_ref[...] = v.astype(o_ref.dtype)

        if nk_total == 1:
            finish(_dot(ab[0][...], ab[1][...], dims))
            return
        acc = refs[-1]
        k = pl.program_id(2)

        @pl.when(k == 0)
        def _():
            acc[...] = jnp.zeros_like(acc)

        for i in range(n_pairs):
            def accumulate(i=i):
                acc[...] += _dot(ab[2 * i][...], ab[2 * i + 1][...], dims)
            if n_pairs == 1:
                accumulate()
            else:
                pl.when((k >= offs[i]) & (k < offs[i] + nks[i]))(accumulate)

        @pl.when(k == nk_total - 1)
        def _():
            finish(acc[...])

    outs = pl.pallas_call(
        body,
        grid=(m_dim // tm, n_dim // tn, nk_total),
        in_specs=in_specs,
        out_specs=[out_spec for _ in out_dtypes],
        out_shape=[jax.ShapeDtypeStruct(out_dims, dt) for dt in out_dtypes],
        scratch_shapes=[pltpu.VMEM((tm, tn), F32)] if nk_total > 1 else [],
        compiler_params=_cparams("parallel", "parallel", "arbitrary"),
        name=name,
    )(*operands)
    return outs[0] if n_out == 1 else outs


def _rowcall(fn, rows, vecs, row_outs, acc_widths, *, name, tr=256, row_cols=None, deps=()):
    s_dim = rows[0].shape[0]
    assert s_dim % tr == 0
    row_cols = row_cols or [None] * len(rows)
    n_r, n_v, n_ro, n_acc = len(rows), len(vecs), len(row_outs), len(acc_widths)
    in_specs = []
    for r, rc in zip(rows, row_cols):
        if rc is None:
            in_specs.append(pl.BlockSpec((tr, r.shape[1]), lambda i: (i, 0)))
        else:
            in_specs.append(pl.BlockSpec((tr, rc[0]), lambda i, c=rc[1]: (i, c)))
    for v in vecs:
        in_specs.append(pl.BlockSpec(v.shape, lambda i, nd=v.ndim: (0,) * nd))
    in_specs += [pl.BlockSpec(memory_space=pl.ANY)] * len(deps)
    n_d = len(deps)

    def body(*refs):
        ins = [r[...] for r in refs[:n_r + n_v]]
        ro = refs[n_r + n_v + n_d:n_r + n_v + n_d + n_ro]
        ao = refs[n_r + n_v + n_d + n_ro:]
        outs = fn(*ins)
        for ref, v in zip(ro, outs[:n_ro]):
            ref[...] = v.astype(ref.dtype)
        if n_acc:
            @pl.when(pl.program_id(0) == 0)
            def _():
                for ref in ao:
                    ref[...] = jnp.zeros_like(ref)
            for ref, v in zip(ao, outs[n_ro:]):
                ref[...] += v

    outs = pl.pallas_call(
        body,
        grid=(s_dim // tr,),
        in_specs=in_specs,
        out_specs=[pl.BlockSpec((tr, w), lambda i: (i, 0)) for w, _ in row_outs]
        + [pl.BlockSpec((1, w), lambda i: (0, 0)) for w in acc_widths],
        out_shape=[jax.ShapeDtypeStruct((s_dim, w), dt) for w, dt in row_outs]
        + [jax.ShapeDtypeStruct((1, w), F32) for w in acc_widths],
        compiler_params=_cparams("arbitrary"),
        name=name,
    )(*rows, *vecs, *deps)
    return outs


def _nrm(x, g):
    r = lax.rsqrt(jnp.mean(x * x, axis=-1, keepdims=True) + EPS)
    n = x * r
    return n * g, n, r


def _nrm_bwd(dy, n, r, g):
    dn = dy * g
    dx = r * (dn - n * jnp.mean(dn * n, axis=-1, keepdims=True))
    return dx, jnp.sum(dy * n, axis=0, keepdims=True)


def _sigmoid(x):
    return 1.0 / (1.0 + jnp.exp(-x))


def _softplus(x):
    return jnp.maximum(x, 0.0) + jnp.log(1.0 + jnp.exp(-jnp.abs(x)))


def _pre_norm(x, g1):
    def fn(xb, g):
        return (_nrm(xb, g)[0],)
    return _rowcall(fn, [x], [g1], [(D_MODEL, BF16)], [], name="pre_norm")[0]


def _post_pre_norm(x, mix, g2, g3):
    def fn(xb, mb, g2b, g3b):
        h1 = xb + _nrm(mb, g2b)[0]
        return h1, _nrm(h1, g3b)[0]
    return _rowcall(fn, [x, mix], [g2, g3], [(D_MODEL, F32), (D_MODEL, BF16)], [], name="post_pre_norm")


def _tail(ff, h1, target, g4):
    def fn(ffb, h1b, tb, g):
        y, n, r = _nrm(ffb, g)
        e = h1b + y - tb
        loss = 0.5 * jnp.sum(jnp.sum(e * e, axis=-1, keepdims=True) * (1.0 / D_MODEL), axis=0, keepdims=True)
        dh2 = e * (1.0 / D_MODEL)
        dff, dg = _nrm_bwd(dh2, n, r, g)
        return dh2, dff, dg, jnp.broadcast_to(loss, (1, LANES))
    return _rowcall(fn, [ff, h1, target], [g4], [(D_MODEL, F32), (D_MODEL, BF16)], [D_MODEL, LANES], name="tail")


def _mid_bwd(du2, h1, dh2, mix, g2, g3, deps=()):
    def fn(du2b, h1b, dh2b, mb, g2b, g3b):
        _, n3, r3 = _nrm(h1b, g3b)
        d3, dg3 = _nrm_bwd(du2b, n3, r3, g3b)
        dh1 = dh2b + d3
        _, n2, r2 = _nrm(mb, g2b)
        dmix, dg2 = _nrm_bwd(dh1, n2, r2, g2b)
        return dh1, dmix, dg3, dg2
    return _rowcall(fn, [du2, h1, dh2, mix], [g2, g3], [(D_MODEL, F32), (D_MODEL, BF16)], [D_MODEL, D_MODEL],
                    name="mid_bwd", deps=deps)


def _first_bwd(du, x, dh1, g1):
    def fn(dub, xb, dh1b, g):
        _, n, r = _nrm(xb, g)
        dx, dg = _nrm_bwd(dub, n, r, g)
        return dh1b + dx, dg
    return _rowcall(fn, [du, x, dh1], [g1], [(D_MODEL, F32)], [D_MODEL], name="first_bwd")


CONV_TILE = 256
CONV_ROWS = 256
PAD = 8


def _conv_taps(w):
    return [w[k:k + 1, :] for k in range(CONV_WIDTH)], w[CONV_WIDTH:CONV_WIDTH + 1, :]


def _conv_fwd(xbc, w8):
    s_dim, c_dim = xbc.shape
    n_steps = s_dim // CONV_ROWS

    def body(x_ref, w_ref, o_ref, xp):
        xp[0:PAD, :] = jnp.zeros((PAD, CONV_TILE), F32)
        xp[PAD:PAD + s_dim, :] = x_ref[...]
        taps, bias = _conv_taps(w_ref[...])

        def step(c, carry):
            base = pl.multiple_of(c * CONV_ROWS, CONV_ROWS)
            win = xp[pl.ds(base, CONV_ROWS + PAD), :]
            pre = bias + taps[3] * win[PAD:, :]
            for j in range(1, CONV_WIDTH):
                pre = pre + taps[3 - j] * pltpu.roll(win, j, axis=0)[PAD:, :]
            o_ref[pl.ds(base, CONV_ROWS), :] = pre * _sigmoid(pre)
            return carry

        lax.fori_loop(0, n_steps, step, 0, unroll=2)

    return pl.pallas_call(
        body,
        grid=(c_dim // CONV_TILE,),
        in_specs=[pl.BlockSpec((s_dim, CONV_TILE), lambda j: (0, j)), pl.BlockSpec((8, CONV_TILE), lambda j: (0, j))],
        out_specs=pl.BlockSpec((s_dim, CONV_TILE), lambda j: (0, j)),
        out_shape=jax.ShapeDtypeStruct((s_dim, c_dim), F32),
        scratch_shapes=[pltpu.VMEM((s_dim + 2 * PAD, CONV_TILE), F32)],
        compiler_params=_cparams("parallel"),
        name="conv_fwd",
    )(xbc, w8)


def _conv_bwd(xbc, w8, dxc):
    s_dim, c_dim = xbc.shape
    n_steps = s_dim // CONV_ROWS

    def body(x_ref, w_ref, d_ref, dx_ref, dw_ref, xp, dp):
        xp[0:PAD, :] = jnp.zeros((PAD, CONV_TILE), F32)
        xp[PAD:PAD + s_dim, :] = x_ref[...]
        dp[PAD + s_dim:, :] = jnp.zeros((PAD, CONV_TILE), F32)
        taps, bias = _conv_taps(w_ref[...])

        def step1(c, sums):
            base = pl.multiple_of(c * CONV_ROWS, CONV_ROWS)
            win = xp[pl.ds(base, CONV_ROWS + PAD), :]
            shifted = [win[PAD:, :]] + [pltpu.roll(win, j, axis=0)[PAD:, :] for j in range(1, CONV_WIDTH)]
            pre = bias
            for j in range(CONV_WIDTH):
                pre = pre + taps[3 - j] * shifted[j]
            sg = _sigmoid(pre)
            dpre = d_ref[pl.ds(base, CONV_ROWS), :] * (sg * (1.0 + pre * (1.0 - sg)))
            dp[pl.ds(base + PAD, CONV_ROWS), :] = dpre
            new = [sums[k] + jnp.sum(dpre * shifted[3 - k], axis=0, keepdims=True) for k in range(CONV_WIDTH)]
            new.append(sums[CONV_WIDTH] + jnp.sum(dpre, axis=0, keepdims=True))
            return tuple(new)

        zero = jnp.zeros((1, CONV_TILE), F32)
        sums = lax.fori_loop(0, n_steps, step1, (zero,) * (CONV_WIDTH + 1), unroll=2)
        dw_ref[...] = jnp.zeros((8, CONV_TILE), F32)
        for k in range(CONV_WIDTH + 1):
            dw_ref[k:k + 1, :] = sums[k]

        def step2(c, carry):
            base = pl.multiple_of(c * CONV_ROWS, CONV_ROWS)
            win = dp[pl.ds(base + PAD, CONV_ROWS + PAD), :]
            dx = taps[3] * win[:CONV_ROWS, :]
            for j in range(1, CONV_WIDTH):
                dx = dx + taps[3 - j] * pltpu.roll(win, CONV_ROWS + PAD - j, axis=0)[:CONV_ROWS, :]
            dx_ref[pl.ds(base, CONV_ROWS), :] = dx.astype(BF16)
            return carry

        lax.fori_loop(0, n_steps, step2, 0, unroll=2)

    col = lambda j: (0, j)
    return pl.pallas_call(
        body,
        grid=(c_dim // CONV_TILE,),
        in_specs=[pl.BlockSpec((s_dim, CONV_TILE), col), pl.BlockSpec((8, CONV_TILE), col),
                  pl.BlockSpec((s_dim, CONV_TILE), col)],
        out_specs=[pl.BlockSpec((s_dim, CONV_TILE), col), pl.BlockSpec((8, CONV_TILE), col)],
        out_shape=[jax.ShapeDtypeStruct((s_dim, c_dim), BF16), jax.ShapeDtypeStruct((8, c_dim), F32)],
        scratch_shapes=[pltpu.VMEM((s_dim + 2 * PAD, CONV_TILE), F32), pltpu.VMEM((s_dim + 2 * PAD, CONV_TILE), F32)],
        compiler_params=_cparams("parallel"),
        name="conv_bwd",
    )(xbc, w8, dxc)


def _perm_cols(a):
    parts = []
    for g in range(SSM_GROUPS):
        parts += [a[..., g * GROUP_X:(g + 1) * GROUP_X],
                  a[..., D_SSM + g * D_STATE:D_SSM + (g + 1) * D_STATE],
                  a[..., D_SSM + SSM_GROUPS * D_STATE + g * D_STATE:D_SSM + SSM_GROUPS * D_STATE + (g + 1) * D_STATE]]
    return jnp.concatenate(parts, axis=-1)


def _unperm_cols(a):
    xs = [a[..., g * GROUP_COLS:g * GROUP_COLS + GROUP_X] for g in range(SSM_GROUPS)]
    bs = [a[..., g * GROUP_COLS + GROUP_X:g * GROUP_COLS + GROUP_X + D_STATE] for g in range(SSM_GROUPS)]
    cs = [a[..., g * GROUP_COLS + GROUP_X + D_STATE:(g + 1) * GROUP_COLS] for g in range(SSM_GROUPS)]
    return jnp.concatenate(xs + bs + cs, axis=-1)


def _dt_to_groups(dt):
    s_dim = dt.shape[0]
    t = dt[:, :SSM_HEADS].reshape(s_dim, SSM_GROUPS, HEADS_PER_GROUP).transpose(1, 0, 2)
    return jnp.pad(t, ((0, 0), (0, 0), (0, LANES - HEADS_PER_GROUP)))


def _dt_from_groups(dtg):
    s_dim = dtg.shape[1]
    return dtg[:, :, :HEADS_PER_GROUP].transpose(1, 0, 2).reshape(s_dim, SSM_HEADS)


def _pack_ssd_params(dt_bias, a_log, d_skip):
    rows = jnp.stack([p.reshape(SSM_GROUPS, HEADS_PER_GROUP) for p in (dt_bias, a_log, d_skip)], axis=1)
    return jnp.pad(rows, ((0, 0), (0, 8 - 3), (0, LANES - HEADS_PER_GROUP)))


def _unpack_ssd_params(par):
    return tuple(par[:, k, :HEADS_PER_GROUP].reshape(SSM_HEADS) for k in range(3))


Q = SSD_CHUNK


def _split3(v):
    hi = v.astype(BF16)
    r1 = v - hi.astype(F32)
    mid = r1.astype(BF16)
    lo = (r1 - mid.astype(F32)).astype(BF16)
    return hi, mid, lo


def _dot_l01(t01, v):
    return sum(_dot(t01, p) for p in _split3(v))


def _dot_r01(v, e01):
    return sum(_dot(p, e01) for p in _split3(v))


def _ssd_consts():
    row = lax.broadcasted_iota(jnp.int32, (Q, Q), 0)
    col = lax.broadcasted_iota(jnp.int32, (Q, Q), 1)
    causal = row >= col
    tril = causal.astype(BF16)
    triu = (col >= row).astype(BF16)
    er = lax.broadcasted_iota(jnp.int32, (LANES, GROUP_X), 0)
    ec = lax.broadcasted_iota(jnp.int32, (LANES, GROUP_X), 1) // SSM_HEAD_DIM
    expand = (er == ec).astype(BF16)
    rr = lax.broadcasted_iota(jnp.int32, (GROUP_X, LANES), 0) // SSM_HEAD_DIM
    rc = lax.broadcasted_iota(jnp.int32, (GROUP_X, LANES), 1)
    reduce = (rr == rc).astype(BF16)
    lane_head = lax.broadcasted_iota(jnp.int32, (Q, GROUP_X), 1) // SSM_HEAD_DIM
    return causal, tril, triu, expand, reduce, lane_head


def _ssd_common(xc_ref, dt_ref, par_ref, consts):
    causal, tril, _, expand, _, _ = consts
    par = par_ref[...]
    bias, alog, dsk = par[0:1, :], par[1:2, :], par[2:3, :]
    a_neg = -jnp.exp(alog)
    dtr = dt_ref[...] + bias
    dt = _softplus(dtr)
    s = _dot_l01(tril, dt * a_neg)
    dt_x = _dot_r01(dt, expand)
    s_x = _dot_r01(s, expand)
    dsk_x = _dot_r01(jnp.broadcast_to(dsk, (8, LANES)), expand)[0:1, :]
    blk = xc_ref[...]
    x = blk[:, :GROUP_X]
    bm = blk[:, GROUP_X:GROUP_X + D_STATE].astype(BF16)
    cm = blk[:, GROUP_X + D_STATE:].astype(BF16)
    xdt = x * dt_x
    g = _dot(cm, bm, _NT)
    return dict(a_neg=a_neg, dtr=dtr, dt=dt, s=s, s_t=s.T, dt_x=dt_x, s_x=s_x, dsk_x=dsk_x, x=x, bm=bm, cm=cm,
                xdt=xdt, g=g)


def _decay(v, r, causal):
    diff = v["s"][:, r:r + 1] - v["s_t"][r:r + 1, :]
    return jnp.exp(jnp.where(causal, diff, -jnp.inf))


def _ssd_specs(n_chunks, rev):
    cidx = (lambda c: n_chunks - 1 - c) if rev else (lambda c: c)
    xc = pl.BlockSpec((Q, GROUP_COLS), lambda g, c: (cidx(c), g))
    gx = pl.BlockSpec((Q, GROUP_X), lambda g, c: (cidx(c), g))
    dt = pl.BlockSpec((None, Q, LANES), lambda g, c: (g, cidx(c), 0))
    par = pl.BlockSpec((None, 8, LANES), lambda g, c: (g, 0, 0))
    nw = pl.BlockSpec((1, GROUP_X), lambda g, c: (0, g))
    hs = pl.BlockSpec((None, None, D_STATE, GROUP_X), lambda g, c: (cidx(c), g, 0, 0))
    return xc, gx, dt, par, nw, hs


def _ssd_fwd(xc, z, dtg, par, nw):
    s_dim = xc.shape[0]
    n_chunks = s_dim // Q
    xc_s, gx_s, dt_s, par_s, nw_s, hs_s = _ssd_specs(n_chunks, False)

    def body(xc_ref, z_ref, dt_ref, par_ref, nw_ref, y_ref, ys_ref, hs_ref, ht):
        @pl.when(pl.program_id(1) == 0)
        def _():
            ht[...] = jnp.zeros_like(ht)

        consts = _ssd_consts()
        causal, lane_head = consts[0], consts[5]
        v = _ssd_common(xc_ref, dt_ref, par_ref, consts)
        xdt_b = v["xdt"].astype(BF16)
        yd = jnp.zeros((Q, GROUP_X), F32)
        for r in range(HEADS_PER_GROUP):
            m = (v["g"] * _decay(v, r, causal)).astype(BF16)
            yd = yd + _dot(m, jnp.where(lane_head == r, xdt_b, jnp.zeros_like(xdt_b)))
        h = ht[...]
        hs_ref[...] = h
        yo = jnp.exp(v["s_x"]) * _dot(v["cm"], h.astype(BF16))
        y = yd + yo + v["dsk_x"] * v["x"]
        s_last = v["s_x"][Q - 1:Q, :]
        snew = _dot(v["bm"], (v["xdt"] * jnp.exp(s_last - v["s_x"])).astype(BF16), _TN)
        ht[...] = jnp.exp(s_last) * h + snew
        zz = z_ref[...]
        yg = y * (zz * _sigmoid(zz))
        y_ref[...] = y
        ys_ref[...] = _nrm(yg, nw_ref[...])[0].astype(BF16)

    return pl.pallas_call(
        body,
        grid=(SSM_GROUPS, n_chunks),
        in_specs=[xc_s, gx_s, dt_s, par_s, nw_s],
        out_specs=[gx_s, gx_s, hs_s],
        out_shape=[jax.ShapeDtypeStruct((s_dim, D_SSM), F32), jax.ShapeDtypeStruct((s_dim, D_SSM), BF16),
                   jax.ShapeDtypeStruct((n_chunks, SSM_GROUPS, D_STATE, GROUP_X), F32)],
        scratch_shapes=[pltpu.VMEM((D_STATE, GROUP_X), F32)],
        compiler_params=_cparams("parallel", "arbitrary"),
        name="ssd_fwd",
    )(xc, z, dtg, par, nw)


def _ssd_bwd(xc, z, dtg, par, nw, y, hs, dymix):
    s_dim = xc.shape[0]
    n_chunks = s_dim // Q
    xc_s, gx_s, dt_s, par_s, nw_s, hs_s = _ssd_specs(n_chunks, True)

    def body(xc_ref, z_ref, dt_ref, par_ref, nw_ref, y_ref, hs_ref, dys_ref,
             dxc_ref, dz_ref, ddt_ref, dpar_ref, dnw_ref, dht):
        @pl.when(pl.program_id(1) == 0)
        def _():
            dht[...] = jnp.zeros_like(dht)
            dpar_ref[...] = jnp.zeros_like(dpar_ref)
            dnw_ref[...] = jnp.zeros_like(dnw_ref)

        consts = _ssd_consts()
        causal, _, triu, _, reduce, lane_head = consts
        v = _ssd_common(xc_ref, dt_ref, par_ref, consts)
        x, bm, cm, xdt, s_x = v["x"], v["bm"], v["cm"], v["xdt"], v["s_x"]
        h = hs_ref[...]
        hb = h.astype(BF16)
        es_x = jnp.exp(s_x)
        yo = es_x * _dot(cm, hb)
        s_last = s_x[Q - 1:Q, :]
        e_x = jnp.exp(s_last - s_x)
        es_last = jnp.exp(s_last)

        yv, zz, nw_v = y_ref[...], z_ref[...], nw_ref[...]
        sg = _sigmoid(zz)
        gz = zz * sg
        _, n, rstd = _nrm(yv * gz, nw_v)
        dout = dys_ref[...]
        dyg, dnw = _nrm_bwd(dout, n, rstd, nw_v)
        dnw_ref[...] += dnw
        dy = dyg * gz
        dz_ref[...] = (dyg * yv * (sg * (1.0 + zz * (1.0 - sg)))).astype(BF16)

        dyb = dy.astype(BF16)
        xdt_b = xdt.astype(BF16)
        dhp = dht[...]
        dhpb = dhp.astype(BF16)
        lane = lax.broadcasted_iota(jnp.int32, (Q, LANES), 1)
        sub = lax.broadcasted_iota(jnp.int32, (LANES, Q), 0)
        dxdt = jnp.zeros((Q, GROUP_X), F32)
        dg = jnp.zeros((Q, Q), F32)
        ds = jnp.zeros((Q, LANES), F32)
        ds_t = jnp.zeros((LANES, Q), F32)
        for r in range(HEADS_PER_GROUP):
            dec = _decay(v, r, causal)
            mf = v["g"] * dec
            dyr = jnp.where(lane_head == r, dyb, jnp.zeros_like(dyb))
            dm = _dot(dyr, xdt_b, _NT)
            dxdt = dxdt + _dot(mf.astype(BF16), dyr, _TN)
            dg = dg + dm * dec
            dd = dm * mf
            ds = ds + jnp.where(lane == r, jnp.sum(dd, axis=1, keepdims=True), 0.0)
            ds_t = ds_t + jnp.where(sub == r, jnp.sum(dd, axis=0, keepdims=True), 0.0)
        ds = ds - ds_t.T
        dgb = dg.astype(BF16)
        dwb = (es_x * dy).astype(BF16)
        dcm = _dot(dgb, bm) + _dot(dwb, hb, _NT)
        dh_prev = _dot(cm, dwb, _TN)
        zst = _dot(bm, dhpb)
        xe = xdt * e_x
        dxdt = dxdt + e_x * zst
        dee = xe * zst
        dbm = _dot(dgb, cm, _TN) + _dot(xe.astype(BF16), dhpb, _NT)
        v_last = jnp.sum(dee, axis=0, keepdims=True) + es_last * jnp.sum(dhp * h, axis=0, keepdims=True)
        row_x = lax.broadcasted_iota(jnp.int32, (Q, GROUP_X), 0)
        tx = dy * yo - dee + jnp.where(row_x == Q - 1, v_last, 0.0)
        ds = ds + _dot_r01(tx, reduce)
        ddta = _dot_l01(triu, ds)
        ddt = ddta * v["a_neg"] + _dot_r01(dxdt * x, reduce)
        dalog = jnp.sum(ddta * v["dt"], axis=0, keepdims=True) * v["a_neg"]
        draw = jnp.where(lane < HEADS_PER_GROUP, ddt * _sigmoid(v["dtr"]), 0.0)
        dbias = jnp.sum(draw, axis=0, keepdims=True)
        ddsk = _dot_r01(jnp.broadcast_to(jnp.sum(dy * x, axis=0, keepdims=True), (8, GROUP_X)), reduce)[0:1, :]
        dht[...] = es_last * dhp + dh_prev
        dxc_ref[:, :GROUP_X] = dxdt * v["dt_x"] + v["dsk_x"] * dy
        dxc_ref[:, GROUP_X:GROUP_X + D_STATE] = dbm
        dxc_ref[:, GROUP_X + D_STATE:] = dcm
        ddt_ref[...] = draw
        dpar_ref[0:1, :] += dbias
        dpar_ref[1:2, :] += dalog
        dpar_ref[2:3, :] += ddsk

    return pl.pallas_call(
        body,
        grid=(SSM_GROUPS, n_chunks),
        in_specs=[xc_s, gx_s, dt_s, par_s, nw_s, gx_s, hs_s, gx_s],
        out_specs=[xc_s, gx_s, dt_s, par_s, nw_s],
        out_shape=[jax.ShapeDtypeStruct((s_dim, SSM_GROUPS * GROUP_COLS), F32),
                   jax.ShapeDtypeStruct((s_dim, D_SSM), BF16),
                   jax.ShapeDtypeStruct((SSM_GROUPS, s_dim, LANES), F32),
                   jax.ShapeDtypeStruct((SSM_GROUPS, 8, LANES), F32),
                   jax.ShapeDtypeStruct((1, D_SSM), F32)],
        scratch_shapes=[pltpu.VMEM((D_STATE, GROUP_X), F32)],
        compiler_params=_cparams("parallel", "arbitrary"),
        name="ssd_bwd",
    )(xc, z, dtg, par, nw, y, hs, dymix)


ATT_SCALE = ATT_HEAD_DIM ** -0.5
NEG_INF = -jnp.inf


def _band_masks():
    qi = lax.broadcasted_iota(jnp.int32, (ATT_BLOCK, ATT_BLOCK), 0)
    kj = lax.broadcasted_iota(jnp.int32, (ATT_BLOCK, ATT_BLOCK), 1)
    return kj <= qi, kj >= qi


WIN = ATT_BLOCK * DILATIONS[-1]
N_BLOCKS = WIN // ATT_BLOCK


def _rows(start, d):
    return pl.ds(start, ATT_BLOCK) if d == 1 else pl.ds(start, ATT_BLOCK, stride=d)


def _block_start(idx, d):
    return (idx // d) * (ATT_BLOCK * d) + idx % d


def _lane_bcast(col):
    return jnp.broadcast_to(col, (col.shape[0], LANES))


def _attn_fused_fwd(qkv):
    s_dim = qkv.shape[0]
    n_win = s_dim // WIN
    blk = (WIN, ATT_HEAD_DIM)
    prev = lambda w: jnp.maximum(w - 1, 0)

    def body(q_ref, kc_ref, kp_ref, vc_ref, vp_ref, y_ref, yf_ref, lse_ref, qf, kf, vf, acc, m_run, l_run):
        w, h = pl.program_id(0), pl.program_id(1)
        qf[...] = q_ref[...].astype(F32)
        kf[0:WIN, :] = kp_ref[...].astype(F32)
        kf[WIN:, :] = kc_ref[...].astype(F32)
        vf[0:WIN, :] = vp_ref[...].astype(F32)
        vf[WIN:, :] = vc_ref[...].astype(F32)
        own, before = _band_masks()

        for d in DILATIONS:
            def block(idx, carry, d=d):
                start = _block_start(idx, d)
                rows = _rows(start, d)
                q = qf[rows, :].astype(BF16)
                kc, vc = kf[_rows(WIN + start, d), :].astype(BF16), vf[_rows(WIN + start, d), :].astype(BF16)
                kp = kf[_rows(WIN + start - ATT_BLOCK * d, d), :].astype(BF16)
                vp = vf[_rows(WIN + start - ATT_BLOCK * d, d), :].astype(BF16)
                has_prev = (idx >= d) | (w > 0)
                sc = jnp.where(own, _dot(q, kc, _NT) * ATT_SCALE, NEG_INF)
                sp = jnp.where(before & has_prev, _dot(q, kp, _NT) * ATT_SCALE, NEG_INF)
                m_blk = jnp.maximum(jnp.max(sc, axis=1, keepdims=True), jnp.max(sp, axis=1, keepdims=True))
                if d == DILATIONS[0]:
                    m_new = m_blk
                else:
                    m_old = m_run[rows, :][:, 0:1]
                    m_new = jnp.maximum(m_old, m_blk)
                pc, pp = jnp.exp(sc - m_new), jnp.exp(sp - m_new)
                l_new = jnp.sum(pc, axis=1, keepdims=True) + jnp.sum(pp, axis=1, keepdims=True)
                o_new = _dot(pc.astype(BF16), vc) + _dot(pp.astype(BF16), vp)
                if d != DILATIONS[0]:
                    alpha = jnp.exp(m_old - m_new)
                    l_new = alpha * l_run[rows, :][:, 0:1] + l_new
                    o_new = alpha * acc[rows, :] + o_new
                m_run[rows, :] = _lane_bcast(m_new)
                l_run[rows, :] = _lane_bcast(l_new)
                acc[rows, :] = o_new
                return carry

            for idx in range(N_BLOCKS):
                block(idx, 0)

        l_all = l_run[...]
        y = acc[...] / l_all
        y_ref[...] = y.astype(BF16)
        yf_ref[...] = y
        @pl.when(h == 0)
        def _():
            lse_ref[...] = jnp.zeros_like(lse_ref)

        lane = lax.broadcasted_iota(jnp.int32, (WIN, LANES), 1)
        lse_ref[...] = jnp.where(lane == h, m_run[...] + jnp.log(l_all), lse_ref[...])

    win_scratch = lambda rows: pltpu.VMEM((rows, ATT_HEAD_DIM), F32)
    return pl.pallas_call(
        body,
        grid=(n_win, ATT_HEADS),
        in_specs=[pl.BlockSpec(blk, lambda w, h: (w, h)),
                  pl.BlockSpec(blk, lambda w, h: (w, ATT_HEADS + h)),
                  pl.BlockSpec(blk, lambda w, h: (prev(w), ATT_HEADS + h)),
                  pl.BlockSpec(blk, lambda w, h: (w, 2 * ATT_HEADS + h)),
                  pl.BlockSpec(blk, lambda w, h: (prev(w), 2 * ATT_HEADS + h))],
        out_specs=[pl.BlockSpec(blk, lambda w, h: (w, h)), pl.BlockSpec(blk, lambda w, h: (w, h)),
                   pl.BlockSpec((WIN, LANES), lambda w, h: (w, 0))],
        out_shape=[jax.ShapeDtypeStruct((s_dim, D_ATT), BF16), jax.ShapeDtypeStruct((s_dim, D_ATT), F32),
                   jax.ShapeDtypeStruct((s_dim, LANES), F32)],
        scratch_shapes=[win_scratch(WIN), win_scratch(2 * WIN), win_scratch(2 * WIN), win_scratch(WIN),
                        win_scratch(WIN), win_scratch(WIN)],
        compiler_params=_cparams("parallel", "arbitrary"),
        name="attn_fused_fwd",
    )(qkv, qkv, qkv, qkv, qkv)


def _attn_fused_bwd(qkv, dymix, y_att, lse, deps=()):
    s_dim = qkv.shape[0]
    n_win = s_dim // WIN
    blk = (WIN, ATT_HEAD_DIM)
    this = lambda w: jnp.minimum(w, n_win - 1)
    prev = lambda w: jnp.maximum(this(w) - 1, 0)
    n_dep = len(deps)

    def body(q_ref, kc_ref, kp_ref, vc_ref, vp_ref, dy_ref, y_ref, l_ref, *rest):
        dq_ref, dkv_ref = rest[n_dep:n_dep + 2]
        qf, kf, vf, dq_acc, dk_acc, dv_acc, ls_c, dl_c = rest[n_dep + 2:]
        h, w = pl.program_id(0), pl.program_id(1)
        slot, late = w % 2, 1 - w % 2

        @pl.when(w == 0)
        def _():
            dk_acc[...] = jnp.zeros_like(dk_acc)
            dv_acc[...] = jnp.zeros_like(dv_acc)

        @pl.when(w < n_win)
        def _():
            qf[...] = q_ref[...].astype(F32)
            kf[0:WIN, :] = kp_ref[...].astype(F32)
            kf[WIN:, :] = kc_ref[...].astype(F32)
            vf[0:WIN, :] = vp_ref[...].astype(F32)
            vf[WIN:, :] = vc_ref[...].astype(F32)
            lane = lax.broadcasted_iota(jnp.int32, (WIN, LANES), 1)
            ls_c[...] = _lane_bcast(jnp.sum(jnp.where(lane == h, l_ref[...], 0.0), axis=1, keepdims=True))
            dl_c[...] = _lane_bcast(jnp.sum(dy_ref[...] * y_ref[...], axis=1, keepdims=True))
            dq_acc[...] = jnp.zeros_like(dq_acc)
            dk_acc[slot] = jnp.zeros((WIN, ATT_HEAD_DIM), F32)
            dv_acc[slot] = jnp.zeros((WIN, ATT_HEAD_DIM), F32)
            own, before = _band_masks()

            def probs(q, k, v, dy, lse_col, dl_col, mask):
                p = jnp.exp(jnp.where(mask, _dot(q, k, _NT) * ATT_SCALE - lse_col, NEG_INF))
                ds = p * (_dot(dy, v, _NT) - dl_col)
                return p.astype(BF16), ds.astype(BF16)

            for d in DILATIONS:
                for idx in range(N_BLOCKS):
                    start = _block_start(idx, d)
                    rows = _rows(start, d)
                    q, dy = qf[rows, :].astype(BF16), dy_ref[rows, :].astype(BF16)
                    lse_col, dl_col = ls_c[rows, :][:, 0:1], dl_c[rows, :][:, 0:1]
                    kc, vc = kf[_rows(WIN + start, d), :].astype(BF16), vf[_rows(WIN + start, d), :].astype(BF16)
                    kp = kf[_rows(WIN + start - ATT_BLOCK * d, d), :].astype(BF16)
                    vp = vf[_rows(WIN + start - ATT_BLOCK * d, d), :].astype(BF16)
                    pc, dsc = probs(q, kc, vc, dy, lse_col, dl_col, own)
                    pp, dsp = probs(q, kp, vp, dy, lse_col, dl_col, before & ((idx >= d) | (w > 0)))
                    dq_acc[rows, :] += (_dot(dsc, kc) + _dot(dsp, kp)) * ATT_SCALE
                    dk_acc[slot, rows, :] += _dot(dsc, q, _TN) * ATT_SCALE
                    dv_acc[slot, rows, :] += _dot(pc, dy, _TN)
                    if idx >= d:
                        prows = _rows(start - ATT_BLOCK * d, d)
                        dk_acc[slot, prows, :] += _dot(dsp, q, _TN) * ATT_SCALE
                        dv_acc[slot, prows, :] += _dot(pp, dy, _TN)
                    else:
                        prows = _rows(WIN + start - ATT_BLOCK * d, d)
                        dk_acc[late, prows, :] += _dot(dsp, q, _TN) * ATT_SCALE
                        dv_acc[late, prows, :] += _dot(pp, dy, _TN)
            dq_ref[...] = dq_acc[...].astype(BF16)

        @pl.when(w > 0)
        def _():
            dkv_ref[0] = dk_acc[late].astype(BF16)
            dkv_ref[1] = dv_acc[late].astype(BF16)

    win_scratch = lambda *shape: pltpu.VMEM(shape + (ATT_HEAD_DIM,), F32)
    cur = lambda c: pl.BlockSpec(blk, lambda h, w: (this(w), c + h))
    before_spec = lambda c: pl.BlockSpec(blk, lambda h, w: (prev(w), c + h))
    return pl.pallas_call(
        body,
        grid=(ATT_HEADS, n_win + 1),
        in_specs=[cur(0), cur(ATT_HEADS), before_spec(ATT_HEADS), cur(2 * ATT_HEADS), before_spec(2 * ATT_HEADS),
                  cur(ATT_HEADS), cur(0), pl.BlockSpec((WIN, LANES), lambda h, w: (this(w), 0))] + [ANY] * n_dep,
        out_specs=[cur(0), pl.BlockSpec((2, WIN, ATT_HEAD_DIM), lambda h, w: (0, jnp.maximum(w - 1, 0), h))],
        out_shape=[jax.ShapeDtypeStruct((s_dim, D_ATT), BF16), jax.ShapeDtypeStruct((2, s_dim, D_ATT), BF16)],
        scratch_shapes=[win_scratch(WIN), win_scratch(2 * WIN), win_scratch(2 * WIN), win_scratch(WIN),
                        win_scratch(2, WIN), win_scratch(2, WIN), win_scratch(WIN), win_scratch(WIN)],
        compiler_params=_cparams("parallel", "arbitrary"),
        name="attn_fused_bwd",
    )(qkv, qkv, qkv, qkv, qkv, dymix, y_att, lse, *deps)


def _adamw(w, g, m, v, name):
    def fn(wb, gb, mb, vb):
        m2 = ADAM_B1 * mb + (1.0 - ADAM_B1) * gb
        v2 = ADAM_B2 * vb + (1.0 - ADAM_B2) * (gb * gb)
        m_hat = m2 / (1.0 - ADAM_B1 ** ADAM_STEP)
        v_hat = v2 / (1.0 - ADAM_B2 ** ADAM_STEP)
        delta = -ADAM_LR * (m_hat / (jnp.sqrt(v_hat) + ADAM_EPS) + ADAM_WD * wb)
        return delta, m2, v2
    cols = w.shape[1]
    tr = 128 if w.shape[0] % 128 == 0 else w.shape[0]
    return _rowcall(fn, [w, g, m, v], [], [(cols, F32)] * 3, [], name=name, tr=tr)


ANY = pl.BlockSpec(memory_space=pl.ANY)
SUM_ROWS = 256


def _position():
    x, y, c = lax.axis_index("x"), lax.axis_index("y"), lax.axis_index("c")
    chips = [(1 - x, y), (x, 1 - y), (1 - x, 1 - y)]
    return x, y, c, chips


def _remote(src, dst, send_sem, recv_sem, device):
    return pltpu.make_async_remote_copy(src_ref=src, dst_ref=dst, send_sem=send_sem, recv_sem=recv_sem,
                                        device_id=device, device_id_type=MESH)


def _handshake(peers):
    barrier = pltpu.get_barrier_semaphore()
    for p in peers:
        pl.semaphore_signal(barrier, inc=1, device_id=p, device_id_type=MESH)
    pl.semaphore_wait(barrier, len(peers))


def _gather_shards_async(shards, collective_id, name):
    n = len(shards)
    srcs = [jax.new_ref(s, memory_space=pltpu.MemorySpace.HBM) for s in shards]
    dsts = [jax.empty_ref(jax.ShapeDtypeStruct((N_CHIPS,) + s.shape, s.dtype), memory_space=pltpu.MemorySpace.HBM)
            for s in shards]

    @pl.kernel(mesh=plsc.ScalarSubcoreMesh(axis_name="seq", num_cores=1), name=name,
               scratch_types=(pltpu.SemaphoreType.DMA((6 * n,)), pltpu.SemaphoreType.DMA((6 * n,))),
               compiler_params=pltpu.CompilerParams(collective_id=collective_id))
    def launch(send_sems, recv_sems):
        x, y, c, chips = _position()
        sibling = (x, y, 1 - c)
        _handshake([(chip[0], chip[1], c) for chip in chips] + [sibling])

        def half(a, j, cc):
            h = shards[a].shape[0] // 2
            return dsts[a].at[j, pl.ds(cc * h, h), :]

        sent = []
        for a in range(n):
            h = shards[a].shape[0] // 2
            for j, chip in enumerate(chips):
                cp = _remote(srcs[a].at[pl.ds(c * h, h), :], half(a, 2 * x + y, c), send_sems.at[6 * a + j],
                             recv_sems.at[6 * a + j], (chip[0], chip[1], c))
                cp.start()
                sent.append(cp)
        for a in range(n):
            for j, chip in enumerate(chips):
                landed = half(a, 2 * chip[0] + chip[1], c)
                _remote(landed, landed, send_sems.at[6 * a + j], recv_sems.at[6 * a + j], (x, y, c)).wait_recv()
                cp = _remote(landed, landed, send_sems.at[6 * a + 3 + j], recv_sems.at[6 * a + 3 + j], sibling)
                cp.start()
                sent.append(cp)
        for a in range(n):
            for j, chip in enumerate(chips):
                handed = half(a, 2 * chip[0] + chip[1], 1 - c)
                _remote(handed, handed, send_sems.at[6 * a + 3 + j], recv_sems.at[6 * a + 3 + j], (x, y, c)).wait_recv()
        for cp in sent:
            cp.wait_send()

    launch()
    return [d[...] for d in dsts]


IN_COLS = {"z": (0, D_SSM), "xbc": (D_SSM, D_SSM + D_XBC), "dt": (D_SSM + D_XBC, D_SSM + D_XBC + SSM_HEADS),
           "qkv": (D_SSM + D_XBC + SSM_HEADS, D_IN_PROJ)}


def _cols_from_quarters(quarters, lo, hi):
    parts = []
    for q in range(N_CHIPS):
        a, b = max(lo, q * W_IN_SHARD), min(hi, (q + 1) * W_IN_SHARD)
        if a < b:
            parts.append(quarters[q][:, a - q * W_IN_SHARD:b - q * W_IN_SHARD])
    return parts[0] if len(parts) == 1 else jnp.concatenate(parts, axis=1)


def _quarters_from_cols(pieces):
    quarters = []
    for q in range(N_CHIPS):
        parts = []
        for name, (lo, hi) in IN_COLS.items():
            a, b = max(lo, q * W_IN_SHARD), min(hi, (q + 1) * W_IN_SHARD)
            if a < b:
                parts.append(pieces[name][:, a - lo:b - lo])
        quarters.append(jnp.concatenate(parts, axis=1))
    return jnp.stack(quarters)


def _by_chip(own, fetched):
    me = 2 * lax.axis_index("x") + lax.axis_index("y")
    return lax.dynamic_update_slice(fetched, own[None], (me, 0, 0))


def _add_sibling(grad, got, place, name, deps=()):
    nq, rows, cols = grad.shape
    h = rows // 2
    tr = SUM_ROWS
    nb = h // tr

    def body(place_ref, a_ref, b_ref, *rest):
        own_ref, ob_ref = rest[len(deps):]
        total = a_ref[...] + b_ref[...]
        ob_ref[...] = total.astype(BF16)

        @pl.when(pl.program_id(1) == place_ref[1])
        def _():
            own_ref[...] = total

    return pl.pallas_call(
        body,
        grid_spec=pltpu.PrefetchScalarGridSpec(
            num_scalar_prefetch=1, grid=(nb, nq),
            in_specs=[pl.BlockSpec((None, tr, cols), lambda i, q, p: (q, p[0] * nb + i, 0)),
                      pl.BlockSpec((None, tr, cols), lambda i, q, p: (q, i, 0))] + [ANY] * len(deps),
            out_specs=[pl.BlockSpec((tr, cols), lambda i, q, p: (i, 0)),
                       pl.BlockSpec((None, tr, cols), lambda i, q, p: (q, i, 0))]),
        out_shape=[jax.ShapeDtypeStruct((h, cols), F32), jax.ShapeDtypeStruct((nq, h, cols), BF16)],
        compiler_params=_cparams("parallel", "arbitrary"),
        name=name,
    )(place, grad, got, *deps)


def _add_chips(part, got, name, deps=()):
    h, cols = part.shape
    tr = SUM_ROWS

    def body(p_ref, g0_ref, g1_ref, g2_ref, *rest):
        o_ref = rest[len(deps)]
        o_ref[...] = ((p_ref[...] + g0_ref[...].astype(F32)) + g1_ref[...].astype(F32)) + g2_ref[...].astype(F32)

    got_spec = lambda j: pl.BlockSpec((None, tr, cols), lambda i: (j, i, 0))
    row_spec = pl.BlockSpec((tr, cols), lambda i: (i, 0))
    return pl.pallas_call(
        body,
        grid=(h // tr,),
        in_specs=[row_spec, got_spec(0), got_spec(1), got_spec(2)] + [ANY] * len(deps),
        out_specs=row_spec,
        out_shape=jax.ShapeDtypeStruct((h, cols), F32),
        compiler_params=_cparams("parallel"),
        name=name,
    )(part, got, got, got, *deps)


def _sequencer_exchange(src, out_shape, collective_id, name, plan, n_copies):
    src_ref = jax.new_ref(src, memory_space=pltpu.MemorySpace.HBM)
    dst_ref = jax.empty_ref(out_shape, memory_space=pltpu.MemorySpace.HBM)

    @pl.kernel(mesh=plsc.ScalarSubcoreMesh(axis_name="seq", num_cores=1), name=name,
               scratch_types=(pltpu.SemaphoreType.DMA((n_copies,)), pltpu.SemaphoreType.DMA((n_copies,))),
               compiler_params=pltpu.CompilerParams(collective_id=collective_id))
    def launch(send_sems, recv_sems):
        x, y, c, chips = _position()
        copies = plan(src_ref, dst_ref, x, y, c, chips)
        _handshake([peer for _, _, peer in copies])
        started = []
        for k, (s, d, peer) in enumerate(copies):
            cp = _remote(s, d, send_sems.at[k], recv_sems.at[k], peer)
            cp.start()
            started.append(cp)
        for cp in started:
            cp.wait()

    launch()
    return dst_ref[...]


class _AsyncReduceScatter:
    def __init__(self, grad, nm, first_id):
        self.grad, self.nm, self.first_id = grad, nm, first_id
        nq, rows, cols = grad.shape
        h = self.h = rows // 2

        def to_sibling(s, d, x, y, c, chips):
            return [(s.at[:, pl.ds((1 - c) * h, h), :], d, (x, y, 1 - c))]

        self.from_sibling = _sequencer_exchange(grad, jax.ShapeDtypeStruct((nq, h, cols), F32), first_id,
                                                f"rs_sibling_{nm}", to_sibling, 1)

    def sibling_sum(self, not_before=()):
        cols = self.grad.shape[2]
        place = jnp.stack([lax.axis_index("c"), 2 * lax.axis_index("x") + lax.axis_index("y")]).astype(jnp.int32)
        self.part, self.part_b = _add_sibling(self.grad, self.from_sibling, place, f"add_sibling_{self.nm}", not_before)

        def to_chips(s, d, x, y, c, chips):
            return [(s.at[2 * chip[0] + chip[1]], d.at[j], (chip[0], chip[1], c)) for j, chip in enumerate(chips)]

        self.from_chips = _sequencer_exchange(self.part_b, jax.ShapeDtypeStruct((3, self.h, cols), BF16),
                                              self.first_id + 1, f"rs_quarters_{self.nm}", to_chips, 3)
        return self.part_b

    def chip_sum(self, not_before=()):
        cols = self.grad.shape[2]
        self.half = _add_chips(self.part, self.from_chips, f"add_chips_{self.nm}", not_before)

        def whole_to_sibling(s, d, x, y, c, chips):
            return [(s, d, (x, y, 1 - c))]

        self.other = _sequencer_exchange(self.half, jax.ShapeDtypeStruct((self.h, cols), F32), self.first_id + 2,
                                         f"rs_share_{self.nm}", whole_to_sibling, 1)
        return self.half

    def share(self):
        return self.half, self.other


def _after(x, deps, name):
    def body(x_ref, *rest):
        rest[-1][...] = x_ref[...]

    vm = pl.BlockSpec(memory_space=pltpu.VMEM)
    return pl.pallas_call(body, in_specs=[vm] + [ANY] * len(deps), out_specs=vm,
                          out_shape=jax.ShapeDtypeStruct(x.shape, x.dtype), name=name)(x, *deps)


def _adamw_halves(w, mine, other, m, v, name):
    rows, cols = w.shape
    tr = SUM_ROWS
    nb = rows // 2 // tr
    c_arr = lax.axis_index("c").astype(jnp.int32).reshape(1)

    def body(c_ref, w_ref, a_ref, b_ref, m_ref, v_ref, g_out, d_out, m_out, v_out):
        is_mine = (pl.program_id(0) // nb) == c_ref[0]
        g = jnp.where(is_mine, a_ref[...], b_ref[...])
        wb, mb, vb = w_ref[...], m_ref[...], v_ref[...]
        m2 = ADAM_B1 * mb + (1.0 - ADAM_B1) * g
        v2 = ADAM_B2 * vb + (1.0 - ADAM_B2) * (g * g)
        m_hat = m2 / (1.0 - ADAM_B1 ** ADAM_STEP)
        v_hat = v2 / (1.0 - ADAM_B2 ** ADAM_STEP)
        g_out[...] = g
        d_out[...] = -ADAM_LR * (m_hat / (jnp.sqrt(v_hat) + ADAM_EPS) + ADAM_WD * wb)
        m_out[...] = m2
        v_out[...] = v2

    full = pl.BlockSpec((tr, cols), lambda i, c: (i, 0))
    mine_spec = pl.BlockSpec((tr, cols), lambda i, c: (jnp.where(i // nb == c[0], i % nb, 0), 0))
    other_spec = pl.BlockSpec((tr, cols), lambda i, c: (jnp.where(i // nb == c[0], 0, i % nb), 0))
    return pl.pallas_call(
        body,
        grid_spec=pltpu.PrefetchScalarGridSpec(
            num_scalar_prefetch=1, grid=(rows // tr,),
            in_specs=[full, mine_spec, other_spec, full, full], out_specs=[full] * 4),
        out_shape=[jax.ShapeDtypeStruct((rows, cols), F32)] * 4,
        compiler_params=_cparams("parallel"),
        name=name,
    )(c_arr, w, mine, other, m, v)


def _adamw_halves_t(w_t, mine_t, other_t, m_t, v_t, name):
    cols, rows = w_t.shape
    tr = cols // 11
    assert tr * 11 == cols and tr % 8 == 0
    c_arr = lax.axis_index("c").astype(jnp.int32).reshape(1)

    def body(c_ref, w_ref, a_ref, b_ref, m_ref, v_ref, g_out, d_out, m_out, v_out):
        first = c_ref[0] == 0
        a, b = a_ref[...], b_ref[...]
        g = jnp.concatenate([jnp.where(first, a, b), jnp.where(first, b, a)], axis=1)
        wb, mb, vb = w_ref[...], m_ref[...], v_ref[...]
        m2 = ADAM_B1 * mb + (1.0 - ADAM_B1) * g
        v2 = ADAM_B2 * vb + (1.0 - ADAM_B2) * (g * g)
        m_hat = m2 / (1.0 - ADAM_B1 ** ADAM_STEP)
        v_hat = v2 / (1.0 - ADAM_B2 ** ADAM_STEP)
        g_out[...] = g
        d_out[...] = -ADAM_LR * (m_hat / (jnp.sqrt(v_hat) + ADAM_EPS) + ADAM_WD * wb)
        m_out[...] = m2
        v_out[...] = v2

    full = pl.BlockSpec((tr, rows), lambda i, c: (i, 0))
    half = pl.BlockSpec((tr, rows // 2), lambda i, c: (i, 0))
    return pl.pallas_call(
        body,
        grid_spec=pltpu.PrefetchScalarGridSpec(
            num_scalar_prefetch=1, grid=(cols // tr,),
            in_specs=[full, half, half, full, full], out_specs=[full] * 4),
        out_shape=[jax.ShapeDtypeStruct((cols, rows), F32)] * 4,
        compiler_params=_cparams("parallel"),
        name=name,
    )(c_arr, w_t, mine_t, other_t, m_t, v_t)


def _all_sum_small(v):
    n_dev = 8

    def body(v_ref, o_ref, gath, send_sems, recv_sems):
        x, y, c, _ = _position()
        me = 4 * x + 2 * y + c
        gath[me] = v_ref[...]
        copies = []
        for k in range(1, n_dev):
            peer = tuple(1 - p if (k >> s) & 1 else p for p, s in ((x, 2), (y, 1), (c, 0)))
            cp = _remote(v_ref, gath.at[me], send_sems.at[k - 1], recv_sems.at[k - 1], peer)
            cp.start()
            copies.append(cp)
        for cp in copies:
            cp.wait()
        acc = gath[0]
        for i in range(1, n_dev):
            acc = acc + gath[i]
        o_ref[...] = acc

    vm = pl.BlockSpec(memory_space=pltpu.VMEM)
    return pl.pallas_call(
        body,
        in_specs=[vm],
        out_specs=vm,
        out_shape=jax.ShapeDtypeStruct(v.shape, F32),
        scratch_shapes=[pltpu.VMEM((n_dev,) + v.shape, F32), pltpu.SemaphoreType.DMA((n_dev - 1,)),
                        pltpu.SemaphoreType.DMA((n_dev - 1,))],
        name="all_sum_small",
    )(v)


def _pack_rows(vectors):
    rows = []
    for v in vectors:
        flat = v.reshape(-1).astype(F32)
        rows.append(jnp.pad(flat, (0, (-flat.shape[0]) % LANES)).reshape(-1, LANES))
    out = jnp.concatenate(rows, axis=0)
    return jnp.pad(out, ((0, (-out.shape[0]) % 8), (0, 0)))


def _unpack_rows(packed, shapes):
    outs, r = [], 0
    for shp in shapes:
        size = math.prod(shp)
        nr = -(-size // LANES)
        outs.append(packed[r:r + nr].reshape(-1)[:size].reshape(shp))
        r += nr
    return outs


def _relu_sq(acc):
    r = jnp.maximum(acc, 0.0)
    return r, r * r


def _relu_sq_bwd(acc, r):
    return (acc * (2.0 * r.astype(F32)),)


def kernel(x, norm_mix_pre, w_in, conv_w, conv_b, dt_bias, a_log, d_skip, ssm_norm_w, w_out, norm_mix_post, norm_mlp_pre, w_up, w_down, norm_mlp_post, loss_target, m_norm_mix_pre, m_w_in, m_conv_w, m_conv_b, m_dt_bias, m_a_log, m_d_skip, m_ssm_norm_w, m_w_out, m_norm_mix_post, m_norm_mlp_pre, m_w_up, m_w_down, m_norm_mlp_post, v_norm_mix_pre, v_w_in, v_conv_w, v_conv_b, v_dt_bias, v_a_log, v_d_skip, v_ssm_norm_w, v_w_out, v_norm_mix_post, v_norm_mlp_pre, v_w_up, v_w_down, v_norm_mlp_post):
    s_dim = x.shape[1]
    xs, target = x[0], loss_target[0]
    chip = 2 * lax.axis_index("x") + lax.axis_index("y")

    own = [w_in[0].astype(BF16), w_out[0].astype(BF16), w_up[0].astype(BF16), w_down[0].astype(BF16)]
    fetched_in = _gather_shards_async(own[:1], 14, "gather_w_in")[0]
    conv_cols = D_XBC // N_CHIPS
    conv_placed = lax.dynamic_update_slice(jnp.zeros((8, D_XBC), F32), 0.5 * conv_w[0], (0, chip * conv_cols))
    conv_full = _all_sum_small(conv_placed.reshape(-1, LANES)).reshape(8, D_XBC)
    w8 = _perm_cols(conv_full.at[CONV_WIDTH].set(conv_b[0]))
    u = _pre_norm(xs, norm_mix_pre)
    fetched_in, u, w8, *rest = lax.optimization_barrier((fetched_in, u, w8, *own[1:]))
    fetched = [fetched_in] + _gather_shards_async(rest, 1, "gather_rest")
    g_in, g_out, g_up, g_down = [_by_chip(o, f) for o, f in zip(own, fetched)]
    w_z = _cols_from_quarters(g_in, *IN_COLS["z"])
    w_xbc = _perm_cols(_cols_from_quarters(g_in, *IN_COLS["xbc"]))
    w_dt = jnp.pad(_cols_from_quarters(g_in, *IN_COLS["dt"]), ((0, 0), (0, LANES - SSM_HEADS)))
    w_qkv = _cols_from_quarters(g_in, *IN_COLS["qkv"])
    w_out_full = g_out.reshape(D_MIX, D_MODEL)
    w_down_full = g_down.reshape(D_FF, D_MODEL)

    z = _matmul([(u, w_z, TK)], "nn", [F32], name="proj_z")
    xbc = _matmul([(u, w_xbc, TK)], "nn", [F32], name="proj_xbc")
    dt_raw = _matmul([(u, w_dt, TK)], "nn", [F32], name="proj_dt")
    qkv = _matmul([(u, w_qkv, TK)], "nn", [BF16], name="proj_qkv")
    xc = _conv_fwd(xbc, w8)
    dtg = _dt_to_groups(dt_raw)
    par = _pack_ssd_params(dt_bias[0], a_log[0], d_skip[0])
    y, y_ssm, states = _ssd_fwd(xc, z, dtg, par, ssm_norm_w)
    y_att, y_att_f32, lse = _attn_fused_fwd(qkv)
    y_mix = jnp.concatenate([y_ssm, y_att], axis=1)
    mix = _matmul([(y_mix, w_out_full, TK)], "nn", [F32], name="out_proj")
    h1, u2 = _post_pre_norm(xs, mix, norm_mix_post, norm_mlp_pre)
    hid, act = _matmul([(u2, g_up, TK)], "nn", [BF16, BF16], name="mlp_up", epilogue=_relu_sq)
    ff = _matmul([(act, w_down_full, TK)], "nn", [F32], name="mlp_down")
    dh2, dff, d_g4, loss_part = _tail(ff, h1, target, norm_mlp_post)

    dhid = _matmul([(dff, w_down_full, TK)], "nt", [BF16], name="mlp_down_dx", epilogue=_relu_sq_bwd, extras=[hid])
    weights = {"norm_mix_pre": (norm_mix_pre, m_norm_mix_pre, v_norm_mix_pre), "w_in": (w_in, m_w_in, v_w_in),
               "conv_w": (conv_w, m_conv_w, v_conv_w), "conv_b": (conv_b, m_conv_b, v_conv_b),
               "dt_bias": (dt_bias, m_dt_bias, v_dt_bias), "a_log": (a_log, m_a_log, v_a_log),
               "d_skip": (d_skip, m_d_skip, v_d_skip), "ssm_norm_w": (ssm_norm_w, m_ssm_norm_w, v_ssm_norm_w),
               "w_out": (w_out, m_w_out, v_w_out), "norm_mix_post": (norm_mix_post, m_norm_mix_post, v_norm_mix_post),
               "norm_mlp_pre": (norm_mlp_pre, m_norm_mlp_pre, v_norm_mlp_pre), "w_up": (w_up, m_w_up, v_w_up),
               "w_down": (w_down, m_w_down, v_w_down),
               "norm_mlp_post": (norm_mlp_post, m_norm_mlp_post, v_norm_mlp_post)}
    grads, delta, new_m, new_v = {}, {}, {}, {}

    def adamw_big(n, halves):
        w, m, v = weights[n]
        g_, d_, m_, v_ = _adamw_halves(w[0], halves[0], halves[1], m[0], v[0], f"adamw_{n}")
        grads[n], delta[n], new_m[n], new_v[n] = g_[None], d_[None], m_[None], v_[None]

    dw_down = _matmul([(act, dff, TK)], "tn", [F32], name="mlp_down_dw")
    rs_down = _AsyncReduceScatter(dw_down.reshape(N_CHIPS, D_FF // N_CHIPS, D_MODEL), "w_down", 11)
    dw_up = _matmul([(u2, dhid, TK)], "tn", [F32], name="mlp_up_dw", deps=[dw_down], out_quarters=True)
    rs_up = _AsyncReduceScatter(dw_up, "w_up", 8)
    du2 = _matmul([(dhid, g_up, TK)], "nt", [F32], name="mlp_up_dx",
                  deps=[rs_down.sibling_sum(not_before=[dw_up])])
    dh1, dmix, d_g3, d_g2 = _mid_bwd(du2, h1, dh2, mix, norm_mix_post, norm_mlp_pre,
                                     deps=[rs_up.sibling_sum(not_before=[du2])])
    dymix = _matmul([(dmix, w_out_full, TK)], "nt", [F32], name="out_proj_dx")
    dw_out = _matmul([(y_mix, dmix, TK)], "tn", [F32], name="out_proj_dw")
    rs_out = _AsyncReduceScatter(dw_out.reshape(N_CHIPS, D_MIX // N_CHIPS, D_MODEL), "w_out", 5)
    dq, dkv = _attn_fused_bwd(qkv, dymix, y_att_f32, lse)
    dqkv = jnp.concatenate([dq[None], dkv], axis=0)
    par_late = _after(par, [rs_down.chip_sum(not_before=[dqkv]), rs_out.sibling_sum(not_before=[dymix])],
                      "after_w_down")
    dxc, dz, ddtg, dpar, d_nw = _ssd_bwd(xc, z, dtg, par_late, ssm_norm_w, y, states, dymix)
    g_down = rs_down.share()
    dxbc, dw8 = _conv_bwd(xbc, _after(w8, [*g_down, rs_up.chip_sum(not_before=[dxc])], "after_w_up"), dxc)
    ddt = jnp.pad(_dt_from_groups(ddtg), ((0, 0), (0, LANES - SSM_HEADS))).astype(BF16)
    g_up = rs_up.share()
    dw_z = _matmul([(u, dz, TK)], "tn", [F32], name="proj_z_dw")
    dw_xbc = _matmul([(u, dxbc, TK)], "tn", [F32], name="proj_xbc_dw",
                     deps=[*g_up, rs_out.chip_sum(not_before=[dxbc])])
    g_out = rs_out.share()
    dw_dt = _matmul([(u, ddt, TK)], "tn", [F32], name="proj_dt_dw")
    dw_qkv = _matmul([(u, dqkv, TK)], "tn", [F32], name="proj_qkv_dw")
    dw_in = _quarters_from_cols({"z": dw_z, "xbc": _unperm_cols(dw_xbc), "dt": dw_dt[:, :SSM_HEADS], "qkv": dw_qkv})
    rs_in = _AsyncReduceScatter(dw_in, "w_in", 2)
    adamw_big("w_down", g_down)
    adamw_big("w_up", g_up)
    rs_in.sibling_sum(not_before=[delta["w_up"]])
    du = _matmul([(dz, w_z, TK_MULTI), (dxbc, w_xbc, TK_MULTI), (dqkv, w_qkv, TK_MULTI), (ddt, w_dt, LANES)], "nt",
                 [F32], name="proj_dx", deps=[*g_out, rs_in.part_b])
    grad_x, d_g1 = _first_bwd(du, xs, dh1, norm_mix_pre)
    adamw_big("w_out", g_out)
    rs_in.chip_sum(not_before=[grad_x, delta["w_out"]])

    dconv = _unperm_cols(dw8)
    d_bias, d_alog, d_dskip = _unpack_ssd_params(dpar)
    small_shapes = [(1, D_MODEL), (CONV_WIDTH, D_XBC), (1, D_XBC), (1, SSM_HEADS), (1, SSM_HEADS), (1, SSM_HEADS),
                    (1, D_SSM), (1, D_MODEL), (1, D_MODEL), (1, D_MODEL), (1, LANES)]
    summed = _unpack_rows(
        _all_sum_small(_pack_rows([d_g1, dconv[:CONV_WIDTH], dconv[CONV_WIDTH:CONV_WIDTH + 1], d_bias, d_alog,
                                   d_dskip, d_nw, d_g2, d_g3, d_g4, loss_part])), small_shapes)
    (g_g1, g_conv_full, g_conv_b, g_bias, g_alog, g_dskip, g_nw, g_g2, g_g3, g_g4, loss_row) = summed
    loss = loss_row[0, 0]
    g_conv_w = lax.dynamic_slice(g_conv_full, (0, chip * conv_cols), (CONV_WIDTH, conv_cols))[None]

    grads.update({"norm_mix_pre": g_g1, "conv_w": g_conv_w, "conv_b": g_conv_b, "dt_bias": g_bias,
                  "a_log": g_alog, "d_skip": g_dskip, "ssm_norm_w": g_nw, "norm_mix_post": g_g2,
                  "norm_mlp_pre": g_g3, "norm_mlp_post": g_g4})
    order = list(weights)
    small_names = [n for n in order if n not in ("w_in", "w_out", "w_up", "w_down")]
    small_w_shapes = [weights[n][0].shape for n in small_names]
    packed = [_pack_rows([weights[n][k] for n in small_names]) for k in range(3)]
    packed_g = _pack_rows([grads[n].reshape(weights[n][0].shape) for n in small_names])
    sd, sm, sv = _adamw(packed[0], packed_g, packed[1], packed[2], "adamw_small")
    for k, n in enumerate(small_names):
        grads[n] = grads[n].reshape(weights[n][0].shape)
    for res, pk in ((delta, sd), (new_m, sm), (new_v, sv)):
        for n, val in zip(small_names, _unpack_rows(pk, small_w_shapes)):
            res[n] = val
    mine, other = rs_in.share()
    w_t, m_t, v_t = [jnp.swapaxes(a[0], 0, 1) for a in weights["w_in"]]
    results_t = _adamw_halves_t(w_t, mine.T, other.T, m_t, v_t, "adamw_w_in")
    grads["w_in"], delta["w_in"], new_m["w_in"], new_v["w_in"] = [jnp.swapaxes(r, 0, 1)[None] for r in results_t]

    return (loss, grad_x[None], *[grads[n] for n in order], *[delta[n] for n in order],
            *[new_m[n] for n in order], *[new_v[n] for n in order])
```

```python
import math

import numpy as np
import jax
import jax.numpy as jnp
from jax import lax
from jax.experimental import pallas as pl
from jax.experimental.pallas import tpu as pltpu
from jax.experimental.pallas import tpu_sc as plsc

F32 = jnp.float32
BF16 = jnp.bfloat16

D_MODEL = 2048
SSM_HEAD_DIM = 64
SSM_GROUPS = 8
HEADS_PER_GROUP = 4
SSM_HEADS = SSM_GROUPS * HEADS_PER_GROUP
D_SSM = SSM_HEADS * SSM_HEAD_DIM
D_STATE = 128
CONV_WIDTH = 4
SSD_CHUNK = 128
D_XBC = D_SSM + 2 * SSM_GROUPS * D_STATE
GROUP_X = HEADS_PER_GROUP * SSM_HEAD_DIM
GROUP_COLS = GROUP_X + 2 * D_STATE
ATT_HEAD_DIM = 128
ATT_HEADS = 16
D_ATT = ATT_HEADS * ATT_HEAD_DIM
DILATIONS = (1, 4, 16)
ATT_BLOCK = 128
D_MIX = D_SSM + D_ATT
D_IN_PROJ = D_SSM + D_XBC + SSM_HEADS + 3 * D_ATT
D_FF = 4 * D_MODEL
EPS = 1e-6
N_CHIPS = 4
W_IN_SHARD = D_IN_PROJ // N_CHIPS

ADAM_LR = 0.001
ADAM_B1 = 0.9
ADAM_B2 = 0.999
ADAM_EPS = 1e-08
ADAM_WD = 0.01
ADAM_STEP = 10

LANES = 128
VMEM_LIMIT = 48 * 1024 * 1024
MESH = pl.DeviceIdType.MESH

_NN = (((1,), (0,)), ((), ()))
_NT = (((1,), (1,)), ((), ()))
_TN = (((0,), (0,)), ((), ()))


def _dot(a, b, dims=_NN):
    return lax.dot_general(a, b, dims, preferred_element_type=F32)


def _cparams(*sem):
    return pltpu.CompilerParams(dimension_semantics=sem, vmem_limit_bytes=VMEM_LIMIT)


TK = 2048
TK_MULTI = 1024
TK_TN = 4096


def _matmul(pairs, mode, out_dtypes, *, name, tm=1024, tn=1024, epilogue=None, extras=(), deps=(), out_quarters=False):
    a0, b0, _ = pairs[0]
    m_dim = a0.shape[-1] if mode == "tn" else a0.shape[-2]
    if b0.ndim == 3:
        n_dim = b0.shape[1] if mode == "nt" else b0.shape[0] * b0.shape[2]
    else:
        n_dim = b0.shape[0] if mode == "nt" else b0.shape[1]
    tm, tn = min(tm, m_dim), min(tn, n_dim)
    nks, offs = [], []
    for a, _, tk in pairs:
        k_part = a.shape[0] if mode == "tn" else a.shape[-1]
        k_dim = k_part * (a.shape[0] if a.ndim == 3 else 1)
        assert k_part % tk == 0, (name, k_part, tk)
        offs.append(sum(nks))
        nks.append(k_dim // tk)
    nk_total = sum(nks)
    assert m_dim % tm == 0 and n_dim % tn == 0, (name, m_dim, n_dim)
    dims = {"nn": _NN, "nt": _NT, "tn": _TN}[mode]
    n_pairs, n_extra, n_out = len(pairs), len(extras), len(out_dtypes)

    in_specs, operands = [], []
    for (a, b, tk), off, nk in zip(pairs, offs, nks):
        def kidx(k, off=off, nk=nk):
            return k if n_pairs == 1 else jnp.clip(k - off, 0, nk - 1)
        if mode == "tn":
            assert a.ndim == 2
            in_specs.append(pl.BlockSpec((tk, tm), lambda m, n, k, f=kidx: (f(k), m)))
        elif a.ndim == 3:
            per = a.shape[2] // tk
            in_specs.append(pl.BlockSpec((None, tm, tk), lambda m, n, k, f=kidx, per=per: (f(k) // per, m, f(k) % per)))
        else:
            in_specs.append(pl.BlockSpec((tm, tk), lambda m, n, k, f=kidx: (m, f(k))))
        if b.ndim == 3 and mode == "nt":
            per = b.shape[2] // tk
            in_specs.append(pl.BlockSpec((None, tn, tk), lambda m, n, k, f=kidx, per=per: (f(k) // per, n, f(k) % per)))
        elif b.ndim == 3:
            per = b.shape[2] // tn
            in_specs.append(pl.BlockSpec((None, tk, tn), lambda m, n, k, f=kidx, per=per: (n // per, f(k), n % per)))
        elif mode == "nt":
            in_specs.append(pl.BlockSpec((tn, tk), lambda m, n, k, f=kidx: (n, f(k))))
        else:
            in_specs.append(pl.BlockSpec((tk, tn), lambda m, n, k, f=kidx: (f(k), n)))
        operands += [a, b]
    for e in extras:
        in_specs.append(pl.BlockSpec((tm, tn), lambda m, n, k: (m, n)))
        operands.append(e)
    in_specs += [pl.BlockSpec(memory_space=pl.ANY)] * len(deps)
    operands += list(deps)
    first_out = 2 * n_pairs + n_extra + len(deps)
    if out_quarters:
        out_per_q = n_dim // N_CHIPS // tn
        out_dims = (N_CHIPS, m_dim, n_dim // N_CHIPS)
        out_spec = pl.BlockSpec((None, tm, tn), lambda m, n, k: (n // out_per_q, m, n % out_per_q))
    else:
        out_dims = (m_dim, n_dim)
        out_spec = pl.BlockSpec((tm, tn), lambda m, n, k: (m, n))

    def body(*refs):
        ab = refs[:2 * n_pairs]
        e_refs = refs[2 * n_pairs:2 * n_pairs + n_extra]
        o_refs = refs[first_out:first_out + n_out]

        def finish(total):
            vals = (total,) if epilogue is None else epilogue(total, *[e[...] for e in e_refs])
            for o_ref, v in zip(o_refs, vals):
                o_ref[...] = v.astype(o_ref.dtype)

        if nk_total == 1:
            finish(_dot(ab[0][...], ab[1][...], dims))
            return
        acc = refs[-1]
        k = pl.program_id(2)

        @pl.when(k == 0)
        def _():
            acc[...] = jnp.zeros_like(acc)

        for i in range(n_pairs):
            def accumulate(i=i):
                acc[...] += _dot(ab[2 * i][...], ab[2 * i + 1][...], dims)
            if n_pairs == 1:
                accumulate()
            else:
                pl.when((k >= offs[i]) & (k < offs[i] + nks[i]))(accumulate)

        @pl.when(k == nk_total - 1)
        def _():
            finish(acc[...])

    outs = pl.pallas_call(
        body,
        grid=(m_dim // tm, n_dim // tn, nk_total),
        in_specs=in_specs,
        out_specs=[out_spec for _ in out_dtypes],
        out_shape=[jax.ShapeDtypeStruct(out_dims, dt) for dt in out_dtypes],
        scratch_shapes=[pltpu.VMEM((tm, tn), F32)] if nk_total > 1 else [],
        compiler_params=_cparams("parallel", "parallel", "arbitrary"),
        name=name,
    )(*operands)
    return outs[0] if n_out == 1 else outs


def _rowcall(fn, rows, vecs, row_outs, acc_widths, *, name, tr=256, row_cols=None, deps=()):
    s_dim = rows[0].shape[0]
    assert s_dim % tr == 0
    row_cols = row_cols or [None] * len(rows)
    n_r, n_v, n_ro, n_acc = len(rows), len(vecs), len(row_outs), len(acc_widths)
    in_specs = []
    for r, rc in zip(rows, row_cols):
        if rc is None:
            in_specs.append(pl.BlockSpec((tr, r.shape[1]), lambda i: (i, 0)))
        else:
            in_specs.append(pl.BlockSpec((tr, rc[0]), lambda i, c=rc[1]: (i, c)))
    for v in vecs:
        in_specs.append(pl.BlockSpec(v.shape, lambda i, nd=v.ndim: (0,) * nd))
    in_specs += [pl.BlockSpec(memory_space=pl.ANY)] * len(deps)
    n_d = len(deps)

    def body(*refs):
        ins = [r[...] for r in refs[:n_r + n_v]]
        ro = refs[n_r + n_v + n_d:n_r + n_v + n_d + n_ro]
        ao = refs[n_r + n_v + n_d + n_ro:]
        outs = fn(*ins)
        for ref, v in zip(ro, outs[:n_ro]):
            ref[...] = v.astype(ref.dtype)
        if n_acc:
            @pl.when(pl.program_id(0) == 0)
            def _():
                for ref in ao:
                    ref[...] = jnp.zeros_like(ref)
            for ref, v in zip(ao, outs[n_ro:]):
                ref[...] += v

    outs = pl.pallas_call(
        body,
        grid=(s_dim // tr,),
        in_specs=in_specs,
        out_specs=[pl.BlockSpec((tr, w), lambda i: (i, 0)) for w, _ in row_outs]
        + [pl.BlockSpec((1, w), lambda i: (0, 0)) for w in acc_widths],
        out_shape=[jax.ShapeDtypeStruct((s_dim, w), dt) for w, dt in row_outs]
        + [jax.ShapeDtypeStruct((1, w), F32) for w in acc_widths],
        compiler_params=_cparams("arbitrary"),
        name=name,
    )(*rows, *vecs, *deps)
    return outs


def _nrm(x, g):
    r = lax.rsqrt(jnp.mean(x * x, axis=-1, keepdims=True) + EPS)
    n = x * r
    return n * g, n, r


def _nrm_bwd(dy, n, r, g):
    dn = dy * g
    dx = r * (dn - n * jnp.mean(dn * n, axis=-1, keepdims=True))
    return dx, jnp.sum(dy * n, axis=0, keepdims=True)


def _sigmoid(x):
    return 1.0 / (1.0 + jnp.exp(-x))


def _softplus(x):
    return jnp.maximum(x, 0.0) + jnp.log(1.0 + jnp.exp(-jnp.abs(x)))


def _pre_norm(x, g1):
    def fn(xb, g):
        return (_nrm(xb, g)[0],)
    return _rowcall(fn, [x], [g1], [(D_MODEL, BF16)], [], name="pre_norm")[0]


def _post_pre_norm(x, mix, g2, g3):
    def fn(xb, mb, g2b, g3b):
        h1 = xb + _nrm(mb, g2b)[0]
        return h1, _nrm(h1, g3b)[0]
    return _rowcall(fn, [x, mix], [g2, g3], [(D_MODEL, F32), (D_MODEL, BF16)], [], name="post_pre_norm")


def _tail(ff, h1, target, g4):
    def fn(ffb, h1b, tb, g):
        y, n, r = _nrm(ffb, g)
        e = h1b + y - tb
        loss = 0.5 * jnp.sum(jnp.sum(e * e, axis=-1, keepdims=True) * (1.0 / D_MODEL), axis=0, keepdims=True)
        dh2 = e * (1.0 / D_MODEL)
        dff, dg = _nrm_bwd(dh2, n, r, g)
        return dh2, dff, dg, jnp.broadcast_to(loss, (1, LANES))
    return _rowcall(fn, [ff, h1, target], [g4], [(D_MODEL, F32), (D_MODEL, BF16)], [D_MODEL, LANES], name="tail")


def _mid_bwd(du2, h1, dh2, mix, g2, g3, deps=()):
    def fn(du2b, h1b, dh2b, mb, g2b, g3b):
        _, n3, r3 = _nrm(h1b, g3b)
        d3, dg3 = _nrm_bwd(du2b, n3, r3, g3b)
        dh1 = dh2b + d3
        _, n2, r2 = _nrm(mb, g2b)
        dmix, dg2 = _nrm_bwd(dh1, n2, r2, g2b)
        return dh1, dmix, dg3, dg2
    return _rowcall(fn, [du2, h1, dh2, mix], [g2, g3], [(D_MODEL, F32), (D_MODEL, BF16)], [D_MODEL, D_MODEL],
                    name="mid_bwd", deps=deps)


def _first_bwd(du, x, dh1, g1):
    def fn(dub, xb, dh1b, g):
        _, n, r = _nrm(xb, g)
        dx, dg = _nrm_bwd(dub, n, r, g)
        return dh1b + dx, dg
    return _rowcall(fn, [du, x, dh1], [g1], [(D_MODEL, F32)], [D_MODEL], name="first_bwd")


CONV_TILE = 256
CONV_ROWS = 256
PAD = 8


def _conv_taps(w):
    return [w[k:k + 1, :] for k in range(CONV_WIDTH)], w[CONV_WIDTH:CONV_WIDTH + 1, :]


def _conv_fwd(xbc, w8):
    s_dim, c_dim = xbc.shape
    n_steps = s_dim // CONV_ROWS

    def body(x_ref, w_ref, o_ref, xp):
        xp[0:PAD, :] = jnp.zeros((PAD, CONV_TILE), F32)
        xp[PAD:PAD + s_dim, :] = x_ref[...]
        taps, bias = _conv_taps(w_ref[...])

        def step(c, carry):
            base = pl.multiple_of(c * CONV_ROWS, CONV_ROWS)
            win = xp[pl.ds(base, CONV_ROWS + PAD), :]
            pre = bias + taps[3] * win[PAD:, :]
            for j in range(1, CONV_WIDTH):
                pre = pre + taps[3 - j] * pltpu.roll(win, j, axis=0)[PAD:, :]
            o_ref[pl.ds(base, CONV_ROWS), :] = pre * _sigmoid(pre)
            return carry

        lax.fori_loop(0, n_steps, step, 0, unroll=2)

    return pl.pallas_call(
        body,
        grid=(c_dim // CONV_TILE,),
        in_specs=[pl.BlockSpec((s_dim, CONV_TILE), lambda j: (0, j)), pl.BlockSpec((8, CONV_TILE), lambda j: (0, j))],
        out_specs=pl.BlockSpec((s_dim, CONV_TILE), lambda j: (0, j)),
        out_shape=jax.ShapeDtypeStruct((s_dim, c_dim), F32),
        scratch_shapes=[pltpu.VMEM((s_dim + 2 * PAD, CONV_TILE), F32)],
        compiler_params=_cparams("parallel"),
        name="conv_fwd",
    )(xbc, w8)


def _conv_bwd(xbc, w8, dxc):
    s_dim, c_dim = xbc.shape
    n_steps = s_dim // CONV_ROWS

    def body(x_ref, w_ref, d_ref, dx_ref, dw_ref, xp, dp):
        xp[0:PAD, :] = jnp.zeros((PAD, CONV_TILE), F32)
        xp[PAD:PAD + s_dim, :] = x_ref[...]
        dp[PAD + s_dim:, :] = jnp.zeros((PAD, CONV_TILE), F32)
        taps, bias = _conv_taps(w_ref[...])

        def step1(c, sums):
            base = pl.multiple_of(c * CONV_ROWS, CONV_ROWS)
            win = xp[pl.ds(base, CONV_ROWS + PAD), :]
            shifted = [win[PAD:, :]] + [pltpu.roll(win, j, axis=0)[PAD:, :] for j in range(1, CONV_WIDTH)]
            pre = bias
            for j in range(CONV_WIDTH):
                pre = pre + taps[3 - j] * shifted[j]
            sg = _sigmoid(pre)
            dpre = d_ref[pl.ds(base, CONV_ROWS), :] * (sg * (1.0 + pre * (1.0 - sg)))
            dp[pl.ds(base + PAD, CONV_ROWS), :] = dpre
            new = [sums[k] + jnp.sum(dpre * shifted[3 - k], axis=0, keepdims=True) for k in range(CONV_WIDTH)]
            new.append(sums[CONV_WIDTH] + jnp.sum(dpre, axis=0, keepdims=True))
            return tuple(new)

        zero = jnp.zeros((1, CONV_TILE), F32)
        sums = lax.fori_loop(0, n_steps, step1, (zero,) * (CONV_WIDTH + 1), unroll=2)
        dw_ref[...] = jnp.zeros((8, CONV_TILE), F32)
        for k in range(CONV_WIDTH + 1):
            dw_ref[k:k + 1, :] = sums[k]

        def step2(c, carry):
            base = pl.multiple_of(c * CONV_ROWS, CONV_ROWS)
            win = dp[pl.ds(base + PAD, CONV_ROWS + PAD), :]
            dx = taps[3] * win[:CONV_ROWS, :]
            for j in range(1, CONV_WIDTH):
                dx = dx + taps[3 - j] * pltpu.roll(win, CONV_ROWS + PAD - j, axis=0)[:CONV_ROWS, :]
            dx_ref[pl.ds(base, CONV_ROWS), :] = dx.astype(BF16)
            return carry

        lax.fori_loop(0, n_steps, step2, 0, unroll=2)

    col = lambda j: (0, j)
    return pl.pallas_call(
        body,
        grid=(c_dim // CONV_TILE,),
        in_specs=[pl.BlockSpec((s_dim, CONV_TILE), col), pl.BlockSpec((8, CONV_TILE), col),
                  pl.BlockSpec((s_dim, CONV_TILE), col)],
        out_specs=[pl.BlockSpec((s_dim, CONV_TILE), col), pl.BlockSpec((8, CONV_TILE), col)],
        out_shape=[jax.ShapeDtypeStruct((s_dim, c_dim), BF16), jax.ShapeDtypeStruct((8, c_dim), F32)],
        scratch_shapes=[pltpu.VMEM((s_dim + 2 * PAD, CONV_TILE), F32), pltpu.VMEM((s_dim + 2 * PAD, CONV_TILE), F32)],
        compiler_params=_cparams("parallel"),
        name="conv_bwd",
    )(xbc, w8, dxc)


def _perm_cols(a):
    parts = []
    for g in range(SSM_GROUPS):
        parts += [a[..., g * GROUP_X:(g + 1) * GROUP_X],
                  a[..., D_SSM + g * D_STATE:D_SSM + (g + 1) * D_STATE],
                  a[..., D_SSM + SSM_GROUPS * D_STATE + g * D_STATE:D_SSM + SSM_GROUPS * D_STATE + (g + 1) * D_STATE]]
    return jnp.concatenate(parts, axis=-1)


def _unperm_cols(a):
    xs = [a[..., g * GROUP_COLS:g * GROUP_COLS + GROUP_X] for g in range(SSM_GROUPS)]
    bs = [a[..., g * GROUP_COLS + GROUP_X:g * GROUP_COLS + GROUP_X + D_STATE] for g in range(SSM_GROUPS)]
    cs = [a[..., g * GROUP_COLS + GROUP_X + D_STATE:(g + 1) * GROUP_COLS] for g in range(SSM_GROUPS)]
    return jnp.concatenate(xs + bs + cs, axis=-1)


def _dt_to_groups(dt):
    s_dim = dt.shape[0]
    t = dt[:, :SSM_HEADS].reshape(s_dim, SSM_GROUPS, HEADS_PER_GROUP).transpose(1, 0, 2)
    return jnp.pad(t, ((0, 0), (0, 0), (0, LANES - HEADS_PER_GROUP)))


def _dt_from_groups(dtg):
    s_dim = dtg.shape[1]
    return dtg[:, :, :HEADS_PER_GROUP].transpose(1, 0, 2).reshape(s_dim, SSM_HEADS)


def _pack_ssd_params(dt_bias, a_log, d_skip):
    rows = jnp.stack([p.reshape(SSM_GROUPS, HEADS_PER_GROUP) for p in (dt_bias, a_log, d_skip)], axis=1)
    return jnp.pad(rows, ((0, 0), (0, 8 - 3), (0, LANES - HEADS_PER_GROUP)))


def _unpack_ssd_params(par):
    return tuple(par[:, k, :HEADS_PER_GROUP].reshape(SSM_HEADS) for k in range(3))


Q = SSD_CHUNK


def _split3(v):
    hi = v.astype(BF16)
    r1 = v - hi.astype(F32)
    mid = r1.astype(BF16)
    lo = (r1 - mid.astype(F32)).astype(BF16)
    return hi, mid, lo


def _dot_l01(t01, v):
    return sum(_dot(t01, p) for p in _split3(v))


def _dot_r01(v, e01):
    return sum(_dot(p, e01) for p in _split3(v))


def _ssd_consts():
    row = lax.broadcasted_iota(jnp.int32, (Q, Q), 0)
    col = lax.broadcasted_iota(jnp.int32, (Q, Q), 1)
    causal = row >= col
    tril = causal.astype(BF16)
    triu = (col >= row).astype(BF16)
    er = lax.broadcasted_iota(jnp.int32, (LANES, GROUP_X), 0)
    ec = lax.broadcasted_iota(jnp.int32, (LANES, GROUP_X), 1) // SSM_HEAD_DIM
    expand = (er == ec).astype(BF16)
    rr = lax.broadcasted_iota(jnp.int32, (GROUP_X, LANES), 0) // SSM_HEAD_DIM
    rc = lax.broadcasted_iota(jnp.int32, (GROUP_X, LANES), 1)
    reduce = (rr == rc).astype(BF16)
    lane_head = lax.broadcasted_iota(jnp.int32, (Q, GROUP_X), 1) // SSM_HEAD_DIM
    return causal, tril, triu, expand, reduce, lane_head


def _ssd_common(xc_ref, dt_ref, par_ref, consts):
    causal, tril, _, expand, _, _ = consts
    par = par_ref[...]
    bias, alog, dsk = par[0:1, :], par[1:2, :], par[2:3, :]
    a_neg = -jnp.exp(alog)
    dtr = dt_ref[...] + bias
    dt = _softplus(dtr)
    s = _dot_l01(tril, dt * a_neg)
    dt_x = _dot_r01(dt, expand)
    s_x = _dot_r01(s, expand)
    dsk_x = _dot_r01(jnp.broadcast_to(dsk, (8, LANES)), expand)[0:1, :]
    blk = xc_ref[...]
    x = blk[:, :GROUP_X]
    bm = blk[:, GROUP_X:GROUP_X + D_STATE].astype(BF16)
    cm = blk[:, GROUP_X + D_STATE:].astype(BF16)
    xdt = x * dt_x
    g = _dot(cm, bm, _NT)
    return dict(a_neg=a_neg, dtr=dtr, dt=dt, s=s, s_t=s.T, dt_x=dt_x, s_x=s_x, dsk_x=dsk_x, x=x, bm=bm, cm=cm,
                xdt=xdt, g=g)


def _decay(v, r, causal):
    diff = v["s"][:, r:r + 1] - v["s_t"][r:r + 1, :]
    return jnp.exp(jnp.where(causal, diff, -jnp.inf))


def _ssd_specs(n_chunks, rev):
    cidx = (lambda c: n_chunks - 1 - c) if rev else (lambda c: c)
    xc = pl.BlockSpec((Q, GROUP_COLS), lambda g, c: (cidx(c), g))
    gx = pl.BlockSpec((Q, GROUP_X), lambda g, c: (cidx(c), g))
    dt = pl.BlockSpec((None, Q, LANES), lambda g, c: (g, cidx(c), 0))
    par = pl.BlockSpec((None, 8, LANES), lambda g, c: (g, 0, 0))
    nw = pl.BlockSpec((1, GROUP_X), lambda g, c: (0, g))
    hs = pl.BlockSpec((None, None, D_STATE, GROUP_X), lambda g, c: (cidx(c), g, 0, 0))
    return xc, gx, dt, par, nw, hs


def _ssd_fwd(xc, z, dtg, par, nw):
    s_dim = xc.shape[0]
    n_chunks = s_dim // Q
    xc_s, gx_s, dt_s, par_s, nw_s, hs_s = _ssd_specs(n_chunks, False)

    def body(xc_ref, z_ref, dt_ref, par_ref, nw_ref, y_ref, ys_ref, hs_ref, ht):
        @pl.when(pl.program_id(1) == 0)
        def _():
            ht[...] = jnp.zeros_like(ht)

        consts = _ssd_consts()
        causal, lane_head = consts[0], consts[5]
        v = _ssd_common(xc_ref, dt_ref, par_ref, consts)
        xdt_b = v["xdt"].astype(BF16)
        yd = jnp.zeros((Q, GROUP_X), F32)
        for r in range(HEADS_PER_GROUP):
            m = (v["g"] * _decay(v, r, causal)).astype(BF16)
            yd = yd + _dot(m, jnp.where(lane_head == r, xdt_b, jnp.zeros_like(xdt_b)))
        h = ht[...]
        hs_ref[...] = h
        yo = jnp.exp(v["s_x"]) * _dot(v["cm"], h.astype(BF16))
        y = yd + yo + v["dsk_x"] * v["x"]
        s_last = v["s_x"][Q - 1:Q, :]
        snew = _dot(v["bm"], (v["xdt"] * jnp.exp(s_last - v["s_x"])).astype(BF16), _TN)
        ht[...] = jnp.exp(s_last) * h + snew
        zz = z_ref[...]
        yg = y * (zz * _sigmoid(zz))
        y_ref[...] = y
        ys_ref[...] = _nrm(yg, nw_ref[...])[0].astype(BF16)

    return pl.pallas_call(
        body,
        grid=(SSM_GROUPS, n_chunks),
        in_specs=[xc_s, gx_s, dt_s, par_s, nw_s],
        out_specs=[gx_s, gx_s, hs_s],
        out_shape=[jax.ShapeDtypeStruct((s_dim, D_SSM), F32), jax.ShapeDtypeStruct((s_dim, D_SSM), BF16),
                   jax.ShapeDtypeStruct((n_chunks, SSM_GROUPS, D_STATE, GROUP_X), F32)],
        scratch_shapes=[pltpu.VMEM((D_STATE, GROUP_X), F32)],
        compiler_params=_cparams("parallel", "arbitrary"),
        name="ssd_fwd",
    )(xc, z, dtg, par, nw)


def _ssd_bwd(xc, z, dtg, par, nw, y, hs, dymix):
    s_dim = xc.shape[0]
    n_chunks = s_dim // Q
    xc_s, gx_s, dt_s, par_s, nw_s, hs_s = _ssd_specs(n_chunks, True)

    def body(xc_ref, z_ref, dt_ref, par_ref, nw_ref, y_ref, hs_ref, dys_ref,
             dxc_ref, dz_ref, ddt_ref, dpar_ref, dnw_ref, dht):
        @pl.when(pl.program_id(1) == 0)
        def _():
            dht[...] = jnp.zeros_like(dht)
            dpar_ref[...] = jnp.zeros_like(dpar_ref)
            dnw_ref[...] = jnp.zeros_like(dnw_ref)

        consts = _ssd_consts()
        causal, _, triu, _, reduce, lane_head = consts
        v = _ssd_common(xc_ref, dt_ref, par_ref, consts)
        x, bm, cm, xdt, s_x = v["x"], v["bm"], v["cm"], v["xdt"], v["s_x"]
        h = hs_ref[...]
        hb = h.astype(BF16)
        es_x = jnp.exp(s_x)
        yo = es_x * _dot(cm, hb)
        s_last = s_x[Q - 1:Q, :]
        e_x = jnp.exp(s_last - s_x)
        es_last = jnp.exp(s_last)

        yv, zz, nw_v = y_ref[...], z_ref[...], nw_ref[...]
        sg = _sigmoid(zz)
        gz = zz * sg
        _, n, rstd = _nrm(yv * gz, nw_v)
        dout = dys_ref[...]
        dyg, dnw = _nrm_bwd(dout, n, rstd, nw_v)
        dnw_ref[...] += dnw
        dy = dyg * gz
        dz_ref[...] = (dyg * yv * (sg * (1.0 + zz * (1.0 - sg)))).astype(BF16)

        dyb = dy.astype(BF16)
        xdt_b = xdt.astype(BF16)
        dhp = dht[...]
        dhpb = dhp.astype(BF16)
        lane = lax.broadcasted_iota(jnp.int32, (Q, LANES), 1)
        sub = lax.broadcasted_iota(jnp.int32, (LANES, Q), 0)
        dxdt = jnp.zeros((Q, GROUP_X), F32)
        dg = jnp.zeros((Q, Q), F32)
        ds = jnp.zeros((Q, LANES), F32)
        ds_t = jnp.zeros((LANES, Q), F32)
        for r in range(HEADS_PER_GROUP):
            dec = _decay(v, r, causal)
            mf = v["g"] * dec
            dyr = jnp.where(lane_head == r, dyb, jnp.zeros_like(dyb))
            dm = _dot(dyr, xdt_b, _NT)
            dxdt = dxdt + _dot(mf.astype(BF16), dyr, _TN)
            dg = dg + dm * dec
            dd = dm * mf
            ds = ds + jnp.where(lane == r, jnp.sum(dd, axis=1, keepdims=True), 0.0)
            ds_t = ds_t + jnp.where(sub == r, jnp.sum(dd, axis=0, keepdims=True), 0.0)
        ds = ds - ds_t.T
        dgb = dg.astype(BF16)
        dwb = (es_x * dy).astype(BF16)
        dcm = _dot(dgb, bm) + _dot(dwb, hb, _NT)
        dh_prev = _dot(cm, dwb, _TN)
        zst = _dot(bm, dhpb)
        xe = xdt * e_x
        dxdt = dxdt + e_x * zst
        dee = xe * zst
        dbm = _dot(dgb, cm, _TN) + _dot(xe.astype(BF16), dhpb, _NT)
        v_last = jnp.sum(dee, axis=0, keepdims=True) + es_last * jnp.sum(dhp * h, axis=0, keepdims=True)
        row_x = lax.broadcasted_iota(jnp.int32, (Q, GROUP_X), 0)
        tx = dy * yo - dee + jnp.where(row_x == Q - 1, v_last, 0.0)
        ds = ds + _dot_r01(tx, reduce)
        ddta = _dot_l01(triu, ds)
        ddt = ddta * v["a_neg"] + _dot_r01(dxdt * x, reduce)
        dalog = jnp.sum(ddta * v["dt"], axis=0, keepdims=True) * v["a_neg"]
        draw = jnp.where(lane < HEADS_PER_GROUP, ddt * _sigmoid(v["dtr"]), 0.0)
        dbias = jnp.sum(draw, axis=0, keepdims=True)
        ddsk = _dot_r01(jnp.broadcast_to(jnp.sum(dy * x, axis=0, keepdims=True), (8, GROUP_X)), reduce)[0:1, :]
        dht[...] = es_last * dhp + dh_prev
        dxc_ref[:, :GROUP_X] = dxdt * v["dt_x"] + v["dsk_x"] * dy
        dxc_ref[:, GROUP_X:GROUP_X + D_STATE] = dbm
        dxc_ref[:, GROUP_X + D_STATE:] = dcm
        ddt_ref[...] = draw
        dpar_ref[0:1, :] += dbias
        dpar_ref[1:2, :] += dalog
        dpar_ref[2:3, :] += ddsk

    return pl.pallas_call(
        body,
        grid=(SSM_GROUPS, n_chunks),
        in_specs=[xc_s, gx_s, dt_s, par_s, nw_s, gx_s, hs_s, gx_s],
        out_specs=[xc_s, gx_s, dt_s, par_s, nw_s],
        out_shape=[jax.ShapeDtypeStruct((s_dim, SSM_GROUPS * GROUP_COLS), F32),
                   jax.ShapeDtypeStruct((s_dim, D_SSM), BF16),
                   jax.ShapeDtypeStruct((SSM_GROUPS, s_dim, LANES), F32),
                   jax.ShapeDtypeStruct((SSM_GROUPS, 8, LANES), F32),
                   jax.ShapeDtypeStruct((1, D_SSM), F32)],
        scratch_shapes=[pltpu.VMEM((D_STATE, GROUP_X), F32)],
        compiler_params=_cparams("parallel", "arbitrary"),
        name="ssd_bwd",
    )(xc, z, dtg, par, nw, y, hs, dymix)


ATT_SCALE = ATT_HEAD_DIM ** -0.5
NEG_INF = -jnp.inf


def _band_masks():
    qi = lax.broadcasted_iota(jnp.int32, (ATT_BLOCK, ATT_BLOCK), 0)
    kj = lax.broadcasted_iota(jnp.int32, (ATT_BLOCK, ATT_BLOCK), 1)
    return kj <= qi, kj >= qi


WIN = ATT_BLOCK * DILATIONS[-1]
N_BLOCKS = WIN // ATT_BLOCK


def _rows(start, d):
    return pl.ds(start, ATT_BLOCK) if d == 1 else pl.ds(start, ATT_BLOCK, stride=d)


def _block_start(idx, d):
    return (idx // d) * (ATT_BLOCK * d) + idx % d


def _lane_bcast(col):
    return jnp.broadcast_to(col, (col.shape[0], LANES))


def _attn_fused_fwd(qkv):
    s_dim = qkv.shape[0]
    n_win = s_dim // WIN
    blk = (WIN, ATT_HEAD_DIM)
    prev = lambda w: jnp.maximum(w - 1, 0)

    def body(q_ref, kc_ref, kp_ref, vc_ref, vp_ref, y_ref, yf_ref, lse_ref, qf, kf, vf, acc, m_run, l_run):
        w, h = pl.program_id(0), pl.program_id(1)
        qf[...] = q_ref[...].astype(F32)
        kf[0:WIN, :] = kp_ref[...].astype(F32)
        kf[WIN:, :] = kc_ref[...].astype(F32)
        vf[0:WIN, :] = vp_ref[...].astype(F32)
        vf[WIN:, :] = vc_ref[...].astype(F32)
        own, before = _band_masks()

        for d in DILATIONS:
            def block(idx, carry, d=d):
                start = _block_start(idx, d)
                rows = _rows(start, d)
                q = qf[rows, :].astype(BF16)
                kc, vc = kf[_rows(WIN + start, d), :].astype(BF16), vf[_rows(WIN + start, d), :].astype(BF16)
                kp = kf[_rows(WIN + start - ATT_BLOCK * d, d), :].astype(BF16)
                vp = vf[_rows(WIN + start - ATT_BLOCK * d, d), :].astype(BF16)
                has_prev = (idx >= d) | (w > 0)
                sc = jnp.where(own, _dot(q, kc, _NT) * ATT_SCALE, NEG_INF)
                sp = jnp.where(before & has_prev, _dot(q, kp, _NT) * ATT_SCALE, NEG_INF)
                m_blk = jnp.maximum(jnp.max(sc, axis=1, keepdims=True), jnp.max(sp, axis=1, keepdims=True))
                if d == DILATIONS[0]:
                    m_new = m_blk
                else:
                    m_old = m_run[rows, :][:, 0:1]
                    m_new = jnp.maximum(m_old, m_blk)
                pc, pp = jnp.exp(sc - m_new), jnp.exp(sp - m_new)
                l_new = jnp.sum(pc, axis=1, keepdims=True) + jnp.sum(pp, axis=1, keepdims=True)
                o_new = _dot(pc.astype(BF16), vc) + _dot(pp.astype(BF16), vp)
                if d != DILATIONS[0]:
                    alpha = jnp.exp(m_old - m_new)
                    l_new = alpha * l_run[rows, :][:, 0:1] + l_new
                    o_new = alpha * acc[rows, :] + o_new
                m_run[rows, :] = _lane_bcast(m_new)
                l_run[rows, :] = _lane_bcast(l_new)
                acc[rows, :] = o_new
                return carry

            for idx in range(N_BLOCKS):
                block(idx, 0)

        l_all = l_run[...]
        y = acc[...] / l_all
        y_ref[...] = y.astype(BF16)
        yf_ref[...] = y
        @pl.when(h == 0)
        def _():
            lse_ref[...] = jnp.zeros_like(lse_ref)

        lane = lax.broadcasted_iota(jnp.int32, (WIN, LANES), 1)
        lse_ref[...] = jnp.where(lane == h, m_run[...] + jnp.log(l_all), lse_ref[...])

    win_scratch = lambda rows: pltpu.VMEM((rows, ATT_HEAD_DIM), F32)
    return pl.pallas_call(
        body,
        grid=(n_win, ATT_HEADS),
        in_specs=[pl.BlockSpec(blk, lambda w, h: (w, h)),
                  pl.BlockSpec(blk, lambda w, h: (w, ATT_HEADS + h)),
                  pl.BlockSpec(blk, lambda w, h: (prev(w), ATT_HEADS + h)),
                  pl.BlockSpec(blk, lambda w, h: (w, 2 * ATT_HEADS + h)),
                  pl.BlockSpec(blk, lambda w, h: (prev(w), 2 * ATT_HEADS + h))],
        out_specs=[pl.BlockSpec(blk, lambda w, h: (w, h)), pl.BlockSpec(blk, lambda w, h: (w, h)),
                   pl.BlockSpec((WIN, LANES), lambda w, h: (w, 0))],
        out_shape=[jax.ShapeDtypeStruct((s_dim, D_ATT), BF16), jax.ShapeDtypeStruct((s_dim, D_ATT), F32),
                   jax.ShapeDtypeStruct((s_dim, LANES), F32)],
        scratch_shapes=[win_scratch(WIN), win_scratch(2 * WIN), win_scratch(2 * WIN), win_scratch(WIN),
                        win_scratch(WIN), win_scratch(WIN)],
        compiler_params=_cparams("parallel", "arbitrary"),
        name="attn_fused_fwd",
    )(qkv, qkv, qkv, qkv, qkv)


def _attn_fused_bwd(qkv, dymix, y_att, lse, deps=()):
    s_dim = qkv.shape[0]
    n_win = s_dim // WIN
    blk = (WIN, ATT_HEAD_DIM)
    this = lambda w: jnp.minimum(w, n_win - 1)
    prev = lambda w: jnp.maximum(this(w) - 1, 0)
    n_dep = len(deps)

    def body(q_ref, kc_ref, kp_ref, vc_ref, vp_ref, dy_ref, y_ref, l_ref, *rest):
        dq_ref, dkv_ref = rest[n_dep:n_dep + 2]
        qf, kf, vf, dq_acc, dk_acc, dv_acc, ls_c, dl_c = rest[n_dep + 2:]
        h, w = pl.program_id(0), pl.program_id(1)
        slot, late = w % 2, 1 - w % 2

        @pl.when(w == 0)
        def _():
            dk_acc[...] = jnp.zeros_like(dk_acc)
            dv_acc[...] = jnp.zeros_like(dv_acc)

        @pl.when(w < n_win)
        def _():
            qf[...] = q_ref[...].astype(F32)
            kf[0:WIN, :] = kp_ref[...].astype(F32)
            kf[WIN:, :] = kc_ref[...].astype(F32)
            vf[0:WIN, :] = vp_ref[...].astype(F32)
            vf[WIN:, :] = vc_ref[...].astype(F32)
            lane = lax.broadcasted_iota(jnp.int32, (WIN, LANES), 1)
            ls_c[...] = _lane_bcast(jnp.sum(jnp.where(lane == h, l_ref[...], 0.0), axis=1, keepdims=True))
            dl_c[...] = _lane_bcast(jnp.sum(dy_ref[...] * y_ref[...], axis=1, keepdims=True))
            dq_acc[...] = jnp.zeros_like(dq_acc)
            dk_acc[slot] = jnp.zeros((WIN, ATT_HEAD_DIM), F32)
            dv_acc[slot] = jnp.zeros((WIN, ATT_HEAD_DIM), F32)
            own, before = _band_masks()

            def probs(q, k, v, dy, lse_col, dl_col, mask):
                p = jnp.exp(jnp.where(mask, _dot(q, k, _NT) * ATT_SCALE - lse_col, NEG_INF))
                ds = p * (_dot(dy, v, _NT) - dl_col)
                return p.astype(BF16), ds.astype(BF16)

            for d in DILATIONS:
                for idx in range(N_BLOCKS):
                    start = _block_start(idx, d)
                    rows = _rows(start, d)
                    q, dy = qf[rows, :].astype(BF16), dy_ref[rows, :].astype(BF16)
                    lse_col, dl_col = ls_c[rows, :][:, 0:1], dl_c[rows, :][:, 0:1]
                    kc, vc = kf[_rows(WIN + start, d), :].astype(BF16), vf[_rows(WIN + start, d), :].astype(BF16)
                    kp = kf[_rows(WIN + start - ATT_BLOCK * d, d), :].astype(BF16)
                    vp = vf[_rows(WIN + start - ATT_BLOCK * d, d), :].astype(BF16)
                    pc, dsc = probs(q, kc, vc, dy, lse_col, dl_col, own)
                    pp, dsp = probs(q, kp, vp, dy, lse_col, dl_col, before & ((idx >= d) | (w > 0)))
                    dq_acc[rows, :] += (_dot(dsc, kc) + _dot(dsp, kp)) * ATT_SCALE
                    dk_acc[slot, rows, :] += _dot(dsc, q, _TN) * ATT_SCALE
                    dv_acc[slot, rows, :] += _dot(pc, dy, _TN)
                    if idx >= d:
                        prows = _rows(start - ATT_BLOCK * d, d)
                        dk_acc[slot, prows, :] += _dot(dsp, q, _TN) * ATT_SCALE
                        dv_acc[slot, prows, :] += _dot(pp, dy, _TN)
                    else:
                        prows = _rows(WIN + start - ATT_BLOCK * d, d)
                        dk_acc[late, prows, :] += _dot(dsp, q, _TN) * ATT_SCALE
                        dv_acc[late, prows, :] += _dot(pp, dy, _TN)
            dq_ref[...] = dq_acc[...].astype(BF16)

        @pl.when(w > 0)
        def _():
            dkv_ref[0] = dk_acc[late].astype(BF16)
            dkv_ref[1] = dv_acc[late].astype(BF16)

    win_scratch = lambda *shape: pltpu.VMEM(shape + (ATT_HEAD_DIM,), F32)
    cur = lambda c: pl.BlockSpec(blk, lambda h, w: (this(w), c + h))
    before_spec = lambda c: pl.BlockSpec(blk, lambda h, w: (prev(w), c + h))
    return pl.pallas_call(
        body,
        grid=(ATT_HEADS, n_win + 1),
        in_specs=[cur(0), cur(ATT_HEADS), before_spec(ATT_HEADS), cur(2 * ATT_HEADS), before_spec(2 * ATT_HEADS),
                  cur(ATT_HEADS), cur(0), pl.BlockSpec((WIN, LANES), lambda h, w: (this(w), 0))] + [ANY] * n_dep,
        out_specs=[cur(0), pl.BlockSpec((2, WIN, ATT_HEAD_DIM), lambda h, w: (0, jnp.maximum(w - 1, 0), h))],
        out_shape=[jax.ShapeDtypeStruct((s_dim, D_ATT), BF16), jax.ShapeDtypeStruct((2, s_dim, D_ATT), BF16)],
        scratch_shapes=[win_scratch(WIN), win_scratch(2 * WIN), win_scratch(2 * WIN), win_scratch(WIN),
                        win_scratch(2, WIN), win_scratch(2, WIN), win_scratch(WIN), win_scratch(WIN)],
        compiler_params=_cparams("parallel", "arbitrary"),
        name="attn_fused_bwd",
    )(qkv, qkv, qkv, qkv, qkv, dymix, y_att, lse, *deps)


def _adamw(w, g, m, v, name):
    def fn(wb, gb, mb, vb):
        m2 = ADAM_B1 * mb + (1.0 - ADAM_B1) * gb
        v2 = ADAM_B2 * vb + (1.0 - ADAM_B2) * (gb * gb)
        m_hat = m2 / (1.0 - ADAM_B1 ** ADAM_STEP)
        v_hat = v2 / (1.0 - ADAM_B2 ** ADAM_STEP)
        delta = -ADAM_LR * (m_hat / (jnp.sqrt(v_hat) + ADAM_EPS) + ADAM_WD * wb)
        return delta, m2, v2
    cols = w.shape[1]
    tr = 128 if w.shape[0] % 128 == 0 else w.shape[0]
    return _rowcall(fn, [w, g, m, v], [], [(cols, F32)] * 3, [], name=name, tr=tr)


ANY = pl.BlockSpec(memory_space=pl.ANY)
SUM_ROWS = 256


def _position():
    x, y, c = lax.axis_index("x"), lax.axis_index("y"), lax.axis_index("c")
    chips = [(1 - x, y), (x, 1 - y), (1 - x, 1 - y)]
    return x, y, c, chips


def _remote(src, dst, send_sem, recv_sem, device):
    return pltpu.make_async_remote_copy(src_ref=src, dst_ref=dst, send_sem=send_sem, recv_sem=recv_sem,
                                        device_id=device, device_id_type=MESH)


def _handshake(peers):
    barrier = pltpu.get_barrier_semaphore()
    for p in peers:
        pl.semaphore_signal(barrier, inc=1, device_id=p, device_id_type=MESH)
    pl.semaphore_wait(barrier, len(peers))


def _gather_shards_async(shards, collective_id, name):
    n = len(shards)
    srcs = [jax.new_ref(s, memory_space=pltpu.MemorySpace.HBM) for s in shards]
    dsts = [jax.empty_ref(jax.ShapeDtypeStruct((N_CHIPS,) + s.shape, s.dtype), memory_space=pltpu.MemorySpace.HBM)
            for s in shards]

    @pl.kernel(mesh=plsc.ScalarSubcoreMesh(axis_name="seq", num_cores=1), name=name,
               scratch_types=(pltpu.SemaphoreType.DMA((6 * n,)), pltpu.SemaphoreType.DMA((6 * n,))),
               compiler_params=pltpu.CompilerParams(collective_id=collective_id))
    def launch(send_sems, recv_sems):
        x, y, c, chips = _position()
        sibling = (x, y, 1 - c)
        _handshake([(chip[0], chip[1], c) for chip in chips] + [sibling])

        def half(a, j, cc):
            h = shards[a].shape[0] // 2
            return dsts[a].at[j, pl.ds(cc * h, h), :]

        sent = []
        for a in range(n):
            h = shards[a].shape[0] // 2
            for j, chip in enumerate(chips):
                cp = _remote(srcs[a].at[pl.ds(c * h, h), :], half(a, 2 * x + y, c), send_sems.at[6 * a + j],
                             recv_sems.at[6 * a + j], (chip[0], chip[1], c))
                cp.start()
                sent.append(cp)
        for a in range(n):
            for j, chip in enumerate(chips):
                landed = half(a, 2 * chip[0] + chip[1], c)
                _remote(landed, landed, send_sems.at[6 * a + j], recv_sems.at[6 * a + j], (x, y, c)).wait_recv()
                cp = _remote(landed, landed, send_sems.at[6 * a + 3 + j], recv_sems.at[6 * a + 3 + j], sibling)
                cp.start()
                sent.append(cp)
        for a in range(n):
            for j, chip in enumerate(chips):
                handed = half(a, 2 * chip[0] + chip[1], 1 - c)
                _remote(handed, handed, send_sems.at[6 * a + 3 + j], recv_sems.at[6 * a + 3 + j], (x, y, c)).wait_recv()
        for cp in sent:
            cp.wait_send()

    launch()
    return [d[...] for d in dsts]


IN_COLS = {"z": (0, D_SSM), "xbc": (D_SSM, D_SSM + D_XBC), "dt": (D_SSM + D_XBC, D_SSM + D_XBC + SSM_HEADS),
           "qkv": (D_SSM + D_XBC + SSM_HEADS, D_IN_PROJ)}


def _cols_from_quarters(quarters, lo, hi):
    parts = []
    for q in range(N_CHIPS):
        a, b = max(lo, q * W_IN_SHARD), min(hi, (q + 1) * W_IN_SHARD)
        if a < b:
            parts.append(quarters[q][:, a - q * W_IN_SHARD:b - q * W_IN_SHARD])
    return parts[0] if len(parts) == 1 else jnp.concatenate(parts, axis=1)


def _quarters_from_cols(pieces):
    quarters = []
    for q in range(N_CHIPS):
        parts = []
        for name, (lo, hi) in IN_COLS.items():
            a, b = max(lo, q * W_IN_SHARD), min(hi, (q + 1) * W_IN_SHARD)
            if a < b:
                parts.append(pieces[name][:, a - lo:b - lo])
        quarters.append(jnp.concatenate(parts, axis=1))
    return jnp.stack(quarters)


def _by_chip(own, fetched):
    me = 2 * lax.axis_index("x") + lax.axis_index("y")
    return lax.dynamic_update_slice(fetched, own[None], (me, 0, 0))


def _add_sibling(grad, got, place, name, deps=()):
    nq, rows, cols = grad.shape
    h = rows // 2
    tr = SUM_ROWS
    nb = h // tr

    def body(place_ref, a_ref, b_ref, *rest):
        own_ref, ob_ref = rest[len(deps):]
        total = a_ref[...] + b_ref[...]
        ob_ref[...] = total.astype(BF16)

        @pl.when(pl.program_id(1) == place_ref[1])
        def _():
            own_ref[...] = total

    return pl.pallas_call(
        body,
        grid_spec=pltpu.PrefetchScalarGridSpec(
            num_scalar_prefetch=1, grid=(nb, nq),
            in_specs=[pl.BlockSpec((None, tr, cols), lambda i, q, p: (q, p[0] * nb + i, 0)),
                      pl.BlockSpec((None, tr, cols), lambda i, q, p: (q, i, 0))] + [ANY] * len(deps),
            out_specs=[pl.BlockSpec((tr, cols), lambda i, q, p: (i, 0)),
                       pl.BlockSpec((None, tr, cols), lambda i, q, p: (q, i, 0))]),
        out_shape=[jax.ShapeDtypeStruct((h, cols), F32), jax.ShapeDtypeStruct((nq, h, cols), BF16)],
        compiler_params=_cparams("parallel", "arbitrary"),
        name=name,
    )(place, grad, got, *deps)


def _add_chips(part, got, name, deps=()):
    h, cols = part.shape
    tr = SUM_ROWS

    def body(p_ref, g0_ref, g1_ref, g2_ref, *rest):
        o_ref = rest[len(deps)]
        o_ref[...] = ((p_ref[...] + g0_ref[...].astype(F32)) + g1_ref[...].astype(F32)) + g2_ref[...].astype(F32)

    got_spec = lambda j: pl.BlockSpec((None, tr, cols), lambda i: (j, i, 0))
    row_spec = pl.BlockSpec((tr, cols), lambda i: (i, 0))
    return pl.pallas_call(
        body,
        grid=(h // tr,),
        in_specs=[row_spec, got_spec(0), got_spec(1), got_spec(2)] + [ANY] * len(deps),
        out_specs=row_spec,
        out_shape=jax.ShapeDtypeStruct((h, cols), F32),
        compiler_params=_cparams("parallel"),
        name=name,
    )(part, got, got, got, *deps)


def _sequencer_exchange(src, out_shape, collective_id, name, plan, n_copies):
    src_ref = jax.new_ref(src, memory_space=pltpu.MemorySpace.HBM)
    dst_ref = jax.empty_ref(out_shape, memory_space=pltpu.MemorySpace.HBM)

    @pl.kernel(mesh=plsc.ScalarSubcoreMesh(axis_name="seq", num_cores=1), name=name,
               scratch_types=(pltpu.SemaphoreType.DMA((n_copies,)), pltpu.SemaphoreType.DMA((n_copies,))),
               compiler_params=pltpu.CompilerParams(collective_id=collective_id))
    def launch(send_sems, recv_sems):
        x, y, c, chips = _position()
        copies = plan(src_ref, dst_ref, x, y, c, chips)
        _handshake([peer for _, _, peer in copies])
        started = []
        for k, (s, d, peer) in enumerate(copies):
            cp = _remote(s, d, send_sems.at[k], recv_sems.at[k], peer)
            cp.start()
            started.append(cp)
        for cp in started:
            cp.wait()

    launch()
    return dst_ref[...]


class _AsyncReduceScatter:
    def __init__(self, grad, nm, first_id):
        self.grad, self.nm, self.first_id = grad, nm, first_id
        nq, rows, cols = grad.shape
        h = self.h = rows // 2

        def to_sibling(s, d, x, y, c, chips):
            return [(s.at[:, pl.ds((1 - c) * h, h), :], d, (x, y, 1 - c))]

        self.from_sibling = _sequencer_exchange(grad, jax.ShapeDtypeStruct((nq, h, cols), F32), first_id,
                                                f"rs_sibling_{nm}", to_sibling, 1)

    def sibling_sum(self, not_before=()):
        cols = self.grad.shape[2]
        place = jnp.stack([lax.axis_index("c"), 2 * lax.axis_index("x") + lax.axis_index("y")]).astype(jnp.int32)
        self.part, self.part_b = _add_sibling(self.grad, self.from_sibling, place, f"add_sibling_{self.nm}", not_before)

        def to_chips(s, d, x, y, c, chips):
            return [(s.at[2 * chip[0] + chip[1]], d.at[j], (chip[0], chip[1], c)) for j, chip in enumerate(chips)]

        self.from_chips = _sequencer_exchange(self.part_b, jax.ShapeDtypeStruct((3, self.h, cols), BF16),
                                              self.first_id + 1, f"rs_quarters_{self.nm}", to_chips, 3)
        return self.part_b

    def chip_sum(self, not_before=()):
        cols = self.grad.shape[2]
        self.half = _add_chips(self.part, self.from_chips, f"add_chips_{self.nm}", not_before)

        def whole_to_sibling(s, d, x, y, c, chips):
            return [(s, d, (x, y, 1 - c))]

        self.other = _sequencer_exchange(self.half, jax.ShapeDtypeStruct((self.h, cols), F32), self.first_id + 2,
                                         f"rs_share_{self.nm}", whole_to_sibling, 1)
        return self.half

    def share(self):
        return self.half, self.other


def _after(x, deps, name):
    def body(x_ref, *rest):
        rest[-1][...] = x_ref[...]

    vm = pl.BlockSpec(memory_space=pltpu.VMEM)
    return pl.pallas_call(body, in_specs=[vm] + [ANY] * len(deps), out_specs=vm,
                          out_shape=jax.ShapeDtypeStruct(x.shape, x.dtype), name=name)(x, *deps)


def _adamw_halves(w, mine, other, m, v, name):
    rows, cols = w.shape
    tr = SUM_ROWS
    nb = rows // 2 // tr
    c_arr = lax.axis_index("c").astype(jnp.int32).reshape(1)

    def body(c_ref, w_ref, a_ref, b_ref, m_ref, v_ref, g_out, d_out, m_out, v_out):
        is_mine = (pl.program_id(0) // nb) == c_ref[0]
        g = jnp.where(is_mine, a_ref[...], b_ref[...])
        wb, mb, vb = w_ref[...], m_ref[...], v_ref[...]
        m2 = ADAM_B1 * mb + (1.0 - ADAM_B1) * g
        v2 = ADAM_B2 * vb + (1.0 - ADAM_B2) * (g * g)
        m_hat = m2 / (1.0 - ADAM_B1 ** ADAM_STEP)
        v_hat = v2 / (1.0 - ADAM_B2 ** ADAM_STEP)
        g_out[...] = g
        d_out[...] = -ADAM_LR * (m_hat / (jnp.sqrt(v_hat) + ADAM_EPS) + ADAM_WD * wb)
        m_out[...] = m2
        v_out[...] = v2

    full = pl.BlockSpec((tr, cols), lambda i, c: (i, 0))
    mine_spec = pl.BlockSpec((tr, cols), lambda i, c: (jnp.where(i // nb == c[0], i % nb, 0), 0))
    other_spec = pl.BlockSpec((tr, cols), lambda i, c: (jnp.where(i // nb == c[0], 0, i % nb), 0))
    return pl.pallas_call(
        body,
        grid_spec=pltpu.PrefetchScalarGridSpec(
            num_scalar_prefetch=1, grid=(rows // tr,),
            in_specs=[full, mine_spec, other_spec, full, full], out_specs=[full] * 4),
        out_shape=[jax.ShapeDtypeStruct((rows, cols), F32)] * 4,
        compiler_params=_cparams("parallel"),
        name=name,
    )(c_arr, w, mine, other, m, v)


def _adamw_halves_t(w_t, mine_t, other_t, m_t, v_t, name):
    cols, rows = w_t.shape
    tr = cols // 11
    assert tr * 11 == cols and tr % 8 == 0
    c_arr = lax.axis_index("c").astype(jnp.int32).reshape(1)

    def body(c_ref, w_ref, a_ref, b_ref, m_ref, v_ref, g_out, d_out, m_out, v_out):
        first = c_ref[0] == 0
        a, b = a_ref[...], b_ref[...]
        g = jnp.concatenate([jnp.where(first, a, b), jnp.where(first, b, a)], axis=1)
        wb, mb, vb = w_ref[...], m_ref[...], v_ref[...]
        m2 = ADAM_B1 * mb + (1.0 - ADAM_B1) * g
        v2 = ADAM_B2 * vb + (1.0 - ADAM_B2) * (g * g)
        m_hat = m2 / (1.0 - ADAM_B1 ** ADAM_STEP)
        v_hat = v2 / (1.0 - ADAM_B2 ** ADAM_STEP)
        g_out[...] = g
        d_out[...] = -ADAM_LR * (m_hat / (jnp.sqrt(v_hat) + ADAM_EPS) + ADAM_WD * wb)
        m_out[...] = m2
        v_out[...] = v2

    full = pl.BlockSpec((tr, rows), lambda i, c: (i, 0))
    half = pl.BlockSpec((tr, rows // 2), lambda i, c: (i, 0))
    return pl.pallas_call(
        body,
        grid_spec=pltpu.PrefetchScalarGridSpec(
            num_scalar_prefetch=1, grid=(cols // tr,),
            in_specs=[full, half, half, full, full], out_specs=[full] * 4),
        out_shape=[jax.ShapeDtypeStruct((cols, rows), F32)] * 4,
        compiler_params=_cparams("parallel"),
        name=name,
    )(c_arr, w_t, mine_t, other_t, m_t, v_t)


def _all_sum_small(v):
    n_dev = 8

    def body(v_ref, o_ref, gath, send_sems, recv_sems):
        x, y, c, _ = _position()
        me = 4 * x + 2 * y + c
        gath[me] = v_ref[...]
        copies = []
        for k in range(1, n_dev):
            peer = tuple(1 - p if (k >> s) & 1 else p for p, s in ((x, 2), (y, 1), (c, 0)))
            cp = _remote(v_ref, gath.at[me], send_sems.at[k - 1], recv_sems.at[k - 1], peer)
            cp.start()
            copies.append(cp)
        for cp in copies:
            cp.wait()
        acc = gath[0]
        for i in range(1, n_dev):
            acc = acc + gath[i]
        o_ref[...] = acc

    vm = pl.BlockSpec(memory_space=pltpu.VMEM)
    return pl.pallas_call(
        body,
        in_specs=[vm],
        out_specs=vm,
        out_shape=jax.ShapeDtypeStruct(v.shape, F32),
        scratch_shapes=[pltpu.VMEM((n_dev,) + v.shape, F32), pltpu.SemaphoreType.DMA((n_dev - 1,)),
                        pltpu.SemaphoreType.DMA((n_dev - 1,))],
        name="all_sum_small",
    )(v)


def _pack_rows(vectors):
    rows = []
    for v in vectors:
        flat = v.reshape(-1).astype(F32)
        rows.append(jnp.pad(flat, (0, (-flat.shape[0]) % LANES)).reshape(-1, LANES))
    out = jnp.concatenate(rows, axis=0)
    return jnp.pad(out, ((0, (-out.shape[0]) % 8), (0, 0)))


def _unpack_rows(packed, shapes):
    outs, r = [], 0
    for shp in shapes:
        size = math.prod(shp)
        nr = -(-size // LANES)
        outs.append(packed[r:r + nr].reshape(-1)[:size].reshape(shp))
        r += nr
    return outs


def _relu_sq(acc):
    r = jnp.maximum(acc, 0.0)
    return r, r * r


def _relu_sq_bwd(acc, r):
    return (acc * (2.0 * r.astype(F32)),)


def kernel(x, norm_mix_pre, w_in, conv_w, conv_b, dt_bias, a_log, d_skip, ssm_norm_w, w_out, norm_mix_post, norm_mlp_pre, w_up, w_down, norm_mlp_post, loss_target, m_norm_mix_pre, m_w_in, m_conv_w, m_conv_b, m_dt_bias, m_a_log, m_d_skip, m_ssm_norm_w, m_w_out, m_norm_mix_post, m_norm_mlp_pre, m_w_up, m_w_down, m_norm_mlp_post, v_norm_mix_pre, v_w_in, v_conv_w, v_conv_b, v_dt_bias, v_a_log, v_d_skip, v_ssm_norm_w, v_w_out, v_norm_mix_post, v_norm_mlp_pre, v_w_up, v_w_down, v_norm_mlp_post):
    s_dim = x.shape[1]
    xs, target = x[0], loss_target[0]
    chip = 2 * lax.axis_index("x") + lax.axis_index("y")

    own = [w_in[0].astype(BF16), w_out[0].astype(BF16), w_up[0].astype(BF16), w_down[0].astype(BF16)]
    fetched_in = _gather_shards_async(own[:1], 14, "gather_w_in")[0]
    conv_cols = D_XBC // N_CHIPS
    conv_placed = lax.dynamic_update_slice(jnp.zeros((8, D_XBC), F32), 0.5 * conv_w[0], (0, chip * conv_cols))
    conv_full = _all_sum_small(conv_placed.reshape(-1, LANES)).reshape(8, D_XBC)
    w8 = _perm_cols(conv_full.at[CONV_WIDTH].set(conv_b[0]))
    u = _pre_norm(xs, norm_mix_pre)
    fetched_in, u, w8, *rest = lax.optimization_barrier((fetched_in, u, w8, *own[1:]))
    fetched = [fetched_in] + _gather_shards_async(rest, 1, "gather_rest")
    g_in, g_out, g_up, g_down = [_by_chip(o, f) for o, f in zip(own, fetched)]
    w_z = _cols_from_quarters(g_in, *IN_COLS["z"])
    w_xbc = _perm_cols(_cols_from_quarters(g_in, *IN_COLS["xbc"]))
    w_dt = jnp.pad(_cols_from_quarters(g_in, *IN_COLS["dt"]), ((0, 0), (0, LANES - SSM_HEADS)))
    w_qkv = _cols_from_quarters(g_in, *IN_COLS["qkv"])
    w_out_full = g_out.reshape(D_MIX, D_MODEL)
    w_down_full = g_down.reshape(D_FF, D_MODEL)

    z = _matmul([(u, w_z, TK)], "nn", [F32], name="proj_z")
    xbc = _matmul([(u, w_xbc, TK)], "nn", [F32], name="proj_xbc")
    dt_raw = _matmul([(u, w_dt, TK)], "nn", [F32], name="proj_dt")
    qkv = _matmul([(u, w_qkv, TK)], "nn", [BF16], name="proj_qkv")
    xc = _conv_fwd(xbc, w8)
    dtg = _dt_to_groups(dt_raw)
    par = _pack_ssd_params(dt_bias[0], a_log[0], d_skip[0])
    y, y_ssm, states = _ssd_fwd(xc, z, dtg, par, ssm_norm_w)
    y_att, y_att_f32, lse = _attn_fused_fwd(qkv)
    y_mix = jnp.concatenate([y_ssm, y_att], axis=1)
    mix = _matmul([(y_mix, w_out_full, TK)], "nn", [F32], name="out_proj")
    h1, u2 = _post_pre_norm(xs, mix, norm_mix_post, norm_mlp_pre)
    hid, act = _matmul([(u2, g_up, TK)], "nn", [BF16, BF16], name="mlp_up", epilogue=_relu_sq)
    ff = _matmul([(act, w_down_full, TK)], "nn", [F32], name="mlp_down")
    dh2, dff, d_g4, loss_part = _tail(ff, h1, target, norm_mlp_post)

    dhid = _matmul([(dff, w_down_full, TK)], "nt", [BF16], name="mlp_down_dx", epilogue=_relu_sq_bwd, extras=[hid])
    weights = {"norm_mix_pre": (norm_mix_pre, m_norm_mix_pre, v_norm_mix_pre), "w_in": (w_in, m_w_in, v_w_in),
               "conv_w": (conv_w, m_conv_w, v_conv_w), "conv_b": (conv_b, m_conv_b, v_conv_b),
               "dt_bias": (dt_bias, m_dt_bias, v_dt_bias), "a_log": (a_log, m_a_log, v_a_log),
               "d_skip": (d_skip, m_d_skip, v_d_skip), "ssm_norm_w": (ssm_norm_w, m_ssm_norm_w, v_ssm_norm_w),
               "w_out": (w_out, m_w_out, v_w_out), "norm_mix_post": (norm_mix_post, m_norm_mix_post, v_norm_mix_post),
               "norm_mlp_pre": (norm_mlp_pre, m_norm_mlp_pre, v_norm_mlp_pre), "w_up": (w_up, m_w_up, v_w_up),
               "w_down": (w_down, m_w_down, v_w_down),
               "norm_mlp_post": (norm_mlp_post, m_norm_mlp_post, v_norm_mlp_post)}
    grads, delta, new_m, new_v = {}, {}, {}, {}

    def adamw_big(n, halves):
        w, m, v = weights[n]
        g_, d_, m_, v_ = _adamw_halves(w[0], halves[0], halves[1], m[0], v[0], f"adamw_{n}")
        grads[n], delta[n], new_m[n], new_v[n] = g_[None], d_[None], m_[None], v_[None]

    dw_down = _matmul([(act, dff, TK_TN)], "tn", [F32], name="mlp_down_dw")
    rs_down = _AsyncReduceScatter(dw_down.reshape(N_CHIPS, D_FF // N_CHIPS, D_MODEL), "w_down", 11)
    dw_up = _matmul([(u2, dhid, TK_TN)], "tn", [F32], name="mlp_up_dw", deps=[dw_down], out_quarters=True)
    rs_up = _AsyncReduceScatter(dw_up, "w_up", 8)
    du2 = _matmul([(dhid, g_up, TK)], "nt", [F32], name="mlp_up_dx",
                  deps=[rs_down.sibling_sum(not_before=[dw_up])])
    dh1, dmix, d_g3, d_g2 = _mid_bwd(du2, h1, dh2, mix, norm_mix_post, norm_mlp_pre,
                                     deps=[rs_up.sibling_sum(not_before=[du2])])
    dymix = _matmul([(dmix, w_out_full, TK)], "nt", [F32], name="out_proj_dx")
    dw_out = _matmul([(y_mix, dmix, TK_TN)], "tn", [F32], name="out_proj_dw")
    rs_out = _AsyncReduceScatter(dw_out.reshape(N_CHIPS, D_MIX // N_CHIPS, D_MODEL), "w_out", 5)
    dq, dkv = _attn_fused_bwd(qkv, dymix, y_att_f32, lse)
    dqkv = jnp.concatenate([dq[None], dkv], axis=0)
    par_late = _after(par, [rs_down.chip_sum(not_before=[dqkv]), rs_out.sibling_sum(not_before=[dymix])],
                      "after_w_down")
    dxc, dz, ddtg, dpar, d_nw = _ssd_bwd(xc, z, dtg, par_late, ssm_norm_w, y, states, dymix)
    g_down = rs_down.share()
    dxbc, dw8 = _conv_bwd(xbc, _after(w8, [*g_down, rs_up.chip_sum(not_before=[dxc])], "after_w_up"), dxc)
    ddt = jnp.pad(_dt_from_groups(ddtg), ((0, 0), (0, LANES - SSM_HEADS))).astype(BF16)
    g_up = rs_up.share()
    dw_z = _matmul([(u, dz, TK_TN)], "tn", [F32], name="proj_z_dw")
    dw_xbc = _matmul([(u, dxbc, TK_TN)], "tn", [F32], name="proj_xbc_dw",
                     deps=[*g_up, rs_out.chip_sum(not_before=[dxbc])])
    g_out = rs_out.share()
    dw_dt = _matmul([(u, ddt, TK_TN)], "tn", [F32], name="proj_dt_dw")
    dw_qkv = _matmul([(u, dqkv, TK_TN)], "tn", [F32], name="proj_qkv_dw")
    dw_in = _quarters_from_cols({"z": dw_z, "xbc": _unperm_cols(dw_xbc), "dt": dw_dt[:, :SSM_HEADS], "qkv": dw_qkv})
    rs_in = _AsyncReduceScatter(dw_in, "w_in", 2)
    adamw_big("w_down", g_down)
    adamw_big("w_up", g_up)
    rs_in.sibling_sum(not_before=[delta["w_up"]])
    du = _matmul([(dz, w_z, TK_MULTI), (dxbc, w_xbc, TK_MULTI), (dqkv, w_qkv, TK_MULTI), (ddt, w_dt, LANES)], "nt",
                 [F32], name="proj_dx", deps=[*g_out, rs_in.part_b])
    grad_x, d_g1 = _first_bwd(du, xs, dh1, norm_mix_pre)
    adamw_big("w_out", g_out)
    rs_in.chip_sum(not_before=[grad_x, delta["w_out"]])

    dconv = _unperm_cols(dw8)
    d_bias, d_alog, d_dskip = _unpack_ssd_params(dpar)
    small_shapes = [(1, D_MODEL), (CONV_WIDTH, D_XBC), (1, D_XBC), (1, SSM_HEADS), (1, SSM_HEADS), (1, SSM_HEADS),
                    (1, D_SSM), (1, D_MODEL), (1, D_MODEL), (1, D_MODEL), (1, LANES)]
    summed = _unpack_rows(
        _all_sum_small(_pack_rows([d_g1, dconv[:CONV_WIDTH], dconv[CONV_WIDTH:CONV_WIDTH + 1], d_bias, d_alog,
                                   d_dskip, d_nw, d_g2, d_g3, d_g4, loss_part])), small_shapes)
    (g_g1, g_conv_full, g_conv_b, g_bias, g_alog, g_dskip, g_nw, g_g2, g_g3, g_g4, loss_row) = summed
    loss = loss_row[0, 0]
    g_conv_w = lax.dynamic_slice(g_conv_full, (0, chip * conv_cols), (CONV_WIDTH, conv_cols))[None]

    grads.update({"norm_mix_pre": g_g1, "conv_w": g_conv_w, "conv_b": g_conv_b, "dt_bias": g_bias,
                  "a_log": g_alog, "d_skip": g_dskip, "ssm_norm_w": g_nw, "norm_mix_post": g_g2,
                  "norm_mlp_pre": g_g3, "norm_mlp_post": g_g4})
    order = list(weights)
    small_names = [n for n in order if n not in ("w_in", "w_out", "w_up", "w_down")]
    small_w_shapes = [weights[n][0].shape for n in small_names]
    packed = [_pack_rows([weights[n][k] for n in small_names]) for k in range(3)]
    packed_g = _pack_rows([grads[n].reshape(weights[n][0].shape) for n in small_names])
    sd, sm, sv = _adamw(packed[0], packed_g, packed[1], packed[2], "adamw_small")
    for k, n in enumerate(small_names):
        grads[n] = grads[n].reshape(weights[n][0].shape)
    for res, pk in ((delta, sd), (new_m, sm), (new_v, sv)):
        for n, val in zip(small_names, _unpack_rows(pk, small_w_shapes)):
            res[n] = val
    mine, other = rs_in.share()
    w_t, m_t, v_t = [jnp.swapaxes(a[0], 0, 1) for a in weights["w_in"]]
    results_t = _adamw_halves_t(w_t, mine.T, other.T, m_t, v_t, "adamw_w_in")
    grads["w_in"], delta["w_in"], new_m["w_in"], new_v["w_in"] = [jnp.swapaxes(r, 0, 1)[None] for r in results_t]

    return (loss, grad_x[None], *[grads[n] for n in order], *[delta[n] for n in order],
            *[new_m[n] for n in order], *[new_v[n] for n in order])
```
